```python
import math
import jax, jax.numpy as jnp
from jax import lax
import numpy as np

D_MODEL = 1024
BATCH = 8
SEQ = 8192
DEPTH = 1

D_MIX = D_MODEL
CONV_CH = D_MIX // 2
CONV_GROUPS = 8
CONV_WIDTH = 31
SB_HEADS = 8
SB_HEAD_DIM = 64
SB_WIDTH = SB_HEADS * SB_HEAD_DIM
Q_BLOCK = 128
D_FF = int(math.ceil((8 * D_MODEL / 3) / 256) * 256)
IN_COLS = 2 * CONV_CH + 3 * SB_WIDTH
EPS = 1e-6

kernel_name = "hymba_conformer_stickbreaking_sandwich"


def _rmsnorm(x, g):
    xf = x.astype(jnp.float32)
    y = xf * lax.rsqrt(jnp.mean(xf * xf, axis=-1, keepdims=True) + EPS)
    return (y * g.astype(jnp.float32)).astype(x.dtype)


def _layernorm(x, g, b):
    xf = x.astype(jnp.float32)
    mu = jnp.mean(xf, axis=-1, keepdims=True)
    var = jnp.mean(jnp.square(xf - mu), axis=-1, keepdims=True)
    y = (xf - mu) * lax.rsqrt(var + EPS)
    return (y * g.astype(jnp.float32) + b.astype(jnp.float32)).astype(x.dtype)


def _conformer_conv(val, gate, conv_w, conv_b, ln_g, ln_b):
    glu = val * jax.nn.sigmoid(gate)
    padded = jnp.pad(glu, ((0, 0), (CONV_WIDTH - 1, 0), (0, 0)))
    y = lax.conv_general_dilated(
        padded, conv_w.astype(glu.dtype), window_strides=(1,), padding="VALID",
        dimension_numbers=("NWC", "WIO", "NWC"), feature_group_count=CONV_CH)
    y = y + conv_b
    y = _layernorm(y, ln_g, ln_b)
    return jax.nn.silu(y)


def _stick_breaking(q, k, v):
    B, S = q.shape[0], q.shape[1]
    scale = 1.0 / math.sqrt(SB_HEAD_DIM)
    qh = jnp.transpose(q, (0, 2, 1, 3))
    kh = jnp.transpose(k, (0, 2, 1, 3))
    vh = jnp.transpose(v, (0, 2, 1, 3))
    nb = S // Q_BLOCK
    q_blocks = jnp.transpose(qh.reshape(B, SB_HEADS, nb, Q_BLOCK, SB_HEAD_DIM), (2, 0, 1, 3, 4))
    starts = jnp.arange(nb, dtype=jnp.int32) * Q_BLOCK
    key_pos = jnp.arange(S, dtype=jnp.int32)

    def one_block(args):
        qb, t0 = args
        z = jnp.einsum("bhqd,bhkd->bhqk", qb, kh,
                       preferred_element_type=jnp.float32) * scale
        q_pos = t0 + jnp.arange(Q_BLOCK, dtype=jnp.int32)
        mask = key_pos[None, :] < q_pos[:, None]
        log_beta = jax.nn.log_sigmoid(z)
        log_1m_beta = jnp.where(mask, jax.nn.log_sigmoid(-z), 0.0)
        between = lax.cumsum(log_1m_beta, axis=3, reverse=True) - log_1m_beta
        attn = jnp.where(mask, jnp.exp(log_beta + between), 0.0)
        return jnp.einsum("bhqk,bhkd->bhqd", attn.astype(vh.dtype), vh)

    out = lax.map(one_block, (q_blocks, starts))
    out = jnp.transpose(out, (1, 0, 3, 2, 4))
    return out.reshape(B, S, SB_HEADS, SB_HEAD_DIM)


def _fwd_setup_inputs(seed: int = 0) -> dict:
    key = jax.random.key(seed)
    ks = jax.random.split(key, 16)
    nrm = jax.random.normal
    def gain(k, shape):
        return 1.0 + 0.05 * nrm(k, shape, jnp.float32)
    return {
        "x": nrm(ks[0], (BATCH, SEQ, D_MODEL), jnp.float32),
        "g_pre_mix": gain(ks[1], (DEPTH, D_MODEL)),
        "w_in": nrm(ks[2], (DEPTH, D_MODEL, IN_COLS), jnp.float32) * D_MODEL ** -0.5,
        "conv_w": nrm(ks[3], (DEPTH, CONV_WIDTH, 1, CONV_CH), jnp.float32) * CONV_WIDTH ** -0.5,
        "conv_b": 0.02 * nrm(ks[4], (DEPTH, CONV_CH), jnp.float32),
        "conv_ln_g": gain(ks[5], (DEPTH, CONV_CH)),
        "conv_ln_b": 0.02 * nrm(ks[6], (DEPTH, CONV_CH), jnp.float32),
        "attn_norm_g": gain(ks[7], (DEPTH, SB_HEADS, SB_HEAD_DIM)),
        "w_out": nrm(ks[8], (DEPTH, D_MIX, D_MODEL), jnp.float32) * D_MIX ** -0.5,
        "g_post_mix": gain(ks[9], (DEPTH, D_MODEL)),
        "g_pre_ffn": gain(ks[10], (DEPTH, D_MODEL)),
        "w_gate": nrm(ks[11], (DEPTH, D_MODEL, D_FF), jnp.float32) * D_MODEL ** -0.5,
        "w_up": nrm(ks[12], (DEPTH, D_MODEL, D_FF), jnp.float32) * D_MODEL ** -0.5,
        "w_down": nrm(ks[13], (DEPTH, D_FF, D_MODEL), jnp.float32) * D_FF ** -0.5,
        "g_post_ffn": gain(ks[14], (DEPTH, D_MODEL)),
    }


def _fwd_reference(x, g_pre_mix, w_in, conv_w, conv_b, conv_ln_g, conv_ln_b, attn_norm_g,
              w_out, g_post_mix, g_pre_ffn, w_gate, w_up, w_down, g_post_ffn):
    B, S, _ = x.shape
    h = x
    for l in range(DEPTH):
        a = _rmsnorm(h, g_pre_mix[l])
        u = jnp.einsum("bsd,dc->bsc", a, w_in[l])
        c_val = u[..., :CONV_CH]
        c_gate = u[..., CONV_CH:2 * CONV_CH]
        qkv = u[..., 2 * CONV_CH:].reshape(B, S, 3, SB_HEADS, SB_HEAD_DIM)
        conv_out = _conformer_conv(c_val, c_gate, conv_w[l], conv_b[l], conv_ln_g[l], conv_ln_b[l])
        attn_out = _stick_breaking(qkv[:, :, 0], qkv[:, :, 1], qkv[:, :, 2])
        attn_out = _rmsnorm(attn_out, attn_norm_g[l]).reshape(B, S, SB_WIDTH)
        mixed = jnp.concatenate([conv_out, attn_out], axis=-1)
        y = jnp.einsum("bsc,cd->bsd", mixed, w_out[l])
        h = h + _rmsnorm(y, g_post_mix[l])
        f_in = _rmsnorm(h, g_pre_ffn[l])
        gt = jnp.einsum("bsd,df->bsf", f_in, w_gate[l])
        up = jnp.einsum("bsd,df->bsf", f_in, w_up[l])
        f = jnp.einsum("bsf,fd->bsd", jax.nn.silu(gt) * up, w_down[l])
        h = h + _rmsnorm(f, g_post_ffn[l])
    return h


import jax as _jax
import jax.numpy as _jnp

TWIN_FORMAT = 'train_step'
FWD_PARAMS = ['x', 'g_pre_mix', 'w_in', 'conv_w', 'conv_b', 'conv_ln_g', 'conv_ln_b', 'attn_norm_g', 'w_out', 'g_post_mix', 'g_pre_ffn', 'w_gate', 'w_up', 'w_down', 'g_post_ffn']
TWIN_WEIGHTS = ['g_pre_mix', 'w_in', 'conv_w', 'conv_b', 'conv_ln_g', 'conv_ln_b', 'attn_norm_g', 'w_out', 'g_post_mix', 'g_pre_ffn', 'w_gate', 'w_up', 'w_down', 'g_post_ffn']
TWIN_DIFF_INPUT = 'x'
TWIN_INPUTS = ['x', 'g_pre_mix', 'w_in', 'conv_w', 'conv_b', 'conv_ln_g', 'conv_ln_b', 'attn_norm_g', 'w_out', 'g_post_mix', 'g_pre_ffn', 'w_gate', 'w_up', 'w_down', 'g_post_ffn', 'loss_target', 'm_g_pre_mix', 'm_w_in', 'm_conv_w', 'm_conv_b', 'm_conv_ln_g', 'm_conv_ln_b', 'm_attn_norm_g', 'm_w_out', 'm_g_post_mix', 'm_g_pre_ffn', 'm_w_gate', 'm_w_up', 'm_w_down', 'm_g_post_ffn', 'v_g_pre_mix', 'v_w_in', 'v_conv_w', 'v_conv_b', 'v_conv_ln_g', 'v_conv_ln_b', 'v_attn_norm_g', 'v_w_out', 'v_g_post_mix', 'v_g_pre_ffn', 'v_w_gate', 'v_w_up', 'v_w_down', 'v_g_post_ffn']
TWIN_OUTPUTS = ['loss', 'grad_x', 'grad_g_pre_mix', 'grad_w_in', 'grad_conv_w', 'grad_conv_b', 'grad_conv_ln_g', 'grad_conv_ln_b', 'grad_attn_norm_g', 'grad_w_out', 'grad_g_post_mix', 'grad_g_pre_ffn', 'grad_w_gate', 'grad_w_up', 'grad_w_down', 'grad_g_post_ffn', 'delta_g_pre_mix', 'delta_w_in', 'delta_conv_w', 'delta_conv_b', 'delta_conv_ln_g', 'delta_conv_ln_b', 'delta_attn_norm_g', 'delta_w_out', 'delta_g_post_mix', 'delta_g_pre_ffn', 'delta_w_gate', 'delta_w_up', 'delta_w_down', 'delta_g_post_ffn', 'new_m_g_pre_mix', 'new_m_w_in', 'new_m_conv_w', 'new_m_conv_b', 'new_m_conv_ln_g', 'new_m_conv_ln_b', 'new_m_attn_norm_g', 'new_m_w_out', 'new_m_g_post_mix', 'new_m_g_pre_ffn', 'new_m_w_gate', 'new_m_w_up', 'new_m_w_down', 'new_m_g_post_ffn', 'new_v_g_pre_mix', 'new_v_w_in', 'new_v_conv_w', 'new_v_conv_b', 'new_v_conv_ln_g', 'new_v_conv_ln_b', 'new_v_attn_norm_g', 'new_v_w_out', 'new_v_g_post_mix', 'new_v_g_pre_ffn', 'new_v_w_gate', 'new_v_w_up', 'new_v_w_down', 'new_v_g_post_ffn']
TWIN_LEAF_KINDS = {'loss': 'loss', 'grad_x': 'grad_x', 'grad_g_pre_mix': 'grad_w', 'grad_w_in': 'grad_w', 'grad_conv_w': 'grad_w', 'grad_conv_b': 'grad_w', 'grad_conv_ln_g': 'grad_w', 'grad_conv_ln_b': 'grad_w', 'grad_attn_norm_g': 'grad_w', 'grad_w_out': 'grad_w', 'grad_g_post_mix': 'grad_w', 'grad_g_pre_ffn': 'grad_w', 'grad_w_gate': 'grad_w', 'grad_w_up': 'grad_w', 'grad_w_down': 'grad_w', 'grad_g_post_ffn': 'grad_w', 'delta_g_pre_mix': 'delta_w', 'delta_w_in': 'delta_w', 'delta_conv_w': 'delta_w', 'delta_conv_b': 'delta_w', 'delta_conv_ln_g': 'delta_w', 'delta_conv_ln_b': 'delta_w', 'delta_attn_norm_g': 'delta_w', 'delta_w_out': 'delta_w', 'delta_g_post_mix': 'delta_w', 'delta_g_pre_ffn': 'delta_w', 'delta_w_gate': 'delta_w', 'delta_w_up': 'delta_w', 'delta_w_down': 'delta_w', 'delta_g_post_ffn': 'delta_w', 'new_m_g_pre_mix': 'new_m', 'new_m_w_in': 'new_m', 'new_m_conv_w': 'new_m', 'new_m_conv_b': 'new_m', 'new_m_conv_ln_g': 'new_m', 'new_m_conv_ln_b': 'new_m', 'new_m_attn_norm_g': 'new_m', 'new_m_w_out': 'new_m', 'new_m_g_post_mix': 'new_m', 'new_m_g_pre_ffn': 'new_m', 'new_m_w_gate': 'new_m', 'new_m_w_up': 'new_m', 'new_m_w_down': 'new_m', 'new_m_g_post_ffn': 'new_m', 'new_v_g_pre_mix': 'new_v', 'new_v_w_in': 'new_v', 'new_v_conv_w': 'new_v', 'new_v_conv_b': 'new_v', 'new_v_conv_ln_g': 'new_v', 'new_v_conv_ln_b': 'new_v', 'new_v_attn_norm_g': 'new_v', 'new_v_w_out': 'new_v', 'new_v_g_post_mix': 'new_v', 'new_v_g_pre_ffn': 'new_v', 'new_v_w_gate': 'new_v', 'new_v_w_up': 'new_v', 'new_v_w_down': 'new_v', 'new_v_g_post_ffn': 'new_v'}


def _forward(args):
    return _fwd_reference(*[args[k] for k in FWD_PARAMS])


def _output_shape():
    def fwd():
        inp = _fwd_setup_inputs(0)
        return _fwd_reference(*[inp[k] for k in FWD_PARAMS])
    out = _jax.eval_shape(fwd)
    return out.shape, out.dtype

N_MICROBATCH = 1
ADAM_LR = 0.001
ADAM_B1 = 0.9
ADAM_B2 = 0.999
ADAM_EPS = 1e-08
ADAM_WD = 0.01
ADAM_STEP = 10
PER_EXAMPLE_BATCH_AXIS = {'x': 0, 'loss_target': 0}
SHARED_INPUTS = []
_WEIGHT_DTYPES = {'g_pre_mix': _jnp.float32, 'w_in': _jnp.float32, 'conv_w': _jnp.float32, 'conv_b': _jnp.float32, 'conv_ln_g': _jnp.float32, 'conv_ln_b': _jnp.float32, 'attn_norm_g': _jnp.float32, 'w_out': _jnp.float32, 'g_post_mix': _jnp.float32, 'g_pre_ffn': _jnp.float32, 'w_gate': _jnp.float32, 'w_up': _jnp.float32, 'w_down': _jnp.float32, 'g_post_ffn': _jnp.float32}
MOMENT_SCALE = {'g_pre_mix': 8.540406e-01, 'w_in': 5.030409e-01, 'conv_w': 5.426342e-01, 'conv_b': 4.943585e+00, 'conv_ln_g': 2.233058e+00, 'conv_ln_b': 3.314206e+00, 'attn_norm_g': 1.458622e+00, 'w_out': 1.088030e+00, 'g_post_mix': 6.433356e+01, 'g_pre_ffn': 1.120779e+00, 'w_gate': 3.525066e-01, 'w_up': 6.109893e-01, 'w_down': 1.025550e+00, 'g_post_ffn': 6.413683e+01}


def _to_microbatches(a, axis):
    t = _jnp.moveaxis(a, axis, 0)
    t = t.reshape((N_MICROBATCH, t.shape[0] // N_MICROBATCH) + t.shape[1:])
    return _jnp.moveaxis(t, 1, axis + 1)


def setup_inputs(seed: int = 0) -> dict:
    inp = _fwd_setup_inputs(seed)
    key = _jax.random.fold_in(_jax.random.key(seed), 7919)
    shape, _ = _output_shape()
    out = dict(inp)
    out["loss_target"] = _jax.random.normal(_jax.random.fold_in(key, 0), shape, _jnp.float32)
    for i, name in enumerate(TWIN_WEIGHTS):
        w = inp[name].astype(_jnp.float32)
        if MOMENT_SCALE is None:
            s = _jnp.sqrt(_jnp.mean(_jnp.square(w)) + 1e-30)
        else:
            s = MOMENT_SCALE[name]
        km, kv = _jax.random.split(_jax.random.fold_in(key, i + 1))
        out[name] = w
        out["m_" + name] = s * _jax.random.normal(km, w.shape, _jnp.float32)
        out["v_" + name] = (s * s) * _jax.random.uniform(kv, w.shape, _jnp.float32, 0.5, 1.5)
    if N_MICROBATCH > 1:
        for name, axis in PER_EXAMPLE_BATCH_AXIS.items():
            out[name] = _to_microbatches(out[name], axis)
    return {'x': out['x'], 'g_pre_mix': out['g_pre_mix'], 'w_in': out['w_in'], 'conv_w': out['conv_w'], 'conv_b': out['conv_b'], 'conv_ln_g': out['conv_ln_g'], 'conv_ln_b': out['conv_ln_b'], 'attn_norm_g': out['attn_norm_g'], 'w_out': out['w_out'], 'g_post_mix': out['g_post_mix'], 'g_pre_ffn': out['g_pre_ffn'], 'w_gate': out['w_gate'], 'w_up': out['w_up'], 'w_down': out['w_down'], 'g_post_ffn': out['g_post_ffn'], 'loss_target': out['loss_target'], 'm_g_pre_mix': out['m_g_pre_mix'], 'm_w_in': out['m_w_in'], 'm_conv_w': out['m_conv_w'], 'm_conv_b': out['m_conv_b'], 'm_conv_ln_g': out['m_conv_ln_g'], 'm_conv_ln_b': out['m_conv_ln_b'], 'm_attn_norm_g': out['m_attn_norm_g'], 'm_w_out': out['m_w_out'], 'm_g_post_mix': out['m_g_post_mix'], 'm_g_pre_ffn': out['m_g_pre_ffn'], 'm_w_gate': out['m_w_gate'], 'm_w_up': out['m_w_up'], 'm_w_down': out['m_w_down'], 'm_g_post_ffn': out['m_g_post_ffn'], 'v_g_pre_mix': out['v_g_pre_mix'], 'v_w_in': out['v_w_in'], 'v_conv_w': out['v_conv_w'], 'v_conv_b': out['v_conv_b'], 'v_conv_ln_g': out['v_conv_ln_g'], 'v_conv_ln_b': out['v_conv_ln_b'], 'v_attn_norm_g': out['v_attn_norm_g'], 'v_w_out': out['v_w_out'], 'v_g_post_mix': out['v_g_post_mix'], 'v_g_pre_ffn': out['v_g_pre_ffn'], 'v_w_gate': out['v_w_gate'], 'v_w_up': out['v_w_up'], 'v_w_down': out['v_w_down'], 'v_g_post_ffn': out['v_g_post_ffn']}


def _loss(weights, diff, rest, loss_target):
    with _jax.named_scope("forward"):
        args = {**rest, TWIN_DIFF_INPUT: diff, **{k: w.astype(_WEIGHT_DTYPES[k]) for k, w in weights.items()}}
        y = _forward(args)
    with _jax.named_scope("loss_head"):
        err = _jnp.square(y.astype(_jnp.float32) - loss_target)
        return 0.5 * _jnp.sum(_jnp.mean(err, axis=-1)) if err.ndim else 0.5 * err


def _adamw(w, g, m, v):
    m = ADAM_B1 * m + (1.0 - ADAM_B1) * g
    v = ADAM_B2 * v + (1.0 - ADAM_B2) * _jnp.square(g)
    m_hat = m / (1.0 - ADAM_B1 ** ADAM_STEP)
    v_hat = v / (1.0 - ADAM_B2 ** ADAM_STEP)
    delta = -ADAM_LR * (m_hat / (_jnp.sqrt(v_hat) + ADAM_EPS) + ADAM_WD * w)
    return delta, m, v


def reference(x, g_pre_mix, w_in, conv_w, conv_b, conv_ln_g, conv_ln_b, attn_norm_g, w_out, g_post_mix, g_pre_ffn, w_gate, w_up, w_down, g_post_ffn, loss_target, m_g_pre_mix, m_w_in, m_conv_w, m_conv_b, m_conv_ln_g, m_conv_ln_b, m_attn_norm_g, m_w_out, m_g_post_mix, m_g_pre_ffn, m_w_gate, m_w_up, m_w_down, m_g_post_ffn, v_g_pre_mix, v_w_in, v_conv_w, v_conv_b, v_conv_ln_g, v_conv_ln_b, v_attn_norm_g, v_w_out, v_g_post_mix, v_g_pre_ffn, v_w_gate, v_w_up, v_w_down, v_g_post_ffn):
    given = dict(x=x, g_pre_mix=g_pre_mix, w_in=w_in, conv_w=conv_w, conv_b=conv_b, conv_ln_g=conv_ln_g, conv_ln_b=conv_ln_b, attn_norm_g=attn_norm_g, w_out=w_out, g_post_mix=g_post_mix, g_pre_ffn=g_pre_ffn, w_gate=w_gate, w_up=w_up, w_down=w_down, g_post_ffn=g_post_ffn, loss_target=loss_target, m_g_pre_mix=m_g_pre_mix, m_w_in=m_w_in, m_conv_w=m_conv_w, m_conv_b=m_conv_b, m_conv_ln_g=m_conv_ln_g, m_conv_ln_b=m_conv_ln_b, m_attn_norm_g=m_attn_norm_g, m_w_out=m_w_out, m_g_post_mix=m_g_post_mix, m_g_pre_ffn=m_g_pre_ffn, m_w_gate=m_w_gate, m_w_up=m_w_up, m_w_down=m_w_down, m_g_post_ffn=m_g_post_ffn, v_g_pre_mix=v_g_pre_mix, v_w_in=v_w_in, v_conv_w=v_conv_w, v_conv_b=v_conv_b, v_conv_ln_g=v_conv_ln_g, v_conv_ln_b=v_conv_ln_b, v_attn_norm_g=v_attn_norm_g, v_w_out=v_w_out, v_g_post_mix=v_g_post_mix, v_g_pre_ffn=v_g_pre_ffn, v_w_gate=v_w_gate, v_w_up=v_w_up, v_w_down=v_w_down, v_g_post_ffn=v_g_post_ffn)
    weights = {n: given[n] for n in TWIN_WEIGHTS}
    shared = {n: given[n] for n in SHARED_INPUTS}
    per_example = {n: given[n] for n in ['x']}
    grad_fn = _jax.value_and_grad(_loss, argnums=(0, 1))

    def one_microbatch(ex, loss_target):
        ex = dict(ex)
        diff = ex.pop(TWIN_DIFF_INPUT)
        return grad_fn(weights, diff, {**shared, **ex}, loss_target)

    if N_MICROBATCH == 1:
        loss, (grad_w, grad_x) = one_microbatch(per_example, given["loss_target"])
    else:
        def body(carry, xs):
            loss_sum, grad_sum = carry
            l_k, (gw_k, gx_k) = one_microbatch(xs[0], xs[1])
            with _jax.named_scope("update"):
                return (loss_sum + l_k, _jax.tree.map(_jnp.add, grad_sum, gw_k)), gx_k

        init = (_jnp.zeros((), _jnp.float32), _jax.tree.map(_jnp.zeros_like, weights))
        (loss, grad_w), grad_x = _jax.lax.scan(body, init, (per_example, given["loss_target"]))
    with _jax.named_scope("update"):
        delta_w, new_m, new_v = {}, {}, {}
        for n in TWIN_WEIGHTS:
            delta_w[n], new_m[n], new_v[n] = _adamw(weights[n], grad_w[n], given["m_" + n], given["v_" + n])
    return (loss, grad_x, *[grad_w[n] for n in TWIN_WEIGHTS], *[delta_w[n] for n in TWIN_WEIGHTS],
            *[new_m[n] for n in TWIN_WEIGHTS], *[new_v[n] for n in TWIN_WEIGHTS])
```

```python
import functools
import math

import jax
import jax.numpy as jnp
from jax import lax
from jax.experimental import pallas as pl
from jax.experimental.pallas import tpu as pltpu

F32 = jnp.float32
BF16 = jnp.bfloat16
MESH = pl.DeviceIdType.MESH

N_DEV = 8
D_MODEL = 1024
CONV_CH = 512
CONV_WIDTH = 31
SB_HEADS = 8
SB_HEAD_DIM = 64
SB_WIDTH = SB_HEADS * SB_HEAD_DIM
D_FF = 2816
EPS = 1e-6
ADAM_LR = 0.001
ADAM_B1 = 0.9
ADAM_B2 = 0.999
ADAM_EPS = 1e-08
ADAM_WD = 0.01
ADAM_STEP = 10

LANES = 128
VMEM_LIMIT = 56 * 1024 * 1024
TOKEN_TILE = 512
FFN_TILE = 256
ATTN_BLOCK = 256
CONV_HALO = 32
CONV_CHUNK = 64
FF_CHUNK = D_FF // 2


def _cparams(**kw):
    return pltpu.CompilerParams(vmem_limit_bytes=VMEM_LIMIT, **kw)


def _resident(shape):
    return pl.BlockSpec(shape, lambda *_: (0,) * len(shape), pipeline_mode=pl.Buffered(1))


def _const(shape):
    return pl.BlockSpec(shape, lambda *_: (0,) * len(shape))


def _rms_r(xf):
    return lax.rsqrt(jnp.mean(xf * xf, axis=-1, keepdims=True) + EPS)


def _rms_bwd(xf, r, g, dout):
    w = dout * g
    return r * (w - xf * (r * r) * jnp.mean(w * xf, axis=-1, keepdims=True))


def _dot(a, b):
    return jnp.dot(a, b, preferred_element_type=F32)


def _dot_nt(a, b):
    return lax.dot_general(a, b, (((1,), (1,)), ((), ())), preferred_element_type=F32)


def _dot_tn(a, b):
    return lax.dot_general(a, b, (((0,), (0,)), ((), ())), preferred_element_type=F32)


def _peer(x, y, c, k):
    px = 1 - x if (k >> 2) & 1 else x
    py = 1 - y if (k >> 1) & 1 else y
    pc = 1 - c if k & 1 else c
    return (px, py, pc), 4 * px + 2 * py + pc


def _all_gather(shards):
    n = len(shards)

    def body(*refs):
        ins, outs = refs[:n], refs[n:2 * n]
        send_sems, recv_sems, local_sems = refs[2 * n:]
        x, y, c = lax.axis_index("x"), lax.axis_index("y"), lax.axis_index("c")
        me = 4 * x + 2 * y + c
        copies = []
        for a in range(n):
            mine = pltpu.make_async_copy(ins[a], outs[a].at[me], local_sems.at[a])
            mine.start()
            copies.append(mine)
        for k in range(1, N_DEV):
            peer, peer_block = _peer(x, y, c, k)
            for a in range(n):
                s = a * (N_DEV - 1) + k - 1
                pltpu.make_async_remote_copy(
                    src_ref=ins[a], dst_ref=outs[a].at[me], send_sem=send_sems.at[s], recv_sem=recv_sems.at[s],
                    device_id=peer, device_id_type=MESH).start()
        for k in range(1, N_DEV):
            peer, peer_block = _peer(x, y, c, k)
            for a in range(n):
                s = a * (N_DEV - 1) + k - 1
                arrived = pltpu.make_async_remote_copy(
                    src_ref=ins[a], dst_ref=outs[a].at[peer_block], send_sem=send_sems.at[s],
                    recv_sem=recv_sems.at[s], device_id=peer, device_id_type=MESH)
                arrived.wait_send()
                arrived.wait_recv()
        for mine in copies:
            mine.wait()

    any_spec = pl.BlockSpec(memory_space=pl.ANY)
    return pl.pallas_call(
        body, name="all_gather_weights",
        out_shape=[jax.ShapeDtypeStruct((N_DEV,) + s.shape, s.dtype) for s in shards],
        in_specs=[any_spec] * n, out_specs=[any_spec] * n,
        scratch_shapes=[pltpu.SemaphoreType.DMA((n * (N_DEV - 1),)), pltpu.SemaphoreType.DMA((n * (N_DEV - 1),)),
                        pltpu.SemaphoreType.DMA((n,))],
        compiler_params=pltpu.CompilerParams(has_side_effects=True),
    )(*shards)


def _adamw(w, g, m, v):
    m = ADAM_B1 * m + (1.0 - ADAM_B1) * g
    v = ADAM_B2 * v + (1.0 - ADAM_B2) * (g * g)
    m_hat = m / (1.0 - ADAM_B1 ** ADAM_STEP)
    v_hat = v / (1.0 - ADAM_B2 ** ADAM_STEP)
    delta = -ADAM_LR * (m_hat / (jnp.sqrt(v_hat) + ADAM_EPS) + ADAM_WD * w)
    return delta, m, v


def _exchange_and_sum(src_block, recv_ref, send_sems, recv_sems, local_sem):
    x, y, c = lax.axis_index("x"), lax.axis_index("y"), lax.axis_index("c")
    me = 4 * x + 2 * y + c
    mine = pltpu.make_async_copy(src_block(me), recv_ref.at[me], local_sem)
    mine.start()
    for k in range(1, N_DEV):
        peer, peer_block = _peer(x, y, c, k)
        pltpu.make_async_remote_copy(
            src_ref=src_block(peer_block), dst_ref=recv_ref.at[me], send_sem=send_sems.at[k - 1],
            recv_sem=recv_sems.at[k - 1], device_id=peer, device_id_type=MESH).start()
    for k in range(1, N_DEV):
        peer, peer_block = _peer(x, y, c, k)
        arrived = pltpu.make_async_remote_copy(
            src_ref=src_block(peer_block), dst_ref=recv_ref.at[peer_block], send_sem=send_sems.at[k - 1],
            recv_sem=recv_sems.at[k - 1], device_id=peer, device_id_type=MESH)
        arrived.wait_send()
        arrived.wait_recv()
    mine.wait()


def _reduce_scatter_adamw(name, g_blocks, w, m, v):
    _, M, N = g_blocks.shape
    rows = math.gcd(M, 32)

    def body(g_ref, w_ref, m_ref, v_ref, grad_ref, delta_ref, nm_ref, nv_ref, recv_ref, send_sems, recv_sems, local_sem):
        _exchange_and_sum(lambda b: g_ref.at[b], recv_ref, send_sems, recv_sems, local_sem)

        def chunk(i, carry):
            r = pl.ds(pl.multiple_of(i * rows, rows), rows)
            g = recv_ref[0, r, :]
            for b in range(1, N_DEV):
                g = g + recv_ref[b, r, :]
            delta, nm, nv = _adamw(w_ref[r, :], g, m_ref[r, :], v_ref[r, :])
            grad_ref[r, :] = g
            delta_ref[r, :] = delta
            nm_ref[r, :] = nm
            nv_ref[r, :] = nv
            return carry

        lax.fori_loop(0, M // rows, chunk, 0)

    vmem = pl.BlockSpec(memory_space=pltpu.VMEM)
    shard = jax.ShapeDtypeStruct((M, N), F32)
    return pl.pallas_call(
        body, name=name, out_shape=[shard] * 4,
        in_specs=[pl.BlockSpec(memory_space=pl.ANY), vmem, vmem, vmem], out_specs=[vmem] * 4,
        scratch_shapes=[pltpu.VMEM((N_DEV, M, N), F32), pltpu.SemaphoreType.DMA((N_DEV - 1,)),
                        pltpu.SemaphoreType.DMA((N_DEV - 1,)), pltpu.SemaphoreType.DMA(())],
        compiler_params=_cparams(has_side_effects=True),
    )(g_blocks, w, m, v)


def _all_reduce_small(g):
    R, C = g.shape

    def body(g_ref, out_ref, recv_ref, send_sems, recv_sems, local_sem):
        _exchange_and_sum(lambda b: g_ref, recv_ref, send_sems, recv_sems, local_sem)
        total = recv_ref[0]
        for b in range(1, N_DEV):
            total = total + recv_ref[b]
        out_ref[...] = total

    vmem = pl.BlockSpec(memory_space=pltpu.VMEM)
    return pl.pallas_call(
        body, name="all_reduce_small_grads", out_shape=jax.ShapeDtypeStruct((R, C), F32),
        in_specs=[vmem], out_specs=vmem,
        scratch_shapes=[pltpu.VMEM((N_DEV, R, C), F32), pltpu.SemaphoreType.DMA((N_DEV - 1,)),
                        pltpu.SemaphoreType.DMA((N_DEV - 1,)), pltpu.SemaphoreType.DMA(())],
        compiler_params=_cparams(has_side_effects=True),
    )(g)


def _adamw_small(name, w, g, m, v):
    def body(w_ref, g_ref, m_ref, v_ref, delta_ref, nm_ref, nv_ref):
        delta, nm, nv = _adamw(w_ref[...], g_ref[...], m_ref[...], v_ref[...])
        delta_ref[...] = delta
        nm_ref[...] = nm
        nv_ref[...] = nv

    vmem = pl.BlockSpec(memory_space=pltpu.VMEM)
    return pl.pallas_call(body, name=name, out_shape=[jax.ShapeDtypeStruct(w.shape, F32)] * 3,
                          in_specs=[vmem] * 4, out_specs=[vmem] * 3)(w, g, m, v)


def _in_proj(x, g, w_in):
    S = x.shape[0]
    tm = min(TOKEN_TILE, S)
    nconv = 2 * CONV_CH

    def body(x_ref, g_ref, w_ref, a_ref, uc_ref, qkv_ref):
        xf = x_ref[...]
        a = (xf * _rms_r(xf) * g_ref[...]).astype(BF16)
        a_ref[...] = a
        uc_ref[...] = _dot(a, w_ref[:, :nconv])
        qkv_ref[:, :SB_WIDTH] = (_dot(a, w_ref[:, nconv:nconv + SB_WIDTH]) * (1.0 / math.sqrt(SB_HEAD_DIM))).astype(BF16)
        qkv_ref[:, SB_WIDTH:] = _dot(a, w_ref[:, nconv + SB_WIDTH:]).astype(BF16)

    row = lambda n: pl.BlockSpec((tm, n), lambda i: (i, 0))
    return pl.pallas_call(
        body, name="in_proj", grid=(S // tm,),
        out_shape=[jax.ShapeDtypeStruct((S, D_MODEL), BF16), jax.ShapeDtypeStruct((S, nconv), F32),
                   jax.ShapeDtypeStruct((S, 3 * SB_WIDTH), BF16)],
        in_specs=[row(D_MODEL), _const((1, D_MODEL)), _resident(w_in.shape)],
        out_specs=[row(D_MODEL), row(nconv), row(3 * SB_WIDTH)],
        compiler_params=_cparams(),
    )(x, g, w_in)


def _glu(u):
    val, gate = u[:, :CONV_CH], u[:, CONV_CH:]
    sg = jax.nn.sigmoid(gate)
    return val, sg, val * sg


def _conv_rows(glu_ext, cw_ref, r0, rows):
    base = r0 + CONV_HALO - (CONV_WIDTH - 1)
    acc = cw_ref[0:1, :] * glu_ext[base:base + rows, :]
    for w in range(1, CONV_WIDTH):
        acc = acc + cw_ref[w:w + 1, :] * glu_ext[base + w:base + w + rows, :]
    return acc


def _conv_fwd(u_conv, conv_w, conv_b, ln_g, ln_b):
    S = u_conv.shape[0]
    tc = min(TOKEN_TILE, S)

    def body(u_ref, cw_ref, cb_ref, lg_ref, lb_ref, out_ref, glu_ext):
        i = pl.program_id(0)

        @pl.when(i == 0)
        def _():
            glu_ext[0:CONV_HALO, :] = jnp.zeros((CONV_HALO, CONV_CH), F32)

        @pl.when(i > 0)
        def _():
            glu_ext[0:CONV_HALO, :] = glu_ext[tc:tc + CONV_HALO, :]

        glu_ext[CONV_HALO:, :] = _glu(u_ref[...])[2]
        for r0 in range(0, tc, CONV_CHUNK):
            y = _conv_rows(glu_ext, cw_ref, r0, CONV_CHUNK) + cb_ref[...]
            mu = jnp.mean(y, axis=-1, keepdims=True)
            yc = y - mu
            yn = yc * lax.rsqrt(jnp.mean(yc * yc, axis=-1, keepdims=True) + EPS)
            yl = yn * lg_ref[...] + lb_ref[...]
            out_ref[r0:r0 + CONV_CHUNK, :] = (yl * jax.nn.sigmoid(yl)).astype(BF16)

    return pl.pallas_call(
        body, name="conv_fwd", grid=(S // tc,),
        out_shape=jax.ShapeDtypeStruct((S, CONV_CH), BF16),
        in_specs=[pl.BlockSpec((tc, 2 * CONV_CH), lambda i: (i, 0)), _const((CONV_HALO, CONV_CH)),
                  _const((1, CONV_CH)), _const((1, CONV_CH)), _const((1, CONV_CH))],
        out_specs=pl.BlockSpec((tc, CONV_CH), lambda i: (i, 0)),
        scratch_shapes=[pltpu.VMEM((tc + CONV_HALO, CONV_CH), F32)],
        compiler_params=_cparams(dimension_semantics=("arbitrary",)),
    )(u_conv, conv_w, conv_b, ln_g, ln_b)


def _head_masks():
    lane = lax.broadcasted_iota(jnp.int32, (1, LANES), 1)
    return lane < SB_HEAD_DIM


def _split_heads(t, first):
    z = jnp.zeros_like(t)
    return jnp.where(first, t, z), jnp.where(first, z, t)


def _head_sum(t, first):
    a = jnp.sum(jnp.where(first, t, 0.0), axis=-1, keepdims=True)
    b = jnp.sum(jnp.where(first, 0.0, t), axis=-1, keepdims=True)
    return a, b


def _log_sigmoids(z):
    lp = jnp.log(1.0 + jnp.exp(-jnp.abs(z)))
    return jnp.minimum(z, 0.0) - lp, -jnp.maximum(z, 0.0) - lp


def _split_bf16(t):
    hi = t.astype(BF16)
    return hi, (t - hi.astype(F32)).astype(BF16)


def _block_masks(n):
    row = lax.broadcasted_iota(jnp.int32, (n, n), 0)
    col = lax.broadcasted_iota(jnp.int32, (n, n), 1)
    return row > col, row < col


def _attn_fwd(qkv, g_attn):
    S = qkv.shape[0]
    Q = min(ATTN_BLOCK, S)
    nq = S // Q
    assert nq <= LANES
    npair = SB_WIDTH // LANES

    def body(q_ref, k_ref, v_ref, g_ref, o_ref, ao_ref, cl_ref):
        i = pl.program_id(1)
        first = _head_masks()
        qh = _split_heads(q_ref[...], first)
        below, _ = _block_masks(Q)
        tri = below.astype(BF16)
        lane = lax.broadcasted_iota(jnp.int32, (1, LANES), 1)

        def step(j, carry, diag):
            c, cl, acc = list(carry[0]), list(carry[1]), carry[2]
            ks = pl.multiple_of(j * Q, Q)
            kb = k_ref[pl.ds(ks, Q), :]
            vh = _split_heads(v_ref[pl.ds(ks, Q), :], first)
            for h in range(2):
                z = _dot_nt(qh[h], kb)
                lb, l = _log_sigmoids(z)
                if diag:
                    l = jnp.where(below, l, 0.0)
                hi, lo = _split_bf16(l)
                a = jnp.exp(lb + (_dot(hi, tri) + _dot(lo, tri)) + c[h])
                if diag:
                    a = jnp.where(below, a, 0.0)
                cl[h] = jnp.where(lane == j, c[h], cl[h])
                c[h] = c[h] + jnp.sum(l, axis=-1, keepdims=True)
                acc = acc + _dot(a.astype(BF16), vh[h])
            return (c[0], c[1]), (cl[0], cl[1]), acc

        zc = jnp.zeros((Q, 1), F32)
        zl = jnp.zeros((Q, LANES), F32)
        carry = step(i, ((zc, zc), (zl, zl), jnp.zeros((Q, LANES), F32)), True)
        carry = lax.fori_loop(0, i, lambda jj, cr: step(i - 1 - jj, cr, False), carry)
        _, cl, o = carry
        sa, sb = _head_sum(o * o, first)
        r = jnp.where(first, lax.rsqrt(sa * (1.0 / SB_HEAD_DIM) + EPS), lax.rsqrt(sb * (1.0 / SB_HEAD_DIM) + EPS))
        o_ref[...] = o
        ao_ref[...] = (o * r * g_ref[...]).astype(BF16)
        cl_ref[:, :LANES] = cl[0]
        cl_ref[:, LANES:] = cl[1]

    kv = lambda off: pl.BlockSpec((S, LANES), lambda p, i: (0, off + p))
    return pl.pallas_call(
        body, name="attn_fwd", grid=(npair, nq),
        out_shape=[jax.ShapeDtypeStruct((S, SB_WIDTH), F32), jax.ShapeDtypeStruct((S, SB_WIDTH), BF16),
                   jax.ShapeDtypeStruct((S, 2 * SB_WIDTH), F32)],
        in_specs=[pl.BlockSpec((Q, LANES), lambda p, i: (i, p)), kv(npair), kv(2 * npair),
                  pl.BlockSpec((1, LANES), lambda p, i: (0, p))],
        out_specs=[pl.BlockSpec((Q, LANES), lambda p, i: (i, p)), pl.BlockSpec((Q, LANES), lambda p, i: (i, p)),
                   pl.BlockSpec((Q, 2 * LANES), lambda p, i: (i, p))],
        compiler_params=_cparams(dimension_semantics=("arbitrary", "arbitrary")),
    )(qkv, qkv, qkv, g_attn)


def _out_proj(conv_out, attn_out, w_out, x, g_post_mix, g_pre_ffn):
    S = x.shape[0]
    tm = min(TOKEN_TILE, S)

    def body(co_ref, ao_ref, w_ref, x_ref, g1_ref, g2_ref, y_ref, h1_ref, fin_ref):
        y = _dot(co_ref[...], w_ref[:CONV_CH, :]) + _dot(ao_ref[...], w_ref[CONV_CH:, :])
        h1 = x_ref[...] + y * _rms_r(y) * g1_ref[...]
        y_ref[...] = y
        h1_ref[...] = h1
        fin_ref[...] = (h1 * _rms_r(h1) * g2_ref[...]).astype(BF16)

    row = lambda n: pl.BlockSpec((tm, n), lambda i: (i, 0))
    return pl.pallas_call(
        body, name="out_proj", grid=(S // tm,),
        out_shape=[jax.ShapeDtypeStruct((S, D_MODEL), F32), jax.ShapeDtypeStruct((S, D_MODEL), F32),
                   jax.ShapeDtypeStruct((S, D_MODEL), BF16)],
        in_specs=[row(CONV_CH), row(SB_WIDTH), _resident(w_out.shape), row(D_MODEL), _const((1, D_MODEL)),
                  _const((1, D_MODEL))],
        out_specs=[row(D_MODEL)] * 3,
        compiler_params=_cparams(),
    )(conv_out, attn_out, w_out, x, g_post_mix, g_pre_ffn)


def _ffn_fwd_loss(f_in, w_gate, w_up, w_down, h1, target, g_post_ffn):
    S = f_in.shape[0]
    tm = min(FFN_TILE, S)
    nt = S // tm

    def body(fin_ref, wg_ref, wu_ref, wd_ref, h1_ref, t_ref, g_ref, gt_ref, up_ref, df_ref, dh2_ref, loss_ref, dg_ref,
             sq_acc):
        i = pl.program_id(0)

        @pl.when(i == 0)
        def _():
            sq_acc[...] = jnp.zeros_like(sq_acc)
            dg_ref[...] = jnp.zeros_like(dg_ref)

        fin = fin_ref[...]
        f = jnp.zeros((tm, D_MODEL), F32)
        for c0 in range(0, D_FF, FF_CHUNK):
            cols = slice(c0, c0 + FF_CHUNK)
            gt = _dot(fin, wg_ref[:, cols])
            up = _dot(fin, wu_ref[:, cols])
            gt_ref[:, cols] = gt.astype(BF16)
            up_ref[:, cols] = up.astype(BF16)
            f = f + _dot((gt * jax.nn.sigmoid(gt) * up).astype(BF16), wd_ref[cols, :])
        r = _rms_r(f)
        g = g_ref[...]
        diff = h1_ref[...] + f * r * g - t_ref[...]
        sq_acc[...] += jnp.sum(diff * diff, axis=0, keepdims=True)
        dh2 = diff * (1.0 / D_MODEL)
        dh2_ref[...] = dh2
        dg_ref[...] += jnp.sum(dh2 * f * r, axis=0, keepdims=True)
        df_ref[...] = _rms_bwd(f, r, g, dh2).astype(BF16)

        @pl.when(i == nt - 1)
        def _():
            loss_ref[...] = jnp.broadcast_to((0.5 / D_MODEL) * jnp.sum(sq_acc[...], axis=-1, keepdims=True), (1, LANES))

    row = lambda n: pl.BlockSpec((tm, n), lambda i: (i, 0))
    return pl.pallas_call(
        body, name="ffn_fwd_loss", grid=(nt,),
        out_shape=[jax.ShapeDtypeStruct((S, D_FF), BF16), jax.ShapeDtypeStruct((S, D_FF), BF16),
                   jax.ShapeDtypeStruct((S, D_MODEL), BF16), jax.ShapeDtypeStruct((S, D_MODEL), F32),
                   jax.ShapeDtypeStruct((1, LANES), F32), jax.ShapeDtypeStruct((1, D_MODEL), F32)],
        in_specs=[row(D_MODEL), _resident(w_gate.shape), _resident(w_up.shape), _resident(w_down.shape),
                  row(D_MODEL), row(D_MODEL), _const((1, D_MODEL))],
        out_specs=[row(D_FF), row(D_FF), row(D_MODEL), row(D_MODEL), _const((1, LANES)), _const((1, D_MODEL))],
        scratch_shapes=[pltpu.VMEM((1, D_MODEL), F32)],
        compiler_params=_cparams(dimension_semantics=("arbitrary",)),
    )(f_in, w_gate, w_up, w_down, h1, target, g_post_ffn)


def _ffn_bwd(df, gt, up, w_gate, w_up, w_down):
    S = df.shape[0]
    tm = min(FFN_TILE, S)

    def body(df_ref, gt_ref, up_ref, wg_ref, wu_ref, wd_ref, dgt_ref, dup_ref, act_ref, dfin_ref):
        df = df_ref[...]
        dfin = jnp.zeros((tm, D_MODEL), F32)
        for c0 in range(0, D_FF, FF_CHUNK):
            cols = slice(c0, c0 + FF_CHUNK)
            dact = _dot_nt(df, wd_ref[cols, :])
            gt = gt_ref[:, cols].astype(F32)
            up = up_ref[:, cols].astype(F32)
            s = jax.nn.sigmoid(gt)
            silu = gt * s
            dgt = (dact * up * (s * (1.0 + gt * (1.0 - s)))).astype(BF16)
            dup = (dact * silu).astype(BF16)
            act_ref[:, cols] = (silu * up).astype(BF16)
            dgt_ref[:, cols] = dgt
            dup_ref[:, cols] = dup
            dfin = dfin + _dot_nt(dgt, wg_ref[:, cols]) + _dot_nt(dup, wu_ref[:, cols])
        dfin_ref[...] = dfin

    row = lambda n: pl.BlockSpec((tm, n), lambda i: (i, 0))
    return pl.pallas_call(
        body, name="ffn_bwd", grid=(S // tm,),
        out_shape=[jax.ShapeDtypeStruct((S, D_FF), BF16)] * 3 + [jax.ShapeDtypeStruct((S, D_MODEL), F32)],
        in_specs=[row(D_MODEL), row(D_FF), row(D_FF), _resident(w_gate.shape), _resident(w_up.shape),
                  _resident(w_down.shape)],
        out_specs=[row(D_FF)] * 3 + [row(D_MODEL)],
        compiler_params=_cparams(),
    )(df, gt, up, w_gate, w_up, w_down)


def _matmul_tn(name, x, y, tn):
    S, K = x.shape
    N = y.shape[1]
    ts = min(TOKEN_TILE, S)

    def body(x_ref, y_ref, o_ref):
        @pl.when(pl.program_id(1) == 0)
        def _():
            o_ref[...] = jnp.zeros_like(o_ref)

        o_ref[...] += _dot_tn(x_ref[...].astype(BF16), y_ref[...].astype(BF16))

    return pl.pallas_call(
        body, name=name, grid=(N // tn, S // ts),
        out_shape=jax.ShapeDtypeStruct((K, N), F32),
        in_specs=[pl.BlockSpec((ts, K), lambda n, s: (s, 0)), pl.BlockSpec((ts, tn), lambda n, s: (s, n))],
        out_specs=pl.BlockSpec((K, tn), lambda n, s: (0, n)),
        compiler_params=_cparams(dimension_semantics=("arbitrary", "arbitrary")),
    )(x, y)


def _mix_bwd(dfin, h1, y, dh2, g_pre_ffn, g_post_mix, w_out):
    S = dfin.shape[0]
    tm = min(TOKEN_TILE, S)

    def body(dfin_ref, h1_ref, y_ref, dh2_ref, g2_ref, g1_ref, w_ref, dh1_ref, dy_ref, dco_ref, dao_ref, dg2_ref, dg1_ref):
        @pl.when(pl.program_id(0) == 0)
        def _():
            dg2_ref[...] = jnp.zeros_like(dg2_ref)
            dg1_ref[...] = jnp.zeros_like(dg1_ref)

        h1, dfin = h1_ref[...], dfin_ref[...]
        r2 = _rms_r(h1)
        dh1 = dh2_ref[...] + _rms_bwd(h1, r2, g2_ref[...], dfin)
        dg2_ref[...] += jnp.sum(dfin * h1 * r2, axis=0, keepdims=True)
        y = y_ref[...]
        r1 = _rms_r(y)
        dy = _rms_bwd(y, r1, g1_ref[...], dh1).astype(BF16)
        dg1_ref[...] += jnp.sum(dh1 * y * r1, axis=0, keepdims=True)
        dh1_ref[...] = dh1
        dy_ref[...] = dy
        dco_ref[...] = _dot_nt(dy, w_ref[:CONV_CH, :])
        dao_ref[...] = _dot_nt(dy, w_ref[CONV_CH:, :])

    row = lambda n: pl.BlockSpec((tm, n), lambda i: (i, 0))
    return pl.pallas_call(
        body, name="mix_bwd", grid=(S // tm,),
        out_shape=[jax.ShapeDtypeStruct((S, D_MODEL), F32), jax.ShapeDtypeStruct((S, D_MODEL), BF16),
                   jax.ShapeDtypeStruct((S, CONV_CH), F32), jax.ShapeDtypeStruct((S, SB_WIDTH), F32),
                   jax.ShapeDtypeStruct((1, D_MODEL), F32), jax.ShapeDtypeStruct((1, D_MODEL), F32)],
        in_specs=[row(D_MODEL)] * 4 + [_const((1, D_MODEL)), _const((1, D_MODEL)), _resident(w_out.shape)],
        out_specs=[row(D_MODEL), row(D_MODEL), row(CONV_CH), row(SB_WIDTH), _const((1, D_MODEL)), _const((1, D_MODEL))],
        compiler_params=_cparams(dimension_semantics=("arbitrary",)),
    )(dfin, h1, y, dh2, g_pre_ffn, g_post_mix, w_out)


def _attn_bwd(qkv, o, dao, cl, g_attn):
    S = qkv.shape[0]
    Q = min(ATTN_BLOCK, S)
    nq = S // Q
    npair = SB_WIDTH // LANES
    inv_dh = 1.0 / SB_HEAD_DIM

    def body(q_ref, k_ref, v_ref, o_ref, dao_ref, cl_ref, g_ref, dq_ref, dk_ref, dv_ref, dg_ref):
        i = pl.program_id(1)

        @pl.when(i == 0)
        def _():
            dk_ref[...] = jnp.zeros_like(dk_ref)
            dv_ref[...] = jnp.zeros_like(dv_ref)
            dg_ref[...] = jnp.zeros_like(dg_ref)

        first = _head_masks()
        lane = lax.broadcasted_iota(jnp.int32, (1, LANES), 1)
        o, dao, g = o_ref[...], dao_ref[...], g_ref[...]
        sa, sb = _head_sum(o * o, first)
        r = jnp.where(first, lax.rsqrt(sa * inv_dh + EPS), lax.rsqrt(sb * inv_dh + EPS))
        w = dao * g
        wa, wb = _head_sum(w * o, first)
        do = r * (w - o * (r * r) * (jnp.where(first, wa, wb) * inv_dh))
        dg_ref[...] += jnp.sum(dao * o * r, axis=0, keepdims=True)
        doh = _split_heads(do.astype(BF16), first)

        qh = _split_heads(q_ref[...], first)
        below, above = _block_masks(Q)
        tri = below.astype(BF16)
        trip = above.astype(BF16)
        clh = (cl_ref[:, :LANES], cl_ref[:, LANES:])

        def step(j, carry, diag):
            pg, dq = list(carry[0]), carry[1]
            ks = pl.multiple_of(j * Q, Q)
            kb = k_ref[pl.ds(ks, Q), :]
            vb = v_ref[pl.ds(ks, Q), :]
            kh = _split_heads(kb, first)
            dk = jnp.zeros((Q, LANES), F32)
            dv = jnp.zeros((Q, LANES), F32)
            for h in range(2):
                z = _dot_nt(qh[h], kb)
                lb, l = _log_sigmoids(z)
                if diag:
                    l = jnp.where(below, l, 0.0)
                hi, lo = _split_bf16(l)
                c = jnp.sum(jnp.where(lane == j, clh[h], 0.0), axis=-1, keepdims=True)
                a = jnp.exp(lb + (_dot(hi, tri) + _dot(lo, tri)) + c)
                if diag:
                    a = jnp.where(below, a, 0.0)
                gg = _dot_nt(doh[h], vb) * a
                hi, lo = _split_bf16(gg)
                prefix = pg[h] + (_dot(hi, trip) + _dot(lo, trip))
                beta = jnp.exp(lb)
                dz = gg * (1.0 - beta) - beta * prefix
                if diag:
                    dz = jnp.where(below, dz, 0.0)
                pg[h] = pg[h] + jnp.sum(gg, axis=-1, keepdims=True)
                dz = dz.astype(BF16)
                dq = dq + _dot(dz, kh[h])
                dk = dk + _dot_tn(dz, qh[h])
                dv = dv + _dot_tn(a.astype(BF16), doh[h])
            dk_ref[pl.ds(ks, Q), :] += dk
            dv_ref[pl.ds(ks, Q), :] += dv
            return (pg[0], pg[1]), dq

        zc = jnp.zeros((Q, 1), F32)
        carry = lax.fori_loop(0, i, lambda j, cr: step(j, cr, False), ((zc, zc), jnp.zeros((Q, LANES), F32)))
        _, dq = step(i, carry, True)
        dq_ref[...] = (dq * (1.0 / math.sqrt(SB_HEAD_DIM))).astype(BF16)

    kv = lambda off: pl.BlockSpec((S, LANES), lambda p, i: (0, off + p))
    blk = pl.BlockSpec((Q, LANES), lambda p, i: (i, p))
    acc = pl.BlockSpec((S, LANES), lambda p, i: (0, p))
    return pl.pallas_call(
        body, name="attn_bwd", grid=(npair, nq),
        out_shape=[jax.ShapeDtypeStruct((S, SB_WIDTH), BF16), jax.ShapeDtypeStruct((S, SB_WIDTH), F32),
                   jax.ShapeDtypeStruct((S, SB_WIDTH), F32), jax.ShapeDtypeStruct((1, SB_WIDTH), F32)],
        in_specs=[blk, kv(npair), kv(2 * npair), blk, blk, pl.BlockSpec((Q, 2 * LANES), lambda p, i: (i, p)),
                  pl.BlockSpec((1, LANES), lambda p, i: (0, p))],
        out_specs=[blk, acc, acc, pl.BlockSpec((1, LANES), lambda p, i: (0, p))],
        compiler_params=_cparams(dimension_semantics=("arbitrary", "arbitrary")),
    )(qkv, qkv, qkv, o, dao, cl, g_attn)


def _conv_bwd(u_conv, dco, conv_w, conv_b, ln_g, ln_b):
    S = u_conv.shape[0]
    tc = min(TOKEN_TILE, S)
    nt = S // tc
    per = tc // CONV_HALO
    groups = CONV_CHUNK // 8

    def body(u_ref, halo_ref, dco_ref, cw_ref, cb_ref, lg_ref, lb_ref, du_ref, dcw_ref, dsm_ref, glu_ext, dyc_ext, sg_buf,
             dcw_acc, dsm_acc):
        i = pl.program_id(0)
        ti = nt - 1 - i

        @pl.when(i == 0)
        def _():
            dyc_ext[tc:, :] = jnp.zeros((CONV_HALO, CONV_CH), F32)
            dcw_acc[...] = jnp.zeros_like(dcw_acc)
            dsm_acc[...] = jnp.zeros_like(dsm_acc)

        @pl.when(i > 0)
        def _():
            dyc_ext[tc:, :] = dyc_ext[0:CONV_HALO, :]

        glu_ext[0:CONV_HALO, :] = jnp.where(ti > 0, _glu(halo_ref[...])[2], 0.0)
        val, sg, glu = _glu(u_ref[...])
        glu_ext[CONV_HALO:, :] = glu
        sg_buf[...] = sg

        dcb = jnp.zeros((8, CONV_CH), F32)
        dlg = jnp.zeros((8, CONV_CH), F32)
        dlb = jnp.zeros((8, CONV_CH), F32)
        fold = lambda t: jnp.sum(t.reshape(groups, 8, CONV_CH), axis=0)
        for r0 in range(0, tc, CONV_CHUNK):
            y = _conv_rows(glu_ext, cw_ref, r0, CONV_CHUNK) + cb_ref[...]
            mu = jnp.mean(y, axis=-1, keepdims=True)
            yc = y - mu
            rstd = lax.rsqrt(jnp.mean(yc * yc, axis=-1, keepdims=True) + EPS)
            yn = yc * rstd
            yl = yn * lg_ref[...] + lb_ref[...]
            s = jax.nn.sigmoid(yl)
            dyl = dco_ref[r0:r0 + CONV_CHUNK, :] * (s * (1.0 + yl * (1.0 - s)))
            dlg = dlg + fold(dyl * yn)
            dlb = dlb + fold(dyl)
            wv = dyl * lg_ref[...]
            dyc = rstd * (wv - jnp.mean(wv, axis=-1, keepdims=True) - yn * jnp.mean(wv * yn, axis=-1, keepdims=True))
            dcb = dcb + fold(dyc)
            dyc_ext[r0:r0 + CONV_CHUNK, :] = dyc
        dsm_acc[0:8, :] += dcb
        dsm_acc[8:16, :] += dlg
        dsm_acc[16:24, :] += dlb

        for r0 in range(0, tc, CONV_CHUNK):
            dyc = dyc_ext[r0:r0 + CONV_CHUNK, :]
            dglu = jnp.zeros((CONV_CHUNK, CONV_CH), F32)
            base = r0 + CONV_HALO - (CONV_WIDTH - 1)
            for w in range(CONV_WIDTH):
                back = r0 + (CONV_WIDTH - 1) - w
                dglu = dglu + cw_ref[w:w + 1, :] * dyc_ext[back:back + CONV_CHUNK, :]
                dcw_acc[8 * w:8 * w + 8, :] += fold(dyc * glu_ext[base + w:base + w + CONV_CHUNK, :])
            sg = sg_buf[r0:r0 + CONV_CHUNK, :]
            v = u_ref[r0:r0 + CONV_CHUNK, :CONV_CH]
            du_ref[r0:r0 + CONV_CHUNK, :CONV_CH] = (dglu * sg).astype(BF16)
            du_ref[r0:r0 + CONV_CHUNK, CONV_CH:] = (dglu * v * sg * (1.0 - sg)).astype(BF16)

        @pl.when(i == nt - 1)
        def _():
            for w in range(CONV_WIDTH):
                dcw_ref[w:w + 1, :] = jnp.sum(dcw_acc[8 * w:8 * w + 8, :], axis=0, keepdims=True)
            dcw_ref[CONV_WIDTH:, :] = jnp.zeros((CONV_HALO - CONV_WIDTH, CONV_CH), F32)
            for k in range(3):
                dsm_ref[k:k + 1, :] = jnp.sum(dsm_acc[8 * k:8 * k + 8, :], axis=0, keepdims=True)
            dsm_ref[3:, :] = jnp.zeros((5, CONV_CH), F32)

    return pl.pallas_call(
        body, name="conv_bwd", grid=(nt,),
        out_shape=[jax.ShapeDtypeStruct((S, 2 * CONV_CH), BF16), jax.ShapeDtypeStruct((CONV_HALO, CONV_CH), F32),
                   jax.ShapeDtypeStruct((8, CONV_CH), F32)],
        in_specs=[pl.BlockSpec((tc, 2 * CONV_CH), lambda i: (nt - 1 - i, 0)),
                  pl.BlockSpec((CONV_HALO, 2 * CONV_CH), lambda i: (jnp.maximum((nt - 1 - i) * per - 1, 0), 0)),
                  pl.BlockSpec((tc, CONV_CH), lambda i: (nt - 1 - i, 0)),
                  _const((CONV_HALO, CONV_CH)), _const((1, CONV_CH)), _const((1, CONV_CH)), _const((1, CONV_CH))],
        out_specs=[pl.BlockSpec((tc, 2 * CONV_CH), lambda i: (nt - 1 - i, 0)), _const((CONV_HALO, CONV_CH)),
                   _const((8, CONV_CH))],
        scratch_shapes=[pltpu.VMEM((tc + CONV_HALO, CONV_CH), F32), pltpu.VMEM((tc + CONV_HALO, CONV_CH), F32),
                        pltpu.VMEM((tc, CONV_CH), F32), pltpu.VMEM((8 * CONV_HALO, CONV_CH), F32),
                        pltpu.VMEM((24, CONV_CH), F32)],
        compiler_params=_cparams(dimension_semantics=("arbitrary",)),
    )(u_conv, u_conv, dco, conv_w, conv_b, ln_g, ln_b)


def _in_proj_bwd(du_conv, dq, dk, dv, w_in, x, g, dh1):
    S = x.shape[0]
    tm = min(TOKEN_TILE, S)
    nconv = 2 * CONV_CH

    def body(duc_ref, dq_ref, dk_ref, dv_ref, w_ref, x_ref, g_ref, dh1_ref, dx_ref, dg_ref):
        @pl.when(pl.program_id(0) == 0)
        def _():
            dg_ref[...] = jnp.zeros_like(dg_ref)

        da = _dot_nt(duc_ref[...], w_ref[:, :nconv])
        for n, ref in enumerate((dq_ref, dk_ref, dv_ref)):
            c0 = nconv + n * SB_WIDTH
            da = da + _dot_nt(ref[...].astype(BF16), w_ref[:, c0:c0 + SB_WIDTH])
        xf = x_ref[...]
        r = _rms_r(xf)
        dx_ref[...] = dh1_ref[...] + _rms_bwd(xf, r, g_ref[...], da)
        dg_ref[...] += jnp.sum(da * xf * r, axis=0, keepdims=True)

    row = lambda n: pl.BlockSpec((tm, n), lambda i: (i, 0))
    return pl.pallas_call(
        body, name="in_proj_bwd", grid=(S // tm,),
        out_shape=[jax.ShapeDtypeStruct((S, D_MODEL), F32), jax.ShapeDtypeStruct((1, D_MODEL), F32)],
        in_specs=[row(nconv), row(SB_WIDTH), row(SB_WIDTH), row(SB_WIDTH), _resident(w_in.shape), row(D_MODEL),
                  _const((1, D_MODEL)), row(D_MODEL)],
        out_specs=[row(D_MODEL), _const((1, D_MODEL))],
        compiler_params=_cparams(dimension_semantics=("arbitrary",)),
    )(du_conv, dq, dk, dv, w_in, x, g, dh1)


def _layer_grads(xs, target, g_pre_mix, w_in_f, conv_w_f, conv_b, conv_ln_g, conv_ln_b, attn_g, w_out_f, g_post_mix,
                 g_pre_ffn, w_gate_f, w_up_f, w_down_f, g_post_ffn):
    a, u_conv, qkv = _in_proj(xs, g_pre_mix, w_in_f)
    conv_out = _conv_fwd(u_conv, conv_w_f, conv_b, conv_ln_g, conv_ln_b)
    o, attn_out, cl = _attn_fwd(qkv, attn_g)
    y, h1, f_in = _out_proj(conv_out, attn_out, w_out_f, xs, g_post_mix, g_pre_ffn)
    gt, up, df, dh2, loss_part, d_g_post_ffn = _ffn_fwd_loss(f_in, w_gate_f, w_up_f, w_down_f, h1, target, g_post_ffn)

    dgt, dup, act, dfin = _ffn_bwd(df, gt, up, w_gate_f, w_up_f, w_down_f)
    d_w_down = _matmul_tn("grad_w_down", act, df, 512)
    d_w_gate = _matmul_tn("grad_w_gate", f_in, dgt, FF_CHUNK)
    d_w_up = _matmul_tn("grad_w_up", f_in, dup, FF_CHUNK)
    dh1, dy, dco, dao, d_g_pre_ffn, d_g_post_mix = _mix_bwd(dfin, h1, y, dh2, g_pre_ffn, g_post_mix, w_out_f)
    d_w_out = jnp.concatenate([_matmul_tn("grad_w_out_conv", conv_out, dy, D_MODEL),
                               _matmul_tn("grad_w_out_attn", attn_out, dy, D_MODEL)], axis=0)
    dq, dk, dv, d_attn_g = _attn_bwd(qkv, o, dao, cl, attn_g)
    du_conv, d_conv_w, d_conv_small = _conv_bwd(u_conv, dco, conv_w_f, conv_b, conv_ln_g, conv_ln_b)
    grad_x, d_g_pre_mix = _in_proj_bwd(du_conv, dq, dk, dv, w_in_f, xs, g_pre_mix, dh1)
    d_w_in = jnp.concatenate([_matmul_tn("grad_w_in_conv", a, du_conv, 2 * CONV_CH),
                              _matmul_tn("grad_w_in_q", a, dq, SB_WIDTH), _matmul_tn("grad_w_in_k", a, dk, SB_WIDTH),
                              _matmul_tn("grad_w_in_v", a, dv, SB_WIDTH)], axis=1)
    return (loss_part, grad_x, d_w_in, d_w_out, d_w_gate, d_w_up, d_w_down, d_conv_w, d_conv_small, d_attn_g,
            d_g_pre_mix, d_g_post_mix, d_g_pre_ffn, d_g_post_ffn)


def _cols_to_blocks(w):
    K, N = w.shape
    return jnp.transpose(w.reshape(K, N_DEV, N // N_DEV), (1, 0, 2))


def _blocks_to_cols(blocks):
    n_dev, K, n = blocks.shape
    return jnp.transpose(blocks, (1, 0, 2)).reshape(K, n_dev * n)


def kernel(x, g_pre_mix, w_in, conv_w, conv_b, conv_ln_g, conv_ln_b, attn_norm_g, w_out, g_post_mix, g_pre_ffn, w_gate, w_up, w_down, g_post_ffn, loss_target, m_g_pre_mix, m_w_in, m_conv_w, m_conv_b, m_conv_ln_g, m_conv_ln_b, m_attn_norm_g, m_w_out, m_g_post_mix, m_g_pre_ffn, m_w_gate, m_w_up, m_w_down, m_g_post_ffn, v_g_pre_mix, v_w_in, v_conv_w, v_conv_b, v_conv_ln_g, v_conv_ln_b, v_attn_norm_g, v_w_out, v_g_post_mix, v_g_pre_ffn, v_w_gate, v_w_up, v_w_down, v_g_post_ffn):
    xs = x[0]
    target = loss_target[0]
    S = xs.shape[0]
    me = 4 * lax.axis_index("x") + 2 * lax.axis_index("y") + lax.axis_index("c")
    cw_shard = conv_w.reshape(CONV_WIDTH, CONV_CH // N_DEV)
    attn_g = attn_norm_g.reshape(1, SB_WIDTH)

    gathered = _all_gather([w_in[0].astype(BF16), w_out[0].astype(BF16), w_gate[0].astype(BF16),
                            w_up[0].astype(BF16), w_down[0].astype(BF16), cw_shard])
    w_in_f = _blocks_to_cols(gathered[0])
    w_out_f = gathered[1].reshape(D_MODEL, D_MODEL)
    w_gate_f = _blocks_to_cols(gathered[2])
    w_up_f = _blocks_to_cols(gathered[3])
    w_down_f = gathered[4].reshape(D_FF, D_MODEL)
    conv_w_f = jnp.pad(_blocks_to_cols(gathered[5]), ((0, CONV_HALO - CONV_WIDTH), (0, 0)))

    (loss_part, grad_x, d_w_in, d_w_out, d_w_gate, d_w_up, d_w_down, d_conv_w, d_conv_small, d_attn_g, d_g_pre_mix,
     d_g_post_mix, d_g_pre_ffn, d_g_post_ffn) = _layer_grads(
        xs, target, g_pre_mix, w_in_f, conv_w_f, conv_b, conv_ln_g, conv_ln_b, attn_g, w_out_f, g_post_mix, g_pre_ffn,
        w_gate_f, w_up_f, w_down_f, g_post_ffn)

    big = {
        "w_in": _reduce_scatter_adamw("rs_adamw_w_in", _cols_to_blocks(d_w_in), w_in[0], m_w_in[0], v_w_in[0]),
        "w_out": _reduce_scatter_adamw("rs_adamw_w_out", d_w_out.reshape(N_DEV, D_MODEL // N_DEV, D_MODEL), w_out[0],
                                       m_w_out[0], v_w_out[0]),
        "w_gate": _reduce_scatter_adamw("rs_adamw_w_gate", _cols_to_blocks(d_w_gate), w_gate[0], m_w_gate[0], v_w_gate[0]),
        "w_up": _reduce_scatter_adamw("rs_adamw_w_up", _cols_to_blocks(d_w_up), w_up[0], m_w_up[0], v_w_up[0]),
        "w_down": _reduce_scatter_adamw("rs_adamw_w_down", d_w_down.reshape(N_DEV, D_FF // N_DEV, D_MODEL), w_down[0],
                                        m_w_down[0], v_w_down[0]),
    }

    two = lambda t: t.reshape(2, CONV_CH)
    small_g = jnp.concatenate([
        d_conv_w,
        d_conv_small[0:3],
        d_attn_g,
        two(d_g_pre_mix), two(d_g_post_mix), two(d_g_pre_ffn), two(d_g_post_ffn),
        jnp.zeros((4, CONV_CH), F32)], axis=0)
    small_g = _all_reduce_small(small_g)
    g_conv_w = lax.dynamic_slice(small_g, (0, me * (CONV_CH // N_DEV)), (CONV_WIDTH, CONV_CH // N_DEV))
    pack = lambda cb, lg, lb, ag, g1, g2, g3, g4: jnp.concatenate(
        [cb, lg, lb, ag.reshape(1, SB_WIDTH), two(g1), two(g2), two(g3), two(g4), jnp.zeros((4, CONV_CH), F32)], axis=0)
    sm_g = small_g[CONV_HALO:]
    sm_delta, sm_m, sm_v = _adamw_small(
        "adamw_small",
        pack(conv_b, conv_ln_g, conv_ln_b, attn_norm_g, g_pre_mix, g_post_mix, g_pre_ffn, g_post_ffn), sm_g,
        pack(m_conv_b, m_conv_ln_g, m_conv_ln_b, m_attn_norm_g, m_g_pre_mix, m_g_post_mix, m_g_pre_ffn, m_g_post_ffn),
        pack(v_conv_b, v_conv_ln_g, v_conv_ln_b, v_attn_norm_g, v_g_pre_mix, v_g_post_mix, v_g_pre_ffn, v_g_post_ffn))
    cw_delta, cw_m, cw_v = _adamw_small("adamw_conv_w", cw_shard, g_conv_w,
                                        m_conv_w.reshape(cw_shard.shape), v_conv_w.reshape(cw_shard.shape))

    def unpack(t):
        return {"conv_b": t[0:1], "conv_ln_g": t[1:2], "conv_ln_b": t[2:3], "attn_norm_g": t[3:4].reshape(1, SB_HEADS, SB_HEAD_DIM),
                "g_pre_mix": t[4:6].reshape(1, D_MODEL), "g_post_mix": t[6:8].reshape(1, D_MODEL),
                "g_pre_ffn": t[8:10].reshape(1, D_MODEL), "g_post_ffn": t[10:12].reshape(1, D_MODEL)}

    names = ["g_pre_mix", "w_in", "conv_w", "conv_b", "conv_ln_g", "conv_ln_b", "attn_norm_g", "w_out", "g_post_mix",
             "g_pre_ffn", "w_gate", "w_up", "w_down", "g_post_ffn"]
    kinds = []
    for idx, small in enumerate((sm_g, sm_delta, sm_m, sm_v)):
        d = unpack(small)
        d["conv_w"] = (g_conv_w, cw_delta, cw_m, cw_v)[idx].reshape(1, CONV_WIDTH, 1, CONV_CH // N_DEV)
        for n in big:
            d[n] = big[n][idx][None]
        kinds.append([d[n] for n in names])

    loss = lax.psum(loss_part[0, 0], ("x", "y", "c"))
    return (loss, grad_x[None], *kinds[0], *kinds[1], *kinds[2], *kinds[3])
```

```python
import functools
import math

import jax
import jax.numpy as jnp
from jax import lax
from jax.experimental import pallas as pl
from jax.experimental.pallas import tpu as pltpu

F32 = jnp.float32
BF16 = jnp.bfloat16
MESH = pl.DeviceIdType.MESH

N_DEV = 8
D_MODEL = 1024
CONV_CH = 512
CONV_WIDTH = 31
SB_HEADS = 8
SB_HEAD_DIM = 64
SB_WIDTH = SB_HEADS * SB_HEAD_DIM
D_FF = 2816
EPS = 1e-6
LOG2E = 1.4426950408889634
MASKED = -1e30
ADAM_LR = 0.001
ADAM_B1 = 0.9
ADAM_B2 = 0.999
ADAM_EPS = 1e-08
ADAM_WD = 0.01
ADAM_STEP = 10

LANES = 128
VMEM_LIMIT = 56 * 1024 * 1024
TOKEN_TILE = 512
FFN_TILE = 256
ATTN_STRIP = 32
ATTN_BLOCK = 256
CONV_HALO = 32
CONV_CHUNK = 64
FF_CHUNK = D_FF // 2


def _cparams(**kw):
    return pltpu.CompilerParams(vmem_limit_bytes=VMEM_LIMIT, **kw)


def _resident(shape):
    return pl.BlockSpec(shape, lambda *_: (0,) * len(shape), pipeline_mode=pl.Buffered(1))


def _const(shape):
    return pl.BlockSpec(shape, lambda *_: (0,) * len(shape))


def _rms_r(xf):
    return lax.rsqrt(jnp.mean(xf * xf, axis=-1, keepdims=True) + EPS)


def _rms_bwd(xf, r, g, dout):
    w = dout * g
    return r * (w - xf * (r * r) * jnp.mean(w * xf, axis=-1, keepdims=True))


def _dot(a, b):
    return jnp.dot(a, b, preferred_element_type=F32)


def _dot_nt(a, b):
    return lax.dot_general(a, b, (((1,), (1,)), ((), ())), preferred_element_type=F32)


def _dot_tn(a, b):
    return lax.dot_general(a, b, (((0,), (0,)), ((), ())), preferred_element_type=F32)


def _peer(x, y, c, k):
    px = 1 - x if (k >> 2) & 1 else x
    py = 1 - y if (k >> 1) & 1 else y
    pc = 1 - c if k & 1 else c
    return (px, py, pc), 4 * px + 2 * py + pc


def _all_gather(shards):
    n = len(shards)

    def body(*refs):
        ins, outs = refs[:n], refs[n:2 * n]
        send_sems, recv_sems, local_sems = refs[2 * n:]
        x, y, c = lax.axis_index("x"), lax.axis_index("y"), lax.axis_index("c")
        me = 4 * x + 2 * y + c
        copies = []
        for a in range(n):
            mine = pltpu.make_async_copy(ins[a], outs[a].at[me], local_sems.at[a])
            mine.start()
            copies.append(mine)
        for k in range(1, N_DEV):
            peer, peer_block = _peer(x, y, c, k)
            for a in range(n):
                s = a * (N_DEV - 1) + k - 1
                pltpu.make_async_remote_copy(
                    src_ref=ins[a], dst_ref=outs[a].at[me], send_sem=send_sems.at[s], recv_sem=recv_sems.at[s],
                    device_id=peer, device_id_type=MESH).start()
        for k in range(1, N_DEV):
            peer, peer_block = _peer(x, y, c, k)
            for a in range(n):
                s = a * (N_DEV - 1) + k - 1
                arrived = pltpu.make_async_remote_copy(
                    src_ref=ins[a], dst_ref=outs[a].at[peer_block], send_sem=send_sems.at[s],
                    recv_sem=recv_sems.at[s], device_id=peer, device_id_type=MESH)
                arrived.wait_send()
                arrived.wait_recv()
        for mine in copies:
            mine.wait()

    any_spec = pl.BlockSpec(memory_space=pl.ANY)
    return pl.pallas_call(
        body, name="all_gather_weights",
        out_shape=[jax.ShapeDtypeStruct((N_DEV,) + s.shape, s.dtype) for s in shards],
        in_specs=[any_spec] * n, out_specs=[any_spec] * n,
        scratch_shapes=[pltpu.SemaphoreType.DMA((n * (N_DEV - 1),)), pltpu.SemaphoreType.DMA((n * (N_DEV - 1),)),
                        pltpu.SemaphoreType.DMA((n,))],
        compiler_params=pltpu.CompilerParams(has_side_effects=True),
    )(*shards)


def _adamw(w, g, m, v):
    m = ADAM_B1 * m + (1.0 - ADAM_B1) * g
    v = ADAM_B2 * v + (1.0 - ADAM_B2) * (g * g)
    m_hat = m / (1.0 - ADAM_B1 ** ADAM_STEP)
    v_hat = v / (1.0 - ADAM_B2 ** ADAM_STEP)
    delta = -ADAM_LR * (m_hat / (jnp.sqrt(v_hat) + ADAM_EPS) + ADAM_WD * w)
    return delta, m, v


def _exchange_and_sum(src_block, recv_ref, send_sems, recv_sems, local_sem):
    x, y, c = lax.axis_index("x"), lax.axis_index("y"), lax.axis_index("c")
    me = 4 * x + 2 * y + c
    mine = pltpu.make_async_copy(src_block(me), recv_ref.at[me], local_sem)
    mine.start()
    for k in range(1, N_DEV):
        peer, peer_block = _peer(x, y, c, k)
        pltpu.make_async_remote_copy(
            src_ref=src_block(peer_block), dst_ref=recv_ref.at[me], send_sem=send_sems.at[k - 1],
            recv_sem=recv_sems.at[k - 1], device_id=peer, device_id_type=MESH).start()
    for k in range(1, N_DEV):
        peer, peer_block = _peer(x, y, c, k)
        arrived = pltpu.make_async_remote_copy(
            src_ref=src_block(peer_block), dst_ref=recv_ref.at[peer_block], send_sem=send_sems.at[k - 1],
            recv_sem=recv_sems.at[k - 1], device_id=peer, device_id_type=MESH)
        arrived.wait_send()
        arrived.wait_recv()
    mine.wait()


def _reduce_scatter_adamw(name, g_blocks, w, m, v):
    _, M, N = g_blocks.shape
    rows = math.gcd(M, 32)

    def body(g_ref, w_ref, m_ref, v_ref, grad_ref, delta_ref, nm_ref, nv_ref, recv_ref, send_sems, recv_sems, local_sem):
        _exchange_and_sum(lambda b: g_ref.at[b], recv_ref, send_sems, recv_sems, local_sem)

        def chunk(i, carry):
            r = pl.ds(pl.multiple_of(i * rows, rows), rows)
            g = recv_ref[0, r, :]
            for b in range(1, N_DEV):
                g = g + recv_ref[b, r, :]
            delta, nm, nv = _adamw(w_ref[r, :], g, m_ref[r, :], v_ref[r, :])
            grad_ref[r, :] = g
            delta_ref[r, :] = delta
            nm_ref[r, :] = nm
            nv_ref[r, :] = nv
            return carry

        lax.fori_loop(0, M // rows, chunk, 0)

    vmem = pl.BlockSpec(memory_space=pltpu.VMEM)
    shard = jax.ShapeDtypeStruct((M, N), F32)
    return pl.pallas_call(
        body, name=name, out_shape=[shard] * 4,
        in_specs=[pl.BlockSpec(memory_space=pl.ANY), vmem, vmem, vmem], out_specs=[vmem] * 4,
        scratch_shapes=[pltpu.VMEM((N_DEV, M, N), F32), pltpu.SemaphoreType.DMA((N_DEV - 1,)),
                        pltpu.SemaphoreType.DMA((N_DEV - 1,)), pltpu.SemaphoreType.DMA(())],
        compiler_params=_cparams(has_side_effects=True),
    )(g_blocks, w, m, v)


def _all_reduce_small(g):
    R, C = g.shape

    def body(g_ref, out_ref, recv_ref, send_sems, recv_sems, local_sem):
        _exchange_and_sum(lambda b: g_ref, recv_ref, send_sems, recv_sems, local_sem)
        total = recv_ref[0]
        for b in range(1, N_DEV):
            total = total + recv_ref[b]
        out_ref[...] = total

    vmem = pl.BlockSpec(memory_space=pltpu.VMEM)
    return pl.pallas_call(
        body, name="all_reduce_small_grads", out_shape=jax.ShapeDtypeStruct((R, C), F32),
        in_specs=[vmem], out_specs=vmem,
        scratch_shapes=[pltpu.VMEM((N_DEV, R, C), F32), pltpu.SemaphoreType.DMA((N_DEV - 1,)),
                        pltpu.SemaphoreType.DMA((N_DEV - 1,)), pltpu.SemaphoreType.DMA(())],
        compiler_params=_cparams(has_side_effects=True),
    )(g)


def _adamw_small(name, w, g, m, v):
    def body(w_ref, g_ref, m_ref, v_ref, delta_ref, nm_ref, nv_ref):
        delta, nm, nv = _adamw(w_ref[...], g_ref[...], m_ref[...], v_ref[...])
        delta_ref[...] = delta
        nm_ref[...] = nm
        nv_ref[...] = nv

    vmem = pl.BlockSpec(memory_space=pltpu.VMEM)
    return pl.pallas_call(body, name=name, out_shape=[jax.ShapeDtypeStruct(w.shape, F32)] * 3,
                          in_specs=[vmem] * 4, out_specs=[vmem] * 3)(w, g, m, v)


def _in_proj(x, g, w_in):
    S = x.shape[0]
    tm = min(TOKEN_TILE, S)
    nconv = 2 * CONV_CH

    def body(x_ref, g_ref, w_ref, a_ref, uc_ref, qkv_ref):
        xf = x_ref[...]
        a = (xf * _rms_r(xf) * g_ref[...]).astype(BF16)
        a_ref[...] = a
        uc_ref[...] = _dot(a, w_ref[:, :nconv])
        qkv_ref[:, :SB_WIDTH] = (_dot(a, w_ref[:, nconv:nconv + SB_WIDTH]) * (1.0 / math.sqrt(SB_HEAD_DIM))).astype(BF16)
        qkv_ref[:, SB_WIDTH:] = _dot(a, w_ref[:, nconv + SB_WIDTH:]).astype(BF16)

    row = lambda n: pl.BlockSpec((tm, n), lambda i: (i, 0))
    return pl.pallas_call(
        body, name="in_proj", grid=(S // tm,),
        out_shape=[jax.ShapeDtypeStruct((S, D_MODEL), BF16), jax.ShapeDtypeStruct((S, nconv), F32),
                   jax.ShapeDtypeStruct((S, 3 * SB_WIDTH), BF16)],
        in_specs=[row(D_MODEL), _const((1, D_MODEL)), _resident(w_in.shape)],
        out_specs=[row(D_MODEL), row(nconv), row(3 * SB_WIDTH)],
        compiler_params=_cparams(),
    )(x, g, w_in)


def _glu(u):
    val, gate = u[:, :CONV_CH], u[:, CONV_CH:]
    sg = jax.nn.sigmoid(gate)
    return val, sg, val * sg


def _conv_rows(glu_ext, cw_ref, r0, rows):
    base = r0 + CONV_HALO - (CONV_WIDTH - 1)
    acc = cw_ref[0:1, :] * glu_ext[base:base + rows, :]
    for w in range(1, CONV_WIDTH):
        acc = acc + cw_ref[w:w + 1, :] * glu_ext[base + w:base + w + rows, :]
    return acc


def _conv_fwd(u_conv, conv_w, conv_b, ln_g, ln_b):
    S = u_conv.shape[0]
    tc = min(TOKEN_TILE, S)

    def body(u_ref, cw_ref, cb_ref, lg_ref, lb_ref, out_ref, glu_ext):
        i = pl.program_id(0)

        @pl.when(i == 0)
        def _():
            glu_ext[0:CONV_HALO, :] = jnp.zeros((CONV_HALO, CONV_CH), F32)

        @pl.when(i > 0)
        def _():
            glu_ext[0:CONV_HALO, :] = glu_ext[tc:tc + CONV_HALO, :]

        glu_ext[CONV_HALO:, :] = _glu(u_ref[...])[2]
        for r0 in range(0, tc, CONV_CHUNK):
            y = _conv_rows(glu_ext, cw_ref, r0, CONV_CHUNK) + cb_ref[...]
            mu = jnp.mean(y, axis=-1, keepdims=True)
            yc = y - mu
            yn = yc * lax.rsqrt(jnp.mean(yc * yc, axis=-1, keepdims=True) + EPS)
            yl = yn * lg_ref[...] + lb_ref[...]
            out_ref[r0:r0 + CONV_CHUNK, :] = (yl * jax.nn.sigmoid(yl)).astype(BF16)

    return pl.pallas_call(
        body, name="conv_fwd", grid=(S // tc,),
        out_shape=jax.ShapeDtypeStruct((S, CONV_CH), BF16),
        in_specs=[pl.BlockSpec((tc, 2 * CONV_CH), lambda i: (i, 0)), _const((CONV_HALO, CONV_CH)),
                  _const((1, CONV_CH)), _const((1, CONV_CH)), _const((1, CONV_CH))],
        out_specs=pl.BlockSpec((tc, CONV_CH), lambda i: (i, 0)),
        scratch_shapes=[pltpu.VMEM((tc + CONV_HALO, CONV_CH), F32)],
        compiler_params=_cparams(dimension_semantics=("arbitrary",)),
    )(u_conv, conv_w, conv_b, ln_g, ln_b)


def _head_masks():
    lane = lax.broadcasted_iota(jnp.int32, (1, LANES), 1)
    return lane < SB_HEAD_DIM


def _split_heads(t, first):
    z = jnp.zeros_like(t)
    return jnp.where(first, t, z), jnp.where(first, z, t)


def _head_sum(t, first):
    a = jnp.sum(jnp.where(first, t, 0.0), axis=-1, keepdims=True)
    b = jnp.sum(jnp.where(first, 0.0, t), axis=-1, keepdims=True)
    return a, b


def _log2_sigmoids(z):
    z2 = z * LOG2E
    lb = jnp.minimum(z2, 0.0) - jnp.log2(1.0 + jnp.exp2(-jnp.abs(z2)))
    return lb, lb - z2


def _split_bf16(t):
    hi = t.astype(BF16)
    return hi, (t - hi.astype(F32)).astype(BF16)


def _block_masks(n):
    row = lax.broadcasted_iota(jnp.int32, (n, n), 0)
    col = lax.broadcasted_iota(jnp.int32, (n, n), 1)
    return row > col, row < col


def _attn_fwd(qkv, g_attn):
    S = qkv.shape[0]
    Q = min(ATTN_BLOCK, S)
    nq = S // Q
    assert nq <= LANES
    npair = SB_WIDTH // LANES

    def body(q_ref, k_ref, v_ref, g_ref, o_ref, ao_ref, cl_ref, z_buf, hi_buf, lo_buf, z2_buf, a_buf, c_buf):
        i = pl.program_id(1)
        first = _head_masks()
        qh = _split_heads(q_ref[...], first)
        row = lax.broadcasted_iota(jnp.int32, (Q, Q), 0)
        col = lax.broadcasted_iota(jnp.int32, (Q, Q), 1)
        tri = (row >= col).astype(BF16)
        lane = lax.broadcasted_iota(jnp.int32, (1, LANES), 1)
        heads = range(2)
        strips = [slice(r0, r0 + ATTN_STRIP) for r0 in range(0, Q, ATTN_STRIP)]
        rows = lambda j: pl.ds(pl.multiple_of(j * Q, Q), Q)

        def scores(j, slot):
            kb = k_ref[rows(j), :]
            for h in heads:
                z_buf[slot, h] = _dot_nt(qh[h], kb)

        def logs(slot, diag):
            for h in heads:
                for r in strips:
                    z2 = z_buf[slot, h, r, :] * LOG2E
                    l = (jnp.minimum(z2, 0.0) - jnp.log2(1.0 + jnp.exp2(-jnp.abs(z2)))) - z2
                    if diag:
                        keep = col[r] < row[r]
                        l = jnp.where(keep, l, 0.0)
                        z2 = jnp.where(keep, z2, MASKED)
                    hi, lo = _split_bf16(l)
                    hi_buf[slot, h, r, :] = hi
                    lo_buf[slot, h, r, :] = lo
                    z2_buf[slot, h, r, :] = z2

        def sums(slot):
            return tuple(_dot(hi_buf[slot, h], tri) + _dot(lo_buf[slot, h], tri) for h in heads)

        def weights(j, slot, sm):
            for h in heads:
                before = c_buf[h]
                for r in strips:
                    a_buf[slot, h, r, :] = jnp.exp2(z2_buf[slot, h, r, :] + sm[h][r] + jnp.tile(before[r], (1, Q // LANES))).astype(BF16)
                hl = slice(h * LANES, (h + 1) * LANES)
                cl_ref[:, hl] = jnp.where(lane == j, before, cl_ref[:, hl])
                c_buf[h] = before + jnp.broadcast_to(sm[h][:, 0:1], (Q, LANES))

        def values(slot, j):
            vh = _split_heads(v_ref[rows(j), :], first)
            return _dot(a_buf[slot, 0], vh[0]) + _dot(a_buf[slot, 1], vh[1])

        def iteration(m, p):
            b = i - 1 - m
            scores(jnp.maximum(b - 1, 0), 1 - p)
            sm = sums(1 - p)
            o_ref[...] += values(p, jnp.minimum(b + 2, i))
            logs(p, False)
            weights(b + 1, 1 - p, sm)

        def two_iterations(t, carry):
            iteration(2 * t, 0)
            iteration(2 * t + 1, 1)
            return carry

        o_ref[...] = jnp.zeros_like(o_ref)
        cl_ref[...] = jnp.zeros_like(cl_ref)
        c_buf[...] = jnp.zeros_like(c_buf)
        a_buf[0] = jnp.zeros((2, Q, Q), BF16)
        scores(i, 1)
        logs(1, True)

        @pl.when(i > 0)
        def _():
            scores(i - 1, 0)

        lax.fori_loop(0, i // 2, two_iterations, 0)

        @pl.when(i % 2 == 1)
        def _():
            iteration(i - 1, 0)

        last = (i + 1) % 2
        weights(0, last, sums(last))
        o = o_ref[...] + values(1 - last, jnp.minimum(1, i)) + values(last, 0)
        sa, sb = _head_sum(o * o, first)
        r = jnp.where(first, lax.rsqrt(sa * (1.0 / SB_HEAD_DIM) + EPS), lax.rsqrt(sb * (1.0 / SB_HEAD_DIM) + EPS))
        o_ref[...] = o
        ao_ref[...] = (o * r * g_ref[...]).astype(BF16)

    kv = lambda off: pl.BlockSpec((S, LANES), lambda p, i: (0, off + p))
    return pl.pallas_call(
        body, name="attn_fwd", grid=(npair, nq),
        out_shape=[jax.ShapeDtypeStruct((S, SB_WIDTH), F32), jax.ShapeDtypeStruct((S, SB_WIDTH), BF16),
                   jax.ShapeDtypeStruct((S, 2 * SB_WIDTH), F32)],
        in_specs=[pl.BlockSpec((Q, LANES), lambda p, i: (i, p)), kv(npair), kv(2 * npair),
                  pl.BlockSpec((1, LANES), lambda p, i: (0, p))],
        out_specs=[pl.BlockSpec((Q, LANES), lambda p, i: (i, p)), pl.BlockSpec((Q, LANES), lambda p, i: (i, p)),
                   pl.BlockSpec((Q, 2 * LANES), lambda p, i: (i, p))],
        scratch_shapes=[pltpu.VMEM((2, 2, Q, Q), F32), pltpu.VMEM((2, 2, Q, Q), BF16), pltpu.VMEM((2, 2, Q, Q), BF16),
                        pltpu.VMEM((2, 2, Q, Q), F32), pltpu.VMEM((2, 2, Q, Q), BF16), pltpu.VMEM((2, Q, LANES), F32)],
        compiler_params=_cparams(dimension_semantics=("arbitrary", "arbitrary")),
    )(qkv, qkv, qkv, g_attn)


def _out_proj(conv_out, attn_out, w_out, x, g_post_mix, g_pre_ffn):
    S = x.shape[0]
    tm = min(TOKEN_TILE, S)

    def body(co_ref, ao_ref, w_ref, x_ref, g1_ref, g2_ref, y_ref, h1_ref, fin_ref):
        y = _dot(co_ref[...], w_ref[:CONV_CH, :]) + _dot(ao_ref[...], w_ref[CONV_CH:, :])
        h1 = x_ref[...] + y * _rms_r(y) * g1_ref[...]
        y_ref[...] = y
        h1_ref[...] = h1
        fin_ref[...] = (h1 * _rms_r(h1) * g2_ref[...]).astype(BF16)

    row = lambda n: pl.BlockSpec((tm, n), lambda i: (i, 0))
    return pl.pallas_call(
        body, name="out_proj", grid=(S // tm,),
        out_shape=[jax.ShapeDtypeStruct((S, D_MODEL), F32), jax.ShapeDtypeStruct((S, D_MODEL), F32),
                   jax.ShapeDtypeStruct((S, D_MODEL), BF16)],
        in_specs=[row(CONV_CH), row(SB_WIDTH), _resident(w_out.shape), row(D_MODEL), _const((1, D_MODEL)),
                  _const((1, D_MODEL))],
        out_specs=[row(D_MODEL)] * 3,
        compiler_params=_cparams(),
    )(conv_out, attn_out, w_out, x, g_post_mix, g_pre_ffn)


def _ffn_fwd_loss(f_in, w_gate, w_up, w_down, h1, target, g_post_ffn):
    S = f_in.shape[0]
    tm = min(FFN_TILE, S)
    nt = S // tm

    def body(fin_ref, wg_ref, wu_ref, wd_ref, h1_ref, t_ref, g_ref, gt_ref, up_ref, df_ref, dh2_ref, loss_ref, dg_ref,
             sq_acc):
        i = pl.program_id(0)

        @pl.when(i == 0)
        def _():
            sq_acc[...] = jnp.zeros_like(sq_acc)
            dg_ref[...] = jnp.zeros_like(dg_ref)

        fin = fin_ref[...]
        f = jnp.zeros((tm, D_MODEL), F32)
        for c0 in range(0, D_FF, FF_CHUNK):
            cols = slice(c0, c0 + FF_CHUNK)
            gt = _dot(fin, wg_ref[:, cols])
            up = _dot(fin, wu_ref[:, cols])
            gt_ref[:, cols] = gt.astype(BF16)
            up_ref[:, cols] = up.astype(BF16)
            f = f + _dot((gt * jax.nn.sigmoid(gt) * up).astype(BF16), wd_ref[cols, :])
        r = _rms_r(f)
        g = g_ref[...]
        diff = h1_ref[...] + f * r * g - t_ref[...]
        sq_acc[...] += jnp.sum(diff * diff, axis=0, keepdims=True)
        dh2 = diff * (1.0 / D_MODEL)
        dh2_ref[...] = dh2
        dg_ref[...] += jnp.sum(dh2 * f * r, axis=0, keepdims=True)
        df_ref[...] = _rms_bwd(f, r, g, dh2).astype(BF16)

        @pl.when(i == nt - 1)
        def _():
            loss_ref[...] = jnp.broadcast_to((0.5 / D_MODEL) * jnp.sum(sq_acc[...], axis=-1, keepdims=True), (1, LANES))

    row = lambda n: pl.BlockSpec((tm, n), lambda i: (i, 0))
    return pl.pallas_call(
        body, name="ffn_fwd_loss", grid=(nt,),
        out_shape=[jax.ShapeDtypeStruct((S, D_FF), BF16), jax.ShapeDtypeStruct((S, D_FF), BF16),
                   jax.ShapeDtypeStruct((S, D_MODEL), BF16), jax.ShapeDtypeStruct((S, D_MODEL), F32),
                   jax.ShapeDtypeStruct((1, LANES), F32), jax.ShapeDtypeStruct((1, D_MODEL), F32)],
        in_specs=[row(D_MODEL), _resident(w_gate.shape), _resident(w_up.shape), _resident(w_down.shape),
                  row(D_MODEL), row(D_MODEL), _const((1, D_MODEL))],
        out_specs=[row(D_FF), row(D_FF), row(D_MODEL), row(D_MODEL), _const((1, LANES)), _const((1, D_MODEL))],
        scratch_shapes=[pltpu.VMEM((1, D_MODEL), F32)],
        compiler_params=_cparams(dimension_semantics=("arbitrary",)),
    )(f_in, w_gate, w_up, w_down, h1, target, g_post_ffn)


def _ffn_bwd(df, gt, up, w_gate, w_up, w_down):
    S = df.shape[0]
    tm = min(FFN_TILE, S)

    def body(df_ref, gt_ref, up_ref, wg_ref, wu_ref, wd_ref, dgt_ref, dup_ref, act_ref, dfin_ref):
        df = df_ref[...]
        dfin = jnp.zeros((tm, D_MODEL), F32)
        for c0 in range(0, D_FF, FF_CHUNK):
            cols = slice(c0, c0 + FF_CHUNK)
            dact = _dot_nt(df, wd_ref[cols, :])
            gt = gt_ref[:, cols].astype(F32)
            up = up_ref[:, cols].astype(F32)
            s = jax.nn.sigmoid(gt)
            silu = gt * s
            dgt = (dact * up * (s * (1.0 + gt * (1.0 - s)))).astype(BF16)
            dup = (dact * silu).astype(BF16)
            act_ref[:, cols] = (silu * up).astype(BF16)
            dgt_ref[:, cols] = dgt
            dup_ref[:, cols] = dup
            dfin = dfin + _dot_nt(dgt, wg_ref[:, cols]) + _dot_nt(dup, wu_ref[:, cols])
        dfin_ref[...] = dfin

    row = lambda n: pl.BlockSpec((tm, n), lambda i: (i, 0))
    return pl.pallas_call(
        body, name="ffn_bwd", grid=(S // tm,),
        out_shape=[jax.ShapeDtypeStruct((S, D_FF), BF16)] * 3 + [jax.ShapeDtypeStruct((S, D_MODEL), F32)],
        in_specs=[row(D_MODEL), row(D_FF), row(D_FF), _resident(w_gate.shape), _resident(w_up.shape),
                  _resident(w_down.shape)],
        out_specs=[row(D_FF)] * 3 + [row(D_MODEL)],
        compiler_params=_cparams(),
    )(df, gt, up, w_gate, w_up, w_down)


def _matmul_tn(name, x, y, tn):
    S, K = x.shape
    N = y.shape[1]
    ts = min(TOKEN_TILE, S)

    def body(x_ref, y_ref, o_ref):
        @pl.when(pl.program_id(1) == 0)
        def _():
            o_ref[...] = jnp.zeros_like(o_ref)

        o_ref[...] += _dot_tn(x_ref[...].astype(BF16), y_ref[...].astype(BF16))

    return pl.pallas_call(
        body, name=name, grid=(N // tn, S // ts),
        out_shape=jax.ShapeDtypeStruct((K, N), F32),
        in_specs=[pl.BlockSpec((ts, K), lambda n, s: (s, 0)), pl.BlockSpec((ts, tn), lambda n, s: (s, n))],
        out_specs=pl.BlockSpec((K, tn), lambda n, s: (0, n)),
        compiler_params=_cparams(dimension_semantics=("arbitrary", "arbitrary")),
    )(x, y)


def _mix_bwd(dfin, h1, y, dh2, g_pre_ffn, g_post_mix, w_out):
    S = dfin.shape[0]
    tm = min(TOKEN_TILE, S)

    def body(dfin_ref, h1_ref, y_ref, dh2_ref, g2_ref, g1_ref, w_ref, dh1_ref, dy_ref, dco_ref, dao_ref, dg2_ref, dg1_ref):
        @pl.when(pl.program_id(0) == 0)
        def _():
            dg2_ref[...] = jnp.zeros_like(dg2_ref)
            dg1_ref[...] = jnp.zeros_like(dg1_ref)

        h1, dfin = h1_ref[...], dfin_ref[...]
        r2 = _rms_r(h1)
        dh1 = dh2_ref[...] + _rms_bwd(h1, r2, g2_ref[...], dfin)
        dg2_ref[...] += jnp.sum(dfin * h1 * r2, axis=0, keepdims=True)
        y = y_ref[...]
        r1 = _rms_r(y)
        dy = _rms_bwd(y, r1, g1_ref[...], dh1).astype(BF16)
        dg1_ref[...] += jnp.sum(dh1 * y * r1, axis=0, keepdims=True)
        dh1_ref[...] = dh1
        dy_ref[...] = dy
        dco_ref[...] = _dot_nt(dy, w_ref[:CONV_CH, :])
        dao_ref[...] = _dot_nt(dy, w_ref[CONV_CH:, :])

    row = lambda n: pl.BlockSpec((tm, n), lambda i: (i, 0))
    return pl.pallas_call(
        body, name="mix_bwd", grid=(S // tm,),
        out_shape=[jax.ShapeDtypeStruct((S, D_MODEL), F32), jax.ShapeDtypeStruct((S, D_MODEL), BF16),
                   jax.ShapeDtypeStruct((S, CONV_CH), F32), jax.ShapeDtypeStruct((S, SB_WIDTH), F32),
                   jax.ShapeDtypeStruct((1, D_MODEL), F32), jax.ShapeDtypeStruct((1, D_MODEL), F32)],
        in_specs=[row(D_MODEL)] * 4 + [_const((1, D_MODEL)), _const((1, D_MODEL)), _resident(w_out.shape)],
        out_specs=[row(D_MODEL), row(D_MODEL), row(CONV_CH), row(SB_WIDTH), _const((1, D_MODEL)), _const((1, D_MODEL))],
        compiler_params=_cparams(dimension_semantics=("arbitrary",)),
    )(dfin, h1, y, dh2, g_pre_ffn, g_post_mix, w_out)


def _attn_bwd(qkv, o, dao, cl, g_attn):
    S = qkv.shape[0]
    Q = min(ATTN_BLOCK, S)
    nq = S // Q
    npair = SB_WIDTH // LANES
    inv_dh = 1.0 / SB_HEAD_DIM

    def body(q_ref, k_ref, v_ref, o_ref, dao_ref, cl_ref, g_ref, dq_ref, dk_ref, dv_ref, dg_ref):
        i = pl.program_id(1)

        @pl.when(i == 0)
        def _():
            dk_ref[...] = jnp.zeros_like(dk_ref)
            dv_ref[...] = jnp.zeros_like(dv_ref)
            dg_ref[...] = jnp.zeros_like(dg_ref)

        first = _head_masks()
        lane = lax.broadcasted_iota(jnp.int32, (1, LANES), 1)
        o, dao, g = o_ref[...], dao_ref[...], g_ref[...]
        sa, sb = _head_sum(o * o, first)
        r = jnp.where(first, lax.rsqrt(sa * inv_dh + EPS), lax.rsqrt(sb * inv_dh + EPS))
        w = dao * g
        wa, wb = _head_sum(w * o, first)
        do = r * (w - o * (r * r) * (jnp.where(first, wa, wb) * inv_dh))
        dg_ref[...] += jnp.sum(dao * o * r, axis=0, keepdims=True)
        doh = _split_heads(do.astype(BF16), first)

        qh = _split_heads(q_ref[...], first)
        below, above = _block_masks(Q)
        tri = below.astype(BF16)
        trip = above.astype(BF16)
        clh = (cl_ref[:, :LANES], cl_ref[:, LANES:])

        heads = range(2)

        def products(j):
            ks = pl.multiple_of(j * Q, Q)
            kb = k_ref[pl.ds(ks, Q), :]
            vb = v_ref[pl.ds(ks, Q), :]
            return tuple(_dot_nt(qh[h], kb) for h in heads) + tuple(_dot_nt(doh[h], vb) for h in heads)

        def step(j, zd, carry, diag):
            pg, dq = list(carry[0]), carry[1]
            ks = pl.multiple_of(j * Q, Q)
            kh = _split_heads(k_ref[pl.ds(ks, Q), :], first)
            ll = [_log2_sigmoids(zd[h]) for h in heads]
            lb = [ll[h][0] for h in heads]
            l = [jnp.where(below, ll[h][1], 0.0) if diag else ll[h][1] for h in heads]
            hl = [_split_bf16(l[h]) for h in heads]
            c = [jnp.sum(jnp.where(lane == j, clh[h], 0.0), axis=-1, keepdims=True) for h in heads]
            bl = [_dot(hl[h][0], tri) + _dot(hl[h][1], tri) for h in heads]
            a = [jnp.exp2(lb[h] + bl[h] + c[h]) for h in heads]
            if diag:
                a = [jnp.where(below, a[h], 0.0) for h in heads]
            gg = [zd[2 + h] * a[h] for h in heads]
            gl = [_split_bf16(gg[h]) for h in heads]
            prefix = [pg[h] + (_dot(gl[h][0], trip) + _dot(gl[h][1], trip)) for h in heads]
            dz = [gg[h] - jnp.exp2(lb[h]) * (gg[h] + prefix[h]) for h in heads]
            if diag:
                dz = [jnp.where(below, dz[h], 0.0) for h in heads]
            dz = [dz[h].astype(BF16) for h in heads]
            ab = [a[h].astype(BF16) for h in heads]
            for h in heads:
                pg[h] = pg[h] + jnp.sum(gg[h], axis=-1, keepdims=True)
            dq = dq + (_dot(dz[0], kh[0]) + _dot(dz[1], kh[1]))
            dk_ref[pl.ds(ks, Q), :] += _dot_tn(dz[0], qh[0]) + _dot_tn(dz[1], qh[1])
            dv_ref[pl.ds(ks, Q), :] += _dot_tn(ab[0], doh[0]) + _dot_tn(ab[1], doh[1])
            return (pg[0], pg[1]), dq

        def pipelined(j, state):
            nxt = products(j + 1)
            return nxt, step(j, state[0], state[1], False)

        zc = jnp.zeros((Q, 1), F32)
        zd, carry = lax.fori_loop(0, i, pipelined, (products(0), ((zc, zc), jnp.zeros((Q, LANES), F32))))
        _, dq = step(i, zd, carry, True)
        dq_ref[...] = (dq * (1.0 / math.sqrt(SB_HEAD_DIM))).astype(BF16)

    kv = lambda off: pl.BlockSpec((S, LANES), lambda p, i: (0, off + p))
    blk = pl.BlockSpec((Q, LANES), lambda p, i: (i, p))
    acc = pl.BlockSpec((S, LANES), lambda p, i: (0, p))
    return pl.pallas_call(
        body, name="attn_bwd", grid=(npair, nq),
        out_shape=[jax.ShapeDtypeStruct((S, SB_WIDTH), BF16), jax.ShapeDtypeStruct((S, SB_WIDTH), F32),
                   jax.ShapeDtypeStruct((S, SB_WIDTH), F32), jax.ShapeDtypeStruct((1, SB_WIDTH), F32)],
        in_specs=[blk, kv(npair), kv(2 * npair), blk, blk, pl.BlockSpec((Q, 2 * LANES), lambda p, i: (i, p)),
                  pl.BlockSpec((1, LANES), lambda p, i: (0, p))],
        out_specs=[blk, acc, acc, pl.BlockSpec((1, LANES), lambda p, i: (0, p))],
        compiler_params=_cparams(dimension_semantics=("arbitrary", "arbitrary")),
    )(qkv, qkv, qkv, o, dao, cl, g_attn)


def _conv_bwd(u_conv, dco, conv_w, conv_b, ln_g, ln_b):
    S = u_conv.shape[0]
    tc = min(TOKEN_TILE, S)
    nt = S // tc
    per = tc // CONV_HALO
    groups = CONV_CHUNK // 8

    def body(u_ref, halo_ref, dco_ref, cw_ref, cb_ref, lg_ref, lb_ref, du_ref, dcw_ref, dsm_ref, glu_ext, dyc_ext, sg_buf,
             dcw_acc, dsm_acc):
        i = pl.program_id(0)
        ti = nt - 1 - i

        @pl.when(i == 0)
        def _():
            dyc_ext[tc:, :] = jnp.zeros((CONV_HALO, CONV_CH), F32)
            dcw_acc[...] = jnp.zeros_like(dcw_acc)
            dsm_acc[...] = jnp.zeros_like(dsm_acc)

        @pl.when(i > 0)
        def _():
            dyc_ext[tc:, :] = dyc_ext[0:CONV_HALO, :]

        glu_ext[0:CONV_HALO, :] = jnp.where(ti > 0, _glu(halo_ref[...])[2], 0.0)
        val, sg, glu = _glu(u_ref[...])
        glu_ext[CONV_HALO:, :] = glu
        sg_buf[...] = sg

        dcb = jnp.zeros((8, CONV_CH), F32)
        dlg = jnp.zeros((8, CONV_CH), F32)
        dlb = jnp.zeros((8, CONV_CH), F32)
        fold = lambda t: jnp.sum(t.reshape(groups, 8, CONV_CH), axis=0)
        for r0 in range(0, tc, CONV_CHUNK):
            y = _conv_rows(glu_ext, cw_ref, r0, CONV_CHUNK) + cb_ref[...]
            mu = jnp.mean(y, axis=-1, keepdims=True)
            yc = y - mu
            rstd = lax.rsqrt(jnp.mean(yc * yc, axis=-1, keepdims=True) + EPS)
            yn = yc * rstd
            yl = yn * lg_ref[...] + lb_ref[...]
            s = jax.nn.sigmoid(yl)
            dyl = dco_ref[r0:r0 + CONV_CHUNK, :] * (s * (1.0 + yl * (1.0 - s)))
            dlg = dlg + fold(dyl * yn)
            dlb = dlb + fold(dyl)
            wv = dyl * lg_ref[...]
            dyc = rstd * (wv - jnp.mean(wv, axis=-1, keepdims=True) - yn * jnp.mean(wv * yn, axis=-1, keepdims=True))
            dcb = dcb + fold(dyc)
            dyc_ext[r0:r0 + CONV_CHUNK, :] = dyc
        dsm_acc[0:8, :] += dcb
        dsm_acc[8:16, :] += dlg
        dsm_acc[16:24, :] += dlb

        for r0 in range(0, tc, CONV_CHUNK):
            dyc = dyc_ext[r0:r0 + CONV_CHUNK, :]
            dglu = jnp.zeros((CONV_CHUNK, CONV_CH), F32)
            base = r0 + CONV_HALO - (CONV_WIDTH - 1)
            for w in range(CONV_WIDTH):
                back = r0 + (CONV_WIDTH - 1) - w
                dglu = dglu + cw_ref[w:w + 1, :] * dyc_ext[back:back + CONV_CHUNK, :]
                dcw_acc[8 * w:8 * w + 8, :] += fold(dyc * glu_ext[base + w:base + w + CONV_CHUNK, :])
            sg = sg_buf[r0:r0 + CONV_CHUNK, :]
            v = u_ref[r0:r0 + CONV_CHUNK, :CONV_CH]
            du_ref[r0:r0 + CONV_CHUNK, :CONV_CH] = (dglu * sg).astype(BF16)
            du_ref[r0:r0 + CONV_CHUNK, CONV_CH:] = (dglu * v * sg * (1.0 - sg)).astype(BF16)

        @pl.when(i == nt - 1)
        def _():
            for w in range(CONV_WIDTH):
                dcw_ref[w:w + 1, :] = jnp.sum(dcw_acc[8 * w:8 * w + 8, :], axis=0, keepdims=True)
            dcw_ref[CONV_WIDTH:, :] = jnp.zeros((CONV_HALO - CONV_WIDTH, CONV_CH), F32)
            for k in range(3):
                dsm_ref[k:k + 1, :] = jnp.sum(dsm_acc[8 * k:8 * k + 8, :], axis=0, keepdims=True)
            dsm_ref[3:, :] = jnp.zeros((5, CONV_CH), F32)

    return pl.pallas_call(
        body, name="conv_bwd", grid=(nt,),
        out_shape=[jax.ShapeDtypeStruct((S, 2 * CONV_CH), BF16), jax.ShapeDtypeStruct((CONV_HALO, CONV_CH), F32),
                   jax.ShapeDtypeStruct((8, CONV_CH), F32)],
        in_specs=[pl.BlockSpec((tc, 2 * CONV_CH), lambda i: (nt - 1 - i, 0)),
                  pl.BlockSpec((CONV_HALO, 2 * CONV_CH), lambda i: (jnp.maximum((nt - 1 - i) * per - 1, 0), 0)),
                  pl.BlockSpec((tc, CONV_CH), lambda i: (nt - 1 - i, 0)),
                  _const((CONV_HALO, CONV_CH)), _const((1, CONV_CH)), _const((1, CONV_CH)), _const((1, CONV_CH))],
        out_specs=[pl.BlockSpec((tc, 2 * CONV_CH), lambda i: (nt - 1 - i, 0)), _const((CONV_HALO, CONV_CH)),
                   _const((8, CONV_CH))],
        scratch_shapes=[pltpu.VMEM((tc + CONV_HALO, CONV_CH), F32), pltpu.VMEM((tc + CONV_HALO, CONV_CH), F32),
                        pltpu.VMEM((tc, CONV_CH), F32), pltpu.VMEM((8 * CONV_HALO, CONV_CH), F32),
                        pltpu.VMEM((24, CONV_CH), F32)],
        compiler_params=_cparams(dimension_semantics=("arbitrary",)),
    )(u_conv, u_conv, dco, conv_w, conv_b, ln_g, ln_b)


def _in_proj_bwd(du_conv, dq, dk, dv, w_in, x, g, dh1):
    S = x.shape[0]
    tm = min(TOKEN_TILE, S)
    nconv = 2 * CONV_CH

    def body(duc_ref, dq_ref, dk_ref, dv_ref, w_ref, x_ref, g_ref, dh1_ref, dx_ref, dg_ref):
        @pl.when(pl.program_id(0) == 0)
        def _():
            dg_ref[...] = jnp.zeros_like(dg_ref)

        da = _dot_nt(duc_ref[...], w_ref[:, :nconv])
        for n, ref in enumerate((dq_ref, dk_ref, dv_ref)):
            c0 = nconv + n * SB_WIDTH
            da = da + _dot_nt(ref[...].astype(BF16), w_ref[:, c0:c0 + SB_WIDTH])
        xf = x_ref[...]
        r = _rms_r(xf)
        dx_ref[...] = dh1_ref[...] + _rms_bwd(xf, r, g_ref[...], da)
        dg_ref[...] += jnp.sum(da * xf * r, axis=0, keepdims=True)

    row = lambda n: pl.BlockSpec((tm, n), lambda i: (i, 0))
    return pl.pallas_call(
        body, name="in_proj_bwd", grid=(S // tm,),
        out_shape=[jax.ShapeDtypeStruct((S, D_MODEL), F32), jax.ShapeDtypeStruct((1, D_MODEL), F32)],
        in_specs=[row(nconv), row(SB_WIDTH), row(SB_WIDTH), row(SB_WIDTH), _resident(w_in.shape), row(D_MODEL),
                  _const((1, D_MODEL)), row(D_MODEL)],
        out_specs=[row(D_MODEL), _const((1, D_MODEL))],
        compiler_params=_cparams(dimension_semantics=("arbitrary",)),
    )(du_conv, dq, dk, dv, w_in, x, g, dh1)


def _layer_grads(xs, target, g_pre_mix, w_in_f, conv_w_f, conv_b, conv_ln_g, conv_ln_b, attn_g, w_out_f, g_post_mix,
                 g_pre_ffn, w_gate_f, w_up_f, w_down_f, g_post_ffn):
    a, u_conv, qkv = _in_proj(xs, g_pre_mix, w_in_f)
    conv_out = _conv_fwd(u_conv, conv_w_f, conv_b, conv_ln_g, conv_ln_b)
    o, attn_out, cl = _attn_fwd(qkv, attn_g)
    y, h1, f_in = _out_proj(conv_out, attn_out, w_out_f, xs, g_post_mix, g_pre_ffn)
    gt, up, df, dh2, loss_part, d_g_post_ffn = _ffn_fwd_loss(f_in, w_gate_f, w_up_f, w_down_f, h1, target, g_post_ffn)

    dgt, dup, act, dfin = _ffn_bwd(df, gt, up, w_gate_f, w_up_f, w_down_f)
    d_w_down = _matmul_tn("grad_w_down", act, df, 512)
    d_w_gate = _matmul_tn("grad_w_gate", f_in, dgt, FF_CHUNK)
    d_w_up = _matmul_tn("grad_w_up", f_in, dup, FF_CHUNK)
    dh1, dy, dco, dao, d_g_pre_ffn, d_g_post_mix = _mix_bwd(dfin, h1, y, dh2, g_pre_ffn, g_post_mix, w_out_f)
    d_w_out = jnp.concatenate([_matmul_tn("grad_w_out_conv", conv_out, dy, D_MODEL),
                               _matmul_tn("grad_w_out_attn", attn_out, dy, D_MODEL)], axis=0)
    dq, dk, dv, d_attn_g = _attn_bwd(qkv, o, dao, cl, attn_g)
    du_conv, d_conv_w, d_conv_small = _conv_bwd(u_conv, dco, conv_w_f, conv_b, conv_ln_g, conv_ln_b)
    grad_x, d_g_pre_mix = _in_proj_bwd(du_conv, dq, dk, dv, w_in_f, xs, g_pre_mix, dh1)
    d_w_in = jnp.concatenate([_matmul_tn("grad_w_in_conv", a, du_conv, 2 * CONV_CH),
                              _matmul_tn("grad_w_in_q", a, dq, SB_WIDTH), _matmul_tn("grad_w_in_k", a, dk, SB_WIDTH),
                              _matmul_tn("grad_w_in_v", a, dv, SB_WIDTH)], axis=1)
    return (loss_part, grad_x, d_w_in, d_w_out, d_w_gate, d_w_up, d_w_down, d_conv_w, d_conv_small, d_attn_g,
            d_g_pre_mix, d_g_post_mix, d_g_pre_ffn, d_g_post_ffn)


def _cols_to_blocks(w):
    K, N = w.shape
    return jnp.transpose(w.reshape(K, N_DEV, N // N_DEV), (1, 0, 2))


def _blocks_to_cols(blocks):
    n_dev, K, n = blocks.shape
    return jnp.transpose(blocks, (1, 0, 2)).reshape(K, n_dev * n)


def kernel(x, g_pre_mix, w_in, conv_w, conv_b, conv_ln_g, conv_ln_b, attn_norm_g, w_out, g_post_mix, g_pre_ffn, w_gate, w_up, w_down, g_post_ffn, loss_target, m_g_pre_mix, m_w_in, m_conv_w, m_conv_b, m_conv_ln_g, m_conv_ln_b, m_attn_norm_g, m_w_out, m_g_post_mix, m_g_pre_ffn, m_w_gate, m_w_up, m_w_down, m_g_post_ffn, v_g_pre_mix, v_w_in, v_conv_w, v_conv_b, v_conv_ln_g, v_conv_ln_b, v_attn_norm_g, v_w_out, v_g_post_mix, v_g_pre_ffn, v_w_gate, v_w_up, v_w_down, v_g_post_ffn):
    xs = x[0]
    target = loss_target[0]
    S = xs.shape[0]
    me = 4 * lax.axis_index("x") + 2 * lax.axis_index("y") + lax.axis_index("c")
    cw_shard = conv_w.reshape(CONV_WIDTH, CONV_CH // N_DEV)
    attn_g = attn_norm_g.reshape(1, SB_WIDTH)

    gathered = _all_gather([w_in[0].astype(BF16), w_out[0].astype(BF16), w_gate[0].astype(BF16),
                            w_up[0].astype(BF16), w_down[0].astype(BF16), cw_shard])
    w_in_f = _blocks_to_cols(gathered[0])
    w_out_f = gathered[1].reshape(D_MODEL, D_MODEL)
    w_gate_f = _blocks_to_cols(gathered[2])
    w_up_f = _blocks_to_cols(gathered[3])
    w_down_f = gathered[4].reshape(D_FF, D_MODEL)
    conv_w_f = jnp.pad(_blocks_to_cols(gathered[5]), ((0, CONV_HALO - CONV_WIDTH), (0, 0)))

    (loss_part, grad_x, d_w_in, d_w_out, d_w_gate, d_w_up, d_w_down, d_conv_w, d_conv_small, d_attn_g, d_g_pre_mix,
     d_g_post_mix, d_g_pre_ffn, d_g_post_ffn) = _layer_grads(
        xs, target, g_pre_mix, w_in_f, conv_w_f, conv_b, conv_ln_g, conv_ln_b, attn_g, w_out_f, g_post_mix, g_pre_ffn,
        w_gate_f, w_up_f, w_down_f, g_post_ffn)

    big = {
        "w_in": _reduce_scatter_adamw("rs_adamw_w_in", _cols_to_blocks(d_w_in), w_in[0], m_w_in[0], v_w_in[0]),
        "w_out": _reduce_scatter_adamw("rs_adamw_w_out", d_w_out.reshape(N_DEV, D_MODEL // N_DEV, D_MODEL), w_out[0],
                                       m_w_out[0], v_w_out[0]),
        "w_gate": _reduce_scatter_adamw("rs_adamw_w_gate", _cols_to_blocks(d_w_gate), w_gate[0], m_w_gate[0], v_w_gate[0]),
        "w_up": _reduce_scatter_adamw("rs_adamw_w_up", _cols_to_blocks(d_w_up), w_up[0], m_w_up[0], v_w_up[0]),
        "w_down": _reduce_scatter_adamw("rs_adamw_w_down", d_w_down.reshape(N_DEV, D_FF // N_DEV, D_MODEL), w_down[0],
                                        m_w_down[0], v_w_down[0]),
    }

    two = lambda t: t.reshape(2, CONV_CH)
    small_g = jnp.concatenate([
        d_conv_w,
        d_conv_small[0:3],
        d_attn_g,
        two(d_g_pre_mix), two(d_g_post_mix), two(d_g_pre_ffn), two(d_g_post_ffn),
        jnp.zeros((4, CONV_CH), F32)], axis=0)
    small_g = _all_reduce_small(small_g)
    g_conv_w = lax.dynamic_slice(small_g, (0, me * (CONV_CH // N_DEV)), (CONV_WIDTH, CONV_CH // N_DEV))
    pack = lambda cb, lg, lb, ag, g1, g2, g3, g4: jnp.concatenate(
        [cb, lg, lb, ag.reshape(1, SB_WIDTH), two(g1), two(g2), two(g3), two(g4), jnp.zeros((4, CONV_CH), F32)], axis=0)
    sm_g = small_g[CONV_HALO:]
    sm_delta, sm_m, sm_v = _adamw_small(
        "adamw_small",
        pack(conv_b, conv_ln_g, conv_ln_b, attn_norm_g, g_pre_mix, g_post_mix, g_pre_ffn, g_post_ffn), sm_g,
        pack(m_conv_b, m_conv_ln_g, m_conv_ln_b, m_attn_norm_g, m_g_pre_mix, m_g_post_mix, m_g_pre_ffn, m_g_post_ffn),
        pack(v_conv_b, v_conv_ln_g, v_conv_ln_b, v_attn_norm_g, v_g_pre_mix, v_g_post_mix, v_g_pre_ffn, v_g_post_ffn))
    cw_delta, cw_m, cw_v = _adamw_small("adamw_conv_w", cw_shard, g_conv_w,
                                        m_conv_w.reshape(cw_shard.shape), v_conv_w.reshape(cw_shard.shape))

    def unpack(t):
        return {"conv_b": t[0:1], "conv_ln_g": t[1:2], "conv_ln_b": t[2:3], "attn_norm_g": t[3:4].reshape(1, SB_HEADS, SB_HEAD_DIM),
                "g_pre_mix": t[4:6].reshape(1, D_MODEL), "g_post_mix": t[6:8].reshape(1, D_MODEL),
                "g_pre_ffn": t[8:10].reshape(1, D_MODEL), "g_post_ffn": t[10:12].reshape(1, D_MODEL)}

    names = ["g_pre_mix", "w_in", "conv_w", "conv_b", "conv_ln_g", "conv_ln_b", "attn_norm_g", "w_out", "g_post_mix",
             "g_pre_ffn", "w_gate", "w_up", "w_down", "g_post_ffn"]
    kinds = []
    for idx, small in enumerate((sm_g, sm_delta, sm_m, sm_v)):
        d = unpack(small)
        d["conv_w"] = (g_conv_w, cw_delta, cw_m, cw_v)[idx].reshape(1, CONV_WIDTH, 1, CONV_CH // N_DEV)
        for n in big:
            d[n] = big[n][idx][None]
        kinds.append([d[n] for n in names])

    loss = lax.psum(loss_part[0, 0], ("x", "y", "c"))
    return (loss, grad_x[None], *kinds[0], *kinds[1], *kinds[2], *kinds[3])
```

```python
import functools
import math

import jax
import jax.numpy as jnp
from jax import lax
from jax.experimental import pallas as pl
from jax.experimental.pallas import tpu as pltpu

F32 = jnp.float32
BF16 = jnp.bfloat16
MESH = pl.DeviceIdType.MESH

N_DEV = 8
D_MODEL = 1024
CONV_CH = 512
CONV_WIDTH = 31
SB_HEADS = 8
SB_HEAD_DIM = 64
SB_WIDTH = SB_HEADS * SB_HEAD_DIM
D_FF = 2816
EPS = 1e-6
LOG2E = 1.4426950408889634
MASKED = -1e30
ADAM_LR = 0.001
ADAM_B1 = 0.9
ADAM_B2 = 0.999
ADAM_EPS = 1e-08
ADAM_WD = 0.01
ADAM_STEP = 10

LANES = 128
VMEM_LIMIT = 56 * 1024 * 1024
TOKEN_TILE = 512
FFN_TILE = 256
ATTN_STRIP = 32
ATTN_BLOCK = 256
CONV_HALO = 32
CONV_CHUNK = 64
FF_CHUNK = D_FF // 2


def _cparams(**kw):
    return pltpu.CompilerParams(vmem_limit_bytes=VMEM_LIMIT, **kw)


def _resident(shape):
    return pl.BlockSpec(shape, lambda *_: (0,) * len(shape), pipeline_mode=pl.Buffered(1))


def _const(shape):
    return pl.BlockSpec(shape, lambda *_: (0,) * len(shape))


def _rms_r(xf):
    return lax.rsqrt(jnp.mean(xf * xf, axis=-1, keepdims=True) + EPS)


def _rms_bwd(xf, r, g, dout):
    w = dout * g
    return r * (w - xf * (r * r) * jnp.mean(w * xf, axis=-1, keepdims=True))


def _dot(a, b):
    return jnp.dot(a, b, preferred_element_type=F32)


def _dot_nt(a, b):
    return lax.dot_general(a, b, (((1,), (1,)), ((), ())), preferred_element_type=F32)


def _dot_tn(a, b):
    return lax.dot_general(a, b, (((0,), (0,)), ((), ())), preferred_element_type=F32)


def _peer(x, y, c, k):
    px = 1 - x if (k >> 2) & 1 else x
    py = 1 - y if (k >> 1) & 1 else y
    pc = 1 - c if k & 1 else c
    return (px, py, pc), 4 * px + 2 * py + pc


def _all_gather(shards):
    n = len(shards)

    def body(*refs):
        ins, outs = refs[:n], refs[n:2 * n]
        send_sems, recv_sems, local_sems = refs[2 * n:]
        x, y, c = lax.axis_index("x"), lax.axis_index("y"), lax.axis_index("c")
        me = 4 * x + 2 * y + c
        copies = []
        for a in range(n):
            mine = pltpu.make_async_copy(ins[a], outs[a].at[me], local_sems.at[a])
            mine.start()
            copies.append(mine)
        for k in range(1, N_DEV):
            peer, peer_block = _peer(x, y, c, k)
            for a in range(n):
                s = a * (N_DEV - 1) + k - 1
                pltpu.make_async_remote_copy(
                    src_ref=ins[a], dst_ref=outs[a].at[me], send_sem=send_sems.at[s], recv_sem=recv_sems.at[s],
                    device_id=peer, device_id_type=MESH).start()
        for k in range(1, N_DEV):
            peer, peer_block = _peer(x, y, c, k)
            for a in range(n):
                s = a * (N_DEV - 1) + k - 1
                arrived = pltpu.make_async_remote_copy(
                    src_ref=ins[a], dst_ref=outs[a].at[peer_block], send_sem=send_sems.at[s],
                    recv_sem=recv_sems.at[s], device_id=peer, device_id_type=MESH)
                arrived.wait_send()
                arrived.wait_recv()
        for mine in copies:
            mine.wait()

    any_spec = pl.BlockSpec(memory_space=pl.ANY)
    return pl.pallas_call(
        body, name="all_gather_weights",
        out_shape=[jax.ShapeDtypeStruct((N_DEV,) + s.shape, s.dtype) for s in shards],
        in_specs=[any_spec] * n, out_specs=[any_spec] * n,
        scratch_shapes=[pltpu.SemaphoreType.DMA((n * (N_DEV - 1),)), pltpu.SemaphoreType.DMA((n * (N_DEV - 1),)),
                        pltpu.SemaphoreType.DMA((n,))],
        compiler_params=pltpu.CompilerParams(has_side_effects=True),
    )(*shards)


def _adamw(w, g, m, v):
    m = ADAM_B1 * m + (1.0 - ADAM_B1) * g
    v = ADAM_B2 * v + (1.0 - ADAM_B2) * (g * g)
    m_hat = m / (1.0 - ADAM_B1 ** ADAM_STEP)
    v_hat = v / (1.0 - ADAM_B2 ** ADAM_STEP)
    delta = -ADAM_LR * (m_hat / (jnp.sqrt(v_hat) + ADAM_EPS) + ADAM_WD * w)
    return delta, m, v


def _exchange_and_sum(src_block, recv_ref, send_sems, recv_sems, local_sem):
    x, y, c = lax.axis_index("x"), lax.axis_index("y"), lax.axis_index("c")
    me = 4 * x + 2 * y + c
    mine = pltpu.make_async_copy(src_block(me), recv_ref.at[me], local_sem)
    mine.start()
    for k in range(1, N_DEV):
        peer, peer_block = _peer(x, y, c, k)
        pltpu.make_async_remote_copy(
            src_ref=src_block(peer_block), dst_ref=recv_ref.at[me], send_sem=send_sems.at[k - 1],
            recv_sem=recv_sems.at[k - 1], device_id=peer, device_id_type=MESH).start()
    for k in range(1, N_DEV):
        peer, peer_block = _peer(x, y, c, k)
        arrived = pltpu.make_async_remote_copy(
            src_ref=src_block(peer_block), dst_ref=recv_ref.at[peer_block], send_sem=send_sems.at[k - 1],
            recv_sem=recv_sems.at[k - 1], device_id=peer, device_id_type=MESH)
        arrived.wait_send()
        arrived.wait_recv()
    mine.wait()


def _reduce_scatter_adamw(name, g_blocks, w, m, v):
    _, M, N = g_blocks.shape
    rows = math.gcd(M, 32)

    def body(g_ref, w_ref, m_ref, v_ref, grad_ref, delta_ref, nm_ref, nv_ref, recv_ref, send_sems, recv_sems, local_sem):
        _exchange_and_sum(lambda b: g_ref.at[b], recv_ref, send_sems, recv_sems, local_sem)

        def chunk(i, carry):
            r = pl.ds(pl.multiple_of(i * rows, rows), rows)
            g = recv_ref[0, r, :]
            for b in range(1, N_DEV):
                g = g + recv_ref[b, r, :]
            delta, nm, nv = _adamw(w_ref[r, :], g, m_ref[r, :], v_ref[r, :])
            grad_ref[r, :] = g
            delta_ref[r, :] = delta
            nm_ref[r, :] = nm
            nv_ref[r, :] = nv
            return carry

        lax.fori_loop(0, M // rows, chunk, 0)

    vmem = pl.BlockSpec(memory_space=pltpu.VMEM)
    shard = jax.ShapeDtypeStruct((M, N), F32)
    return pl.pallas_call(
        body, name=name, out_shape=[shard] * 4,
        in_specs=[pl.BlockSpec(memory_space=pl.ANY), vmem, vmem, vmem], out_specs=[vmem] * 4,
        scratch_shapes=[pltpu.VMEM((N_DEV, M, N), F32), pltpu.SemaphoreType.DMA((N_DEV - 1,)),
                        pltpu.SemaphoreType.DMA((N_DEV - 1,)), pltpu.SemaphoreType.DMA(())],
        compiler_params=_cparams(has_side_effects=True),
    )(g_blocks, w, m, v)


def _all_reduce_small(g):
    R, C = g.shape

    def body(g_ref, out_ref, recv_ref, send_sems, recv_sems, local_sem):
        _exchange_and_sum(lambda b: g_ref, recv_ref, send_sems, recv_sems, local_sem)
        total = recv_ref[0]
        for b in range(1, N_DEV):
            total = total + recv_ref[b]
        out_ref[...] = total

    vmem = pl.BlockSpec(memory_space=pltpu.VMEM)
    return pl.pallas_call(
        body, name="all_reduce_small_grads", out_shape=jax.ShapeDtypeStruct((R, C), F32),
        in_specs=[vmem], out_specs=vmem,
        scratch_shapes=[pltpu.VMEM((N_DEV, R, C), F32), pltpu.SemaphoreType.DMA((N_DEV - 1,)),
                        pltpu.SemaphoreType.DMA((N_DEV - 1,)), pltpu.SemaphoreType.DMA(())],
        compiler_params=_cparams(has_side_effects=True),
    )(g)


def _adamw_small(name, w, g, m, v):
    def body(w_ref, g_ref, m_ref, v_ref, delta_ref, nm_ref, nv_ref):
        delta, nm, nv = _adamw(w_ref[...], g_ref[...], m_ref[...], v_ref[...])
        delta_ref[...] = delta
        nm_ref[...] = nm
        nv_ref[...] = nv

    vmem = pl.BlockSpec(memory_space=pltpu.VMEM)
    return pl.pallas_call(body, name=name, out_shape=[jax.ShapeDtypeStruct(w.shape, F32)] * 3,
                          in_specs=[vmem] * 4, out_specs=[vmem] * 3)(w, g, m, v)


def _in_proj(x, g, w_in):
    S = x.shape[0]
    tm = min(TOKEN_TILE, S)
    nconv = 2 * CONV_CH

    def body(x_ref, g_ref, w_ref, a_ref, uc_ref, qkv_ref):
        xf = x_ref[...]
        a = (xf * _rms_r(xf) * g_ref[...]).astype(BF16)
        a_ref[...] = a
        uc_ref[...] = _dot(a, w_ref[:, :nconv])
        qkv_ref[:, :SB_WIDTH] = (_dot(a, w_ref[:, nconv:nconv + SB_WIDTH]) * (1.0 / math.sqrt(SB_HEAD_DIM))).astype(BF16)
        qkv_ref[:, SB_WIDTH:] = _dot(a, w_ref[:, nconv + SB_WIDTH:]).astype(BF16)

    row = lambda n: pl.BlockSpec((tm, n), lambda i: (i, 0))
    return pl.pallas_call(
        body, name="in_proj", grid=(S // tm,),
        out_shape=[jax.ShapeDtypeStruct((S, D_MODEL), BF16), jax.ShapeDtypeStruct((S, nconv), F32),
                   jax.ShapeDtypeStruct((S, 3 * SB_WIDTH), BF16)],
        in_specs=[row(D_MODEL), _const((1, D_MODEL)), _resident(w_in.shape)],
        out_specs=[row(D_MODEL), row(nconv), row(3 * SB_WIDTH)],
        compiler_params=_cparams(),
    )(x, g, w_in)


def _glu(u):
    val, gate = u[:, :CONV_CH], u[:, CONV_CH:]
    sg = jax.nn.sigmoid(gate)
    return val, sg, val * sg


def _conv_rows(glu_ext, cw_ref, r0, rows):
    base = r0 + CONV_HALO - (CONV_WIDTH - 1)
    acc = cw_ref[0:1, :] * glu_ext[base:base + rows, :]
    for w in range(1, CONV_WIDTH):
        acc = acc + cw_ref[w:w + 1, :] * glu_ext[base + w:base + w + rows, :]
    return acc


def _conv_fwd(u_conv, conv_w, conv_b, ln_g, ln_b):
    S = u_conv.shape[0]
    tc = min(TOKEN_TILE, S)

    def body(u_ref, cw_ref, cb_ref, lg_ref, lb_ref, out_ref, glu_ext):
        i = pl.program_id(0)

        @pl.when(i == 0)
        def _():
            glu_ext[0:CONV_HALO, :] = jnp.zeros((CONV_HALO, CONV_CH), F32)

        @pl.when(i > 0)
        def _():
            glu_ext[0:CONV_HALO, :] = glu_ext[tc:tc + CONV_HALO, :]

        glu_ext[CONV_HALO:, :] = _glu(u_ref[...])[2]
        for r0 in range(0, tc, CONV_CHUNK):
            y = _conv_rows(glu_ext, cw_ref, r0, CONV_CHUNK) + cb_ref[...]
            mu = jnp.mean(y, axis=-1, keepdims=True)
            yc = y - mu
            yn = yc * lax.rsqrt(jnp.mean(yc * yc, axis=-1, keepdims=True) + EPS)
            yl = yn * lg_ref[...] + lb_ref[...]
            out_ref[r0:r0 + CONV_CHUNK, :] = (yl * jax.nn.sigmoid(yl)).astype(BF16)

    return pl.pallas_call(
        body, name="conv_fwd", grid=(S // tc,),
        out_shape=jax.ShapeDtypeStruct((S, CONV_CH), BF16),
        in_specs=[pl.BlockSpec((tc, 2 * CONV_CH), lambda i: (i, 0)), _const((CONV_HALO, CONV_CH)),
                  _const((1, CONV_CH)), _const((1, CONV_CH)), _const((1, CONV_CH))],
        out_specs=pl.BlockSpec((tc, CONV_CH), lambda i: (i, 0)),
        scratch_shapes=[pltpu.VMEM((tc + CONV_HALO, CONV_CH), F32)],
        compiler_params=_cparams(dimension_semantics=("arbitrary",)),
    )(u_conv, conv_w, conv_b, ln_g, ln_b)


def _head_masks():
    lane = lax.broadcasted_iota(jnp.int32, (1, LANES), 1)
    return lane < SB_HEAD_DIM


def _split_heads(t, first):
    z = jnp.zeros_like(t)
    return jnp.where(first, t, z), jnp.where(first, z, t)


def _head_sum(t, first):
    a = jnp.sum(jnp.where(first, t, 0.0), axis=-1, keepdims=True)
    b = jnp.sum(jnp.where(first, 0.0, t), axis=-1, keepdims=True)
    return a, b


def _log2_sigmoids(z):
    z2 = z * LOG2E
    lb = jnp.minimum(z2, 0.0) - jnp.log2(1.0 + jnp.exp2(-jnp.abs(z2)))
    return lb, lb - z2


def _split_bf16(t):
    hi = t.astype(BF16)
    return hi, (t - hi.astype(F32)).astype(BF16)


def _block_masks(n):
    row = lax.broadcasted_iota(jnp.int32, (n, n), 0)
    col = lax.broadcasted_iota(jnp.int32, (n, n), 1)
    return row > col, row < col


def _attn_fwd(qkv, g_attn):
    S = qkv.shape[0]
    Q = min(ATTN_BLOCK, S)
    nq = S // Q
    assert nq <= LANES
    npair = SB_WIDTH // LANES

    def body(q_ref, k_ref, v_ref, g_ref, o_ref, ao_ref, cl_ref, z_buf, hi_buf, lo_buf, z2_buf, a_buf, c_buf):
        i = pl.program_id(1)
        first = _head_masks()
        qh = _split_heads(q_ref[...], first)
        row = lax.broadcasted_iota(jnp.int32, (Q, Q), 0)
        col = lax.broadcasted_iota(jnp.int32, (Q, Q), 1)
        tri = (row >= col).astype(BF16)
        lane = lax.broadcasted_iota(jnp.int32, (1, LANES), 1)
        heads = range(2)
        strips = [slice(r0, r0 + ATTN_STRIP) for r0 in range(0, Q, ATTN_STRIP)]
        rows = lambda j: pl.ds(pl.multiple_of(j * Q, Q), Q)

        def scores(j, slot):
            kb = k_ref[rows(j), :]
            for h in heads:
                z_buf[slot, h] = _dot_nt(qh[h], kb)

        def logs(slot, diag):
            for h in heads:
                for r in strips:
                    z2 = z_buf[slot, h, r, :] * LOG2E
                    l = (jnp.minimum(z2, 0.0) - jnp.log2(1.0 + jnp.exp2(-jnp.abs(z2)))) - z2
                    if diag:
                        keep = col[r] < row[r]
                        l = jnp.where(keep, l, 0.0)
                        z2 = jnp.where(keep, z2, MASKED)
                    hi, lo = _split_bf16(l)
                    hi_buf[slot, h, r, :] = hi
                    lo_buf[slot, h, r, :] = lo
                    z2_buf[slot, h, r, :] = z2

        def sums(slot):
            return tuple(_dot(hi_buf[slot, h], tri) + _dot(lo_buf[slot, h], tri) for h in heads)

        def weights(j, slot, sm):
            for h in heads:
                before = c_buf[h]
                for r in strips:
                    a_buf[slot, h, r, :] = jnp.exp2(z2_buf[slot, h, r, :] + sm[h][r] + jnp.tile(before[r], (1, Q // LANES))).astype(BF16)
                hl = slice(h * LANES, (h + 1) * LANES)
                cl_ref[:, hl] = jnp.where(lane == j, before, cl_ref[:, hl])
                c_buf[h] = before + jnp.broadcast_to(sm[h][:, 0:1], (Q, LANES))

        def values(slot, j):
            vh = _split_heads(v_ref[rows(j), :], first)
            return _dot(a_buf[slot, 0], vh[0]) + _dot(a_buf[slot, 1], vh[1])

        def iteration(m, p):
            b = i - 1 - m
            scores(jnp.maximum(b - 1, 0), 1 - p)
            sm = sums(1 - p)
            o_ref[...] += values(p, jnp.minimum(b + 2, i))
            logs(p, False)
            weights(b + 1, 1 - p, sm)

        def two_iterations(t, carry):
            iteration(2 * t, 0)
            iteration(2 * t + 1, 1)
            return carry

        o_ref[...] = jnp.zeros_like(o_ref)
        cl_ref[...] = jnp.zeros_like(cl_ref)
        c_buf[...] = jnp.zeros_like(c_buf)
        a_buf[0] = jnp.zeros((2, Q, Q), BF16)
        scores(i, 1)
        logs(1, True)

        @pl.when(i > 0)
        def _():
            scores(i - 1, 0)

        lax.fori_loop(0, i // 2, two_iterations, 0)

        @pl.when(i % 2 == 1)
        def _():
            iteration(i - 1, 0)

        last = (i + 1) % 2
        weights(0, last, sums(last))
        o = o_ref[...] + values(1 - last, jnp.minimum(1, i)) + values(last, 0)
        sa, sb = _head_sum(o * o, first)
        r = jnp.where(first, lax.rsqrt(sa * (1.0 / SB_HEAD_DIM) + EPS), lax.rsqrt(sb * (1.0 / SB_HEAD_DIM) + EPS))
        o_ref[...] = o
        ao_ref[...] = (o * r * g_ref[...]).astype(BF16)

    kv = lambda off: pl.BlockSpec((S, LANES), lambda p, i: (0, off + p))
    return pl.pallas_call(
        body, name="attn_fwd", grid=(npair, nq),
        out_shape=[jax.ShapeDtypeStruct((S, SB_WIDTH), F32), jax.ShapeDtypeStruct((S, SB_WIDTH), BF16),
                   jax.ShapeDtypeStruct((S, 2 * SB_WIDTH), F32)],
        in_specs=[pl.BlockSpec((Q, LANES), lambda p, i: (i, p)), kv(npair), kv(2 * npair),
                  pl.BlockSpec((1, LANES), lambda p, i: (0, p))],
        out_specs=[pl.BlockSpec((Q, LANES), lambda p, i: (i, p)), pl.BlockSpec((Q, LANES), lambda p, i: (i, p)),
                   pl.BlockSpec((Q, 2 * LANES), lambda p, i: (i, p))],
        scratch_shapes=[pltpu.VMEM((2, 2, Q, Q), F32), pltpu.VMEM((2, 2, Q, Q), BF16), pltpu.VMEM((2, 2, Q, Q), BF16),
                        pltpu.VMEM((2, 2, Q, Q), F32), pltpu.VMEM((2, 2, Q, Q), BF16), pltpu.VMEM((2, Q, LANES), F32)],
        compiler_params=_cparams(dimension_semantics=("arbitrary", "arbitrary")),
    )(qkv, qkv, qkv, g_attn)


def _out_proj(conv_out, attn_out, w_out, x, g_post_mix, g_pre_ffn):
    S = x.shape[0]
    tm = min(TOKEN_TILE, S)

    def body(co_ref, ao_ref, w_ref, x_ref, g1_ref, g2_ref, y_ref, h1_ref, fin_ref):
        y = _dot(co_ref[...], w_ref[:CONV_CH, :]) + _dot(ao_ref[...], w_ref[CONV_CH:, :])
        h1 = x_ref[...] + y * _rms_r(y) * g1_ref[...]
        y_ref[...] = y
        h1_ref[...] = h1
        fin_ref[...] = (h1 * _rms_r(h1) * g2_ref[...]).astype(BF16)

    row = lambda n: pl.BlockSpec((tm, n), lambda i: (i, 0))
    return pl.pallas_call(
        body, name="out_proj", grid=(S // tm,),
        out_shape=[jax.ShapeDtypeStruct((S, D_MODEL), F32), jax.ShapeDtypeStruct((S, D_MODEL), F32),
                   jax.ShapeDtypeStruct((S, D_MODEL), BF16)],
        in_specs=[row(CONV_CH), row(SB_WIDTH), _resident(w_out.shape), row(D_MODEL), _const((1, D_MODEL)),
                  _const((1, D_MODEL))],
        out_specs=[row(D_MODEL)] * 3,
        compiler_params=_cparams(),
    )(conv_out, attn_out, w_out, x, g_post_mix, g_pre_ffn)


def _ffn_fwd_loss(f_in, w_gate, w_up, w_down, h1, target, g_post_ffn):
    S = f_in.shape[0]
    tm = min(FFN_TILE, S)
    nt = S // tm

    def body(fin_ref, wg_ref, wu_ref, wd_ref, h1_ref, t_ref, g_ref, gt_ref, up_ref, df_ref, dh2_ref, loss_ref, dg_ref,
             sq_acc):
        i = pl.program_id(0)

        @pl.when(i == 0)
        def _():
            sq_acc[...] = jnp.zeros_like(sq_acc)
            dg_ref[...] = jnp.zeros_like(dg_ref)

        fin = fin_ref[...]
        f = jnp.zeros((tm, D_MODEL), F32)
        for c0 in range(0, D_FF, FF_CHUNK):
            cols = slice(c0, c0 + FF_CHUNK)
            gt = _dot(fin, wg_ref[:, cols])
            up = _dot(fin, wu_ref[:, cols])
            gt_ref[:, cols] = gt.astype(BF16)
            up_ref[:, cols] = up.astype(BF16)
            f = f + _dot((gt * jax.nn.sigmoid(gt) * up).astype(BF16), wd_ref[cols, :])
        r = _rms_r(f)
        g = g_ref[...]
        diff = h1_ref[...] + f * r * g - t_ref[...]
        sq_acc[...] += jnp.sum(diff * diff, axis=0, keepdims=True)
        dh2 = diff * (1.0 / D_MODEL)
        dh2_ref[...] = dh2
        dg_ref[...] += jnp.sum(dh2 * f * r, axis=0, keepdims=True)
        df_ref[...] = _rms_bwd(f, r, g, dh2).astype(BF16)

        @pl.when(i == nt - 1)
        def _():
            loss_ref[...] = jnp.broadcast_to((0.5 / D_MODEL) * jnp.sum(sq_acc[...], axis=-1, keepdims=True), (1, LANES))

    row = lambda n: pl.BlockSpec((tm, n), lambda i: (i, 0))
    return pl.pallas_call(
        body, name="ffn_fwd_loss", grid=(nt,),
        out_shape=[jax.ShapeDtypeStruct((S, D_FF), BF16), jax.ShapeDtypeStruct((S, D_FF), BF16),
                   jax.ShapeDtypeStruct((S, D_MODEL), BF16), jax.ShapeDtypeStruct((S, D_MODEL), F32),
                   jax.ShapeDtypeStruct((1, LANES), F32), jax.ShapeDtypeStruct((1, D_MODEL), F32)],
        in_specs=[row(D_MODEL), _resident(w_gate.shape), _resident(w_up.shape), _resident(w_down.shape),
                  row(D_MODEL), row(D_MODEL), _const((1, D_MODEL))],
        out_specs=[row(D_FF), row(D_FF), row(D_MODEL), row(D_MODEL), _const((1, LANES)), _const((1, D_MODEL))],
        scratch_shapes=[pltpu.VMEM((1, D_MODEL), F32)],
        compiler_params=_cparams(dimension_semantics=("arbitrary",)),
    )(f_in, w_gate, w_up, w_down, h1, target, g_post_ffn)


def _ffn_bwd(df, gt, up, w_gate, w_up, w_down):
    S = df.shape[0]
    tm = min(FFN_TILE, S)

    def body(df_ref, gt_ref, up_ref, wg_ref, wu_ref, wd_ref, dgt_ref, dup_ref, act_ref, dfin_ref):
        df = df_ref[...]
        dfin = jnp.zeros((tm, D_MODEL), F32)
        for c0 in range(0, D_FF, FF_CHUNK):
            cols = slice(c0, c0 + FF_CHUNK)
            dact = _dot_nt(df, wd_ref[cols, :])
            gt = gt_ref[:, cols].astype(F32)
            up = up_ref[:, cols].astype(F32)
            s = jax.nn.sigmoid(gt)
            silu = gt * s
            dgt = (dact * up * (s * (1.0 + gt * (1.0 - s)))).astype(BF16)
            dup = (dact * silu).astype(BF16)
            act_ref[:, cols] = (silu * up).astype(BF16)
            dgt_ref[:, cols] = dgt
            dup_ref[:, cols] = dup
            dfin = dfin + _dot_nt(dgt, wg_ref[:, cols]) + _dot_nt(dup, wu_ref[:, cols])
        dfin_ref[...] = dfin

    row = lambda n: pl.BlockSpec((tm, n), lambda i: (i, 0))
    return pl.pallas_call(
        body, name="ffn_bwd", grid=(S // tm,),
        out_shape=[jax.ShapeDtypeStruct((S, D_FF), BF16)] * 3 + [jax.ShapeDtypeStruct((S, D_MODEL), F32)],
        in_specs=[row(D_MODEL), row(D_FF), row(D_FF), _resident(w_gate.shape), _resident(w_up.shape),
                  _resident(w_down.shape)],
        out_specs=[row(D_FF)] * 3 + [row(D_MODEL)],
        compiler_params=_cparams(),
    )(df, gt, up, w_gate, w_up, w_down)


def _matmul_tn(name, x, y, tn):
    S, K = x.shape
    N = y.shape[1]
    ts = min(TOKEN_TILE, S)

    def body(x_ref, y_ref, o_ref):
        @pl.when(pl.program_id(1) == 0)
        def _():
            o_ref[...] = jnp.zeros_like(o_ref)

        o_ref[...] += _dot_tn(x_ref[...].astype(BF16), y_ref[...].astype(BF16))

    return pl.pallas_call(
        body, name=name, grid=(N // tn, S // ts),
        out_shape=jax.ShapeDtypeStruct((K, N), F32),
        in_specs=[pl.BlockSpec((ts, K), lambda n, s: (s, 0)), pl.BlockSpec((ts, tn), lambda n, s: (s, n))],
        out_specs=pl.BlockSpec((K, tn), lambda n, s: (0, n)),
        compiler_params=_cparams(dimension_semantics=("arbitrary", "arbitrary")),
    )(x, y)


def _mix_bwd(dfin, h1, y, dh2, g_pre_ffn, g_post_mix, w_out):
    S = dfin.shape[0]
    tm = min(TOKEN_TILE, S)

    def body(dfin_ref, h1_ref, y_ref, dh2_ref, g2_ref, g1_ref, w_ref, dh1_ref, dy_ref, dco_ref, dao_ref, dg2_ref, dg1_ref):
        @pl.when(pl.program_id(0) == 0)
        def _():
            dg2_ref[...] = jnp.zeros_like(dg2_ref)
            dg1_ref[...] = jnp.zeros_like(dg1_ref)

        h1, dfin = h1_ref[...], dfin_ref[...]
        r2 = _rms_r(h1)
        dh1 = dh2_ref[...] + _rms_bwd(h1, r2, g2_ref[...], dfin)
        dg2_ref[...] += jnp.sum(dfin * h1 * r2, axis=0, keepdims=True)
        y = y_ref[...]
        r1 = _rms_r(y)
        dy = _rms_bwd(y, r1, g1_ref[...], dh1).astype(BF16)
        dg1_ref[...] += jnp.sum(dh1 * y * r1, axis=0, keepdims=True)
        dh1_ref[...] = dh1
        dy_ref[...] = dy
        dco_ref[...] = _dot_nt(dy, w_ref[:CONV_CH, :])
        dao_ref[...] = _dot_nt(dy, w_ref[CONV_CH:, :])

    row = lambda n: pl.BlockSpec((tm, n), lambda i: (i, 0))
    return pl.pallas_call(
        body, name="mix_bwd", grid=(S // tm,),
        out_shape=[jax.ShapeDtypeStruct((S, D_MODEL), F32), jax.ShapeDtypeStruct((S, D_MODEL), BF16),
                   jax.ShapeDtypeStruct((S, CONV_CH), F32), jax.ShapeDtypeStruct((S, SB_WIDTH), F32),
                   jax.ShapeDtypeStruct((1, D_MODEL), F32), jax.ShapeDtypeStruct((1, D_MODEL), F32)],
        in_specs=[row(D_MODEL)] * 4 + [_const((1, D_MODEL)), _const((1, D_MODEL)), _resident(w_out.shape)],
        out_specs=[row(D_MODEL), row(D_MODEL), row(CONV_CH), row(SB_WIDTH), _const((1, D_MODEL)), _const((1, D_MODEL))],
        compiler_params=_cparams(dimension_semantics=("arbitrary",)),
    )(dfin, h1, y, dh2, g_pre_ffn, g_post_mix, w_out)


def _attn_bwd(qkv, o, dao, cl, g_attn):
    S = qkv.shape[0]
    Q = min(ATTN_BLOCK, S)
    nq = S // Q
    npair = SB_WIDTH // LANES
    inv_dh = 1.0 / SB_HEAD_DIM

    def body(q_ref, k_ref, v_ref, o_ref, dao_ref, cl_ref, g_ref, dq_ref, dk_ref, dv_ref, dg_ref,
             z_buf, lb_buf, be_buf, g_buf, hi_buf, lo_buf, a_buf, ghi_buf, glo_buf, dz_buf, pg_buf, dq_acc):
        i = pl.program_id(1)

        @pl.when(i == 0)
        def _():
            dk_ref[...] = jnp.zeros_like(dk_ref)
            dv_ref[...] = jnp.zeros_like(dv_ref)
            dg_ref[...] = jnp.zeros_like(dg_ref)

        first = _head_masks()
        lane = lax.broadcasted_iota(jnp.int32, (1, LANES), 1)
        o, dao, g = o_ref[...], dao_ref[...], g_ref[...]
        sa, sb = _head_sum(o * o, first)
        r = jnp.where(first, lax.rsqrt(sa * inv_dh + EPS), lax.rsqrt(sb * inv_dh + EPS))
        w = dao * g
        wa, wb = _head_sum(w * o, first)
        do = r * (w - o * (r * r) * (jnp.where(first, wa, wb) * inv_dh))
        dg_ref[...] += jnp.sum(dao * o * r, axis=0, keepdims=True)
        doh = _split_heads(do.astype(BF16), first)

        qh = _split_heads(q_ref[...], first)
        row = lax.broadcasted_iota(jnp.int32, (Q, Q), 0)
        col = lax.broadcasted_iota(jnp.int32, (Q, Q), 1)
        tri = (row > col).astype(BF16)
        tpi = (row <= col).astype(BF16)
        heads = range(2)
        strips = [slice(r0, r0 + ATTN_STRIP) for r0 in range(0, Q, ATTN_STRIP)]
        rows = lambda j: pl.ds(pl.multiple_of(j * Q, Q), Q)
        wide = lambda t: jnp.tile(t, (1, Q // LANES))

        def scores(j, slot):
            kb = k_ref[rows(j), :]
            for h in heads:
                z_buf[slot, h] = _dot_nt(qh[h], kb)

        def logs(slot, diag):
            for h in heads:
                for r in strips:
                    z2 = z_buf[slot, h, r, :] * LOG2E
                    lb = jnp.minimum(z2, 0.0) - jnp.log2(1.0 + jnp.exp2(-jnp.abs(z2)))
                    l = lb - z2
                    if diag:
                        keep = col[r] < row[r]
                        l = jnp.where(keep, l, 0.0)
                        lb = jnp.where(keep, lb, MASKED)
                    hi, lo = _split_bf16(l)
                    hi_buf[slot, h, r, :] = hi
                    lo_buf[slot, h, r, :] = lo
                    lb_buf[slot, h, r, :] = lb

        def sums(j, slot):
            vb = v_ref[rows(j), :]
            return (tuple(_dot(hi_buf[slot, h], tri) + _dot(lo_buf[slot, h], tri) for h in heads),
                    tuple(_dot_nt(doh[h], vb) for h in heads))

        def weights(j, slot, sm, da):
            for h in heads:
                hl = slice(h * LANES, (h + 1) * LANES)
                c = jnp.sum(jnp.where(lane == j, cl_ref[:, hl], 0.0), axis=-1, keepdims=True)
                c = jnp.broadcast_to(c, (Q, LANES))
                for r in strips:
                    lb = lb_buf[slot, h, r, :]
                    a = jnp.exp2(lb + sm[h][r] + wide(c[r]))
                    g = da[h][r] * a
                    a_buf[slot, h, r, :] = a.astype(BF16)
                    be_buf[slot, h, r, :] = jnp.exp2(lb)
                    g_buf[slot, h, r, :] = g
                    hi, lo = _split_bf16(g)
                    ghi_buf[slot, h, r, :] = hi
                    glo_buf[slot, h, r, :] = lo

        def prefix(j, slot):
            dv_ref[rows(j), :] += _dot_tn(a_buf[slot, 0], doh[0]) + _dot_tn(a_buf[slot, 1], doh[1])
            return tuple(_dot(ghi_buf[slot, h], tpi) + _dot(glo_buf[slot, h], tpi) for h in heads)

        def dscores(slot, pm):
            for h in heads:
                pg = pg_buf[h]
                for r in strips:
                    dz = g_buf[slot, h, r, :] - be_buf[slot, h, r, :] * (pm[h][r] + wide(pg[r]))
                    dz_buf[slot, h, r, :] = dz.astype(BF16)
                pg_buf[h] = pg + jnp.broadcast_to(pm[h][:, Q - 1:Q], (Q, LANES))

        def grads(j, slot):
            kh = _split_heads(k_ref[rows(j), :], first)
            dq_acc[...] += _dot(dz_buf[slot, 0], kh[0]) + _dot(dz_buf[slot, 1], kh[1])
            dk_ref[rows(j), :] += _dot_tn(dz_buf[slot, 0], qh[0]) + _dot_tn(dz_buf[slot, 1], qh[1])

        def iteration(n, p, a=True, b="plain", c=True, d=True, e=True):
            if a:
                scores(n + 1, 1 - p)
            if c:
                sm, da = sums(n - 1, 1 - p)
            if d:
                pm = prefix(n - 2, p)
            if e:
                grads(n - 3, 1 - p)
            if b:
                logs(p, b == "diag")
            if c:
                weights(n - 1, 1 - p, sm, da)
            if d:
                dscores(p, pm)

        def two_iterations(t, carry):
            iteration(3 + 2 * t, 1)
            iteration(4 + 2 * t, 0)
            return carry

        pg_buf[...] = jnp.zeros_like(pg_buf)
        dq_acc[...] = jnp.zeros_like(dq_acc)

        @pl.when(i < 3)
        def _():
            def one_block(j, diag):
                scores(j, 0)
                logs(0, diag)
                weights(j, 0, *sums(j, 0))
                dscores(0, prefix(j, 0))
                grads(j, 0)

            lax.fori_loop(0, i, lambda j, carry: (one_block(j, False), carry)[1], 0)
            one_block(i, True)

        @pl.when(i >= 3)
        def _():
            scores(0, 0)
            iteration(0, 0, c=False, d=False, e=False)
            iteration(1, 1, d=False, e=False)
            iteration(2, 0, e=False)
            lax.fori_loop(0, (i - 3) // 2, two_iterations, 0)

            @pl.when(i % 2 == 0)
            def _():
                iteration(i - 1, 1)

            p = i % 2
            iteration(i, p, a=False, b="diag")
            iteration(i + 1, 1 - p, a=False, b=None)
            iteration(i + 2, p, a=False, b=None, c=False)
            iteration(i + 3, 1 - p, a=False, b=None, c=False, d=False)

        dq_ref[...] = (dq_acc[...] * (1.0 / math.sqrt(SB_HEAD_DIM))).astype(BF16)


    kv = lambda off: pl.BlockSpec((S, LANES), lambda p, i: (0, off + p))
    blk = pl.BlockSpec((Q, LANES), lambda p, i: (i, p))
    acc = pl.BlockSpec((S, LANES), lambda p, i: (0, p))
    return pl.pallas_call(
        body, name="attn_bwd", grid=(npair, nq),
        out_shape=[jax.ShapeDtypeStruct((S, SB_WIDTH), BF16), jax.ShapeDtypeStruct((S, SB_WIDTH), F32),
                   jax.ShapeDtypeStruct((S, SB_WIDTH), F32), jax.ShapeDtypeStruct((1, SB_WIDTH), F32)],
        in_specs=[blk, kv(npair), kv(2 * npair), blk, blk, pl.BlockSpec((Q, 2 * LANES), lambda p, i: (i, p)),
                  pl.BlockSpec((1, LANES), lambda p, i: (0, p))],
        out_specs=[blk, acc, acc, pl.BlockSpec((1, LANES), lambda p, i: (0, p))],
        scratch_shapes=[pltpu.VMEM((2, 2, Q, Q), F32)] * 4 + [pltpu.VMEM((2, 2, Q, Q), BF16)] * 6
        + [pltpu.VMEM((2, Q, LANES), F32), pltpu.VMEM((Q, LANES), F32)],
        compiler_params=_cparams(dimension_semantics=("arbitrary", "arbitrary")),
    )(qkv, qkv, qkv, o, dao, cl, g_attn)


def _conv_bwd(u_conv, dco, conv_w, conv_b, ln_g, ln_b):
    S = u_conv.shape[0]
    tc = min(TOKEN_TILE, S)
    nt = S // tc
    per = tc // CONV_HALO
    groups = CONV_CHUNK // 8

    def body(u_ref, halo_ref, dco_ref, cw_ref, cb_ref, lg_ref, lb_ref, du_ref, dcw_ref, dsm_ref, glu_ext, dyc_ext, sg_buf,
             dcw_acc, dsm_acc):
        i = pl.program_id(0)
        ti = nt - 1 - i

        @pl.when(i == 0)
        def _():
            dyc_ext[tc:, :] = jnp.zeros((CONV_HALO, CONV_CH), F32)
            dcw_acc[...] = jnp.zeros_like(dcw_acc)
            dsm_acc[...] = jnp.zeros_like(dsm_acc)

        @pl.when(i > 0)
        def _():
            dyc_ext[tc:, :] = dyc_ext[0:CONV_HALO, :]

        glu_ext[0:CONV_HALO, :] = jnp.where(ti > 0, _glu(halo_ref[...])[2], 0.0)
        val, sg, glu = _glu(u_ref[...])
        glu_ext[CONV_HALO:, :] = glu
        sg_buf[...] = sg

        dcb = jnp.zeros((8, CONV_CH), F32)
        dlg = jnp.zeros((8, CONV_CH), F32)
        dlb = jnp.zeros((8, CONV_CH), F32)
        fold = lambda t: jnp.sum(t.reshape(groups, 8, CONV_CH), axis=0)
        for r0 in range(0, tc, CONV_CHUNK):
            y = _conv_rows(glu_ext, cw_ref, r0, CONV_CHUNK) + cb_ref[...]
            mu = jnp.mean(y, axis=-1, keepdims=True)
            yc = y - mu
            rstd = lax.rsqrt(jnp.mean(yc * yc, axis=-1, keepdims=True) + EPS)
            yn = yc * rstd
            yl = yn * lg_ref[...] + lb_ref[...]
            s = jax.nn.sigmoid(yl)
            dyl = dco_ref[r0:r0 + CONV_CHUNK, :] * (s * (1.0 + yl * (1.0 - s)))
            dlg = dlg + fold(dyl * yn)
            dlb = dlb + fold(dyl)
            wv = dyl * lg_ref[...]
            dyc = rstd * (wv - jnp.mean(wv, axis=-1, keepdims=True) - yn * jnp.mean(wv * yn, axis=-1, keepdims=True))
            dcb = dcb + fold(dyc)
            dyc_ext[r0:r0 + CONV_CHUNK, :] = dyc
        dsm_acc[0:8, :] += dcb
        dsm_acc[8:16, :] += dlg
        dsm_acc[16:24, :] += dlb

        for r0 in range(0, tc, CONV_CHUNK):
            dyc = dyc_ext[r0:r0 + CONV_CHUNK, :]
            dglu = jnp.zeros((CONV_CHUNK, CONV_CH), F32)
            base = r0 + CONV_HALO - (CONV_WIDTH - 1)
            for w in range(CONV_WIDTH):
                back = r0 + (CONV_WIDTH - 1) - w
                dglu = dglu + cw_ref[w:w + 1, :] * dyc_ext[back:back + CONV_CHUNK, :]
                dcw_acc[8 * w:8 * w + 8, :] += fold(dyc * glu_ext[base + w:base + w + CONV_CHUNK, :])
            sg = sg_buf[r0:r0 + CONV_CHUNK, :]
            v = u_ref[r0:r0 + CONV_CHUNK, :CONV_CH]
            du_ref[r0:r0 + CONV_CHUNK, :CONV_CH] = (dglu * sg).astype(BF16)
            du_ref[r0:r0 + CONV_CHUNK, CONV_CH:] = (dglu * v * sg * (1.0 - sg)).astype(BF16)

        @pl.when(i == nt - 1)
        def _():
            for w in range(CONV_WIDTH):
                dcw_ref[w:w + 1, :] = jnp.sum(dcw_acc[8 * w:8 * w + 8, :], axis=0, keepdims=True)
            dcw_ref[CONV_WIDTH:, :] = jnp.zeros((CONV_HALO - CONV_WIDTH, CONV_CH), F32)
            for k in range(3):
                dsm_ref[k:k + 1, :] = jnp.sum(dsm_acc[8 * k:8 * k + 8, :], axis=0, keepdims=True)
            dsm_ref[3:, :] = jnp.zeros((5, CONV_CH), F32)

    return pl.pallas_call(
        body, name="conv_bwd", grid=(nt,),
        out_shape=[jax.ShapeDtypeStruct((S, 2 * CONV_CH), BF16), jax.ShapeDtypeStruct((CONV_HALO, CONV_CH), F32),
                   jax.ShapeDtypeStruct((8, CONV_CH), F32)],
        in_specs=[pl.BlockSpec((tc, 2 * CONV_CH), lambda i: (nt - 1 - i, 0)),
                  pl.BlockSpec((CONV_HALO, 2 * CONV_CH), lambda i: (jnp.maximum((nt - 1 - i) * per - 1, 0), 0)),
                  pl.BlockSpec((tc, CONV_CH), lambda i: (nt - 1 - i, 0)),
                  _const((CONV_HALO, CONV_CH)), _const((1, CONV_CH)), _const((1, CONV_CH)), _const((1, CONV_CH))],
        out_specs=[pl.BlockSpec((tc, 2 * CONV_CH), lambda i: (nt - 1 - i, 0)), _const((CONV_HALO, CONV_CH)),
                   _const((8, CONV_CH))],
        scratch_shapes=[pltpu.VMEM((tc + CONV_HALO, CONV_CH), F32), pltpu.VMEM((tc + CONV_HALO, CONV_CH), F32),
                        pltpu.VMEM((tc, CONV_CH), F32), pltpu.VMEM((8 * CONV_HALO, CONV_CH), F32),
                        pltpu.VMEM((24, CONV_CH), F32)],
        compiler_params=_cparams(dimension_semantics=("arbitrary",)),
    )(u_conv, u_conv, dco, conv_w, conv_b, ln_g, ln_b)


def _in_proj_bwd(du_conv, dq, dk, dv, w_in, x, g, dh1):
    S = x.shape[0]
    tm = min(TOKEN_TILE, S)
    nconv = 2 * CONV_CH

    def body(duc_ref, dq_ref, dk_ref, dv_ref, w_ref, x_ref, g_ref, dh1_ref, dx_ref, dg_ref):
        @pl.when(pl.program_id(0) == 0)
        def _():
            dg_ref[...] = jnp.zeros_like(dg_ref)

        da = _dot_nt(duc_ref[...], w_ref[:, :nconv])
        for n, ref in enumerate((dq_ref, dk_ref, dv_ref)):
            c0 = nconv + n * SB_WIDTH
            da = da + _dot_nt(ref[...].astype(BF16), w_ref[:, c0:c0 + SB_WIDTH])
        xf = x_ref[...]
        r = _rms_r(xf)
        dx_ref[...] = dh1_ref[...] + _rms_bwd(xf, r, g_ref[...], da)
        dg_ref[...] += jnp.sum(da * xf * r, axis=0, keepdims=True)

    row = lambda n: pl.BlockSpec((tm, n), lambda i: (i, 0))
    return pl.pallas_call(
        body, name="in_proj_bwd", grid=(S // tm,),
        out_shape=[jax.ShapeDtypeStruct((S, D_MODEL), F32), jax.ShapeDtypeStruct((1, D_MODEL), F32)],
        in_specs=[row(nconv), row(SB_WIDTH), row(SB_WIDTH), row(SB_WIDTH), _resident(w_in.shape), row(D_MODEL),
                  _const((1, D_MODEL)), row(D_MODEL)],
        out_specs=[row(D_MODEL), _const((1, D_MODEL))],
        compiler_params=_cparams(dimension_semantics=("arbitrary",)),
    )(du_conv, dq, dk, dv, w_in, x, g, dh1)


def _layer_grads(xs, target, g_pre_mix, w_in_f, conv_w_f, conv_b, conv_ln_g, conv_ln_b, attn_g, w_out_f, g_post_mix,
                 g_pre_ffn, w_gate_f, w_up_f, w_down_f, g_post_ffn):
    a, u_conv, qkv = _in_proj(xs, g_pre_mix, w_in_f)
    conv_out = _conv_fwd(u_conv, conv_w_f, conv_b, conv_ln_g, conv_ln_b)
    o, attn_out, cl = _attn_fwd(qkv, attn_g)
    y, h1, f_in = _out_proj(conv_out, attn_out, w_out_f, xs, g_post_mix, g_pre_ffn)
    gt, up, df, dh2, loss_part, d_g_post_ffn = _ffn_fwd_loss(f_in, w_gate_f, w_up_f, w_down_f, h1, target, g_post_ffn)

    dgt, dup, act, dfin = _ffn_bwd(df, gt, up, w_gate_f, w_up_f, w_down_f)
    d_w_down = _matmul_tn("grad_w_down", act, df, 512)
    d_w_gate = _matmul_tn("grad_w_gate", f_in, dgt, FF_CHUNK)
    d_w_up = _matmul_tn("grad_w_up", f_in, dup, FF_CHUNK)
    dh1, dy, dco, dao, d_g_pre_ffn, d_g_post_mix = _mix_bwd(dfin, h1, y, dh2, g_pre_ffn, g_post_mix, w_out_f)
    d_w_out = jnp.concatenate([_matmul_tn("grad_w_out_conv", conv_out, dy, D_MODEL),
                               _matmul_tn("grad_w_out_attn", attn_out, dy, D_MODEL)], axis=0)
    dq, dk, dv, d_attn_g = _attn_bwd(qkv, o, dao, cl, attn_g)
    du_conv, d_conv_w, d_conv_small = _conv_bwd(u_conv, dco, conv_w_f, conv_b, conv_ln_g, conv_ln_b)
    grad_x, d_g_pre_mix = _in_proj_bwd(du_conv, dq, dk, dv, w_in_f, xs, g_pre_mix, dh1)
    d_w_in = jnp.concatenate([_matmul_tn("grad_w_in_conv", a, du_conv, 2 * CONV_CH),
                              _matmul_tn("grad_w_in_q", a, dq, SB_WIDTH), _matmul_tn("grad_w_in_k", a, dk, SB_WIDTH),
                              _matmul_tn("grad_w_in_v", a, dv, SB_WIDTH)], axis=1)
    return (loss_part, grad_x, d_w_in, d_w_out, d_w_gate, d_w_up, d_w_down, d_conv_w, d_conv_small, d_attn_g,
            d_g_pre_mix, d_g_post_mix, d_g_pre_ffn, d_g_post_ffn)


def _cols_to_blocks(w):
    K, N = w.shape
    return jnp.transpose(w.reshape(K, N_DEV, N // N_DEV), (1, 0, 2))


def _blocks_to_cols(blocks):
    n_dev, K, n = blocks.shape
    return jnp.transpose(blocks, (1, 0, 2)).reshape(K, n_dev * n)


def kernel(x, g_pre_mix, w_in, conv_w, conv_b, conv_ln_g, conv_ln_b, attn_norm_g, w_out, g_post_mix, g_pre_ffn, w_gate, w_up, w_down, g_post_ffn, loss_target, m_g_pre_mix, m_w_in, m_conv_w, m_conv_b, m_conv_ln_g, m_conv_ln_b, m_attn_norm_g, m_w_out, m_g_post_mix, m_g_pre_ffn, m_w_gate, m_w_up, m_w_down, m_g_post_ffn, v_g_pre_mix, v_w_in, v_conv_w, v_conv_b, v_conv_ln_g, v_conv_ln_b, v_attn_norm_g, v_w_out, v_g_post_mix, v_g_pre_ffn, v_w_gate, v_w_up, v_w_down, v_g_post_ffn):
    xs = x[0]
    target = loss_target[0]
    S = xs.shape[0]
    me = 4 * lax.axis_index("x") + 2 * lax.axis_index("y") + lax.axis_index("c")
    cw_shard = conv_w.reshape(CONV_WIDTH, CONV_CH // N_DEV)
    attn_g = attn_norm_g.reshape(1, SB_WIDTH)

    gathered = _all_gather([w_in[0].astype(BF16), w_out[0].astype(BF16), w_gate[0].astype(BF16),
                            w_up[0].astype(BF16), w_down[0].astype(BF16), cw_shard])
    w_in_f = _blocks_to_cols(gathered[0])
    w_out_f = gathered[1].reshape(D_MODEL, D_MODEL)
    w_gate_f = _blocks_to_cols(gathered[2])
    w_up_f = _blocks_to_cols(gathered[3])
    w_down_f = gathered[4].reshape(D_FF, D_MODEL)
    conv_w_f = jnp.pad(_blocks_to_cols(gathered[5]), ((0, CONV_HALO - CONV_WIDTH), (0, 0)))

    (loss_part, grad_x, d_w_in, d_w_out, d_w_gate, d_w_up, d_w_down, d_conv_w, d_conv_small, d_attn_g, d_g_pre_mix,
     d_g_post_mix, d_g_pre_ffn, d_g_post_ffn) = _layer_grads(
        xs, target, g_pre_mix, w_in_f, conv_w_f, conv_b, conv_ln_g, conv_ln_b, attn_g, w_out_f, g_post_mix, g_pre_ffn,
        w_gate_f, w_up_f, w_down_f, g_post_ffn)

    big = {
        "w_in": _reduce_scatter_adamw("rs_adamw_w_in", _cols_to_blocks(d_w_in), w_in[0], m_w_in[0], v_w_in[0]),
        "w_out": _reduce_scatter_adamw("rs_adamw_w_out", d_w_out.reshape(N_DEV, D_MODEL // N_DEV, D_MODEL), w_out[0],
                                       m_w_out[0], v_w_out[0]),
        "w_gate": _reduce_scatter_adamw("rs_adamw_w_gate", _cols_to_blocks(d_w_gate), w_gate[0], m_w_gate[0], v_w_gate[0]),
        "w_up": _reduce_scatter_adamw("rs_adamw_w_up", _cols_to_blocks(d_w_up), w_up[0], m_w_up[0], v_w_up[0]),
        "w_down": _reduce_scatter_adamw("rs_adamw_w_down", d_w_down.reshape(N_DEV, D_FF // N_DEV, D_MODEL), w_down[0],
                                        m_w_down[0], v_w_down[0]),
    }

    two = lambda t: t.reshape(2, CONV_CH)
    small_g = jnp.concatenate([
        d_conv_w,
        d_conv_small[0:3],
        d_attn_g,
        two(d_g_pre_mix), two(d_g_post_mix), two(d_g_pre_ffn), two(d_g_post_ffn),
        jnp.zeros((4, CONV_CH), F32)], axis=0)
    small_g = _all_reduce_small(small_g)
    g_conv_w = lax.dynamic_slice(small_g, (0, me * (CONV_CH // N_DEV)), (CONV_WIDTH, CONV_CH // N_DEV))
    pack = lambda cb, lg, lb, ag, g1, g2, g3, g4: jnp.concatenate(
        [cb, lg, lb, ag.reshape(1, SB_WIDTH), two(g1), two(g2), two(g3), two(g4), jnp.zeros((4, CONV_CH), F32)], axis=0)
    sm_g = small_g[CONV_HALO:]
    sm_delta, sm_m, sm_v = _adamw_small(
        "adamw_small",
        pack(conv_b, conv_ln_g, conv_ln_b, attn_norm_g, g_pre_mix, g_post_mix, g_pre_ffn, g_post_ffn), sm_g,
        pack(m_conv_b, m_conv_ln_g, m_conv_ln_b, m_attn_norm_g, m_g_pre_mix, m_g_post_mix, m_g_pre_ffn, m_g_post_ffn),
        pack(v_conv_b, v_conv_ln_g, v_conv_ln_b, v_attn_norm_g, v_g_pre_mix, v_g_post_mix, v_g_pre_ffn, v_g_post_ffn))
    cw_delta, cw_m, cw_v = _adamw_small("adamw_conv_w", cw_shard, g_conv_w,
                                        m_conv_w.reshape(cw_shard.shape), v_conv_w.reshape(cw_shard.shape))

    def unpack(t):
        return {"conv_b": t[0:1], "conv_ln_g": t[1:2], "conv_ln_b": t[2:3], "attn_norm_g": t[3:4].reshape(1, SB_HEADS, SB_HEAD_DIM),
                "g_pre_mix": t[4:6].reshape(1, D_MODEL), "g_post_mix": t[6:8].reshape(1, D_MODEL),
                "g_pre_ffn": t[8:10].reshape(1, D_MODEL), "g_post_ffn": t[10:12].reshape(1, D_MODEL)}

    names = ["g_pre_mix", "w_in", "conv_w", "conv_b", "conv_ln_g", "conv_ln_b", "attn_norm_g", "w_out", "g_post_mix",
             "g_pre_ffn", "w_gate", "w_up", "w_down", "g_post_ffn"]
    kinds = []
    for idx, small in enumerate((sm_g, sm_delta, sm_m, sm_v)):
        d = unpack(small)
        d["conv_w"] = (g_conv_w, cw_delta, cw_m, cw_v)[idx].reshape(1, CONV_WIDTH, 1, CONV_CH // N_DEV)
        for n in big:
            d[n] = big[n][idx][None]
        kinds.append([d[n] for n in names])

    loss = lax.psum(loss_part[0, 0], ("x", "y", "c"))
    return (loss, grad_x[None], *kinds[0], *kinds[1], *kinds[2], *kinds[3])
```

```python
import functools
import math

import jax
import jax.numpy as jnp
from jax import lax
from jax.experimental import pallas as pl
from jax.experimental.pallas import tpu as pltpu

F32 = jnp.float32
BF16 = jnp.bfloat16
MESH = pl.DeviceIdType.MESH

N_DEV = 8
D_MODEL = 1024
CONV_CH = 512
CONV_WIDTH = 31
SB_HEADS = 8
SB_HEAD_DIM = 64
SB_WIDTH = SB_HEADS * SB_HEAD_DIM
D_FF = 2816
EPS = 1e-6
LOG2E = 1.4426950408889634
MASKED = -1e30
ADAM_LR = 0.001
ADAM_B1 = 0.9
ADAM_B2 = 0.999
ADAM_EPS = 1e-08
ADAM_WD = 0.01
ADAM_STEP = 10

LANES = 128
VMEM_LIMIT = 56 * 1024 * 1024
TOKEN_TILE = 512
FFN_TILE = 256
ATTN_STRIP = 32
ATTN_BLOCK = 256
CONV_HALO = 32
CONV_CHUNK = 64
FF_CHUNK = D_FF // 2


def _cparams(**kw):
    return pltpu.CompilerParams(vmem_limit_bytes=VMEM_LIMIT, **kw)


def _resident(shape):
    return pl.BlockSpec(shape, lambda *_: (0,) * len(shape), pipeline_mode=pl.Buffered(1))


def _const(shape):
    return pl.BlockSpec(shape, lambda *_: (0,) * len(shape))


def _rms_r(xf):
    return lax.rsqrt(jnp.mean(xf * xf, axis=-1, keepdims=True) + EPS)


def _rms_bwd(xf, r, g, dout):
    w = dout * g
    return r * (w - xf * (r * r) * jnp.mean(w * xf, axis=-1, keepdims=True))


def _dot(a, b):
    return jnp.dot(a, b, preferred_element_type=F32)


def _dot_nt(a, b):
    return lax.dot_general(a, b, (((1,), (1,)), ((), ())), preferred_element_type=F32)


def _dot_tn(a, b):
    return lax.dot_general(a, b, (((0,), (0,)), ((), ())), preferred_element_type=F32)


def _peer(x, y, c, k):
    px = 1 - x if (k >> 2) & 1 else x
    py = 1 - y if (k >> 1) & 1 else y
    pc = 1 - c if k & 1 else c
    return (px, py, pc), 4 * px + 2 * py + pc


def _all_gather(shards):
    n = len(shards)

    def body(*refs):
        ins, outs = refs[:n], refs[n:2 * n]
        send_sems, recv_sems, local_sems = refs[2 * n:]
        x, y, c = lax.axis_index("x"), lax.axis_index("y"), lax.axis_index("c")
        me = 4 * x + 2 * y + c
        copies = []
        for a in range(n):
            mine = pltpu.make_async_copy(ins[a], outs[a].at[me], local_sems.at[a])
            mine.start()
            copies.append(mine)
        for k in range(1, N_DEV):
            peer, peer_block = _peer(x, y, c, k)
            for a in range(n):
                s = a * (N_DEV - 1) + k - 1
                pltpu.make_async_remote_copy(
                    src_ref=ins[a], dst_ref=outs[a].at[me], send_sem=send_sems.at[s], recv_sem=recv_sems.at[s],
                    device_id=peer, device_id_type=MESH).start()
        for k in range(1, N_DEV):
            peer, peer_block = _peer(x, y, c, k)
            for a in range(n):
                s = a * (N_DEV - 1) + k - 1
                arrived = pltpu.make_async_remote_copy(
                    src_ref=ins[a], dst_ref=outs[a].at[peer_block], send_sem=send_sems.at[s],
                    recv_sem=recv_sems.at[s], device_id=peer, device_id_type=MESH)
                arrived.wait_send()
                arrived.wait_recv()
        for mine in copies:
            mine.wait()

    any_spec = pl.BlockSpec(memory_space=pl.ANY)
    return pl.pallas_call(
        body, name="all_gather_weights",
        out_shape=[jax.ShapeDtypeStruct((N_DEV,) + s.shape, s.dtype) for s in shards],
        in_specs=[any_spec] * n, out_specs=[any_spec] * n,
        scratch_shapes=[pltpu.SemaphoreType.DMA((n * (N_DEV - 1),)), pltpu.SemaphoreType.DMA((n * (N_DEV - 1),)),
                        pltpu.SemaphoreType.DMA((n,))],
        compiler_params=pltpu.CompilerParams(has_side_effects=True),
    )(*shards)


def _adamw(w, g, m, v):
    m = ADAM_B1 * m + (1.0 - ADAM_B1) * g
    v = ADAM_B2 * v + (1.0 - ADAM_B2) * (g * g)
    m_hat = m / (1.0 - ADAM_B1 ** ADAM_STEP)
    v_hat = v / (1.0 - ADAM_B2 ** ADAM_STEP)
    delta = -ADAM_LR * (m_hat / (jnp.sqrt(v_hat) + ADAM_EPS) + ADAM_WD * w)
    return delta, m, v


def _exchange_and_sum(src_block, recv_ref, send_sems, recv_sems, local_sem):
    x, y, c = lax.axis_index("x"), lax.axis_index("y"), lax.axis_index("c")
    me = 4 * x + 2 * y + c
    mine = pltpu.make_async_copy(src_block(me), recv_ref.at[me], local_sem)
    mine.start()
    for k in range(1, N_DEV):
        peer, peer_block = _peer(x, y, c, k)
        pltpu.make_async_remote_copy(
            src_ref=src_block(peer_block), dst_ref=recv_ref.at[me], send_sem=send_sems.at[k - 1],
            recv_sem=recv_sems.at[k - 1], device_id=peer, device_id_type=MESH).start()
    for k in range(1, N_DEV):
        peer, peer_block = _peer(x, y, c, k)
        arrived = pltpu.make_async_remote_copy(
            src_ref=src_block(peer_block), dst_ref=recv_ref.at[peer_block], send_sem=send_sems.at[k - 1],
            recv_sem=recv_sems.at[k - 1], device_id=peer, device_id_type=MESH)
        arrived.wait_send()
        arrived.wait_recv()
    mine.wait()


HBM_SPEC = pl.BlockSpec(memory_space=pltpu.HBM)
SEM_SPEC = pl.BlockSpec(memory_space=pltpu.SEMAPHORE)
DATAFLOW = pltpu.SideEffectType.DATAFLOW_SIDE_EFFECTING


def _exchange_copies(srcs, lands, send_sems, recv_sems, scatter, wait):
    x, y, c = lax.axis_index("x"), lax.axis_index("y"), lax.axis_index("c")
    me = 4 * x + 2 * y + c
    for k in range(1, N_DEV):
        peer, peer_block = _peer(x, y, c, k)
        for a in range(len(srcs)):
            s = a * (N_DEV - 1) + k - 1
            src = srcs[a].at[peer_block] if scatter else srcs[a]
            copy = pltpu.make_async_remote_copy(
                src_ref=src, dst_ref=lands[a].at[peer_block if wait else me], send_sem=send_sems.at[s],
                recv_sem=recv_sems.at[s], device_id=peer, device_id_type=MESH)
            if wait:
                copy.wait_send()
                copy.wait_recv()
            else:
                copy.start()


def _exchange_start(name, arrays, scatter):
    n = len(arrays)
    land_shapes = [a.shape if scatter else (N_DEV,) + a.shape for a in arrays]

    def body(*refs):
        _exchange_copies(refs[:n], refs[n:2 * n], refs[2 * n], refs[2 * n + 1], scatter, wait=False)
        refs[-1][...] = jnp.zeros_like(refs[-1])

    sems = pltpu.SemaphoreType.DMA((n * (N_DEV - 1),))
    hbm = lambda t: pltpu.with_memory_space_constraint(t, pltpu.HBM)
    return pl.pallas_call(
        body, name=name,
        out_shape=(sems, sems, *[pltpu.HBM(a.shape, a.dtype) for a in arrays],
                   *[pltpu.HBM(ls, a.dtype) for ls, a in zip(land_shapes, arrays)], jax.ShapeDtypeStruct((8, LANES), F32)),
        in_specs=[HBM_SPEC] * (2 * n),
        out_specs=(SEM_SPEC, SEM_SPEC, *[HBM_SPEC] * (2 * n), pl.BlockSpec(memory_space=pltpu.VMEM)),
        input_output_aliases={a: 2 + a for a in range(2 * n)},
        compiler_params=pltpu.CompilerParams(has_side_effects=DATAFLOW),
    )(*[hbm(a) for a in arrays], *[hbm(lax.empty(ls, a.dtype)) for ls, a in zip(land_shapes, arrays)])


def _exchange_wait(name, started, scatter, after):
    n = (len(started) - 3) // 2
    send_sems, recv_sems = started[0], started[1]
    arrays, lands = started[2:2 + n], started[2 + n:2 + 2 * n]

    def body(*refs):
        _exchange_copies(refs[:n], refs[n:2 * n], refs[2 * n], refs[2 * n + 1], scatter, wait=True)

    return pl.pallas_call(
        body, name=name,
        out_shape=[pltpu.HBM(t.shape, t.dtype) for t in (*arrays, *lands)],
        in_specs=[HBM_SPEC] * (2 * n) + [SEM_SPEC, SEM_SPEC, pl.BlockSpec(memory_space=pl.ANY)],
        out_specs=[HBM_SPEC] * (2 * n),
        input_output_aliases={a: a for a in range(2 * n)},
        compiler_params=pltpu.CompilerParams(has_side_effects=DATAFLOW),
    )(*arrays, *lands, send_sems, recv_sems, after)[n:]


def _sum_adamw(name, land, own, w, m, v):
    _, M, N = land.shape
    rows = math.gcd(M, 128)

    def body(land_ref, own_ref, w_ref, m_ref, v_ref, grad_ref, delta_ref, nm_ref, nv_ref):
        x, y, c = lax.axis_index("x"), lax.axis_index("y"), lax.axis_index("c")
        g = own_ref[...]
        for k in range(1, N_DEV):
            g = g + land_ref[_peer(x, y, c, k)[1]]
        delta, nm, nv = _adamw(w_ref[...], g, m_ref[...], v_ref[...])
        grad_ref[...] = g
        delta_ref[...] = delta
        nm_ref[...] = nm
        nv_ref[...] = nv

    row = pl.BlockSpec((rows, N), lambda i: (i, 0))
    return pl.pallas_call(
        body, name=name, grid=(M // rows,), out_shape=[jax.ShapeDtypeStruct((M, N), F32)] * 4,
        in_specs=[pl.BlockSpec((N_DEV, rows, N), lambda i: (0, i, 0)), row, row, row, row], out_specs=[row] * 4,
        compiler_params=_cparams(),
    )(land, own, w, m, v)


def _all_reduce_small(g):
    R, C = g.shape

    def body(g_ref, out_ref, recv_ref, send_sems, recv_sems, local_sem):
        _exchange_and_sum(lambda b: g_ref, recv_ref, send_sems, recv_sems, local_sem)
        total = recv_ref[0]
        for b in range(1, N_DEV):
            total = total + recv_ref[b]
        out_ref[...] = total

    vmem = pl.BlockSpec(memory_space=pltpu.VMEM)
    return pl.pallas_call(
        body, name="all_reduce_small_grads", out_shape=jax.ShapeDtypeStruct((R, C), F32),
        in_specs=[vmem], out_specs=vmem,
        scratch_shapes=[pltpu.VMEM((N_DEV, R, C), F32), pltpu.SemaphoreType.DMA((N_DEV - 1,)),
                        pltpu.SemaphoreType.DMA((N_DEV - 1,)), pltpu.SemaphoreType.DMA(())],
        compiler_params=_cparams(has_side_effects=True),
    )(g)


def _adamw_small(name, w, g, m, v):
    def body(w_ref, g_ref, m_ref, v_ref, delta_ref, nm_ref, nv_ref):
        delta, nm, nv = _adamw(w_ref[...], g_ref[...], m_ref[...], v_ref[...])
        delta_ref[...] = delta
        nm_ref[...] = nm
        nv_ref[...] = nv

    vmem = pl.BlockSpec(memory_space=pltpu.VMEM)
    return pl.pallas_call(body, name=name, out_shape=[jax.ShapeDtypeStruct(w.shape, F32)] * 3,
                          in_specs=[vmem] * 4, out_specs=[vmem] * 3)(w, g, m, v)


def _in_proj(x, g, w_in):
    S = x.shape[0]
    tm = min(TOKEN_TILE, S)
    nconv = 2 * CONV_CH

    def body(x_ref, g_ref, w_ref, a_ref, uc_ref, qkv_ref):
        xf = x_ref[...]
        a = (xf * _rms_r(xf) * g_ref[...]).astype(BF16)
        a_ref[...] = a
        uc_ref[...] = _dot(a, w_ref[:, :nconv])
        qkv_ref[:, :SB_WIDTH] = (_dot(a, w_ref[:, nconv:nconv + SB_WIDTH]) * (1.0 / math.sqrt(SB_HEAD_DIM))).astype(BF16)
        qkv_ref[:, SB_WIDTH:] = _dot(a, w_ref[:, nconv + SB_WIDTH:]).astype(BF16)

    row = lambda n: pl.BlockSpec((tm, n), lambda i: (i, 0))
    return pl.pallas_call(
        body, name="in_proj", grid=(S // tm,),
        out_shape=[jax.ShapeDtypeStruct((S, D_MODEL), BF16), jax.ShapeDtypeStruct((S, nconv), F32),
                   jax.ShapeDtypeStruct((S, 3 * SB_WIDTH), BF16)],
        in_specs=[row(D_MODEL), _const((1, D_MODEL)), _resident(w_in.shape)],
        out_specs=[row(D_MODEL), row(nconv), row(3 * SB_WIDTH)],
        compiler_params=_cparams(),
    )(x, g, w_in)


def _glu(u):
    val, gate = u[:, :CONV_CH], u[:, CONV_CH:]
    sg = jax.nn.sigmoid(gate)
    return val, sg, val * sg


def _conv_rows(glu_ext, cw_ref, r0, rows):
    base = r0 + CONV_HALO - (CONV_WIDTH - 1)
    acc = cw_ref[0:1, :] * glu_ext[base:base + rows, :]
    for w in range(1, CONV_WIDTH):
        acc = acc + cw_ref[w:w + 1, :] * glu_ext[base + w:base + w + rows, :]
    return acc


def _conv_fwd(u_conv, conv_w, conv_b, ln_g, ln_b):
    S = u_conv.shape[0]
    tc = min(TOKEN_TILE, S)

    def body(u_ref, cw_ref, cb_ref, lg_ref, lb_ref, out_ref, glu_ext):
        i = pl.program_id(0)

        @pl.when(i == 0)
        def _():
            glu_ext[0:CONV_HALO, :] = jnp.zeros((CONV_HALO, CONV_CH), F32)

        @pl.when(i > 0)
        def _():
            glu_ext[0:CONV_HALO, :] = glu_ext[tc:tc + CONV_HALO, :]

        glu_ext[CONV_HALO:, :] = _glu(u_ref[...])[2]
        for r0 in range(0, tc, CONV_CHUNK):
            y = _conv_rows(glu_ext, cw_ref, r0, CONV_CHUNK) + cb_ref[...]
            mu = jnp.mean(y, axis=-1, keepdims=True)
            yc = y - mu
            yn = yc * lax.rsqrt(jnp.mean(yc * yc, axis=-1, keepdims=True) + EPS)
            yl = yn * lg_ref[...] + lb_ref[...]
            out_ref[r0:r0 + CONV_CHUNK, :] = (yl * jax.nn.sigmoid(yl)).astype(BF16)

    return pl.pallas_call(
        body, name="conv_fwd", grid=(S // tc,),
        out_shape=jax.ShapeDtypeStruct((S, CONV_CH), BF16),
        in_specs=[pl.BlockSpec((tc, 2 * CONV_CH), lambda i: (i, 0)), _const((CONV_HALO, CONV_CH)),
                  _const((1, CONV_CH)), _const((1, CONV_CH)), _const((1, CONV_CH))],
        out_specs=pl.BlockSpec((tc, CONV_CH), lambda i: (i, 0)),
        scratch_shapes=[pltpu.VMEM((tc + CONV_HALO, CONV_CH), F32)],
        compiler_params=_cparams(dimension_semantics=("arbitrary",)),
    )(u_conv, conv_w, conv_b, ln_g, ln_b)


def _head_masks():
    lane = lax.broadcasted_iota(jnp.int32, (1, LANES), 1)
    return lane < SB_HEAD_DIM


def _split_heads(t, first):
    z = jnp.zeros_like(t)
    return jnp.where(first, t, z), jnp.where(first, z, t)


def _head_sum(t, first):
    a = jnp.sum(jnp.where(first, t, 0.0), axis=-1, keepdims=True)
    b = jnp.sum(jnp.where(first, 0.0, t), axis=-1, keepdims=True)
    return a, b


def _log2_sigmoids(z):
    z2 = z * LOG2E
    lb = jnp.minimum(z2, 0.0) - jnp.log2(1.0 + jnp.exp2(-jnp.abs(z2)))
    return lb, lb - z2


def _split_bf16(t):
    hi = t.astype(BF16)
    return hi, (t - hi.astype(F32)).astype(BF16)


def _block_masks(n):
    row = lax.broadcasted_iota(jnp.int32, (n, n), 0)
    col = lax.broadcasted_iota(jnp.int32, (n, n), 1)
    return row > col, row < col


def _attn_fwd(qkv, g_attn):
    S = qkv.shape[0]
    Q = min(ATTN_BLOCK, S)
    nq = S // Q
    assert nq <= LANES
    npair = SB_WIDTH // LANES

    def body(q_ref, k_ref, v_ref, g_ref, o_ref, ao_ref, cl_ref, z_buf, hi_buf, lo_buf, z2_buf, a_buf, c_buf):
        i = pl.program_id(1)
        first = _head_masks()
        qh = _split_heads(q_ref[...], first)
        row = lax.broadcasted_iota(jnp.int32, (Q, Q), 0)
        col = lax.broadcasted_iota(jnp.int32, (Q, Q), 1)
        tri = (row >= col).astype(BF16)
        lane = lax.broadcasted_iota(jnp.int32, (1, LANES), 1)
        heads = range(2)
        strips = [slice(r0, r0 + ATTN_STRIP) for r0 in range(0, Q, ATTN_STRIP)]
        rows = lambda j: pl.ds(pl.multiple_of(j * Q, Q), Q)

        def scores(j, slot):
            kb = k_ref[rows(j), :]
            for h in heads:
                z_buf[slot, h] = _dot_nt(qh[h], kb)

        def logs(slot, diag):
            for h in heads:
                for r in strips:
                    z2 = z_buf[slot, h, r, :] * LOG2E
                    l = (jnp.minimum(z2, 0.0) - jnp.log2(1.0 + jnp.exp2(-jnp.abs(z2)))) - z2
                    if diag:
                        keep = col[r] < row[r]
                        l = jnp.where(keep, l, 0.0)
                        z2 = jnp.where(keep, z2, MASKED)
                    hi, lo = _split_bf16(l)
                    hi_buf[slot, h, r, :] = hi
                    lo_buf[slot, h, r, :] = lo
                    z2_buf[slot, h, r, :] = z2

        def sums(slot):
            return tuple(_dot(hi_buf[slot, h], tri) + _dot(lo_buf[slot, h], tri) for h in heads)

        def weights(j, slot, sm):
            for h in heads:
                before = c_buf[h]
                for r in strips:
                    a_buf[slot, h, r, :] = jnp.exp2(z2_buf[slot, h, r, :] + sm[h][r] + jnp.tile(before[r], (1, Q // LANES))).astype(BF16)
                hl = slice(h * LANES, (h + 1) * LANES)
                cl_ref[:, hl] = jnp.where(lane == j, before, cl_ref[:, hl])
                c_buf[h] = before + jnp.broadcast_to(sm[h][:, 0:1], (Q, LANES))

        def values(slot, j):
            vh = _split_heads(v_ref[rows(j), :], first)
            return _dot(a_buf[slot, 0], vh[0]) + _dot(a_buf[slot, 1], vh[1])

        def iteration(m, p):
            b = i - 1 - m
            scores(jnp.maximum(b - 1, 0), 1 - p)
            sm = sums(1 - p)
            o_ref[...] += values(p, jnp.minimum(b + 2, i))
            logs(p, False)
            weights(b + 1, 1 - p, sm)

        def two_iterations(t, carry):
            iteration(2 * t, 0)
            iteration(2 * t + 1, 1)
            return carry

        o_ref[...] = jnp.zeros_like(o_ref)
        cl_ref[...] = jnp.zeros_like(cl_ref)
        c_buf[...] = jnp.zeros_like(c_buf)
        a_buf[0] = jnp.zeros((2, Q, Q), BF16)
        scores(i, 1)
        logs(1, True)

        @pl.when(i > 0)
        def _():
            scores(i - 1, 0)

        lax.fori_loop(0, i // 2, two_iterations, 0)

        @pl.when(i % 2 == 1)
        def _():
            iteration(i - 1, 0)

        last = (i + 1) % 2
        weights(0, last, sums(last))
        o = o_ref[...] + values(1 - last, jnp.minimum(1, i)) + values(last, 0)
        sa, sb = _head_sum(o * o, first)
        r = jnp.where(first, lax.rsqrt(sa * (1.0 / SB_HEAD_DIM) + EPS), lax.rsqrt(sb * (1.0 / SB_HEAD_DIM) + EPS))
        o_ref[...] = o
        ao_ref[...] = (o * r * g_ref[...]).astype(BF16)

    kv = lambda off: pl.BlockSpec((S, LANES), lambda p, i: (0, off + p))
    return pl.pallas_call(
        body, name="attn_fwd", grid=(npair, nq),
        out_shape=[jax.ShapeDtypeStruct((S, SB_WIDTH), F32), jax.ShapeDtypeStruct((S, SB_WIDTH), BF16),
                   jax.ShapeDtypeStruct((S, 2 * SB_WIDTH), F32)],
        in_specs=[pl.BlockSpec((Q, LANES), lambda p, i: (i, p)), kv(npair), kv(2 * npair),
                  pl.BlockSpec((1, LANES), lambda p, i: (0, p))],
        out_specs=[pl.BlockSpec((Q, LANES), lambda p, i: (i, p)), pl.BlockSpec((Q, LANES), lambda p, i: (i, p)),
                   pl.BlockSpec((Q, 2 * LANES), lambda p, i: (i, p))],
        scratch_shapes=[pltpu.VMEM((2, 2, Q, Q), F32), pltpu.VMEM((2, 2, Q, Q), BF16), pltpu.VMEM((2, 2, Q, Q), BF16),
                        pltpu.VMEM((2, 2, Q, Q), F32), pltpu.VMEM((2, 2, Q, Q), BF16), pltpu.VMEM((2, Q, LANES), F32)],
        compiler_params=_cparams(dimension_semantics=("arbitrary", "arbitrary")),
    )(qkv, qkv, qkv, g_attn)


def _out_proj(conv_out, attn_out, w_out, x, g_post_mix, g_pre_ffn):
    S = x.shape[0]
    tm = min(TOKEN_TILE, S)

    def body(co_ref, ao_ref, w_ref, x_ref, g1_ref, g2_ref, y_ref, h1_ref, fin_ref):
        y = _dot(co_ref[...], w_ref[:CONV_CH, :]) + _dot(ao_ref[...], w_ref[CONV_CH:, :])
        h1 = x_ref[...] + y * _rms_r(y) * g1_ref[...]
        y_ref[...] = y
        h1_ref[...] = h1
        fin_ref[...] = (h1 * _rms_r(h1) * g2_ref[...]).astype(BF16)

    row = lambda n: pl.BlockSpec((tm, n), lambda i: (i, 0))
    return pl.pallas_call(
        body, name="out_proj", grid=(S // tm,),
        out_shape=[jax.ShapeDtypeStruct((S, D_MODEL), F32), jax.ShapeDtypeStruct((S, D_MODEL), F32),
                   jax.ShapeDtypeStruct((S, D_MODEL), BF16)],
        in_specs=[row(CONV_CH), row(SB_WIDTH), _resident(w_out.shape), row(D_MODEL), _const((1, D_MODEL)),
                  _const((1, D_MODEL))],
        out_specs=[row(D_MODEL)] * 3,
        compiler_params=_cparams(),
    )(conv_out, attn_out, w_out, x, g_post_mix, g_pre_ffn)


def _ffn_fwd_loss(f_in, w_gate, w_up, w_down, h1, target, g_post_ffn):
    S = f_in.shape[0]
    tm = min(FFN_TILE, S)
    nt = S // tm

    def body(fin_ref, wg_ref, wu_ref, wd_ref, h1_ref, t_ref, g_ref, gt_ref, up_ref, df_ref, dh2_ref, loss_ref, dg_ref,
             sq_acc):
        i = pl.program_id(0)

        @pl.when(i == 0)
        def _():
            sq_acc[...] = jnp.zeros_like(sq_acc)
            dg_ref[...] = jnp.zeros_like(dg_ref)

        fin = fin_ref[...]
        f = jnp.zeros((tm, D_MODEL), F32)
        for c0 in range(0, D_FF, FF_CHUNK):
            cols = slice(c0, c0 + FF_CHUNK)
            gt = _dot(fin, wg_ref[:, cols])
            up = _dot(fin, wu_ref[:, cols])
            gt_ref[:, cols] = gt.astype(BF16)
            up_ref[:, cols] = up.astype(BF16)
            f = f + _dot((gt * jax.nn.sigmoid(gt) * up).astype(BF16), wd_ref[cols, :])
        r = _rms_r(f)
        g = g_ref[...]
        diff = h1_ref[...] + f * r * g - t_ref[...]
        sq_acc[...] += jnp.sum(diff * diff, axis=0, keepdims=True)
        dh2 = diff * (1.0 / D_MODEL)
        dh2_ref[...] = dh2
        dg_ref[...] += jnp.sum(dh2 * f * r, axis=0, keepdims=True)
        df_ref[...] = _rms_bwd(f, r, g, dh2).astype(BF16)

        @pl.when(i == nt - 1)
        def _():
            loss_ref[...] = jnp.broadcast_to((0.5 / D_MODEL) * jnp.sum(sq_acc[...], axis=-1, keepdims=True), (1, LANES))

    row = lambda n: pl.BlockSpec((tm, n), lambda i: (i, 0))
    return pl.pallas_call(
        body, name="ffn_fwd_loss", grid=(nt,),
        out_shape=[jax.ShapeDtypeStruct((S, D_FF), BF16), jax.ShapeDtypeStruct((S, D_FF), BF16),
                   jax.ShapeDtypeStruct((S, D_MODEL), BF16), jax.ShapeDtypeStruct((S, D_MODEL), F32),
                   jax.ShapeDtypeStruct((1, LANES), F32), jax.ShapeDtypeStruct((1, D_MODEL), F32)],
        in_specs=[row(D_MODEL), _resident(w_gate.shape), _resident(w_up.shape), _resident(w_down.shape),
                  row(D_MODEL), row(D_MODEL), _const((1, D_MODEL))],
        out_specs=[row(D_FF), row(D_FF), row(D_MODEL), row(D_MODEL), _const((1, LANES)), _const((1, D_MODEL))],
        scratch_shapes=[pltpu.VMEM((1, D_MODEL), F32)],
        compiler_params=_cparams(dimension_semantics=("arbitrary",)),
    )(f_in, w_gate, w_up, w_down, h1, target, g_post_ffn)


def _ffn_bwd(df, gt, up, w_gate, w_up, w_down):
    S = df.shape[0]
    tm = min(FFN_TILE, S)

    def body(df_ref, gt_ref, up_ref, wg_ref, wu_ref, wd_ref, dgt_ref, dup_ref, act_ref, dfin_ref):
        df = df_ref[...]
        dfin = jnp.zeros((tm, D_MODEL), F32)
        for c0 in range(0, D_FF, FF_CHUNK):
            cols = slice(c0, c0 + FF_CHUNK)
            dact = _dot_nt(df, wd_ref[cols, :])
            gt = gt_ref[:, cols].astype(F32)
            up = up_ref[:, cols].astype(F32)
            s = jax.nn.sigmoid(gt)
            silu = gt * s
            dgt = (dact * up * (s * (1.0 + gt * (1.0 - s)))).astype(BF16)
            dup = (dact * silu).astype(BF16)
            act_ref[:, cols] = (silu * up).astype(BF16)
            dgt_ref[:, cols] = dgt
            dup_ref[:, cols] = dup
            dfin = dfin + _dot_nt(dgt, wg_ref[:, cols]) + _dot_nt(dup, wu_ref[:, cols])
        dfin_ref[...] = dfin

    row = lambda n: pl.BlockSpec((tm, n), lambda i: (i, 0))
    return pl.pallas_call(
        body, name="ffn_bwd", grid=(S // tm,),
        out_shape=[jax.ShapeDtypeStruct((S, D_FF), BF16)] * 3 + [jax.ShapeDtypeStruct((S, D_MODEL), F32)],
        in_specs=[row(D_MODEL), row(D_FF), row(D_FF), _resident(w_gate.shape), _resident(w_up.shape),
                  _resident(w_down.shape)],
        out_specs=[row(D_FF)] * 3 + [row(D_MODEL)],
        compiler_params=_cparams(),
    )(df, gt, up, w_gate, w_up, w_down)


def _matmul_tn(name, x, y, tn):
    S, K = x.shape
    N = y.shape[1]
    ts = min(TOKEN_TILE, S)

    def body(x_ref, y_ref, o_ref):
        @pl.when(pl.program_id(1) == 0)
        def _():
            o_ref[...] = jnp.zeros_like(o_ref)

        o_ref[...] += _dot_tn(x_ref[...].astype(BF16), y_ref[...].astype(BF16))

    return pl.pallas_call(
        body, name=name, grid=(N // tn, S // ts),
        out_shape=jax.ShapeDtypeStruct((K, N), F32),
        in_specs=[pl.BlockSpec((ts, K), lambda n, s: (s, 0)), pl.BlockSpec((ts, tn), lambda n, s: (s, n))],
        out_specs=pl.BlockSpec((K, tn), lambda n, s: (0, n)),
        compiler_params=_cparams(dimension_semantics=("arbitrary", "arbitrary")),
    )(x, y)


def _mix_bwd(dfin, h1, y, dh2, g_pre_ffn, g_post_mix, w_out):
    S = dfin.shape[0]
    tm = min(TOKEN_TILE, S)

    def body(dfin_ref, h1_ref, y_ref, dh2_ref, g2_ref, g1_ref, w_ref, dh1_ref, dy_ref, dco_ref, dao_ref, dg2_ref, dg1_ref):
        @pl.when(pl.program_id(0) == 0)
        def _():
            dg2_ref[...] = jnp.zeros_like(dg2_ref)
            dg1_ref[...] = jnp.zeros_like(dg1_ref)

        h1, dfin = h1_ref[...], dfin_ref[...]
        r2 = _rms_r(h1)
        dh1 = dh2_ref[...] + _rms_bwd(h1, r2, g2_ref[...], dfin)
        dg2_ref[...] += jnp.sum(dfin * h1 * r2, axis=0, keepdims=True)
        y = y_ref[...]
        r1 = _rms_r(y)
        dy = _rms_bwd(y, r1, g1_ref[...], dh1).astype(BF16)
        dg1_ref[...] += jnp.sum(dh1 * y * r1, axis=0, keepdims=True)
        dh1_ref[...] = dh1
        dy_ref[...] = dy
        dco_ref[...] = _dot_nt(dy, w_ref[:CONV_CH, :])
        dao_ref[...] = _dot_nt(dy, w_ref[CONV_CH:, :])

    row = lambda n: pl.BlockSpec((tm, n), lambda i: (i, 0))
    return pl.pallas_call(
        body, name="mix_bwd", grid=(S // tm,),
        out_shape=[jax.ShapeDtypeStruct((S, D_MODEL), F32), jax.ShapeDtypeStruct((S, D_MODEL), BF16),
                   jax.ShapeDtypeStruct((S, CONV_CH), F32), jax.ShapeDtypeStruct((S, SB_WIDTH), F32),
                   jax.ShapeDtypeStruct((1, D_MODEL), F32), jax.ShapeDtypeStruct((1, D_MODEL), F32)],
        in_specs=[row(D_MODEL)] * 4 + [_const((1, D_MODEL)), _const((1, D_MODEL)), _resident(w_out.shape)],
        out_specs=[row(D_MODEL), row(D_MODEL), row(CONV_CH), row(SB_WIDTH), _const((1, D_MODEL)), _const((1, D_MODEL))],
        compiler_params=_cparams(dimension_semantics=("arbitrary",)),
    )(dfin, h1, y, dh2, g_pre_ffn, g_post_mix, w_out)


def _attn_bwd(qkv, o, dao, cl, g_attn):
    S = qkv.shape[0]
    Q = min(ATTN_BLOCK, S)
    nq = S // Q
    npair = SB_WIDTH // LANES
    inv_dh = 1.0 / SB_HEAD_DIM

    def body(q_ref, k_ref, v_ref, o_ref, dao_ref, cl_ref, g_ref, dq_ref, dk_ref, dv_ref, dg_ref,
             z_buf, lb_buf, be_buf, g_buf, hi_buf, lo_buf, a_buf, ghi_buf, glo_buf, dz_buf, pg_buf, dq_acc):
        i = pl.program_id(1)

        @pl.when(i == 0)
        def _():
            dk_ref[...] = jnp.zeros_like(dk_ref)
            dv_ref[...] = jnp.zeros_like(dv_ref)
            dg_ref[...] = jnp.zeros_like(dg_ref)

        first = _head_masks()
        lane = lax.broadcasted_iota(jnp.int32, (1, LANES), 1)
        o, dao, g = o_ref[...], dao_ref[...], g_ref[...]
        sa, sb = _head_sum(o * o, first)
        r = jnp.where(first, lax.rsqrt(sa * inv_dh + EPS), lax.rsqrt(sb * inv_dh + EPS))
        w = dao * g
        wa, wb = _head_sum(w * o, first)
        do = r * (w - o * (r * r) * (jnp.where(first, wa, wb) * inv_dh))
        dg_ref[...] += jnp.sum(dao * o * r, axis=0, keepdims=True)
        doh = _split_heads(do.astype(BF16), first)

        qh = _split_heads(q_ref[...], first)
        row = lax.broadcasted_iota(jnp.int32, (Q, Q), 0)
        col = lax.broadcasted_iota(jnp.int32, (Q, Q), 1)
        tri = (row > col).astype(BF16)
        tpi = (row <= col).astype(BF16)
        heads = range(2)
        strips = [slice(r0, r0 + ATTN_STRIP) for r0 in range(0, Q, ATTN_STRIP)]
        rows = lambda j: pl.ds(pl.multiple_of(j * Q, Q), Q)
        wide = lambda t: jnp.tile(t, (1, Q // LANES))

        def scores(j, slot):
            kb = k_ref[rows(j), :]
            for h in heads:
                z_buf[slot, h] = _dot_nt(qh[h], kb)

        def logs(slot, diag):
            for h in heads:
                for r in strips:
                    z2 = z_buf[slot, h, r, :] * LOG2E
                    lb = jnp.minimum(z2, 0.0) - jnp.log2(1.0 + jnp.exp2(-jnp.abs(z2)))
                    l = lb - z2
                    if diag:
                        keep = col[r] < row[r]
                        l = jnp.where(keep, l, 0.0)
                        lb = jnp.where(keep, lb, MASKED)
                    hi, lo = _split_bf16(l)
                    hi_buf[slot, h, r, :] = hi
                    lo_buf[slot, h, r, :] = lo
                    lb_buf[slot, h, r, :] = lb

        def sums(j, slot):
            vb = v_ref[rows(j), :]
            return (tuple(_dot(hi_buf[slot, h], tri) + _dot(lo_buf[slot, h], tri) for h in heads),
                    tuple(_dot_nt(doh[h], vb) for h in heads))

        def weights(j, slot, sm, da):
            for h in heads:
                hl = slice(h * LANES, (h + 1) * LANES)
                c = jnp.sum(jnp.where(lane == j, cl_ref[:, hl], 0.0), axis=-1, keepdims=True)
                c = jnp.broadcast_to(c, (Q, LANES))
                for r in strips:
                    lb = lb_buf[slot, h, r, :]
                    a = jnp.exp2(lb + sm[h][r] + wide(c[r]))
                    g = da[h][r] * a
                    a_buf[slot, h, r, :] = a.astype(BF16)
                    be_buf[slot, h, r, :] = jnp.exp2(lb)
                    g_buf[slot, h, r, :] = g
                    hi, lo = _split_bf16(g)
                    ghi_buf[slot, h, r, :] = hi
                    glo_buf[slot, h, r, :] = lo

        def prefix(j, slot):
            dv_ref[rows(j), :] += _dot_tn(a_buf[slot, 0], doh[0]) + _dot_tn(a_buf[slot, 1], doh[1])
            return tuple(_dot(ghi_buf[slot, h], tpi) + _dot(glo_buf[slot, h], tpi) for h in heads)

        def dscores(slot, pm):
            for h in heads:
                pg = pg_buf[h]
                for r in strips:
                    dz = g_buf[slot, h, r, :] - be_buf[slot, h, r, :] * (pm[h][r] + wide(pg[r]))
                    dz_buf[slot, h, r, :] = dz.astype(BF16)
                pg_buf[h] = pg + jnp.broadcast_to(pm[h][:, Q - 1:Q], (Q, LANES))

        def grads(j, slot):
            kh = _split_heads(k_ref[rows(j), :], first)
            dq_acc[...] += _dot(dz_buf[slot, 0], kh[0]) + _dot(dz_buf[slot, 1], kh[1])
            dk_ref[rows(j), :] += _dot_tn(dz_buf[slot, 0], qh[0]) + _dot_tn(dz_buf[slot, 1], qh[1])

        def iteration(n, p, a=True, b="plain", c=True, d=True, e=True):
            if a:
                scores(n + 1, 1 - p)
            if c:
                sm, da = sums(n - 1, 1 - p)
            if d:
                pm = prefix(n - 2, p)
            if e:
                grads(n - 3, 1 - p)
            if b:
                logs(p, b == "diag")
            if c:
                weights(n - 1, 1 - p, sm, da)
            if d:
                dscores(p, pm)

        def two_iterations(t, carry):
            iteration(3 + 2 * t, 1)
            iteration(4 + 2 * t, 0)
            return carry

        pg_buf[...] = jnp.zeros_like(pg_buf)
        dq_acc[...] = jnp.zeros_like(dq_acc)

        @pl.when(i < 3)
        def _():
            def one_block(j, diag):
                scores(j, 0)
                logs(0, diag)
                weights(j, 0, *sums(j, 0))
                dscores(0, prefix(j, 0))
                grads(j, 0)

            lax.fori_loop(0, i, lambda j, carry: (one_block(j, False), carry)[1], 0)
            one_block(i, True)

        @pl.when(i >= 3)
        def _():
            scores(0, 0)
            iteration(0, 0, c=False, d=False, e=False)
            iteration(1, 1, d=False, e=False)
            iteration(2, 0, e=False)
            lax.fori_loop(0, (i - 3) // 2, two_iterations, 0)

            @pl.when(i % 2 == 0)
            def _():
                iteration(i - 1, 1)

            p = i % 2
            iteration(i, p, a=False, b="diag")
            iteration(i + 1, 1 - p, a=False, b=None)
            iteration(i + 2, p, a=False, b=None, c=False)
            iteration(i + 3, 1 - p, a=False, b=None, c=False, d=False)

        dq_ref[...] = (dq_acc[...] * (1.0 / math.sqrt(SB_HEAD_DIM))).astype(BF16)


    kv = lambda off: pl.BlockSpec((S, LANES), lambda p, i: (0, off + p))
    blk = pl.BlockSpec((Q, LANES), lambda p, i: (i, p))
    acc = pl.BlockSpec((S, LANES), lambda p, i: (0, p))
    return pl.pallas_call(
        body, name="attn_bwd", grid=(npair, nq),
        out_shape=[jax.ShapeDtypeStruct((S, SB_WIDTH), BF16), jax.ShapeDtypeStruct((S, SB_WIDTH), F32),
                   jax.ShapeDtypeStruct((S, SB_WIDTH), F32), jax.ShapeDtypeStruct((1, SB_WIDTH), F32)],
        in_specs=[blk, kv(npair), kv(2 * npair), blk, blk, pl.BlockSpec((Q, 2 * LANES), lambda p, i: (i, p)),
                  pl.BlockSpec((1, LANES), lambda p, i: (0, p))],
        out_specs=[blk, acc, acc, pl.BlockSpec((1, LANES), lambda p, i: (0, p))],
        scratch_shapes=[pltpu.VMEM((2, 2, Q, Q), F32)] * 4 + [pltpu.VMEM((2, 2, Q, Q), BF16)] * 6
        + [pltpu.VMEM((2, Q, LANES), F32), pltpu.VMEM((Q, LANES), F32)],
        compiler_params=_cparams(dimension_semantics=("arbitrary", "arbitrary")),
    )(qkv, qkv, qkv, o, dao, cl, g_attn)


def _conv_bwd(u_conv, dco, conv_w, conv_b, ln_g, ln_b):
    S = u_conv.shape[0]
    tc = min(TOKEN_TILE, S)
    nt = S // tc
    per = tc // CONV_HALO
    groups = CONV_CHUNK // 8

    def body(u_ref, halo_ref, dco_ref, cw_ref, cb_ref, lg_ref, lb_ref, du_ref, dcw_ref, dsm_ref, glu_ext, dyc_ext, sg_buf,
             dcw_acc, dsm_acc):
        i = pl.program_id(0)
        ti = nt - 1 - i

        @pl.when(i == 0)
        def _():
            dyc_ext[tc:, :] = jnp.zeros((CONV_HALO, CONV_CH), F32)
            dcw_acc[...] = jnp.zeros_like(dcw_acc)
            dsm_acc[...] = jnp.zeros_like(dsm_acc)

        @pl.when(i > 0)
        def _():
            dyc_ext[tc:, :] = dyc_ext[0:CONV_HALO, :]

        glu_ext[0:CONV_HALO, :] = jnp.where(ti > 0, _glu(halo_ref[...])[2], 0.0)
        val, sg, glu = _glu(u_ref[...])
        glu_ext[CONV_HALO:, :] = glu
        sg_buf[...] = sg

        dcb = jnp.zeros((8, CONV_CH), F32)
        dlg = jnp.zeros((8, CONV_CH), F32)
        dlb = jnp.zeros((8, CONV_CH), F32)
        fold = lambda t: jnp.sum(t.reshape(groups, 8, CONV_CH), axis=0)
        for r0 in range(0, tc, CONV_CHUNK):
            y = _conv_rows(glu_ext, cw_ref, r0, CONV_CHUNK) + cb_ref[...]
            mu = jnp.mean(y, axis=-1, keepdims=True)
            yc = y - mu
            rstd = lax.rsqrt(jnp.mean(yc * yc, axis=-1, keepdims=True) + EPS)
            yn = yc * rstd
            yl = yn * lg_ref[...] + lb_ref[...]
            s = jax.nn.sigmoid(yl)
            dyl = dco_ref[r0:r0 + CONV_CHUNK, :] * (s * (1.0 + yl * (1.0 - s)))
            dlg = dlg + fold(dyl * yn)
            dlb = dlb + fold(dyl)
            wv = dyl * lg_ref[...]
            dyc = rstd * (wv - jnp.mean(wv, axis=-1, keepdims=True) - yn * jnp.mean(wv * yn, axis=-1, keepdims=True))
            dcb = dcb + fold(dyc)
            dyc_ext[r0:r0 + CONV_CHUNK, :] = dyc
        dsm_acc[0:8, :] += dcb
        dsm_acc[8:16, :] += dlg
        dsm_acc[16:24, :] += dlb

        for r0 in range(0, tc, CONV_CHUNK):
            dyc = dyc_ext[r0:r0 + CONV_CHUNK, :]
            dglu = jnp.zeros((CONV_CHUNK, CONV_CH), F32)
            base = r0 + CONV_HALO - (CONV_WIDTH - 1)
            for w in range(CONV_WIDTH):
                back = r0 + (CONV_WIDTH - 1) - w
                dglu = dglu + cw_ref[w:w + 1, :] * dyc_ext[back:back + CONV_CHUNK, :]
                dcw_acc[8 * w:8 * w + 8, :] += fold(dyc * glu_ext[base + w:base + w + CONV_CHUNK, :])
            sg = sg_buf[r0:r0 + CONV_CHUNK, :]
            v = u_ref[r0:r0 + CONV_CHUNK, :CONV_CH]
            du_ref[r0:r0 + CONV_CHUNK, :CONV_CH] = (dglu * sg).astype(BF16)
            du_ref[r0:r0 + CONV_CHUNK, CONV_CH:] = (dglu * v * sg * (1.0 - sg)).astype(BF16)

        @pl.when(i == nt - 1)
        def _():
            for w in range(CONV_WIDTH):
                dcw_ref[w:w + 1, :] = jnp.sum(dcw_acc[8 * w:8 * w + 8, :], axis=0, keepdims=True)
            dcw_ref[CONV_WIDTH:, :] = jnp.zeros((CONV_HALO - CONV_WIDTH, CONV_CH), F32)
            for k in range(3):
                dsm_ref[k:k + 1, :] = jnp.sum(dsm_acc[8 * k:8 * k + 8, :], axis=0, keepdims=True)
            dsm_ref[3:, :] = jnp.zeros((5, CONV_CH), F32)

    return pl.pallas_call(
        body, name="conv_bwd", grid=(nt,),
        out_shape=[jax.ShapeDtypeStruct((S, 2 * CONV_CH), BF16), jax.ShapeDtypeStruct((CONV_HALO, CONV_CH), F32),
                   jax.ShapeDtypeStruct((8, CONV_CH), F32)],
        in_specs=[pl.BlockSpec((tc, 2 * CONV_CH), lambda i: (nt - 1 - i, 0)),
                  pl.BlockSpec((CONV_HALO, 2 * CONV_CH), lambda i: (jnp.maximum((nt - 1 - i) * per - 1, 0), 0)),
                  pl.BlockSpec((tc, CONV_CH), lambda i: (nt - 1 - i, 0)),
                  _const((CONV_HALO, CONV_CH)), _const((1, CONV_CH)), _const((1, CONV_CH)), _const((1, CONV_CH))],
        out_specs=[pl.BlockSpec((tc, 2 * CONV_CH), lambda i: (nt - 1 - i, 0)), _const((CONV_HALO, CONV_CH)),
                   _const((8, CONV_CH))],
        scratch_shapes=[pltpu.VMEM((tc + CONV_HALO, CONV_CH), F32), pltpu.VMEM((tc + CONV_HALO, CONV_CH), F32),
                        pltpu.VMEM((tc, CONV_CH), F32), pltpu.VMEM((8 * CONV_HALO, CONV_CH), F32),
                        pltpu.VMEM((24, CONV_CH), F32)],
        compiler_params=_cparams(dimension_semantics=("arbitrary",)),
    )(u_conv, u_conv, dco, conv_w, conv_b, ln_g, ln_b)


def _in_proj_bwd(du_conv, dq, dk, dv, w_in, x, g, dh1):
    S = x.shape[0]
    tm = min(TOKEN_TILE, S)
    nconv = 2 * CONV_CH

    def body(duc_ref, dq_ref, dk_ref, dv_ref, w_ref, x_ref, g_ref, dh1_ref, dx_ref, dg_ref):
        @pl.when(pl.program_id(0) == 0)
        def _():
            dg_ref[...] = jnp.zeros_like(dg_ref)

        da = _dot_nt(duc_ref[...], w_ref[:, :nconv])
        for n, ref in enumerate((dq_ref, dk_ref, dv_ref)):
            c0 = nconv + n * SB_WIDTH
            da = da + _dot_nt(ref[...].astype(BF16), w_ref[:, c0:c0 + SB_WIDTH])
        xf = x_ref[...]
        r = _rms_r(xf)
        dx_ref[...] = dh1_ref[...] + _rms_bwd(xf, r, g_ref[...], da)
        dg_ref[...] += jnp.sum(da * xf * r, axis=0, keepdims=True)

    row = lambda n: pl.BlockSpec((tm, n), lambda i: (i, 0))
    return pl.pallas_call(
        body, name="in_proj_bwd", grid=(S // tm,),
        out_shape=[jax.ShapeDtypeStruct((S, D_MODEL), F32), jax.ShapeDtypeStruct((1, D_MODEL), F32)],
        in_specs=[row(nconv), row(SB_WIDTH), row(SB_WIDTH), row(SB_WIDTH), _resident(w_in.shape), row(D_MODEL),
                  _const((1, D_MODEL)), row(D_MODEL)],
        out_specs=[row(D_MODEL), _const((1, D_MODEL))],
        compiler_params=_cparams(dimension_semantics=("arbitrary",)),
    )(du_conv, dq, dk, dv, w_in, x, g, dh1)


def _layer_grads(xs, target, g_pre_mix, w_in_f, conv_w_f, conv_b, conv_ln_g, conv_ln_b, attn_g, g_post_mix, g_pre_ffn,
                 g_post_ffn, late_weights, send_grads):
    a, u_conv, qkv = _in_proj(xs, g_pre_mix, w_in_f)
    conv_out = _conv_fwd(u_conv, conv_w_f, conv_b, conv_ln_g, conv_ln_b)
    o, attn_out, cl = _attn_fwd(qkv, attn_g)
    w_out_f, w_gate_f, w_up_f, w_down_f = late_weights(attn_out)
    y, h1, f_in = _out_proj(conv_out, attn_out, w_out_f, xs, g_post_mix, g_pre_ffn)
    gt, up, df, dh2, loss_part, d_g_post_ffn = _ffn_fwd_loss(f_in, w_gate_f, w_up_f, w_down_f, h1, target, g_post_ffn)

    dgt, dup, act, dfin = _ffn_bwd(df, gt, up, w_gate_f, w_up_f, w_down_f)
    d_w_down = _matmul_tn("grad_w_down", act, df, 512)
    d_w_gate = _matmul_tn("grad_w_gate", f_in, dgt, FF_CHUNK)
    d_w_up = _matmul_tn("grad_w_up", f_in, dup, FF_CHUNK)
    sent = send_grads("ffn", (d_w_gate, d_w_up, d_w_down))
    dh1, dy, dco, dao, d_g_pre_ffn, d_g_post_mix = _mix_bwd(dfin, h1, y, dh2, g_pre_ffn + sent, g_post_mix, w_out_f)
    d_w_out = jnp.concatenate([_matmul_tn("grad_w_out_conv", conv_out, dy, D_MODEL),
                               _matmul_tn("grad_w_out_attn", attn_out, dy, D_MODEL)], axis=0)
    sent = send_grads("w_out", (d_w_out,))
    dq, dk, dv, d_attn_g = _attn_bwd(qkv, o, dao, cl, attn_g + sent)
    du_conv, d_conv_w, d_conv_small = _conv_bwd(u_conv, dco, conv_w_f, conv_b, conv_ln_g, conv_ln_b)
    grad_x, d_g_pre_mix = _in_proj_bwd(du_conv, dq, dk, dv, w_in_f, xs, g_pre_mix, dh1)
    d_w_in = jnp.concatenate([_matmul_tn("grad_w_in_conv", a, du_conv, 2 * CONV_CH),
                              _matmul_tn("grad_w_in_q", a, dq, SB_WIDTH), _matmul_tn("grad_w_in_k", a, dk, SB_WIDTH),
                              _matmul_tn("grad_w_in_v", a, dv, SB_WIDTH)], axis=1)
    return (loss_part, grad_x, d_w_in, d_conv_w, d_conv_small, d_attn_g, d_g_pre_mix, d_g_post_mix, d_g_pre_ffn,
            d_g_post_ffn)


def _cols_to_blocks(w):
    K, N = w.shape
    return jnp.transpose(w.reshape(K, N_DEV, N // N_DEV), (1, 0, 2))


def _blocks_to_cols(blocks):
    n_dev, K, n = blocks.shape
    return jnp.transpose(blocks, (1, 0, 2)).reshape(K, n_dev * n)


def kernel(x, g_pre_mix, w_in, conv_w, conv_b, conv_ln_g, conv_ln_b, attn_norm_g, w_out, g_post_mix, g_pre_ffn, w_gate, w_up, w_down, g_post_ffn, loss_target, m_g_pre_mix, m_w_in, m_conv_w, m_conv_b, m_conv_ln_g, m_conv_ln_b, m_attn_norm_g, m_w_out, m_g_post_mix, m_g_pre_ffn, m_w_gate, m_w_up, m_w_down, m_g_post_ffn, v_g_pre_mix, v_w_in, v_conv_w, v_conv_b, v_conv_ln_g, v_conv_ln_b, v_attn_norm_g, v_w_out, v_g_post_mix, v_g_pre_ffn, v_w_gate, v_w_up, v_w_down, v_g_post_ffn):
    xs = x[0]
    target = loss_target[0]
    S = xs.shape[0]
    me = 4 * lax.axis_index("x") + 2 * lax.axis_index("y") + lax.axis_index("c")
    cw_shard = conv_w.reshape(CONV_WIDTH, CONV_CH // N_DEV)
    attn_g = attn_norm_g.reshape(1, SB_WIDTH)

    gathered = _all_gather([w_in[0].astype(BF16), cw_shard])
    w_in_f = _blocks_to_cols(gathered[0])
    conv_w_f = jnp.pad(_blocks_to_cols(gathered[1]), ((0, CONV_HALO - CONV_WIDTH), (0, 0)))
    late = [w_out[0].astype(BF16), w_gate[0].astype(BF16), w_up[0].astype(BF16), w_down[0].astype(BF16)]
    late_started = _exchange_start("all_gather_late_start", late, scatter=False)

    def late_weights(after):
        lands = _exchange_wait("all_gather_late_wait", late_started, False, after)
        wo, wg, wu, wd = [lax.dynamic_update_index_in_dim(land, own, me, 0) for land, own in zip(lands, late)]
        return wo.reshape(D_MODEL, D_MODEL), _blocks_to_cols(wg), _blocks_to_cols(wu), wd.reshape(D_FF, D_MODEL)

    started = {}

    def send_grads(name, grads):
        blocks = [g.reshape(N_DEV, g.shape[0] // N_DEV, g.shape[1]) if g.shape[1] == D_MODEL else _cols_to_blocks(g)
                  for g in grads]
        started[name] = (_exchange_start("reduce_scatter_" + name + "_start", blocks, scatter=True), blocks)
        return started[name][0][-1][0:1, 0:1]

    (loss_part, grad_x, d_w_in, d_conv_w, d_conv_small, d_attn_g, d_g_pre_mix, d_g_post_mix, d_g_pre_ffn,
     d_g_post_ffn) = _layer_grads(
        xs, target, g_pre_mix + late_started[-1][0:1, 0:1], w_in_f, conv_w_f, conv_b, conv_ln_g, conv_ln_b, attn_g,
        g_post_mix, g_pre_ffn, g_post_ffn, late_weights, send_grads)
    send_grads("w_in", (d_w_in,))

    def reduced(name, after, shards):
        st, blocks = started[name]
        lands = _exchange_wait("reduce_scatter_" + name + "_wait", st, True, after)
        return [_sum_adamw("adamw_" + wn, land, lax.dynamic_index_in_dim(blk, me, 0, keepdims=False), w[0], m[0], v[0])
                for land, blk, (wn, w, m, v) in zip(lands, blocks, shards)]

    two = lambda t: t.reshape(2, CONV_CH)
    small_g = jnp.concatenate([
        d_conv_w,
        d_conv_small[0:3],
        d_attn_g,
        two(d_g_pre_mix), two(d_g_post_mix), two(d_g_pre_ffn), two(d_g_post_ffn),
        jnp.zeros((4, CONV_CH), F32)], axis=0)
    small_g = _all_reduce_small(small_g)
    g_conv_w = lax.dynamic_slice(small_g, (0, me * (CONV_CH // N_DEV)), (CONV_WIDTH, CONV_CH // N_DEV))
    pack = lambda cb, lg, lb, ag, g1, g2, g3, g4: jnp.concatenate(
        [cb, lg, lb, ag.reshape(1, SB_WIDTH), two(g1), two(g2), two(g3), two(g4), jnp.zeros((4, CONV_CH), F32)], axis=0)
    sm_g = small_g[CONV_HALO:]
    sm_delta, sm_m, sm_v = _adamw_small(
        "adamw_small",
        pack(conv_b, conv_ln_g, conv_ln_b, attn_norm_g, g_pre_mix, g_post_mix, g_pre_ffn, g_post_ffn), sm_g,
        pack(m_conv_b, m_conv_ln_g, m_conv_ln_b, m_attn_norm_g, m_g_pre_mix, m_g_post_mix, m_g_pre_ffn, m_g_post_ffn),
        pack(v_conv_b, v_conv_ln_g, v_conv_ln_b, v_attn_norm_g, v_g_pre_mix, v_g_post_mix, v_g_pre_ffn, v_g_post_ffn))
    cw_delta, cw_m, cw_v = _adamw_small("adamw_conv_w", cw_shard, g_conv_w,
                                        m_conv_w.reshape(cw_shard.shape), v_conv_w.reshape(cw_shard.shape))

    ffn = reduced("ffn", grad_x, [("w_gate", w_gate, m_w_gate, v_w_gate), ("w_up", w_up, m_w_up, v_w_up),
                                  ("w_down", w_down, m_w_down, v_w_down)])
    big = {"w_gate": ffn[0], "w_up": ffn[1], "w_down": ffn[2],
           "w_out": reduced("w_out", ffn[2][0], [("w_out", w_out, m_w_out, v_w_out)])[0]}
    big["w_in"] = reduced("w_in", big["w_out"][0], [("w_in", w_in, m_w_in, v_w_in)])[0]

    def unpack(t):
        return {"conv_b": t[0:1], "conv_ln_g": t[1:2], "conv_ln_b": t[2:3], "attn_norm_g": t[3:4].reshape(1, SB_HEADS, SB_HEAD_DIM),
                "g_pre_mix": t[4:6].reshape(1, D_MODEL), "g_post_mix": t[6:8].reshape(1, D_MODEL),
                "g_pre_ffn": t[8:10].reshape(1, D_MODEL), "g_post_ffn": t[10:12].reshape(1, D_MODEL)}

    names = ["g_pre_mix", "w_in", "conv_w", "conv_b", "conv_ln_g", "conv_ln_b", "attn_norm_g", "w_out", "g_post_mix",
             "g_pre_ffn", "w_gate", "w_up", "w_down", "g_post_ffn"]
    kinds = []
    for idx, small in enumerate((sm_g, sm_delta, sm_m, sm_v)):
        d = unpack(small)
        d["conv_w"] = (g_conv_w, cw_delta, cw_m, cw_v)[idx].reshape(1, CONV_WIDTH, 1, CONV_CH // N_DEV)
        for n in big:
            d[n] = big[n][idx][None]
        kinds.append([d[n] for n in names])

    loss = lax.psum(loss_part[0, 0], ("x", "y", "c"))
    return (loss, grad_x[None], *kinds[0], *kinds[1], *kinds[2], *kinds[3])
```

```python
import functools
import math

import jax
import jax.numpy as jnp
from jax import lax
from jax.experimental import pallas as pl
from jax.experimental.pallas import tpu as pltpu

F32 = jnp.float32
BF16 = jnp.bfloat16
MESH = pl.DeviceIdType.MESH

N_DEV = 8
D_MODEL = 1024
CONV_CH = 512
CONV_WIDTH = 31
SB_HEADS = 8
SB_HEAD_DIM = 64
SB_WIDTH = SB_HEADS * SB_HEAD_DIM
D_FF = 2816
EPS = 1e-6
LOG2E = 1.4426950408889634
MASKED = -1e30
ADAM_LR = 0.001
ADAM_B1 = 0.9
ADAM_B2 = 0.999
ADAM_EPS = 1e-08
ADAM_WD = 0.01
ADAM_STEP = 10

LANES = 128
VMEM_LIMIT = 56 * 1024 * 1024
TOKEN_TILE = 512
FFN_TILE = 256
ATTN_STRIP = 32
ATTN_BLOCK = 256
CONV_HALO = 32
CONV_CHUNK = 64
FF_CHUNK = D_FF // 2


def _cparams(**kw):
    return pltpu.CompilerParams(vmem_limit_bytes=VMEM_LIMIT, **kw)


def _resident(shape):
    return pl.BlockSpec(shape, lambda *_: (0,) * len(shape), pipeline_mode=pl.Buffered(1))


def _const(shape):
    return pl.BlockSpec(shape, lambda *_: (0,) * len(shape))


def _rms_r(xf):
    return lax.rsqrt(jnp.mean(xf * xf, axis=-1, keepdims=True) + EPS)


def _rms_bwd(xf, r, g, dout):
    w = dout * g
    return r * (w - xf * (r * r) * jnp.mean(w * xf, axis=-1, keepdims=True))


def _dot(a, b):
    return jnp.dot(a, b, preferred_element_type=F32)


def _dot_nt(a, b):
    return lax.dot_general(a, b, (((1,), (1,)), ((), ())), preferred_element_type=F32)


def _dot_tn(a, b):
    return lax.dot_general(a, b, (((0,), (0,)), ((), ())), preferred_element_type=F32)


def _peer(x, y, c, k):
    px = 1 - x if (k >> 2) & 1 else x
    py = 1 - y if (k >> 1) & 1 else y
    pc = 1 - c if k & 1 else c
    return (px, py, pc), 4 * px + 2 * py + pc


def _all_gather(shards):
    n = len(shards)

    def body(*refs):
        ins, outs, done = refs[:n], refs[n:2 * n], refs[2 * n]
        send_sems, recv_sems, local_sems = refs[2 * n + 1:]
        x, y, c = lax.axis_index("x"), lax.axis_index("y"), lax.axis_index("c")
        me = 4 * x + 2 * y + c
        copies = []
        for a in range(n):
            mine = pltpu.make_async_copy(ins[a], outs[a].at[me], local_sems.at[a])
            mine.start()
            copies.append(mine)
        for k in range(1, N_DEV):
            peer, peer_block = _peer(x, y, c, k)
            for a in range(n):
                s = a * (N_DEV - 1) + k - 1
                pltpu.make_async_remote_copy(
                    src_ref=ins[a], dst_ref=outs[a].at[me], send_sem=send_sems.at[s], recv_sem=recv_sems.at[s],
                    device_id=peer, device_id_type=MESH).start()
        for k in range(1, N_DEV):
            peer, peer_block = _peer(x, y, c, k)
            for a in range(n):
                s = a * (N_DEV - 1) + k - 1
                arrived = pltpu.make_async_remote_copy(
                    src_ref=ins[a], dst_ref=outs[a].at[peer_block], send_sem=send_sems.at[s],
                    recv_sem=recv_sems.at[s], device_id=peer, device_id_type=MESH)
                arrived.wait_send()
                arrived.wait_recv()
        for mine in copies:
            mine.wait()
        done[...] = jnp.zeros_like(done)

    any_spec = pl.BlockSpec(memory_space=pl.ANY)
    return pl.pallas_call(
        body, name="all_gather_weights",
        out_shape=[jax.ShapeDtypeStruct((N_DEV,) + s.shape, s.dtype) for s in shards] + [jax.ShapeDtypeStruct((8, LANES), F32)],
        in_specs=[any_spec] * n, out_specs=[any_spec] * n + [pl.BlockSpec(memory_space=pltpu.VMEM)],
        scratch_shapes=[pltpu.SemaphoreType.DMA((n * (N_DEV - 1),)), pltpu.SemaphoreType.DMA((n * (N_DEV - 1),)),
                        pltpu.SemaphoreType.DMA((n,))],
        compiler_params=pltpu.CompilerParams(has_side_effects=True),
    )(*shards)


def _adamw(w, g, m, v):
    m = ADAM_B1 * m + (1.0 - ADAM_B1) * g
    v = ADAM_B2 * v + (1.0 - ADAM_B2) * (g * g)
    m_hat = m / (1.0 - ADAM_B1 ** ADAM_STEP)
    v_hat = v / (1.0 - ADAM_B2 ** ADAM_STEP)
    delta = -ADAM_LR * (m_hat / (jnp.sqrt(v_hat) + ADAM_EPS) + ADAM_WD * w)
    return delta, m, v


def _exchange_and_sum(src_block, recv_ref, send_sems, recv_sems, local_sem):
    x, y, c = lax.axis_index("x"), lax.axis_index("y"), lax.axis_index("c")
    me = 4 * x + 2 * y + c
    mine = pltpu.make_async_copy(src_block(me), recv_ref.at[me], local_sem)
    mine.start()
    for k in range(1, N_DEV):
        peer, peer_block = _peer(x, y, c, k)
        pltpu.make_async_remote_copy(
            src_ref=src_block(peer_block), dst_ref=recv_ref.at[me], send_sem=send_sems.at[k - 1],
            recv_sem=recv_sems.at[k - 1], device_id=peer, device_id_type=MESH).start()
    for k in range(1, N_DEV):
        peer, peer_block = _peer(x, y, c, k)
        arrived = pltpu.make_async_remote_copy(
            src_ref=src_block(peer_block), dst_ref=recv_ref.at[peer_block], send_sem=send_sems.at[k - 1],
            recv_sem=recv_sems.at[k - 1], device_id=peer, device_id_type=MESH)
        arrived.wait_send()
        arrived.wait_recv()
    mine.wait()


HBM_SPEC = pl.BlockSpec(memory_space=pltpu.HBM)
SEM_SPEC = pl.BlockSpec(memory_space=pltpu.SEMAPHORE)
DATAFLOW = pltpu.SideEffectType.DATAFLOW_SIDE_EFFECTING


def _exchange_copies(srcs, lands, send_sems, recv_sems, scatter, wait):
    x, y, c = lax.axis_index("x"), lax.axis_index("y"), lax.axis_index("c")
    me = 4 * x + 2 * y + c
    for k in range(1, N_DEV):
        peer, peer_block = _peer(x, y, c, k)
        for a in range(len(srcs)):
            s = a * (N_DEV - 1) + k - 1
            src = srcs[a].at[peer_block] if scatter else srcs[a]
            copy = pltpu.make_async_remote_copy(
                src_ref=src, dst_ref=lands[a].at[peer_block if wait else me], send_sem=send_sems.at[s],
                recv_sem=recv_sems.at[s], device_id=peer, device_id_type=MESH)
            if wait:
                copy.wait_send()
                copy.wait_recv()
            else:
                copy.start()


def _exchange_start(name, arrays, scatter):
    n = len(arrays)
    land_shapes = [a.shape if scatter else (N_DEV,) + a.shape for a in arrays]

    def body(*refs):
        _exchange_copies(refs[:n], refs[n:2 * n], refs[2 * n], refs[2 * n + 1], scatter, wait=False)
        refs[-1][...] = jnp.zeros_like(refs[-1])

    sems = pltpu.SemaphoreType.DMA((n * (N_DEV - 1),))
    hbm = lambda t: pltpu.with_memory_space_constraint(t, pltpu.HBM)
    return pl.pallas_call(
        body, name=name,
        out_shape=(sems, sems, *[pltpu.HBM(a.shape, a.dtype) for a in arrays],
                   *[pltpu.HBM(ls, a.dtype) for ls, a in zip(land_shapes, arrays)], jax.ShapeDtypeStruct((8, LANES), F32)),
        in_specs=[HBM_SPEC] * (2 * n),
        out_specs=(SEM_SPEC, SEM_SPEC, *[HBM_SPEC] * (2 * n), pl.BlockSpec(memory_space=pltpu.VMEM)),
        input_output_aliases={a: 2 + a for a in range(2 * n)},
        compiler_params=pltpu.CompilerParams(has_side_effects=DATAFLOW),
    )(*[hbm(a) for a in arrays], *[hbm(lax.empty(ls, a.dtype)) for ls, a in zip(land_shapes, arrays)])


def _exchange_wait(name, started, scatter, after):
    n = (len(started) - 3) // 2
    send_sems, recv_sems = started[0], started[1]
    arrays, lands = started[2:2 + n], started[2 + n:2 + 2 * n]

    def body(*refs):
        _exchange_copies(refs[:n], refs[n:2 * n], refs[2 * n], refs[2 * n + 1], scatter, wait=True)

    return pl.pallas_call(
        body, name=name,
        out_shape=[pltpu.HBM(t.shape, t.dtype) for t in (*arrays, *lands)],
        in_specs=[HBM_SPEC] * (2 * n) + [SEM_SPEC, SEM_SPEC, pl.BlockSpec(memory_space=pl.ANY)],
        out_specs=[HBM_SPEC] * (2 * n),
        input_output_aliases={a: a for a in range(2 * n)},
        compiler_params=pltpu.CompilerParams(has_side_effects=DATAFLOW),
    )(*arrays, *lands, send_sems, recv_sems, after)[n:]


def _sum_adamw(name, land, own, w, m, v):
    _, M, N = land.shape
    rows = math.gcd(M, 128)

    def body(land_ref, own_ref, w_ref, m_ref, v_ref, grad_ref, delta_ref, nm_ref, nv_ref):
        x, y, c = lax.axis_index("x"), lax.axis_index("y"), lax.axis_index("c")
        g = own_ref[...]
        for k in range(1, N_DEV):
            g = g + land_ref[_peer(x, y, c, k)[1]]
        delta, nm, nv = _adamw(w_ref[...], g, m_ref[...], v_ref[...])
        grad_ref[...] = g
        delta_ref[...] = delta
        nm_ref[...] = nm
        nv_ref[...] = nv

    row = pl.BlockSpec((rows, N), lambda i: (i, 0))
    return pl.pallas_call(
        body, name=name, grid=(M // rows,), out_shape=[jax.ShapeDtypeStruct((M, N), F32)] * 4,
        in_specs=[pl.BlockSpec((N_DEV, rows, N), lambda i: (0, i, 0)), row, row, row, row], out_specs=[row] * 4,
        compiler_params=_cparams(),
    )(land, own, w, m, v)


def _all_reduce_small(g):
    R, C = g.shape

    def body(g_ref, out_ref, recv_ref, send_sems, recv_sems, local_sem):
        _exchange_and_sum(lambda b: g_ref, recv_ref, send_sems, recv_sems, local_sem)
        total = recv_ref[0]
        for b in range(1, N_DEV):
            total = total + recv_ref[b]
        out_ref[...] = total

    vmem = pl.BlockSpec(memory_space=pltpu.VMEM)
    return pl.pallas_call(
        body, name="all_reduce_small_grads", out_shape=jax.ShapeDtypeStruct((R, C), F32),
        in_specs=[vmem], out_specs=vmem,
        scratch_shapes=[pltpu.VMEM((N_DEV, R, C), F32), pltpu.SemaphoreType.DMA((N_DEV - 1,)),
                        pltpu.SemaphoreType.DMA((N_DEV - 1,)), pltpu.SemaphoreType.DMA(())],
        compiler_params=_cparams(has_side_effects=True),
    )(g)


def _adamw_small(name, w, g, m, v):
    def body(w_ref, g_ref, m_ref, v_ref, delta_ref, nm_ref, nv_ref):
        delta, nm, nv = _adamw(w_ref[...], g_ref[...], m_ref[...], v_ref[...])
        delta_ref[...] = delta
        nm_ref[...] = nm
        nv_ref[...] = nv

    vmem = pl.BlockSpec(memory_space=pltpu.VMEM)
    return pl.pallas_call(body, name=name, out_shape=[jax.ShapeDtypeStruct(w.shape, F32)] * 3,
                          in_specs=[vmem] * 4, out_specs=[vmem] * 3)(w, g, m, v)


def _in_proj(x, g, w_in):
    S = x.shape[0]
    tm = min(TOKEN_TILE, S)
    nconv = 2 * CONV_CH

    def body(x_ref, g_ref, w_ref, a_ref, uc_ref, qkv_ref):
        xf = x_ref[...]
        a = (xf * _rms_r(xf) * g_ref[...]).astype(BF16)
        a_ref[...] = a
        uc_ref[...] = _dot(a, w_ref[:, :nconv])
        qkv_ref[:, :SB_WIDTH] = (_dot(a, w_ref[:, nconv:nconv + SB_WIDTH]) * (1.0 / math.sqrt(SB_HEAD_DIM))).astype(BF16)
        qkv_ref[:, SB_WIDTH:] = _dot(a, w_ref[:, nconv + SB_WIDTH:]).astype(BF16)

    row = lambda n: pl.BlockSpec((tm, n), lambda i: (i, 0))
    return pl.pallas_call(
        body, name="in_proj", grid=(S // tm,),
        out_shape=[jax.ShapeDtypeStruct((S, D_MODEL), BF16), jax.ShapeDtypeStruct((S, nconv), F32),
                   jax.ShapeDtypeStruct((S, 3 * SB_WIDTH), BF16)],
        in_specs=[row(D_MODEL), _const((1, D_MODEL)), _resident(w_in.shape)],
        out_specs=[row(D_MODEL), row(nconv), row(3 * SB_WIDTH)],
        compiler_params=_cparams(),
    )(x, g, w_in)


def _glu(u):
    val, gate = u[:, :CONV_CH], u[:, CONV_CH:]
    sg = jax.nn.sigmoid(gate)
    return val, sg, val * sg


def _conv_rows(glu_ext, cw_ref, r0, rows):
    base = r0 + CONV_HALO - (CONV_WIDTH - 1)
    acc = cw_ref[0:1, :] * glu_ext[base:base + rows, :]
    for w in range(1, CONV_WIDTH):
        acc = acc + cw_ref[w:w + 1, :] * glu_ext[base + w:base + w + rows, :]
    return acc


def _conv_fwd(u_conv, conv_w, conv_b, ln_g, ln_b):
    S = u_conv.shape[0]
    tc = min(TOKEN_TILE, S)

    def body(u_ref, cw_ref, cb_ref, lg_ref, lb_ref, out_ref, glu_ext):
        i = pl.program_id(0)

        @pl.when(i == 0)
        def _():
            glu_ext[0:CONV_HALO, :] = jnp.zeros((CONV_HALO, CONV_CH), F32)

        @pl.when(i > 0)
        def _():
            glu_ext[0:CONV_HALO, :] = glu_ext[tc:tc + CONV_HALO, :]

        glu_ext[CONV_HALO:, :] = _glu(u_ref[...])[2]
        for r0 in range(0, tc, CONV_CHUNK):
            y = _conv_rows(glu_ext, cw_ref, r0, CONV_CHUNK) + cb_ref[...]
            mu = jnp.mean(y, axis=-1, keepdims=True)
            yc = y - mu
            yn = yc * lax.rsqrt(jnp.mean(yc * yc, axis=-1, keepdims=True) + EPS)
            yl = yn * lg_ref[...] + lb_ref[...]
            out_ref[r0:r0 + CONV_CHUNK, :] = (yl * jax.nn.sigmoid(yl)).astype(BF16)

    return pl.pallas_call(
        body, name="conv_fwd", grid=(S // tc,),
        out_shape=jax.ShapeDtypeStruct((S, CONV_CH), BF16),
        in_specs=[pl.BlockSpec((tc, 2 * CONV_CH), lambda i: (i, 0)), _const((CONV_HALO, CONV_CH)),
                  _const((1, CONV_CH)), _const((1, CONV_CH)), _const((1, CONV_CH))],
        out_specs=pl.BlockSpec((tc, CONV_CH), lambda i: (i, 0)),
        scratch_shapes=[pltpu.VMEM((tc + CONV_HALO, CONV_CH), F32)],
        compiler_params=_cparams(dimension_semantics=("arbitrary",)),
    )(u_conv, conv_w, conv_b, ln_g, ln_b)


def _head_masks():
    lane = lax.broadcasted_iota(jnp.int32, (1, LANES), 1)
    return lane < SB_HEAD_DIM


def _split_heads(t, first):
    z = jnp.zeros_like(t)
    return jnp.where(first, t, z), jnp.where(first, z, t)


def _head_sum(t, first):
    a = jnp.sum(jnp.where(first, t, 0.0), axis=-1, keepdims=True)
    b = jnp.sum(jnp.where(first, 0.0, t), axis=-1, keepdims=True)
    return a, b


def _log2_sigmoids(z):
    z2 = z * LOG2E
    lb = jnp.minimum(z2, 0.0) - jnp.log2(1.0 + jnp.exp2(-jnp.abs(z2)))
    return lb, lb - z2


def _block_masks(n):
    row = lax.broadcasted_iota(jnp.int32, (n, n), 0)
    col = lax.broadcasted_iota(jnp.int32, (n, n), 1)
    return row > col, row < col


def _attn_fwd(qkv, g_attn):
    S = qkv.shape[0]
    Q = min(ATTN_BLOCK, S)
    nq = S // Q
    assert nq <= LANES
    npair = SB_WIDTH // LANES

    def body(q_ref, k_ref, v_ref, g_ref, o_ref, ao_ref, cl_ref, z_buf, l_buf, z2_buf, a_buf, c_buf):
        i = pl.program_id(1)
        first = _head_masks()
        qh = _split_heads(q_ref[...], first)
        row = lax.broadcasted_iota(jnp.int32, (Q, Q), 0)
        col = lax.broadcasted_iota(jnp.int32, (Q, Q), 1)
        tri = (row >= col).astype(BF16)
        lane = lax.broadcasted_iota(jnp.int32, (1, LANES), 1)
        heads = range(2)
        strips = [slice(r0, r0 + ATTN_STRIP) for r0 in range(0, Q, ATTN_STRIP)]
        rows = lambda j: pl.ds(pl.multiple_of(j * Q, Q), Q)

        def scores(j, slot):
            kb = k_ref[rows(j), :]
            for h in heads:
                z_buf[slot, h] = _dot_nt(qh[h], kb)

        def logs(slot, diag):
            for h in heads:
                for r in strips:
                    z2 = z_buf[slot, h, r, :] * LOG2E
                    l = (jnp.minimum(z2, 0.0) - jnp.log2(1.0 + jnp.exp2(-jnp.abs(z2)))) - z2
                    if diag:
                        keep = col[r] < row[r]
                        l = jnp.where(keep, l, 0.0)
                        z2 = jnp.where(keep, z2, MASKED)
                    l_buf[slot, h, r, :] = l.astype(BF16)
                    z2_buf[slot, h, r, :] = z2

        def sums(slot):
            return tuple(_dot(l_buf[slot, h], tri) for h in heads)

        def weights(j, slot, sm):
            for h in heads:
                before = c_buf[h]
                for r in strips:
                    a_buf[slot, h, r, :] = jnp.exp2(z2_buf[slot, h, r, :] + sm[h][r] + jnp.tile(before[r], (1, Q // LANES))).astype(BF16)
                hl = slice(h * LANES, (h + 1) * LANES)
                cl_ref[:, hl] = jnp.where(lane == j, before, cl_ref[:, hl])
                c_buf[h] = before + jnp.broadcast_to(sm[h][:, 0:1], (Q, LANES))

        def values(slot, j):
            vh = _split_heads(v_ref[rows(j), :], first)
            return _dot(a_buf[slot, 0], vh[0]) + _dot(a_buf[slot, 1], vh[1])

        def iteration(m, p):
            b = i - 1 - m
            scores(jnp.maximum(b - 1, 0), 1 - p)
            sm = sums(1 - p)
            o_ref[...] += values(p, jnp.minimum(b + 2, i))
            logs(p, False)
            weights(b + 1, 1 - p, sm)

        def two_iterations(t, carry):
            iteration(2 * t, 0)
            iteration(2 * t + 1, 1)
            return carry

        o_ref[...] = jnp.zeros_like(o_ref)
        cl_ref[...] = jnp.zeros_like(cl_ref)
        c_buf[...] = jnp.zeros_like(c_buf)
        a_buf[0] = jnp.zeros((2, Q, Q), BF16)
        scores(i, 1)
        logs(1, True)

        @pl.when(i > 0)
        def _():
            scores(i - 1, 0)

        lax.fori_loop(0, i // 2, two_iterations, 0)

        @pl.when(i % 2 == 1)
        def _():
            iteration(i - 1, 0)

        last = (i + 1) % 2
        weights(0, last, sums(last))
        o = o_ref[...] + values(1 - last, jnp.minimum(1, i)) + values(last, 0)
        sa, sb = _head_sum(o * o, first)
        r = jnp.where(first, lax.rsqrt(sa * (1.0 / SB_HEAD_DIM) + EPS), lax.rsqrt(sb * (1.0 / SB_HEAD_DIM) + EPS))
        o_ref[...] = o
        ao_ref[...] = (o * r * g_ref[...]).astype(BF16)

    kv = lambda off: pl.BlockSpec((S, LANES), lambda p, i: (0, off + p))
    return pl.pallas_call(
        body, name="attn_fwd", grid=(npair, nq),
        out_shape=[jax.ShapeDtypeStruct((S, SB_WIDTH), F32), jax.ShapeDtypeStruct((S, SB_WIDTH), BF16),
                   jax.ShapeDtypeStruct((S, 2 * SB_WIDTH), F32)],
        in_specs=[pl.BlockSpec((Q, LANES), lambda p, i: (i, p)), kv(npair), kv(2 * npair),
                  pl.BlockSpec((1, LANES), lambda p, i: (0, p))],
        out_specs=[pl.BlockSpec((Q, LANES), lambda p, i: (i, p)), pl.BlockSpec((Q, LANES), lambda p, i: (i, p)),
                   pl.BlockSpec((Q, 2 * LANES), lambda p, i: (i, p))],
        scratch_shapes=[pltpu.VMEM((2, 2, Q, Q), F32), pltpu.VMEM((2, 2, Q, Q), BF16), pltpu.VMEM((2, 2, Q, Q), F32),
                        pltpu.VMEM((2, 2, Q, Q), BF16), pltpu.VMEM((2, Q, LANES), F32)],
        compiler_params=_cparams(dimension_semantics=("arbitrary", "arbitrary")),
    )(qkv, qkv, qkv, g_attn)


def _out_proj(conv_out, attn_out, w_out, x, g_post_mix, g_pre_ffn):
    S = x.shape[0]
    tm = min(TOKEN_TILE, S)

    def body(co_ref, ao_ref, w_ref, x_ref, g1_ref, g2_ref, y_ref, h1_ref, fin_ref):
        y = _dot(co_ref[...], w_ref[:CONV_CH, :]) + _dot(ao_ref[...], w_ref[CONV_CH:, :])
        h1 = x_ref[...] + y * _rms_r(y) * g1_ref[...]
        y_ref[...] = y
        h1_ref[...] = h1
        fin_ref[...] = (h1 * _rms_r(h1) * g2_ref[...]).astype(BF16)

    row = lambda n: pl.BlockSpec((tm, n), lambda i: (i, 0))
    return pl.pallas_call(
        body, name="out_proj", grid=(S // tm,),
        out_shape=[jax.ShapeDtypeStruct((S, D_MODEL), F32), jax.ShapeDtypeStruct((S, D_MODEL), F32),
                   jax.ShapeDtypeStruct((S, D_MODEL), BF16)],
        in_specs=[row(CONV_CH), row(SB_WIDTH), _resident(w_out.shape), row(D_MODEL), _const((1, D_MODEL)),
                  _const((1, D_MODEL))],
        out_specs=[row(D_MODEL)] * 3,
        compiler_params=_cparams(),
    )(conv_out, attn_out, w_out, x, g_post_mix, g_pre_ffn)


def _ffn_fwd_loss(f_in, w_gate, w_up, w_down, h1, target, g_post_ffn):
    S = f_in.shape[0]
    tm = min(FFN_TILE, S)
    nt = S // tm

    def body(fin_ref, wg_ref, wu_ref, wd_ref, h1_ref, t_ref, g_ref, gt_ref, up_ref, df_ref, dh2_ref, loss_ref, dg_ref,
             sq_acc):
        i = pl.program_id(0)

        @pl.when(i == 0)
        def _():
            sq_acc[...] = jnp.zeros_like(sq_acc)
            dg_ref[...] = jnp.zeros_like(dg_ref)

        fin = fin_ref[...]
        f = jnp.zeros((tm, D_MODEL), F32)
        for c0 in range(0, D_FF, FF_CHUNK):
            cols = slice(c0, c0 + FF_CHUNK)
            gt = _dot(fin, wg_ref[:, cols])
            up = _dot(fin, wu_ref[:, cols])
            gt_ref[:, cols] = gt.astype(BF16)
            up_ref[:, cols] = up.astype(BF16)
            f = f + _dot((gt * jax.nn.sigmoid(gt) * up).astype(BF16), wd_ref[cols, :])
        r = _rms_r(f)
        g = g_ref[...]
        diff = h1_ref[...] + f * r * g - t_ref[...]
        sq_acc[...] += jnp.sum(diff * diff, axis=0, keepdims=True)
        dh2 = diff * (1.0 / D_MODEL)
        dh2_ref[...] = dh2
        dg_ref[...] += jnp.sum(dh2 * f * r, axis=0, keepdims=True)
        df_ref[...] = _rms_bwd(f, r, g, dh2).astype(BF16)

        @pl.when(i == nt - 1)
        def _():
            loss_ref[...] = jnp.broadcast_to((0.5 / D_MODEL) * jnp.sum(sq_acc[...], axis=-1, keepdims=True), (1, LANES))

    row = lambda n: pl.BlockSpec((tm, n), lambda i: (i, 0))
    return pl.pallas_call(
        body, name="ffn_fwd_loss", grid=(nt,),
        out_shape=[jax.ShapeDtypeStruct((S, D_FF), BF16), jax.ShapeDtypeStruct((S, D_FF), BF16),
                   jax.ShapeDtypeStruct((S, D_MODEL), BF16), jax.ShapeDtypeStruct((S, D_MODEL), F32),
                   jax.ShapeDtypeStruct((1, LANES), F32), jax.ShapeDtypeStruct((1, D_MODEL), F32)],
        in_specs=[row(D_MODEL), _resident(w_gate.shape), _resident(w_up.shape), _resident(w_down.shape),
                  row(D_MODEL), row(D_MODEL), _const((1, D_MODEL))],
        out_specs=[row(D_FF), row(D_FF), row(D_MODEL), row(D_MODEL), _const((1, LANES)), _const((1, D_MODEL))],
        scratch_shapes=[pltpu.VMEM((1, D_MODEL), F32)],
        compiler_params=_cparams(dimension_semantics=("arbitrary",)),
    )(f_in, w_gate, w_up, w_down, h1, target, g_post_ffn)


def _ffn_bwd(df, gt, up, w_gate, w_up, w_down):
    S = df.shape[0]
    tm = min(FFN_TILE, S)

    def body(df_ref, gt_ref, up_ref, wg_ref, wu_ref, wd_ref, dgt_ref, dup_ref, act_ref, dfin_ref):
        df = df_ref[...]
        dfin = jnp.zeros((tm, D_MODEL), F32)
        for c0 in range(0, D_FF, FF_CHUNK):
            cols = slice(c0, c0 + FF_CHUNK)
            dact = _dot_nt(df, wd_ref[cols, :])
            gt = gt_ref[:, cols].astype(F32)
            up = up_ref[:, cols].astype(F32)
            s = jax.nn.sigmoid(gt)
            silu = gt * s
            dgt = (dact * up * (s * (1.0 + gt * (1.0 - s)))).astype(BF16)
            dup = (dact * silu).astype(BF16)
            act_ref[:, cols] = (silu * up).astype(BF16)
            dgt_ref[:, cols] = dgt
            dup_ref[:, cols] = dup
            dfin = dfin + _dot_nt(dgt, wg_ref[:, cols]) + _dot_nt(dup, wu_ref[:, cols])
        dfin_ref[...] = dfin

    row = lambda n: pl.BlockSpec((tm, n), lambda i: (i, 0))
    return pl.pallas_call(
        body, name="ffn_bwd", grid=(S // tm,),
        out_shape=[jax.ShapeDtypeStruct((S, D_FF), BF16)] * 3 + [jax.ShapeDtypeStruct((S, D_MODEL), F32)],
        in_specs=[row(D_MODEL), row(D_FF), row(D_FF), _resident(w_gate.shape), _resident(w_up.shape),
                  _resident(w_down.shape)],
        out_specs=[row(D_FF)] * 3 + [row(D_MODEL)],
        compiler_params=_cparams(),
    )(df, gt, up, w_gate, w_up, w_down)


def _matmul_tn(name, x, y, tn):
    S, K = x.shape
    N = y.shape[1]
    ts = min(TOKEN_TILE, S)

    def body(x_ref, y_ref, o_ref):
        @pl.when(pl.program_id(1) == 0)
        def _():
            o_ref[...] = jnp.zeros_like(o_ref)

        o_ref[...] += _dot_tn(x_ref[...].astype(BF16), y_ref[...].astype(BF16))

    return pl.pallas_call(
        body, name=name, grid=(N // tn, S // ts),
        out_shape=jax.ShapeDtypeStruct((K, N), F32),
        in_specs=[pl.BlockSpec((ts, K), lambda n, s: (s, 0)), pl.BlockSpec((ts, tn), lambda n, s: (s, n))],
        out_specs=pl.BlockSpec((K, tn), lambda n, s: (0, n)),
        compiler_params=_cparams(dimension_semantics=("arbitrary", "arbitrary")),
    )(x, y)


def _mix_bwd(dfin, h1, y, dh2, g_pre_ffn, g_post_mix, w_out):
    S = dfin.shape[0]
    tm = min(TOKEN_TILE, S)

    def body(dfin_ref, h1_ref, y_ref, dh2_ref, g2_ref, g1_ref, w_ref, dh1_ref, dy_ref, dco_ref, dao_ref, dg2_ref, dg1_ref):
        @pl.when(pl.program_id(0) == 0)
        def _():
            dg2_ref[...] = jnp.zeros_like(dg2_ref)
            dg1_ref[...] = jnp.zeros_like(dg1_ref)

        h1, dfin = h1_ref[...], dfin_ref[...]
        r2 = _rms_r(h1)
        dh1 = dh2_ref[...] + _rms_bwd(h1, r2, g2_ref[...], dfin)
        dg2_ref[...] += jnp.sum(dfin * h1 * r2, axis=0, keepdims=True)
        y = y_ref[...]
        r1 = _rms_r(y)
        dy = _rms_bwd(y, r1, g1_ref[...], dh1).astype(BF16)
        dg1_ref[...] += jnp.sum(dh1 * y * r1, axis=0, keepdims=True)
        dh1_ref[...] = dh1
        dy_ref[...] = dy
        dco_ref[...] = _dot_nt(dy, w_ref[:CONV_CH, :])
        dao_ref[...] = _dot_nt(dy, w_ref[CONV_CH:, :])

    row = lambda n: pl.BlockSpec((tm, n), lambda i: (i, 0))
    return pl.pallas_call(
        body, name="mix_bwd", grid=(S // tm,),
        out_shape=[jax.ShapeDtypeStruct((S, D_MODEL), F32), jax.ShapeDtypeStruct((S, D_MODEL), BF16),
                   jax.ShapeDtypeStruct((S, CONV_CH), F32), jax.ShapeDtypeStruct((S, SB_WIDTH), F32),
                   jax.ShapeDtypeStruct((1, D_MODEL), F32), jax.ShapeDtypeStruct((1, D_MODEL), F32)],
        in_specs=[row(D_MODEL)] * 4 + [_const((1, D_MODEL)), _const((1, D_MODEL)), _resident(w_out.shape)],
        out_specs=[row(D_MODEL), row(D_MODEL), row(CONV_CH), row(SB_WIDTH), _const((1, D_MODEL)), _const((1, D_MODEL))],
        compiler_params=_cparams(dimension_semantics=("arbitrary",)),
    )(dfin, h1, y, dh2, g_pre_ffn, g_post_mix, w_out)


def _attn_bwd(qkv, o, dao, cl, g_attn):
    S = qkv.shape[0]
    Q = min(ATTN_BLOCK, S)
    nq = S // Q
    npair = SB_WIDTH // LANES
    inv_dh = 1.0 / SB_HEAD_DIM

    def body(q_ref, k_ref, v_ref, o_ref, dao_ref, cl_ref, g_ref, dq_ref, dk_ref, dv_ref, dg_ref,
             z_buf, lb_buf, be_buf, g_buf, l_buf, a_buf, gb_buf, dz_buf, pg_buf, dq_acc):
        i = pl.program_id(1)

        @pl.when(i == 0)
        def _():
            dk_ref[...] = jnp.zeros_like(dk_ref)
            dv_ref[...] = jnp.zeros_like(dv_ref)
            dg_ref[...] = jnp.zeros_like(dg_ref)

        first = _head_masks()
        lane = lax.broadcasted_iota(jnp.int32, (1, LANES), 1)
        o, dao, g = o_ref[...], dao_ref[...], g_ref[...]
        sa, sb = _head_sum(o * o, first)
        r = jnp.where(first, lax.rsqrt(sa * inv_dh + EPS), lax.rsqrt(sb * inv_dh + EPS))
        w = dao * g
        wa, wb = _head_sum(w * o, first)
        do = r * (w - o * (r * r) * (jnp.where(first, wa, wb) * inv_dh))
        dg_ref[...] += jnp.sum(dao * o * r, axis=0, keepdims=True)
        doh = _split_heads(do.astype(BF16), first)

        qh = _split_heads(q_ref[...], first)
        row = lax.broadcasted_iota(jnp.int32, (Q, Q), 0)
        col = lax.broadcasted_iota(jnp.int32, (Q, Q), 1)
        tri = (row > col).astype(BF16)
        tpi = (row <= col).astype(BF16)
        heads = range(2)
        strips = [slice(r0, r0 + ATTN_STRIP) for r0 in range(0, Q, ATTN_STRIP)]
        rows = lambda j: pl.ds(pl.multiple_of(j * Q, Q), Q)
        wide = lambda t: jnp.tile(t, (1, Q // LANES))

        def scores(j, slot):
            kb = k_ref[rows(j), :]
            for h in heads:
                z_buf[slot, h] = _dot_nt(qh[h], kb)

        def logs(slot, diag):
            for h in heads:
                for r in strips:
                    z2 = z_buf[slot, h, r, :] * LOG2E
                    lb = jnp.minimum(z2, 0.0) - jnp.log2(1.0 + jnp.exp2(-jnp.abs(z2)))
                    l = lb - z2
                    if diag:
                        keep = col[r] < row[r]
                        l = jnp.where(keep, l, 0.0)
                        lb = jnp.where(keep, lb, MASKED)
                    l_buf[slot, h, r, :] = l.astype(BF16)
                    lb_buf[slot, h, r, :] = lb

        def sums(j, slot):
            vb = v_ref[rows(j), :]
            return (tuple(_dot(l_buf[slot, h], tri) for h in heads),
                    tuple(_dot_nt(doh[h], vb) for h in heads))

        def weights(j, slot, sm, da):
            for h in heads:
                hl = slice(h * LANES, (h + 1) * LANES)
                c = jnp.sum(jnp.where(lane == j, cl_ref[:, hl], 0.0), axis=-1, keepdims=True)
                c = jnp.broadcast_to(c, (Q, LANES))
                for r in strips:
                    lb = lb_buf[slot, h, r, :]
                    a = jnp.exp2(lb + sm[h][r] + wide(c[r]))
                    g = da[h][r] * a
                    a_buf[slot, h, r, :] = a.astype(BF16)
                    be_buf[slot, h, r, :] = jnp.exp2(lb)
                    g_buf[slot, h, r, :] = g
                    gb_buf[slot, h, r, :] = g.astype(BF16)

        def prefix(j, slot):
            dv_ref[rows(j), :] += _dot_tn(a_buf[slot, 0], doh[0]) + _dot_tn(a_buf[slot, 1], doh[1])
            return tuple(_dot(gb_buf[slot, h], tpi) for h in heads)

        def dscores(slot, pm):
            for h in heads:
                pg = pg_buf[h]
                for r in strips:
                    dz = g_buf[slot, h, r, :] - be_buf[slot, h, r, :] * (pm[h][r] + wide(pg[r]))
                    dz_buf[slot, h, r, :] = dz.astype(BF16)
                pg_buf[h] = pg + jnp.broadcast_to(pm[h][:, Q - 1:Q], (Q, LANES))

        def grads(j, slot):
            kh = _split_heads(k_ref[rows(j), :], first)
            dq_acc[...] += _dot(dz_buf[slot, 0], kh[0]) + _dot(dz_buf[slot, 1], kh[1])
            dk_ref[rows(j), :] += _dot_tn(dz_buf[slot, 0], qh[0]) + _dot_tn(dz_buf[slot, 1], qh[1])

        def iteration(n, p, a=True, b="plain", c=True, d=True, e=True):
            if a:
                scores(n + 1, 1 - p)
            if c:
                sm, da = sums(n - 1, 1 - p)
            if d:
                pm = prefix(n - 2, p)
            if e:
                grads(n - 3, 1 - p)
            if b:
                logs(p, b == "diag")
            if c:
                weights(n - 1, 1 - p, sm, da)
            if d:
                dscores(p, pm)

        def two_iterations(t, carry):
            iteration(3 + 2 * t, 1)
            iteration(4 + 2 * t, 0)
            return carry

        pg_buf[...] = jnp.zeros_like(pg_buf)
        dq_acc[...] = jnp.zeros_like(dq_acc)

        @pl.when(i < 3)
        def _():
            def one_block(j, diag):
                scores(j, 0)
                logs(0, diag)
                weights(j, 0, *sums(j, 0))
                dscores(0, prefix(j, 0))
                grads(j, 0)

            lax.fori_loop(0, i, lambda j, carry: (one_block(j, False), carry)[1], 0)
            one_block(i, True)

        @pl.when(i >= 3)
        def _():
            scores(0, 0)
            iteration(0, 0, c=False, d=False, e=False)
            iteration(1, 1, d=False, e=False)
            iteration(2, 0, e=False)
            lax.fori_loop(0, (i - 3) // 2, two_iterations, 0)

            @pl.when(i % 2 == 0)
            def _():
                iteration(i - 1, 1)

            p = i % 2
            iteration(i, p, a=False, b="diag")
            iteration(i + 1, 1 - p, a=False, b=None)
            iteration(i + 2, p, a=False, b=None, c=False)
            iteration(i + 3, 1 - p, a=False, b=None, c=False, d=False)

        dq_ref[...] = (dq_acc[...] * (1.0 / math.sqrt(SB_HEAD_DIM))).astype(BF16)


    kv = lambda off: pl.BlockSpec((S, LANES), lambda p, i: (0, off + p))
    blk = pl.BlockSpec((Q, LANES), lambda p, i: (i, p))
    acc = pl.BlockSpec((S, LANES), lambda p, i: (0, p))
    return pl.pallas_call(
        body, name="attn_bwd", grid=(npair, nq),
        out_shape=[jax.ShapeDtypeStruct((S, SB_WIDTH), BF16), jax.ShapeDtypeStruct((S, SB_WIDTH), F32),
                   jax.ShapeDtypeStruct((S, SB_WIDTH), F32), jax.ShapeDtypeStruct((1, SB_WIDTH), F32)],
        in_specs=[blk, kv(npair), kv(2 * npair), blk, blk, pl.BlockSpec((Q, 2 * LANES), lambda p, i: (i, p)),
                  pl.BlockSpec((1, LANES), lambda p, i: (0, p))],
        out_specs=[blk, acc, acc, pl.BlockSpec((1, LANES), lambda p, i: (0, p))],
        scratch_shapes=[pltpu.VMEM((2, 2, Q, Q), F32)] * 4 + [pltpu.VMEM((2, 2, Q, Q), BF16)] * 4
        + [pltpu.VMEM((2, Q, LANES), F32), pltpu.VMEM((Q, LANES), F32)],
        compiler_params=_cparams(dimension_semantics=("arbitrary", "arbitrary")),
    )(qkv, qkv, qkv, o, dao, cl, g_attn)


def _conv_bwd(u_conv, dco, conv_w, conv_b, ln_g, ln_b):
    S = u_conv.shape[0]
    tc = min(TOKEN_TILE, S)
    nt = S // tc
    per = tc // CONV_HALO
    groups = CONV_CHUNK // 8

    def body(u_ref, halo_ref, dco_ref, cw_ref, cb_ref, lg_ref, lb_ref, du_ref, dcw_ref, dsm_ref, glu_ext, dyc_ext, sg_buf,
             dcw_acc, dsm_acc):
        i = pl.program_id(0)
        ti = nt - 1 - i

        @pl.when(i == 0)
        def _():
            dyc_ext[tc:, :] = jnp.zeros((CONV_HALO, CONV_CH), F32)
            dcw_acc[...] = jnp.zeros_like(dcw_acc)
            dsm_acc[...] = jnp.zeros_like(dsm_acc)

        @pl.when(i > 0)
        def _():
            dyc_ext[tc:, :] = dyc_ext[0:CONV_HALO, :]

        glu_ext[0:CONV_HALO, :] = jnp.where(ti > 0, _glu(halo_ref[...])[2], 0.0)
        val, sg, glu = _glu(u_ref[...])
        glu_ext[CONV_HALO:, :] = glu
        sg_buf[...] = sg

        dcb = jnp.zeros((8, CONV_CH), F32)
        dlg = jnp.zeros((8, CONV_CH), F32)
        dlb = jnp.zeros((8, CONV_CH), F32)
        fold = lambda t: jnp.sum(t.reshape(groups, 8, CONV_CH), axis=0)
        for r0 in range(0, tc, CONV_CHUNK):
            y = _conv_rows(glu_ext, cw_ref, r0, CONV_CHUNK) + cb_ref[...]
            mu = jnp.mean(y, axis=-1, keepdims=True)
            yc = y - mu
            rstd = lax.rsqrt(jnp.mean(yc * yc, axis=-1, keepdims=True) + EPS)
            yn = yc * rstd
            yl = yn * lg_ref[...] + lb_ref[...]
            s = jax.nn.sigmoid(yl)
            dyl = dco_ref[r0:r0 + CONV_CHUNK, :] * (s * (1.0 + yl * (1.0 - s)))
            dlg = dlg + fold(dyl * yn)
            dlb = dlb + fold(dyl)
            wv = dyl * lg_ref[...]
            dyc = rstd * (wv - jnp.mean(wv, axis=-1, keepdims=True) - yn * jnp.mean(wv * yn, axis=-1, keepdims=True))
            dcb = dcb + fold(dyc)
            dyc_ext[r0:r0 + CONV_CHUNK, :] = dyc
        dsm_acc[0:8, :] += dcb
        dsm_acc[8:16, :] += dlg
        dsm_acc[16:24, :] += dlb

        for r0 in range(0, tc, CONV_CHUNK):
            dyc = dyc_ext[r0:r0 + CONV_CHUNK, :]
            dglu = jnp.zeros((CONV_CHUNK, CONV_CH), F32)
            base = r0 + CONV_HALO - (CONV_WIDTH - 1)
            for w in range(CONV_WIDTH):
                back = r0 + (CONV_WIDTH - 1) - w
                dglu = dglu + cw_ref[w:w + 1, :] * dyc_ext[back:back + CONV_CHUNK, :]
                dcw_acc[8 * w:8 * w + 8, :] += fold(dyc * glu_ext[base + w:base + w + CONV_CHUNK, :])
            sg = sg_buf[r0:r0 + CONV_CHUNK, :]
            v = u_ref[r0:r0 + CONV_CHUNK, :CONV_CH]
            du_ref[r0:r0 + CONV_CHUNK, :CONV_CH] = (dglu * sg).astype(BF16)
            du_ref[r0:r0 + CONV_CHUNK, CONV_CH:] = (dglu * v * sg * (1.0 - sg)).astype(BF16)

        @pl.when(i == nt - 1)
        def _():
            for w in range(CONV_WIDTH):
                dcw_ref[w:w + 1, :] = jnp.sum(dcw_acc[8 * w:8 * w + 8, :], axis=0, keepdims=True)
            dcw_ref[CONV_WIDTH:, :] = jnp.zeros((CONV_HALO - CONV_WIDTH, CONV_CH), F32)
            for k in range(3):
                dsm_ref[k:k + 1, :] = jnp.sum(dsm_acc[8 * k:8 * k + 8, :], axis=0, keepdims=True)
            dsm_ref[3:, :] = jnp.zeros((5, CONV_CH), F32)

    return pl.pallas_call(
        body, name="conv_bwd", grid=(nt,),
        out_shape=[jax.ShapeDtypeStruct((S, 2 * CONV_CH), BF16), jax.ShapeDtypeStruct((CONV_HALO, CONV_CH), F32),
                   jax.ShapeDtypeStruct((8, CONV_CH), F32)],
        in_specs=[pl.BlockSpec((tc, 2 * CONV_CH), lambda i: (nt - 1 - i, 0)),
                  pl.BlockSpec((CONV_HALO, 2 * CONV_CH), lambda i: (jnp.maximum((nt - 1 - i) * per - 1, 0), 0)),
                  pl.BlockSpec((tc, CONV_CH), lambda i: (nt - 1 - i, 0)),
                  _const((CONV_HALO, CONV_CH)), _const((1, CONV_CH)), _const((1, CONV_CH)), _const((1, CONV_CH))],
        out_specs=[pl.BlockSpec((tc, 2 * CONV_CH), lambda i: (nt - 1 - i, 0)), _const((CONV_HALO, CONV_CH)),
                   _const((8, CONV_CH))],
        scratch_shapes=[pltpu.VMEM((tc + CONV_HALO, CONV_CH), F32), pltpu.VMEM((tc + CONV_HALO, CONV_CH), F32),
                        pltpu.VMEM((tc, CONV_CH), F32), pltpu.VMEM((8 * CONV_HALO, CONV_CH), F32),
                        pltpu.VMEM((24, CONV_CH), F32)],
        compiler_params=_cparams(dimension_semantics=("arbitrary",)),
    )(u_conv, u_conv, dco, conv_w, conv_b, ln_g, ln_b)


def _in_proj_bwd(du_conv, dq, dk, dv, w_in, x, g, dh1):
    S = x.shape[0]
    tm = min(TOKEN_TILE, S)
    nconv = 2 * CONV_CH

    def body(duc_ref, dq_ref, dk_ref, dv_ref, w_ref, x_ref, g_ref, dh1_ref, dx_ref, dg_ref):
        @pl.when(pl.program_id(0) == 0)
        def _():
            dg_ref[...] = jnp.zeros_like(dg_ref)

        da = _dot_nt(duc_ref[...], w_ref[:, :nconv])
        for n, ref in enumerate((dq_ref, dk_ref, dv_ref)):
            c0 = nconv + n * SB_WIDTH
            da = da + _dot_nt(ref[...].astype(BF16), w_ref[:, c0:c0 + SB_WIDTH])
        xf = x_ref[...]
        r = _rms_r(xf)
        dx_ref[...] = dh1_ref[...] + _rms_bwd(xf, r, g_ref[...], da)
        dg_ref[...] += jnp.sum(da * xf * r, axis=0, keepdims=True)

    row = lambda n: pl.BlockSpec((tm, n), lambda i: (i, 0))
    return pl.pallas_call(
        body, name="in_proj_bwd", grid=(S // tm,),
        out_shape=[jax.ShapeDtypeStruct((S, D_MODEL), F32), jax.ShapeDtypeStruct((1, D_MODEL), F32)],
        in_specs=[row(nconv), row(SB_WIDTH), row(SB_WIDTH), row(SB_WIDTH), _resident(w_in.shape), row(D_MODEL),
                  _const((1, D_MODEL)), row(D_MODEL)],
        out_specs=[row(D_MODEL), _const((1, D_MODEL))],
        compiler_params=_cparams(dimension_semantics=("arbitrary",)),
    )(du_conv, dq, dk, dv, w_in, x, g, dh1)


def _layer_grads(xs, target, g_pre_mix, w_in_f, conv_w_f, conv_b, conv_ln_g, conv_ln_b, attn_g, g_post_mix, g_pre_ffn,
                 g_post_ffn, late_weights, send_grads):
    a, u_conv, qkv = _in_proj(xs, g_pre_mix, w_in_f)
    conv_out = _conv_fwd(u_conv, conv_w_f, conv_b, conv_ln_g, conv_ln_b)
    o, attn_out, cl = _attn_fwd(qkv, attn_g)
    w_out_f, w_gate_f, w_up_f, w_down_f = late_weights(attn_out)
    y, h1, f_in = _out_proj(conv_out, attn_out, w_out_f, xs, g_post_mix, g_pre_ffn)
    gt, up, df, dh2, loss_part, d_g_post_ffn = _ffn_fwd_loss(f_in, w_gate_f, w_up_f, w_down_f, h1, target, g_post_ffn)

    dgt, dup, act, dfin = _ffn_bwd(df, gt, up, w_gate_f, w_up_f, w_down_f)
    d_w_down = _matmul_tn("grad_w_down", act, df, 512)
    d_w_gate = _matmul_tn("grad_w_gate", f_in, dgt, FF_CHUNK)
    d_w_up = _matmul_tn("grad_w_up", f_in, dup, FF_CHUNK)
    sent = send_grads("ffn", (d_w_gate, d_w_up, d_w_down))
    dh1, dy, dco, dao, d_g_pre_ffn, d_g_post_mix = _mix_bwd(dfin, h1, y, dh2, g_pre_ffn + sent, g_post_mix, w_out_f)
    d_w_out = jnp.concatenate([_matmul_tn("grad_w_out_conv", conv_out, dy, D_MODEL),
                               _matmul_tn("grad_w_out_attn", attn_out, dy, D_MODEL)], axis=0)
    sent = send_grads("w_out", (d_w_out,))
    dq, dk, dv, d_attn_g = _attn_bwd(qkv, o, dao, cl, attn_g + sent)
    du_conv, d_conv_w, d_conv_small = _conv_bwd(u_conv, dco, conv_w_f, conv_b, conv_ln_g, conv_ln_b)
    grad_x, d_g_pre_mix = _in_proj_bwd(du_conv, dq, dk, dv, w_in_f, xs, g_pre_mix, dh1)
    d_w_in = jnp.concatenate([_matmul_tn("grad_w_in_conv", a, du_conv, 2 * CONV_CH),
                              _matmul_tn("grad_w_in_q", a, dq, SB_WIDTH), _matmul_tn("grad_w_in_k", a, dk, SB_WIDTH),
                              _matmul_tn("grad_w_in_v", a, dv, SB_WIDTH)], axis=1)
    return (loss_part, grad_x, d_w_in, d_conv_w, d_conv_small, d_attn_g, d_g_pre_mix, d_g_post_mix, d_g_pre_ffn,
            d_g_post_ffn)


def _cols_to_blocks(w):
    K, N = w.shape
    return jnp.transpose(w.reshape(K, N_DEV, N // N_DEV), (1, 0, 2))


def _blocks_to_cols(blocks):
    n_dev, K, n = blocks.shape
    return jnp.transpose(blocks, (1, 0, 2)).reshape(K, n_dev * n)


def kernel(x, g_pre_mix, w_in, conv_w, conv_b, conv_ln_g, conv_ln_b, attn_norm_g, w_out, g_post_mix, g_pre_ffn, w_gate, w_up, w_down, g_post_ffn, loss_target, m_g_pre_mix, m_w_in, m_conv_w, m_conv_b, m_conv_ln_g, m_conv_ln_b, m_attn_norm_g, m_w_out, m_g_post_mix, m_g_pre_ffn, m_w_gate, m_w_up, m_w_down, m_g_post_ffn, v_g_pre_mix, v_w_in, v_conv_w, v_conv_b, v_conv_ln_g, v_conv_ln_b, v_attn_norm_g, v_w_out, v_g_post_mix, v_g_pre_ffn, v_w_gate, v_w_up, v_w_down, v_g_post_ffn):
    xs = x[0]
    target = loss_target[0]
    S = xs.shape[0]
    me = 4 * lax.axis_index("x") + 2 * lax.axis_index("y") + lax.axis_index("c")
    cw_shard = conv_w.reshape(CONV_WIDTH, CONV_CH // N_DEV)
    attn_g = attn_norm_g.reshape(1, SB_WIDTH)

    gathered = _all_gather([w_in[0].astype(BF16), cw_shard])
    w_in_f = _blocks_to_cols(gathered[0])
    conv_w_f = jnp.pad(_blocks_to_cols(gathered[1]), ((0, CONV_HALO - CONV_WIDTH), (0, 0)))
    gathered_zero = gathered[2][0:1, 0:1].astype(BF16)
    late = [w_out[0].astype(BF16) + gathered_zero, w_gate[0].astype(BF16), w_up[0].astype(BF16), w_down[0].astype(BF16)]
    late_started = _exchange_start("all_gather_late_start", late, scatter=False)

    def late_weights(after):
        lands = _exchange_wait("all_gather_late_wait", late_started, False, after)
        wo, wg, wu, wd = [lax.dynamic_update_index_in_dim(land, own, me, 0) for land, own in zip(lands, late)]
        return wo.reshape(D_MODEL, D_MODEL), _blocks_to_cols(wg), _blocks_to_cols(wu), wd.reshape(D_FF, D_MODEL)

    started = {}

    def send_grads(name, grads):
        blocks = [g.reshape(N_DEV, g.shape[0] // N_DEV, g.shape[1]) if g.shape[1] == D_MODEL else _cols_to_blocks(g)
                  for g in grads]
        started[name] = (_exchange_start("reduce_scatter_" + name + "_start", blocks, scatter=True), blocks)
        return started[name][0][-1][0:1, 0:1]

    (loss_part, grad_x, d_w_in, d_conv_w, d_conv_small, d_attn_g, d_g_pre_mix, d_g_post_mix, d_g_pre_ffn,
     d_g_post_ffn) = _layer_grads(
        xs, target, g_pre_mix + late_started[-1][0:1, 0:1], w_in_f, conv_w_f, conv_b, conv_ln_g, conv_ln_b, attn_g,
        g_post_mix, g_pre_ffn, g_post_ffn, late_weights, send_grads)
    sent = send_grads("w_in", (d_w_in,))

    def reduced(name, after, shards):
        st, blocks = started[name]
        lands = _exchange_wait("reduce_scatter_" + name + "_wait", st, True, after)
        return [_sum_adamw("adamw_" + wn, land, lax.dynamic_index_in_dim(blk, me, 0, keepdims=False), w[0], m[0], v[0])
                for land, blk, (wn, w, m, v) in zip(lands, blocks, shards)]

    two = lambda t: t.reshape(2, CONV_CH)
    small_g = jnp.concatenate([
        d_conv_w,
        d_conv_small[0:3],
        d_attn_g,
        two(d_g_pre_mix), two(d_g_post_mix), two(d_g_pre_ffn), two(d_g_post_ffn),
        jnp.zeros((4, CONV_CH), F32) + sent], axis=0)
    small_g = _all_reduce_small(small_g)
    g_conv_w = lax.dynamic_slice(small_g, (0, me * (CONV_CH // N_DEV)), (CONV_WIDTH, CONV_CH // N_DEV))
    pack = lambda cb, lg, lb, ag, g1, g2, g3, g4: jnp.concatenate(
        [cb, lg, lb, ag.reshape(1, SB_WIDTH), two(g1), two(g2), two(g3), two(g4), jnp.zeros((4, CONV_CH), F32)], axis=0)
    sm_g = small_g[CONV_HALO:]
    sm_delta, sm_m, sm_v = _adamw_small(
        "adamw_small",
        pack(conv_b, conv_ln_g, conv_ln_b, attn_norm_g, g_pre_mix, g_post_mix, g_pre_ffn, g_post_ffn), sm_g,
        pack(m_conv_b, m_conv_ln_g, m_conv_ln_b, m_attn_norm_g, m_g_pre_mix, m_g_post_mix, m_g_pre_ffn, m_g_post_ffn),
        pack(v_conv_b, v_conv_ln_g, v_conv_ln_b, v_attn_norm_g, v_g_pre_mix, v_g_post_mix, v_g_pre_ffn, v_g_post_ffn))
    cw_delta, cw_m, cw_v = _adamw_small("adamw_conv_w", cw_shard, g_conv_w,
                                        m_conv_w.reshape(cw_shard.shape), v_conv_w.reshape(cw_shard.shape))

    ffn = reduced("ffn", grad_x, [("w_gate", w_gate, m_w_gate, v_w_gate), ("w_up", w_up, m_w_up, v_w_up),
                                  ("w_down", w_down, m_w_down, v_w_down)])
    big = {"w_gate": ffn[0], "w_up": ffn[1], "w_down": ffn[2],
           "w_out": reduced("w_out", ffn[2][0], [("w_out", w_out, m_w_out, v_w_out)])[0]}
    big["w_in"] = reduced("w_in", big["w_out"][0], [("w_in", w_in, m_w_in, v_w_in)])[0]

    def unpack(t):
        return {"conv_b": t[0:1], "conv_ln_g": t[1:2], "conv_ln_b": t[2:3], "attn_norm_g": t[3:4].reshape(1, SB_HEADS, SB_HEAD_DIM),
                "g_pre_mix": t[4:6].reshape(1, D_MODEL), "g_post_mix": t[6:8].reshape(1, D_MODEL),
                "g_pre_ffn": t[8:10].reshape(1, D_MODEL), "g_post_ffn": t[10:12].reshape(1, D_MODEL)}

    names = ["g_pre_mix", "w_in", "conv_w", "conv_b", "conv_ln_g", "conv_ln_b", "attn_norm_g", "w_out", "g_post_mix",
             "g_pre_ffn", "w_gate", "w_up", "w_down", "g_post_ffn"]
    kinds = []
    for idx, small in enumerate((sm_g, sm_delta, sm_m, sm_v)):
        d = unpack(small)
        d["conv_w"] = (g_conv_w, cw_delta, cw_m, cw_v)[idx].reshape(1, CONV_WIDTH, 1, CONV_CH // N_DEV)
        for n in big:
            d[n] = big[n][idx][None]
        kinds.append([d[n] for n in names])

    loss = lax.psum(loss_part[0, 0], ("x", "y", "c"))
    return (loss, grad_x[None], *kinds[0], *kinds[1], *kinds[2], *kinds[3])
```

```python
import functools
import math

import jax
import jax.numpy as jnp
from jax import lax
from jax.experimental import pallas as pl
from jax.experimental.pallas import tpu as pltpu

F32 = jnp.float32
BF16 = jnp.bfloat16
MESH = pl.DeviceIdType.MESH

N_DEV = 8
D_MODEL = 1024
CONV_CH = 512
CONV_WIDTH = 31
SB_HEADS = 8
SB_HEAD_DIM = 64
SB_WIDTH = SB_HEADS * SB_HEAD_DIM
D_FF = 2816
EPS = 1e-6
LOG2E = 1.4426950408889634
MASKED = -1e30
ADAM_LR = 0.001
ADAM_B1 = 0.9
ADAM_B2 = 0.999
ADAM_EPS = 1e-08
ADAM_WD = 0.01
ADAM_STEP = 10

LANES = 128
VMEM_LIMIT = 56 * 1024 * 1024
TOKEN_TILE = 512
FFN_TILE = 256
ATTN_STRIP = 32
ATTN_BLOCK = 256
CONV_HALO = 32
CONV_CHUNK = 64
FF_CHUNK = D_FF // 2


def _cparams(**kw):
    return pltpu.CompilerParams(vmem_limit_bytes=VMEM_LIMIT, **kw)


def _resident(shape):
    return pl.BlockSpec(shape, lambda *_: (0,) * len(shape), pipeline_mode=pl.Buffered(1))


def _const(shape):
    return pl.BlockSpec(shape, lambda *_: (0,) * len(shape))


def _rms_r(xf):
    return lax.rsqrt(jnp.mean(xf * xf, axis=-1, keepdims=True) + EPS)


def _rms_bwd(xf, r, g, dout):
    w = dout * g
    return r * (w - xf * (r * r) * jnp.mean(w * xf, axis=-1, keepdims=True))


def _dot(a, b):
    return jnp.dot(a, b, preferred_element_type=F32)


def _dot_nt(a, b):
    return lax.dot_general(a, b, (((1,), (1,)), ((), ())), preferred_element_type=F32)


def _dot_tn(a, b):
    return lax.dot_general(a, b, (((0,), (0,)), ((), ())), preferred_element_type=F32)


def _peer(x, y, c, k):
    px = 1 - x if (k >> 2) & 1 else x
    py = 1 - y if (k >> 1) & 1 else y
    pc = 1 - c if k & 1 else c
    return (px, py, pc), 4 * px + 2 * py + pc


def _all_gather(shards):
    n = len(shards)

    def body(*refs):
        ins, outs, done = refs[:n], refs[n:2 * n], refs[2 * n]
        send_sems, recv_sems, local_sems = refs[2 * n + 1:]
        x, y, c = lax.axis_index("x"), lax.axis_index("y"), lax.axis_index("c")
        me = 4 * x + 2 * y + c
        copies = []
        for a in range(n):
            mine = pltpu.make_async_copy(ins[a], outs[a].at[me], local_sems.at[a])
            mine.start()
            copies.append(mine)
        for k in range(1, N_DEV):
            peer, peer_block = _peer(x, y, c, k)
            for a in range(n):
                s = a * (N_DEV - 1) + k - 1
                pltpu.make_async_remote_copy(
                    src_ref=ins[a], dst_ref=outs[a].at[me], send_sem=send_sems.at[s], recv_sem=recv_sems.at[s],
                    device_id=peer, device_id_type=MESH).start()
        for k in range(1, N_DEV):
            peer, peer_block = _peer(x, y, c, k)
            for a in range(n):
                s = a * (N_DEV - 1) + k - 1
                arrived = pltpu.make_async_remote_copy(
                    src_ref=ins[a], dst_ref=outs[a].at[peer_block], send_sem=send_sems.at[s],
                    recv_sem=recv_sems.at[s], device_id=peer, device_id_type=MESH)
                arrived.wait_send()
                arrived.wait_recv()
        for mine in copies:
            mine.wait()
        done[...] = jnp.zeros_like(done)

    any_spec = pl.BlockSpec(memory_space=pl.ANY)
    return pl.pallas_call(
        body, name="all_gather_weights",
        out_shape=[jax.ShapeDtypeStruct((N_DEV,) + s.shape, s.dtype) for s in shards] + [jax.ShapeDtypeStruct((8, LANES), F32)],
        in_specs=[any_spec] * n, out_specs=[any_spec] * n + [pl.BlockSpec(memory_space=pltpu.VMEM)],
        scratch_shapes=[pltpu.SemaphoreType.DMA((n * (N_DEV - 1),)), pltpu.SemaphoreType.DMA((n * (N_DEV - 1),)),
                        pltpu.SemaphoreType.DMA((n,))],
        compiler_params=pltpu.CompilerParams(has_side_effects=True),
    )(*shards)


def _adamw(w, g, m, v):
    m = ADAM_B1 * m + (1.0 - ADAM_B1) * g
    v = ADAM_B2 * v + (1.0 - ADAM_B2) * (g * g)
    m_hat = m / (1.0 - ADAM_B1 ** ADAM_STEP)
    v_hat = v / (1.0 - ADAM_B2 ** ADAM_STEP)
    delta = -ADAM_LR * (m_hat / (jnp.sqrt(v_hat) + ADAM_EPS) + ADAM_WD * w)
    return delta, m, v


def _exchange_and_sum(src_block, recv_ref, send_sems, recv_sems, local_sem):
    x, y, c = lax.axis_index("x"), lax.axis_index("y"), lax.axis_index("c")
    me = 4 * x + 2 * y + c
    mine = pltpu.make_async_copy(src_block(me), recv_ref.at[me], local_sem)
    mine.start()
    for k in range(1, N_DEV):
        peer, peer_block = _peer(x, y, c, k)
        pltpu.make_async_remote_copy(
            src_ref=src_block(peer_block), dst_ref=recv_ref.at[me], send_sem=send_sems.at[k - 1],
            recv_sem=recv_sems.at[k - 1], device_id=peer, device_id_type=MESH).start()
    for k in range(1, N_DEV):
        peer, peer_block = _peer(x, y, c, k)
        arrived = pltpu.make_async_remote_copy(
            src_ref=src_block(peer_block), dst_ref=recv_ref.at[peer_block], send_sem=send_sems.at[k - 1],
            recv_sem=recv_sems.at[k - 1], device_id=peer, device_id_type=MESH)
        arrived.wait_send()
        arrived.wait_recv()
    mine.wait()


HBM_SPEC = pl.BlockSpec(memory_space=pltpu.HBM)
SEM_SPEC = pl.BlockSpec(memory_space=pltpu.SEMAPHORE)
DATAFLOW = pltpu.SideEffectType.DATAFLOW_SIDE_EFFECTING


def _exchange_copies(srcs, lands, send_sems, recv_sems, scatter, wait):
    x, y, c = lax.axis_index("x"), lax.axis_index("y"), lax.axis_index("c")
    me = 4 * x + 2 * y + c
    for k in range(1, N_DEV):
        peer, peer_block = _peer(x, y, c, k)
        for a in range(len(srcs)):
            s = a * (N_DEV - 1) + k - 1
            src = srcs[a].at[peer_block] if scatter else srcs[a]
            copy = pltpu.make_async_remote_copy(
                src_ref=src, dst_ref=lands[a].at[peer_block if wait else me], send_sem=send_sems.at[s],
                recv_sem=recv_sems.at[s], device_id=peer, device_id_type=MESH)
            if wait:
                copy.wait_send()
                copy.wait_recv()
            else:
                copy.start()


def _exchange_start(name, arrays, scatter):
    n = len(arrays)
    land_shapes = [a.shape if scatter else (N_DEV,) + a.shape for a in arrays]

    def body(*refs):
        _exchange_copies(refs[:n], refs[n:2 * n], refs[2 * n], refs[2 * n + 1], scatter, wait=False)
        refs[-1][...] = jnp.zeros_like(refs[-1])

    sems = pltpu.SemaphoreType.DMA((n * (N_DEV - 1),))
    hbm = lambda t: pltpu.with_memory_space_constraint(t, pltpu.HBM)
    return pl.pallas_call(
        body, name=name,
        out_shape=(sems, sems, *[pltpu.HBM(a.shape, a.dtype) for a in arrays],
                   *[pltpu.HBM(ls, a.dtype) for ls, a in zip(land_shapes, arrays)], jax.ShapeDtypeStruct((8, LANES), F32)),
        in_specs=[HBM_SPEC] * (2 * n),
        out_specs=(SEM_SPEC, SEM_SPEC, *[HBM_SPEC] * (2 * n), pl.BlockSpec(memory_space=pltpu.VMEM)),
        input_output_aliases={a: 2 + a for a in range(2 * n)},
        compiler_params=pltpu.CompilerParams(has_side_effects=DATAFLOW),
    )(*[hbm(a) for a in arrays], *[hbm(lax.empty(ls, a.dtype)) for ls, a in zip(land_shapes, arrays)])


def _exchange_wait(name, started, scatter, after):
    n = (len(started) - 3) // 2
    send_sems, recv_sems = started[0], started[1]
    arrays, lands = started[2:2 + n], started[2 + n:2 + 2 * n]

    def body(*refs):
        _exchange_copies(refs[:n], refs[n:2 * n], refs[2 * n], refs[2 * n + 1], scatter, wait=True)

    return pl.pallas_call(
        body, name=name,
        out_shape=[pltpu.HBM(t.shape, t.dtype) for t in (*arrays, *lands)],
        in_specs=[HBM_SPEC] * (2 * n) + [SEM_SPEC, SEM_SPEC, pl.BlockSpec(memory_space=pl.ANY)],
        out_specs=[HBM_SPEC] * (2 * n),
        input_output_aliases={a: a for a in range(2 * n)},
        compiler_params=pltpu.CompilerParams(has_side_effects=DATAFLOW),
    )(*arrays, *lands, send_sems, recv_sems, after)[n:]


def _sum_adamw(name, land, own, w, m, v):
    _, M, N = land.shape
    rows = math.gcd(M, 128)

    def body(land_ref, own_ref, w_ref, m_ref, v_ref, grad_ref, delta_ref, nm_ref, nv_ref):
        x, y, c = lax.axis_index("x"), lax.axis_index("y"), lax.axis_index("c")
        g = own_ref[...]
        for k in range(1, N_DEV):
            g = g + land_ref[_peer(x, y, c, k)[1]]
        delta, nm, nv = _adamw(w_ref[...], g, m_ref[...], v_ref[...])
        grad_ref[...] = g
        delta_ref[...] = delta
        nm_ref[...] = nm
        nv_ref[...] = nv

    row = pl.BlockSpec((rows, N), lambda i: (i, 0))
    return pl.pallas_call(
        body, name=name, grid=(M // rows,), out_shape=[jax.ShapeDtypeStruct((M, N), F32)] * 4,
        in_specs=[pl.BlockSpec((N_DEV, rows, N), lambda i: (0, i, 0)), row, row, row, row], out_specs=[row] * 4,
        compiler_params=_cparams(),
    )(land, own, w, m, v)


def _all_reduce_small(g):
    R, C = g.shape

    def body(g_ref, out_ref, recv_ref, send_sems, recv_sems, local_sem):
        _exchange_and_sum(lambda b: g_ref, recv_ref, send_sems, recv_sems, local_sem)
        total = recv_ref[0]
        for b in range(1, N_DEV):
            total = total + recv_ref[b]
        out_ref[...] = total

    vmem = pl.BlockSpec(memory_space=pltpu.VMEM)
    return pl.pallas_call(
        body, name="all_reduce_small_grads", out_shape=jax.ShapeDtypeStruct((R, C), F32),
        in_specs=[vmem], out_specs=vmem,
        scratch_shapes=[pltpu.VMEM((N_DEV, R, C), F32), pltpu.SemaphoreType.DMA((N_DEV - 1,)),
                        pltpu.SemaphoreType.DMA((N_DEV - 1,)), pltpu.SemaphoreType.DMA(())],
        compiler_params=_cparams(has_side_effects=True),
    )(g)


def _adamw_small(name, w, g, m, v):
    def body(w_ref, g_ref, m_ref, v_ref, delta_ref, nm_ref, nv_ref):
        delta, nm, nv = _adamw(w_ref[...], g_ref[...], m_ref[...], v_ref[...])
        delta_ref[...] = delta
        nm_ref[...] = nm
        nv_ref[...] = nv

    vmem = pl.BlockSpec(memory_space=pltpu.VMEM)
    return pl.pallas_call(body, name=name, out_shape=[jax.ShapeDtypeStruct(w.shape, F32)] * 3,
                          in_specs=[vmem] * 4, out_specs=[vmem] * 3)(w, g, m, v)


def _in_proj(x, g, w_in):
    S = x.shape[0]
    tm = min(TOKEN_TILE, S)
    nconv = 2 * CONV_CH

    def body(x_ref, g_ref, w_ref, a_ref, uc_ref, qkv_ref):
        xf = x_ref[...]
        a = (xf * _rms_r(xf) * g_ref[...]).astype(BF16)
        a_ref[...] = a
        uc_ref[...] = _dot(a, w_ref[:, :nconv])
        qkv_ref[:, :SB_WIDTH] = (_dot(a, w_ref[:, nconv:nconv + SB_WIDTH]) * (1.0 / math.sqrt(SB_HEAD_DIM))).astype(BF16)
        qkv_ref[:, SB_WIDTH:] = _dot(a, w_ref[:, nconv + SB_WIDTH:]).astype(BF16)

    row = lambda n: pl.BlockSpec((tm, n), lambda i: (i, 0))
    return pl.pallas_call(
        body, name="in_proj", grid=(S // tm,),
        out_shape=[jax.ShapeDtypeStruct((S, D_MODEL), BF16), jax.ShapeDtypeStruct((S, nconv), F32),
                   jax.ShapeDtypeStruct((S, 3 * SB_WIDTH), BF16)],
        in_specs=[row(D_MODEL), _const((1, D_MODEL)), _resident(w_in.shape)],
        out_specs=[row(D_MODEL), row(nconv), row(3 * SB_WIDTH)],
        compiler_params=_cparams(),
    )(x, g, w_in)


def _glu(u):
    val, gate = u[:, :CONV_CH], u[:, CONV_CH:]
    sg = jax.nn.sigmoid(gate)
    return val, sg, val * sg


def _conv_rows(glu_ext, cw_ref, r0, rows):
    base = r0 + CONV_HALO - (CONV_WIDTH - 1)
    acc = cw_ref[0:1, :] * glu_ext[base:base + rows, :]
    for w in range(1, CONV_WIDTH):
        acc = acc + cw_ref[w:w + 1, :] * glu_ext[base + w:base + w + rows, :]
    return acc


def _conv_fwd(u_conv, conv_w, conv_b, ln_g, ln_b):
    S = u_conv.shape[0]
    tc = min(TOKEN_TILE, S)

    def body(u_ref, cw_ref, cb_ref, lg_ref, lb_ref, out_ref, glu_ext):
        i = pl.program_id(0)

        @pl.when(i == 0)
        def _():
            glu_ext[0:CONV_HALO, :] = jnp.zeros((CONV_HALO, CONV_CH), F32)

        @pl.when(i > 0)
        def _():
            glu_ext[0:CONV_HALO, :] = glu_ext[tc:tc + CONV_HALO, :]

        glu_ext[CONV_HALO:, :] = _glu(u_ref[...])[2]
        for r0 in range(0, tc, CONV_CHUNK):
            y = _conv_rows(glu_ext, cw_ref, r0, CONV_CHUNK) + cb_ref[...]
            mu = jnp.mean(y, axis=-1, keepdims=True)
            yc = y - mu
            yn = yc * lax.rsqrt(jnp.mean(yc * yc, axis=-1, keepdims=True) + EPS)
            yl = yn * lg_ref[...] + lb_ref[...]
            out_ref[r0:r0 + CONV_CHUNK, :] = (yl * jax.nn.sigmoid(yl)).astype(BF16)

    return pl.pallas_call(
        body, name="conv_fwd", grid=(S // tc,),
        out_shape=jax.ShapeDtypeStruct((S, CONV_CH), BF16),
        in_specs=[pl.BlockSpec((tc, 2 * CONV_CH), lambda i: (i, 0)), _const((CONV_HALO, CONV_CH)),
                  _const((1, CONV_CH)), _const((1, CONV_CH)), _const((1, CONV_CH))],
        out_specs=pl.BlockSpec((tc, CONV_CH), lambda i: (i, 0)),
        scratch_shapes=[pltpu.VMEM((tc + CONV_HALO, CONV_CH), F32)],
        compiler_params=_cparams(dimension_semantics=("arbitrary",)),
    )(u_conv, conv_w, conv_b, ln_g, ln_b)


def _head_masks():
    lane = lax.broadcasted_iota(jnp.int32, (1, LANES), 1)
    return lane < SB_HEAD_DIM


def _split_heads(t, first):
    z = jnp.zeros_like(t)
    return jnp.where(first, t, z), jnp.where(first, z, t)


def _head_sum(t, first):
    a = jnp.sum(jnp.where(first, t, 0.0), axis=-1, keepdims=True)
    b = jnp.sum(jnp.where(first, 0.0, t), axis=-1, keepdims=True)
    return a, b


def _attn_fwd(qkv, g_attn):
    S = qkv.shape[0]
    Q = min(ATTN_BLOCK, S)
    nq = S // Q
    assert nq <= LANES
    ntiles = nq * (nq + 1) // 2
    assert ntiles % 2 == 0 and ntiles >= 8
    npair = SB_WIDTH // LANES
    tiles = [(i, j) for i in range(nq) for j in range(i, -1, -1)]

    def body(q_ref, k_ref, v_ref, g_ref, o_ref, ao_ref, cl_ref, z_buf, l_buf, z2_buf, a_buf, c_buf, mask_buf):
        first = _head_masks()
        lane = lax.broadcasted_iota(jnp.int32, (1, LANES), 1)
        row = lax.broadcasted_iota(jnp.int32, (Q, Q), 0)
        col = lax.broadcasted_iota(jnp.int32, (Q, Q), 1)
        tri = (row >= col).astype(BF16)
        heads = range(2)
        strips = [slice(r0, r0 + ATTN_STRIP) for r0 in range(0, Q, ATTN_STRIP)]
        rows = lambda j: pl.ds(pl.multiple_of(j * Q, Q), Q)
        wide = lambda t: jnp.tile(t, (1, Q // LANES))
        as_int = lambda t: int(t) if isinstance(t, (bool, int)) else t.astype(jnp.int32)

        keep = col < row
        mask_buf[0, 0] = jnp.ones((Q, Q), F32)
        mask_buf[0, 1] = jnp.zeros((Q, Q), F32)
        mask_buf[1, 0] = jnp.where(keep, 1.0, 0.0)
        mask_buf[1, 1] = jnp.where(keep, 0.0, MASKED)
        o_ref[...] = jnp.zeros_like(o_ref)

        def scores(t, slot):
            i, j = t
            qh = _split_heads(q_ref[rows(i), :], first)
            kb = k_ref[rows(j), :]
            for h in heads:
                z_buf[slot, h] = _dot_nt(qh[h], kb)

        def logs(t, slot):
            i, j = t
            diag = as_int(i == j)
            for h in heads:
                for r in strips:
                    z2 = z_buf[slot, h, r, :] * LOG2E
                    l = (jnp.minimum(z2, 0.0) - jnp.log2(1.0 + jnp.exp2(-jnp.abs(z2)))) - z2
                    l_buf[slot, h, r, :] = (l * mask_buf[diag, 0, r, :]).astype(BF16)
                    z2_buf[slot, h, r, :] = z2 + mask_buf[diag, 1, r, :]

        def sums(slot):
            return tuple(_dot(l_buf[slot, h], tri) for h in heads)

        def weights(t, slot, sm):
            i, j = t
            running = jnp.where(j == i, 0.0, 1.0)
            for h in heads:
                before = c_buf[h] * running
                for r in strips:
                    a_buf[slot, h, r, :] = jnp.exp2(z2_buf[slot, h, r, :] + sm[h][r] + wide(before[r])).astype(BF16)
                hl = slice(h * LANES, (h + 1) * LANES)
                cl_ref[rows(i), hl] = jnp.where(lane == j, before, cl_ref[rows(i), hl] * running)
                c_buf[h] = before + jnp.broadcast_to(sm[h][:, 0:1], (Q, LANES))

        def values(t, slot):
            i, j = t
            vh = _split_heads(v_ref[rows(j), :], first)
            o_ref[rows(i), :] += _dot(a_buf[slot, 0], vh[0]) + _dot(a_buf[slot, 1], vh[1])

        def iteration(t, p):
            ta, tb, tc, td = t
            if ta is not None:
                scores(ta, p)
            if tc is not None:
                sm = sums(p)
            if td is not None:
                values(td, 1 - p)
            if tb is not None:
                logs(tb, 1 - p)
            if tc is not None:
                weights(tc, p, sm)

        def window(n):
            return tuple(tiles[n - k] if 0 <= n - k < ntiles else None for k in range(4))

        def following(t):
            i, j = t
            last = j == 0
            return jnp.where(last, i + 1, i), jnp.where(last, i + 1, j - 1)

        def two_iterations(_, t):
            iteration(t, 0)
            t = (following(t[0]),) + t[:3]
            iteration(t, 1)
            return (following(t[0]),) + t[:3]

        c_buf[...] = jnp.zeros_like(c_buf)
        for n in range(4):
            iteration(window(n), n % 2)
        first_window = tuple((jnp.int32(i), jnp.int32(j)) for i, j in window(4))
        lax.fori_loop(0, (ntiles - 4) // 2, two_iterations, first_window)
        for n in range(ntiles, ntiles + 3):
            iteration(window(n), n % 2)

        def head_norm(b, carry):
            o = o_ref[rows(b), :]
            sa, sb = _head_sum(o * o, first)
            r = jnp.where(first, lax.rsqrt(sa * (1.0 / SB_HEAD_DIM) + EPS), lax.rsqrt(sb * (1.0 / SB_HEAD_DIM) + EPS))
            ao_ref[rows(b), :] = (o * r * g_ref[...]).astype(BF16)
            return carry

        lax.fori_loop(0, nq, head_norm, 0)

    col_block = lambda off: pl.BlockSpec((S, LANES), lambda p: (0, off + p), pipeline_mode=pl.Buffered(1))
    out_block = lambda n: pl.BlockSpec((S, n), lambda p: (0, p), pipeline_mode=pl.Buffered(1))
    return pl.pallas_call(
        body, name="attn_fwd", grid=(npair,),
        out_shape=[jax.ShapeDtypeStruct((S, SB_WIDTH), F32), jax.ShapeDtypeStruct((S, SB_WIDTH), BF16),
                   jax.ShapeDtypeStruct((S, 2 * SB_WIDTH), F32)],
        in_specs=[col_block(0), col_block(npair), col_block(2 * npair), pl.BlockSpec((1, LANES), lambda p: (0, p))],
        out_specs=[out_block(LANES), out_block(LANES), out_block(2 * LANES)],
        scratch_shapes=[pltpu.VMEM((2, 2, Q, Q), F32), pltpu.VMEM((2, 2, Q, Q), BF16), pltpu.VMEM((2, 2, Q, Q), F32),
                        pltpu.VMEM((2, 2, Q, Q), BF16), pltpu.VMEM((2, Q, LANES), F32), pltpu.VMEM((2, 2, Q, Q), F32)],
        compiler_params=_cparams(dimension_semantics=("arbitrary",)),
    )(qkv, qkv, qkv, g_attn)


def _out_proj(conv_out, attn_out, w_out, x, g_post_mix, g_pre_ffn):
    S = x.shape[0]
    tm = min(TOKEN_TILE, S)

    def body(co_ref, ao_ref, w_ref, x_ref, g1_ref, g2_ref, y_ref, h1_ref, fin_ref):
        y = _dot(co_ref[...], w_ref[:CONV_CH, :]) + _dot(ao_ref[...], w_ref[CONV_CH:, :])
        h1 = x_ref[...] + y * _rms_r(y) * g1_ref[...]
        y_ref[...] = y
        h1_ref[...] = h1
        fin_ref[...] = (h1 * _rms_r(h1) * g2_ref[...]).astype(BF16)

    row = lambda n: pl.BlockSpec((tm, n), lambda i: (i, 0))
    return pl.pallas_call(
        body, name="out_proj", grid=(S // tm,),
        out_shape=[jax.ShapeDtypeStruct((S, D_MODEL), F32), jax.ShapeDtypeStruct((S, D_MODEL), F32),
                   jax.ShapeDtypeStruct((S, D_MODEL), BF16)],
        in_specs=[row(CONV_CH), row(SB_WIDTH), _resident(w_out.shape), row(D_MODEL), _const((1, D_MODEL)),
                  _const((1, D_MODEL))],
        out_specs=[row(D_MODEL)] * 3,
        compiler_params=_cparams(),
    )(conv_out, attn_out, w_out, x, g_post_mix, g_pre_ffn)


def _ffn_fwd_loss(f_in, w_gate, w_up, w_down, h1, target, g_post_ffn):
    S = f_in.shape[0]
    tm = min(FFN_TILE, S)
    nt = S // tm

    def body(fin_ref, wg_ref, wu_ref, wd_ref, h1_ref, t_ref, g_ref, gt_ref, up_ref, df_ref, dh2_ref, loss_ref, dg_ref,
             sq_acc):
        i = pl.program_id(0)

        @pl.when(i == 0)
        def _():
            sq_acc[...] = jnp.zeros_like(sq_acc)
            dg_ref[...] = jnp.zeros_like(dg_ref)

        fin = fin_ref[...]
        f = jnp.zeros((tm, D_MODEL), F32)
        for c0 in range(0, D_FF, FF_CHUNK):
            cols = slice(c0, c0 + FF_CHUNK)
            gt = _dot(fin, wg_ref[:, cols])
            up = _dot(fin, wu_ref[:, cols])
            gt_ref[:, cols] = gt.astype(BF16)
            up_ref[:, cols] = up.astype(BF16)
            f = f + _dot((gt * jax.nn.sigmoid(gt) * up).astype(BF16), wd_ref[cols, :])
        r = _rms_r(f)
        g = g_ref[...]
        diff = h1_ref[...] + f * r * g - t_ref[...]
        sq_acc[...] += jnp.sum(diff * diff, axis=0, keepdims=True)
        dh2 = diff * (1.0 / D_MODEL)
        dh2_ref[...] = dh2
        dg_ref[...] += jnp.sum(dh2 * f * r, axis=0, keepdims=True)
        df_ref[...] = _rms_bwd(f, r, g, dh2).astype(BF16)

        @pl.when(i == nt - 1)
        def _():
            loss_ref[...] = jnp.broadcast_to((0.5 / D_MODEL) * jnp.sum(sq_acc[...], axis=-1, keepdims=True), (1, LANES))

    row = lambda n: pl.BlockSpec((tm, n), lambda i: (i, 0))
    return pl.pallas_call(
        body, name="ffn_fwd_loss", grid=(nt,),
        out_shape=[jax.ShapeDtypeStruct((S, D_FF), BF16), jax.ShapeDtypeStruct((S, D_FF), BF16),
                   jax.ShapeDtypeStruct((S, D_MODEL), BF16), jax.ShapeDtypeStruct((S, D_MODEL), F32),
                   jax.ShapeDtypeStruct((1, LANES), F32), jax.ShapeDtypeStruct((1, D_MODEL), F32)],
        in_specs=[row(D_MODEL), _resident(w_gate.shape), _resident(w_up.shape), _resident(w_down.shape),
                  row(D_MODEL), row(D_MODEL), _const((1, D_MODEL))],
        out_specs=[row(D_FF), row(D_FF), row(D_MODEL), row(D_MODEL), _const((1, LANES)), _const((1, D_MODEL))],
        scratch_shapes=[pltpu.VMEM((1, D_MODEL), F32)],
        compiler_params=_cparams(dimension_semantics=("arbitrary",)),
    )(f_in, w_gate, w_up, w_down, h1, target, g_post_ffn)


def _ffn_bwd(df, gt, up, w_gate, w_up, w_down):
    S = df.shape[0]
    tm = min(FFN_TILE, S)

    def body(df_ref, gt_ref, up_ref, wg_ref, wu_ref, wd_ref, dgt_ref, dup_ref, act_ref, dfin_ref):
        df = df_ref[...]
        dfin = jnp.zeros((tm, D_MODEL), F32)
        for c0 in range(0, D_FF, FF_CHUNK):
            cols = slice(c0, c0 + FF_CHUNK)
            dact = _dot_nt(df, wd_ref[cols, :])
            gt = gt_ref[:, cols].astype(F32)
            up = up_ref[:, cols].astype(F32)
            s = jax.nn.sigmoid(gt)
            silu = gt * s
            dgt = (dact * up * (s * (1.0 + gt * (1.0 - s)))).astype(BF16)
            dup = (dact * silu).astype(BF16)
            act_ref[:, cols] = (silu * up).astype(BF16)
            dgt_ref[:, cols] = dgt
            dup_ref[:, cols] = dup
            dfin = dfin + _dot_nt(dgt, wg_ref[:, cols]) + _dot_nt(dup, wu_ref[:, cols])
        dfin_ref[...] = dfin

    row = lambda n: pl.BlockSpec((tm, n), lambda i: (i, 0))
    return pl.pallas_call(
        body, name="ffn_bwd", grid=(S // tm,),
        out_shape=[jax.ShapeDtypeStruct((S, D_FF), BF16)] * 3 + [jax.ShapeDtypeStruct((S, D_MODEL), F32)],
        in_specs=[row(D_MODEL), row(D_FF), row(D_FF), _resident(w_gate.shape), _resident(w_up.shape),
                  _resident(w_down.shape)],
        out_specs=[row(D_FF)] * 3 + [row(D_MODEL)],
        compiler_params=_cparams(),
    )(df, gt, up, w_gate, w_up, w_down)


def _matmul_tn(name, x, y, tn):
    S, K = x.shape
    N = y.shape[1]
    ts = min(TOKEN_TILE, S)

    def body(x_ref, y_ref, o_ref):
        @pl.when(pl.program_id(1) == 0)
        def _():
            o_ref[...] = jnp.zeros_like(o_ref)

        o_ref[...] += _dot_tn(x_ref[...].astype(BF16), y_ref[...].astype(BF16))

    return pl.pallas_call(
        body, name=name, grid=(N // tn, S // ts),
        out_shape=jax.ShapeDtypeStruct((K, N), F32),
        in_specs=[pl.BlockSpec((ts, K), lambda n, s: (s, 0)), pl.BlockSpec((ts, tn), lambda n, s: (s, n))],
        out_specs=pl.BlockSpec((K, tn), lambda n, s: (0, n)),
        compiler_params=_cparams(dimension_semantics=("arbitrary", "arbitrary")),
    )(x, y)


def _mix_bwd(dfin, h1, y, dh2, g_pre_ffn, g_post_mix, w_out):
    S = dfin.shape[0]
    tm = min(TOKEN_TILE, S)

    def body(dfin_ref, h1_ref, y_ref, dh2_ref, g2_ref, g1_ref, w_ref, dh1_ref, dy_ref, dco_ref, dao_ref, dg2_ref, dg1_ref):
        @pl.when(pl.program_id(0) == 0)
        def _():
            dg2_ref[...] = jnp.zeros_like(dg2_ref)
            dg1_ref[...] = jnp.zeros_like(dg1_ref)

        h1, dfin = h1_ref[...], dfin_ref[...]
        r2 = _rms_r(h1)
        dh1 = dh2_ref[...] + _rms_bwd(h1, r2, g2_ref[...], dfin)
        dg2_ref[...] += jnp.sum(dfin * h1 * r2, axis=0, keepdims=True)
        y = y_ref[...]
        r1 = _rms_r(y)
        dy = _rms_bwd(y, r1, g1_ref[...], dh1).astype(BF16)
        dg1_ref[...] += jnp.sum(dh1 * y * r1, axis=0, keepdims=True)
        dh1_ref[...] = dh1
        dy_ref[...] = dy
        dco_ref[...] = _dot_nt(dy, w_ref[:CONV_CH, :])
        dao_ref[...] = _dot_nt(dy, w_ref[CONV_CH:, :])

    row = lambda n: pl.BlockSpec((tm, n), lambda i: (i, 0))
    return pl.pallas_call(
        body, name="mix_bwd", grid=(S // tm,),
        out_shape=[jax.ShapeDtypeStruct((S, D_MODEL), F32), jax.ShapeDtypeStruct((S, D_MODEL), BF16),
                   jax.ShapeDtypeStruct((S, CONV_CH), F32), jax.ShapeDtypeStruct((S, SB_WIDTH), F32),
                   jax.ShapeDtypeStruct((1, D_MODEL), F32), jax.ShapeDtypeStruct((1, D_MODEL), F32)],
        in_specs=[row(D_MODEL)] * 4 + [_const((1, D_MODEL)), _const((1, D_MODEL)), _resident(w_out.shape)],
        out_specs=[row(D_MODEL), row(D_MODEL), row(CONV_CH), row(SB_WIDTH), _const((1, D_MODEL)), _const((1, D_MODEL))],
        compiler_params=_cparams(dimension_semantics=("arbitrary",)),
    )(dfin, h1, y, dh2, g_pre_ffn, g_post_mix, w_out)


def _attn_norm_bwd(o, dao, g_attn):
    S = o.shape[0]
    tm = min(TOKEN_TILE, S)
    inv_dh = 1.0 / SB_HEAD_DIM

    def body(o_ref, dao_ref, g_ref, do_ref, dg_ref):
        @pl.when(pl.program_id(0) == 0)
        def _():
            dg_ref[...] = jnp.zeros_like(dg_ref)

        first = _head_masks()
        for p in range(SB_WIDTH // LANES):
            cols = slice(p * LANES, (p + 1) * LANES)
            o, dao, g = o_ref[:, cols], dao_ref[:, cols], g_ref[:, cols]
            sa, sb = _head_sum(o * o, first)
            r = jnp.where(first, lax.rsqrt(sa * inv_dh + EPS), lax.rsqrt(sb * inv_dh + EPS))
            w = dao * g
            wa, wb = _head_sum(w * o, first)
            do_ref[:, cols] = (r * (w - o * (r * r) * (jnp.where(first, wa, wb) * inv_dh))).astype(BF16)
            dg_ref[:, cols] += jnp.sum(dao * o * r, axis=0, keepdims=True)

    row = pl.BlockSpec((tm, SB_WIDTH), lambda i: (i, 0))
    return pl.pallas_call(
        body, name="attn_norm_bwd", grid=(S // tm,),
        out_shape=[jax.ShapeDtypeStruct((S, SB_WIDTH), BF16), jax.ShapeDtypeStruct((1, SB_WIDTH), F32)],
        in_specs=[row, row, _const((1, SB_WIDTH))], out_specs=[row, _const((1, SB_WIDTH))],
        compiler_params=_cparams(dimension_semantics=("arbitrary",)),
    )(o, dao, g_attn)


def _attn_bwd(qkv, do, cl):
    S = qkv.shape[0]
    Q = min(ATTN_BLOCK, S)
    nq = S // Q
    ntiles = nq * (nq + 1) // 2
    assert ntiles % 2 == 0 and ntiles >= 8
    npair = SB_WIDTH // LANES
    tiles = [(i, j) for i in range(nq) for j in range(i + 1)]

    def body(q_ref, k_ref, v_ref, do_ref, cl_ref, dq_ref, dk_ref, dv_ref,
             z_buf, lb_buf, be_buf, g_buf, l_buf, a_buf, gb_buf, dz_buf, pg_buf, mask_buf):
        first = _head_masks()
        lane = lax.broadcasted_iota(jnp.int32, (1, LANES), 1)
        row = lax.broadcasted_iota(jnp.int32, (Q, Q), 0)
        col = lax.broadcasted_iota(jnp.int32, (Q, Q), 1)
        tri = (row > col).astype(BF16)
        tpi = (row <= col).astype(BF16)
        heads = range(2)
        strips = [slice(r0, r0 + ATTN_STRIP) for r0 in range(0, Q, ATTN_STRIP)]
        rows = lambda j: pl.ds(pl.multiple_of(j * Q, Q), Q)
        wide = lambda t: jnp.tile(t, (1, Q // LANES))
        as_int = lambda t: int(t) if isinstance(t, (bool, int)) else t.astype(jnp.int32)

        keep = col < row
        mask_buf[0, 0] = jnp.ones((Q, Q), F32)
        mask_buf[0, 1] = jnp.zeros((Q, Q), F32)
        mask_buf[1, 0] = jnp.where(keep, 1.0, 0.0)
        mask_buf[1, 1] = jnp.where(keep, 0.0, MASKED)
        dq_ref[...] = jnp.zeros_like(dq_ref)
        dk_ref[...] = jnp.zeros_like(dk_ref)
        dv_ref[...] = jnp.zeros_like(dv_ref)

        def scores(t, slot):
            i, j = t
            qh = _split_heads(q_ref[rows(i), :], first)
            kb = k_ref[rows(j), :]
            for h in heads:
                z_buf[slot, h] = _dot_nt(qh[h], kb)

        def logs(t, slot):
            i, j = t
            diag = as_int(i == j)
            for h in heads:
                for r in strips:
                    z2 = z_buf[slot, h, r, :] * LOG2E
                    lb = jnp.minimum(z2, 0.0) - jnp.log2(1.0 + jnp.exp2(-jnp.abs(z2)))
                    l = (lb - z2) * mask_buf[diag, 0, r, :]
                    l_buf[slot, h, r, :] = l.astype(BF16)
                    lb_buf[slot, h, r, :] = lb + mask_buf[diag, 1, r, :]

        def sums(t, slot):
            i, j = t
            doh = _split_heads(do_ref[rows(i), :], first)
            vb = v_ref[rows(j), :]
            return (tuple(_dot(l_buf[slot, h], tri) for h in heads), tuple(_dot_nt(doh[h], vb) for h in heads))

        def weights(t, slot, sm, da):
            i, j = t
            for h in heads:
                c = jnp.sum(jnp.where(lane == j, cl_ref[rows(i), h * LANES:(h + 1) * LANES], 0.0), axis=-1, keepdims=True)
                c = jnp.broadcast_to(c, (Q, LANES))
                for r in strips:
                    lb = lb_buf[slot, h, r, :]
                    a = jnp.exp2(lb + sm[h][r] + wide(c[r]))
                    g = da[h][r] * a
                    a_buf[slot, h, r, :] = a.astype(BF16)
                    be_buf[slot, h, r, :] = jnp.exp2(lb)
                    g_buf[slot, h, r, :] = g
                    gb_buf[slot, h, r, :] = g.astype(BF16)

        def prefix(t, slot):
            i, j = t
            doh = _split_heads(do_ref[rows(i), :], first)
            dv_ref[rows(j), :] += _dot_tn(a_buf[slot, 0], doh[0]) + _dot_tn(a_buf[slot, 1], doh[1])
            return tuple(_dot(gb_buf[slot, h], tpi) for h in heads)

        def dscores(t, slot, pm):
            i, j = t
            for h in heads:
                pg = pg_buf[h] * jnp.where(j == 0, 0.0, 1.0)
                for r in strips:
                    dz = g_buf[slot, h, r, :] - be_buf[slot, h, r, :] * (pm[h][r] + wide(pg[r]))
                    dz_buf[slot, h, r, :] = dz.astype(BF16)
                pg_buf[h] = pg + jnp.broadcast_to(pm[h][:, Q - 1:Q], (Q, LANES))

        def grads(t, slot):
            i, j = t
            qh = _split_heads(q_ref[rows(i), :], first)
            kh = _split_heads(k_ref[rows(j), :], first)
            dq_ref[rows(i), :] += _dot(dz_buf[slot, 0], kh[0]) + _dot(dz_buf[slot, 1], kh[1])
            dk_ref[rows(j), :] += _dot_tn(dz_buf[slot, 0], qh[0]) + _dot_tn(dz_buf[slot, 1], qh[1])

        def iteration(t, p):
            ta, tb, tc, td, te = t
            if ta is not None:
                scores(ta, p)
            if tc is not None:
                sm, da = sums(tc, p)
            if td is not None:
                pm = prefix(td, 1 - p)
            if te is not None:
                grads(te, p)
            if tb is not None:
                logs(tb, 1 - p)
            if tc is not None:
                weights(tc, p, sm, da)
            if td is not None:
                dscores(td, 1 - p, pm)

        def window(n):
            return tuple(tiles[n - k] if 0 <= n - k < ntiles else None for k in range(5))

        def following(t):
            i, j = t
            last = j == i
            return jnp.where(last, i + 1, i), jnp.where(last, 0, j + 1)

        def two_iterations(_, t):
            iteration(t, 0)
            t = (following(t[0]),) + t[:4]
            iteration(t, 1)
            return (following(t[0]),) + t[:4]

        pg_buf[...] = jnp.zeros_like(pg_buf)
        for n in range(4):
            iteration(window(n), n % 2)
        first_window = tuple((jnp.int32(i), jnp.int32(j)) for i, j in window(4))
        lax.fori_loop(0, (ntiles - 4) // 2, two_iterations, first_window)
        for n in range(ntiles, ntiles + 4):
            iteration(window(n), n % 2)
        dq_ref[...] = dq_ref[...] * (1.0 / math.sqrt(SB_HEAD_DIM))

    col_block = lambda off: pl.BlockSpec((S, LANES), lambda p: (0, off + p), pipeline_mode=pl.Buffered(1))
    return pl.pallas_call(
        body, name="attn_bwd", grid=(npair,),
        out_shape=[jax.ShapeDtypeStruct((S, SB_WIDTH), F32)] * 3,
        in_specs=[col_block(0), col_block(npair), col_block(2 * npair), col_block(0),
                  pl.BlockSpec((S, 2 * LANES), lambda p: (0, p), pipeline_mode=pl.Buffered(1))],
        out_specs=[pl.BlockSpec((S, LANES), lambda p: (0, p), pipeline_mode=pl.Buffered(1))] * 3,
        scratch_shapes=[pltpu.VMEM((2, 2, Q, Q), F32)] * 4 + [pltpu.VMEM((2, 2, Q, Q), BF16)] * 4
        + [pltpu.VMEM((2, Q, LANES), F32), pltpu.VMEM((2, 2, Q, Q), F32)],
        compiler_params=_cparams(dimension_semantics=("arbitrary",)),
    )(qkv, qkv, qkv, do, cl)


def _conv_bwd(u_conv, dco, conv_w, conv_b, ln_g, ln_b):
    S = u_conv.shape[0]
    tc = min(TOKEN_TILE, S)
    nt = S // tc
    per = tc // CONV_HALO
    groups = CONV_CHUNK // 8

    def body(u_ref, halo_ref, dco_ref, cw_ref, cb_ref, lg_ref, lb_ref, du_ref, dcw_ref, dsm_ref, glu_ext, dyc_ext, sg_buf,
             dcw_acc, dsm_acc):
        i = pl.program_id(0)
        ti = nt - 1 - i

        @pl.when(i == 0)
        def _():
            dyc_ext[tc:, :] = jnp.zeros((CONV_HALO, CONV_CH), F32)
            dcw_acc[...] = jnp.zeros_like(dcw_acc)
            dsm_acc[...] = jnp.zeros_like(dsm_acc)

        @pl.when(i > 0)
        def _():
            dyc_ext[tc:, :] = dyc_ext[0:CONV_HALO, :]

        glu_ext[0:CONV_HALO, :] = jnp.where(ti > 0, _glu(halo_ref[...])[2], 0.0)
        val, sg, glu = _glu(u_ref[...])
        glu_ext[CONV_HALO:, :] = glu
        sg_buf[...] = sg

        dcb = jnp.zeros((8, CONV_CH), F32)
        dlg = jnp.zeros((8, CONV_CH), F32)
        dlb = jnp.zeros((8, CONV_CH), F32)
        fold = lambda t: jnp.sum(t.reshape(groups, 8, CONV_CH), axis=0)
        for r0 in range(0, tc, CONV_CHUNK):
            y = _conv_rows(glu_ext, cw_ref, r0, CONV_CHUNK) + cb_ref[...]
            mu = jnp.mean(y, axis=-1, keepdims=True)
            yc = y - mu
            rstd = lax.rsqrt(jnp.mean(yc * yc, axis=-1, keepdims=True) + EPS)
            yn = yc * rstd
            yl = yn * lg_ref[...] + lb_ref[...]
            s = jax.nn.sigmoid(yl)
            dyl = dco_ref[r0:r0 + CONV_CHUNK, :] * (s * (1.0 + yl * (1.0 - s)))
            dlg = dlg + fold(dyl * yn)
            dlb = dlb + fold(dyl)
            wv = dyl * lg_ref[...]
            dyc = rstd * (wv - jnp.mean(wv, axis=-1, keepdims=True) - yn * jnp.mean(wv * yn, axis=-1, keepdims=True))
            dcb = dcb + fold(dyc)
            dyc_ext[r0:r0 + CONV_CHUNK, :] = dyc
        dsm_acc[0:8, :] += dcb
        dsm_acc[8:16, :] += dlg
        dsm_acc[16:24, :] += dlb

        for r0 in range(0, tc, CONV_CHUNK):
            dyc = dyc_ext[r0:r0 + CONV_CHUNK, :]
            dglu = jnp.zeros((CONV_CHUNK, CONV_CH), F32)
            base = r0 + CONV_HALO - (CONV_WIDTH - 1)
            for w in range(CONV_WIDTH):
                back = r0 + (CONV_WIDTH - 1) - w
                dglu = dglu + cw_ref[w:w + 1, :] * dyc_ext[back:back + CONV_CHUNK, :]
                dcw_acc[8 * w:8 * w + 8, :] += fold(dyc * glu_ext[base + w:base + w + CONV_CHUNK, :])
            sg = sg_buf[r0:r0 + CONV_CHUNK, :]
            v = u_ref[r0:r0 + CONV_CHUNK, :CONV_CH]
            du_ref[r0:r0 + CONV_CHUNK, :CONV_CH] = (dglu * sg).astype(BF16)
            du_ref[r0:r0 + CONV_CHUNK, CONV_CH:] = (dglu * v * sg * (1.0 - sg)).astype(BF16)

        @pl.when(i == nt - 1)
        def _():
            for w in range(CONV_WIDTH):
                dcw_ref[w:w + 1, :] = jnp.sum(dcw_acc[8 * w:8 * w + 8, :], axis=0, keepdims=True)
            dcw_ref[CONV_WIDTH:, :] = jnp.zeros((CONV_HALO - CONV_WIDTH, CONV_CH), F32)
            for k in range(3):
                dsm_ref[k:k + 1, :] = jnp.sum(dsm_acc[8 * k:8 * k + 8, :], axis=0, keepdims=True)
            dsm_ref[3:, :] = jnp.zeros((5, CONV_CH), F32)

    return pl.pallas_call(
        body, name="conv_bwd", grid=(nt,),
        out_shape=[jax.ShapeDtypeStruct((S, 2 * CONV_CH), BF16), jax.ShapeDtypeStruct((CONV_HALO, CONV_CH), F32),
                   jax.ShapeDtypeStruct((8, CONV_CH), F32)],
        in_specs=[pl.BlockSpec((tc, 2 * CONV_CH), lambda i: (nt - 1 - i, 0)),
                  pl.BlockSpec((CONV_HALO, 2 * CONV_CH), lambda i: (jnp.maximum((nt - 1 - i) * per - 1, 0), 0)),
                  pl.BlockSpec((tc, CONV_CH), lambda i: (nt - 1 - i, 0)),
                  _const((CONV_HALO, CONV_CH)), _const((1, CONV_CH)), _const((1, CONV_CH)), _const((1, CONV_CH))],
        out_specs=[pl.BlockSpec((tc, 2 * CONV_CH), lambda i: (nt - 1 - i, 0)), _const((CONV_HALO, CONV_CH)),
                   _const((8, CONV_CH))],
        scratch_shapes=[pltpu.VMEM((tc + CONV_HALO, CONV_CH), F32), pltpu.VMEM((tc + CONV_HALO, CONV_CH), F32),
                        pltpu.VMEM((tc, CONV_CH), F32), pltpu.VMEM((8 * CONV_HALO, CONV_CH), F32),
                        pltpu.VMEM((24, CONV_CH), F32)],
        compiler_params=_cparams(dimension_semantics=("arbitrary",)),
    )(u_conv, u_conv, dco, conv_w, conv_b, ln_g, ln_b)


def _in_proj_bwd(du_conv, dq, dk, dv, w_in, x, g, dh1):
    S = x.shape[0]
    tm = min(TOKEN_TILE, S)
    nconv = 2 * CONV_CH

    def body(duc_ref, dq_ref, dk_ref, dv_ref, w_ref, x_ref, g_ref, dh1_ref, dx_ref, dg_ref):
        @pl.when(pl.program_id(0) == 0)
        def _():
            dg_ref[...] = jnp.zeros_like(dg_ref)

        da = _dot_nt(duc_ref[...], w_ref[:, :nconv])
        for n, ref in enumerate((dq_ref, dk_ref, dv_ref)):
            c0 = nconv + n * SB_WIDTH
            da = da + _dot_nt(ref[...].astype(BF16), w_ref[:, c0:c0 + SB_WIDTH])
        xf = x_ref[...]
        r = _rms_r(xf)
        dx_ref[...] = dh1_ref[...] + _rms_bwd(xf, r, g_ref[...], da)
        dg_ref[...] += jnp.sum(da * xf * r, axis=0, keepdims=True)

    row = lambda n: pl.BlockSpec((tm, n), lambda i: (i, 0))
    return pl.pallas_call(
        body, name="in_proj_bwd", grid=(S // tm,),
        out_shape=[jax.ShapeDtypeStruct((S, D_MODEL), F32), jax.ShapeDtypeStruct((1, D_MODEL), F32)],
        in_specs=[row(nconv), row(SB_WIDTH), row(SB_WIDTH), row(SB_WIDTH), _resident(w_in.shape), row(D_MODEL),
                  _const((1, D_MODEL)), row(D_MODEL)],
        out_specs=[row(D_MODEL), _const((1, D_MODEL))],
        compiler_params=_cparams(dimension_semantics=("arbitrary",)),
    )(du_conv, dq, dk, dv, w_in, x, g, dh1)


def _layer_grads(xs, target, g_pre_mix, w_in_f, conv_w_f, conv_b, conv_ln_g, conv_ln_b, attn_g, g_post_mix, g_pre_ffn,
                 g_post_ffn, late_weights, send_grads):
    a, u_conv, qkv = _in_proj(xs, g_pre_mix, w_in_f)
    conv_out = _conv_fwd(u_conv, conv_w_f, conv_b, conv_ln_g, conv_ln_b)
    o, attn_out, cl = _attn_fwd(qkv, attn_g)
    w_out_f, w_gate_f, w_up_f, w_down_f = late_weights(attn_out)
    y, h1, f_in = _out_proj(conv_out, attn_out, w_out_f, xs, g_post_mix, g_pre_ffn)
    gt, up, df, dh2, loss_part, d_g_post_ffn = _ffn_fwd_loss(f_in, w_gate_f, w_up_f, w_down_f, h1, target, g_post_ffn)

    dgt, dup, act, dfin = _ffn_bwd(df, gt, up, w_gate_f, w_up_f, w_down_f)
    d_w_down = _matmul_tn("grad_w_down", act, df, 512)
    d_w_gate = _matmul_tn("grad_w_gate", f_in, dgt, FF_CHUNK)
    d_w_up = _matmul_tn("grad_w_up", f_in, dup, FF_CHUNK)
    sent = send_grads("ffn", (d_w_gate, d_w_up, d_w_down))
    dh1, dy, dco, dao, d_g_pre_ffn, d_g_post_mix = _mix_bwd(dfin, h1, y, dh2, g_pre_ffn + sent, g_post_mix, w_out_f)
    d_w_out = jnp.concatenate([_matmul_tn("grad_w_out_conv", conv_out, dy, D_MODEL),
                               _matmul_tn("grad_w_out_attn", attn_out, dy, D_MODEL)], axis=0)
    sent = send_grads("w_out", (d_w_out,))
    do, d_attn_g = _attn_norm_bwd(o, dao, attn_g + sent)
    dq, dk, dv = _attn_bwd(qkv, do, cl)
    du_conv, d_conv_w, d_conv_small = _conv_bwd(u_conv, dco, conv_w_f, conv_b, conv_ln_g, conv_ln_b)
    grad_x, d_g_pre_mix = _in_proj_bwd(du_conv, dq, dk, dv, w_in_f, xs, g_pre_mix, dh1)
    d_w_in = jnp.concatenate([_matmul_tn("grad_w_in_conv", a, du_conv, 2 * CONV_CH),
                              _matmul_tn("grad_w_in_q", a, dq, SB_WIDTH), _matmul_tn("grad_w_in_k", a, dk, SB_WIDTH),
                              _matmul_tn("grad_w_in_v", a, dv, SB_WIDTH)], axis=1)
    return (loss_part, grad_x, d_w_in, d_conv_w, d_conv_small, d_attn_g, d_g_pre_mix, d_g_post_mix, d_g_pre_ffn,
            d_g_post_ffn)


def _cols_to_blocks(w):
    K, N = w.shape
    return jnp.transpose(w.reshape(K, N_DEV, N // N_DEV), (1, 0, 2))


def _blocks_to_cols(blocks):
    n_dev, K, n = blocks.shape
    return jnp.transpose(blocks, (1, 0, 2)).reshape(K, n_dev * n)


def kernel(x, g_pre_mix, w_in, conv_w, conv_b, conv_ln_g, conv_ln_b, attn_norm_g, w_out, g_post_mix, g_pre_ffn, w_gate, w_up, w_down, g_post_ffn, loss_target, m_g_pre_mix, m_w_in, m_conv_w, m_conv_b, m_conv_ln_g, m_conv_ln_b, m_attn_norm_g, m_w_out, m_g_post_mix, m_g_pre_ffn, m_w_gate, m_w_up, m_w_down, m_g_post_ffn, v_g_pre_mix, v_w_in, v_conv_w, v_conv_b, v_conv_ln_g, v_conv_ln_b, v_attn_norm_g, v_w_out, v_g_post_mix, v_g_pre_ffn, v_w_gate, v_w_up, v_w_down, v_g_post_ffn):
    xs = x[0]
    target = loss_target[0]
    S = xs.shape[0]
    me = 4 * lax.axis_index("x") + 2 * lax.axis_index("y") + lax.axis_index("c")
    cw_shard = conv_w.reshape(CONV_WIDTH, CONV_CH // N_DEV)
    attn_g = attn_norm_g.reshape(1, SB_WIDTH)

    gathered = _all_gather([w_in[0].astype(BF16), cw_shard])
    w_in_f = _blocks_to_cols(gathered[0])
    conv_w_f = jnp.pad(_blocks_to_cols(gathered[1]), ((0, CONV_HALO - CONV_WIDTH), (0, 0)))
    gathered_zero = gathered[2][0:1, 0:1].astype(BF16)
    late = [w_out[0].astype(BF16) + gathered_zero, w_gate[0].astype(BF16), w_up[0].astype(BF16), w_down[0].astype(BF16)]
    late_started = _exchange_start("all_gather_late_start", late, scatter=False)

    def late_weights(after):
        lands = _exchange_wait("all_gather_late_wait", late_started, False, after)
        wo, wg, wu, wd = [lax.dynamic_update_index_in_dim(land, own, me, 0) for land, own in zip(lands, late)]
        return wo.reshape(D_MODEL, D_MODEL), _blocks_to_cols(wg), _blocks_to_cols(wu), wd.reshape(D_FF, D_MODEL)

    started = {}

    def send_grads(name, grads):
        blocks = [g.reshape(N_DEV, g.shape[0] // N_DEV, g.shape[1]) if g.shape[1] == D_MODEL else _cols_to_blocks(g)
                  for g in grads]
        started[name] = (_exchange_start("reduce_scatter_" + name + "_start", blocks, scatter=True), blocks)
        return started[name][0][-1][0:1, 0:1]

    (loss_part, grad_x, d_w_in, d_conv_w, d_conv_small, d_attn_g, d_g_pre_mix, d_g_post_mix, d_g_pre_ffn,
     d_g_post_ffn) = _layer_grads(
        xs, target, g_pre_mix + late_started[-1][0:1, 0:1], w_in_f, conv_w_f, conv_b, conv_ln_g, conv_ln_b, attn_g,
        g_post_mix, g_pre_ffn, g_post_ffn, late_weights, send_grads)
    sent = send_grads("w_in", (d_w_in,))

    def reduced(name, after, shards):
        st, blocks = started[name]
        lands = _exchange_wait("reduce_scatter_" + name + "_wait", st, True, after)
        return [_sum_adamw("adamw_" + wn, land, lax.dynamic_index_in_dim(blk, me, 0, keepdims=False), w[0], m[0], v[0])
                for land, blk, (wn, w, m, v) in zip(lands, blocks, shards)]

    two = lambda t: t.reshape(2, CONV_CH)
    small_g = jnp.concatenate([
        d_conv_w,
        d_conv_small[0:3],
        d_attn_g,
        two(d_g_pre_mix), two(d_g_post_mix), two(d_g_pre_ffn), two(d_g_post_ffn),
        jnp.zeros((4, CONV_CH), F32) + sent], axis=0)
    small_g = _all_reduce_small(small_g)
    g_conv_w = lax.dynamic_slice(small_g, (0, me * (CONV_CH // N_DEV)), (CONV_WIDTH, CONV_CH // N_DEV))
    pack = lambda cb, lg, lb, ag, g1, g2, g3, g4: jnp.concatenate(
        [cb, lg, lb, ag.reshape(1, SB_WIDTH), two(g1), two(g2), two(g3), two(g4), jnp.zeros((4, CONV_CH), F32)], axis=0)
    sm_g = small_g[CONV_HALO:]
    sm_delta, sm_m, sm_v = _adamw_small(
        "adamw_small",
        pack(conv_b, conv_ln_g, conv_ln_b, attn_norm_g, g_pre_mix, g_post_mix, g_pre_ffn, g_post_ffn), sm_g,
        pack(m_conv_b, m_conv_ln_g, m_conv_ln_b, m_attn_norm_g, m_g_pre_mix, m_g_post_mix, m_g_pre_ffn, m_g_post_ffn),
        pack(v_conv_b, v_conv_ln_g, v_conv_ln_b, v_attn_norm_g, v_g_pre_mix, v_g_post_mix, v_g_pre_ffn, v_g_post_ffn))
    cw_delta, cw_m, cw_v = _adamw_small("adamw_conv_w", cw_shard, g_conv_w,
                                        m_conv_w.reshape(cw_shard.shape), v_conv_w.reshape(cw_shard.shape))

    ffn = reduced("ffn", grad_x, [("w_gate", w_gate, m_w_gate, v_w_gate), ("w_up", w_up, m_w_up, v_w_up),
                                  ("w_down", w_down, m_w_down, v_w_down)])
    big = {"w_gate": ffn[0], "w_up": ffn[1], "w_down": ffn[2],
           "w_out": reduced("w_out", ffn[2][0], [("w_out", w_out, m_w_out, v_w_out)])[0]}
    big["w_in"] = reduced("w_in", big["w_out"][0], [("w_in", w_in, m_w_in, v_w_in)])[0]

    def unpack(t):
        return {"conv_b": t[0:1], "conv_ln_g": t[1:2], "conv_ln_b": t[2:3], "attn_norm_g": t[3:4].reshape(1, SB_HEADS, SB_HEAD_DIM),
                "g_pre_mix": t[4:6].reshape(1, D_MODEL), "g_post_mix": t[6:8].reshape(1, D_MODEL),
                "g_pre_ffn": t[8:10].reshape(1, D_MODEL), "g_post_ffn": t[10:12].reshape(1, D_MODEL)}

    names = ["g_pre_mix", "w_in", "conv_w", "conv_b", "conv_ln_g", "conv_ln_b", "attn_norm_g", "w_out", "g_post_mix",
             "g_pre_ffn", "w_gate", "w_up", "w_down", "g_post_ffn"]
    kinds = []
    for idx, small in enumerate((sm_g, sm_delta, sm_m, sm_v)):
        d = unpack(small)
        d["conv_w"] = (g_conv_w, cw_delta, cw_m, cw_v)[idx].reshape(1, CONV_WIDTH, 1, CONV_CH // N_DEV)
        for n in big:
            d[n] = big[n][idx][None]
        kinds.append([d[n] for n in names])

    loss = lax.psum(loss_part[0, 0], ("x", "y", "c"))
    return (loss, grad_x[None], *kinds[0], *kinds[1], *kinds[2], *kinds[3])
```

```python
import functools
import math

import jax
import jax.numpy as jnp
from jax import lax
from jax.experimental import pallas as pl
from jax.experimental.pallas import tpu as pltpu

F32 = jnp.float32
BF16 = jnp.bfloat16
MESH = pl.DeviceIdType.MESH

N_DEV = 8
D_MODEL = 1024
CONV_CH = 512
CONV_WIDTH = 31
SB_HEADS = 8
SB_HEAD_DIM = 64
SB_WIDTH = SB_HEADS * SB_HEAD_DIM
D_FF = 2816
EPS = 1e-6
LOG2E = 1.4426950408889634
MASKED = -1e30
ADAM_LR = 0.001
ADAM_B1 = 0.9
ADAM_B2 = 0.999
ADAM_EPS = 1e-08
ADAM_WD = 0.01
ADAM_STEP = 10

SUBLANES = 8
LANES = 128
VMEM_LIMIT = 56 * 1024 * 1024
TOKEN_TILE = 512
FFN_TILE = 256
ATTN_STRIP = 32
ATTN_BLOCK = 256
CONV_HALO = 32
CONV_CHUNK = 64
FF_CHUNK = D_FF // 2


def _cparams(**kw):
    return pltpu.CompilerParams(vmem_limit_bytes=VMEM_LIMIT, **kw)


def _resident(shape):
    return pl.BlockSpec(shape, lambda *_: (0,) * len(shape), pipeline_mode=pl.Buffered(1))


def _const(shape):
    return pl.BlockSpec(shape, lambda *_: (0,) * len(shape))


def _rms_r(xf):
    return lax.rsqrt(jnp.mean(xf * xf, axis=-1, keepdims=True) + EPS)


def _rms_bwd(xf, r, g, dout):
    w = dout * g
    return r * (w - xf * (r * r) * jnp.mean(w * xf, axis=-1, keepdims=True))


def _dot(a, b):
    return jnp.dot(a, b, preferred_element_type=F32)


def _dot_nt(a, b):
    return lax.dot_general(a, b, (((1,), (1,)), ((), ())), preferred_element_type=F32)


def _dot_tn(a, b):
    return lax.dot_general(a, b, (((0,), (0,)), ((), ())), preferred_element_type=F32)


def _peer(x, y, c, k):
    px = 1 - x if (k >> 2) & 1 else x
    py = 1 - y if (k >> 1) & 1 else y
    pc = 1 - c if k & 1 else c
    return (px, py, pc), 4 * px + 2 * py + pc


def _all_gather(shards):
    n = len(shards)

    def body(*refs):
        ins, outs, done = refs[:n], refs[n:2 * n], refs[2 * n]
        send_sems, recv_sems, local_sems = refs[2 * n + 1:]
        x, y, c = lax.axis_index("x"), lax.axis_index("y"), lax.axis_index("c")
        me = 4 * x + 2 * y + c
        copies = []
        for a in range(n):
            mine = pltpu.make_async_copy(ins[a], outs[a].at[me], local_sems.at[a])
            mine.start()
            copies.append(mine)
        for k in range(1, N_DEV):
            peer, peer_block = _peer(x, y, c, k)
            for a in range(n):
                s = a * (N_DEV - 1) + k - 1
                pltpu.make_async_remote_copy(
                    src_ref=ins[a], dst_ref=outs[a].at[me], send_sem=send_sems.at[s], recv_sem=recv_sems.at[s],
                    device_id=peer, device_id_type=MESH).start()
        for k in range(1, N_DEV):
            peer, peer_block = _peer(x, y, c, k)
            for a in range(n):
                s = a * (N_DEV - 1) + k - 1
                arrived = pltpu.make_async_remote_copy(
                    src_ref=ins[a], dst_ref=outs[a].at[peer_block], send_sem=send_sems.at[s],
                    recv_sem=recv_sems.at[s], device_id=peer, device_id_type=MESH)
                arrived.wait_send()
                arrived.wait_recv()
        for mine in copies:
            mine.wait()
        done[...] = jnp.zeros_like(done)

    any_spec = pl.BlockSpec(memory_space=pl.ANY)
    return pl.pallas_call(
        body, name="all_gather_weights",
        out_shape=[jax.ShapeDtypeStruct((N_DEV,) + s.shape, s.dtype) for s in shards] + [jax.ShapeDtypeStruct((8, LANES), F32)],
        in_specs=[any_spec] * n, out_specs=[any_spec] * n + [pl.BlockSpec(memory_space=pltpu.VMEM)],
        scratch_shapes=[pltpu.SemaphoreType.DMA((n * (N_DEV - 1),)), pltpu.SemaphoreType.DMA((n * (N_DEV - 1),)),
                        pltpu.SemaphoreType.DMA((n,))],
        compiler_params=pltpu.CompilerParams(has_side_effects=True),
    )(*shards)


def _adamw(w, g, m, v):
    m = ADAM_B1 * m + (1.0 - ADAM_B1) * g
    v = ADAM_B2 * v + (1.0 - ADAM_B2) * (g * g)
    m_hat = m / (1.0 - ADAM_B1 ** ADAM_STEP)
    v_hat = v / (1.0 - ADAM_B2 ** ADAM_STEP)
    delta = -ADAM_LR * (m_hat / (jnp.sqrt(v_hat) + ADAM_EPS) + ADAM_WD * w)
    return delta, m, v


def _exchange_and_sum(src_block, recv_ref, send_sems, recv_sems, local_sem):
    x, y, c = lax.axis_index("x"), lax.axis_index("y"), lax.axis_index("c")
    me = 4 * x + 2 * y + c
    mine = pltpu.make_async_copy(src_block(me), recv_ref.at[me], local_sem)
    mine.start()
    for k in range(1, N_DEV):
        peer, peer_block = _peer(x, y, c, k)
        pltpu.make_async_remote_copy(
            src_ref=src_block(peer_block), dst_ref=recv_ref.at[me], send_sem=send_sems.at[k - 1],
            recv_sem=recv_sems.at[k - 1], device_id=peer, device_id_type=MESH).start()
    for k in range(1, N_DEV):
        peer, peer_block = _peer(x, y, c, k)
        arrived = pltpu.make_async_remote_copy(
            src_ref=src_block(peer_block), dst_ref=recv_ref.at[peer_block], send_sem=send_sems.at[k - 1],
            recv_sem=recv_sems.at[k - 1], device_id=peer, device_id_type=MESH)
        arrived.wait_send()
        arrived.wait_recv()
    mine.wait()


HBM_SPEC = pl.BlockSpec(memory_space=pltpu.HBM)
SEM_SPEC = pl.BlockSpec(memory_space=pltpu.SEMAPHORE)
DATAFLOW = pltpu.SideEffectType.DATAFLOW_SIDE_EFFECTING


def _exchange_copies(srcs, lands, send_sems, recv_sems, scatter, wait):
    x, y, c = lax.axis_index("x"), lax.axis_index("y"), lax.axis_index("c")
    me = 4 * x + 2 * y + c
    for k in range(1, N_DEV):
        peer, peer_block = _peer(x, y, c, k)
        for a in range(len(srcs)):
            s = a * (N_DEV - 1) + k - 1
            src = srcs[a].at[peer_block] if scatter else srcs[a]
            copy = pltpu.make_async_remote_copy(
                src_ref=src, dst_ref=lands[a].at[peer_block if wait else me], send_sem=send_sems.at[s],
                recv_sem=recv_sems.at[s], device_id=peer, device_id_type=MESH)
            if wait:
                copy.wait_send()
                copy.wait_recv()
            else:
                copy.start()


def _exchange_start(name, arrays, scatter):
    n = len(arrays)
    land_shapes = [a.shape if scatter else (N_DEV,) + a.shape for a in arrays]

    def body(*refs):
        _exchange_copies(refs[:n], refs[n:2 * n], refs[2 * n], refs[2 * n + 1], scatter, wait=False)
        refs[-1][...] = jnp.zeros_like(refs[-1])

    sems = pltpu.SemaphoreType.DMA((n * (N_DEV - 1),))
    hbm = lambda t: pltpu.with_memory_space_constraint(t, pltpu.HBM)
    return pl.pallas_call(
        body, name=name,
        out_shape=(sems, sems, *[pltpu.HBM(a.shape, a.dtype) for a in arrays],
                   *[pltpu.HBM(ls, a.dtype) for ls, a in zip(land_shapes, arrays)], jax.ShapeDtypeStruct((8, LANES), F32)),
        in_specs=[HBM_SPEC] * (2 * n),
        out_specs=(SEM_SPEC, SEM_SPEC, *[HBM_SPEC] * (2 * n), pl.BlockSpec(memory_space=pltpu.VMEM)),
        input_output_aliases={a: 2 + a for a in range(2 * n)},
        compiler_params=pltpu.CompilerParams(has_side_effects=DATAFLOW),
    )(*[hbm(a) for a in arrays], *[hbm(lax.empty(ls, a.dtype)) for ls, a in zip(land_shapes, arrays)])


def _exchange_wait(name, started, scatter, after):
    n = (len(started) - 3) // 2
    send_sems, recv_sems = started[0], started[1]
    arrays, lands = started[2:2 + n], started[2 + n:2 + 2 * n]

    def body(*refs):
        _exchange_copies(refs[:n], refs[n:2 * n], refs[2 * n], refs[2 * n + 1], scatter, wait=True)

    return pl.pallas_call(
        body, name=name,
        out_shape=[pltpu.HBM(t.shape, t.dtype) for t in (*arrays, *lands)],
        in_specs=[HBM_SPEC] * (2 * n) + [SEM_SPEC, SEM_SPEC, pl.BlockSpec(memory_space=pl.ANY)],
        out_specs=[HBM_SPEC] * (2 * n),
        input_output_aliases={a: a for a in range(2 * n)},
        compiler_params=pltpu.CompilerParams(has_side_effects=DATAFLOW),
    )(*arrays, *lands, send_sems, recv_sems, after)[n:]


def _sum_adamw(name, land, own, w, m, v):
    _, M, N = land.shape
    rows = math.gcd(M, 128)

    def body(land_ref, own_ref, w_ref, m_ref, v_ref, grad_ref, delta_ref, nm_ref, nv_ref):
        x, y, c = lax.axis_index("x"), lax.axis_index("y"), lax.axis_index("c")
        g = own_ref[...]
        for k in range(1, N_DEV):
            g = g + land_ref[_peer(x, y, c, k)[1]].astype(F32)
        delta, nm, nv = _adamw(w_ref[...], g, m_ref[...], v_ref[...])
        grad_ref[...] = g
        delta_ref[...] = delta
        nm_ref[...] = nm
        nv_ref[...] = nv

    row = pl.BlockSpec((rows, N), lambda i: (i, 0))
    return pl.pallas_call(
        body, name=name, grid=(M // rows,), out_shape=[jax.ShapeDtypeStruct((M, N), F32)] * 4,
        in_specs=[pl.BlockSpec((N_DEV, rows, N), lambda i: (0, i, 0)), row, row, row, row], out_specs=[row] * 4,
        compiler_params=_cparams(),
    )(land, own, w, m, v)


def _all_reduce_small(g):
    R, C = g.shape

    def body(g_ref, out_ref, recv_ref, send_sems, recv_sems, local_sem):
        _exchange_and_sum(lambda b: g_ref, recv_ref, send_sems, recv_sems, local_sem)
        total = recv_ref[0]
        for b in range(1, N_DEV):
            total = total + recv_ref[b]
        out_ref[...] = total

    vmem = pl.BlockSpec(memory_space=pltpu.VMEM)
    return pl.pallas_call(
        body, name="all_reduce_small_grads", out_shape=jax.ShapeDtypeStruct((R, C), F32),
        in_specs=[vmem], out_specs=vmem,
        scratch_shapes=[pltpu.VMEM((N_DEV, R, C), F32), pltpu.SemaphoreType.DMA((N_DEV - 1,)),
                        pltpu.SemaphoreType.DMA((N_DEV - 1,)), pltpu.SemaphoreType.DMA(())],
        compiler_params=_cparams(has_side_effects=True),
    )(g)


def _adamw_small(name, w, g, m, v):
    def body(w_ref, g_ref, m_ref, v_ref, delta_ref, nm_ref, nv_ref):
        delta, nm, nv = _adamw(w_ref[...], g_ref[...], m_ref[...], v_ref[...])
        delta_ref[...] = delta
        nm_ref[...] = nm
        nv_ref[...] = nv

    vmem = pl.BlockSpec(memory_space=pltpu.VMEM)
    return pl.pallas_call(body, name=name, out_shape=[jax.ShapeDtypeStruct(w.shape, F32)] * 3,
                          in_specs=[vmem] * 4, out_specs=[vmem] * 3)(w, g, m, v)


def _in_proj(x, g, w_in):
    S = x.shape[0]
    tm = min(TOKEN_TILE, S)
    nconv = 2 * CONV_CH

    def body(x_ref, g_ref, w_ref, a_ref, uc_ref, qkv_ref):
        xf = x_ref[...]
        a = (xf * _rms_r(xf) * g_ref[...]).astype(BF16)
        a_ref[...] = a
        uc_ref[...] = _dot(a, w_ref[:, :nconv])
        qkv_ref[:, :SB_WIDTH] = (_dot(a, w_ref[:, nconv:nconv + SB_WIDTH]) * (1.0 / math.sqrt(SB_HEAD_DIM))).astype(BF16)
        qkv_ref[:, SB_WIDTH:] = _dot(a, w_ref[:, nconv + SB_WIDTH:]).astype(BF16)

    row = lambda n: pl.BlockSpec((tm, n), lambda i: (i, 0))
    return pl.pallas_call(
        body, name="in_proj", grid=(S // tm,),
        out_shape=[jax.ShapeDtypeStruct((S, D_MODEL), BF16), jax.ShapeDtypeStruct((S, nconv), F32),
                   jax.ShapeDtypeStruct((S, 3 * SB_WIDTH), BF16)],
        in_specs=[row(D_MODEL), _const((1, D_MODEL)), _resident(w_in.shape)],
        out_specs=[row(D_MODEL), row(nconv), row(3 * SB_WIDTH)],
        compiler_params=_cparams(),
    )(x, g, w_in)


def _glu(u):
    val, gate = u[:, :CONV_CH], u[:, CONV_CH:]
    sg = jax.nn.sigmoid(gate)
    return val, sg, val * sg


def _shift_copies(ext, shifted):
    n = shifted.shape[1]
    for r in range(1, SUBLANES):
        shifted[r - 1] = ext[r:r + n, :]


def _window(ext, shifted, start, rows):
    r = start % SUBLANES
    return ext[start:start + rows, :] if r == 0 else shifted[r - 1, start - r:start - r + rows, :]


def _conv_rows(glu_ext, glu_sh, cw_ref, r0, rows):
    base = r0 + CONV_HALO - (CONV_WIDTH - 1)
    acc = cw_ref[0:1, :] * _window(glu_ext, glu_sh, base, rows)
    for w in range(1, CONV_WIDTH):
        acc = acc + cw_ref[w:w + 1, :] * _window(glu_ext, glu_sh, base + w, rows)
    return acc


def _conv_fwd(u_conv, conv_w, conv_b, ln_g, ln_b):
    S = u_conv.shape[0]
    tc = min(TOKEN_TILE, S)

    def body(u_ref, cw_ref, cb_ref, lg_ref, lb_ref, out_ref, glu_ext, glu_sh):
        i = pl.program_id(0)

        @pl.when(i == 0)
        def _():
            glu_ext[0:CONV_HALO, :] = jnp.zeros((CONV_HALO, CONV_CH), F32)

        @pl.when(i > 0)
        def _():
            glu_ext[0:CONV_HALO, :] = glu_ext[tc:tc + CONV_HALO, :]

        glu_ext[CONV_HALO:, :] = _glu(u_ref[...])[2]
        _shift_copies(glu_ext, glu_sh)
        for r0 in range(0, tc, CONV_CHUNK):
            y = _conv_rows(glu_ext, glu_sh, cw_ref, r0, CONV_CHUNK) + cb_ref[...]
            mu = jnp.mean(y, axis=-1, keepdims=True)
            yc = y - mu
            yn = yc * lax.rsqrt(jnp.mean(yc * yc, axis=-1, keepdims=True) + EPS)
            yl = yn * lg_ref[...] + lb_ref[...]
            out_ref[r0:r0 + CONV_CHUNK, :] = (yl * jax.nn.sigmoid(yl)).astype(BF16)

    return pl.pallas_call(
        body, name="conv_fwd", grid=(S // tc,),
        out_shape=jax.ShapeDtypeStruct((S, CONV_CH), BF16),
        in_specs=[pl.BlockSpec((tc, 2 * CONV_CH), lambda i: (i, 0)), _const((CONV_HALO, CONV_CH)),
                  _const((1, CONV_CH)), _const((1, CONV_CH)), _const((1, CONV_CH))],
        out_specs=pl.BlockSpec((tc, CONV_CH), lambda i: (i, 0)),
        scratch_shapes=[pltpu.VMEM((tc + CONV_HALO, CONV_CH), F32),
                        pltpu.VMEM((SUBLANES - 1, tc + CONV_HALO - SUBLANES, CONV_CH), F32)],
        compiler_params=_cparams(dimension_semantics=("arbitrary",)),
    )(u_conv, conv_w, conv_b, ln_g, ln_b)


def _head_masks():
    lane = lax.broadcasted_iota(jnp.int32, (1, LANES), 1)
    return lane < SB_HEAD_DIM


def _split_heads(t, first):
    z = jnp.zeros_like(t)
    return jnp.where(first, t, z), jnp.where(first, z, t)


def _head_sum(t, first):
    a = jnp.sum(jnp.where(first, t, 0.0), axis=-1, keepdims=True)
    b = jnp.sum(jnp.where(first, 0.0, t), axis=-1, keepdims=True)
    return a, b


def _attn_fwd(qkv, g_attn):
    S = qkv.shape[0]
    Q = min(ATTN_BLOCK, S)
    nq = S // Q
    assert nq <= LANES
    ntiles = nq * (nq + 1) // 2
    assert ntiles % 2 == 0 and ntiles >= 8
    npair = SB_WIDTH // LANES
    tiles = [(i, j) for i in range(nq) for j in range(i, -1, -1)]

    def body(q_ref, k_ref, v_ref, g_ref, o_ref, ao_ref, cl_ref, z_buf, l_buf, z2_buf, a_buf, c_buf, mask_buf):
        first = _head_masks()
        lane = lax.broadcasted_iota(jnp.int32, (1, LANES), 1)
        row = lax.broadcasted_iota(jnp.int32, (Q, Q), 0)
        col = lax.broadcasted_iota(jnp.int32, (Q, Q), 1)
        tri = (row >= col).astype(BF16)
        heads = range(2)
        strips = [slice(r0, r0 + ATTN_STRIP) for r0 in range(0, Q, ATTN_STRIP)]
        rows = lambda j: pl.ds(pl.multiple_of(j * Q, Q), Q)
        wide = lambda t: jnp.tile(t, (1, Q // LANES))
        as_int = lambda t: int(t) if isinstance(t, (bool, int)) else t.astype(jnp.int32)

        keep = col < row
        mask_buf[0, 0] = jnp.ones((Q, Q), F32)
        mask_buf[0, 1] = jnp.zeros((Q, Q), F32)
        mask_buf[1, 0] = jnp.where(keep, 1.0, 0.0)
        mask_buf[1, 1] = jnp.where(keep, 0.0, MASKED)
        o_ref[...] = jnp.zeros_like(o_ref)

        def scores(t, slot):
            i, j = t
            qh = _split_heads(q_ref[rows(i), :], first)
            kb = k_ref[rows(j), :]
            for h in heads:
                z_buf[slot, h] = _dot_nt(qh[h], kb)

        def logs(t, slot):
            i, j = t
            diag = as_int(i == j)
            for h in heads:
                for r in strips:
                    z2 = z_buf[slot, h, r, :] * LOG2E
                    l = (jnp.minimum(z2, 0.0) - jnp.log2(1.0 + jnp.exp2(-jnp.abs(z2)))) - z2
                    l_buf[slot, h, r, :] = (l * mask_buf[diag, 0, r, :]).astype(BF16)
                    z2_buf[slot, h, r, :] = z2 + mask_buf[diag, 1, r, :]

        def sums(slot):
            return tuple(_dot(l_buf[slot, h], tri) for h in heads)

        def weights(t, slot, sm):
            i, j = t
            running = jnp.where(j == i, 0.0, 1.0)
            for h in heads:
                before = c_buf[h] * running
                for r in strips:
                    a_buf[slot, h, r, :] = jnp.exp2(z2_buf[slot, h, r, :] + sm[h][r] + wide(before[r])).astype(BF16)
                hl = slice(h * LANES, (h + 1) * LANES)
                cl_ref[rows(i), hl] = jnp.where(lane == j, before, cl_ref[rows(i), hl] * running)
                c_buf[h] = before + jnp.broadcast_to(sm[h][:, 0:1], (Q, LANES))

        def values(t, slot):
            i, j = t
            vh = _split_heads(v_ref[rows(j), :], first)
            o_ref[rows(i), :] += _dot(a_buf[slot, 0], vh[0]) + _dot(a_buf[slot, 1], vh[1])

        def iteration(t, p):
            ta, tb, tc, td = t
            if ta is not None:
                scores(ta, p)
            if tc is not None:
                sm = sums(p)
            if td is not None:
                values(td, 1 - p)
            if tb is not None:
                logs(tb, 1 - p)
            if tc is not None:
                weights(tc, p, sm)

        def window(n):
            return tuple(tiles[n - k] if 0 <= n - k < ntiles else None for k in range(4))

        def following(t):
            i, j = t
            last = j == 0
            return jnp.where(last, i + 1, i), jnp.where(last, i + 1, j - 1)

        def two_iterations(_, t):
            iteration(t, 0)
            t = (following(t[0]),) + t[:3]
            iteration(t, 1)
            return (following(t[0]),) + t[:3]

        c_buf[...] = jnp.zeros_like(c_buf)
        for n in range(4):
            iteration(window(n), n % 2)
        first_window = tuple((jnp.int32(i), jnp.int32(j)) for i, j in window(4))
        lax.fori_loop(0, (ntiles - 4) // 2, two_iterations, first_window)
        for n in range(ntiles, ntiles + 3):
            iteration(window(n), n % 2)

        def head_norm(b, carry):
            o = o_ref[rows(b), :]
            sa, sb = _head_sum(o * o, first)
            r = jnp.where(first, lax.rsqrt(sa * (1.0 / SB_HEAD_DIM) + EPS), lax.rsqrt(sb * (1.0 / SB_HEAD_DIM) + EPS))
            ao_ref[rows(b), :] = (o * r * g_ref[...]).astype(BF16)
            return carry

        lax.fori_loop(0, nq, head_norm, 0)

    col_block = lambda off: pl.BlockSpec((S, LANES), lambda p: (0, off + p), pipeline_mode=pl.Buffered(1))
    out_block = lambda n: pl.BlockSpec((S, n), lambda p: (0, p), pipeline_mode=pl.Buffered(1))
    return pl.pallas_call(
        body, name="attn_fwd", grid=(npair,),
        out_shape=[jax.ShapeDtypeStruct((S, SB_WIDTH), F32), jax.ShapeDtypeStruct((S, SB_WIDTH), BF16),
                   jax.ShapeDtypeStruct((S, 2 * SB_WIDTH), F32)],
        in_specs=[col_block(0), col_block(npair), col_block(2 * npair), pl.BlockSpec((1, LANES), lambda p: (0, p))],
        out_specs=[out_block(LANES), out_block(LANES), out_block(2 * LANES)],
        scratch_shapes=[pltpu.VMEM((2, 2, Q, Q), F32), pltpu.VMEM((2, 2, Q, Q), BF16), pltpu.VMEM((2, 2, Q, Q), F32),
                        pltpu.VMEM((2, 2, Q, Q), BF16), pltpu.VMEM((2, Q, LANES), F32), pltpu.VMEM((2, 2, Q, Q), F32)],
        compiler_params=_cparams(dimension_semantics=("arbitrary",)),
    )(qkv, qkv, qkv, g_attn)


def _out_proj(conv_out, attn_out, w_out, x, g_post_mix, g_pre_ffn):
    S = x.shape[0]
    tm = min(TOKEN_TILE, S)

    def body(co_ref, ao_ref, w_ref, x_ref, g1_ref, g2_ref, y_ref, h1_ref, fin_ref):
        y = _dot(co_ref[...], w_ref[:CONV_CH, :]) + _dot(ao_ref[...], w_ref[CONV_CH:, :])
        h1 = x_ref[...] + y * _rms_r(y) * g1_ref[...]
        y_ref[...] = y
        h1_ref[...] = h1
        fin_ref[...] = (h1 * _rms_r(h1) * g2_ref[...]).astype(BF16)

    row = lambda n: pl.BlockSpec((tm, n), lambda i: (i, 0))
    return pl.pallas_call(
        body, name="out_proj", grid=(S // tm,),
        out_shape=[jax.ShapeDtypeStruct((S, D_MODEL), F32), jax.ShapeDtypeStruct((S, D_MODEL), F32),
                   jax.ShapeDtypeStruct((S, D_MODEL), BF16)],
        in_specs=[row(CONV_CH), row(SB_WIDTH), _resident(w_out.shape), row(D_MODEL), _const((1, D_MODEL)),
                  _const((1, D_MODEL))],
        out_specs=[row(D_MODEL)] * 3,
        compiler_params=_cparams(),
    )(conv_out, attn_out, w_out, x, g_post_mix, g_pre_ffn)


def _ffn_fwd_loss(f_in, w_gate, w_up, w_down, h1, target, g_post_ffn):
    S = f_in.shape[0]
    tm = min(FFN_TILE, S)
    nt = S // tm

    def body(fin_ref, wg_ref, wu_ref, wd_ref, h1_ref, t_ref, g_ref, gt_ref, up_ref, df_ref, dh2_ref, loss_ref, dg_ref,
             sq_acc):
        i = pl.program_id(0)

        @pl.when(i == 0)
        def _():
            sq_acc[...] = jnp.zeros_like(sq_acc)
            dg_ref[...] = jnp.zeros_like(dg_ref)

        fin = fin_ref[...]
        f = jnp.zeros((tm, D_MODEL), F32)
        for c0 in range(0, D_FF, FF_CHUNK):
            cols = slice(c0, c0 + FF_CHUNK)
            gt = _dot(fin, wg_ref[:, cols])
            up = _dot(fin, wu_ref[:, cols])
            gt_ref[:, cols] = gt.astype(BF16)
            up_ref[:, cols] = up.astype(BF16)
            f = f + _dot((gt * jax.nn.sigmoid(gt) * up).astype(BF16), wd_ref[cols, :])
        r = _rms_r(f)
        g = g_ref[...]
        diff = h1_ref[...] + f * r * g - t_ref[...]
        sq_acc[...] += jnp.sum(diff * diff, axis=0, keepdims=True)
        dh2 = diff * (1.0 / D_MODEL)
        dh2_ref[...] = dh2
        dg_ref[...] += jnp.sum(dh2 * f * r, axis=0, keepdims=True)
        df_ref[...] = _rms_bwd(f, r, g, dh2).astype(BF16)

        @pl.when(i == nt - 1)
        def _():
            loss_ref[...] = jnp.broadcast_to((0.5 / D_MODEL) * jnp.sum(sq_acc[...], axis=-1, keepdims=True), (1, LANES))

    row = lambda n: pl.BlockSpec((tm, n), lambda i: (i, 0))
    return pl.pallas_call(
        body, name="ffn_fwd_loss", grid=(nt,),
        out_shape=[jax.ShapeDtypeStruct((S, D_FF), BF16), jax.ShapeDtypeStruct((S, D_FF), BF16),
                   jax.ShapeDtypeStruct((S, D_MODEL), BF16), jax.ShapeDtypeStruct((S, D_MODEL), F32),
                   jax.ShapeDtypeStruct((1, LANES), F32), jax.ShapeDtypeStruct((1, D_MODEL), F32)],
        in_specs=[row(D_MODEL), _resident(w_gate.shape), _resident(w_up.shape), _resident(w_down.shape),
                  row(D_MODEL), row(D_MODEL), _const((1, D_MODEL))],
        out_specs=[row(D_FF), row(D_FF), row(D_MODEL), row(D_MODEL), _const((1, LANES)), _const((1, D_MODEL))],
        scratch_shapes=[pltpu.VMEM((1, D_MODEL), F32)],
        compiler_params=_cparams(dimension_semantics=("arbitrary",)),
    )(f_in, w_gate, w_up, w_down, h1, target, g_post_ffn)


def _ffn_bwd(df, gt, up, w_gate, w_up, w_down):
    S = df.shape[0]
    tm = min(FFN_TILE, S)

    def body(df_ref, gt_ref, up_ref, wg_ref, wu_ref, wd_ref, dgt_ref, dup_ref, act_ref, dfin_ref):
        df = df_ref[...]
        dfin = jnp.zeros((tm, D_MODEL), F32)
        for c0 in range(0, D_FF, FF_CHUNK):
            cols = slice(c0, c0 + FF_CHUNK)
            dact = _dot_nt(df, wd_ref[cols, :])
            gt = gt_ref[:, cols].astype(F32)
            up = up_ref[:, cols].astype(F32)
            s = jax.nn.sigmoid(gt)
            silu = gt * s
            dgt = (dact * up * (s * (1.0 + gt * (1.0 - s)))).astype(BF16)
            dup = (dact * silu).astype(BF16)
            act_ref[:, cols] = (silu * up).astype(BF16)
            dgt_ref[:, cols] = dgt
            dup_ref[:, cols] = dup
            dfin = dfin + _dot_nt(dgt, wg_ref[:, cols]) + _dot_nt(dup, wu_ref[:, cols])
        dfin_ref[...] = dfin

    row = lambda n: pl.BlockSpec((tm, n), lambda i: (i, 0))
    return pl.pallas_call(
        body, name="ffn_bwd", grid=(S // tm,),
        out_shape=[jax.ShapeDtypeStruct((S, D_FF), BF16)] * 3 + [jax.ShapeDtypeStruct((S, D_MODEL), F32)],
        in_specs=[row(D_MODEL), row(D_FF), row(D_FF), _resident(w_gate.shape), _resident(w_up.shape),
                  _resident(w_down.shape)],
        out_specs=[row(D_FF)] * 3 + [row(D_MODEL)],
        compiler_params=_cparams(),
    )(df, gt, up, w_gate, w_up, w_down)


def _matmul_tn(name, x, y, tn):
    S, K = x.shape
    N = y.shape[1]
    ts = min(TOKEN_TILE, S)

    def body(x_ref, y_ref, o_ref):
        @pl.when(pl.program_id(1) == 0)
        def _():
            o_ref[...] = jnp.zeros_like(o_ref)

        o_ref[...] += _dot_tn(x_ref[...].astype(BF16), y_ref[...].astype(BF16))

    return pl.pallas_call(
        body, name=name, grid=(N // tn, S // ts),
        out_shape=jax.ShapeDtypeStruct((K, N), F32),
        in_specs=[pl.BlockSpec((ts, K), lambda n, s: (s, 0)), pl.BlockSpec((ts, tn), lambda n, s: (s, n))],
        out_specs=pl.BlockSpec((K, tn), lambda n, s: (0, n)),
        compiler_params=_cparams(dimension_semantics=("arbitrary", "arbitrary")),
    )(x, y)


def _mix_bwd(dfin, h1, y, dh2, g_pre_ffn, g_post_mix, w_out):
    S = dfin.shape[0]
    tm = min(TOKEN_TILE, S)

    def body(dfin_ref, h1_ref, y_ref, dh2_ref, g2_ref, g1_ref, w_ref, dh1_ref, dy_ref, dco_ref, dao_ref, dg2_ref, dg1_ref):
        @pl.when(pl.program_id(0) == 0)
        def _():
            dg2_ref[...] = jnp.zeros_like(dg2_ref)
            dg1_ref[...] = jnp.zeros_like(dg1_ref)

        h1, dfin = h1_ref[...], dfin_ref[...]
        r2 = _rms_r(h1)
        dh1 = dh2_ref[...] + _rms_bwd(h1, r2, g2_ref[...], dfin)
        dg2_ref[...] += jnp.sum(dfin * h1 * r2, axis=0, keepdims=True)
        y = y_ref[...]
        r1 = _rms_r(y)
        dy = _rms_bwd(y, r1, g1_ref[...], dh1).astype(BF16)
        dg1_ref[...] += jnp.sum(dh1 * y * r1, axis=0, keepdims=True)
        dh1_ref[...] = dh1
        dy_ref[...] = dy
        dco_ref[...] = _dot_nt(dy, w_ref[:CONV_CH, :])
        dao_ref[...] = _dot_nt(dy, w_ref[CONV_CH:, :])

    row = lambda n: pl.BlockSpec((tm, n), lambda i: (i, 0))
    return pl.pallas_call(
        body, name="mix_bwd", grid=(S // tm,),
        out_shape=[jax.ShapeDtypeStruct((S, D_MODEL), F32), jax.ShapeDtypeStruct((S, D_MODEL), BF16),
                   jax.ShapeDtypeStruct((S, CONV_CH), F32), jax.ShapeDtypeStruct((S, SB_WIDTH), F32),
                   jax.ShapeDtypeStruct((1, D_MODEL), F32), jax.ShapeDtypeStruct((1, D_MODEL), F32)],
        in_specs=[row(D_MODEL)] * 4 + [_const((1, D_MODEL)), _const((1, D_MODEL)), _resident(w_out.shape)],
        out_specs=[row(D_MODEL), row(D_MODEL), row(CONV_CH), row(SB_WIDTH), _const((1, D_MODEL)), _const((1, D_MODEL))],
        compiler_params=_cparams(dimension_semantics=("arbitrary",)),
    )(dfin, h1, y, dh2, g_pre_ffn, g_post_mix, w_out)


def _attn_norm_bwd(o, dao, g_attn):
    S = o.shape[0]
    tm = min(TOKEN_TILE, S)
    inv_dh = 1.0 / SB_HEAD_DIM

    def body(o_ref, dao_ref, g_ref, do_ref, dg_ref):
        @pl.when(pl.program_id(0) == 0)
        def _():
            dg_ref[...] = jnp.zeros_like(dg_ref)

        first = _head_masks()
        for p in range(SB_WIDTH // LANES):
            cols = slice(p * LANES, (p + 1) * LANES)
            o, dao, g = o_ref[:, cols], dao_ref[:, cols], g_ref[:, cols]
            sa, sb = _head_sum(o * o, first)
            r = jnp.where(first, lax.rsqrt(sa * inv_dh + EPS), lax.rsqrt(sb * inv_dh + EPS))
            w = dao * g
            wa, wb = _head_sum(w * o, first)
            do_ref[:, cols] = (r * (w - o * (r * r) * (jnp.where(first, wa, wb) * inv_dh))).astype(BF16)
            dg_ref[:, cols] += jnp.sum(dao * o * r, axis=0, keepdims=True)

    row = pl.BlockSpec((tm, SB_WIDTH), lambda i: (i, 0))
    return pl.pallas_call(
        body, name="attn_norm_bwd", grid=(S // tm,),
        out_shape=[jax.ShapeDtypeStruct((S, SB_WIDTH), BF16), jax.ShapeDtypeStruct((1, SB_WIDTH), F32)],
        in_specs=[row, row, _const((1, SB_WIDTH))], out_specs=[row, _const((1, SB_WIDTH))],
        compiler_params=_cparams(dimension_semantics=("arbitrary",)),
    )(o, dao, g_attn)


def _attn_bwd(qkv, do, cl):
    S = qkv.shape[0]
    Q = min(ATTN_BLOCK, S)
    nq = S // Q
    ntiles = nq * (nq + 1) // 2
    assert ntiles % 2 == 0 and ntiles >= 8
    npair = SB_WIDTH // LANES
    tiles = [(i, j) for i in range(nq) for j in range(i + 1)]

    def body(q_ref, k_ref, v_ref, do_ref, cl_ref, dq_ref, dk_ref, dv_ref,
             z_buf, lb_buf, be_buf, g_buf, l_buf, a_buf, gb_buf, dz_buf, pg_buf, mask_buf):
        first = _head_masks()
        lane = lax.broadcasted_iota(jnp.int32, (1, LANES), 1)
        row = lax.broadcasted_iota(jnp.int32, (Q, Q), 0)
        col = lax.broadcasted_iota(jnp.int32, (Q, Q), 1)
        tri = (row > col).astype(BF16)
        tpi = (row <= col).astype(BF16)
        heads = range(2)
        strips = [slice(r0, r0 + ATTN_STRIP) for r0 in range(0, Q, ATTN_STRIP)]
        rows = lambda j: pl.ds(pl.multiple_of(j * Q, Q), Q)
        wide = lambda t: jnp.tile(t, (1, Q // LANES))
        as_int = lambda t: int(t) if isinstance(t, (bool, int)) else t.astype(jnp.int32)

        keep = col < row
        mask_buf[0, 0] = jnp.ones((Q, Q), F32)
        mask_buf[0, 1] = jnp.zeros((Q, Q), F32)
        mask_buf[1, 0] = jnp.where(keep, 1.0, 0.0)
        mask_buf[1, 1] = jnp.where(keep, 0.0, MASKED)
        dq_ref[...] = jnp.zeros_like(dq_ref)
        dk_ref[...] = jnp.zeros_like(dk_ref)
        dv_ref[...] = jnp.zeros_like(dv_ref)

        def scores(t, slot):
            i, j = t
            qh = _split_heads(q_ref[rows(i), :], first)
            kb = k_ref[rows(j), :]
            for h in heads:
                z_buf[slot, h] = _dot_nt(qh[h], kb)

        def logs(t, slot):
            i, j = t
            diag = as_int(i == j)
            for h in heads:
                for r in strips:
                    z2 = z_buf[slot, h, r, :] * LOG2E
                    lb = jnp.minimum(z2, 0.0) - jnp.log2(1.0 + jnp.exp2(-jnp.abs(z2)))
                    l = (lb - z2) * mask_buf[diag, 0, r, :]
                    l_buf[slot, h, r, :] = l.astype(BF16)
                    lb_buf[slot, h, r, :] = lb + mask_buf[diag, 1, r, :]

        def sums(t, slot):
            i, j = t
            doh = _split_heads(do_ref[rows(i), :], first)
            vb = v_ref[rows(j), :]
            return (tuple(_dot(l_buf[slot, h], tri) for h in heads), tuple(_dot_nt(doh[h], vb) for h in heads))

        def weights(t, slot, sm, da):
            i, j = t
            for h in heads:
                c = jnp.sum(jnp.where(lane == j, cl_ref[rows(i), h * LANES:(h + 1) * LANES], 0.0), axis=-1, keepdims=True)
                c = jnp.broadcast_to(c, (Q, LANES))
                for r in strips:
                    lb = lb_buf[slot, h, r, :]
                    a = jnp.exp2(lb + sm[h][r] + wide(c[r]))
                    g = da[h][r] * a
                    a_buf[slot, h, r, :] = a.astype(BF16)
                    be_buf[slot, h, r, :] = jnp.exp2(lb)
                    g_buf[slot, h, r, :] = g
                    gb_buf[slot, h, r, :] = g.astype(BF16)

        def prefix(t, slot):
            i, j = t
            doh = _split_heads(do_ref[rows(i), :], first)
            dv_ref[rows(j), :] += _dot_tn(a_buf[slot, 0], doh[0]) + _dot_tn(a_buf[slot, 1], doh[1])
            return tuple(_dot(gb_buf[slot, h], tpi) for h in heads)

        def dscores(t, slot, pm):
            i, j = t
            for h in heads:
                pg = pg_buf[h] * jnp.where(j == 0, 0.0, 1.0)
                for r in strips:
                    dz = g_buf[slot, h, r, :] - be_buf[slot, h, r, :] * (pm[h][r] + wide(pg[r]))
                    dz_buf[slot, h, r, :] = dz.astype(BF16)
                pg_buf[h] = pg + jnp.broadcast_to(pm[h][:, Q - 1:Q], (Q, LANES))

        def grads(t, slot):
            i, j = t
            qh = _split_heads(q_ref[rows(i), :], first)
            kh = _split_heads(k_ref[rows(j), :], first)
            dq_ref[rows(i), :] += _dot(dz_buf[slot, 0], kh[0]) + _dot(dz_buf[slot, 1], kh[1])
            dk_ref[rows(j), :] += _dot_tn(dz_buf[slot, 0], qh[0]) + _dot_tn(dz_buf[slot, 1], qh[1])

        def iteration(t, p):
            ta, tb, tc, td, te = t
            if ta is not None:
                scores(ta, p)
            if tc is not None:
                sm, da = sums(tc, p)
            if td is not None:
                pm = prefix(td, 1 - p)
            if te is not None:
                grads(te, p)
            if tb is not None:
                logs(tb, 1 - p)
            if tc is not None:
                weights(tc, p, sm, da)
            if td is not None:
                dscores(td, 1 - p, pm)

        def window(n):
            return tuple(tiles[n - k] if 0 <= n - k < ntiles else None for k in range(5))

        def following(t):
            i, j = t
            last = j == i
            return jnp.where(last, i + 1, i), jnp.where(last, 0, j + 1)

        def two_iterations(_, t):
            iteration(t, 0)
            t = (following(t[0]),) + t[:4]
            iteration(t, 1)
            return (following(t[0]),) + t[:4]

        pg_buf[...] = jnp.zeros_like(pg_buf)
        for n in range(4):
            iteration(window(n), n % 2)
        first_window = tuple((jnp.int32(i), jnp.int32(j)) for i, j in window(4))
        lax.fori_loop(0, (ntiles - 4) // 2, two_iterations, first_window)
        for n in range(ntiles, ntiles + 4):
            iteration(window(n), n % 2)
        dq_ref[...] = dq_ref[...] * (1.0 / math.sqrt(SB_HEAD_DIM))

    col_block = lambda off: pl.BlockSpec((S, LANES), lambda p: (0, off + p), pipeline_mode=pl.Buffered(1))
    return pl.pallas_call(
        body, name="attn_bwd", grid=(npair,),
        out_shape=[jax.ShapeDtypeStruct((S, SB_WIDTH), F32)] * 3,
        in_specs=[col_block(0), col_block(npair), col_block(2 * npair), col_block(0),
                  pl.BlockSpec((S, 2 * LANES), lambda p: (0, p), pipeline_mode=pl.Buffered(1))],
        out_specs=[pl.BlockSpec((S, LANES), lambda p: (0, p), pipeline_mode=pl.Buffered(1))] * 3,
        scratch_shapes=[pltpu.VMEM((2, 2, Q, Q), F32)] * 4 + [pltpu.VMEM((2, 2, Q, Q), BF16)] * 4
        + [pltpu.VMEM((2, Q, LANES), F32), pltpu.VMEM((2, 2, Q, Q), F32)],
        compiler_params=_cparams(dimension_semantics=("arbitrary",)),
    )(qkv, qkv, qkv, do, cl)


def _conv_bwd(u_conv, dco, conv_w, conv_b, ln_g, ln_b):
    S = u_conv.shape[0]
    tc = min(TOKEN_TILE, S)
    nt = S // tc
    per = tc // CONV_HALO
    groups = CONV_CHUNK // 8

    def body(u_ref, halo_ref, dco_ref, cw_ref, cb_ref, lg_ref, lb_ref, du_ref, dcw_ref, dsm_ref, glu_ext, dyc_ext, sg_buf,
             dcw_acc, dsm_acc, glu_sh, dyc_sh):
        i = pl.program_id(0)
        ti = nt - 1 - i

        @pl.when(i == 0)
        def _():
            dyc_ext[tc:, :] = jnp.zeros((CONV_HALO, CONV_CH), F32)
            dcw_acc[...] = jnp.zeros_like(dcw_acc)
            dsm_acc[...] = jnp.zeros_like(dsm_acc)

        @pl.when(i > 0)
        def _():
            dyc_ext[tc:, :] = dyc_ext[0:CONV_HALO, :]

        glu_ext[0:CONV_HALO, :] = jnp.where(ti > 0, _glu(halo_ref[...])[2], 0.0)
        val, sg, glu = _glu(u_ref[...])
        glu_ext[CONV_HALO:, :] = glu
        sg_buf[...] = sg
        _shift_copies(glu_ext, glu_sh)

        dcb = jnp.zeros((8, CONV_CH), F32)
        dlg = jnp.zeros((8, CONV_CH), F32)
        dlb = jnp.zeros((8, CONV_CH), F32)
        fold = lambda t: jnp.sum(t.reshape(groups, 8, CONV_CH), axis=0)
        for r0 in range(0, tc, CONV_CHUNK):
            y = _conv_rows(glu_ext, glu_sh, cw_ref, r0, CONV_CHUNK) + cb_ref[...]
            mu = jnp.mean(y, axis=-1, keepdims=True)
            yc = y - mu
            rstd = lax.rsqrt(jnp.mean(yc * yc, axis=-1, keepdims=True) + EPS)
            yn = yc * rstd
            yl = yn * lg_ref[...] + lb_ref[...]
            s = jax.nn.sigmoid(yl)
            dyl = dco_ref[r0:r0 + CONV_CHUNK, :] * (s * (1.0 + yl * (1.0 - s)))
            dlg = dlg + fold(dyl * yn)
            dlb = dlb + fold(dyl)
            wv = dyl * lg_ref[...]
            dyc = rstd * (wv - jnp.mean(wv, axis=-1, keepdims=True) - yn * jnp.mean(wv * yn, axis=-1, keepdims=True))
            dcb = dcb + fold(dyc)
            dyc_ext[r0:r0 + CONV_CHUNK, :] = dyc
        dsm_acc[0:8, :] += dcb
        dsm_acc[8:16, :] += dlg
        dsm_acc[16:24, :] += dlb
        _shift_copies(dyc_ext, dyc_sh)

        for r0 in range(0, tc, CONV_CHUNK):
            dyc = dyc_ext[r0:r0 + CONV_CHUNK, :]
            dglu = jnp.zeros((CONV_CHUNK, CONV_CH), F32)
            base = r0 + CONV_HALO - (CONV_WIDTH - 1)
            for w in range(CONV_WIDTH):
                back = r0 + (CONV_WIDTH - 1) - w
                dglu = dglu + cw_ref[w:w + 1, :] * _window(dyc_ext, dyc_sh, back, CONV_CHUNK)
                dcw_acc[8 * w:8 * w + 8, :] += fold(dyc * _window(glu_ext, glu_sh, base + w, CONV_CHUNK))
            sg = sg_buf[r0:r0 + CONV_CHUNK, :]
            v = u_ref[r0:r0 + CONV_CHUNK, :CONV_CH]
            du_ref[r0:r0 + CONV_CHUNK, :CONV_CH] = (dglu * sg).astype(BF16)
            du_ref[r0:r0 + CONV_CHUNK, CONV_CH:] = (dglu * v * sg * (1.0 - sg)).astype(BF16)

        @pl.when(i == nt - 1)
        def _():
            for w in range(CONV_WIDTH):
                dcw_ref[w:w + 1, :] = jnp.sum(dcw_acc[8 * w:8 * w + 8, :], axis=0, keepdims=True)
            dcw_ref[CONV_WIDTH:, :] = jnp.zeros((CONV_HALO - CONV_WIDTH, CONV_CH), F32)
            for k in range(3):
                dsm_ref[k:k + 1, :] = jnp.sum(dsm_acc[8 * k:8 * k + 8, :], axis=0, keepdims=True)
            dsm_ref[3:, :] = jnp.zeros((5, CONV_CH), F32)

    return pl.pallas_call(
        body, name="conv_bwd", grid=(nt,),
        out_shape=[jax.ShapeDtypeStruct((S, 2 * CONV_CH), BF16), jax.ShapeDtypeStruct((CONV_HALO, CONV_CH), F32),
                   jax.ShapeDtypeStruct((8, CONV_CH), F32)],
        in_specs=[pl.BlockSpec((tc, 2 * CONV_CH), lambda i: (nt - 1 - i, 0)),
                  pl.BlockSpec((CONV_HALO, 2 * CONV_CH), lambda i: (jnp.maximum((nt - 1 - i) * per - 1, 0), 0)),
                  pl.BlockSpec((tc, CONV_CH), lambda i: (nt - 1 - i, 0)),
                  _const((CONV_HALO, CONV_CH)), _const((1, CONV_CH)), _const((1, CONV_CH)), _const((1, CONV_CH))],
        out_specs=[pl.BlockSpec((tc, 2 * CONV_CH), lambda i: (nt - 1 - i, 0)), _const((CONV_HALO, CONV_CH)),
                   _const((8, CONV_CH))],
        scratch_shapes=[pltpu.VMEM((tc + CONV_HALO, CONV_CH), F32), pltpu.VMEM((tc + CONV_HALO, CONV_CH), F32),
                        pltpu.VMEM((tc, CONV_CH), F32), pltpu.VMEM((8 * CONV_HALO, CONV_CH), F32),
                        pltpu.VMEM((24, CONV_CH), F32)]
        + [pltpu.VMEM((SUBLANES - 1, tc + CONV_HALO - SUBLANES, CONV_CH), F32)] * 2,
        compiler_params=_cparams(dimension_semantics=("arbitrary",)),
    )(u_conv, u_conv, dco, conv_w, conv_b, ln_g, ln_b)


def _in_proj_bwd(du_conv, dq, dk, dv, w_in, x, g, dh1):
    S = x.shape[0]
    tm = min(TOKEN_TILE, S)
    nconv = 2 * CONV_CH

    def body(duc_ref, dq_ref, dk_ref, dv_ref, w_ref, x_ref, g_ref, dh1_ref, dx_ref, dg_ref):
        @pl.when(pl.program_id(0) == 0)
        def _():
            dg_ref[...] = jnp.zeros_like(dg_ref)

        da = _dot_nt(duc_ref[...], w_ref[:, :nconv])
        for n, ref in enumerate((dq_ref, dk_ref, dv_ref)):
            c0 = nconv + n * SB_WIDTH
            da = da + _dot_nt(ref[...].astype(BF16), w_ref[:, c0:c0 + SB_WIDTH])
        xf = x_ref[...]
        r = _rms_r(xf)
        dx_ref[...] = dh1_ref[...] + _rms_bwd(xf, r, g_ref[...], da)
        dg_ref[...] += jnp.sum(da * xf * r, axis=0, keepdims=True)

    row = lambda n: pl.BlockSpec((tm, n), lambda i: (i, 0))
    return pl.pallas_call(
        body, name="in_proj_bwd", grid=(S // tm,),
        out_shape=[jax.ShapeDtypeStruct((S, D_MODEL), F32), jax.ShapeDtypeStruct((1, D_MODEL), F32)],
        in_specs=[row(nconv), row(SB_WIDTH), row(SB_WIDTH), row(SB_WIDTH), _resident(w_in.shape), row(D_MODEL),
                  _const((1, D_MODEL)), row(D_MODEL)],
        out_specs=[row(D_MODEL), _const((1, D_MODEL))],
        compiler_params=_cparams(dimension_semantics=("arbitrary",)),
    )(du_conv, dq, dk, dv, w_in, x, g, dh1)


def _layer_grads(xs, target, g_pre_mix, w_in_f, conv_w_f, conv_b, conv_ln_g, conv_ln_b, attn_g, g_post_mix, g_pre_ffn,
                 g_post_ffn, late_weights, send_grads):
    a, u_conv, qkv = _in_proj(xs, g_pre_mix, w_in_f)
    conv_out = _conv_fwd(u_conv, conv_w_f, conv_b, conv_ln_g, conv_ln_b)
    o, attn_out, cl = _attn_fwd(qkv, attn_g)
    w_out_f, w_gate_f, w_up_f, w_down_f = late_weights(attn_out)
    y, h1, f_in = _out_proj(conv_out, attn_out, w_out_f, xs, g_post_mix, g_pre_ffn)
    gt, up, df, dh2, loss_part, d_g_post_ffn = _ffn_fwd_loss(f_in, w_gate_f, w_up_f, w_down_f, h1, target, g_post_ffn)

    dgt, dup, act, dfin = _ffn_bwd(df, gt, up, w_gate_f, w_up_f, w_down_f)
    d_w_down = _matmul_tn("grad_w_down", act, df, 512)
    d_w_gate = _matmul_tn("grad_w_gate", f_in, dgt, FF_CHUNK)
    d_w_up = _matmul_tn("grad_w_up", f_in, dup, FF_CHUNK)
    sent = send_grads("ffn", (d_w_gate, d_w_up, d_w_down))
    dh1, dy, dco, dao, d_g_pre_ffn, d_g_post_mix = _mix_bwd(dfin, h1, y, dh2, g_pre_ffn + sent, g_post_mix, w_out_f)
    d_w_out = jnp.concatenate([_matmul_tn("grad_w_out_conv", conv_out, dy, D_MODEL),
                               _matmul_tn("grad_w_out_attn", attn_out, dy, D_MODEL)], axis=0)
    sent = send_grads("w_out", (d_w_out,))
    do, d_attn_g = _attn_norm_bwd(o, dao, attn_g + sent)
    dq, dk, dv = _attn_bwd(qkv, do, cl)
    du_conv, d_conv_w, d_conv_small = _conv_bwd(u_conv, dco, conv_w_f, conv_b, conv_ln_g, conv_ln_b)
    grad_x, d_g_pre_mix = _in_proj_bwd(du_conv, dq, dk, dv, w_in_f, xs, g_pre_mix, dh1)
    d_w_in = jnp.concatenate([_matmul_tn("grad_w_in_conv", a, du_conv, 2 * CONV_CH),
                              _matmul_tn("grad_w_in_q", a, dq, SB_WIDTH), _matmul_tn("grad_w_in_k", a, dk, SB_WIDTH),
                              _matmul_tn("grad_w_in_v", a, dv, SB_WIDTH)], axis=1)
    return (loss_part, grad_x, d_w_in, d_conv_w, d_conv_small, d_attn_g, d_g_pre_mix, d_g_post_mix, d_g_pre_ffn,
            d_g_post_ffn)


def _cols_to_blocks(w):
    K, N = w.shape
    return jnp.transpose(w.reshape(K, N_DEV, N // N_DEV), (1, 0, 2))


def _blocks_to_cols(blocks):
    n_dev, K, n = blocks.shape
    return jnp.transpose(blocks, (1, 0, 2)).reshape(K, n_dev * n)


def kernel(x, g_pre_mix, w_in, conv_w, conv_b, conv_ln_g, conv_ln_b, attn_norm_g, w_out, g_post_mix, g_pre_ffn, w_gate, w_up, w_down, g_post_ffn, loss_target, m_g_pre_mix, m_w_in, m_conv_w, m_conv_b, m_conv_ln_g, m_conv_ln_b, m_attn_norm_g, m_w_out, m_g_post_mix, m_g_pre_ffn, m_w_gate, m_w_up, m_w_down, m_g_post_ffn, v_g_pre_mix, v_w_in, v_conv_w, v_conv_b, v_conv_ln_g, v_conv_ln_b, v_attn_norm_g, v_w_out, v_g_post_mix, v_g_pre_ffn, v_w_gate, v_w_up, v_w_down, v_g_post_ffn):
    xs = x[0]
    target = loss_target[0]
    S = xs.shape[0]
    me = 4 * lax.axis_index("x") + 2 * lax.axis_index("y") + lax.axis_index("c")
    cw_shard = conv_w.reshape(CONV_WIDTH, CONV_CH // N_DEV)
    attn_g = attn_norm_g.reshape(1, SB_WIDTH)

    gathered = _all_gather([w_in[0].astype(BF16), cw_shard])
    w_in_f = _blocks_to_cols(gathered[0])
    conv_w_f = jnp.pad(_blocks_to_cols(gathered[1]), ((0, CONV_HALO - CONV_WIDTH), (0, 0)))
    gathered_zero = gathered[2][0:1, 0:1].astype(BF16)
    late = [w_out[0].astype(BF16) + gathered_zero, w_gate[0].astype(BF16), w_up[0].astype(BF16), w_down[0].astype(BF16)]
    late_started = _exchange_start("all_gather_late_start", late, scatter=False)

    def late_weights(after):
        lands = _exchange_wait("all_gather_late_wait", late_started, False, after)
        wo, wg, wu, wd = [lax.dynamic_update_index_in_dim(land, own, me, 0) for land, own in zip(lands, late)]
        return wo.reshape(D_MODEL, D_MODEL), _blocks_to_cols(wg), _blocks_to_cols(wu), wd.reshape(D_FF, D_MODEL)

    started = {}

    def send_grads(name, grads, payload=F32):
        blocks = [g.reshape(N_DEV, g.shape[0] // N_DEV, g.shape[1]) if g.shape[1] == D_MODEL else _cols_to_blocks(g)
                  for g in grads]
        sent = _exchange_start("reduce_scatter_" + name + "_start", [b.astype(payload) for b in blocks], scatter=True)
        started[name] = (sent, blocks)
        return sent[-1][0:1, 0:1]

    (loss_part, grad_x, d_w_in, d_conv_w, d_conv_small, d_attn_g, d_g_pre_mix, d_g_post_mix, d_g_pre_ffn,
     d_g_post_ffn) = _layer_grads(
        xs, target, g_pre_mix + late_started[-1][0:1, 0:1], w_in_f, conv_w_f, conv_b, conv_ln_g, conv_ln_b, attn_g,
        g_post_mix, g_pre_ffn, g_post_ffn, late_weights, send_grads)
    sent = send_grads("w_in", (d_w_in,), payload=BF16)

    def reduced(name, after, shards):
        st, blocks = started[name]
        lands = _exchange_wait("reduce_scatter_" + name + "_wait", st, True, after)
        return [_sum_adamw("adamw_" + wn, land, lax.dynamic_index_in_dim(blk, me, 0, keepdims=False), w[0], m[0], v[0])
                for land, blk, (wn, w, m, v) in zip(lands, blocks, shards)]

    two = lambda t: t.reshape(2, CONV_CH)
    small_g = jnp.concatenate([
        d_conv_w,
        d_conv_small[0:3],
        d_attn_g,
        two(d_g_pre_mix), two(d_g_post_mix), two(d_g_pre_ffn), two(d_g_post_ffn),
        jnp.zeros((4, CONV_CH), F32) + sent], axis=0)
    small_g = _all_reduce_small(small_g)
    g_conv_w = lax.dynamic_slice(small_g, (0, me * (CONV_CH // N_DEV)), (CONV_WIDTH, CONV_CH // N_DEV))
    pack = lambda cb, lg, lb, ag, g1, g2, g3, g4: jnp.concatenate(
        [cb, lg, lb, ag.reshape(1, SB_WIDTH), two(g1), two(g2), two(g3), two(g4), jnp.zeros((4, CONV_CH), F32)], axis=0)
    sm_g = small_g[CONV_HALO:]
    sm_delta, sm_m, sm_v = _adamw_small(
        "adamw_small",
        pack(conv_b, conv_ln_g, conv_ln_b, attn_norm_g, g_pre_mix, g_post_mix, g_pre_ffn, g_post_ffn), sm_g,
        pack(m_conv_b, m_conv_ln_g, m_conv_ln_b, m_attn_norm_g, m_g_pre_mix, m_g_post_mix, m_g_pre_ffn, m_g_post_ffn),
        pack(v_conv_b, v_conv_ln_g, v_conv_ln_b, v_attn_norm_g, v_g_pre_mix, v_g_post_mix, v_g_pre_ffn, v_g_post_ffn))
    cw_delta, cw_m, cw_v = _adamw_small("adamw_conv_w", cw_shard, g_conv_w,
                                        m_conv_w.reshape(cw_shard.shape), v_conv_w.reshape(cw_shard.shape))

    ffn = reduced("ffn", grad_x, [("w_gate", w_gate, m_w_gate, v_w_gate), ("w_up", w_up, m_w_up, v_w_up),
                                  ("w_down", w_down, m_w_down, v_w_down)])
    big = {"w_gate": ffn[0], "w_up": ffn[1], "w_down": ffn[2],
           "w_out": reduced("w_out", ffn[2][0], [("w_out", w_out, m_w_out, v_w_out)])[0]}
    big["w_in"] = reduced("w_in", big["w_out"][0], [("w_in", w_in, m_w_in, v_w_in)])[0]

    def unpack(t):
        return {"conv_b": t[0:1], "conv_ln_g": t[1:2], "conv_ln_b": t[2:3], "attn_norm_g": t[3:4].reshape(1, SB_HEADS, SB_HEAD_DIM),
                "g_pre_mix": t[4:6].reshape(1, D_MODEL), "g_post_mix": t[6:8].reshape(1, D_MODEL),
                "g_pre_ffn": t[8:10].reshape(1, D_MODEL), "g_post_ffn": t[10:12].reshape(1, D_MODEL)}

    names = ["g_pre_mix", "w_in", "conv_w", "conv_b", "conv_ln_g", "conv_ln_b", "attn_norm_g", "w_out", "g_post_mix",
             "g_pre_ffn", "w_gate", "w_up", "w_down", "g_post_ffn"]
    kinds = []
    for idx, small in enumerate((sm_g, sm_delta, sm_m, sm_v)):
        d = unpack(small)
        d["conv_w"] = (g_conv_w, cw_delta, cw_m, cw_v)[idx].reshape(1, CONV_WIDTH, 1, CONV_CH // N_DEV)
        for n in big:
            d[n] = big[n][idx][None]
        kinds.append([d[n] for n in names])

    loss = lax.psum(loss_part[0, 0], ("x", "y", "c"))
    return (loss, grad_x[None], *kinds[0], *kinds[1], *kinds[2], *kinds[3])
```

```python
import functools
import math

import jax
import jax.numpy as jnp
from jax import lax
from jax.experimental import pallas as pl
from jax.experimental.pallas import tpu as pltpu

F32 = jnp.float32
BF16 = jnp.bfloat16
MESH = pl.DeviceIdType.MESH

N_DEV = 8
D_MODEL = 1024
CONV_CH = 512
CONV_WIDTH = 31
SB_HEADS = 8
SB_HEAD_DIM = 64
SB_WIDTH = SB_HEADS * SB_HEAD_DIM
D_FF = 2816
EPS = 1e-6
LOG2E = 1.4426950408889634
MASKED = -1e30
ADAM_LR = 0.001
ADAM_B1 = 0.9
ADAM_B2 = 0.999
ADAM_EPS = 1e-08
ADAM_WD = 0.01
ADAM_STEP = 10

SUBLANES = 8
LANES = 128
VMEM_LIMIT = 56 * 1024 * 1024
TOKEN_TILE = 512
FFN_TILE = 256
ATTN_UNROLL = 4
ATTN_STRIP = 32
ATTN_BLOCK = 256
CONV_HALO = 32
CONV_CHUNK = 64
FF_CHUNK = D_FF // 2


def _cparams(**kw):
    return pltpu.CompilerParams(vmem_limit_bytes=VMEM_LIMIT, **kw)


def _resident(shape):
    return pl.BlockSpec(shape, lambda *_: (0,) * len(shape), pipeline_mode=pl.Buffered(1))


def _const(shape):
    return pl.BlockSpec(shape, lambda *_: (0,) * len(shape))


def _rms_r(xf):
    return lax.rsqrt(jnp.mean(xf * xf, axis=-1, keepdims=True) + EPS)


def _rms_bwd(xf, r, g, dout):
    w = dout * g
    return r * (w - xf * (r * r) * jnp.mean(w * xf, axis=-1, keepdims=True))


def _dot(a, b):
    return jnp.dot(a, b, preferred_element_type=F32)


def _dot_nt(a, b):
    return lax.dot_general(a, b, (((1,), (1,)), ((), ())), preferred_element_type=F32)


def _dot_tn(a, b):
    return lax.dot_general(a, b, (((0,), (0,)), ((), ())), preferred_element_type=F32)


def _peer(x, y, c, k):
    px = 1 - x if (k >> 2) & 1 else x
    py = 1 - y if (k >> 1) & 1 else y
    pc = 1 - c if k & 1 else c
    return (px, py, pc), 4 * px + 2 * py + pc


def _all_gather(shards):
    n = len(shards)

    def body(*refs):
        ins, outs, done = refs[:n], refs[n:2 * n], refs[2 * n]
        send_sems, recv_sems, local_sems = refs[2 * n + 1:]
        x, y, c = lax.axis_index("x"), lax.axis_index("y"), lax.axis_index("c")
        me = 4 * x + 2 * y + c
        copies = []
        for a in range(n):
            mine = pltpu.make_async_copy(ins[a], outs[a].at[me], local_sems.at[a])
            mine.start()
            copies.append(mine)
        for k in range(1, N_DEV):
            peer, peer_block = _peer(x, y, c, k)
            for a in range(n):
                s = a * (N_DEV - 1) + k - 1
                pltpu.make_async_remote_copy(
                    src_ref=ins[a], dst_ref=outs[a].at[me], send_sem=send_sems.at[s], recv_sem=recv_sems.at[s],
                    device_id=peer, device_id_type=MESH).start()
        for k in range(1, N_DEV):
            peer, peer_block = _peer(x, y, c, k)
            for a in range(n):
                s = a * (N_DEV - 1) + k - 1
                arrived = pltpu.make_async_remote_copy(
                    src_ref=ins[a], dst_ref=outs[a].at[peer_block], send_sem=send_sems.at[s],
                    recv_sem=recv_sems.at[s], device_id=peer, device_id_type=MESH)
                arrived.wait_send()
                arrived.wait_recv()
        for mine in copies:
            mine.wait()
        done[...] = jnp.zeros_like(done)

    any_spec = pl.BlockSpec(memory_space=pl.ANY)
    return pl.pallas_call(
        body, name="all_gather_weights",
        out_shape=[jax.ShapeDtypeStruct((N_DEV,) + s.shape, s.dtype) for s in shards] + [jax.ShapeDtypeStruct((8, LANES), F32)],
        in_specs=[any_spec] * n, out_specs=[any_spec] * n + [pl.BlockSpec(memory_space=pltpu.VMEM)],
        scratch_shapes=[pltpu.SemaphoreType.DMA((n * (N_DEV - 1),)), pltpu.SemaphoreType.DMA((n * (N_DEV - 1),)),
                        pltpu.SemaphoreType.DMA((n,))],
        compiler_params=pltpu.CompilerParams(has_side_effects=True),
    )(*shards)


def _adamw(w, g, m, v):
    m = ADAM_B1 * m + (1.0 - ADAM_B1) * g
    v = ADAM_B2 * v + (1.0 - ADAM_B2) * (g * g)
    m_hat = m / (1.0 - ADAM_B1 ** ADAM_STEP)
    v_hat = v / (1.0 - ADAM_B2 ** ADAM_STEP)
    delta = -ADAM_LR * (m_hat / (jnp.sqrt(v_hat) + ADAM_EPS) + ADAM_WD * w)
    return delta, m, v


def _exchange_and_sum(src_block, recv_ref, send_sems, recv_sems, local_sem):
    x, y, c = lax.axis_index("x"), lax.axis_index("y"), lax.axis_index("c")
    me = 4 * x + 2 * y + c
    mine = pltpu.make_async_copy(src_block(me), recv_ref.at[me], local_sem)
    mine.start()
    for k in range(1, N_DEV):
        peer, peer_block = _peer(x, y, c, k)
        pltpu.make_async_remote_copy(
            src_ref=src_block(peer_block), dst_ref=recv_ref.at[me], send_sem=send_sems.at[k - 1],
            recv_sem=recv_sems.at[k - 1], device_id=peer, device_id_type=MESH).start()
    for k in range(1, N_DEV):
        peer, peer_block = _peer(x, y, c, k)
        arrived = pltpu.make_async_remote_copy(
            src_ref=src_block(peer_block), dst_ref=recv_ref.at[peer_block], send_sem=send_sems.at[k - 1],
            recv_sem=recv_sems.at[k - 1], device_id=peer, device_id_type=MESH)
        arrived.wait_send()
        arrived.wait_recv()
    mine.wait()


HBM_SPEC = pl.BlockSpec(memory_space=pltpu.HBM)
SEM_SPEC = pl.BlockSpec(memory_space=pltpu.SEMAPHORE)
DATAFLOW = pltpu.SideEffectType.DATAFLOW_SIDE_EFFECTING


def _exchange_copies(srcs, lands, send_sems, recv_sems, scatter, wait):
    x, y, c = lax.axis_index("x"), lax.axis_index("y"), lax.axis_index("c")
    me = 4 * x + 2 * y + c
    for k in range(1, N_DEV):
        peer, peer_block = _peer(x, y, c, k)
        for a in range(len(srcs)):
            s = a * (N_DEV - 1) + k - 1
            src = srcs[a].at[peer_block] if scatter else srcs[a]
            copy = pltpu.make_async_remote_copy(
                src_ref=src, dst_ref=lands[a].at[peer_block if wait else me], send_sem=send_sems.at[s],
                recv_sem=recv_sems.at[s], device_id=peer, device_id_type=MESH)
            if wait:
                copy.wait_send()
                copy.wait_recv()
            else:
                copy.start()


def _exchange_start(name, arrays, scatter):
    n = len(arrays)
    land_shapes = [a.shape if scatter else (N_DEV,) + a.shape for a in arrays]

    def body(*refs):
        _exchange_copies(refs[:n], refs[n:2 * n], refs[2 * n], refs[2 * n + 1], scatter, wait=False)
        refs[-1][...] = jnp.zeros_like(refs[-1])

    sems = pltpu.SemaphoreType.DMA((n * (N_DEV - 1),))
    hbm = lambda t: pltpu.with_memory_space_constraint(t, pltpu.HBM)
    return pl.pallas_call(
        body, name=name,
        out_shape=(sems, sems, *[pltpu.HBM(a.shape, a.dtype) for a in arrays],
                   *[pltpu.HBM(ls, a.dtype) for ls, a in zip(land_shapes, arrays)], jax.ShapeDtypeStruct((8, LANES), F32)),
        in_specs=[HBM_SPEC] * (2 * n),
        out_specs=(SEM_SPEC, SEM_SPEC, *[HBM_SPEC] * (2 * n), pl.BlockSpec(memory_space=pltpu.VMEM)),
        input_output_aliases={a: 2 + a for a in range(2 * n)},
        compiler_params=pltpu.CompilerParams(has_side_effects=DATAFLOW),
    )(*[hbm(a) for a in arrays], *[hbm(lax.empty(ls, a.dtype)) for ls, a in zip(land_shapes, arrays)])


def _exchange_wait(name, started, scatter, after):
    n = (len(started) - 3) // 2
    send_sems, recv_sems = started[0], started[1]
    arrays, lands = started[2:2 + n], started[2 + n:2 + 2 * n]

    def body(*refs):
        _exchange_copies(refs[:n], refs[n:2 * n], refs[2 * n], refs[2 * n + 1], scatter, wait=True)

    return pl.pallas_call(
        body, name=name,
        out_shape=[pltpu.HBM(t.shape, t.dtype) for t in (*arrays, *lands)],
        in_specs=[HBM_SPEC] * (2 * n) + [SEM_SPEC, SEM_SPEC, pl.BlockSpec(memory_space=pl.ANY)],
        out_specs=[HBM_SPEC] * (2 * n),
        input_output_aliases={a: a for a in range(2 * n)},
        compiler_params=pltpu.CompilerParams(has_side_effects=DATAFLOW),
    )(*arrays, *lands, send_sems, recv_sems, after)[n:]


def _sum_adamw(name, land, own, w, m, v):
    _, M, N = land.shape
    rows = math.gcd(M, 128)

    def body(land_ref, own_ref, w_ref, m_ref, v_ref, grad_ref, delta_ref, nm_ref, nv_ref):
        x, y, c = lax.axis_index("x"), lax.axis_index("y"), lax.axis_index("c")
        g = own_ref[...]
        for k in range(1, N_DEV):
            g = g + land_ref[_peer(x, y, c, k)[1]].astype(F32)
        delta, nm, nv = _adamw(w_ref[...], g, m_ref[...], v_ref[...])
        grad_ref[...] = g
        delta_ref[...] = delta
        nm_ref[...] = nm
        nv_ref[...] = nv

    row = pl.BlockSpec((rows, N), lambda i: (i, 0))
    return pl.pallas_call(
        body, name=name, grid=(M // rows,), out_shape=[jax.ShapeDtypeStruct((M, N), F32)] * 4,
        in_specs=[pl.BlockSpec((N_DEV, rows, N), lambda i: (0, i, 0)), row, row, row, row], out_specs=[row] * 4,
        compiler_params=_cparams(),
    )(land, own, w, m, v)


def _all_reduce_small(g):
    R, C = g.shape

    def body(g_ref, out_ref, recv_ref, send_sems, recv_sems, local_sem):
        _exchange_and_sum(lambda b: g_ref, recv_ref, send_sems, recv_sems, local_sem)
        total = recv_ref[0]
        for b in range(1, N_DEV):
            total = total + recv_ref[b]
        out_ref[...] = total

    vmem = pl.BlockSpec(memory_space=pltpu.VMEM)
    return pl.pallas_call(
        body, name="all_reduce_small_grads", out_shape=jax.ShapeDtypeStruct((R, C), F32),
        in_specs=[vmem], out_specs=vmem,
        scratch_shapes=[pltpu.VMEM((N_DEV, R, C), F32), pltpu.SemaphoreType.DMA((N_DEV - 1,)),
                        pltpu.SemaphoreType.DMA((N_DEV - 1,)), pltpu.SemaphoreType.DMA(())],
        compiler_params=_cparams(has_side_effects=True),
    )(g)


def _adamw_small(name, w, g, m, v):
    def body(w_ref, g_ref, m_ref, v_ref, delta_ref, nm_ref, nv_ref):
        delta, nm, nv = _adamw(w_ref[...], g_ref[...], m_ref[...], v_ref[...])
        delta_ref[...] = delta
        nm_ref[...] = nm
        nv_ref[...] = nv

    vmem = pl.BlockSpec(memory_space=pltpu.VMEM)
    return pl.pallas_call(body, name=name, out_shape=[jax.ShapeDtypeStruct(w.shape, F32)] * 3,
                          in_specs=[vmem] * 4, out_specs=[vmem] * 3)(w, g, m, v)


def _in_proj(x, g, w_in):
    S = x.shape[0]
    tm = min(TOKEN_TILE, S)
    nconv = 2 * CONV_CH

    def body(x_ref, g_ref, w_ref, a_ref, uc_ref, qkv_ref):
        xf = x_ref[...]
        a = (xf * _rms_r(xf) * g_ref[...]).astype(BF16)
        a_ref[...] = a
        uc_ref[...] = _dot(a, w_ref[:, :nconv])
        qkv_ref[:, :SB_WIDTH] = (_dot(a, w_ref[:, nconv:nconv + SB_WIDTH]) * (1.0 / math.sqrt(SB_HEAD_DIM))).astype(BF16)
        qkv_ref[:, SB_WIDTH:] = _dot(a, w_ref[:, nconv + SB_WIDTH:]).astype(BF16)

    row = lambda n: pl.BlockSpec((tm, n), lambda i: (i, 0))
    return pl.pallas_call(
        body, name="in_proj", grid=(S // tm,),
        out_shape=[jax.ShapeDtypeStruct((S, D_MODEL), BF16), jax.ShapeDtypeStruct((S, nconv), F32),
                   jax.ShapeDtypeStruct((S, 3 * SB_WIDTH), BF16)],
        in_specs=[row(D_MODEL), _const((1, D_MODEL)), _resident(w_in.shape)],
        out_specs=[row(D_MODEL), row(nconv), row(3 * SB_WIDTH)],
        compiler_params=_cparams(),
    )(x, g, w_in)


def _glu(u):
    val, gate = u[:, :CONV_CH], u[:, CONV_CH:]
    sg = jax.nn.sigmoid(gate)
    return val, sg, val * sg


def _shift_copies(ext, shifted):
    n = shifted.shape[1]
    for r in range(1, SUBLANES):
        shifted[r - 1] = ext[r:r + n, :]


def _window(ext, shifted, start, rows):
    r = start % SUBLANES
    return ext[start:start + rows, :] if r == 0 else shifted[r - 1, start - r:start - r + rows, :]


def _conv_rows(glu_ext, glu_sh, cw_ref, r0, rows):
    base = r0 + CONV_HALO - (CONV_WIDTH - 1)
    acc = cw_ref[0:1, :] * _window(glu_ext, glu_sh, base, rows)
    for w in range(1, CONV_WIDTH):
        acc = acc + cw_ref[w:w + 1, :] * _window(glu_ext, glu_sh, base + w, rows)
    return acc


def _conv_fwd(u_conv, conv_w, conv_b, ln_g, ln_b):
    S = u_conv.shape[0]
    tc = min(TOKEN_TILE, S)

    def body(u_ref, cw_ref, cb_ref, lg_ref, lb_ref, out_ref, glu_ext, glu_sh):
        i = pl.program_id(0)

        @pl.when(i == 0)
        def _():
            glu_ext[0:CONV_HALO, :] = jnp.zeros((CONV_HALO, CONV_CH), F32)

        @pl.when(i > 0)
        def _():
            glu_ext[0:CONV_HALO, :] = glu_ext[tc:tc + CONV_HALO, :]

        glu_ext[CONV_HALO:, :] = _glu(u_ref[...])[2]
        _shift_copies(glu_ext, glu_sh)
        for r0 in range(0, tc, CONV_CHUNK):
            y = _conv_rows(glu_ext, glu_sh, cw_ref, r0, CONV_CHUNK) + cb_ref[...]
            mu = jnp.mean(y, axis=-1, keepdims=True)
            yc = y - mu
            yn = yc * lax.rsqrt(jnp.mean(yc * yc, axis=-1, keepdims=True) + EPS)
            yl = yn * lg_ref[...] + lb_ref[...]
            out_ref[r0:r0 + CONV_CHUNK, :] = (yl * jax.nn.sigmoid(yl)).astype(BF16)

    return pl.pallas_call(
        body, name="conv_fwd", grid=(S // tc,),
        out_shape=jax.ShapeDtypeStruct((S, CONV_CH), BF16),
        in_specs=[pl.BlockSpec((tc, 2 * CONV_CH), lambda i: (i, 0)), _const((CONV_HALO, CONV_CH)),
                  _const((1, CONV_CH)), _const((1, CONV_CH)), _const((1, CONV_CH))],
        out_specs=pl.BlockSpec((tc, CONV_CH), lambda i: (i, 0)),
        scratch_shapes=[pltpu.VMEM((tc + CONV_HALO, CONV_CH), F32),
                        pltpu.VMEM((SUBLANES - 1, tc + CONV_HALO - SUBLANES, CONV_CH), F32)],
        compiler_params=_cparams(dimension_semantics=("arbitrary",)),
    )(u_conv, conv_w, conv_b, ln_g, ln_b)


def _head_masks():
    lane = lax.broadcasted_iota(jnp.int32, (1, LANES), 1)
    return lane < SB_HEAD_DIM


def _split_heads(t, first):
    z = jnp.zeros_like(t)
    return jnp.where(first, t, z), jnp.where(first, z, t)


def _head_sum(t, first):
    a = jnp.sum(jnp.where(first, t, 0.0), axis=-1, keepdims=True)
    b = jnp.sum(jnp.where(first, 0.0, t), axis=-1, keepdims=True)
    return a, b


def _attn_fwd(qkv, g_attn):
    S = qkv.shape[0]
    Q = min(ATTN_BLOCK, S)
    nq = S // Q
    assert nq <= LANES
    ntiles = nq * (nq + 1) // 2
    assert ATTN_UNROLL % 2 == 0 and ntiles >= 4 + ATTN_UNROLL
    npair = SB_WIDTH // LANES
    tiles = [(i, j) for i in range(nq) for j in range(i, -1, -1)]

    def body(q_ref, k_ref, v_ref, g_ref, o_ref, ao_ref, cl_ref, z_buf, l_buf, z2_buf, a_buf, c_buf, mask_buf):
        first = _head_masks()
        lane = lax.broadcasted_iota(jnp.int32, (1, LANES), 1)
        row = lax.broadcasted_iota(jnp.int32, (Q, Q), 0)
        col = lax.broadcasted_iota(jnp.int32, (Q, Q), 1)
        tri = (row >= col).astype(BF16)
        heads = range(2)
        strips = [slice(r0, r0 + ATTN_STRIP) for r0 in range(0, Q, ATTN_STRIP)]
        rows = lambda j: pl.ds(pl.multiple_of(j * Q, Q), Q)
        wide = lambda t: jnp.tile(t, (1, Q // LANES))
        as_int = lambda t: int(t) if isinstance(t, (bool, int)) else t.astype(jnp.int32)

        keep = col < row
        mask_buf[0, 0] = jnp.ones((Q, Q), F32)
        mask_buf[0, 1] = jnp.zeros((Q, Q), F32)
        mask_buf[1, 0] = jnp.where(keep, 1.0, 0.0)
        mask_buf[1, 1] = jnp.where(keep, 0.0, MASKED)
        o_ref[...] = jnp.zeros_like(o_ref)

        def scores(t, slot):
            i, j = t
            qh = _split_heads(q_ref[rows(i), :], first)
            kb = k_ref[rows(j), :]
            for h in heads:
                z_buf[slot, h] = _dot_nt(qh[h], kb)

        def logs(t, slot):
            i, j = t
            diag = as_int(i == j)
            for h in heads:
                for r in strips:
                    z2 = z_buf[slot, h, r, :] * LOG2E
                    l = (jnp.minimum(z2, 0.0) - jnp.log2(1.0 + jnp.exp2(-jnp.abs(z2)))) - z2
                    l_buf[slot, h, r, :] = (l * mask_buf[diag, 0, r, :]).astype(BF16)
                    z2_buf[slot, h, r, :] = z2 + mask_buf[diag, 1, r, :]

        def sums(slot):
            return tuple(_dot(l_buf[slot, h], tri) for h in heads)

        def weights(t, slot, sm):
            i, j = t
            running = jnp.where(j == i, 0.0, 1.0)
            for h in heads:
                before = c_buf[h] * running
                for r in strips:
                    a_buf[slot, h, r, :] = jnp.exp2(z2_buf[slot, h, r, :] + sm[h][r] + wide(before[r])).astype(BF16)
                hl = slice(h * LANES, (h + 1) * LANES)
                cl_ref[rows(i), hl] = jnp.where(lane == j, before, cl_ref[rows(i), hl] * running)
                c_buf[h] = before + jnp.broadcast_to(sm[h][:, 0:1], (Q, LANES))

        def values(t, slot):
            i, j = t
            vh = _split_heads(v_ref[rows(j), :], first)
            o_ref[rows(i), :] += _dot(a_buf[slot, 0], vh[0]) + _dot(a_buf[slot, 1], vh[1])

        def iteration(t, p):
            ta, tb, tc, td = t
            if ta is not None:
                scores(ta, p)
            if tc is not None:
                sm = sums(p)
            if td is not None:
                values(td, 1 - p)
            if tb is not None:
                logs(tb, 1 - p)
            if tc is not None:
                weights(tc, p, sm)

        def window(n):
            return tuple(tiles[n - k] if 0 <= n - k < ntiles else None for k in range(4))

        def following(t):
            i, j = t
            last = j == 0
            return jnp.where(last, i + 1, i), jnp.where(last, i + 1, j - 1)

        peeled = 4 + (ntiles - 4) % ATTN_UNROLL

        def unrolled_iterations(_, t):
            for n in range(peeled, peeled + ATTN_UNROLL):
                iteration(t, n % 2)
                t = (following(t[0]),) + t[:3]
            return t

        c_buf[...] = jnp.zeros_like(c_buf)
        for n in range(peeled):
            iteration(window(n), n % 2)
        first_window = tuple((jnp.int32(i), jnp.int32(j)) for i, j in window(peeled))
        lax.fori_loop(0, (ntiles - peeled) // ATTN_UNROLL, unrolled_iterations, first_window)
        for n in range(ntiles, ntiles + 3):
            iteration(window(n), n % 2)

        def head_norm(b, carry):
            o = o_ref[rows(b), :]
            sa, sb = _head_sum(o * o, first)
            r = jnp.where(first, lax.rsqrt(sa * (1.0 / SB_HEAD_DIM) + EPS), lax.rsqrt(sb * (1.0 / SB_HEAD_DIM) + EPS))
            ao_ref[rows(b), :] = (o * r * g_ref[...]).astype(BF16)
            return carry

        lax.fori_loop(0, nq, head_norm, 0)

    col_block = lambda off: pl.BlockSpec((S, LANES), lambda p: (0, off + p), pipeline_mode=pl.Buffered(1))
    out_block = lambda n: pl.BlockSpec((S, n), lambda p: (0, p), pipeline_mode=pl.Buffered(1))
    return pl.pallas_call(
        body, name="attn_fwd", grid=(npair,),
        out_shape=[jax.ShapeDtypeStruct((S, SB_WIDTH), F32), jax.ShapeDtypeStruct((S, SB_WIDTH), BF16),
                   jax.ShapeDtypeStruct((S, 2 * SB_WIDTH), F32)],
        in_specs=[col_block(0), col_block(npair), col_block(2 * npair), pl.BlockSpec((1, LANES), lambda p: (0, p))],
        out_specs=[out_block(LANES), out_block(LANES), out_block(2 * LANES)],
        scratch_shapes=[pltpu.VMEM((2, 2, Q, Q), F32), pltpu.VMEM((2, 2, Q, Q), BF16), pltpu.VMEM((2, 2, Q, Q), F32),
                        pltpu.VMEM((2, 2, Q, Q), BF16), pltpu.VMEM((2, Q, LANES), F32), pltpu.VMEM((2, 2, Q, Q), F32)],
        compiler_params=_cparams(dimension_semantics=("arbitrary",)),
    )(qkv, qkv, qkv, g_attn)


def _out_proj(conv_out, attn_out, w_out, x, g_post_mix, g_pre_ffn):
    S = x.shape[0]
    tm = min(TOKEN_TILE, S)

    def body(co_ref, ao_ref, w_ref, x_ref, g1_ref, g2_ref, y_ref, h1_ref, fin_ref):
        y = _dot(co_ref[...], w_ref[:CONV_CH, :]) + _dot(ao_ref[...], w_ref[CONV_CH:, :])
        h1 = x_ref[...] + y * _rms_r(y) * g1_ref[...]
        y_ref[...] = y
        h1_ref[...] = h1
        fin_ref[...] = (h1 * _rms_r(h1) * g2_ref[...]).astype(BF16)

    row = lambda n: pl.BlockSpec((tm, n), lambda i: (i, 0))
    return pl.pallas_call(
        body, name="out_proj", grid=(S // tm,),
        out_shape=[jax.ShapeDtypeStruct((S, D_MODEL), F32), jax.ShapeDtypeStruct((S, D_MODEL), F32),
                   jax.ShapeDtypeStruct((S, D_MODEL), BF16)],
        in_specs=[row(CONV_CH), row(SB_WIDTH), _resident(w_out.shape), row(D_MODEL), _const((1, D_MODEL)),
                  _const((1, D_MODEL))],
        out_specs=[row(D_MODEL)] * 3,
        compiler_params=_cparams(),
    )(conv_out, attn_out, w_out, x, g_post_mix, g_pre_ffn)


def _ffn_fwd_loss(f_in, w_gate, w_up, w_down, h1, target, g_post_ffn):
    S = f_in.shape[0]
    tm = min(FFN_TILE, S)
    nt = S // tm

    def body(fin_ref, wg_ref, wu_ref, wd_ref, h1_ref, t_ref, g_ref, gt_ref, up_ref, df_ref, dh2_ref, loss_ref, dg_ref,
             sq_acc):
        i = pl.program_id(0)

        @pl.when(i == 0)
        def _():
            sq_acc[...] = jnp.zeros_like(sq_acc)
            dg_ref[...] = jnp.zeros_like(dg_ref)

        fin = fin_ref[...]
        f = jnp.zeros((tm, D_MODEL), F32)
        for c0 in range(0, D_FF, FF_CHUNK):
            cols = slice(c0, c0 + FF_CHUNK)
            gt = _dot(fin, wg_ref[:, cols])
            up = _dot(fin, wu_ref[:, cols])
            gt_ref[:, cols] = gt.astype(BF16)
            up_ref[:, cols] = up.astype(BF16)
            f = f + _dot((gt * jax.nn.sigmoid(gt) * up).astype(BF16), wd_ref[cols, :])
        r = _rms_r(f)
        g = g_ref[...]
        diff = h1_ref[...] + f * r * g - t_ref[...]
        sq_acc[...] += jnp.sum(diff * diff, axis=0, keepdims=True)
        dh2 = diff * (1.0 / D_MODEL)
        dh2_ref[...] = dh2
        dg_ref[...] += jnp.sum(dh2 * f * r, axis=0, keepdims=True)
        df_ref[...] = _rms_bwd(f, r, g, dh2).astype(BF16)

        @pl.when(i == nt - 1)
        def _():
            loss_ref[...] = jnp.broadcast_to((0.5 / D_MODEL) * jnp.sum(sq_acc[...], axis=-1, keepdims=True), (1, LANES))

    row = lambda n: pl.BlockSpec((tm, n), lambda i: (i, 0))
    return pl.pallas_call(
        body, name="ffn_fwd_loss", grid=(nt,),
        out_shape=[jax.ShapeDtypeStruct((S, D_FF), BF16), jax.ShapeDtypeStruct((S, D_FF), BF16),
                   jax.ShapeDtypeStruct((S, D_MODEL), BF16), jax.ShapeDtypeStruct((S, D_MODEL), F32),
                   jax.ShapeDtypeStruct((1, LANES), F32), jax.ShapeDtypeStruct((1, D_MODEL), F32)],
        in_specs=[row(D_MODEL), _resident(w_gate.shape), _resident(w_up.shape), _resident(w_down.shape),
                  row(D_MODEL), row(D_MODEL), _const((1, D_MODEL))],
        out_specs=[row(D_FF), row(D_FF), row(D_MODEL), row(D_MODEL), _const((1, LANES)), _const((1, D_MODEL))],
        scratch_shapes=[pltpu.VMEM((1, D_MODEL), F32)],
        compiler_params=_cparams(dimension_semantics=("arbitrary",)),
    )(f_in, w_gate, w_up, w_down, h1, target, g_post_ffn)


def _ffn_bwd(df, gt, up, w_gate, w_up, w_down):
    S = df.shape[0]
    tm = min(FFN_TILE, S)

    def body(df_ref, gt_ref, up_ref, wg_ref, wu_ref, wd_ref, dgt_ref, dup_ref, act_ref, dfin_ref):
        df = df_ref[...]
        dfin = jnp.zeros((tm, D_MODEL), F32)
        for c0 in range(0, D_FF, FF_CHUNK):
            cols = slice(c0, c0 + FF_CHUNK)
            dact = _dot_nt(df, wd_ref[cols, :])
            gt = gt_ref[:, cols].astype(F32)
            up = up_ref[:, cols].astype(F32)
            s = jax.nn.sigmoid(gt)
            silu = gt * s
            dgt = (dact * up * (s * (1.0 + gt * (1.0 - s)))).astype(BF16)
            dup = (dact * silu).astype(BF16)
            act_ref[:, cols] = (silu * up).astype(BF16)
            dgt_ref[:, cols] = dgt
            dup_ref[:, cols] = dup
            dfin = dfin + _dot_nt(dgt, wg_ref[:, cols]) + _dot_nt(dup, wu_ref[:, cols])
        dfin_ref[...] = dfin

    row = lambda n: pl.BlockSpec((tm, n), lambda i: (i, 0))
    return pl.pallas_call(
        body, name="ffn_bwd", grid=(S // tm,),
        out_shape=[jax.ShapeDtypeStruct((S, D_FF), BF16)] * 3 + [jax.ShapeDtypeStruct((S, D_MODEL), F32)],
        in_specs=[row(D_MODEL), row(D_FF), row(D_FF), _resident(w_gate.shape), _resident(w_up.shape),
                  _resident(w_down.shape)],
        out_specs=[row(D_FF)] * 3 + [row(D_MODEL)],
        compiler_params=_cparams(),
    )(df, gt, up, w_gate, w_up, w_down)


def _matmul_tn(name, x, y, tn):
    S, K = x.shape
    N = y.shape[1]
    ts = min(TOKEN_TILE, S)

    def body(x_ref, y_ref, o_ref):
        @pl.when(pl.program_id(1) == 0)
        def _():
            o_ref[...] = jnp.zeros_like(o_ref)

        o_ref[...] += _dot_tn(x_ref[...].astype(BF16), y_ref[...].astype(BF16))

    return pl.pallas_call(
        body, name=name, grid=(N // tn, S // ts),
        out_shape=jax.ShapeDtypeStruct((K, N), F32),
        in_specs=[pl.BlockSpec((ts, K), lambda n, s: (s, 0)), pl.BlockSpec((ts, tn), lambda n, s: (s, n))],
        out_specs=pl.BlockSpec((K, tn), lambda n, s: (0, n)),
        compiler_params=_cparams(dimension_semantics=("arbitrary", "arbitrary")),
    )(x, y)


def _mix_bwd(dfin, h1, y, dh2, g_pre_ffn, g_post_mix, w_out):
    S = dfin.shape[0]
    tm = min(TOKEN_TILE, S)

    def body(dfin_ref, h1_ref, y_ref, dh2_ref, g2_ref, g1_ref, w_ref, dh1_ref, dy_ref, dco_ref, dao_ref, dg2_ref, dg1_ref):
        @pl.when(pl.program_id(0) == 0)
        def _():
            dg2_ref[...] = jnp.zeros_like(dg2_ref)
            dg1_ref[...] = jnp.zeros_like(dg1_ref)

        h1, dfin = h1_ref[...], dfin_ref[...]
        r2 = _rms_r(h1)
        dh1 = dh2_ref[...] + _rms_bwd(h1, r2, g2_ref[...], dfin)
        dg2_ref[...] += jnp.sum(dfin * h1 * r2, axis=0, keepdims=True)
        y = y_ref[...]
        r1 = _rms_r(y)
        dy = _rms_bwd(y, r1, g1_ref[...], dh1).astype(BF16)
        dg1_ref[...] += jnp.sum(dh1 * y * r1, axis=0, keepdims=True)
        dh1_ref[...] = dh1
        dy_ref[...] = dy
        dco_ref[...] = _dot_nt(dy, w_ref[:CONV_CH, :])
        dao_ref[...] = _dot_nt(dy, w_ref[CONV_CH:, :])

    row = lambda n: pl.BlockSpec((tm, n), lambda i: (i, 0))
    return pl.pallas_call(
        body, name="mix_bwd", grid=(S // tm,),
        out_shape=[jax.ShapeDtypeStruct((S, D_MODEL), F32), jax.ShapeDtypeStruct((S, D_MODEL), BF16),
                   jax.ShapeDtypeStruct((S, CONV_CH), F32), jax.ShapeDtypeStruct((S, SB_WIDTH), F32),
                   jax.ShapeDtypeStruct((1, D_MODEL), F32), jax.ShapeDtypeStruct((1, D_MODEL), F32)],
        in_specs=[row(D_MODEL)] * 4 + [_const((1, D_MODEL)), _const((1, D_MODEL)), _resident(w_out.shape)],
        out_specs=[row(D_MODEL), row(D_MODEL), row(CONV_CH), row(SB_WIDTH), _const((1, D_MODEL)), _const((1, D_MODEL))],
        compiler_params=_cparams(dimension_semantics=("arbitrary",)),
    )(dfin, h1, y, dh2, g_pre_ffn, g_post_mix, w_out)


def _attn_norm_bwd(o, dao, g_attn):
    S = o.shape[0]
    tm = min(TOKEN_TILE, S)
    inv_dh = 1.0 / SB_HEAD_DIM

    def body(o_ref, dao_ref, g_ref, do_ref, dg_ref):
        @pl.when(pl.program_id(0) == 0)
        def _():
            dg_ref[...] = jnp.zeros_like(dg_ref)

        first = _head_masks()
        for p in range(SB_WIDTH // LANES):
            cols = slice(p * LANES, (p + 1) * LANES)
            o, dao, g = o_ref[:, cols], dao_ref[:, cols], g_ref[:, cols]
            sa, sb = _head_sum(o * o, first)
            r = jnp.where(first, lax.rsqrt(sa * inv_dh + EPS), lax.rsqrt(sb * inv_dh + EPS))
            w = dao * g
            wa, wb = _head_sum(w * o, first)
            do_ref[:, cols] = (r * (w - o * (r * r) * (jnp.where(first, wa, wb) * inv_dh))).astype(BF16)
            dg_ref[:, cols] += jnp.sum(dao * o * r, axis=0, keepdims=True)

    row = pl.BlockSpec((tm, SB_WIDTH), lambda i: (i, 0))
    return pl.pallas_call(
        body, name="attn_norm_bwd", grid=(S // tm,),
        out_shape=[jax.ShapeDtypeStruct((S, SB_WIDTH), BF16), jax.ShapeDtypeStruct((1, SB_WIDTH), F32)],
        in_specs=[row, row, _const((1, SB_WIDTH))], out_specs=[row, _const((1, SB_WIDTH))],
        compiler_params=_cparams(dimension_semantics=("arbitrary",)),
    )(o, dao, g_attn)


def _attn_bwd(qkv, do, cl):
    S = qkv.shape[0]
    Q = min(ATTN_BLOCK, S)
    nq = S // Q
    ntiles = nq * (nq + 1) // 2
    assert ATTN_UNROLL % 2 == 0 and ntiles >= 4 + ATTN_UNROLL
    npair = SB_WIDTH // LANES
    tiles = [(i, j) for i in range(nq) for j in range(i + 1)]

    def body(q_ref, k_ref, v_ref, do_ref, cl_ref, dq_ref, dk_ref, dv_ref,
             z_buf, lb_buf, be_buf, g_buf, l_buf, a_buf, gb_buf, dz_buf, pg_buf, mask_buf):
        first = _head_masks()
        lane = lax.broadcasted_iota(jnp.int32, (1, LANES), 1)
        row = lax.broadcasted_iota(jnp.int32, (Q, Q), 0)
        col = lax.broadcasted_iota(jnp.int32, (Q, Q), 1)
        tri = (row > col).astype(BF16)
        tpi = (row <= col).astype(BF16)
        heads = range(2)
        strips = [slice(r0, r0 + ATTN_STRIP) for r0 in range(0, Q, ATTN_STRIP)]
        rows = lambda j: pl.ds(pl.multiple_of(j * Q, Q), Q)
        wide = lambda t: jnp.tile(t, (1, Q // LANES))
        as_int = lambda t: int(t) if isinstance(t, (bool, int)) else t.astype(jnp.int32)

        keep = col < row
        mask_buf[0, 0] = jnp.ones((Q, Q), F32)
        mask_buf[0, 1] = jnp.zeros((Q, Q), F32)
        mask_buf[1, 0] = jnp.where(keep, 1.0, 0.0)
        mask_buf[1, 1] = jnp.where(keep, 0.0, MASKED)
        dq_ref[...] = jnp.zeros_like(dq_ref)
        dk_ref[...] = jnp.zeros_like(dk_ref)
        dv_ref[...] = jnp.zeros_like(dv_ref)

        def scores(t, slot):
            i, j = t
            qh = _split_heads(q_ref[rows(i), :], first)
            kb = k_ref[rows(j), :]
            for h in heads:
                z_buf[slot, h] = _dot_nt(qh[h], kb)

        def logs(t, slot):
            i, j = t
            diag = as_int(i == j)
            for h in heads:
                for r in strips:
                    z2 = z_buf[slot, h, r, :] * LOG2E
                    lb = jnp.minimum(z2, 0.0) - jnp.log2(1.0 + jnp.exp2(-jnp.abs(z2)))
                    l = (lb - z2) * mask_buf[diag, 0, r, :]
                    l_buf[slot, h, r, :] = l.astype(BF16)
                    lb_buf[slot, h, r, :] = lb + mask_buf[diag, 1, r, :]

        def sums(t, slot):
            i, j = t
            doh = _split_heads(do_ref[rows(i), :], first)
            vb = v_ref[rows(j), :]
            return (tuple(_dot(l_buf[slot, h], tri) for h in heads), tuple(_dot_nt(doh[h], vb) for h in heads))

        def weights(t, slot, sm, da):
            i, j = t
            for h in heads:
                c = jnp.sum(jnp.where(lane == j, cl_ref[rows(i), h * LANES:(h + 1) * LANES], 0.0), axis=-1, keepdims=True)
                c = jnp.broadcast_to(c, (Q, LANES))
                for r in strips:
                    lb = lb_buf[slot, h, r, :]
                    a = jnp.exp2(lb + sm[h][r] + wide(c[r]))
                    g = da[h][r] * a
                    a_buf[slot, h, r, :] = a.astype(BF16)
                    be_buf[slot, h, r, :] = jnp.exp2(lb)
                    g_buf[slot, h, r, :] = g
                    gb_buf[slot, h, r, :] = g.astype(BF16)

        def prefix(t, slot):
            i, j = t
            doh = _split_heads(do_ref[rows(i), :], first)
            dv_ref[rows(j), :] += _dot_tn(a_buf[slot, 0], doh[0]) + _dot_tn(a_buf[slot, 1], doh[1])
            return tuple(_dot(gb_buf[slot, h], tpi) for h in heads)

        def dscores(t, slot, pm):
            i, j = t
            for h in heads:
                pg = pg_buf[h] * jnp.where(j == 0, 0.0, 1.0)
                for r in strips:
                    dz = g_buf[slot, h, r, :] - be_buf[slot, h, r, :] * (pm[h][r] + wide(pg[r]))
                    dz_buf[slot, h, r, :] = dz.astype(BF16)
                pg_buf[h] = pg + jnp.broadcast_to(pm[h][:, Q - 1:Q], (Q, LANES))

        def grads(t, slot):
            i, j = t
            qh = _split_heads(q_ref[rows(i), :], first)
            kh = _split_heads(k_ref[rows(j), :], first)
            dq_ref[rows(i), :] += _dot(dz_buf[slot, 0], kh[0]) + _dot(dz_buf[slot, 1], kh[1])
            dk_ref[rows(j), :] += _dot_tn(dz_buf[slot, 0], qh[0]) + _dot_tn(dz_buf[slot, 1], qh[1])

        def iteration(t, p):
            ta, tb, tc, td, te = t
            if ta is not None:
                scores(ta, p)
            if tc is not None:
                sm, da = sums(tc, p)
            if td is not None:
                pm = prefix(td, 1 - p)
            if te is not None:
                grads(te, p)
            if tb is not None:
                logs(tb, 1 - p)
            if tc is not None:
                weights(tc, p, sm, da)
            if td is not None:
                dscores(td, 1 - p, pm)

        def window(n):
            return tuple(tiles[n - k] if 0 <= n - k < ntiles else None for k in range(5))

        def following(t):
            i, j = t
            last = j == i
            return jnp.where(last, i + 1, i), jnp.where(last, 0, j + 1)

        peeled = 4 + (ntiles - 4) % ATTN_UNROLL

        def unrolled_iterations(_, t):
            for n in range(peeled, peeled + ATTN_UNROLL):
                iteration(t, n % 2)
                t = (following(t[0]),) + t[:4]
            return t

        pg_buf[...] = jnp.zeros_like(pg_buf)
        for n in range(peeled):
            iteration(window(n), n % 2)
        first_window = tuple((jnp.int32(i), jnp.int32(j)) for i, j in window(peeled))
        lax.fori_loop(0, (ntiles - peeled) // ATTN_UNROLL, unrolled_iterations, first_window)
        for n in range(ntiles, ntiles + 4):
            iteration(window(n), n % 2)
        dq_ref[...] = dq_ref[...] * (1.0 / math.sqrt(SB_HEAD_DIM))

    col_block = lambda off: pl.BlockSpec((S, LANES), lambda p: (0, off + p), pipeline_mode=pl.Buffered(1))
    return pl.pallas_call(
        body, name="attn_bwd", grid=(npair,),
        out_shape=[jax.ShapeDtypeStruct((S, SB_WIDTH), F32)] * 3,
        in_specs=[col_block(0), col_block(npair), col_block(2 * npair), col_block(0),
                  pl.BlockSpec((S, 2 * LANES), lambda p: (0, p), pipeline_mode=pl.Buffered(1))],
        out_specs=[pl.BlockSpec((S, LANES), lambda p: (0, p), pipeline_mode=pl.Buffered(1))] * 3,
        scratch_shapes=[pltpu.VMEM((2, 2, Q, Q), F32)] * 4 + [pltpu.VMEM((2, 2, Q, Q), BF16)] * 4
        + [pltpu.VMEM((2, Q, LANES), F32), pltpu.VMEM((2, 2, Q, Q), F32)],
        compiler_params=_cparams(dimension_semantics=("arbitrary",)),
    )(qkv, qkv, qkv, do, cl)


def _conv_bwd(u_conv, dco, conv_w, conv_b, ln_g, ln_b):
    S = u_conv.shape[0]
    tc = min(TOKEN_TILE, S)
    nt = S // tc
    per = tc // CONV_HALO
    groups = CONV_CHUNK // 8

    def body(u_ref, halo_ref, dco_ref, cw_ref, cb_ref, lg_ref, lb_ref, du_ref, dcw_ref, dsm_ref, glu_ext, dyc_ext, sg_buf,
             dcw_acc, dsm_acc, glu_sh, dyc_sh):
        i = pl.program_id(0)
        ti = nt - 1 - i

        @pl.when(i == 0)
        def _():
            dyc_ext[tc:, :] = jnp.zeros((CONV_HALO, CONV_CH), F32)
            dcw_acc[...] = jnp.zeros_like(dcw_acc)
            dsm_acc[...] = jnp.zeros_like(dsm_acc)

        @pl.when(i > 0)
        def _():
            dyc_ext[tc:, :] = dyc_ext[0:CONV_HALO, :]

        glu_ext[0:CONV_HALO, :] = jnp.where(ti > 0, _glu(halo_ref[...])[2], 0.0)
        val, sg, glu = _glu(u_ref[...])
        glu_ext[CONV_HALO:, :] = glu
        sg_buf[...] = sg
        _shift_copies(glu_ext, glu_sh)

        dcb = jnp.zeros((8, CONV_CH), F32)
        dlg = jnp.zeros((8, CONV_CH), F32)
        dlb = jnp.zeros((8, CONV_CH), F32)
        fold = lambda t: jnp.sum(t.reshape(groups, 8, CONV_CH), axis=0)
        for r0 in range(0, tc, CONV_CHUNK):
            y = _conv_rows(glu_ext, glu_sh, cw_ref, r0, CONV_CHUNK) + cb_ref[...]
            mu = jnp.mean(y, axis=-1, keepdims=True)
            yc = y - mu
            rstd = lax.rsqrt(jnp.mean(yc * yc, axis=-1, keepdims=True) + EPS)
            yn = yc * rstd
            yl = yn * lg_ref[...] + lb_ref[...]
            s = jax.nn.sigmoid(yl)
            dyl = dco_ref[r0:r0 + CONV_CHUNK, :] * (s * (1.0 + yl * (1.0 - s)))
            dlg = dlg + fold(dyl * yn)
            dlb = dlb + fold(dyl)
            wv = dyl * lg_ref[...]
            dyc = rstd * (wv - jnp.mean(wv, axis=-1, keepdims=True) - yn * jnp.mean(wv * yn, axis=-1, keepdims=True))
            dcb = dcb + fold(dyc)
            dyc_ext[r0:r0 + CONV_CHUNK, :] = dyc
        dsm_acc[0:8, :] += dcb
        dsm_acc[8:16, :] += dlg
        dsm_acc[16:24, :] += dlb
        _shift_copies(dyc_ext, dyc_sh)

        for r0 in range(0, tc, CONV_CHUNK):
            dyc = dyc_ext[r0:r0 + CONV_CHUNK, :]
            dglu = jnp.zeros((CONV_CHUNK, CONV_CH), F32)
            base = r0 + CONV_HALO - (CONV_WIDTH - 1)
            for w in range(CONV_WIDTH):
                back = r0 + (CONV_WIDTH - 1) - w
                dglu = dglu + cw_ref[w:w + 1, :] * _window(dyc_ext, dyc_sh, back, CONV_CHUNK)
                dcw_acc[8 * w:8 * w + 8, :] += fold(dyc * _window(glu_ext, glu_sh, base + w, CONV_CHUNK))
            sg = sg_buf[r0:r0 + CONV_CHUNK, :]
            v = u_ref[r0:r0 + CONV_CHUNK, :CONV_CH]
            du_ref[r0:r0 + CONV_CHUNK, :CONV_CH] = (dglu * sg).astype(BF16)
            du_ref[r0:r0 + CONV_CHUNK, CONV_CH:] = (dglu * v * sg * (1.0 - sg)).astype(BF16)

        @pl.when(i == nt - 1)
        def _():
            for w in range(CONV_WIDTH):
                dcw_ref[w:w + 1, :] = jnp.sum(dcw_acc[8 * w:8 * w + 8, :], axis=0, keepdims=True)
            dcw_ref[CONV_WIDTH:, :] = jnp.zeros((CONV_HALO - CONV_WIDTH, CONV_CH), F32)
            for k in range(3):
                dsm_ref[k:k + 1, :] = jnp.sum(dsm_acc[8 * k:8 * k + 8, :], axis=0, keepdims=True)
            dsm_ref[3:, :] = jnp.zeros((5, CONV_CH), F32)

    return pl.pallas_call(
        body, name="conv_bwd", grid=(nt,),
        out_shape=[jax.ShapeDtypeStruct((S, 2 * CONV_CH), BF16), jax.ShapeDtypeStruct((CONV_HALO, CONV_CH), F32),
                   jax.ShapeDtypeStruct((8, CONV_CH), F32)],
        in_specs=[pl.BlockSpec((tc, 2 * CONV_CH), lambda i: (nt - 1 - i, 0)),
                  pl.BlockSpec((CONV_HALO, 2 * CONV_CH), lambda i: (jnp.maximum((nt - 1 - i) * per - 1, 0), 0)),
                  pl.BlockSpec((tc, CONV_CH), lambda i: (nt - 1 - i, 0)),
                  _const((CONV_HALO, CONV_CH)), _const((1, CONV_CH)), _const((1, CONV_CH)), _const((1, CONV_CH))],
        out_specs=[pl.BlockSpec((tc, 2 * CONV_CH), lambda i: (nt - 1 - i, 0)), _const((CONV_HALO, CONV_CH)),
                   _const((8, CONV_CH))],
        scratch_shapes=[pltpu.VMEM((tc + CONV_HALO, CONV_CH), F32), pltpu.VMEM((tc + CONV_HALO, CONV_CH), F32),
                        pltpu.VMEM((tc, CONV_CH), F32), pltpu.VMEM((8 * CONV_HALO, CONV_CH), F32),
                        pltpu.VMEM((24, CONV_CH), F32)]
        + [pltpu.VMEM((SUBLANES - 1, tc + CONV_HALO - SUBLANES, CONV_CH), F32)] * 2,
        compiler_params=_cparams(dimension_semantics=("arbitrary",)),
    )(u_conv, u_conv, dco, conv_w, conv_b, ln_g, ln_b)


def _in_proj_bwd(du_conv, dq, dk, dv, w_in, x, g, dh1):
    S = x.shape[0]
    tm = min(TOKEN_TILE, S)
    nconv = 2 * CONV_CH

    def body(duc_ref, dq_ref, dk_ref, dv_ref, w_ref, x_ref, g_ref, dh1_ref, dx_ref, dg_ref):
        @pl.when(pl.program_id(0) == 0)
        def _():
            dg_ref[...] = jnp.zeros_like(dg_ref)

        da = _dot_nt(duc_ref[...], w_ref[:, :nconv])
        for n, ref in enumerate((dq_ref, dk_ref, dv_ref)):
            c0 = nconv + n * SB_WIDTH
            da = da + _dot_nt(ref[...].astype(BF16), w_ref[:, c0:c0 + SB_WIDTH])
        xf = x_ref[...]
        r = _rms_r(xf)
        dx_ref[...] = dh1_ref[...] + _rms_bwd(xf, r, g_ref[...], da)
        dg_ref[...] += jnp.sum(da * xf * r, axis=0, keepdims=True)

    row = lambda n: pl.BlockSpec((tm, n), lambda i: (i, 0))
    return pl.pallas_call(
        body, name="in_proj_bwd", grid=(S // tm,),
        out_shape=[jax.ShapeDtypeStruct((S, D_MODEL), F32), jax.ShapeDtypeStruct((1, D_MODEL), F32)],
        in_specs=[row(nconv), row(SB_WIDTH), row(SB_WIDTH), row(SB_WIDTH), _resident(w_in.shape), row(D_MODEL),
                  _const((1, D_MODEL)), row(D_MODEL)],
        out_specs=[row(D_MODEL), _const((1, D_MODEL))],
        compiler_params=_cparams(dimension_semantics=("arbitrary",)),
    )(du_conv, dq, dk, dv, w_in, x, g, dh1)


def _layer_grads(xs, target, g_pre_mix, w_in_f, conv_w_f, conv_b, conv_ln_g, conv_ln_b, attn_g, g_post_mix, g_pre_ffn,
                 g_post_ffn, late_weights, send_grads):
    a, u_conv, qkv = _in_proj(xs, g_pre_mix, w_in_f)
    conv_out = _conv_fwd(u_conv, conv_w_f, conv_b, conv_ln_g, conv_ln_b)
    o, attn_out, cl = _attn_fwd(qkv, attn_g)
    w_out_f, w_gate_f, w_up_f, w_down_f = late_weights(attn_out)
    y, h1, f_in = _out_proj(conv_out, attn_out, w_out_f, xs, g_post_mix, g_pre_ffn)
    gt, up, df, dh2, loss_part, d_g_post_ffn = _ffn_fwd_loss(f_in, w_gate_f, w_up_f, w_down_f, h1, target, g_post_ffn)

    dgt, dup, act, dfin = _ffn_bwd(df, gt, up, w_gate_f, w_up_f, w_down_f)
    d_w_down = _matmul_tn("grad_w_down", act, df, 512)
    d_w_gate = _matmul_tn("grad_w_gate", f_in, dgt, FF_CHUNK)
    d_w_up = _matmul_tn("grad_w_up", f_in, dup, FF_CHUNK)
    sent = send_grads("ffn", (d_w_gate, d_w_up, d_w_down))
    dh1, dy, dco, dao, d_g_pre_ffn, d_g_post_mix = _mix_bwd(dfin, h1, y, dh2, g_pre_ffn + sent, g_post_mix, w_out_f)
    d_w_out = jnp.concatenate([_matmul_tn("grad_w_out_conv", conv_out, dy, D_MODEL),
                               _matmul_tn("grad_w_out_attn", attn_out, dy, D_MODEL)], axis=0)
    sent = send_grads("w_out", (d_w_out,))
    do, d_attn_g = _attn_norm_bwd(o, dao, attn_g + sent)
    dq, dk, dv = _attn_bwd(qkv, do, cl)
    du_conv, d_conv_w, d_conv_small = _conv_bwd(u_conv, dco, conv_w_f, conv_b, conv_ln_g, conv_ln_b)
    grad_x, d_g_pre_mix = _in_proj_bwd(du_conv, dq, dk, dv, w_in_f, xs, g_pre_mix, dh1)
    d_w_in = jnp.concatenate([_matmul_tn("grad_w_in_conv", a, du_conv, 2 * CONV_CH),
                              _matmul_tn("grad_w_in_q", a, dq, SB_WIDTH), _matmul_tn("grad_w_in_k", a, dk, SB_WIDTH),
                              _matmul_tn("grad_w_in_v", a, dv, SB_WIDTH)], axis=1)
    return (loss_part, grad_x, d_w_in, d_conv_w, d_conv_small, d_attn_g, d_g_pre_mix, d_g_post_mix, d_g_pre_ffn,
            d_g_post_ffn)


def _cols_to_blocks(w):
    K, N = w.shape
    return jnp.transpose(w.reshape(K, N_DEV, N // N_DEV), (1, 0, 2))


def _blocks_to_cols(blocks):
    n_dev, K, n = blocks.shape
    return jnp.transpose(blocks, (1, 0, 2)).reshape(K, n_dev * n)


def kernel(x, g_pre_mix, w_in, conv_w, conv_b, conv_ln_g, conv_ln_b, attn_norm_g, w_out, g_post_mix, g_pre_ffn, w_gate, w_up, w_down, g_post_ffn, loss_target, m_g_pre_mix, m_w_in, m_conv_w, m_conv_b, m_conv_ln_g, m_conv_ln_b, m_attn_norm_g, m_w_out, m_g_post_mix, m_g_pre_ffn, m_w_gate, m_w_up, m_w_down, m_g_post_ffn, v_g_pre_mix, v_w_in, v_conv_w, v_conv_b, v_conv_ln_g, v_conv_ln_b, v_attn_norm_g, v_w_out, v_g_post_mix, v_g_pre_ffn, v_w_gate, v_w_up, v_w_down, v_g_post_ffn):
    xs = x[0]
    target = loss_target[0]
    S = xs.shape[0]
    me = 4 * lax.axis_index("x") + 2 * lax.axis_index("y") + lax.axis_index("c")
    cw_shard = conv_w.reshape(CONV_WIDTH, CONV_CH // N_DEV)
    attn_g = attn_norm_g.reshape(1, SB_WIDTH)

    gathered = _all_gather([w_in[0].astype(BF16), cw_shard])
    w_in_f = _blocks_to_cols(gathered[0])
    conv_w_f = jnp.pad(_blocks_to_cols(gathered[1]), ((0, CONV_HALO - CONV_WIDTH), (0, 0)))
    gathered_zero = gathered[2][0:1, 0:1].astype(BF16)
    late = [w_out[0].astype(BF16) + gathered_zero, w_gate[0].astype(BF16), w_up[0].astype(BF16), w_down[0].astype(BF16)]
    late_started = _exchange_start("all_gather_late_start", late, scatter=False)

    def late_weights(after):
        lands = _exchange_wait("all_gather_late_wait", late_started, False, after)
        wo, wg, wu, wd = [lax.dynamic_update_index_in_dim(land, own, me, 0) for land, own in zip(lands, late)]
        return wo.reshape(D_MODEL, D_MODEL), _blocks_to_cols(wg), _blocks_to_cols(wu), wd.reshape(D_FF, D_MODEL)

    started = {}

    def send_grads(name, grads, payload=F32):
        blocks = [g.reshape(N_DEV, g.shape[0] // N_DEV, g.shape[1]) if g.shape[1] == D_MODEL else _cols_to_blocks(g)
                  for g in grads]
        sent = _exchange_start("reduce_scatter_" + name + "_start", [b.astype(payload) for b in blocks], scatter=True)
        started[name] = (sent, blocks)
        return sent[-1][0:1, 0:1]

    (loss_part, grad_x, d_w_in, d_conv_w, d_conv_small, d_attn_g, d_g_pre_mix, d_g_post_mix, d_g_pre_ffn,
     d_g_post_ffn) = _layer_grads(
        xs, target, g_pre_mix + late_started[-1][0:1, 0:1], w_in_f, conv_w_f, conv_b, conv_ln_g, conv_ln_b, attn_g,
        g_post_mix, g_pre_ffn, g_post_ffn, late_weights, send_grads)
    sent = send_grads("w_in", (d_w_in,), payload=BF16)

    def reduced(name, after, shards):
        st, blocks = started[name]
        lands = _exchange_wait("reduce_scatter_" + name + "_wait", st, True, after)
        return [_sum_adamw("adamw_" + wn, land, lax.dynamic_index_in_dim(blk, me, 0, keepdims=False), w[0], m[0], v[0])
                for land, blk, (wn, w, m, v) in zip(lands, blocks, shards)]

    two = lambda t: t.reshape(2, CONV_CH)
    small_g = jnp.concatenate([
        d_conv_w,
        d_conv_small[0:3],
        d_attn_g,
        two(d_g_pre_mix), two(d_g_post_mix), two(d_g_pre_ffn), two(d_g_post_ffn),
        jnp.zeros((4, CONV_CH), F32) + sent], axis=0)
    small_g = _all_reduce_small(small_g)
    g_conv_w = lax.dynamic_slice(small_g, (0, me * (CONV_CH // N_DEV)), (CONV_WIDTH, CONV_CH // N_DEV))
    pack = lambda cb, lg, lb, ag, g1, g2, g3, g4: jnp.concatenate(
        [cb, lg, lb, ag.reshape(1, SB_WIDTH), two(g1), two(g2), two(g3), two(g4), jnp.zeros((4, CONV_CH), F32)], axis=0)
    sm_g = small_g[CONV_HALO:]
    sm_delta, sm_m, sm_v = _adamw_small(
        "adamw_small",
        pack(conv_b, conv_ln_g, conv_ln_b, attn_norm_g, g_pre_mix, g_post_mix, g_pre_ffn, g_post_ffn), sm_g,
        pack(m_conv_b, m_conv_ln_g, m_conv_ln_b, m_attn_norm_g, m_g_pre_mix, m_g_post_mix, m_g_pre_ffn, m_g_post_ffn),
        pack(v_conv_b, v_conv_ln_g, v_conv_ln_b, v_attn_norm_g, v_g_pre_mix, v_g_post_mix, v_g_pre_ffn, v_g_post_ffn))
    cw_delta, cw_m, cw_v = _adamw_small("adamw_conv_w", cw_shard, g_conv_w,
                                        m_conv_w.reshape(cw_shard.shape), v_conv_w.reshape(cw_shard.shape))

    ffn = reduced("ffn", grad_x, [("w_gate", w_gate, m_w_gate, v_w_gate), ("w_up", w_up, m_w_up, v_w_up),
                                  ("w_down", w_down, m_w_down, v_w_down)])
    big = {"w_gate": ffn[0], "w_up": ffn[1], "w_down": ffn[2],
           "w_out": reduced("w_out", ffn[2][0], [("w_out", w_out, m_w_out, v_w_out)])[0]}
    big["w_in"] = reduced("w_in", big["w_out"][0], [("w_in", w_in, m_w_in, v_w_in)])[0]

    def unpack(t):
        return {"conv_b": t[0:1], "conv_ln_g": t[1:2], "conv_ln_b": t[2:3], "attn_norm_g": t[3:4].reshape(1, SB_HEADS, SB_HEAD_DIM),
                "g_pre_mix": t[4:6].reshape(1, D_MODEL), "g_post_mix": t[6:8].reshape(1, D_MODEL),
                "g_pre_ffn": t[8:10].reshape(1, D_MODEL), "g_post_ffn": t[10:12].reshape(1, D_MODEL)}

    names = ["g_pre_mix", "w_in", "conv_w", "conv_b", "conv_ln_g", "conv_ln_b", "attn_norm_g", "w_out", "g_post_mix",
             "g_pre_ffn", "w_gate", "w_up", "w_down", "g_post_ffn"]
    kinds = []
    for idx, small in enumerate((sm_g, sm_delta, sm_m, sm_v)):
        d = unpack(small)
        d["conv_w"] = (g_conv_w, cw_delta, cw_m, cw_v)[idx].reshape(1, CONV_WIDTH, 1, CONV_CH // N_DEV)
        for n in big:
            d[n] = big[n][idx][None]
        kinds.append([d[n] for n in names])

    loss = lax.psum(loss_part[0, 0], ("x", "y", "c"))
    return (loss, grad_x[None], *kinds[0], *kinds[1], *kinds[2], *kinds[3])
```

```python
import functools
import math

import jax
import jax.numpy as jnp
from jax import lax
from jax.experimental import pallas as pl
from jax.experimental.pallas import tpu as pltpu

F32 = jnp.float32
BF16 = jnp.bfloat16
MESH = pl.DeviceIdType.MESH

N_DEV = 8
D_MODEL = 1024
CONV_CH = 512
CONV_WIDTH = 31
SB_HEADS = 8
SB_HEAD_DIM = 64
SB_WIDTH = SB_HEADS * SB_HEAD_DIM
D_FF = 2816
EPS = 1e-6
LOG2E = 1.4426950408889634
MASKED = -1e30
ADAM_LR = 0.001
ADAM_B1 = 0.9
ADAM_B2 = 0.999
ADAM_EPS = 1e-08
ADAM_WD = 0.01
ADAM_STEP = 10

SUBLANES = 8
LANES = 128
VMEM_LIMIT = 56 * 1024 * 1024
TOKEN_TILE = 512
FFN_TILE = 256
ATTN_UNROLL = 4
ATTN_STRIP = 32
ATTN_BLOCK = 256
CONV_HALO = 32
CONV_CHUNK = 64
FF_CHUNK = D_FF // 2


def _cparams(**kw):
    return pltpu.CompilerParams(vmem_limit_bytes=VMEM_LIMIT, **kw)


def _resident(shape):
    return pl.BlockSpec(shape, lambda *_: (0,) * len(shape), pipeline_mode=pl.Buffered(1))


def _const(shape):
    return pl.BlockSpec(shape, lambda *_: (0,) * len(shape))


def _rms_r(xf):
    return lax.rsqrt(jnp.mean(xf * xf, axis=-1, keepdims=True) + EPS)


def _rms_bwd(xf, r, g, dout):
    w = dout * g
    return r * (w - xf * (r * r) * jnp.mean(w * xf, axis=-1, keepdims=True))


def _dot(a, b):
    return jnp.dot(a, b, preferred_element_type=F32)


def _dot_nt(a, b):
    return lax.dot_general(a, b, (((1,), (1,)), ((), ())), preferred_element_type=F32)


def _dot_tn(a, b):
    return lax.dot_general(a, b, (((0,), (0,)), ((), ())), preferred_element_type=F32)


def _peer(x, y, c, k):
    px = 1 - x if (k >> 2) & 1 else x
    py = 1 - y if (k >> 1) & 1 else y
    pc = 1 - c if k & 1 else c
    return (px, py, pc), 4 * px + 2 * py + pc


def _all_gather(shards):
    n = len(shards)

    def body(*refs):
        ins, outs, done = refs[:n], refs[n:2 * n], refs[2 * n]
        send_sems, recv_sems, local_sems = refs[2 * n + 1:]
        x, y, c = lax.axis_index("x"), lax.axis_index("y"), lax.axis_index("c")
        me, sibling = (x, y, c), (x, y, 1 - c)
        chips = [(1 - x, y), (x, 1 - y), (1 - x, 1 - y)]
        number = lambda d: 4 * d[0] + 2 * d[1] + d[2]

        def copy(a, k, block, to, src=None):
            rows = outs[a].at[number(block)]
            return pltpu.make_async_remote_copy(
                src_ref=rows if src is None else src, dst_ref=rows, send_sem=send_sems.at[a * (N_DEV - 1) + k],
                recv_sem=recv_sems.at[a * (N_DEV - 1) + k], device_id=to, device_id_type=MESH)

        copies = [pltpu.make_async_copy(ins[a], outs[a].at[number(me)], local_sems.at[a]) for a in range(n)]
        for mine in copies:
            mine.start()
        sent = [copy(a, 0, me, sibling, src=ins[a]) for a in range(n)]
        sent += [copy(a, 1 + j, me, (*chip, c), src=ins[a]) for j, chip in enumerate(chips) for a in range(n)]
        for cp in sent:
            cp.start()
        for j, chip in enumerate(chips):
            for a in range(n):
                copy(a, 1 + j, (*chip, c), me).wait_recv()
                passed = copy(a, 4 + j, (*chip, c), sibling)
                passed.start()
                sent.append(passed)
        for a in range(n):
            copy(a, 0, sibling, me).wait_recv()
            for j, chip in enumerate(chips):
                copy(a, 4 + j, (*chip, 1 - c), me).wait_recv()
        for cp in sent:
            cp.wait_send()
        for mine in copies:
            mine.wait()
        done[...] = jnp.zeros_like(done)

    any_spec = pl.BlockSpec(memory_space=pl.ANY)
    return pl.pallas_call(
        body, name="all_gather_weights",
        out_shape=[jax.ShapeDtypeStruct((N_DEV,) + s.shape, s.dtype) for s in shards] + [jax.ShapeDtypeStruct((8, LANES), F32)],
        in_specs=[any_spec] * n, out_specs=[any_spec] * n + [pl.BlockSpec(memory_space=pltpu.VMEM)],
        scratch_shapes=[pltpu.SemaphoreType.DMA((n * (N_DEV - 1),)), pltpu.SemaphoreType.DMA((n * (N_DEV - 1),)),
                        pltpu.SemaphoreType.DMA((n,))],
        compiler_params=pltpu.CompilerParams(has_side_effects=True),
    )(*shards)


def _adamw(w, g, m, v):
    m = ADAM_B1 * m + (1.0 - ADAM_B1) * g
    v = ADAM_B2 * v + (1.0 - ADAM_B2) * (g * g)
    m_hat = m / (1.0 - ADAM_B1 ** ADAM_STEP)
    v_hat = v / (1.0 - ADAM_B2 ** ADAM_STEP)
    delta = -ADAM_LR * (m_hat / (jnp.sqrt(v_hat) + ADAM_EPS) + ADAM_WD * w)
    return delta, m, v


def _exchange_and_sum(src_block, recv_ref, send_sems, recv_sems, local_sem):
    x, y, c = lax.axis_index("x"), lax.axis_index("y"), lax.axis_index("c")
    me = 4 * x + 2 * y + c
    mine = pltpu.make_async_copy(src_block(me), recv_ref.at[me], local_sem)
    mine.start()
    for k in range(1, N_DEV):
        peer, peer_block = _peer(x, y, c, k)
        pltpu.make_async_remote_copy(
            src_ref=src_block(peer_block), dst_ref=recv_ref.at[me], send_sem=send_sems.at[k - 1],
            recv_sem=recv_sems.at[k - 1], device_id=peer, device_id_type=MESH).start()
    for k in range(1, N_DEV):
        peer, peer_block = _peer(x, y, c, k)
        arrived = pltpu.make_async_remote_copy(
            src_ref=src_block(peer_block), dst_ref=recv_ref.at[peer_block], send_sem=send_sems.at[k - 1],
            recv_sem=recv_sems.at[k - 1], device_id=peer, device_id_type=MESH)
        arrived.wait_send()
        arrived.wait_recv()
    mine.wait()


HBM_SPEC = pl.BlockSpec(memory_space=pltpu.HBM)
SEM_SPEC = pl.BlockSpec(memory_space=pltpu.SEMAPHORE)
DATAFLOW = pltpu.SideEffectType.DATAFLOW_SIDE_EFFECTING


def _exchange_copies(srcs, lands, send_sems, recv_sems, scatter, wait):
    x, y, c = lax.axis_index("x"), lax.axis_index("y"), lax.axis_index("c")
    me = 4 * x + 2 * y + c
    for k in range(1, N_DEV):
        peer, peer_block = _peer(x, y, c, k)
        for a in range(len(srcs)):
            s = a * (N_DEV - 1) + k - 1
            src = srcs[a].at[peer_block] if scatter else srcs[a]
            copy = pltpu.make_async_remote_copy(
                src_ref=src, dst_ref=lands[a].at[peer_block if wait else me], send_sem=send_sems.at[s],
                recv_sem=recv_sems.at[s], device_id=peer, device_id_type=MESH)
            if wait:
                copy.wait_send()
                copy.wait_recv()
            else:
                copy.start()


def _exchange_start(name, arrays, scatter):
    n = len(arrays)
    land_shapes = [a.shape if scatter else (N_DEV,) + a.shape for a in arrays]

    def body(*refs):
        _exchange_copies(refs[:n], refs[n:2 * n], refs[2 * n], refs[2 * n + 1], scatter, wait=False)
        refs[-1][...] = jnp.zeros_like(refs[-1])

    sems = pltpu.SemaphoreType.DMA((n * (N_DEV - 1),))
    hbm = lambda t: pltpu.with_memory_space_constraint(t, pltpu.HBM)
    return pl.pallas_call(
        body, name=name,
        out_shape=(sems, sems, *[pltpu.HBM(a.shape, a.dtype) for a in arrays],
                   *[pltpu.HBM(ls, a.dtype) for ls, a in zip(land_shapes, arrays)], jax.ShapeDtypeStruct((8, LANES), F32)),
        in_specs=[HBM_SPEC] * (2 * n),
        out_specs=(SEM_SPEC, SEM_SPEC, *[HBM_SPEC] * (2 * n), pl.BlockSpec(memory_space=pltpu.VMEM)),
        input_output_aliases={a: 2 + a for a in range(2 * n)},
        compiler_params=pltpu.CompilerParams(has_side_effects=DATAFLOW),
    )(*[hbm(a) for a in arrays], *[hbm(lax.empty(ls, a.dtype)) for ls, a in zip(land_shapes, arrays)])


def _exchange_wait(name, started, scatter, after):
    n = (len(started) - 3) // 2
    send_sems, recv_sems = started[0], started[1]
    arrays, lands = started[2:2 + n], started[2 + n:2 + 2 * n]

    def body(*refs):
        _exchange_copies(refs[:n], refs[n:2 * n], refs[2 * n], refs[2 * n + 1], scatter, wait=True)

    return pl.pallas_call(
        body, name=name,
        out_shape=[pltpu.HBM(t.shape, t.dtype) for t in (*arrays, *lands)],
        in_specs=[HBM_SPEC] * (2 * n) + [SEM_SPEC, SEM_SPEC, pl.BlockSpec(memory_space=pl.ANY)],
        out_specs=[HBM_SPEC] * (2 * n),
        input_output_aliases={a: a for a in range(2 * n)},
        compiler_params=pltpu.CompilerParams(has_side_effects=DATAFLOW),
    )(*arrays, *lands, send_sems, recv_sems, after)[n:]


def _sum_adamw(name, land, own, w, m, v):
    _, M, N = land.shape
    rows = math.gcd(M, 128)

    def body(land_ref, own_ref, w_ref, m_ref, v_ref, grad_ref, delta_ref, nm_ref, nv_ref):
        x, y, c = lax.axis_index("x"), lax.axis_index("y"), lax.axis_index("c")
        g = own_ref[...]
        for k in range(1, N_DEV):
            g = g + land_ref[_peer(x, y, c, k)[1]].astype(F32)
        delta, nm, nv = _adamw(w_ref[...], g, m_ref[...], v_ref[...])
        grad_ref[...] = g
        delta_ref[...] = delta
        nm_ref[...] = nm
        nv_ref[...] = nv

    row = pl.BlockSpec((rows, N), lambda i: (i, 0))
    return pl.pallas_call(
        body, name=name, grid=(M // rows,), out_shape=[jax.ShapeDtypeStruct((M, N), F32)] * 4,
        in_specs=[pl.BlockSpec((N_DEV, rows, N), lambda i: (0, i, 0)), row, row, row, row], out_specs=[row] * 4,
        compiler_params=_cparams(),
    )(land, own, w, m, v)


def _all_reduce_small(g):
    R, C = g.shape

    def body(g_ref, out_ref, recv_ref, send_sems, recv_sems, local_sem):
        _exchange_and_sum(lambda b: g_ref, recv_ref, send_sems, recv_sems, local_sem)
        total = recv_ref[0]
        for b in range(1, N_DEV):
            total = total + recv_ref[b]
        out_ref[...] = total

    vmem = pl.BlockSpec(memory_space=pltpu.VMEM)
    return pl.pallas_call(
        body, name="all_reduce_small_grads", out_shape=jax.ShapeDtypeStruct((R, C), F32),
        in_specs=[vmem], out_specs=vmem,
        scratch_shapes=[pltpu.VMEM((N_DEV, R, C), F32), pltpu.SemaphoreType.DMA((N_DEV - 1,)),
                        pltpu.SemaphoreType.DMA((N_DEV - 1,)), pltpu.SemaphoreType.DMA(())],
        compiler_params=_cparams(has_side_effects=True),
    )(g)


def _adamw_small(name, w, g, m, v):
    def body(w_ref, g_ref, m_ref, v_ref, delta_ref, nm_ref, nv_ref):
        delta, nm, nv = _adamw(w_ref[...], g_ref[...], m_ref[...], v_ref[...])
        delta_ref[...] = delta
        nm_ref[...] = nm
        nv_ref[...] = nv

    vmem = pl.BlockSpec(memory_space=pltpu.VMEM)
    return pl.pallas_call(body, name=name, out_shape=[jax.ShapeDtypeStruct(w.shape, F32)] * 3,
                          in_specs=[vmem] * 4, out_specs=[vmem] * 3)(w, g, m, v)


def _in_proj(x, g, w_in):
    S = x.shape[0]
    tm = min(TOKEN_TILE, S)
    nconv = 2 * CONV_CH

    def body(x_ref, g_ref, w_ref, a_ref, uc_ref, qkv_ref):
        xf = x_ref[...]
        a = (xf * _rms_r(xf) * g_ref[...]).astype(BF16)
        a_ref[...] = a
        uc_ref[...] = _dot(a, w_ref[:, :nconv])
        qkv_ref[:, :SB_WIDTH] = (_dot(a, w_ref[:, nconv:nconv + SB_WIDTH]) * (1.0 / math.sqrt(SB_HEAD_DIM))).astype(BF16)
        qkv_ref[:, SB_WIDTH:] = _dot(a, w_ref[:, nconv + SB_WIDTH:]).astype(BF16)

    row = lambda n: pl.BlockSpec((tm, n), lambda i: (i, 0))
    return pl.pallas_call(
        body, name="in_proj", grid=(S // tm,),
        out_shape=[jax.ShapeDtypeStruct((S, D_MODEL), BF16), jax.ShapeDtypeStruct((S, nconv), F32),
                   jax.ShapeDtypeStruct((S, 3 * SB_WIDTH), BF16)],
        in_specs=[row(D_MODEL), _const((1, D_MODEL)), _resident(w_in.shape)],
        out_specs=[row(D_MODEL), row(nconv), row(3 * SB_WIDTH)],
        compiler_params=_cparams(),
    )(x, g, w_in)


def _glu(u):
    val, gate = u[:, :CONV_CH], u[:, CONV_CH:]
    sg = jax.nn.sigmoid(gate)
    return val, sg, val * sg


def _shift_copies(ext, shifted):
    n = shifted.shape[1]
    for r in range(1, SUBLANES):
        shifted[r - 1] = ext[r:r + n, :]


def _window(ext, shifted, start, rows):
    r = start % SUBLANES
    return ext[start:start + rows, :] if r == 0 else shifted[r - 1, start - r:start - r + rows, :]


def _conv_rows(glu_ext, glu_sh, cw_ref, r0, rows):
    base = r0 + CONV_HALO - (CONV_WIDTH - 1)
    acc = cw_ref[0:1, :] * _window(glu_ext, glu_sh, base, rows)
    for w in range(1, CONV_WIDTH):
        acc = acc + cw_ref[w:w + 1, :] * _window(glu_ext, glu_sh, base + w, rows)
    return acc


def _conv_fwd(u_conv, conv_w, conv_b, ln_g, ln_b):
    S = u_conv.shape[0]
    tc = min(TOKEN_TILE, S)

    def body(u_ref, cw_ref, cb_ref, lg_ref, lb_ref, out_ref, glu_ext, glu_sh):
        i = pl.program_id(0)

        @pl.when(i == 0)
        def _():
            glu_ext[0:CONV_HALO, :] = jnp.zeros((CONV_HALO, CONV_CH), F32)

        @pl.when(i > 0)
        def _():
            glu_ext[0:CONV_HALO, :] = glu_ext[tc:tc + CONV_HALO, :]

        glu_ext[CONV_HALO:, :] = _glu(u_ref[...])[2]
        _shift_copies(glu_ext, glu_sh)
        for r0 in range(0, tc, CONV_CHUNK):
            y = _conv_rows(glu_ext, glu_sh, cw_ref, r0, CONV_CHUNK) + cb_ref[...]
            mu = jnp.mean(y, axis=-1, keepdims=True)
            yc = y - mu
            yn = yc * lax.rsqrt(jnp.mean(yc * yc, axis=-1, keepdims=True) + EPS)
            yl = yn * lg_ref[...] + lb_ref[...]
            out_ref[r0:r0 + CONV_CHUNK, :] = (yl * jax.nn.sigmoid(yl)).astype(BF16)

    return pl.pallas_call(
        body, name="conv_fwd", grid=(S // tc,),
        out_shape=jax.ShapeDtypeStruct((S, CONV_CH), BF16),
        in_specs=[pl.BlockSpec((tc, 2 * CONV_CH), lambda i: (i, 0)), _const((CONV_HALO, CONV_CH)),
                  _const((1, CONV_CH)), _const((1, CONV_CH)), _const((1, CONV_CH))],
        out_specs=pl.BlockSpec((tc, CONV_CH), lambda i: (i, 0)),
        scratch_shapes=[pltpu.VMEM((tc + CONV_HALO, CONV_CH), F32),
                        pltpu.VMEM((SUBLANES - 1, tc + CONV_HALO - SUBLANES, CONV_CH), F32)],
        compiler_params=_cparams(dimension_semantics=("arbitrary",)),
    )(u_conv, conv_w, conv_b, ln_g, ln_b)


def _head_masks():
    lane = lax.broadcasted_iota(jnp.int32, (1, LANES), 1)
    return lane < SB_HEAD_DIM


def _split_heads(t, first):
    z = jnp.zeros_like(t)
    return jnp.where(first, t, z), jnp.where(first, z, t)


def _head_sum(t, first):
    a = jnp.sum(jnp.where(first, t, 0.0), axis=-1, keepdims=True)
    b = jnp.sum(jnp.where(first, 0.0, t), axis=-1, keepdims=True)
    return a, b


def _attn_fwd(qkv, g_attn):
    S = qkv.shape[0]
    Q = min(ATTN_BLOCK, S)
    nq = S // Q
    assert nq <= LANES
    ntiles = nq * (nq + 1) // 2
    assert ATTN_UNROLL % 2 == 0 and ntiles >= 4 + ATTN_UNROLL
    npair = SB_WIDTH // LANES
    tiles = [(i, j) for i in range(nq) for j in range(i, -1, -1)]

    def body(q_ref, k_ref, v_ref, g_ref, o_ref, ao_ref, cl_ref, z_buf, l_buf, z2_buf, a_buf, c_buf, mask_buf):
        first = _head_masks()
        lane = lax.broadcasted_iota(jnp.int32, (1, LANES), 1)
        row = lax.broadcasted_iota(jnp.int32, (Q, Q), 0)
        col = lax.broadcasted_iota(jnp.int32, (Q, Q), 1)
        tri = (row >= col).astype(BF16)
        heads = range(2)
        strips = [slice(r0, r0 + ATTN_STRIP) for r0 in range(0, Q, ATTN_STRIP)]
        rows = lambda j: pl.ds(pl.multiple_of(j * Q, Q), Q)
        wide = lambda t: jnp.tile(t, (1, Q // LANES))
        as_int = lambda t: int(t) if isinstance(t, (bool, int)) else t.astype(jnp.int32)

        keep = col < row
        mask_buf[0, 0] = jnp.ones((Q, Q), F32)
        mask_buf[0, 1] = jnp.zeros((Q, Q), F32)
        mask_buf[1, 0] = jnp.where(keep, 1.0, 0.0)
        mask_buf[1, 1] = jnp.where(keep, 0.0, MASKED)
        o_ref[...] = jnp.zeros_like(o_ref)

        def scores(t, slot):
            i, j = t
            qh = _split_heads(q_ref[rows(i), :], first)
            kb = k_ref[rows(j), :]
            for h in heads:
                z_buf[slot, h] = _dot_nt(qh[h], kb)

        def logs(t, slot):
            i, j = t
            diag = as_int(i == j)
            for h in heads:
                for r in strips:
                    z2 = z_buf[slot, h, r, :] * LOG2E
                    l = (jnp.minimum(z2, 0.0) - jnp.log2(1.0 + jnp.exp2(-jnp.abs(z2)))) - z2
                    l_buf[slot, h, r, :] = (l * mask_buf[diag, 0, r, :]).astype(BF16)
                    z2_buf[slot, h, r, :] = z2 + mask_buf[diag, 1, r, :]

        def sums(slot):
            return tuple(_dot(l_buf[slot, h], tri) for h in heads)

        def weights(t, slot, sm):
            i, j = t
            running = jnp.where(j == i, 0.0, 1.0)
            for h in heads:
                before = c_buf[h] * running
                for r in strips:
                    a_buf[slot, h, r, :] = jnp.exp2(z2_buf[slot, h, r, :] + sm[h][r] + wide(before[r])).astype(BF16)
                hl = slice(h * LANES, (h + 1) * LANES)
                cl_ref[rows(i), hl] = jnp.where(lane == j, before, cl_ref[rows(i), hl] * running)
                c_buf[h] = before + jnp.broadcast_to(sm[h][:, 0:1], (Q, LANES))

        def values(t, slot):
            i, j = t
            vh = _split_heads(v_ref[rows(j), :], first)
            o_ref[rows(i), :] += _dot(a_buf[slot, 0], vh[0]) + _dot(a_buf[slot, 1], vh[1])

        def iteration(t, p):
            ta, tb, tc, td = t
            if ta is not None:
                scores(ta, p)
            if tc is not None:
                sm = sums(p)
            if td is not None:
                values(td, 1 - p)
            if tb is not None:
                logs(tb, 1 - p)
            if tc is not None:
                weights(tc, p, sm)

        def window(n):
            return tuple(tiles[n - k] if 0 <= n - k < ntiles else None for k in range(4))

        def following(t):
            i, j = t
            last = j == 0
            return jnp.where(last, i + 1, i), jnp.where(last, i + 1, j - 1)

        peeled = 4 + (ntiles - 4) % ATTN_UNROLL

        def unrolled_iterations(_, t):
            for n in range(peeled, peeled + ATTN_UNROLL):
                iteration(t, n % 2)
                t = (following(t[0]),) + t[:3]
            return t

        c_buf[...] = jnp.zeros_like(c_buf)
        for n in range(peeled):
            iteration(window(n), n % 2)
        first_window = tuple((jnp.int32(i), jnp.int32(j)) for i, j in window(peeled))
        lax.fori_loop(0, (ntiles - peeled) // ATTN_UNROLL, unrolled_iterations, first_window)
        for n in range(ntiles, ntiles + 3):
            iteration(window(n), n % 2)

        def head_norm(b, carry):
            o = o_ref[rows(b), :]
            sa, sb = _head_sum(o * o, first)
            r = jnp.where(first, lax.rsqrt(sa * (1.0 / SB_HEAD_DIM) + EPS), lax.rsqrt(sb * (1.0 / SB_HEAD_DIM) + EPS))
            ao_ref[rows(b), :] = (o * r * g_ref[...]).astype(BF16)
            return carry

        lax.fori_loop(0, nq, head_norm, 0)

    col_block = lambda off: pl.BlockSpec((S, LANES), lambda p: (0, off + p), pipeline_mode=pl.Buffered(1))
    out_block = lambda n: pl.BlockSpec((S, n), lambda p: (0, p), pipeline_mode=pl.Buffered(1))
    return pl.pallas_call(
        body, name="attn_fwd", grid=(npair,),
        out_shape=[jax.ShapeDtypeStruct((S, SB_WIDTH), F32), jax.ShapeDtypeStruct((S, SB_WIDTH), BF16),
                   jax.ShapeDtypeStruct((S, 2 * SB_WIDTH), F32)],
        in_specs=[col_block(0), col_block(npair), col_block(2 * npair), pl.BlockSpec((1, LANES), lambda p: (0, p))],
        out_specs=[out_block(LANES), out_block(LANES), out_block(2 * LANES)],
        scratch_shapes=[pltpu.VMEM((2, 2, Q, Q), F32), pltpu.VMEM((2, 2, Q, Q), BF16), pltpu.VMEM((2, 2, Q, Q), F32),
                        pltpu.VMEM((2, 2, Q, Q), BF16), pltpu.VMEM((2, Q, LANES), F32), pltpu.VMEM((2, 2, Q, Q), F32)],
        compiler_params=_cparams(dimension_semantics=("arbitrary",)),
    )(qkv, qkv, qkv, g_attn)


def _out_proj(conv_out, attn_out, w_out, x, g_post_mix, g_pre_ffn):
    S = x.shape[0]
    tm = min(TOKEN_TILE, S)

    def body(co_ref, ao_ref, w_ref, x_ref, g1_ref, g2_ref, y_ref, h1_ref, fin_ref):
        y = _dot(co_ref[...], w_ref[:CONV_CH, :]) + _dot(ao_ref[...], w_ref[CONV_CH:, :])
        h1 = x_ref[...] + y * _rms_r(y) * g1_ref[...]
        y_ref[...] = y
        h1_ref[...] = h1
        fin_ref[...] = (h1 * _rms_r(h1) * g2_ref[...]).astype(BF16)

    row = lambda n: pl.BlockSpec((tm, n), lambda i: (i, 0))
    return pl.pallas_call(
        body, name="out_proj", grid=(S // tm,),
        out_shape=[jax.ShapeDtypeStruct((S, D_MODEL), F32), jax.ShapeDtypeStruct((S, D_MODEL), F32),
                   jax.ShapeDtypeStruct((S, D_MODEL), BF16)],
        in_specs=[row(CONV_CH), row(SB_WIDTH), _resident(w_out.shape), row(D_MODEL), _const((1, D_MODEL)),
                  _const((1, D_MODEL))],
        out_specs=[row(D_MODEL)] * 3,
        compiler_params=_cparams(),
    )(conv_out, attn_out, w_out, x, g_post_mix, g_pre_ffn)


def _ffn_fwd_loss(f_in, w_gate, w_up, w_down, h1, target, g_post_ffn):
    S = f_in.shape[0]
    tm = min(FFN_TILE, S)
    nt = S // tm

    def body(fin_ref, wg_ref, wu_ref, wd_ref, h1_ref, t_ref, g_ref, gt_ref, up_ref, df_ref, dh2_ref, loss_ref, dg_ref,
             sq_acc):
        i = pl.program_id(0)

        @pl.when(i == 0)
        def _():
            sq_acc[...] = jnp.zeros_like(sq_acc)
            dg_ref[...] = jnp.zeros_like(dg_ref)

        fin = fin_ref[...]
        f = jnp.zeros((tm, D_MODEL), F32)
        for c0 in range(0, D_FF, FF_CHUNK):
            cols = slice(c0, c0 + FF_CHUNK)
            gt = _dot(fin, wg_ref[:, cols])
            up = _dot(fin, wu_ref[:, cols])
            gt_ref[:, cols] = gt.astype(BF16)
            up_ref[:, cols] = up.astype(BF16)
            f = f + _dot((gt * jax.nn.sigmoid(gt) * up).astype(BF16), wd_ref[cols, :])
        r = _rms_r(f)
        g = g_ref[...]
        diff = h1_ref[...] + f * r * g - t_ref[...]
        sq_acc[...] += jnp.sum(diff * diff, axis=0, keepdims=True)
        dh2 = diff * (1.0 / D_MODEL)
        dh2_ref[...] = dh2
        dg_ref[...] += jnp.sum(dh2 * f * r, axis=0, keepdims=True)
        df_ref[...] = _rms_bwd(f, r, g, dh2).astype(BF16)

        @pl.when(i == nt - 1)
        def _():
            loss_ref[...] = jnp.broadcast_to((0.5 / D_MODEL) * jnp.sum(sq_acc[...], axis=-1, keepdims=True), (1, LANES))

    row = lambda n: pl.BlockSpec((tm, n), lambda i: (i, 0))
    return pl.pallas_call(
        body, name="ffn_fwd_loss", grid=(nt,),
        out_shape=[jax.ShapeDtypeStruct((S, D_FF), BF16), jax.ShapeDtypeStruct((S, D_FF), BF16),
                   jax.ShapeDtypeStruct((S, D_MODEL), BF16), jax.ShapeDtypeStruct((S, D_MODEL), F32),
                   jax.ShapeDtypeStruct((1, LANES), F32), jax.ShapeDtypeStruct((1, D_MODEL), F32)],
        in_specs=[row(D_MODEL), _resident(w_gate.shape), _resident(w_up.shape), _resident(w_down.shape),
                  row(D_MODEL), row(D_MODEL), _const((1, D_MODEL))],
        out_specs=[row(D_FF), row(D_FF), row(D_MODEL), row(D_MODEL), _const((1, LANES)), _const((1, D_MODEL))],
        scratch_shapes=[pltpu.VMEM((1, D_MODEL), F32)],
        compiler_params=_cparams(dimension_semantics=("arbitrary",)),
    )(f_in, w_gate, w_up, w_down, h1, target, g_post_ffn)


def _ffn_bwd(df, gt, up, w_gate, w_up, w_down):
    S = df.shape[0]
    tm = min(FFN_TILE, S)

    def body(df_ref, gt_ref, up_ref, wg_ref, wu_ref, wd_ref, dgt_ref, dup_ref, act_ref, dfin_ref):
        df = df_ref[...]
        dfin = jnp.zeros((tm, D_MODEL), F32)
        for c0 in range(0, D_FF, FF_CHUNK):
            cols = slice(c0, c0 + FF_CHUNK)
            dact = _dot_nt(df, wd_ref[cols, :])
            gt = gt_ref[:, cols].astype(F32)
            up = up_ref[:, cols].astype(F32)
            s = jax.nn.sigmoid(gt)
            silu = gt * s
            dgt = (dact * up * (s * (1.0 + gt * (1.0 - s)))).astype(BF16)
            dup = (dact * silu).astype(BF16)
            act_ref[:, cols] = (silu * up).astype(BF16)
            dgt_ref[:, cols] = dgt
            dup_ref[:, cols] = dup
            dfin = dfin + _dot_nt(dgt, wg_ref[:, cols]) + _dot_nt(dup, wu_ref[:, cols])
        dfin_ref[...] = dfin

    row = lambda n: pl.BlockSpec((tm, n), lambda i: (i, 0))
    return pl.pallas_call(
        body, name="ffn_bwd", grid=(S // tm,),
        out_shape=[jax.ShapeDtypeStruct((S, D_FF), BF16)] * 3 + [jax.ShapeDtypeStruct((S, D_MODEL), F32)],
        in_specs=[row(D_MODEL), row(D_FF), row(D_FF), _resident(w_gate.shape), _resident(w_up.shape),
                  _resident(w_down.shape)],
        out_specs=[row(D_FF)] * 3 + [row(D_MODEL)],
        compiler_params=_cparams(),
    )(df, gt, up, w_gate, w_up, w_down)


def _matmul_tn(name, x, y, tn):
    S, K = x.shape
    N = y.shape[1]
    ts = min(TOKEN_TILE, S)

    def body(x_ref, y_ref, o_ref):
        @pl.when(pl.program_id(1) == 0)
        def _():
            o_ref[...] = jnp.zeros_like(o_ref)

        o_ref[...] += _dot_tn(x_ref[...].astype(BF16), y_ref[...].astype(BF16))

    return pl.pallas_call(
        body, name=name, grid=(N // tn, S // ts),
        out_shape=jax.ShapeDtypeStruct((K, N), F32),
        in_specs=[pl.BlockSpec((ts, K), lambda n, s: (s, 0)), pl.BlockSpec((ts, tn), lambda n, s: (s, n))],
        out_specs=pl.BlockSpec((K, tn), lambda n, s: (0, n)),
        compiler_params=_cparams(dimension_semantics=("arbitrary", "arbitrary")),
    )(x, y)


def _mix_bwd(dfin, h1, y, dh2, g_pre_ffn, g_post_mix, w_out):
    S = dfin.shape[0]
    tm = min(TOKEN_TILE, S)

    def body(dfin_ref, h1_ref, y_ref, dh2_ref, g2_ref, g1_ref, w_ref, dh1_ref, dy_ref, dco_ref, dao_ref, dg2_ref, dg1_ref):
        @pl.when(pl.program_id(0) == 0)
        def _():
            dg2_ref[...] = jnp.zeros_like(dg2_ref)
            dg1_ref[...] = jnp.zeros_like(dg1_ref)

        h1, dfin = h1_ref[...], dfin_ref[...]
        r2 = _rms_r(h1)
        dh1 = dh2_ref[...] + _rms_bwd(h1, r2, g2_ref[...], dfin)
        dg2_ref[...] += jnp.sum(dfin * h1 * r2, axis=0, keepdims=True)
        y = y_ref[...]
        r1 = _rms_r(y)
        dy = _rms_bwd(y, r1, g1_ref[...], dh1).astype(BF16)
        dg1_ref[...] += jnp.sum(dh1 * y * r1, axis=0, keepdims=True)
        dh1_ref[...] = dh1
        dy_ref[...] = dy
        dco_ref[...] = _dot_nt(dy, w_ref[:CONV_CH, :])
        dao_ref[...] = _dot_nt(dy, w_ref[CONV_CH:, :])

    row = lambda n: pl.BlockSpec((tm, n), lambda i: (i, 0))
    return pl.pallas_call(
        body, name="mix_bwd", grid=(S // tm,),
        out_shape=[jax.ShapeDtypeStruct((S, D_MODEL), F32), jax.ShapeDtypeStruct((S, D_MODEL), BF16),
                   jax.ShapeDtypeStruct((S, CONV_CH), F32), jax.ShapeDtypeStruct((S, SB_WIDTH), F32),
                   jax.ShapeDtypeStruct((1, D_MODEL), F32), jax.ShapeDtypeStruct((1, D_MODEL), F32)],
        in_specs=[row(D_MODEL)] * 4 + [_const((1, D_MODEL)), _const((1, D_MODEL)), _resident(w_out.shape)],
        out_specs=[row(D_MODEL), row(D_MODEL), row(CONV_CH), row(SB_WIDTH), _const((1, D_MODEL)), _const((1, D_MODEL))],
        compiler_params=_cparams(dimension_semantics=("arbitrary",)),
    )(dfin, h1, y, dh2, g_pre_ffn, g_post_mix, w_out)


def _attn_norm_bwd(o, dao, g_attn):
    S = o.shape[0]
    tm = min(TOKEN_TILE, S)
    inv_dh = 1.0 / SB_HEAD_DIM

    def body(o_ref, dao_ref, g_ref, do_ref, dg_ref):
        @pl.when(pl.program_id(0) == 0)
        def _():
            dg_ref[...] = jnp.zeros_like(dg_ref)

        first = _head_masks()
        for p in range(SB_WIDTH // LANES):
            cols = slice(p * LANES, (p + 1) * LANES)
            o, dao, g = o_ref[:, cols], dao_ref[:, cols], g_ref[:, cols]
            sa, sb = _head_sum(o * o, first)
            r = jnp.where(first, lax.rsqrt(sa * inv_dh + EPS), lax.rsqrt(sb * inv_dh + EPS))
            w = dao * g
            wa, wb = _head_sum(w * o, first)
            do_ref[:, cols] = (r * (w - o * (r * r) * (jnp.where(first, wa, wb) * inv_dh))).astype(BF16)
            dg_ref[:, cols] += jnp.sum(dao * o * r, axis=0, keepdims=True)

    row = pl.BlockSpec((tm, SB_WIDTH), lambda i: (i, 0))
    return pl.pallas_call(
        body, name="attn_norm_bwd", grid=(S // tm,),
        out_shape=[jax.ShapeDtypeStruct((S, SB_WIDTH), BF16), jax.ShapeDtypeStruct((1, SB_WIDTH), F32)],
        in_specs=[row, row, _const((1, SB_WIDTH))], out_specs=[row, _const((1, SB_WIDTH))],
        compiler_params=_cparams(dimension_semantics=("arbitrary",)),
    )(o, dao, g_attn)


def _attn_bwd(qkv, do, cl):
    S = qkv.shape[0]
    Q = min(ATTN_BLOCK, S)
    nq = S // Q
    ntiles = nq * (nq + 1) // 2
    assert ATTN_UNROLL % 2 == 0 and ntiles >= 4 + ATTN_UNROLL
    npair = SB_WIDTH // LANES
    tiles = [(i, j) for i in range(nq) for j in range(i + 1)]

    def body(q_ref, k_ref, v_ref, do_ref, cl_ref, dq_ref, dk_ref, dv_ref,
             z_buf, lb_buf, be_buf, g_buf, l_buf, a_buf, gb_buf, dz_buf, pg_buf, mask_buf):
        first = _head_masks()
        lane = lax.broadcasted_iota(jnp.int32, (1, LANES), 1)
        row = lax.broadcasted_iota(jnp.int32, (Q, Q), 0)
        col = lax.broadcasted_iota(jnp.int32, (Q, Q), 1)
        tri = (row > col).astype(BF16)
        tpi = (row <= col).astype(BF16)
        heads = range(2)
        strips = [slice(r0, r0 + ATTN_STRIP) for r0 in range(0, Q, ATTN_STRIP)]
        rows = lambda j: pl.ds(pl.multiple_of(j * Q, Q), Q)
        wide = lambda t: jnp.tile(t, (1, Q // LANES))
        as_int = lambda t: int(t) if isinstance(t, (bool, int)) else t.astype(jnp.int32)

        keep = col < row
        mask_buf[0, 0] = jnp.ones((Q, Q), F32)
        mask_buf[0, 1] = jnp.zeros((Q, Q), F32)
        mask_buf[1, 0] = jnp.where(keep, 1.0, 0.0)
        mask_buf[1, 1] = jnp.where(keep, 0.0, MASKED)
        dq_ref[...] = jnp.zeros_like(dq_ref)
        dk_ref[...] = jnp.zeros_like(dk_ref)
        dv_ref[...] = jnp.zeros_like(dv_ref)

        def scores(t, slot):
            i, j = t
            qh = _split_heads(q_ref[rows(i), :], first)
            kb = k_ref[rows(j), :]
            for h in heads:
                z_buf[slot, h] = _dot_nt(qh[h], kb)

        def logs(t, slot):
            i, j = t
            diag = as_int(i == j)
            for h in heads:
                for r in strips:
                    z2 = z_buf[slot, h, r, :] * LOG2E
                    lb = jnp.minimum(z2, 0.0) - jnp.log2(1.0 + jnp.exp2(-jnp.abs(z2)))
                    l = (lb - z2) * mask_buf[diag, 0, r, :]
                    l_buf[slot, h, r, :] = l.astype(BF16)
                    lb_buf[slot, h, r, :] = lb + mask_buf[diag, 1, r, :]

        def sums(t, slot):
            i, j = t
            doh = _split_heads(do_ref[rows(i), :], first)
            vb = v_ref[rows(j), :]
            return (tuple(_dot(l_buf[slot, h], tri) for h in heads), tuple(_dot_nt(doh[h], vb) for h in heads))

        def weights(t, slot, sm, da):
            i, j = t
            for h in heads:
                c = jnp.sum(jnp.where(lane == j, cl_ref[rows(i), h * LANES:(h + 1) * LANES], 0.0), axis=-1, keepdims=True)
                c = jnp.broadcast_to(c, (Q, LANES))
                for r in strips:
                    lb = lb_buf[slot, h, r, :]
                    a = jnp.exp2(lb + sm[h][r] + wide(c[r]))
                    g = da[h][r] * a
                    a_buf[slot, h, r, :] = a.astype(BF16)
                    be_buf[slot, h, r, :] = jnp.exp2(lb)
                    g_buf[slot, h, r, :] = g
                    gb_buf[slot, h, r, :] = g.astype(BF16)

        def prefix(t, slot):
            i, j = t
            doh = _split_heads(do_ref[rows(i), :], first)
            dv_ref[rows(j), :] += _dot_tn(a_buf[slot, 0], doh[0]) + _dot_tn(a_buf[slot, 1], doh[1])
            return tuple(_dot(gb_buf[slot, h], tpi) for h in heads)

        def dscores(t, slot, pm):
            i, j = t
            for h in heads:
                pg = pg_buf[h] * jnp.where(j == 0, 0.0, 1.0)
                for r in strips:
                    dz = g_buf[slot, h, r, :] - be_buf[slot, h, r, :] * (pm[h][r] + wide(pg[r]))
                    dz_buf[slot, h, r, :] = dz.astype(BF16)
                pg_buf[h] = pg + jnp.broadcast_to(pm[h][:, Q - 1:Q], (Q, LANES))

        def grads(t, slot):
            i, j = t
            qh = _split_heads(q_ref[rows(i), :], first)
            kh = _split_heads(k_ref[rows(j), :], first)
            dq_ref[rows(i), :] += _dot(dz_buf[slot, 0], kh[0]) + _dot(dz_buf[slot, 1], kh[1])
            dk_ref[rows(j), :] += _dot_tn(dz_buf[slot, 0], qh[0]) + _dot_tn(dz_buf[slot, 1], qh[1])

        def iteration(t, p):
            ta, tb, tc, td, te = t
            if ta is not None:
                scores(ta, p)
            if tc is not None:
                sm, da = sums(tc, p)
            if td is not None:
                pm = prefix(td, 1 - p)
            if te is not None:
                grads(te, p)
            if tb is not None:
                logs(tb, 1 - p)
            if tc is not None:
                weights(tc, p, sm, da)
            if td is not None:
                dscores(td, 1 - p, pm)

        def window(n):
            return tuple(tiles[n - k] if 0 <= n - k < ntiles else None for k in range(5))

        def following(t):
            i, j = t
            last = j == i
            return jnp.where(last, i + 1, i), jnp.where(last, 0, j + 1)

        peeled = 4 + (ntiles - 4) % ATTN_UNROLL

        def unrolled_iterations(_, t):
            for n in range(peeled, peeled + ATTN_UNROLL):
                iteration(t, n % 2)
                t = (following(t[0]),) + t[:4]
            return t

        pg_buf[...] = jnp.zeros_like(pg_buf)
        for n in range(peeled):
            iteration(window(n), n % 2)
        first_window = tuple((jnp.int32(i), jnp.int32(j)) for i, j in window(peeled))
        lax.fori_loop(0, (ntiles - peeled) // ATTN_UNROLL, unrolled_iterations, first_window)
        for n in range(ntiles, ntiles + 4):
            iteration(window(n), n % 2)
        dq_ref[...] = dq_ref[...] * (1.0 / math.sqrt(SB_HEAD_DIM))

    col_block = lambda off: pl.BlockSpec((S, LANES), lambda p: (0, off + p), pipeline_mode=pl.Buffered(1))
    return pl.pallas_call(
        body, name="attn_bwd", grid=(npair,),
        out_shape=[jax.ShapeDtypeStruct((S, SB_WIDTH), F32)] * 3,
        in_specs=[col_block(0), col_block(npair), col_block(2 * npair), col_block(0),
                  pl.BlockSpec((S, 2 * LANES), lambda p: (0, p), pipeline_mode=pl.Buffered(1))],
        out_specs=[pl.BlockSpec((S, LANES), lambda p: (0, p), pipeline_mode=pl.Buffered(1))] * 3,
        scratch_shapes=[pltpu.VMEM((2, 2, Q, Q), F32)] * 4 + [pltpu.VMEM((2, 2, Q, Q), BF16)] * 4
        + [pltpu.VMEM((2, Q, LANES), F32), pltpu.VMEM((2, 2, Q, Q), F32)],
        compiler_params=_cparams(dimension_semantics=("arbitrary",)),
    )(qkv, qkv, qkv, do, cl)


def _conv_bwd(u_conv, dco, conv_w, conv_b, ln_g, ln_b):
    S = u_conv.shape[0]
    tc = min(TOKEN_TILE, S)
    nt = S // tc
    per = tc // CONV_HALO
    groups = CONV_CHUNK // 8

    def body(u_ref, halo_ref, dco_ref, cw_ref, cb_ref, lg_ref, lb_ref, du_ref, dcw_ref, dsm_ref, glu_ext, dyc_ext, sg_buf,
             dcw_acc, dsm_acc, glu_sh, dyc_sh):
        i = pl.program_id(0)
        ti = nt - 1 - i

        @pl.when(i == 0)
        def _():
            dyc_ext[tc:, :] = jnp.zeros((CONV_HALO, CONV_CH), F32)
            dcw_acc[...] = jnp.zeros_like(dcw_acc)
            dsm_acc[...] = jnp.zeros_like(dsm_acc)

        @pl.when(i > 0)
        def _():
            dyc_ext[tc:, :] = dyc_ext[0:CONV_HALO, :]

        glu_ext[0:CONV_HALO, :] = jnp.where(ti > 0, _glu(halo_ref[...])[2], 0.0)
        val, sg, glu = _glu(u_ref[...])
        glu_ext[CONV_HALO:, :] = glu
        sg_buf[...] = sg
        _shift_copies(glu_ext, glu_sh)

        dcb = jnp.zeros((8, CONV_CH), F32)
        dlg = jnp.zeros((8, CONV_CH), F32)
        dlb = jnp.zeros((8, CONV_CH), F32)
        fold = lambda t: jnp.sum(t.reshape(groups, 8, CONV_CH), axis=0)
        for r0 in range(0, tc, CONV_CHUNK):
            y = _conv_rows(glu_ext, glu_sh, cw_ref, r0, CONV_CHUNK) + cb_ref[...]
            mu = jnp.mean(y, axis=-1, keepdims=True)
            yc = y - mu
            rstd = lax.rsqrt(jnp.mean(yc * yc, axis=-1, keepdims=True) + EPS)
            yn = yc * rstd
            yl = yn * lg_ref[...] + lb_ref[...]
            s = jax.nn.sigmoid(yl)
            dyl = dco_ref[r0:r0 + CONV_CHUNK, :] * (s * (1.0 + yl * (1.0 - s)))
            dlg = dlg + fold(dyl * yn)
            dlb = dlb + fold(dyl)
            wv = dyl * lg_ref[...]
            dyc = rstd * (wv - jnp.mean(wv, axis=-1, keepdims=True) - yn * jnp.mean(wv * yn, axis=-1, keepdims=True))
            dcb = dcb + fold(dyc)
            dyc_ext[r0:r0 + CONV_CHUNK, :] = dyc
        dsm_acc[0:8, :] += dcb
        dsm_acc[8:16, :] += dlg
        dsm_acc[16:24, :] += dlb
        _shift_copies(dyc_ext, dyc_sh)

        for r0 in range(0, tc, CONV_CHUNK):
            dyc = dyc_ext[r0:r0 + CONV_CHUNK, :]
            dglu = jnp.zeros((CONV_CHUNK, CONV_CH), F32)
            base = r0 + CONV_HALO - (CONV_WIDTH - 1)
            for w in range(CONV_WIDTH):
                back = r0 + (CONV_WIDTH - 1) - w
                dglu = dglu + cw_ref[w:w + 1, :] * _window(dyc_ext, dyc_sh, back, CONV_CHUNK)
                dcw_acc[8 * w:8 * w + 8, :] += fold(dyc * _window(glu_ext, glu_sh, base + w, CONV_CHUNK))
            sg = sg_buf[r0:r0 + CONV_CHUNK, :]
            v = u_ref[r0:r0 + CONV_CHUNK, :CONV_CH]
            du_ref[r0:r0 + CONV_CHUNK, :CONV_CH] = (dglu * sg).astype(BF16)
            du_ref[r0:r0 + CONV_CHUNK, CONV_CH:] = (dglu * v * sg * (1.0 - sg)).astype(BF16)

        @pl.when(i == nt - 1)
        def _():
            for w in range(CONV_WIDTH):
                dcw_ref[w:w + 1, :] = jnp.sum(dcw_acc[8 * w:8 * w + 8, :], axis=0, keepdims=True)
            dcw_ref[CONV_WIDTH:, :] = jnp.zeros((CONV_HALO - CONV_WIDTH, CONV_CH), F32)
            for k in range(3):
                dsm_ref[k:k + 1, :] = jnp.sum(dsm_acc[8 * k:8 * k + 8, :], axis=0, keepdims=True)
            dsm_ref[3:, :] = jnp.zeros((5, CONV_CH), F32)

    return pl.pallas_call(
        body, name="conv_bwd", grid=(nt,),
        out_shape=[jax.ShapeDtypeStruct((S, 2 * CONV_CH), BF16), jax.ShapeDtypeStruct((CONV_HALO, CONV_CH), F32),
                   jax.ShapeDtypeStruct((8, CONV_CH), F32)],
        in_specs=[pl.BlockSpec((tc, 2 * CONV_CH), lambda i: (nt - 1 - i, 0)),
                  pl.BlockSpec((CONV_HALO, 2 * CONV_CH), lambda i: (jnp.maximum((nt - 1 - i) * per - 1, 0), 0)),
                  pl.BlockSpec((tc, CONV_CH), lambda i: (nt - 1 - i, 0)),
                  _const((CONV_HALO, CONV_CH)), _const((1, CONV_CH)), _const((1, CONV_CH)), _const((1, CONV_CH))],
        out_specs=[pl.BlockSpec((tc, 2 * CONV_CH), lambda i: (nt - 1 - i, 0)), _const((CONV_HALO, CONV_CH)),
                   _const((8, CONV_CH))],
        scratch_shapes=[pltpu.VMEM((tc + CONV_HALO, CONV_CH), F32), pltpu.VMEM((tc + CONV_HALO, CONV_CH), F32),
                        pltpu.VMEM((tc, CONV_CH), F32), pltpu.VMEM((8 * CONV_HALO, CONV_CH), F32),
                        pltpu.VMEM((24, CONV_CH), F32)]
        + [pltpu.VMEM((SUBLANES - 1, tc + CONV_HALO - SUBLANES, CONV_CH), F32)] * 2,
        compiler_params=_cparams(dimension_semantics=("arbitrary",)),
    )(u_conv, u_conv, dco, conv_w, conv_b, ln_g, ln_b)


def _in_proj_bwd(du_conv, dq, dk, dv, w_in, x, g, dh1):
    S = x.shape[0]
    tm = min(TOKEN_TILE, S)
    nconv = 2 * CONV_CH

    def body(duc_ref, dq_ref, dk_ref, dv_ref, w_ref, x_ref, g_ref, dh1_ref, dx_ref, dg_ref):
        @pl.when(pl.program_id(0) == 0)
        def _():
            dg_ref[...] = jnp.zeros_like(dg_ref)

        da = _dot_nt(duc_ref[...], w_ref[:, :nconv])
        for n, ref in enumerate((dq_ref, dk_ref, dv_ref)):
            c0 = nconv + n * SB_WIDTH
            da = da + _dot_nt(ref[...].astype(BF16), w_ref[:, c0:c0 + SB_WIDTH])
        xf = x_ref[...]
        r = _rms_r(xf)
        dx_ref[...] = dh1_ref[...] + _rms_bwd(xf, r, g_ref[...], da)
        dg_ref[...] += jnp.sum(da * xf * r, axis=0, keepdims=True)

    row = lambda n: pl.BlockSpec((tm, n), lambda i: (i, 0))
    return pl.pallas_call(
        body, name="in_proj_bwd", grid=(S // tm,),
        out_shape=[jax.ShapeDtypeStruct((S, D_MODEL), F32), jax.ShapeDtypeStruct((1, D_MODEL), F32)],
        in_specs=[row(nconv), row(SB_WIDTH), row(SB_WIDTH), row(SB_WIDTH), _resident(w_in.shape), row(D_MODEL),
                  _const((1, D_MODEL)), row(D_MODEL)],
        out_specs=[row(D_MODEL), _const((1, D_MODEL))],
        compiler_params=_cparams(dimension_semantics=("arbitrary",)),
    )(du_conv, dq, dk, dv, w_in, x, g, dh1)


def _layer_grads(xs, target, g_pre_mix, w_in_f, conv_w_f, conv_b, conv_ln_g, conv_ln_b, attn_g, g_post_mix, g_pre_ffn,
                 g_post_ffn, late_weights, send_grads):
    a, u_conv, qkv = _in_proj(xs, g_pre_mix, w_in_f)
    conv_out = _conv_fwd(u_conv, conv_w_f, conv_b, conv_ln_g, conv_ln_b)
    o, attn_out, cl = _attn_fwd(qkv, attn_g)
    w_out_f, w_gate_f, w_up_f, w_down_f = late_weights(attn_out)
    y, h1, f_in = _out_proj(conv_out, attn_out, w_out_f, xs, g_post_mix, g_pre_ffn)
    gt, up, df, dh2, loss_part, d_g_post_ffn = _ffn_fwd_loss(f_in, w_gate_f, w_up_f, w_down_f, h1, target, g_post_ffn)

    dgt, dup, act, dfin = _ffn_bwd(df, gt, up, w_gate_f, w_up_f, w_down_f)
    d_w_down = _matmul_tn("grad_w_down", act, df, 512)
    d_w_gate = _matmul_tn("grad_w_gate", f_in, dgt, FF_CHUNK)
    d_w_up = _matmul_tn("grad_w_up", f_in, dup, FF_CHUNK)
    sent = send_grads("ffn", (d_w_gate, d_w_up, d_w_down))
    dh1, dy, dco, dao, d_g_pre_ffn, d_g_post_mix = _mix_bwd(dfin, h1, y, dh2, g_pre_ffn + sent, g_post_mix, w_out_f)
    d_w_out = jnp.concatenate([_matmul_tn("grad_w_out_conv", conv_out, dy, D_MODEL),
                               _matmul_tn("grad_w_out_attn", attn_out, dy, D_MODEL)], axis=0)
    sent = send_grads("w_out", (d_w_out,))
    do, d_attn_g = _attn_norm_bwd(o, dao, attn_g + sent)
    dq, dk, dv = _attn_bwd(qkv, do, cl)
    du_conv, d_conv_w, d_conv_small = _conv_bwd(u_conv, dco, conv_w_f, conv_b, conv_ln_g, conv_ln_b)
    grad_x, d_g_pre_mix = _in_proj_bwd(du_conv, dq, dk, dv, w_in_f, xs, g_pre_mix, dh1)
    d_w_in = jnp.concatenate([_matmul_tn("grad_w_in_conv", a, du_conv, 2 * CONV_CH),
                              _matmul_tn("grad_w_in_q", a, dq, SB_WIDTH), _matmul_tn("grad_w_in_k", a, dk, SB_WIDTH),
                              _matmul_tn("grad_w_in_v", a, dv, SB_WIDTH)], axis=1)
    return (loss_part, grad_x, d_w_in, d_conv_w, d_conv_small, d_attn_g, d_g_pre_mix, d_g_post_mix, d_g_pre_ffn,
            d_g_post_ffn)


def _cols_to_blocks(w):
    K, N = w.shape
    return jnp.transpose(w.reshape(K, N_DEV, N // N_DEV), (1, 0, 2))


def _blocks_to_cols(blocks):
    n_dev, K, n = blocks.shape
    return jnp.transpose(blocks, (1, 0, 2)).reshape(K, n_dev * n)


def kernel(x, g_pre_mix, w_in, conv_w, conv_b, conv_ln_g, conv_ln_b, attn_norm_g, w_out, g_post_mix, g_pre_ffn, w_gate, w_up, w_down, g_post_ffn, loss_target, m_g_pre_mix, m_w_in, m_conv_w, m_conv_b, m_conv_ln_g, m_conv_ln_b, m_attn_norm_g, m_w_out, m_g_post_mix, m_g_pre_ffn, m_w_gate, m_w_up, m_w_down, m_g_post_ffn, v_g_pre_mix, v_w_in, v_conv_w, v_conv_b, v_conv_ln_g, v_conv_ln_b, v_attn_norm_g, v_w_out, v_g_post_mix, v_g_pre_ffn, v_w_gate, v_w_up, v_w_down, v_g_post_ffn):
    xs = x[0]
    target = loss_target[0]
    S = xs.shape[0]
    me = 4 * lax.axis_index("x") + 2 * lax.axis_index("y") + lax.axis_index("c")
    cw_shard = conv_w.reshape(CONV_WIDTH, CONV_CH // N_DEV)
    attn_g = attn_norm_g.reshape(1, SB_WIDTH)

    gathered = _all_gather([w_in[0].astype(BF16), cw_shard])
    w_in_f = _blocks_to_cols(gathered[0])
    conv_w_f = jnp.pad(_blocks_to_cols(gathered[1]), ((0, CONV_HALO - CONV_WIDTH), (0, 0)))
    gathered_zero = gathered[2][0:1, 0:1].astype(BF16)
    late = [w_out[0].astype(BF16) + gathered_zero, w_gate[0].astype(BF16), w_up[0].astype(BF16), w_down[0].astype(BF16)]
    late_started = _exchange_start("all_gather_late_start", late, scatter=False)

    def late_weights(after):
        lands = _exchange_wait("all_gather_late_wait", late_started, False, after)
        wo, wg, wu, wd = [lax.dynamic_update_index_in_dim(land, own, me, 0) for land, own in zip(lands, late)]
        return wo.reshape(D_MODEL, D_MODEL), _blocks_to_cols(wg), _blocks_to_cols(wu), wd.reshape(D_FF, D_MODEL)

    started = {}

    def send_grads(name, grads, payload=F32, after=None):
        blocks = [g.reshape(N_DEV, g.shape[0] // N_DEV, g.shape[1]) if g.shape[1] == D_MODEL else _cols_to_blocks(g)
                  for g in grads]
        payloads = [(b if after is None else b + after).astype(payload) for b in blocks]
        sent = _exchange_start("reduce_scatter_" + name + "_start", payloads, scatter=True)
        started[name] = (sent, blocks)
        return sent[-1][0:1, 0:1]

    (loss_part, grad_x, d_w_in, d_conv_w, d_conv_small, d_attn_g, d_g_pre_mix, d_g_post_mix, d_g_pre_ffn,
     d_g_post_ffn) = _layer_grads(
        xs, target, g_pre_mix + late_started[-1][0:1, 0:1], w_in_f, conv_w_f, conv_b, conv_ln_g, conv_ln_b, attn_g,
        g_post_mix, g_pre_ffn, g_post_ffn, late_weights, send_grads)

    def reduced(name, after, shards):
        st, blocks = started[name]
        lands = _exchange_wait("reduce_scatter_" + name + "_wait", st, True, after)
        return [_sum_adamw("adamw_" + wn, land, lax.dynamic_index_in_dim(blk, me, 0, keepdims=False), w[0], m[0], v[0])
                for land, blk, (wn, w, m, v) in zip(lands, blocks, shards)]

    two = lambda t: t.reshape(2, CONV_CH)
    small_g = jnp.concatenate([
        d_conv_w,
        d_conv_small[0:3],
        d_attn_g,
        two(d_g_pre_mix), two(d_g_post_mix), two(d_g_pre_ffn), two(d_g_post_ffn),
        jnp.broadcast_to(loss_part[0:1, 0:1], (1, CONV_CH)),
        jnp.zeros((3, CONV_CH), F32)], axis=0)
    small_g = _all_reduce_small(small_g)
    loss = small_g[44, 0]
    send_grads("w_in", (d_w_in,), payload=BF16, after=small_g[45:46, 0:1])
    g_conv_w = lax.dynamic_slice(small_g, (0, me * (CONV_CH // N_DEV)), (CONV_WIDTH, CONV_CH // N_DEV))
    pack = lambda cb, lg, lb, ag, g1, g2, g3, g4: jnp.concatenate(
        [cb, lg, lb, ag.reshape(1, SB_WIDTH), two(g1), two(g2), two(g3), two(g4), jnp.zeros((4, CONV_CH), F32)], axis=0)
    sm_g = small_g[CONV_HALO:]
    sm_delta, sm_m, sm_v = _adamw_small(
        "adamw_small",
        pack(conv_b, conv_ln_g, conv_ln_b, attn_norm_g, g_pre_mix, g_post_mix, g_pre_ffn, g_post_ffn), sm_g,
        pack(m_conv_b, m_conv_ln_g, m_conv_ln_b, m_attn_norm_g, m_g_pre_mix, m_g_post_mix, m_g_pre_ffn, m_g_post_ffn),
        pack(v_conv_b, v_conv_ln_g, v_conv_ln_b, v_attn_norm_g, v_g_pre_mix, v_g_post_mix, v_g_pre_ffn, v_g_post_ffn))
    cw_delta, cw_m, cw_v = _adamw_small("adamw_conv_w", cw_shard, g_conv_w,
                                        m_conv_w.reshape(cw_shard.shape), v_conv_w.reshape(cw_shard.shape))

    ffn = reduced("ffn", grad_x, [("w_gate", w_gate, m_w_gate, v_w_gate), ("w_up", w_up, m_w_up, v_w_up),
                                  ("w_down", w_down, m_w_down, v_w_down)])
    big = {"w_gate": ffn[0], "w_up": ffn[1], "w_down": ffn[2],
           "w_out": reduced("w_out", ffn[2][0], [("w_out", w_out, m_w_out, v_w_out)])[0]}
    big["w_in"] = reduced("w_in", big["w_out"][0], [("w_in", w_in, m_w_in, v_w_in)])[0]

    def unpack(t):
        return {"conv_b": t[0:1], "conv_ln_g": t[1:2], "conv_ln_b": t[2:3], "attn_norm_g": t[3:4].reshape(1, SB_HEADS, SB_HEAD_DIM),
                "g_pre_mix": t[4:6].reshape(1, D_MODEL), "g_post_mix": t[6:8].reshape(1, D_MODEL),
                "g_pre_ffn": t[8:10].reshape(1, D_MODEL), "g_post_ffn": t[10:12].reshape(1, D_MODEL)}

    names = ["g_pre_mix", "w_in", "conv_w", "conv_b", "conv_ln_g", "conv_ln_b", "attn_norm_g", "w_out", "g_post_mix",
             "g_pre_ffn", "w_gate", "w_up", "w_down", "g_post_ffn"]
    kinds = []
    for idx, small in enumerate((sm_g, sm_delta, sm_m, sm_v)):
        d = unpack(small)
        d["conv_w"] = (g_conv_w, cw_delta, cw_m, cw_v)[idx].reshape(1, CONV_WIDTH, 1, CONV_CH // N_DEV)
        for n in big:
            d[n] = big[n][idx][None]
        kinds.append([d[n] for n in names])

    return (loss, grad_x[None], *kinds[0], *kinds[1], *kinds[2], *kinds[3])
```

```python
import functools
import math

import jax
import jax.numpy as jnp
from jax import lax
from jax.experimental import pallas as pl
from jax.experimental.pallas import tpu as pltpu

F32 = jnp.float32
BF16 = jnp.bfloat16
MESH = pl.DeviceIdType.MESH

N_DEV = 8
D_MODEL = 1024
CONV_CH = 512
CONV_WIDTH = 31
SB_HEADS = 8
SB_HEAD_DIM = 64
SB_WIDTH = SB_HEADS * SB_HEAD_DIM
D_FF = 2816
EPS = 1e-6
LOG2E = 1.4426950408889634
MASKED = -1e30
ADAM_LR = 0.001
ADAM_B1 = 0.9
ADAM_B2 = 0.999
ADAM_EPS = 1e-08
ADAM_WD = 0.01
ADAM_STEP = 10

SUBLANES = 8
LANES = 128
VMEM_LIMIT = 56 * 1024 * 1024
TOKEN_TILE = 512
FFN_TILE = 256
ATTN_UNROLL = 4
ATTN_STRIP = 32
ATTN_BLOCK = 256
CONV_HALO = 32
CONV_CHUNK = 64
FF_CHUNK = D_FF // 2


def _cparams(**kw):
    return pltpu.CompilerParams(vmem_limit_bytes=VMEM_LIMIT, **kw)


def _resident(shape):
    return pl.BlockSpec(shape, lambda *_: (0,) * len(shape), pipeline_mode=pl.Buffered(1))


def _const(shape):
    return pl.BlockSpec(shape, lambda *_: (0,) * len(shape))


def _rms_r(xf):
    return lax.rsqrt(jnp.mean(xf * xf, axis=-1, keepdims=True) + EPS)


def _rms_bwd(xf, r, g, dout):
    w = dout * g
    return r * (w - xf * (r * r) * jnp.mean(w * xf, axis=-1, keepdims=True))


def _dot(a, b):
    return jnp.dot(a, b, preferred_element_type=F32)


def _dot_nt(a, b):
    return lax.dot_general(a, b, (((1,), (1,)), ((), ())), preferred_element_type=F32)


def _dot_tn(a, b):
    return lax.dot_general(a, b, (((0,), (0,)), ((), ())), preferred_element_type=F32)


def _peer(x, y, c, k):
    px = 1 - x if (k >> 2) & 1 else x
    py = 1 - y if (k >> 1) & 1 else y
    pc = 1 - c if k & 1 else c
    return (px, py, pc), 4 * px + 2 * py + pc


def _all_gather(shards):
    n = len(shards)

    def body(*refs):
        ins, outs, done = refs[:n], refs[n:2 * n], refs[2 * n]
        send_sems, recv_sems, local_sems = refs[2 * n + 1:]
        x, y, c = lax.axis_index("x"), lax.axis_index("y"), lax.axis_index("c")
        me, sibling = (x, y, c), (x, y, 1 - c)
        chips = [(1 - x, y), (x, 1 - y), (1 - x, 1 - y)]
        number = lambda d: 4 * d[0] + 2 * d[1] + d[2]

        def copy(a, k, block, to, src=None):
            rows = outs[a].at[number(block)]
            return pltpu.make_async_remote_copy(
                src_ref=rows if src is None else src, dst_ref=rows, send_sem=send_sems.at[a * (N_DEV - 1) + k],
                recv_sem=recv_sems.at[a * (N_DEV - 1) + k], device_id=to, device_id_type=MESH)

        copies = [pltpu.make_async_copy(ins[a], outs[a].at[number(me)], local_sems.at[a]) for a in range(n)]
        for mine in copies:
            mine.start()
        sent = [copy(a, 0, me, sibling, src=ins[a]) for a in range(n)]
        sent += [copy(a, 1 + j, me, (*chip, c), src=ins[a]) for j, chip in enumerate(chips) for a in range(n)]
        for cp in sent:
            cp.start()
        for j, chip in enumerate(chips):
            for a in range(n):
                copy(a, 1 + j, (*chip, c), me).wait_recv()
                passed = copy(a, 4 + j, (*chip, c), sibling)
                passed.start()
                sent.append(passed)
        for a in range(n):
            copy(a, 0, sibling, me).wait_recv()
            for j, chip in enumerate(chips):
                copy(a, 4 + j, (*chip, 1 - c), me).wait_recv()
        for cp in sent:
            cp.wait_send()
        for mine in copies:
            mine.wait()
        done[...] = jnp.zeros_like(done)

    any_spec = pl.BlockSpec(memory_space=pl.ANY)
    return pl.pallas_call(
        body, name="all_gather_weights",
        out_shape=[jax.ShapeDtypeStruct((N_DEV,) + s.shape, s.dtype) for s in shards] + [jax.ShapeDtypeStruct((8, LANES), F32)],
        in_specs=[any_spec] * n, out_specs=[any_spec] * n + [pl.BlockSpec(memory_space=pltpu.VMEM)],
        scratch_shapes=[pltpu.SemaphoreType.DMA((n * (N_DEV - 1),)), pltpu.SemaphoreType.DMA((n * (N_DEV - 1),)),
                        pltpu.SemaphoreType.DMA((n,))],
        compiler_params=pltpu.CompilerParams(has_side_effects=True),
    )(*shards)


def _adamw(w, g, m, v):
    m = ADAM_B1 * m + (1.0 - ADAM_B1) * g
    v = ADAM_B2 * v + (1.0 - ADAM_B2) * (g * g)
    m_hat = m / (1.0 - ADAM_B1 ** ADAM_STEP)
    v_hat = v / (1.0 - ADAM_B2 ** ADAM_STEP)
    delta = -ADAM_LR * (m_hat / (jnp.sqrt(v_hat) + ADAM_EPS) + ADAM_WD * w)
    return delta, m, v


def _exchange_and_sum(src_block, recv_ref, send_sems, recv_sems, local_sem):
    x, y, c = lax.axis_index("x"), lax.axis_index("y"), lax.axis_index("c")
    me = 4 * x + 2 * y + c
    mine = pltpu.make_async_copy(src_block(me), recv_ref.at[me], local_sem)
    mine.start()
    for k in range(1, N_DEV):
        peer, peer_block = _peer(x, y, c, k)
        pltpu.make_async_remote_copy(
            src_ref=src_block(peer_block), dst_ref=recv_ref.at[me], send_sem=send_sems.at[k - 1],
            recv_sem=recv_sems.at[k - 1], device_id=peer, device_id_type=MESH).start()
    for k in range(1, N_DEV):
        peer, peer_block = _peer(x, y, c, k)
        arrived = pltpu.make_async_remote_copy(
            src_ref=src_block(peer_block), dst_ref=recv_ref.at[peer_block], send_sem=send_sems.at[k - 1],
            recv_sem=recv_sems.at[k - 1], device_id=peer, device_id_type=MESH)
        arrived.wait_send()
        arrived.wait_recv()
    mine.wait()


HBM_SPEC = pl.BlockSpec(memory_space=pltpu.HBM)
SEM_SPEC = pl.BlockSpec(memory_space=pltpu.SEMAPHORE)
DATAFLOW = pltpu.SideEffectType.DATAFLOW_SIDE_EFFECTING


def _exchange_copies(srcs, lands, send_sems, recv_sems, scatter, wait):
    x, y, c = lax.axis_index("x"), lax.axis_index("y"), lax.axis_index("c")
    me = 4 * x + 2 * y + c
    for k in range(1, N_DEV):
        peer, peer_block = _peer(x, y, c, k)
        for a in range(len(srcs)):
            s = a * (N_DEV - 1) + k - 1
            src = srcs[a].at[peer_block] if scatter else srcs[a]
            copy = pltpu.make_async_remote_copy(
                src_ref=src, dst_ref=lands[a].at[peer_block if wait else me], send_sem=send_sems.at[s],
                recv_sem=recv_sems.at[s], device_id=peer, device_id_type=MESH)
            if wait:
                copy.wait_send()
                copy.wait_recv()
            else:
                copy.start()


def _exchange_start(name, arrays, scatter):
    n = len(arrays)
    land_shapes = [a.shape if scatter else (N_DEV,) + a.shape for a in arrays]

    def body(*refs):
        _exchange_copies(refs[:n], refs[n:2 * n], refs[2 * n], refs[2 * n + 1], scatter, wait=False)
        refs[-1][...] = jnp.zeros_like(refs[-1])

    sems = pltpu.SemaphoreType.DMA((n * (N_DEV - 1),))
    hbm = lambda t: pltpu.with_memory_space_constraint(t, pltpu.HBM)
    return pl.pallas_call(
        body, name=name,
        out_shape=(sems, sems, *[pltpu.HBM(a.shape, a.dtype) for a in arrays],
                   *[pltpu.HBM(ls, a.dtype) for ls, a in zip(land_shapes, arrays)], jax.ShapeDtypeStruct((8, LANES), F32)),
        in_specs=[HBM_SPEC] * (2 * n),
        out_specs=(SEM_SPEC, SEM_SPEC, *[HBM_SPEC] * (2 * n), pl.BlockSpec(memory_space=pltpu.VMEM)),
        input_output_aliases={a: 2 + a for a in range(2 * n)},
        compiler_params=pltpu.CompilerParams(has_side_effects=DATAFLOW),
    )(*[hbm(a) for a in arrays], *[hbm(lax.empty(ls, a.dtype)) for ls, a in zip(land_shapes, arrays)])


def _exchange_wait(name, started, scatter, after):
    n = (len(started) - 3) // 2
    send_sems, recv_sems = started[0], started[1]
    arrays, lands = started[2:2 + n], started[2 + n:2 + 2 * n]

    def body(*refs):
        _exchange_copies(refs[:n], refs[n:2 * n], refs[2 * n], refs[2 * n + 1], scatter, wait=True)

    return pl.pallas_call(
        body, name=name,
        out_shape=[pltpu.HBM(t.shape, t.dtype) for t in (*arrays, *lands)],
        in_specs=[HBM_SPEC] * (2 * n) + [SEM_SPEC, SEM_SPEC, pl.BlockSpec(memory_space=pl.ANY)],
        out_specs=[HBM_SPEC] * (2 * n),
        input_output_aliases={a: a for a in range(2 * n)},
        compiler_params=pltpu.CompilerParams(has_side_effects=DATAFLOW),
    )(*arrays, *lands, send_sems, recv_sems, after)[n:]


def _sum_adamw(name, land, own, w, m, v):
    _, M, N = land.shape
    rows = math.gcd(M, 128)

    def body(land_ref, own_ref, w_ref, m_ref, v_ref, grad_ref, delta_ref, nm_ref, nv_ref):
        x, y, c = lax.axis_index("x"), lax.axis_index("y"), lax.axis_index("c")
        g = own_ref[...]
        for k in range(1, N_DEV):
            g = g + land_ref[_peer(x, y, c, k)[1]].astype(F32)
        delta, nm, nv = _adamw(w_ref[...], g, m_ref[...], v_ref[...])
        grad_ref[...] = g
        delta_ref[...] = delta
        nm_ref[...] = nm
        nv_ref[...] = nv

    row = pl.BlockSpec((rows, N), lambda i: (i, 0))
    return pl.pallas_call(
        body, name=name, grid=(M // rows,), out_shape=[jax.ShapeDtypeStruct((M, N), F32)] * 4,
        in_specs=[pl.BlockSpec((N_DEV, rows, N), lambda i: (0, i, 0)), row, row, row, row], out_specs=[row] * 4,
        compiler_params=_cparams(),
    )(land, own, w, m, v)


def _all_reduce_small(g):
    R, C = g.shape

    def body(g_ref, out_ref, recv_ref, send_sems, recv_sems, local_sem):
        _exchange_and_sum(lambda b: g_ref, recv_ref, send_sems, recv_sems, local_sem)
        total = recv_ref[0]
        for b in range(1, N_DEV):
            total = total + recv_ref[b]
        out_ref[...] = total

    vmem = pl.BlockSpec(memory_space=pltpu.VMEM)
    return pl.pallas_call(
        body, name="all_reduce_small_grads", out_shape=jax.ShapeDtypeStruct((R, C), F32),
        in_specs=[vmem], out_specs=vmem,
        scratch_shapes=[pltpu.VMEM((N_DEV, R, C), F32), pltpu.SemaphoreType.DMA((N_DEV - 1,)),
                        pltpu.SemaphoreType.DMA((N_DEV - 1,)), pltpu.SemaphoreType.DMA(())],
        compiler_params=_cparams(has_side_effects=True),
    )(g)


def _adamw_small(name, w, g, m, v):
    def body(w_ref, g_ref, m_ref, v_ref, delta_ref, nm_ref, nv_ref):
        delta, nm, nv = _adamw(w_ref[...], g_ref[...], m_ref[...], v_ref[...])
        delta_ref[...] = delta
        nm_ref[...] = nm
        nv_ref[...] = nv

    vmem = pl.BlockSpec(memory_space=pltpu.VMEM)
    return pl.pallas_call(body, name=name, out_shape=[jax.ShapeDtypeStruct(w.shape, F32)] * 3,
                          in_specs=[vmem] * 4, out_specs=[vmem] * 3)(w, g, m, v)


def _in_proj(x, g, w_in):
    S = x.shape[0]
    tm = min(TOKEN_TILE, S)
    nconv = 2 * CONV_CH

    def body(x_ref, g_ref, w_ref, a_ref, uc_ref, qkv_ref):
        xf = x_ref[...]
        a = (xf * _rms_r(xf) * g_ref[...]).astype(BF16)
        a_ref[...] = a
        uc_ref[...] = _dot(a, w_ref[:, :nconv])
        qkv_ref[:, :SB_WIDTH] = (_dot(a, w_ref[:, nconv:nconv + SB_WIDTH]) * (1.0 / math.sqrt(SB_HEAD_DIM))).astype(BF16)
        qkv_ref[:, SB_WIDTH:] = _dot(a, w_ref[:, nconv + SB_WIDTH:]).astype(BF16)

    row = lambda n: pl.BlockSpec((tm, n), lambda i: (i, 0))
    return pl.pallas_call(
        body, name="in_proj", grid=(S // tm,),
        out_shape=[jax.ShapeDtypeStruct((S, D_MODEL), BF16), jax.ShapeDtypeStruct((S, nconv), F32),
                   jax.ShapeDtypeStruct((S, 3 * SB_WIDTH), BF16)],
        in_specs=[row(D_MODEL), _const((1, D_MODEL)), _resident(w_in.shape)],
        out_specs=[row(D_MODEL), row(nconv), row(3 * SB_WIDTH)],
        compiler_params=_cparams(),
    )(x, g, w_in)


def _glu(u):
    val, gate = u[:, :CONV_CH], u[:, CONV_CH:]
    sg = jax.nn.sigmoid(gate)
    return val, sg, val * sg


def _shift_copies(ext, shifted):
    n = shifted.shape[1]
    for r in range(1, SUBLANES):
        shifted[r - 1] = ext[r:r + n, :]


def _window(ext, shifted, start, rows):
    r = start % SUBLANES
    return ext[start:start + rows, :] if r == 0 else shifted[r - 1, start - r:start - r + rows, :]


def _conv_rows(glu_ext, glu_sh, cw_ref, r0, rows):
    base = r0 + CONV_HALO - (CONV_WIDTH - 1)
    acc = cw_ref[0:1, :] * _window(glu_ext, glu_sh, base, rows)
    for w in range(1, CONV_WIDTH):
        acc = acc + cw_ref[w:w + 1, :] * _window(glu_ext, glu_sh, base + w, rows)
    return acc


def _conv_fwd(u_conv, conv_w, conv_b, ln_g, ln_b):
    S = u_conv.shape[0]
    tc = min(TOKEN_TILE, S)

    def body(u_ref, cw_ref, cb_ref, lg_ref, lb_ref, out_ref, glu_ext, glu_sh):
        i = pl.program_id(0)

        @pl.when(i == 0)
        def _():
            glu_ext[0:CONV_HALO, :] = jnp.zeros((CONV_HALO, CONV_CH), F32)

        @pl.when(i > 0)
        def _():
            glu_ext[0:CONV_HALO, :] = glu_ext[tc:tc + CONV_HALO, :]

        glu_ext[CONV_HALO:, :] = _glu(u_ref[...])[2]
        _shift_copies(glu_ext, glu_sh)
        for r0 in range(0, tc, CONV_CHUNK):
            y = _conv_rows(glu_ext, glu_sh, cw_ref, r0, CONV_CHUNK) + cb_ref[...]
            mu = jnp.mean(y, axis=-1, keepdims=True)
            yc = y - mu
            yn = yc * lax.rsqrt(jnp.mean(yc * yc, axis=-1, keepdims=True) + EPS)
            yl = yn * lg_ref[...] + lb_ref[...]
            out_ref[r0:r0 + CONV_CHUNK, :] = (yl * jax.nn.sigmoid(yl)).astype(BF16)

    return pl.pallas_call(
        body, name="conv_fwd", grid=(S // tc,),
        out_shape=jax.ShapeDtypeStruct((S, CONV_CH), BF16),
        in_specs=[pl.BlockSpec((tc, 2 * CONV_CH), lambda i: (i, 0)), _const((CONV_HALO, CONV_CH)),
                  _const((1, CONV_CH)), _const((1, CONV_CH)), _const((1, CONV_CH))],
        out_specs=pl.BlockSpec((tc, CONV_CH), lambda i: (i, 0)),
        scratch_shapes=[pltpu.VMEM((tc + CONV_HALO, CONV_CH), F32),
                        pltpu.VMEM((SUBLANES - 1, tc + CONV_HALO - SUBLANES, CONV_CH), F32)],
        compiler_params=_cparams(dimension_semantics=("arbitrary",)),
    )(u_conv, conv_w, conv_b, ln_g, ln_b)


def _head_masks():
    lane = lax.broadcasted_iota(jnp.int32, (1, LANES), 1)
    return lane < SB_HEAD_DIM


def _split_heads(t, first):
    z = jnp.zeros_like(t)
    return jnp.where(first, t, z), jnp.where(first, z, t)


def _head_sum(t, first):
    a = jnp.sum(jnp.where(first, t, 0.0), axis=-1, keepdims=True)
    b = jnp.sum(jnp.where(first, 0.0, t), axis=-1, keepdims=True)
    return a, b


def _attn_fwd(qkv, g_attn):
    S = qkv.shape[0]
    Q = min(ATTN_BLOCK, S)
    nq = S // Q
    assert nq <= LANES
    ntiles = nq * (nq + 1) // 2
    assert ATTN_UNROLL % 2 == 0 and ntiles >= 4 + ATTN_UNROLL
    npair = SB_WIDTH // LANES
    tiles = [(i, j) for i in range(nq) for j in range(i, -1, -1)]

    def body(q_ref, k_ref, v_ref, g_ref, o_ref, ao_ref, cl_ref, z_buf, l_buf, z2_buf, a_buf, c_buf, mask_buf):
        first = _head_masks()
        lane = lax.broadcasted_iota(jnp.int32, (1, LANES), 1)
        row = lax.broadcasted_iota(jnp.int32, (Q, Q), 0)
        col = lax.broadcasted_iota(jnp.int32, (Q, Q), 1)
        tri = (row >= col).astype(BF16)
        heads = range(2)
        strips = [slice(r0, r0 + ATTN_STRIP) for r0 in range(0, Q, ATTN_STRIP)]
        rows = lambda j: pl.ds(pl.multiple_of(j * Q, Q), Q)
        wide = lambda t: jnp.tile(t, (1, Q // LANES))
        as_int = lambda t: int(t) if isinstance(t, (bool, int)) else t.astype(jnp.int32)

        keep = col < row
        mask_buf[0, 0] = jnp.ones((Q, Q), F32)
        mask_buf[0, 1] = jnp.zeros((Q, Q), F32)
        mask_buf[1, 0] = jnp.where(keep, 1.0, 0.0)
        mask_buf[1, 1] = jnp.where(keep, 0.0, MASKED)
        o_ref[...] = jnp.zeros_like(o_ref)

        def scores(t, slot):
            i, j = t
            qh = _split_heads(q_ref[rows(i), :], first)
            kb = k_ref[rows(j), :]
            for h in heads:
                z_buf[slot, h] = _dot_nt(qh[h], kb)

        def logs(t, slot):
            i, j = t
            diag = as_int(i == j)
            for h in heads:
                for r in strips:
                    z2 = z_buf[slot, h, r, :] * LOG2E
                    l = (jnp.minimum(z2, 0.0) - jnp.log2(1.0 + jnp.exp2(-jnp.abs(z2)))) - z2
                    l_buf[slot, h, r, :] = (l * mask_buf[diag, 0, r, :]).astype(BF16)
                    z2_buf[slot, h, r, :] = z2 + mask_buf[diag, 1, r, :]

        def sums(slot):
            return tuple(_dot(l_buf[slot, h], tri) for h in heads)

        def weights(t, slot, sm):
            i, j = t
            running = jnp.where(j == i, 0.0, 1.0)
            for h in heads:
                before = c_buf[h] * running
                for r in strips:
                    a_buf[slot, h, r, :] = jnp.exp2(z2_buf[slot, h, r, :] + sm[h][r] + wide(before[r])).astype(BF16)
                hl = slice(h * LANES, (h + 1) * LANES)
                cl_ref[rows(i), hl] = jnp.where(lane == j, before, cl_ref[rows(i), hl] * running)
                c_buf[h] = before + jnp.broadcast_to(sm[h][:, 0:1], (Q, LANES))

        def values(t, slot):
            i, j = t
            vh = _split_heads(v_ref[rows(j), :], first)
            o_ref[rows(i), :] += _dot(a_buf[slot, 0], vh[0]) + _dot(a_buf[slot, 1], vh[1])

        def iteration(t, p):
            ta, tb, tc, td = t
            if ta is not None:
                scores(ta, p)
            if tc is not None:
                sm = sums(p)
            if td is not None:
                values(td, 1 - p)
            if tb is not None:
                logs(tb, 1 - p)
            if tc is not None:
                weights(tc, p, sm)

        def window(n):
            return tuple(tiles[n - k] if 0 <= n - k < ntiles else None for k in range(4))

        def following(t):
            i, j = t
            last = j == 0
            return jnp.where(last, i + 1, i), jnp.where(last, i + 1, j - 1)

        peeled = 4 + (ntiles - 4) % ATTN_UNROLL

        def unrolled_iterations(_, t):
            for n in range(peeled, peeled + ATTN_UNROLL):
                iteration(t, n % 2)
                t = (following(t[0]),) + t[:3]
            return t

        c_buf[...] = jnp.zeros_like(c_buf)
        for n in range(peeled):
            iteration(window(n), n % 2)
        first_window = tuple((jnp.int32(i), jnp.int32(j)) for i, j in window(peeled))
        lax.fori_loop(0, (ntiles - peeled) // ATTN_UNROLL, unrolled_iterations, first_window)
        for n in range(ntiles, ntiles + 3):
            iteration(window(n), n % 2)

        def head_norm(b, carry):
            o = o_ref[rows(b), :]
            sa, sb = _head_sum(o * o, first)
            r = jnp.where(first, lax.rsqrt(sa * (1.0 / SB_HEAD_DIM) + EPS), lax.rsqrt(sb * (1.0 / SB_HEAD_DIM) + EPS))
            ao_ref[rows(b), :] = (o * r * g_ref[...]).astype(BF16)
            return carry

        lax.fori_loop(0, nq, head_norm, 0)

    col_block = lambda off: pl.BlockSpec((S, LANES), lambda p: (0, off + p), pipeline_mode=pl.Buffered(1))
    out_block = lambda n: pl.BlockSpec((S, n), lambda p: (0, p), pipeline_mode=pl.Buffered(1))
    return pl.pallas_call(
        body, name="attn_fwd", grid=(npair,),
        out_shape=[jax.ShapeDtypeStruct((S, SB_WIDTH), F32), jax.ShapeDtypeStruct((S, SB_WIDTH), BF16),
                   jax.ShapeDtypeStruct((S, 2 * SB_WIDTH), F32)],
        in_specs=[col_block(0), col_block(npair), col_block(2 * npair), pl.BlockSpec((1, LANES), lambda p: (0, p))],
        out_specs=[out_block(LANES), out_block(LANES), out_block(2 * LANES)],
        scratch_shapes=[pltpu.VMEM((2, 2, Q, Q), F32), pltpu.VMEM((2, 2, Q, Q), BF16), pltpu.VMEM((2, 2, Q, Q), F32),
                        pltpu.VMEM((2, 2, Q, Q), BF16), pltpu.VMEM((2, Q, LANES), F32), pltpu.VMEM((2, 2, Q, Q), F32)],
        compiler_params=_cparams(dimension_semantics=("arbitrary",)),
    )(qkv, qkv, qkv, g_attn)


def _out_proj(conv_out, attn_out, w_out, x, g_post_mix, g_pre_ffn):
    S = x.shape[0]
    tm = min(TOKEN_TILE, S)

    def body(co_ref, ao_ref, w_ref, x_ref, g1_ref, g2_ref, y_ref, h1_ref, fin_ref):
        y = _dot(co_ref[...], w_ref[:CONV_CH, :]) + _dot(ao_ref[...], w_ref[CONV_CH:, :])
        h1 = x_ref[...] + y * _rms_r(y) * g1_ref[...]
        y_ref[...] = y
        h1_ref[...] = h1
        fin_ref[...] = (h1 * _rms_r(h1) * g2_ref[...]).astype(BF16)

    row = lambda n: pl.BlockSpec((tm, n), lambda i: (i, 0))
    return pl.pallas_call(
        body, name="out_proj", grid=(S // tm,),
        out_shape=[jax.ShapeDtypeStruct((S, D_MODEL), F32), jax.ShapeDtypeStruct((S, D_MODEL), F32),
                   jax.ShapeDtypeStruct((S, D_MODEL), BF16)],
        in_specs=[row(CONV_CH), row(SB_WIDTH), _resident(w_out.shape), row(D_MODEL), _const((1, D_MODEL)),
                  _const((1, D_MODEL))],
        out_specs=[row(D_MODEL)] * 3,
        compiler_params=_cparams(),
    )(conv_out, attn_out, w_out, x, g_post_mix, g_pre_ffn)


def _ffn_fwd_loss(f_in, w_gate, w_up, w_down, h1, target, g_post_ffn):
    S = f_in.shape[0]
    tm = min(FFN_TILE, S)
    nt = S // tm

    def body(fin_ref, wg_ref, wu_ref, wd_ref, h1_ref, t_ref, g_ref, gt_ref, up_ref, df_ref, dh2_ref, loss_ref, dg_ref,
             sq_acc):
        i = pl.program_id(0)

        @pl.when(i == 0)
        def _():
            sq_acc[...] = jnp.zeros_like(sq_acc)
            dg_ref[...] = jnp.zeros_like(dg_ref)

        fin = fin_ref[...]
        f = jnp.zeros((tm, D_MODEL), F32)
        for c0 in range(0, D_FF, FF_CHUNK):
            cols = slice(c0, c0 + FF_CHUNK)
            gt = _dot(fin, wg_ref[:, cols])
            up = _dot(fin, wu_ref[:, cols])
            gt_ref[:, cols] = gt.astype(BF16)
            up_ref[:, cols] = up.astype(BF16)
            f = f + _dot((gt * jax.nn.sigmoid(gt) * up).astype(BF16), wd_ref[cols, :])
        r = _rms_r(f)
        g = g_ref[...]
        diff = h1_ref[...] + f * r * g - t_ref[...]
        sq_acc[...] += jnp.sum(diff * diff, axis=0, keepdims=True)
        dh2 = diff * (1.0 / D_MODEL)
        dh2_ref[...] = dh2
        dg_ref[...] += jnp.sum(dh2 * f * r, axis=0, keepdims=True)
        df_ref[...] = _rms_bwd(f, r, g, dh2).astype(BF16)

        @pl.when(i == nt - 1)
        def _():
            loss_ref[...] = jnp.broadcast_to((0.5 / D_MODEL) * jnp.sum(sq_acc[...], axis=-1, keepdims=True), (1, LANES))

    row = lambda n: pl.BlockSpec((tm, n), lambda i: (i, 0))
    return pl.pallas_call(
        body, name="ffn_fwd_loss", grid=(nt,),
        out_shape=[jax.ShapeDtypeStruct((S, D_FF), BF16), jax.ShapeDtypeStruct((S, D_FF), BF16),
                   jax.ShapeDtypeStruct((S, D_MODEL), BF16), jax.ShapeDtypeStruct((S, D_MODEL), F32),
                   jax.ShapeDtypeStruct((1, LANES), F32), jax.ShapeDtypeStruct((1, D_MODEL), F32)],
        in_specs=[row(D_MODEL), _resident(w_gate.shape), _resident(w_up.shape), _resident(w_down.shape),
                  row(D_MODEL), row(D_MODEL), _const((1, D_MODEL))],
        out_specs=[row(D_FF), row(D_FF), row(D_MODEL), row(D_MODEL), _const((1, LANES)), _const((1, D_MODEL))],
        scratch_shapes=[pltpu.VMEM((1, D_MODEL), F32)],
        compiler_params=_cparams(dimension_semantics=("arbitrary",)),
    )(f_in, w_gate, w_up, w_down, h1, target, g_post_ffn)


def _ffn_bwd(df, gt, up, w_gate, w_up, w_down):
    S = df.shape[0]
    tm = min(FFN_TILE, S)

    def body(df_ref, gt_ref, up_ref, wg_ref, wu_ref, wd_ref, dgt_ref, dup_ref, act_ref, dfin_ref):
        df = df_ref[...]
        dfin = jnp.zeros((tm, D_MODEL), F32)
        for c0 in range(0, D_FF, FF_CHUNK):
            cols = slice(c0, c0 + FF_CHUNK)
            dact = _dot_nt(df, wd_ref[cols, :])
            gt = gt_ref[:, cols].astype(F32)
            up = up_ref[:, cols].astype(F32)
            s = jax.nn.sigmoid(gt)
            silu = gt * s
            dgt = (dact * up * (s * (1.0 + gt * (1.0 - s)))).astype(BF16)
            dup = (dact * silu).astype(BF16)
            act_ref[:, cols] = (silu * up).astype(BF16)
            dgt_ref[:, cols] = dgt
            dup_ref[:, cols] = dup
            dfin = dfin + _dot_nt(dgt, wg_ref[:, cols]) + _dot_nt(dup, wu_ref[:, cols])
        dfin_ref[...] = dfin

    row = lambda n: pl.BlockSpec((tm, n), lambda i: (i, 0))
    return pl.pallas_call(
        body, name="ffn_bwd", grid=(S // tm,),
        out_shape=[jax.ShapeDtypeStruct((S, D_FF), BF16)] * 3 + [jax.ShapeDtypeStruct((S, D_MODEL), F32)],
        in_specs=[row(D_MODEL), row(D_FF), row(D_FF), _resident(w_gate.shape), _resident(w_up.shape),
                  _resident(w_down.shape)],
        out_specs=[row(D_FF)] * 3 + [row(D_MODEL)],
        compiler_params=_cparams(),
    )(df, gt, up, w_gate, w_up, w_down)


def _matmul_tn(name, x, y, tn):
    S, K = x.shape
    N = y.shape[1]
    ts = min(TOKEN_TILE, S)

    def body(x_ref, y_ref, o_ref):
        @pl.when(pl.program_id(1) == 0)
        def _():
            o_ref[...] = jnp.zeros_like(o_ref)

        o_ref[...] += _dot_tn(x_ref[...].astype(BF16), y_ref[...].astype(BF16))

    return pl.pallas_call(
        body, name=name, grid=(N // tn, S // ts),
        out_shape=jax.ShapeDtypeStruct((K, N), F32),
        in_specs=[pl.BlockSpec((ts, K), lambda n, s: (s, 0)), pl.BlockSpec((ts, tn), lambda n, s: (s, n))],
        out_specs=pl.BlockSpec((K, tn), lambda n, s: (0, n)),
        compiler_params=_cparams(dimension_semantics=("arbitrary", "arbitrary")),
    )(x, y)


def _mix_bwd(dfin, h1, y, dh2, g_pre_ffn, g_post_mix, w_out):
    S = dfin.shape[0]
    tm = min(TOKEN_TILE, S)

    def body(dfin_ref, h1_ref, y_ref, dh2_ref, g2_ref, g1_ref, w_ref, dh1_ref, dy_ref, dco_ref, dao_ref, dg2_ref, dg1_ref):
        @pl.when(pl.program_id(0) == 0)
        def _():
            dg2_ref[...] = jnp.zeros_like(dg2_ref)
            dg1_ref[...] = jnp.zeros_like(dg1_ref)

        h1, dfin = h1_ref[...], dfin_ref[...]
        r2 = _rms_r(h1)
        dh1 = dh2_ref[...] + _rms_bwd(h1, r2, g2_ref[...], dfin)
        dg2_ref[...] += jnp.sum(dfin * h1 * r2, axis=0, keepdims=True)
        y = y_ref[...]
        r1 = _rms_r(y)
        dy = _rms_bwd(y, r1, g1_ref[...], dh1).astype(BF16)
        dg1_ref[...] += jnp.sum(dh1 * y * r1, axis=0, keepdims=True)
        dh1_ref[...] = dh1
        dy_ref[...] = dy
        dco_ref[...] = _dot_nt(dy, w_ref[:CONV_CH, :])
        dao_ref[...] = _dot_nt(dy, w_ref[CONV_CH:, :])

    row = lambda n: pl.BlockSpec((tm, n), lambda i: (i, 0))
    return pl.pallas_call(
        body, name="mix_bwd", grid=(S // tm,),
        out_shape=[jax.ShapeDtypeStruct((S, D_MODEL), F32), jax.ShapeDtypeStruct((S, D_MODEL), BF16),
                   jax.ShapeDtypeStruct((S, CONV_CH), F32), jax.ShapeDtypeStruct((S, SB_WIDTH), F32),
                   jax.ShapeDtypeStruct((1, D_MODEL), F32), jax.ShapeDtypeStruct((1, D_MODEL), F32)],
        in_specs=[row(D_MODEL)] * 4 + [_const((1, D_MODEL)), _const((1, D_MODEL)), _resident(w_out.shape)],
        out_specs=[row(D_MODEL), row(D_MODEL), row(CONV_CH), row(SB_WIDTH), _const((1, D_MODEL)), _const((1, D_MODEL))],
        compiler_params=_cparams(dimension_semantics=("arbitrary",)),
    )(dfin, h1, y, dh2, g_pre_ffn, g_post_mix, w_out)


def _attn_norm_bwd(o, dao, g_attn):
    S = o.shape[0]
    tm = min(TOKEN_TILE, S)
    inv_dh = 1.0 / SB_HEAD_DIM

    def body(o_ref, dao_ref, g_ref, do_ref, dg_ref):
        @pl.when(pl.program_id(0) == 0)
        def _():
            dg_ref[...] = jnp.zeros_like(dg_ref)

        first = _head_masks()
        for p in range(SB_WIDTH // LANES):
            cols = slice(p * LANES, (p + 1) * LANES)
            o, dao, g = o_ref[:, cols], dao_ref[:, cols], g_ref[:, cols]
            sa, sb = _head_sum(o * o, first)
            r = jnp.where(first, lax.rsqrt(sa * inv_dh + EPS), lax.rsqrt(sb * inv_dh + EPS))
            w = dao * g
            wa, wb = _head_sum(w * o, first)
            do_ref[:, cols] = (r * (w - o * (r * r) * (jnp.where(first, wa, wb) * inv_dh))).astype(BF16)
            dg_ref[:, cols] += jnp.sum(dao * o * r, axis=0, keepdims=True)

    row = pl.BlockSpec((tm, SB_WIDTH), lambda i: (i, 0))
    return pl.pallas_call(
        body, name="attn_norm_bwd", grid=(S // tm,),
        out_shape=[jax.ShapeDtypeStruct((S, SB_WIDTH), BF16), jax.ShapeDtypeStruct((1, SB_WIDTH), F32)],
        in_specs=[row, row, _const((1, SB_WIDTH))], out_specs=[row, _const((1, SB_WIDTH))],
        compiler_params=_cparams(dimension_semantics=("arbitrary",)),
    )(o, dao, g_attn)


def _attn_bwd(qkv, do, cl):
    S = qkv.shape[0]
    Q = min(ATTN_BLOCK, S)
    nq = S // Q
    ntiles = nq * (nq + 1) // 2
    assert ATTN_UNROLL % 2 == 0 and ntiles >= 4 + ATTN_UNROLL
    npair = SB_WIDTH // LANES
    tiles = [(i, j) for i in range(nq) for j in range(i + 1)]

    def body(q_ref, k_ref, v_ref, do_ref, cl_ref, dq_ref, dk_ref, dv_ref,
             z_buf, lb_buf, be_buf, g_buf, l_buf, a_buf, gb_buf, dz_buf, pg_buf, mask_buf):
        first = _head_masks()
        lane = lax.broadcasted_iota(jnp.int32, (1, LANES), 1)
        row = lax.broadcasted_iota(jnp.int32, (Q, Q), 0)
        col = lax.broadcasted_iota(jnp.int32, (Q, Q), 1)
        tri = (row > col).astype(BF16)
        tpi = (row <= col).astype(BF16)
        heads = range(2)
        strips = [slice(r0, r0 + ATTN_STRIP) for r0 in range(0, Q, ATTN_STRIP)]
        rows = lambda j: pl.ds(pl.multiple_of(j * Q, Q), Q)
        wide = lambda t: jnp.tile(t, (1, Q // LANES))
        as_int = lambda t: int(t) if isinstance(t, (bool, int)) else t.astype(jnp.int32)

        keep = col < row
        mask_buf[0, 0] = jnp.ones((Q, Q), F32)
        mask_buf[0, 1] = jnp.zeros((Q, Q), F32)
        mask_buf[1, 0] = jnp.where(keep, 1.0, 0.0)
        mask_buf[1, 1] = jnp.where(keep, 0.0, MASKED)
        dq_ref[...] = jnp.zeros_like(dq_ref)
        dk_ref[...] = jnp.zeros_like(dk_ref)
        dv_ref[...] = jnp.zeros_like(dv_ref)

        def scores(t, slot):
            i, j = t
            qh = _split_heads(q_ref[rows(i), :], first)
            kb = k_ref[rows(j), :]
            for h in heads:
                z_buf[slot, h] = _dot_nt(qh[h], kb)

        def logs(t, slot):
            i, j = t
            diag = as_int(i == j)
            for h in heads:
                for r in strips:
                    z2 = z_buf[slot, h, r, :] * LOG2E
                    lb = jnp.minimum(z2, 0.0) - jnp.log2(1.0 + jnp.exp2(-jnp.abs(z2)))
                    l = (lb - z2) * mask_buf[diag, 0, r, :]
                    l_buf[slot, h, r, :] = l.astype(BF16)
                    lb_buf[slot, h, r, :] = lb + mask_buf[diag, 1, r, :]

        def sums(t, slot):
            i, j = t
            doh = _split_heads(do_ref[rows(i), :], first)
            vb = v_ref[rows(j), :]
            return (tuple(_dot(l_buf[slot, h], tri) for h in heads), tuple(_dot_nt(doh[h], vb) for h in heads))

        def weights(t, slot, sm, da):
            i, j = t
            for h in heads:
                c = jnp.sum(jnp.where(lane == j, cl_ref[rows(i), h * LANES:(h + 1) * LANES], 0.0), axis=-1, keepdims=True)
                c = jnp.broadcast_to(c, (Q, LANES))
                for r in strips:
                    lb = lb_buf[slot, h, r, :]
                    a = jnp.exp2(lb + sm[h][r] + wide(c[r]))
                    g = da[h][r] * a
                    a_buf[slot, h, r, :] = a.astype(BF16)
                    be_buf[slot, h, r, :] = jnp.exp2(lb)
                    g_buf[slot, h, r, :] = g
                    gb_buf[slot, h, r, :] = g.astype(BF16)

        def prefix(t, slot):
            i, j = t
            doh = _split_heads(do_ref[rows(i), :], first)
            dv_ref[rows(j), :] += _dot_tn(a_buf[slot, 0], doh[0]) + _dot_tn(a_buf[slot, 1], doh[1])
            return tuple(_dot(gb_buf[slot, h], tpi) for h in heads)

        def dscores(t, slot, pm):
            i, j = t
            for h in heads:
                pg = pg_buf[h] * jnp.where(j == 0, 0.0, 1.0)
                for r in strips:
                    dz = g_buf[slot, h, r, :] - be_buf[slot, h, r, :] * (pm[h][r] + wide(pg[r]))
                    dz_buf[slot, h, r, :] = dz.astype(BF16)
                pg_buf[h] = pg + jnp.broadcast_to(pm[h][:, Q - 1:Q], (Q, LANES))

        def grads(t, slot):
            i, j = t
            qh = _split_heads(q_ref[rows(i), :], first)
            kh = _split_heads(k_ref[rows(j), :], first)
            dq_ref[rows(i), :] += _dot(dz_buf[slot, 0], kh[0]) + _dot(dz_buf[slot, 1], kh[1])
            dk_ref[rows(j), :] += _dot_tn(dz_buf[slot, 0], qh[0]) + _dot_tn(dz_buf[slot, 1], qh[1])

        def iteration(t, p):
            ta, tb, tc, td, te = t
            if ta is not None:
                scores(ta, p)
            if tc is not None:
                sm, da = sums(tc, p)
            if td is not None:
                pm = prefix(td, 1 - p)
            if te is not None:
                grads(te, p)
            if tb is not None:
                logs(tb, 1 - p)
            if tc is not None:
                weights(tc, p, sm, da)
            if td is not None:
                dscores(td, 1 - p, pm)

        def window(n):
            return tuple(tiles[n - k] if 0 <= n - k < ntiles else None for k in range(5))

        def following(t):
            i, j = t
            last = j == i
            return jnp.where(last, i + 1, i), jnp.where(last, 0, j + 1)

        peeled = 4 + (ntiles - 4) % ATTN_UNROLL

        def unrolled_iterations(_, t):
            for n in range(peeled, peeled + ATTN_UNROLL):
                iteration(t, n % 2)
                t = (following(t[0]),) + t[:4]
            return t

        pg_buf[...] = jnp.zeros_like(pg_buf)
        for n in range(peeled):
            iteration(window(n), n % 2)
        first_window = tuple((jnp.int32(i), jnp.int32(j)) for i, j in window(peeled))
        lax.fori_loop(0, (ntiles - peeled) // ATTN_UNROLL, unrolled_iterations, first_window)
        for n in range(ntiles, ntiles + 4):
            iteration(window(n), n % 2)
        dq_ref[...] = dq_ref[...] * (1.0 / math.sqrt(SB_HEAD_DIM))

    col_block = lambda off: pl.BlockSpec((S, LANES), lambda p: (0, off + p), pipeline_mode=pl.Buffered(1))
    return pl.pallas_call(
        body, name="attn_bwd", grid=(npair,),
        out_shape=[jax.ShapeDtypeStruct((S, SB_WIDTH), F32)] * 3,
        in_specs=[col_block(0), col_block(npair), col_block(2 * npair), col_block(0),
                  pl.BlockSpec((S, 2 * LANES), lambda p: (0, p), pipeline_mode=pl.Buffered(1))],
        out_specs=[pl.BlockSpec((S, LANES), lambda p: (0, p), pipeline_mode=pl.Buffered(1))] * 3,
        scratch_shapes=[pltpu.VMEM((2, 2, Q, Q), F32)] * 4 + [pltpu.VMEM((2, 2, Q, Q), BF16)] * 4
        + [pltpu.VMEM((2, Q, LANES), F32), pltpu.VMEM((2, 2, Q, Q), F32)],
        compiler_params=_cparams(dimension_semantics=("arbitrary",)),
    )(qkv, qkv, qkv, do, cl)


def _conv_bwd(u_conv, dco, conv_w, conv_b, ln_g, ln_b):
    S = u_conv.shape[0]
    tc = min(TOKEN_TILE, S)
    nt = S // tc
    per = tc // CONV_HALO
    groups = CONV_CHUNK // 8

    def body(u_ref, halo_ref, dco_ref, cw_ref, cb_ref, lg_ref, lb_ref, du_ref, dcw_ref, dsm_ref, glu_ext, dyc_ext, sg_buf,
             dcw_acc, dsm_acc, glu_sh, dyc_sh):
        i = pl.program_id(0)
        ti = nt - 1 - i

        @pl.when(i == 0)
        def _():
            dyc_ext[tc:, :] = jnp.zeros((CONV_HALO, CONV_CH), F32)
            dcw_acc[...] = jnp.zeros_like(dcw_acc)
            dsm_acc[...] = jnp.zeros_like(dsm_acc)

        @pl.when(i > 0)
        def _():
            dyc_ext[tc:, :] = dyc_ext[0:CONV_HALO, :]

        glu_ext[0:CONV_HALO, :] = jnp.where(ti > 0, _glu(halo_ref[...])[2], 0.0)
        val, sg, glu = _glu(u_ref[...])
        glu_ext[CONV_HALO:, :] = glu
        sg_buf[...] = sg
        _shift_copies(glu_ext, glu_sh)

        dcb = jnp.zeros((8, CONV_CH), F32)
        dlg = jnp.zeros((8, CONV_CH), F32)
        dlb = jnp.zeros((8, CONV_CH), F32)
        fold = lambda t: jnp.sum(t.reshape(groups, 8, CONV_CH), axis=0)
        for r0 in range(0, tc, CONV_CHUNK):
            y = _conv_rows(glu_ext, glu_sh, cw_ref, r0, CONV_CHUNK) + cb_ref[...]
            mu = jnp.mean(y, axis=-1, keepdims=True)
            yc = y - mu
            rstd = lax.rsqrt(jnp.mean(yc * yc, axis=-1, keepdims=True) + EPS)
            yn = yc * rstd
            yl = yn * lg_ref[...] + lb_ref[...]
            s = jax.nn.sigmoid(yl)
            dyl = dco_ref[r0:r0 + CONV_CHUNK, :] * (s * (1.0 + yl * (1.0 - s)))
            dlg = dlg + fold(dyl * yn)
            dlb = dlb + fold(dyl)
            wv = dyl * lg_ref[...]
            dyc = rstd * (wv - jnp.mean(wv, axis=-1, keepdims=True) - yn * jnp.mean(wv * yn, axis=-1, keepdims=True))
            dcb = dcb + fold(dyc)
            dyc_ext[r0:r0 + CONV_CHUNK, :] = dyc
        dsm_acc[0:8, :] += dcb
        dsm_acc[8:16, :] += dlg
        dsm_acc[16:24, :] += dlb
        _shift_copies(dyc_ext, dyc_sh)

        for r0 in range(0, tc, CONV_CHUNK):
            dyc = dyc_ext[r0:r0 + CONV_CHUNK, :]
            dglu = jnp.zeros((CONV_CHUNK, CONV_CH), F32)
            base = r0 + CONV_HALO - (CONV_WIDTH - 1)
            for w in range(CONV_WIDTH):
                back = r0 + (CONV_WIDTH - 1) - w
                dglu = dglu + cw_ref[w:w + 1, :] * _window(dyc_ext, dyc_sh, back, CONV_CHUNK)
                dcw_acc[8 * w:8 * w + 8, :] += fold(dyc * _window(glu_ext, glu_sh, base + w, CONV_CHUNK))
            sg = sg_buf[r0:r0 + CONV_CHUNK, :]
            v = u_ref[r0:r0 + CONV_CHUNK, :CONV_CH]
            du_ref[r0:r0 + CONV_CHUNK, :CONV_CH] = (dglu * sg).astype(BF16)
            du_ref[r0:r0 + CONV_CHUNK, CONV_CH:] = (dglu * v * sg * (1.0 - sg)).astype(BF16)

        @pl.when(i == nt - 1)
        def _():
            for w in range(CONV_WIDTH):
                dcw_ref[w:w + 1, :] = jnp.sum(dcw_acc[8 * w:8 * w + 8, :], axis=0, keepdims=True)
            dcw_ref[CONV_WIDTH:, :] = jnp.zeros((CONV_HALO - CONV_WIDTH, CONV_CH), F32)
            for k in range(3):
                dsm_ref[k:k + 1, :] = jnp.sum(dsm_acc[8 * k:8 * k + 8, :], axis=0, keepdims=True)
            dsm_ref[3:, :] = jnp.zeros((5, CONV_CH), F32)

    return pl.pallas_call(
        body, name="conv_bwd", grid=(nt,),
        out_shape=[jax.ShapeDtypeStruct((S, 2 * CONV_CH), BF16), jax.ShapeDtypeStruct((CONV_HALO, CONV_CH), F32),
                   jax.ShapeDtypeStruct((8, CONV_CH), F32)],
        in_specs=[pl.BlockSpec((tc, 2 * CONV_CH), lambda i: (nt - 1 - i, 0)),
                  pl.BlockSpec((CONV_HALO, 2 * CONV_CH), lambda i: (jnp.maximum((nt - 1 - i) * per - 1, 0), 0)),
                  pl.BlockSpec((tc, CONV_CH), lambda i: (nt - 1 - i, 0)),
                  _const((CONV_HALO, CONV_CH)), _const((1, CONV_CH)), _const((1, CONV_CH)), _const((1, CONV_CH))],
        out_specs=[pl.BlockSpec((tc, 2 * CONV_CH), lambda i: (nt - 1 - i, 0)), _const((CONV_HALO, CONV_CH)),
                   _const((8, CONV_CH))],
        scratch_shapes=[pltpu.VMEM((tc + CONV_HALO, CONV_CH), F32), pltpu.VMEM((tc + CONV_HALO, CONV_CH), F32),
                        pltpu.VMEM((tc, CONV_CH), F32), pltpu.VMEM((8 * CONV_HALO, CONV_CH), F32),
                        pltpu.VMEM((24, CONV_CH), F32)]
        + [pltpu.VMEM((SUBLANES - 1, tc + CONV_HALO - SUBLANES, CONV_CH), F32)] * 2,
        compiler_params=_cparams(dimension_semantics=("arbitrary",)),
    )(u_conv, u_conv, dco, conv_w, conv_b, ln_g, ln_b)


def _in_proj_bwd(du_conv, dq, dk, dv, w_in, x, g, dh1):
    S = x.shape[0]
    tm = min(TOKEN_TILE, S)
    nconv = 2 * CONV_CH

    def body(duc_ref, dq_ref, dk_ref, dv_ref, w_ref, x_ref, g_ref, dh1_ref, dx_ref, dg_ref):
        @pl.when(pl.program_id(0) == 0)
        def _():
            dg_ref[...] = jnp.zeros_like(dg_ref)

        da = _dot_nt(duc_ref[...], w_ref[:, :nconv])
        for n, ref in enumerate((dq_ref, dk_ref, dv_ref)):
            c0 = nconv + n * SB_WIDTH
            da = da + _dot_nt(ref[...].astype(BF16), w_ref[:, c0:c0 + SB_WIDTH])
        xf = x_ref[...]
        r = _rms_r(xf)
        dx_ref[...] = dh1_ref[...] + _rms_bwd(xf, r, g_ref[...], da)
        dg_ref[...] += jnp.sum(da * xf * r, axis=0, keepdims=True)

    row = lambda n: pl.BlockSpec((tm, n), lambda i: (i, 0))
    return pl.pallas_call(
        body, name="in_proj_bwd", grid=(S // tm,),
        out_shape=[jax.ShapeDtypeStruct((S, D_MODEL), F32), jax.ShapeDtypeStruct((1, D_MODEL), F32)],
        in_specs=[row(nconv), row(SB_WIDTH), row(SB_WIDTH), row(SB_WIDTH), _resident(w_in.shape), row(D_MODEL),
                  _const((1, D_MODEL)), row(D_MODEL)],
        out_specs=[row(D_MODEL), _const((1, D_MODEL))],
        compiler_params=_cparams(dimension_semantics=("arbitrary",)),
    )(du_conv, dq, dk, dv, w_in, x, g, dh1)


def _layer_grads(xs, target, g_pre_mix, w_in_f, conv_w_f, conv_b, conv_ln_g, conv_ln_b, attn_g, g_post_mix, g_pre_ffn,
                 g_post_ffn, late_weights, send_grads):
    a, u_conv, qkv = _in_proj(xs, g_pre_mix, w_in_f)
    conv_out = _conv_fwd(u_conv, conv_w_f, conv_b, conv_ln_g, conv_ln_b)
    o, attn_out, cl = _attn_fwd(qkv, attn_g)
    w_out_f, w_gate_f, w_up_f, w_down_f = late_weights(attn_out)
    y, h1, f_in = _out_proj(conv_out, attn_out, w_out_f, xs, g_post_mix, g_pre_ffn)
    gt, up, df, dh2, loss_part, d_g_post_ffn = _ffn_fwd_loss(f_in, w_gate_f, w_up_f, w_down_f, h1, target, g_post_ffn)

    dgt, dup, act, dfin = _ffn_bwd(df, gt, up, w_gate_f, w_up_f, w_down_f)
    d_w_down = _matmul_tn("grad_w_down", act, df, 512)
    d_w_gate = _matmul_tn("grad_w_gate", f_in, dgt, FF_CHUNK)
    d_w_up = _matmul_tn("grad_w_up", f_in, dup, FF_CHUNK)
    sent = send_grads("ffn", (d_w_gate, d_w_up, d_w_down))
    dh1, dy, dco, dao, d_g_pre_ffn, d_g_post_mix = _mix_bwd(dfin, h1, y, dh2, g_pre_ffn + sent, g_post_mix, w_out_f)
    d_w_out = jnp.concatenate([_matmul_tn("grad_w_out_conv", conv_out, dy, D_MODEL),
                               _matmul_tn("grad_w_out_attn", attn_out, dy, D_MODEL)], axis=0)
    sent = send_grads("w_out", (d_w_out,))
    do, d_attn_g = _attn_norm_bwd(o, dao, attn_g + sent)
    dq, dk, dv = _attn_bwd(qkv, do, cl)
    du_conv, d_conv_w, d_conv_small = _conv_bwd(u_conv, dco, conv_w_f, conv_b, conv_ln_g, conv_ln_b)
    d_w_in = jnp.concatenate([_matmul_tn("grad_w_in_conv", a, du_conv, 2 * CONV_CH),
                              _matmul_tn("grad_w_in_q", a, dq, SB_WIDTH), _matmul_tn("grad_w_in_k", a, dk, SB_WIDTH),
                              _matmul_tn("grad_w_in_v", a, dv, SB_WIDTH)], axis=1)
    sent = send_grads("w_in", (d_w_in,))
    grad_x, d_g_pre_mix = _in_proj_bwd(du_conv, dq, dk, dv, w_in_f, xs, g_pre_mix + sent, dh1)
    return (loss_part, grad_x, d_conv_w, d_conv_small, d_attn_g, d_g_pre_mix, d_g_post_mix, d_g_pre_ffn, d_g_post_ffn)


def _cols_to_blocks(w):
    K, N = w.shape
    return jnp.transpose(w.reshape(K, N_DEV, N // N_DEV), (1, 0, 2))


def _blocks_to_cols(blocks):
    n_dev, K, n = blocks.shape
    return jnp.transpose(blocks, (1, 0, 2)).reshape(K, n_dev * n)


def kernel(x, g_pre_mix, w_in, conv_w, conv_b, conv_ln_g, conv_ln_b, attn_norm_g, w_out, g_post_mix, g_pre_ffn, w_gate, w_up, w_down, g_post_ffn, loss_target, m_g_pre_mix, m_w_in, m_conv_w, m_conv_b, m_conv_ln_g, m_conv_ln_b, m_attn_norm_g, m_w_out, m_g_post_mix, m_g_pre_ffn, m_w_gate, m_w_up, m_w_down, m_g_post_ffn, v_g_pre_mix, v_w_in, v_conv_w, v_conv_b, v_conv_ln_g, v_conv_ln_b, v_attn_norm_g, v_w_out, v_g_post_mix, v_g_pre_ffn, v_w_gate, v_w_up, v_w_down, v_g_post_ffn):
    xs = x[0]
    target = loss_target[0]
    S = xs.shape[0]
    me = 4 * lax.axis_index("x") + 2 * lax.axis_index("y") + lax.axis_index("c")
    cw_shard = conv_w.reshape(CONV_WIDTH, CONV_CH // N_DEV)
    attn_g = attn_norm_g.reshape(1, SB_WIDTH)

    gathered = _all_gather([w_in[0].astype(BF16), cw_shard])
    w_in_f = _blocks_to_cols(gathered[0])
    conv_w_f = jnp.pad(_blocks_to_cols(gathered[1]), ((0, CONV_HALO - CONV_WIDTH), (0, 0)))
    gathered_zero = gathered[2][0:1, 0:1].astype(BF16)
    late = [w_out[0].astype(BF16) + gathered_zero, w_gate[0].astype(BF16), w_up[0].astype(BF16), w_down[0].astype(BF16)]
    late_started = _exchange_start("all_gather_late_start", late, scatter=False)

    def late_weights(after):
        lands = _exchange_wait("all_gather_late_wait", late_started, False, after)
        wo, wg, wu, wd = [lax.dynamic_update_index_in_dim(land, own, me, 0) for land, own in zip(lands, late)]
        return wo.reshape(D_MODEL, D_MODEL), _blocks_to_cols(wg), _blocks_to_cols(wu), wd.reshape(D_FF, D_MODEL)

    started = {}

    def send_grads(name, grads):
        blocks = [g.reshape(N_DEV, g.shape[0] // N_DEV, g.shape[1]) if g.shape[1] == D_MODEL else _cols_to_blocks(g)
                  for g in grads]
        payload = BF16 if name == "w_in" else F32
        sent = _exchange_start("reduce_scatter_" + name + "_start", [b.astype(payload) for b in blocks], scatter=True)
        started[name] = (sent, blocks)
        return sent[-1][0:1, 0:1]

    (loss_part, grad_x, d_conv_w, d_conv_small, d_attn_g, d_g_pre_mix, d_g_post_mix, d_g_pre_ffn,
     d_g_post_ffn) = _layer_grads(
        xs, target, g_pre_mix + late_started[-1][0:1, 0:1], w_in_f, conv_w_f, conv_b, conv_ln_g, conv_ln_b, attn_g,
        g_post_mix, g_pre_ffn, g_post_ffn, late_weights, send_grads)

    def reduced(name, after, shards):
        st, blocks = started[name]
        lands = _exchange_wait("reduce_scatter_" + name + "_wait", st, True, after)
        return [_sum_adamw("adamw_" + wn, land, lax.dynamic_index_in_dim(blk, me, 0, keepdims=False), w[0], m[0], v[0])
                for land, blk, (wn, w, m, v) in zip(lands, blocks, shards)]

    two = lambda t: t.reshape(2, CONV_CH)
    small_g = jnp.concatenate([
        d_conv_w,
        d_conv_small[0:3],
        d_attn_g,
        two(d_g_pre_mix), two(d_g_post_mix), two(d_g_pre_ffn), two(d_g_post_ffn),
        jnp.broadcast_to(loss_part[0:1, 0:1], (1, CONV_CH)),
        jnp.zeros((3, CONV_CH), F32)], axis=0)
    small_g = _all_reduce_small(small_g)
    loss = small_g[44, 0]
    g_conv_w = lax.dynamic_slice(small_g, (0, me * (CONV_CH // N_DEV)), (CONV_WIDTH, CONV_CH // N_DEV))
    pack = lambda cb, lg, lb, ag, g1, g2, g3, g4: jnp.concatenate(
        [cb, lg, lb, ag.reshape(1, SB_WIDTH), two(g1), two(g2), two(g3), two(g4), jnp.zeros((4, CONV_CH), F32)], axis=0)
    sm_g = small_g[CONV_HALO:]
    sm_delta, sm_m, sm_v = _adamw_small(
        "adamw_small",
        pack(conv_b, conv_ln_g, conv_ln_b, attn_norm_g, g_pre_mix, g_post_mix, g_pre_ffn, g_post_ffn), sm_g,
        pack(m_conv_b, m_conv_ln_g, m_conv_ln_b, m_attn_norm_g, m_g_pre_mix, m_g_post_mix, m_g_pre_ffn, m_g_post_ffn),
        pack(v_conv_b, v_conv_ln_g, v_conv_ln_b, v_attn_norm_g, v_g_pre_mix, v_g_post_mix, v_g_pre_ffn, v_g_post_ffn))
    cw_delta, cw_m, cw_v = _adamw_small("adamw_conv_w", cw_shard, g_conv_w,
                                        m_conv_w.reshape(cw_shard.shape), v_conv_w.reshape(cw_shard.shape))

    ffn = reduced("ffn", grad_x, [("w_gate", w_gate, m_w_gate, v_w_gate), ("w_up", w_up, m_w_up, v_w_up),
                                  ("w_down", w_down, m_w_down, v_w_down)])
    big = {"w_gate": ffn[0], "w_up": ffn[1], "w_down": ffn[2],
           "w_out": reduced("w_out", ffn[2][0], [("w_out", w_out, m_w_out, v_w_out)])[0]}
    big["w_in"] = reduced("w_in", big["w_out"][0], [("w_in", w_in, m_w_in, v_w_in)])[0]

    def unpack(t):
        return {"conv_b": t[0:1], "conv_ln_g": t[1:2], "conv_ln_b": t[2:3], "attn_norm_g": t[3:4].reshape(1, SB_HEADS, SB_HEAD_DIM),
                "g_pre_mix": t[4:6].reshape(1, D_MODEL), "g_post_mix": t[6:8].reshape(1, D_MODEL),
                "g_pre_ffn": t[8:10].reshape(1, D_MODEL), "g_post_ffn": t[10:12].reshape(1, D_MODEL)}

    names = ["g_pre_mix", "w_in", "conv_w", "conv_b", "conv_ln_g", "conv_ln_b", "attn_norm_g", "w_out", "g_post_mix",
             "g_pre_ffn", "w_gate", "w_up", "w_down", "g_post_ffn"]
    kinds = []
    for idx, small in enumerate((sm_g, sm_delta, sm_m, sm_v)):
        d = unpack(small)
        d["conv_w"] = (g_conv_w, cw_delta, cw_m, cw_v)[idx].reshape(1, CONV_WIDTH, 1, CONV_CH // N_DEV)
        for n in big:
            d[n] = big[n][idx][None]
        kinds.append([d[n] for n in names])

    return (loss, grad_x[None], *kinds[0], *kinds[1], *kinds[2], *kinds[3])
```

```python
import functools
import math

import jax
import jax.numpy as jnp
from jax import lax
from jax.experimental import pallas as pl
from jax.experimental.pallas import tpu as pltpu

F32 = jnp.float32
BF16 = jnp.bfloat16
MESH = pl.DeviceIdType.MESH

N_DEV = 8
D_MODEL = 1024
CONV_CH = 512
CONV_WIDTH = 31
SB_HEADS = 8
SB_HEAD_DIM = 64
SB_WIDTH = SB_HEADS * SB_HEAD_DIM
D_FF = 2816
EPS = 1e-6
LOG2E = 1.4426950408889634
MASKED = -1e30
ADAM_LR = 0.001
ADAM_B1 = 0.9
ADAM_B2 = 0.999
ADAM_EPS = 1e-08
ADAM_WD = 0.01
ADAM_STEP = 10

SUBLANES = 8
LANES = 128
VMEM_LIMIT = 56 * 1024 * 1024
TOKEN_TILE = 512
GRAD_TILE = 1024
FFN_TILE = 256
ATTN_UNROLL = 4
ATTN_STRIP = 32
ATTN_BLOCK = 256
CONV_HALO = 32
CONV_CHUNK = 64
FF_CHUNK = D_FF // 2


def _cparams(**kw):
    return pltpu.CompilerParams(vmem_limit_bytes=VMEM_LIMIT, **kw)


def _resident(shape):
    return pl.BlockSpec(shape, lambda *_: (0,) * len(shape), pipeline_mode=pl.Buffered(1))


def _const(shape):
    return pl.BlockSpec(shape, lambda *_: (0,) * len(shape))


def _rms_r(xf):
    return lax.rsqrt(jnp.mean(xf * xf, axis=-1, keepdims=True) + EPS)


def _rms_bwd(xf, r, g, dout):
    w = dout * g
    return r * (w - xf * (r * r) * jnp.mean(w * xf, axis=-1, keepdims=True))


def _dot(a, b):
    return jnp.dot(a, b, preferred_element_type=F32)


def _dot_nt(a, b):
    return lax.dot_general(a, b, (((1,), (1,)), ((), ())), preferred_element_type=F32)


def _dot_tn(a, b):
    return lax.dot_general(a, b, (((0,), (0,)), ((), ())), preferred_element_type=F32)


def _peer(x, y, c, k):
    px = 1 - x if (k >> 2) & 1 else x
    py = 1 - y if (k >> 1) & 1 else y
    pc = 1 - c if k & 1 else c
    return (px, py, pc), 4 * px + 2 * py + pc


def _all_gather(shards):
    n = len(shards)

    def body(*refs):
        ins, outs, done = refs[:n], refs[n:2 * n], refs[2 * n]
        send_sems, recv_sems, local_sems = refs[2 * n + 1:]
        x, y, c = lax.axis_index("x"), lax.axis_index("y"), lax.axis_index("c")
        me, sibling = (x, y, c), (x, y, 1 - c)
        chips = [(1 - x, y), (x, 1 - y), (1 - x, 1 - y)]
        number = lambda d: 4 * d[0] + 2 * d[1] + d[2]

        def copy(a, k, block, to, src=None):
            rows = outs[a].at[number(block)]
            return pltpu.make_async_remote_copy(
                src_ref=rows if src is None else src, dst_ref=rows, send_sem=send_sems.at[a * (N_DEV - 1) + k],
                recv_sem=recv_sems.at[a * (N_DEV - 1) + k], device_id=to, device_id_type=MESH)

        copies = [pltpu.make_async_copy(ins[a], outs[a].at[number(me)], local_sems.at[a]) for a in range(n)]
        for mine in copies:
            mine.start()
        sent = [copy(a, 0, me, sibling, src=ins[a]) for a in range(n)]
        sent += [copy(a, 1 + j, me, (*chip, c), src=ins[a]) for j, chip in enumerate(chips) for a in range(n)]
        for cp in sent:
            cp.start()
        for j, chip in enumerate(chips):
            for a in range(n):
                copy(a, 1 + j, (*chip, c), me).wait_recv()
                passed = copy(a, 4 + j, (*chip, c), sibling)
                passed.start()
                sent.append(passed)
        for a in range(n):
            copy(a, 0, sibling, me).wait_recv()
            for j, chip in enumerate(chips):
                copy(a, 4 + j, (*chip, 1 - c), me).wait_recv()
        for cp in sent:
            cp.wait_send()
        for mine in copies:
            mine.wait()
        done[...] = jnp.zeros_like(done)

    any_spec = pl.BlockSpec(memory_space=pl.ANY)
    return pl.pallas_call(
        body, name="all_gather_weights",
        out_shape=[jax.ShapeDtypeStruct((N_DEV,) + s.shape, s.dtype) for s in shards] + [jax.ShapeDtypeStruct((8, LANES), F32)],
        in_specs=[any_spec] * n, out_specs=[any_spec] * n + [pl.BlockSpec(memory_space=pltpu.VMEM)],
        scratch_shapes=[pltpu.SemaphoreType.DMA((n * (N_DEV - 1),)), pltpu.SemaphoreType.DMA((n * (N_DEV - 1),)),
                        pltpu.SemaphoreType.DMA((n,))],
        compiler_params=pltpu.CompilerParams(has_side_effects=True),
    )(*shards)


def _adamw(w, g, m, v):
    m = ADAM_B1 * m + (1.0 - ADAM_B1) * g
    v = ADAM_B2 * v + (1.0 - ADAM_B2) * (g * g)
    m_hat = m / (1.0 - ADAM_B1 ** ADAM_STEP)
    v_hat = v / (1.0 - ADAM_B2 ** ADAM_STEP)
    delta = -ADAM_LR * (m_hat / (jnp.sqrt(v_hat) + ADAM_EPS) + ADAM_WD * w)
    return delta, m, v


def _exchange_and_sum(src_block, recv_ref, send_sems, recv_sems, local_sem):
    x, y, c = lax.axis_index("x"), lax.axis_index("y"), lax.axis_index("c")
    me = 4 * x + 2 * y + c
    mine = pltpu.make_async_copy(src_block(me), recv_ref.at[me], local_sem)
    mine.start()
    for k in range(1, N_DEV):
        peer, peer_block = _peer(x, y, c, k)
        pltpu.make_async_remote_copy(
            src_ref=src_block(peer_block), dst_ref=recv_ref.at[me], send_sem=send_sems.at[k - 1],
            recv_sem=recv_sems.at[k - 1], device_id=peer, device_id_type=MESH).start()
    for k in range(1, N_DEV):
        peer, peer_block = _peer(x, y, c, k)
        arrived = pltpu.make_async_remote_copy(
            src_ref=src_block(peer_block), dst_ref=recv_ref.at[peer_block], send_sem=send_sems.at[k - 1],
            recv_sem=recv_sems.at[k - 1], device_id=peer, device_id_type=MESH)
        arrived.wait_send()
        arrived.wait_recv()
    mine.wait()


HBM_SPEC = pl.BlockSpec(memory_space=pltpu.HBM)
SEM_SPEC = pl.BlockSpec(memory_space=pltpu.SEMAPHORE)
DATAFLOW = pltpu.SideEffectType.DATAFLOW_SIDE_EFFECTING


def _exchange_copies(srcs, lands, send_sems, recv_sems, scatter, wait):
    x, y, c = lax.axis_index("x"), lax.axis_index("y"), lax.axis_index("c")
    me = 4 * x + 2 * y + c
    for k in range(1, N_DEV):
        peer, peer_block = _peer(x, y, c, k)
        for a in range(len(srcs)):
            s = a * (N_DEV - 1) + k - 1
            src = srcs[a].at[peer_block] if scatter else srcs[a]
            copy = pltpu.make_async_remote_copy(
                src_ref=src, dst_ref=lands[a].at[peer_block if wait else me], send_sem=send_sems.at[s],
                recv_sem=recv_sems.at[s], device_id=peer, device_id_type=MESH)
            if wait:
                copy.wait_send()
                copy.wait_recv()
            else:
                copy.start()


def _exchange_start(name, arrays, scatter):
    n = len(arrays)
    land_shapes = [a.shape if scatter else (N_DEV,) + a.shape for a in arrays]

    def body(*refs):
        _exchange_copies(refs[:n], refs[n:2 * n], refs[2 * n], refs[2 * n + 1], scatter, wait=False)
        refs[-1][...] = jnp.zeros_like(refs[-1])

    sems = pltpu.SemaphoreType.DMA((n * (N_DEV - 1),))
    hbm = lambda t: pltpu.with_memory_space_constraint(t, pltpu.HBM)
    return pl.pallas_call(
        body, name=name,
        out_shape=(sems, sems, *[pltpu.HBM(a.shape, a.dtype) for a in arrays],
                   *[pltpu.HBM(ls, a.dtype) for ls, a in zip(land_shapes, arrays)], jax.ShapeDtypeStruct((8, LANES), F32)),
        in_specs=[HBM_SPEC] * (2 * n),
        out_specs=(SEM_SPEC, SEM_SPEC, *[HBM_SPEC] * (2 * n), pl.BlockSpec(memory_space=pltpu.VMEM)),
        input_output_aliases={a: 2 + a for a in range(2 * n)},
        compiler_params=pltpu.CompilerParams(has_side_effects=DATAFLOW),
    )(*[hbm(a) for a in arrays], *[hbm(lax.empty(ls, a.dtype)) for ls, a in zip(land_shapes, arrays)])


def _exchange_wait(name, started, scatter, after):
    n = (len(started) - 3) // 2
    send_sems, recv_sems = started[0], started[1]
    arrays, lands = started[2:2 + n], started[2 + n:2 + 2 * n]

    def body(*refs):
        _exchange_copies(refs[:n], refs[n:2 * n], refs[2 * n], refs[2 * n + 1], scatter, wait=True)

    return pl.pallas_call(
        body, name=name,
        out_shape=[pltpu.HBM(t.shape, t.dtype) for t in (*arrays, *lands)],
        in_specs=[HBM_SPEC] * (2 * n) + [SEM_SPEC, SEM_SPEC, pl.BlockSpec(memory_space=pl.ANY)],
        out_specs=[HBM_SPEC] * (2 * n),
        input_output_aliases={a: a for a in range(2 * n)},
        compiler_params=pltpu.CompilerParams(has_side_effects=DATAFLOW),
    )(*arrays, *lands, send_sems, recv_sems, after)[n:]


def _sum_adamw(name, land, own, w, m, v):
    _, M, N = land.shape
    rows = math.gcd(M, 128)

    def body(land_ref, own_ref, w_ref, m_ref, v_ref, grad_ref, delta_ref, nm_ref, nv_ref):
        x, y, c = lax.axis_index("x"), lax.axis_index("y"), lax.axis_index("c")
        g = own_ref[...]
        for k in range(1, N_DEV):
            g = g + land_ref[_peer(x, y, c, k)[1]].astype(F32)
        delta, nm, nv = _adamw(w_ref[...], g, m_ref[...], v_ref[...])
        grad_ref[...] = g
        delta_ref[...] = delta
        nm_ref[...] = nm
        nv_ref[...] = nv

    row = pl.BlockSpec((rows, N), lambda i: (i, 0))
    return pl.pallas_call(
        body, name=name, grid=(M // rows,), out_shape=[jax.ShapeDtypeStruct((M, N), F32)] * 4,
        in_specs=[pl.BlockSpec((N_DEV, rows, N), lambda i: (0, i, 0)), row, row, row, row], out_specs=[row] * 4,
        compiler_params=_cparams(),
    )(land, own, w, m, v)


def _all_reduce_small(g):
    R, C = g.shape

    def body(g_ref, out_ref, recv_ref, send_sems, recv_sems, local_sem):
        _exchange_and_sum(lambda b: g_ref, recv_ref, send_sems, recv_sems, local_sem)
        total = recv_ref[0]
        for b in range(1, N_DEV):
            total = total + recv_ref[b]
        out_ref[...] = total

    vmem = pl.BlockSpec(memory_space=pltpu.VMEM)
    return pl.pallas_call(
        body, name="all_reduce_small_grads", out_shape=jax.ShapeDtypeStruct((R, C), F32),
        in_specs=[vmem], out_specs=vmem,
        scratch_shapes=[pltpu.VMEM((N_DEV, R, C), F32), pltpu.SemaphoreType.DMA((N_DEV - 1,)),
                        pltpu.SemaphoreType.DMA((N_DEV - 1,)), pltpu.SemaphoreType.DMA(())],
        compiler_params=_cparams(has_side_effects=True),
    )(g)


def _adamw_small(name, w, g, m, v):
    def body(w_ref, g_ref, m_ref, v_ref, delta_ref, nm_ref, nv_ref):
        delta, nm, nv = _adamw(w_ref[...], g_ref[...], m_ref[...], v_ref[...])
        delta_ref[...] = delta
        nm_ref[...] = nm
        nv_ref[...] = nv

    vmem = pl.BlockSpec(memory_space=pltpu.VMEM)
    return pl.pallas_call(body, name=name, out_shape=[jax.ShapeDtypeStruct(w.shape, F32)] * 3,
                          in_specs=[vmem] * 4, out_specs=[vmem] * 3)(w, g, m, v)


def _in_proj(x, g, w_in):
    S = x.shape[0]
    tm = min(TOKEN_TILE, S)
    nconv = 2 * CONV_CH

    def body(x_ref, g_ref, w_ref, a_ref, uc_ref, qkv_ref):
        xf = x_ref[...]
        a = (xf * _rms_r(xf) * g_ref[...]).astype(BF16)
        a_ref[...] = a
        uc_ref[...] = _dot(a, w_ref[:, :nconv])
        qkv_ref[:, :SB_WIDTH] = (_dot(a, w_ref[:, nconv:nconv + SB_WIDTH]) * (1.0 / math.sqrt(SB_HEAD_DIM))).astype(BF16)
        qkv_ref[:, SB_WIDTH:] = _dot(a, w_ref[:, nconv + SB_WIDTH:]).astype(BF16)

    row = lambda n: pl.BlockSpec((tm, n), lambda i: (i, 0))
    return pl.pallas_call(
        body, name="in_proj", grid=(S // tm,),
        out_shape=[jax.ShapeDtypeStruct((S, D_MODEL), BF16), jax.ShapeDtypeStruct((S, nconv), F32),
                   jax.ShapeDtypeStruct((S, 3 * SB_WIDTH), BF16)],
        in_specs=[row(D_MODEL), _const((1, D_MODEL)), _resident(w_in.shape)],
        out_specs=[row(D_MODEL), row(nconv), row(3 * SB_WIDTH)],
        compiler_params=_cparams(),
    )(x, g, w_in)


def _glu(u):
    val, gate = u[:, :CONV_CH], u[:, CONV_CH:]
    sg = jax.nn.sigmoid(gate)
    return val, sg, val * sg


def _shift_copies(ext, shifted):
    n = shifted.shape[1]
    for r in range(1, SUBLANES):
        shifted[r - 1] = ext[r:r + n, :]


def _window(ext, shifted, start, rows):
    r = start % SUBLANES
    return ext[start:start + rows, :] if r == 0 else shifted[r - 1, start - r:start - r + rows, :]


def _conv_rows(glu_ext, glu_sh, cw_ref, r0, rows):
    base = r0 + CONV_HALO - (CONV_WIDTH - 1)
    acc = cw_ref[0:1, :] * _window(glu_ext, glu_sh, base, rows)
    for w in range(1, CONV_WIDTH):
        acc = acc + cw_ref[w:w + 1, :] * _window(glu_ext, glu_sh, base + w, rows)
    return acc


def _conv_fwd(u_conv, conv_w, conv_b, ln_g, ln_b):
    S = u_conv.shape[0]
    tc = min(TOKEN_TILE, S)

    def body(u_ref, cw_ref, cb_ref, lg_ref, lb_ref, out_ref, y_ref, glu_ext, glu_sh):
        i = pl.program_id(0)

        @pl.when(i == 0)
        def _():
            glu_ext[0:CONV_HALO, :] = jnp.zeros((CONV_HALO, CONV_CH), F32)

        @pl.when(i > 0)
        def _():
            glu_ext[0:CONV_HALO, :] = glu_ext[tc:tc + CONV_HALO, :]

        glu_ext[CONV_HALO:, :] = _glu(u_ref[...])[2]
        _shift_copies(glu_ext, glu_sh)
        for r0 in range(0, tc, CONV_CHUNK):
            y = _conv_rows(glu_ext, glu_sh, cw_ref, r0, CONV_CHUNK) + cb_ref[...]
            y_ref[r0:r0 + CONV_CHUNK, :] = y
            mu = jnp.mean(y, axis=-1, keepdims=True)
            yc = y - mu
            yn = yc * lax.rsqrt(jnp.mean(yc * yc, axis=-1, keepdims=True) + EPS)
            yl = yn * lg_ref[...] + lb_ref[...]
            out_ref[r0:r0 + CONV_CHUNK, :] = (yl * jax.nn.sigmoid(yl)).astype(BF16)

    return pl.pallas_call(
        body, name="conv_fwd", grid=(S // tc,),
        out_shape=[jax.ShapeDtypeStruct((S, CONV_CH), BF16), jax.ShapeDtypeStruct((S, CONV_CH), F32)],
        in_specs=[pl.BlockSpec((tc, 2 * CONV_CH), lambda i: (i, 0)), _const((CONV_HALO, CONV_CH)),
                  _const((1, CONV_CH)), _const((1, CONV_CH)), _const((1, CONV_CH))],
        out_specs=[pl.BlockSpec((tc, CONV_CH), lambda i: (i, 0))] * 2,
        scratch_shapes=[pltpu.VMEM((tc + CONV_HALO, CONV_CH), F32),
                        pltpu.VMEM((SUBLANES - 1, tc + CONV_HALO - SUBLANES, CONV_CH), F32)],
        compiler_params=_cparams(dimension_semantics=("arbitrary",)),
    )(u_conv, conv_w, conv_b, ln_g, ln_b)


def _head_masks():
    lane = lax.broadcasted_iota(jnp.int32, (1, LANES), 1)
    return lane < SB_HEAD_DIM


def _split_heads(t, first):
    z = jnp.zeros_like(t)
    return jnp.where(first, t, z), jnp.where(first, z, t)


def _head_sum(t, first):
    a = jnp.sum(jnp.where(first, t, 0.0), axis=-1, keepdims=True)
    b = jnp.sum(jnp.where(first, 0.0, t), axis=-1, keepdims=True)
    return a, b


def _attn_fwd(qkv, g_attn):
    S = qkv.shape[0]
    Q = min(ATTN_BLOCK, S)
    nq = S // Q
    assert nq <= LANES
    ntiles = nq * (nq + 1) // 2
    assert ATTN_UNROLL % 2 == 0 and ntiles >= 4 + ATTN_UNROLL
    npair = SB_WIDTH // LANES
    tiles = [(i, j) for i in range(nq) for j in range(i, -1, -1)]

    def body(q_ref, k_ref, v_ref, g_ref, o_ref, ao_ref, cl_ref, z_buf, l_buf, z2_buf, a_buf, c_buf, mask_buf):
        first = _head_masks()
        lane = lax.broadcasted_iota(jnp.int32, (1, LANES), 1)
        row = lax.broadcasted_iota(jnp.int32, (Q, Q), 0)
        col = lax.broadcasted_iota(jnp.int32, (Q, Q), 1)
        tri = (row >= col).astype(BF16)
        heads = range(2)
        strips = [slice(r0, r0 + ATTN_STRIP) for r0 in range(0, Q, ATTN_STRIP)]
        rows = lambda j: pl.ds(pl.multiple_of(j * Q, Q), Q)
        wide = lambda t: jnp.tile(t, (1, Q // LANES))
        as_int = lambda t: int(t) if isinstance(t, (bool, int)) else t.astype(jnp.int32)

        keep = col < row
        mask_buf[0, 0] = jnp.ones((Q, Q), F32)
        mask_buf[0, 1] = jnp.zeros((Q, Q), F32)
        mask_buf[1, 0] = jnp.where(keep, 1.0, 0.0)
        mask_buf[1, 1] = jnp.where(keep, 0.0, MASKED)
        o_ref[...] = jnp.zeros_like(o_ref)

        def scores(t, slot):
            i, j = t
            qh = _split_heads(q_ref[rows(i), :], first)
            kb = k_ref[rows(j), :]
            for h in heads:
                z_buf[slot, h] = _dot_nt(qh[h], kb)

        def logs(t, slot):
            i, j = t
            diag = as_int(i == j)
            for h in heads:
                for r in strips:
                    z2 = z_buf[slot, h, r, :] * LOG2E
                    l = (jnp.minimum(z2, 0.0) - jnp.log2(1.0 + jnp.exp2(-jnp.abs(z2)))) - z2
                    l_buf[slot, h, r, :] = (l * mask_buf[diag, 0, r, :]).astype(BF16)
                    z2_buf[slot, h, r, :] = z2 + mask_buf[diag, 1, r, :]

        def sums(slot):
            return tuple(_dot(l_buf[slot, h], tri) for h in heads)

        def weights(t, slot, sm):
            i, j = t
            running = jnp.where(j == i, 0.0, 1.0)
            for h in heads:
                before = c_buf[h] * running
                for r in strips:
                    a_buf[slot, h, r, :] = jnp.exp2(z2_buf[slot, h, r, :] + sm[h][r] + wide(before[r])).astype(BF16)
                hl = slice(h * LANES, (h + 1) * LANES)
                cl_ref[rows(i), hl] = jnp.where(lane == j, before, cl_ref[rows(i), hl] * running)
                c_buf[h] = before + jnp.broadcast_to(sm[h][:, 0:1], (Q, LANES))

        def values(t, slot):
            i, j = t
            vh = _split_heads(v_ref[rows(j), :], first)
            o_ref[rows(i), :] += _dot(a_buf[slot, 0], vh[0]) + _dot(a_buf[slot, 1], vh[1])

        def iteration(t, p):
            ta, tb, tc, td = t
            if ta is not None:
                scores(ta, p)
            if tc is not None:
                sm = sums(p)
            if td is not None:
                values(td, 1 - p)
            if tb is not None:
                logs(tb, 1 - p)
            if tc is not None:
                weights(tc, p, sm)

        def window(n):
            return tuple(tiles[n - k] if 0 <= n - k < ntiles else None for k in range(4))

        def following(t):
            i, j = t
            last = j == 0
            return jnp.where(last, i + 1, i), jnp.where(last, i + 1, j - 1)

        peeled = 4 + (ntiles - 4) % ATTN_UNROLL

        def unrolled_iterations(_, t):
            for n in range(peeled, peeled + ATTN_UNROLL):
                iteration(t, n % 2)
                t = (following(t[0]),) + t[:3]
            return t

        c_buf[...] = jnp.zeros_like(c_buf)
        for n in range(peeled):
            iteration(window(n), n % 2)
        first_window = tuple((jnp.int32(i), jnp.int32(j)) for i, j in window(peeled))
        lax.fori_loop(0, (ntiles - peeled) // ATTN_UNROLL, unrolled_iterations, first_window)
        for n in range(ntiles, ntiles + 3):
            iteration(window(n), n % 2)

        def head_norm(b, carry):
            o = o_ref[rows(b), :]
            sa, sb = _head_sum(o * o, first)
            r = jnp.where(first, lax.rsqrt(sa * (1.0 / SB_HEAD_DIM) + EPS), lax.rsqrt(sb * (1.0 / SB_HEAD_DIM) + EPS))
            ao_ref[rows(b), :] = (o * r * g_ref[...]).astype(BF16)
            return carry

        lax.fori_loop(0, nq, head_norm, 0)

    col_block = lambda off: pl.BlockSpec((S, LANES), lambda p: (0, off + p), pipeline_mode=pl.Buffered(1))
    out_block = lambda n: pl.BlockSpec((S, n), lambda p: (0, p), pipeline_mode=pl.Buffered(1))
    return pl.pallas_call(
        body, name="attn_fwd", grid=(npair,),
        out_shape=[jax.ShapeDtypeStruct((S, SB_WIDTH), F32), jax.ShapeDtypeStruct((S, SB_WIDTH), BF16),
                   jax.ShapeDtypeStruct((S, 2 * SB_WIDTH), F32)],
        in_specs=[col_block(0), col_block(npair), col_block(2 * npair), pl.BlockSpec((1, LANES), lambda p: (0, p))],
        out_specs=[out_block(LANES), out_block(LANES), out_block(2 * LANES)],
        scratch_shapes=[pltpu.VMEM((2, 2, Q, Q), F32), pltpu.VMEM((2, 2, Q, Q), BF16), pltpu.VMEM((2, 2, Q, Q), F32),
                        pltpu.VMEM((2, 2, Q, Q), BF16), pltpu.VMEM((2, Q, LANES), F32), pltpu.VMEM((2, 2, Q, Q), F32)],
        compiler_params=_cparams(dimension_semantics=("arbitrary",)),
    )(qkv, qkv, qkv, g_attn)


def _out_proj(conv_out, attn_out, w_out, x, g_post_mix, g_pre_ffn):
    S = x.shape[0]
    tm = min(TOKEN_TILE, S)

    def body(co_ref, ao_ref, w_ref, x_ref, g1_ref, g2_ref, y_ref, h1_ref, fin_ref):
        y = _dot(co_ref[...], w_ref[:CONV_CH, :]) + _dot(ao_ref[...], w_ref[CONV_CH:, :])
        h1 = x_ref[...] + y * _rms_r(y) * g1_ref[...]
        y_ref[...] = y
        h1_ref[...] = h1
        fin_ref[...] = (h1 * _rms_r(h1) * g2_ref[...]).astype(BF16)

    row = lambda n: pl.BlockSpec((tm, n), lambda i: (i, 0))
    return pl.pallas_call(
        body, name="out_proj", grid=(S // tm,),
        out_shape=[jax.ShapeDtypeStruct((S, D_MODEL), F32), jax.ShapeDtypeStruct((S, D_MODEL), F32),
                   jax.ShapeDtypeStruct((S, D_MODEL), BF16)],
        in_specs=[row(CONV_CH), row(SB_WIDTH), _resident(w_out.shape), row(D_MODEL), _const((1, D_MODEL)),
                  _const((1, D_MODEL))],
        out_specs=[row(D_MODEL)] * 3,
        compiler_params=_cparams(),
    )(conv_out, attn_out, w_out, x, g_post_mix, g_pre_ffn)


def _ffn_fwd_loss(f_in, w_gate, w_up, w_down, h1, target, g_post_ffn):
    S = f_in.shape[0]
    tm = min(FFN_TILE, S)
    nt = S // tm

    def body(fin_ref, wg_ref, wu_ref, wd_ref, h1_ref, t_ref, g_ref, gt_ref, up_ref, df_ref, dh2_ref, loss_ref, dg_ref,
             sq_acc):
        i = pl.program_id(0)

        @pl.when(i == 0)
        def _():
            sq_acc[...] = jnp.zeros_like(sq_acc)
            dg_ref[...] = jnp.zeros_like(dg_ref)

        fin = fin_ref[...]
        f = jnp.zeros((tm, D_MODEL), F32)
        for c0 in range(0, D_FF, FF_CHUNK):
            cols = slice(c0, c0 + FF_CHUNK)
            gt = _dot(fin, wg_ref[:, cols])
            up = _dot(fin, wu_ref[:, cols])
            gt_ref[:, cols] = gt.astype(BF16)
            up_ref[:, cols] = up.astype(BF16)
            f = f + _dot((gt * jax.nn.sigmoid(gt) * up).astype(BF16), wd_ref[cols, :])
        r = _rms_r(f)
        g = g_ref[...]
        diff = h1_ref[...] + f * r * g - t_ref[...]
        sq_acc[...] += jnp.sum(diff * diff, axis=0, keepdims=True)
        dh2 = diff * (1.0 / D_MODEL)
        dh2_ref[...] = dh2
        dg_ref[...] += jnp.sum(dh2 * f * r, axis=0, keepdims=True)
        df_ref[...] = _rms_bwd(f, r, g, dh2).astype(BF16)

        @pl.when(i == nt - 1)
        def _():
            loss_ref[...] = jnp.broadcast_to((0.5 / D_MODEL) * jnp.sum(sq_acc[...], axis=-1, keepdims=True), (1, LANES))

    row = lambda n: pl.BlockSpec((tm, n), lambda i: (i, 0))
    return pl.pallas_call(
        body, name="ffn_fwd_loss", grid=(nt,),
        out_shape=[jax.ShapeDtypeStruct((S, D_FF), BF16), jax.ShapeDtypeStruct((S, D_FF), BF16),
                   jax.ShapeDtypeStruct((S, D_MODEL), BF16), jax.ShapeDtypeStruct((S, D_MODEL), F32),
                   jax.ShapeDtypeStruct((1, LANES), F32), jax.ShapeDtypeStruct((1, D_MODEL), F32)],
        in_specs=[row(D_MODEL), _resident(w_gate.shape), _resident(w_up.shape), _resident(w_down.shape),
                  row(D_MODEL), row(D_MODEL), _const((1, D_MODEL))],
        out_specs=[row(D_FF), row(D_FF), row(D_MODEL), row(D_MODEL), _const((1, LANES)), _const((1, D_MODEL))],
        scratch_shapes=[pltpu.VMEM((1, D_MODEL), F32)],
        compiler_params=_cparams(dimension_semantics=("arbitrary",)),
    )(f_in, w_gate, w_up, w_down, h1, target, g_post_ffn)


def _ffn_bwd(df, gt, up, w_gate, w_up, w_down):
    S = df.shape[0]
    tm = min(FFN_TILE, S)

    def body(df_ref, gt_ref, up_ref, wg_ref, wu_ref, wd_ref, dgt_ref, dup_ref, act_ref, dfin_ref):
        df = df_ref[...]
        dfin = jnp.zeros((tm, D_MODEL), F32)
        for c0 in range(0, D_FF, FF_CHUNK):
            cols = slice(c0, c0 + FF_CHUNK)
            dact = _dot_nt(df, wd_ref[cols, :])
            gt = gt_ref[:, cols].astype(F32)
            up = up_ref[:, cols].astype(F32)
            s = jax.nn.sigmoid(gt)
            silu = gt * s
            dgt = (dact * up * (s * (1.0 + gt * (1.0 - s)))).astype(BF16)
            dup = (dact * silu).astype(BF16)
            act_ref[:, cols] = (silu * up).astype(BF16)
            dgt_ref[:, cols] = dgt
            dup_ref[:, cols] = dup
            dfin = dfin + _dot_nt(dgt, wg_ref[:, cols]) + _dot_nt(dup, wu_ref[:, cols])
        dfin_ref[...] = dfin

    row = lambda n: pl.BlockSpec((tm, n), lambda i: (i, 0))
    return pl.pallas_call(
        body, name="ffn_bwd", grid=(S // tm,),
        out_shape=[jax.ShapeDtypeStruct((S, D_FF), BF16)] * 3 + [jax.ShapeDtypeStruct((S, D_MODEL), F32)],
        in_specs=[row(D_MODEL), row(D_FF), row(D_FF), _resident(w_gate.shape), _resident(w_up.shape),
                  _resident(w_down.shape)],
        out_specs=[row(D_FF)] * 3 + [row(D_MODEL)],
        compiler_params=_cparams(),
    )(df, gt, up, w_gate, w_up, w_down)


def _matmul_tn(name, x, y, tn):
    S, K = x.shape
    N = y.shape[1]
    ts = min(GRAD_TILE, S)

    def body(x_ref, y_ref, o_ref):
        @pl.when(pl.program_id(1) == 0)
        def _():
            o_ref[...] = jnp.zeros_like(o_ref)

        o_ref[...] += _dot_tn(x_ref[...].astype(BF16), y_ref[...].astype(BF16))

    return pl.pallas_call(
        body, name=name, grid=(N // tn, S // ts),
        out_shape=jax.ShapeDtypeStruct((K, N), F32),
        in_specs=[pl.BlockSpec((ts, K), lambda n, s: (s, 0)), pl.BlockSpec((ts, tn), lambda n, s: (s, n))],
        out_specs=pl.BlockSpec((K, tn), lambda n, s: (0, n)),
        compiler_params=_cparams(dimension_semantics=("arbitrary", "arbitrary")),
    )(x, y)


def _mix_bwd(dfin, h1, y, dh2, g_pre_ffn, g_post_mix, w_out):
    S = dfin.shape[0]
    tm = min(TOKEN_TILE, S)

    def body(dfin_ref, h1_ref, y_ref, dh2_ref, g2_ref, g1_ref, w_ref, dh1_ref, dy_ref, dco_ref, dao_ref, dg2_ref, dg1_ref):
        @pl.when(pl.program_id(0) == 0)
        def _():
            dg2_ref[...] = jnp.zeros_like(dg2_ref)
            dg1_ref[...] = jnp.zeros_like(dg1_ref)

        h1, dfin = h1_ref[...], dfin_ref[...]
        r2 = _rms_r(h1)
        dh1 = dh2_ref[...] + _rms_bwd(h1, r2, g2_ref[...], dfin)
        dg2_ref[...] += jnp.sum(dfin * h1 * r2, axis=0, keepdims=True)
        y = y_ref[...]
        r1 = _rms_r(y)
        dy = _rms_bwd(y, r1, g1_ref[...], dh1).astype(BF16)
        dg1_ref[...] += jnp.sum(dh1 * y * r1, axis=0, keepdims=True)
        dh1_ref[...] = dh1
        dy_ref[...] = dy
        dco_ref[...] = _dot_nt(dy, w_ref[:CONV_CH, :])
        dao_ref[...] = _dot_nt(dy, w_ref[CONV_CH:, :])

    row = lambda n: pl.BlockSpec((tm, n), lambda i: (i, 0))
    return pl.pallas_call(
        body, name="mix_bwd", grid=(S // tm,),
        out_shape=[jax.ShapeDtypeStruct((S, D_MODEL), F32), jax.ShapeDtypeStruct((S, D_MODEL), BF16),
                   jax.ShapeDtypeStruct((S, CONV_CH), F32), jax.ShapeDtypeStruct((S, SB_WIDTH), F32),
                   jax.ShapeDtypeStruct((1, D_MODEL), F32), jax.ShapeDtypeStruct((1, D_MODEL), F32)],
        in_specs=[row(D_MODEL)] * 4 + [_const((1, D_MODEL)), _const((1, D_MODEL)), _resident(w_out.shape)],
        out_specs=[row(D_MODEL), row(D_MODEL), row(CONV_CH), row(SB_WIDTH), _const((1, D_MODEL)), _const((1, D_MODEL))],
        compiler_params=_cparams(dimension_semantics=("arbitrary",)),
    )(dfin, h1, y, dh2, g_pre_ffn, g_post_mix, w_out)


def _attn_norm_bwd(o, dao, g_attn):
    S = o.shape[0]
    tm = min(TOKEN_TILE, S)
    inv_dh = 1.0 / SB_HEAD_DIM

    def body(o_ref, dao_ref, g_ref, do_ref, dg_ref):
        @pl.when(pl.program_id(0) == 0)
        def _():
            dg_ref[...] = jnp.zeros_like(dg_ref)

        first = _head_masks()
        for p in range(SB_WIDTH // LANES):
            cols = slice(p * LANES, (p + 1) * LANES)
            o, dao, g = o_ref[:, cols], dao_ref[:, cols], g_ref[:, cols]
            sa, sb = _head_sum(o * o, first)
            r = jnp.where(first, lax.rsqrt(sa * inv_dh + EPS), lax.rsqrt(sb * inv_dh + EPS))
            w = dao * g
            wa, wb = _head_sum(w * o, first)
            do_ref[:, cols] = (r * (w - o * (r * r) * (jnp.where(first, wa, wb) * inv_dh))).astype(BF16)
            dg_ref[:, cols] += jnp.sum(dao * o * r, axis=0, keepdims=True)

    row = pl.BlockSpec((tm, SB_WIDTH), lambda i: (i, 0))
    return pl.pallas_call(
        body, name="attn_norm_bwd", grid=(S // tm,),
        out_shape=[jax.ShapeDtypeStruct((S, SB_WIDTH), BF16), jax.ShapeDtypeStruct((1, SB_WIDTH), F32)],
        in_specs=[row, row, _const((1, SB_WIDTH))], out_specs=[row, _const((1, SB_WIDTH))],
        compiler_params=_cparams(dimension_semantics=("arbitrary",)),
    )(o, dao, g_attn)


def _attn_bwd(qkv, do, cl):
    S = qkv.shape[0]
    Q = min(ATTN_BLOCK, S)
    nq = S // Q
    ntiles = nq * (nq + 1) // 2
    assert ATTN_UNROLL % 2 == 0 and ntiles >= 4 + ATTN_UNROLL
    npair = SB_WIDTH // LANES
    tiles = [(i, j) for i in range(nq) for j in range(i + 1)]

    def body(q_ref, k_ref, v_ref, do_ref, cl_ref, dq_ref, dk_ref, dv_ref,
             z_buf, lb_buf, be_buf, g_buf, l_buf, a_buf, gb_buf, dz_buf, pg_buf, mask_buf):
        first = _head_masks()
        lane = lax.broadcasted_iota(jnp.int32, (1, LANES), 1)
        row = lax.broadcasted_iota(jnp.int32, (Q, Q), 0)
        col = lax.broadcasted_iota(jnp.int32, (Q, Q), 1)
        tri = (row > col).astype(BF16)
        tpi = (row <= col).astype(BF16)
        heads = range(2)
        strips = [slice(r0, r0 + ATTN_STRIP) for r0 in range(0, Q, ATTN_STRIP)]
        rows = lambda j: pl.ds(pl.multiple_of(j * Q, Q), Q)
        wide = lambda t: jnp.tile(t, (1, Q // LANES))
        as_int = lambda t: int(t) if isinstance(t, (bool, int)) else t.astype(jnp.int32)

        keep = col < row
        mask_buf[0, 0] = jnp.ones((Q, Q), F32)
        mask_buf[0, 1] = jnp.zeros((Q, Q), F32)
        mask_buf[1, 0] = jnp.where(keep, 1.0, 0.0)
        mask_buf[1, 1] = jnp.where(keep, 0.0, MASKED)
        dq_ref[...] = jnp.zeros_like(dq_ref)
        dk_ref[...] = jnp.zeros_like(dk_ref)
        dv_ref[...] = jnp.zeros_like(dv_ref)

        def scores(t, slot):
            i, j = t
            qh = _split_heads(q_ref[rows(i), :], first)
            kb = k_ref[rows(j), :]
            for h in heads:
                z_buf[slot, h] = _dot_nt(qh[h], kb)

        def logs(t, slot):
            i, j = t
            diag = as_int(i == j)
            for h in heads:
                for r in strips:
                    z2 = z_buf[slot, h, r, :] * LOG2E
                    lb = jnp.minimum(z2, 0.0) - jnp.log2(1.0 + jnp.exp2(-jnp.abs(z2)))
                    l = (lb - z2) * mask_buf[diag, 0, r, :]
                    l_buf[slot, h, r, :] = l.astype(BF16)
                    lb_buf[slot, h, r, :] = lb + mask_buf[diag, 1, r, :]

        def sums(t, slot):
            i, j = t
            doh = _split_heads(do_ref[rows(i), :], first)
            vb = v_ref[rows(j), :]
            return (tuple(_dot(l_buf[slot, h], tri) for h in heads), tuple(_dot_nt(doh[h], vb) for h in heads))

        def weights(t, slot, sm, da):
            i, j = t
            for h in heads:
                c = jnp.sum(jnp.where(lane == j, cl_ref[rows(i), h * LANES:(h + 1) * LANES], 0.0), axis=-1, keepdims=True)
                c = jnp.broadcast_to(c, (Q, LANES))
                for r in strips:
                    lb = lb_buf[slot, h, r, :]
                    a = jnp.exp2(lb + sm[h][r] + wide(c[r]))
                    g = da[h][r] * a
                    a_buf[slot, h, r, :] = a.astype(BF16)
                    be_buf[slot, h, r, :] = jnp.exp2(lb)
                    g_buf[slot, h, r, :] = g
                    gb_buf[slot, h, r, :] = g.astype(BF16)

        def prefix(t, slot):
            i, j = t
            doh = _split_heads(do_ref[rows(i), :], first)
            dv_ref[rows(j), :] += _dot_tn(a_buf[slot, 0], doh[0]) + _dot_tn(a_buf[slot, 1], doh[1])
            return tuple(_dot(gb_buf[slot, h], tpi) for h in heads)

        def dscores(t, slot, pm):
            i, j = t
            for h in heads:
                pg = pg_buf[h] * jnp.where(j == 0, 0.0, 1.0)
                for r in strips:
                    dz = g_buf[slot, h, r, :] - be_buf[slot, h, r, :] * (pm[h][r] + wide(pg[r]))
                    dz_buf[slot, h, r, :] = dz.astype(BF16)
                pg_buf[h] = pg + jnp.broadcast_to(pm[h][:, Q - 1:Q], (Q, LANES))

        def grads(t, slot):
            i, j = t
            qh = _split_heads(q_ref[rows(i), :], first)
            kh = _split_heads(k_ref[rows(j), :], first)
            dq_ref[rows(i), :] += _dot(dz_buf[slot, 0], kh[0]) + _dot(dz_buf[slot, 1], kh[1])
            dk_ref[rows(j), :] += _dot_tn(dz_buf[slot, 0], qh[0]) + _dot_tn(dz_buf[slot, 1], qh[1])

        def iteration(t, p):
            ta, tb, tc, td, te = t
            if ta is not None:
                scores(ta, p)
            if tc is not None:
                sm, da = sums(tc, p)
            if td is not None:
                pm = prefix(td, 1 - p)
            if te is not None:
                grads(te, p)
            if tb is not None:
                logs(tb, 1 - p)
            if tc is not None:
                weights(tc, p, sm, da)
            if td is not None:
                dscores(td, 1 - p, pm)

        def window(n):
            return tuple(tiles[n - k] if 0 <= n - k < ntiles else None for k in range(5))

        def following(t):
            i, j = t
            last = j == i
            return jnp.where(last, i + 1, i), jnp.where(last, 0, j + 1)

        peeled = 4 + (ntiles - 4) % ATTN_UNROLL

        def unrolled_iterations(_, t):
            for n in range(peeled, peeled + ATTN_UNROLL):
                iteration(t, n % 2)
                t = (following(t[0]),) + t[:4]
            return t

        pg_buf[...] = jnp.zeros_like(pg_buf)
        for n in range(peeled):
            iteration(window(n), n % 2)
        first_window = tuple((jnp.int32(i), jnp.int32(j)) for i, j in window(peeled))
        lax.fori_loop(0, (ntiles - peeled) // ATTN_UNROLL, unrolled_iterations, first_window)
        for n in range(ntiles, ntiles + 4):
            iteration(window(n), n % 2)
        dq_ref[...] = dq_ref[...] * (1.0 / math.sqrt(SB_HEAD_DIM))

    col_block = lambda off: pl.BlockSpec((S, LANES), lambda p: (0, off + p), pipeline_mode=pl.Buffered(1))
    return pl.pallas_call(
        body, name="attn_bwd", grid=(npair,),
        out_shape=[jax.ShapeDtypeStruct((S, SB_WIDTH), F32)] * 3,
        in_specs=[col_block(0), col_block(npair), col_block(2 * npair), col_block(0),
                  pl.BlockSpec((S, 2 * LANES), lambda p: (0, p), pipeline_mode=pl.Buffered(1))],
        out_specs=[pl.BlockSpec((S, LANES), lambda p: (0, p), pipeline_mode=pl.Buffered(1))] * 3,
        scratch_shapes=[pltpu.VMEM((2, 2, Q, Q), F32)] * 4 + [pltpu.VMEM((2, 2, Q, Q), BF16)] * 4
        + [pltpu.VMEM((2, Q, LANES), F32), pltpu.VMEM((2, 2, Q, Q), F32)],
        compiler_params=_cparams(dimension_semantics=("arbitrary",)),
    )(qkv, qkv, qkv, do, cl)


def _conv_bwd(u_conv, y_conv, dco, conv_w, ln_g, ln_b):
    S = u_conv.shape[0]
    tc = min(TOKEN_TILE, S)
    nt = S // tc
    per = tc // CONV_HALO
    groups = CONV_CHUNK // 8

    def body(u_ref, halo_ref, y_ref, dco_ref, cw_ref, lg_ref, lb_ref, du_ref, dcw_ref, dsm_ref, glu_ext, dyc_ext, sg_buf,
             dcw_acc, dsm_acc, glu_sh, dyc_sh):
        i = pl.program_id(0)
        ti = nt - 1 - i

        @pl.when(i == 0)
        def _():
            dyc_ext[tc:, :] = jnp.zeros((CONV_HALO, CONV_CH), F32)
            dcw_acc[...] = jnp.zeros_like(dcw_acc)
            dsm_acc[...] = jnp.zeros_like(dsm_acc)

        @pl.when(i > 0)
        def _():
            dyc_ext[tc:, :] = dyc_ext[0:CONV_HALO, :]

        glu_ext[0:CONV_HALO, :] = jnp.where(ti > 0, _glu(halo_ref[...])[2], 0.0)
        val, sg, glu = _glu(u_ref[...])
        glu_ext[CONV_HALO:, :] = glu
        sg_buf[...] = sg
        _shift_copies(glu_ext, glu_sh)

        dcb = jnp.zeros((8, CONV_CH), F32)
        dlg = jnp.zeros((8, CONV_CH), F32)
        dlb = jnp.zeros((8, CONV_CH), F32)
        fold = lambda t: jnp.sum(t.reshape(groups, 8, CONV_CH), axis=0)
        for r0 in range(0, tc, CONV_CHUNK):
            y = y_ref[r0:r0 + CONV_CHUNK, :]
            mu = jnp.mean(y, axis=-1, keepdims=True)
            yc = y - mu
            rstd = lax.rsqrt(jnp.mean(yc * yc, axis=-1, keepdims=True) + EPS)
            yn = yc * rstd
            yl = yn * lg_ref[...] + lb_ref[...]
            s = jax.nn.sigmoid(yl)
            dyl = dco_ref[r0:r0 + CONV_CHUNK, :] * (s * (1.0 + yl * (1.0 - s)))
            dlg = dlg + fold(dyl * yn)
            dlb = dlb + fold(dyl)
            wv = dyl * lg_ref[...]
            dyc = rstd * (wv - jnp.mean(wv, axis=-1, keepdims=True) - yn * jnp.mean(wv * yn, axis=-1, keepdims=True))
            dcb = dcb + fold(dyc)
            dyc_ext[r0:r0 + CONV_CHUNK, :] = dyc
        dsm_acc[0:8, :] += dcb
        dsm_acc[8:16, :] += dlg
        dsm_acc[16:24, :] += dlb
        _shift_copies(dyc_ext, dyc_sh)

        for r0 in range(0, tc, CONV_CHUNK):
            dyc = dyc_ext[r0:r0 + CONV_CHUNK, :]
            dglu = jnp.zeros((CONV_CHUNK, CONV_CH), F32)
            base = r0 + CONV_HALO - (CONV_WIDTH - 1)
            for w in range(CONV_WIDTH):
                back = r0 + (CONV_WIDTH - 1) - w
                dglu = dglu + cw_ref[w:w + 1, :] * _window(dyc_ext, dyc_sh, back, CONV_CHUNK)
                dcw_acc[8 * w:8 * w + 8, :] += fold(dyc * _window(glu_ext, glu_sh, base + w, CONV_CHUNK))
            sg = sg_buf[r0:r0 + CONV_CHUNK, :]
            v = u_ref[r0:r0 + CONV_CHUNK, :CONV_CH]
            du_ref[r0:r0 + CONV_CHUNK, :CONV_CH] = (dglu * sg).astype(BF16)
            du_ref[r0:r0 + CONV_CHUNK, CONV_CH:] = (dglu * v * sg * (1.0 - sg)).astype(BF16)

        @pl.when(i == nt - 1)
        def _():
            for w in range(CONV_WIDTH):
                dcw_ref[w:w + 1, :] = jnp.sum(dcw_acc[8 * w:8 * w + 8, :], axis=0, keepdims=True)
            dcw_ref[CONV_WIDTH:, :] = jnp.zeros((CONV_HALO - CONV_WIDTH, CONV_CH), F32)
            for k in range(3):
                dsm_ref[k:k + 1, :] = jnp.sum(dsm_acc[8 * k:8 * k + 8, :], axis=0, keepdims=True)
            dsm_ref[3:, :] = jnp.zeros((5, CONV_CH), F32)

    return pl.pallas_call(
        body, name="conv_bwd", grid=(nt,),
        out_shape=[jax.ShapeDtypeStruct((S, 2 * CONV_CH), BF16), jax.ShapeDtypeStruct((CONV_HALO, CONV_CH), F32),
                   jax.ShapeDtypeStruct((8, CONV_CH), F32)],
        in_specs=[pl.BlockSpec((tc, 2 * CONV_CH), lambda i: (nt - 1 - i, 0)),
                  pl.BlockSpec((CONV_HALO, 2 * CONV_CH), lambda i: (jnp.maximum((nt - 1 - i) * per - 1, 0), 0)),
                  pl.BlockSpec((tc, CONV_CH), lambda i: (nt - 1 - i, 0)), pl.BlockSpec((tc, CONV_CH), lambda i: (nt - 1 - i, 0)),
                  _const((CONV_HALO, CONV_CH)), _const((1, CONV_CH)), _const((1, CONV_CH))],
        out_specs=[pl.BlockSpec((tc, 2 * CONV_CH), lambda i: (nt - 1 - i, 0)), _const((CONV_HALO, CONV_CH)),
                   _const((8, CONV_CH))],
        scratch_shapes=[pltpu.VMEM((tc + CONV_HALO, CONV_CH), F32), pltpu.VMEM((tc + CONV_HALO, CONV_CH), F32),
                        pltpu.VMEM((tc, CONV_CH), F32), pltpu.VMEM((8 * CONV_HALO, CONV_CH), F32),
                        pltpu.VMEM((24, CONV_CH), F32)]
        + [pltpu.VMEM((SUBLANES - 1, tc + CONV_HALO - SUBLANES, CONV_CH), F32)] * 2,
        compiler_params=_cparams(dimension_semantics=("arbitrary",)),
    )(u_conv, u_conv, y_conv, dco, conv_w, ln_g, ln_b)


def _in_proj_bwd(du_conv, dq, dk, dv, w_in, x, g, dh1):
    S = x.shape[0]
    tm = min(TOKEN_TILE, S)
    nconv = 2 * CONV_CH

    def body(duc_ref, dq_ref, dk_ref, dv_ref, w_ref, x_ref, g_ref, dh1_ref, dx_ref, dg_ref):
        @pl.when(pl.program_id(0) == 0)
        def _():
            dg_ref[...] = jnp.zeros_like(dg_ref)

        da = _dot_nt(duc_ref[...], w_ref[:, :nconv])
        for n, ref in enumerate((dq_ref, dk_ref, dv_ref)):
            c0 = nconv + n * SB_WIDTH
            da = da + _dot_nt(ref[...].astype(BF16), w_ref[:, c0:c0 + SB_WIDTH])
        xf = x_ref[...]
        r = _rms_r(xf)
        dx_ref[...] = dh1_ref[...] + _rms_bwd(xf, r, g_ref[...], da)
        dg_ref[...] += jnp.sum(da * xf * r, axis=0, keepdims=True)

    row = lambda n: pl.BlockSpec((tm, n), lambda i: (i, 0))
    return pl.pallas_call(
        body, name="in_proj_bwd", grid=(S // tm,),
        out_shape=[jax.ShapeDtypeStruct((S, D_MODEL), F32), jax.ShapeDtypeStruct((1, D_MODEL), F32)],
        in_specs=[row(nconv), row(SB_WIDTH), row(SB_WIDTH), row(SB_WIDTH), _resident(w_in.shape), row(D_MODEL),
                  _const((1, D_MODEL)), row(D_MODEL)],
        out_specs=[row(D_MODEL), _const((1, D_MODEL))],
        compiler_params=_cparams(dimension_semantics=("arbitrary",)),
    )(du_conv, dq, dk, dv, w_in, x, g, dh1)


def _layer_grads(xs, target, g_pre_mix, w_in_f, conv_w_f, conv_b, conv_ln_g, conv_ln_b, attn_g, g_post_mix, g_pre_ffn,
                 g_post_ffn, late_weights, send_grads):
    a, u_conv, qkv = _in_proj(xs, g_pre_mix, w_in_f)
    conv_out, y_conv = _conv_fwd(u_conv, conv_w_f, conv_b, conv_ln_g, conv_ln_b)
    o, attn_out, cl = _attn_fwd(qkv, attn_g)
    w_out_f, w_gate_f, w_up_f, w_down_f = late_weights(attn_out)
    y, h1, f_in = _out_proj(conv_out, attn_out, w_out_f, xs, g_post_mix, g_pre_ffn)
    gt, up, df, dh2, loss_part, d_g_post_ffn = _ffn_fwd_loss(f_in, w_gate_f, w_up_f, w_down_f, h1, target, g_post_ffn)

    dgt, dup, act, dfin = _ffn_bwd(df, gt, up, w_gate_f, w_up_f, w_down_f)
    d_w_down = _matmul_tn("grad_w_down", act, df, 512)
    d_w_gate = _matmul_tn("grad_w_gate", f_in, dgt, FF_CHUNK)
    d_w_up = _matmul_tn("grad_w_up", f_in, dup, FF_CHUNK)
    sent = send_grads("ffn", (d_w_gate, d_w_up, d_w_down))
    dh1, dy, dco, dao, d_g_pre_ffn, d_g_post_mix = _mix_bwd(dfin, h1, y, dh2, g_pre_ffn + sent, g_post_mix, w_out_f)
    d_w_out = jnp.concatenate([_matmul_tn("grad_w_out_conv", conv_out, dy, D_MODEL),
                               _matmul_tn("grad_w_out_attn", attn_out, dy, D_MODEL)], axis=0)
    sent = send_grads("w_out", (d_w_out,))
    do, d_attn_g = _attn_norm_bwd(o, dao, attn_g + sent)
    dq, dk, dv = _attn_bwd(qkv, do, cl)
    du_conv, d_conv_w, d_conv_small = _conv_bwd(u_conv, y_conv, dco, conv_w_f, conv_ln_g, conv_ln_b)
    d_w_in = jnp.concatenate([_matmul_tn("grad_w_in_conv", a, du_conv, 2 * CONV_CH),
                              _matmul_tn("grad_w_in_q", a, dq, SB_WIDTH), _matmul_tn("grad_w_in_k", a, dk, SB_WIDTH),
                              _matmul_tn("grad_w_in_v", a, dv, SB_WIDTH)], axis=1)
    sent = send_grads("w_in", (d_w_in,))
    grad_x, d_g_pre_mix = _in_proj_bwd(du_conv, dq, dk, dv, w_in_f, xs, g_pre_mix + sent, dh1)
    return (loss_part, grad_x, d_conv_w, d_conv_small, d_attn_g, d_g_pre_mix, d_g_post_mix, d_g_pre_ffn, d_g_post_ffn)


def _cols_to_blocks(w):
    K, N = w.shape
    return jnp.transpose(w.reshape(K, N_DEV, N // N_DEV), (1, 0, 2))


def _blocks_to_cols(blocks):
    n_dev, K, n = blocks.shape
    return jnp.transpose(blocks, (1, 0, 2)).reshape(K, n_dev * n)


def kernel(x, g_pre_mix, w_in, conv_w, conv_b, conv_ln_g, conv_ln_b, attn_norm_g, w_out, g_post_mix, g_pre_ffn, w_gate, w_up, w_down, g_post_ffn, loss_target, m_g_pre_mix, m_w_in, m_conv_w, m_conv_b, m_conv_ln_g, m_conv_ln_b, m_attn_norm_g, m_w_out, m_g_post_mix, m_g_pre_ffn, m_w_gate, m_w_up, m_w_down, m_g_post_ffn, v_g_pre_mix, v_w_in, v_conv_w, v_conv_b, v_conv_ln_g, v_conv_ln_b, v_attn_norm_g, v_w_out, v_g_post_mix, v_g_pre_ffn, v_w_gate, v_w_up, v_w_down, v_g_post_ffn):
    xs = x[0]
    target = loss_target[0]
    S = xs.shape[0]
    me = 4 * lax.axis_index("x") + 2 * lax.axis_index("y") + lax.axis_index("c")
    cw_shard = conv_w.reshape(CONV_WIDTH, CONV_CH // N_DEV)
    attn_g = attn_norm_g.reshape(1, SB_WIDTH)

    gathered = _all_gather([w_in[0].astype(BF16), cw_shard])
    w_in_f = _blocks_to_cols(gathered[0])
    conv_w_f = jnp.pad(_blocks_to_cols(gathered[1]), ((0, CONV_HALO - CONV_WIDTH), (0, 0)))
    gathered_zero = gathered[2][0:1, 0:1].astype(BF16)
    late = [w_out[0].astype(BF16) + gathered_zero, w_gate[0].astype(BF16), w_up[0].astype(BF16), w_down[0].astype(BF16)]
    late_started = _exchange_start("all_gather_late_start", late, scatter=False)

    def late_weights(after):
        lands = _exchange_wait("all_gather_late_wait", late_started, False, after)
        wo, wg, wu, wd = [lax.dynamic_update_index_in_dim(land, own, me, 0) for land, own in zip(lands, late)]
        return wo.reshape(D_MODEL, D_MODEL), _blocks_to_cols(wg), _blocks_to_cols(wu), wd.reshape(D_FF, D_MODEL)

    started = {}

    def send_grads(name, grads):
        blocks = [g.reshape(N_DEV, g.shape[0] // N_DEV, g.shape[1]) if g.shape[1] == D_MODEL else _cols_to_blocks(g)
                  for g in grads]
        payload = BF16 if name == "w_in" else F32
        sent = _exchange_start("reduce_scatter_" + name + "_start", [b.astype(payload) for b in blocks], scatter=True)
        started[name] = (sent, blocks)
        return sent[-1][0:1, 0:1]

    (loss_part, grad_x, d_conv_w, d_conv_small, d_attn_g, d_g_pre_mix, d_g_post_mix, d_g_pre_ffn,
     d_g_post_ffn) = _layer_grads(
        xs, target, g_pre_mix + late_started[-1][0:1, 0:1], w_in_f, conv_w_f, conv_b, conv_ln_g, conv_ln_b, attn_g,
        g_post_mix, g_pre_ffn, g_post_ffn, late_weights, send_grads)

    def reduced(name, after, shards):
        st, blocks = started[name]
        lands = _exchange_wait("reduce_scatter_" + name + "_wait", st, True, after)
        return [_sum_adamw("adamw_" + wn, land, lax.dynamic_index_in_dim(blk, me, 0, keepdims=False), w[0], m[0], v[0])
                for land, blk, (wn, w, m, v) in zip(lands, blocks, shards)]

    two = lambda t: t.reshape(2, CONV_CH)
    small_g = jnp.concatenate([
        d_conv_w,
        d_conv_small[0:3],
        d_attn_g,
        two(d_g_pre_mix), two(d_g_post_mix), two(d_g_pre_ffn), two(d_g_post_ffn),
        jnp.broadcast_to(loss_part[0:1, 0:1], (1, CONV_CH)),
        jnp.zeros((3, CONV_CH), F32)], axis=0)
    small_g = _all_reduce_small(small_g)
    loss = small_g[44, 0]
    g_conv_w = lax.dynamic_slice(small_g, (0, me * (CONV_CH // N_DEV)), (CONV_WIDTH, CONV_CH // N_DEV))
    pack = lambda cb, lg, lb, ag, g1, g2, g3, g4: jnp.concatenate(
        [cb, lg, lb, ag.reshape(1, SB_WIDTH), two(g1), two(g2), two(g3), two(g4), jnp.zeros((4, CONV_CH), F32)], axis=0)
    sm_g = small_g[CONV_HALO:]
    sm_delta, sm_m, sm_v = _adamw_small(
        "adamw_small",
        pack(conv_b, conv_ln_g, conv_ln_b, attn_norm_g, g_pre_mix, g_post_mix, g_pre_ffn, g_post_ffn), sm_g,
        pack(m_conv_b, m_conv_ln_g, m_conv_ln_b, m_attn_norm_g, m_g_pre_mix, m_g_post_mix, m_g_pre_ffn, m_g_post_ffn),
        pack(v_conv_b, v_conv_ln_g, v_conv_ln_b, v_attn_norm_g, v_g_pre_mix, v_g_post_mix, v_g_pre_ffn, v_g_post_ffn))
    cw_delta, cw_m, cw_v = _adamw_small("adamw_conv_w", cw_shard, g_conv_w,
                                        m_conv_w.reshape(cw_shard.shape), v_conv_w.reshape(cw_shard.shape))

    ffn = reduced("ffn", grad_x, [("w_gate", w_gate, m_w_gate, v_w_gate), ("w_up", w_up, m_w_up, v_w_up),
                                  ("w_down", w_down, m_w_down, v_w_down)])
    big = {"w_gate": ffn[0], "w_up": ffn[1], "w_down": ffn[2],
           "w_out": reduced("w_out", ffn[2][0], [("w_out", w_out, m_w_out, v_w_out)])[0]}
    big["w_in"] = reduced("w_in", big["w_out"][0], [("w_in", w_in, m_w_in, v_w_in)])[0]

    def unpack(t):
        return {"conv_b": t[0:1], "conv_ln_g": t[1:2], "conv_ln_b": t[2:3], "attn_norm_g": t[3:4].reshape(1, SB_HEADS, SB_HEAD_DIM),
                "g_pre_mix": t[4:6].reshape(1, D_MODEL), "g_post_mix": t[6:8].reshape(1, D_MODEL),
                "g_pre_ffn": t[8:10].reshape(1, D_MODEL), "g_post_ffn": t[10:12].reshape(1, D_MODEL)}

    names = ["g_pre_mix", "w_in", "conv_w", "conv_b", "conv_ln_g", "conv_ln_b", "attn_norm_g", "w_out", "g_post_mix",
             "g_pre_ffn", "w_gate", "w_up", "w_down", "g_post_ffn"]
    kinds = []
    for idx, small in enumerate((sm_g, sm_delta, sm_m, sm_v)):
        d = unpack(small)
        d["conv_w"] = (g_conv_w, cw_delta, cw_m, cw_v)[idx].reshape(1, CONV_WIDTH, 1, CONV_CH // N_DEV)
        for n in big:
            d[n] = big[n][idx][None]
        kinds.append([d[n] for n in names])

    return (loss, grad_x[None], *kinds[0], *kinds[1], *kinds[2], *kinds[3])
```

```python
import functools
import math

import jax
import jax.numpy as jnp
from jax import lax
from jax.experimental import pallas as pl
from jax.experimental.pallas import tpu as pltpu

F32 = jnp.float32
BF16 = jnp.bfloat16
MESH = pl.DeviceIdType.MESH

N_DEV = 8
D_MODEL = 1024
CONV_CH = 512
CONV_WIDTH = 31
SB_HEADS = 8
SB_HEAD_DIM = 64
SB_WIDTH = SB_HEADS * SB_HEAD_DIM
D_FF = 2816
EPS = 1e-6
LOG2E = 1.4426950408889634
Z2_MAX = 100.0
MASKED = -1e30
ADAM_LR = 0.001
ADAM_B1 = 0.9
ADAM_B2 = 0.999
ADAM_EPS = 1e-08
ADAM_WD = 0.01
ADAM_STEP = 10

SUBLANES = 8
LANES = 128
VMEM_LIMIT = 56 * 1024 * 1024
TOKEN_TILE = 512
GRAD_TILE = 1024
FFN_TILE = 256
ATTN_UNROLL = 4
ATTN_STRIP = 32
ATTN_BLOCK = 256
CONV_HALO = 32
CONV_CHUNK = 64
FF_CHUNK = D_FF // 2


def _cparams(**kw):
    return pltpu.CompilerParams(vmem_limit_bytes=VMEM_LIMIT, **kw)


def _resident(shape):
    return pl.BlockSpec(shape, lambda *_: (0,) * len(shape), pipeline_mode=pl.Buffered(1))


def _const(shape):
    return pl.BlockSpec(shape, lambda *_: (0,) * len(shape))


def _rms_r(xf):
    return lax.rsqrt(jnp.mean(xf * xf, axis=-1, keepdims=True) + EPS)


def _rms_bwd(xf, r, g, dout):
    w = dout * g
    return r * (w - xf * (r * r) * jnp.mean(w * xf, axis=-1, keepdims=True))


def _dot(a, b):
    return jnp.dot(a, b, preferred_element_type=F32)


def _dot_nt(a, b):
    return lax.dot_general(a, b, (((1,), (1,)), ((), ())), preferred_element_type=F32)


def _dot_tn(a, b):
    return lax.dot_general(a, b, (((0,), (0,)), ((), ())), preferred_element_type=F32)


def _peer(x, y, c, k):
    px = 1 - x if (k >> 2) & 1 else x
    py = 1 - y if (k >> 1) & 1 else y
    pc = 1 - c if k & 1 else c
    return (px, py, pc), 4 * px + 2 * py + pc


def _all_gather(shards):
    n = len(shards)

    def body(*refs):
        ins, outs, done = refs[:n], refs[n:2 * n], refs[2 * n]
        send_sems, recv_sems, local_sems = refs[2 * n + 1:]
        x, y, c = lax.axis_index("x"), lax.axis_index("y"), lax.axis_index("c")
        me, sibling = (x, y, c), (x, y, 1 - c)
        chips = [(1 - x, y), (x, 1 - y), (1 - x, 1 - y)]
        number = lambda d: 4 * d[0] + 2 * d[1] + d[2]

        def copy(a, k, block, to, src=None):
            rows = outs[a].at[number(block)]
            return pltpu.make_async_remote_copy(
                src_ref=rows if src is None else src, dst_ref=rows, send_sem=send_sems.at[a * (N_DEV - 1) + k],
                recv_sem=recv_sems.at[a * (N_DEV - 1) + k], device_id=to, device_id_type=MESH)

        copies = [pltpu.make_async_copy(ins[a], outs[a].at[number(me)], local_sems.at[a]) for a in range(n)]
        for mine in copies:
            mine.start()
        sent = [copy(a, 0, me, sibling, src=ins[a]) for a in range(n)]
        sent += [copy(a, 1 + j, me, (*chip, c), src=ins[a]) for j, chip in enumerate(chips) for a in range(n)]
        for cp in sent:
            cp.start()
        for j, chip in enumerate(chips):
            for a in range(n):
                copy(a, 1 + j, (*chip, c), me).wait_recv()
                passed = copy(a, 4 + j, (*chip, c), sibling)
                passed.start()
                sent.append(passed)
        for a in range(n):
            copy(a, 0, sibling, me).wait_recv()
            for j, chip in enumerate(chips):
                copy(a, 4 + j, (*chip, 1 - c), me).wait_recv()
        for cp in sent:
            cp.wait_send()
        for mine in copies:
            mine.wait()
        done[...] = jnp.zeros_like(done)

    any_spec = pl.BlockSpec(memory_space=pl.ANY)
    return pl.pallas_call(
        body, name="all_gather_weights",
        out_shape=[jax.ShapeDtypeStruct((N_DEV,) + s.shape, s.dtype) for s in shards] + [jax.ShapeDtypeStruct((8, LANES), F32)],
        in_specs=[any_spec] * n, out_specs=[any_spec] * n + [pl.BlockSpec(memory_space=pltpu.VMEM)],
        scratch_shapes=[pltpu.SemaphoreType.DMA((n * (N_DEV - 1),)), pltpu.SemaphoreType.DMA((n * (N_DEV - 1),)),
                        pltpu.SemaphoreType.DMA((n,))],
        compiler_params=pltpu.CompilerParams(has_side_effects=True),
    )(*shards)


def _adamw(w, g, m, v):
    m = ADAM_B1 * m + (1.0 - ADAM_B1) * g
    v = ADAM_B2 * v + (1.0 - ADAM_B2) * (g * g)
    m_hat = m / (1.0 - ADAM_B1 ** ADAM_STEP)
    v_hat = v / (1.0 - ADAM_B2 ** ADAM_STEP)
    delta = -ADAM_LR * (m_hat / (jnp.sqrt(v_hat) + ADAM_EPS) + ADAM_WD * w)
    return delta, m, v


def _exchange_and_sum(src_block, recv_ref, send_sems, recv_sems, local_sem):
    x, y, c = lax.axis_index("x"), lax.axis_index("y"), lax.axis_index("c")
    me = 4 * x + 2 * y + c
    mine = pltpu.make_async_copy(src_block(me), recv_ref.at[me], local_sem)
    mine.start()
    for k in range(1, N_DEV):
        peer, peer_block = _peer(x, y, c, k)
        pltpu.make_async_remote_copy(
            src_ref=src_block(peer_block), dst_ref=recv_ref.at[me], send_sem=send_sems.at[k - 1],
            recv_sem=recv_sems.at[k - 1], device_id=peer, device_id_type=MESH).start()
    for k in range(1, N_DEV):
        peer, peer_block = _peer(x, y, c, k)
        arrived = pltpu.make_async_remote_copy(
            src_ref=src_block(peer_block), dst_ref=recv_ref.at[peer_block], send_sem=send_sems.at[k - 1],
            recv_sem=recv_sems.at[k - 1], device_id=peer, device_id_type=MESH)
        arrived.wait_send()
        arrived.wait_recv()
    mine.wait()


HBM_SPEC = pl.BlockSpec(memory_space=pltpu.HBM)
SEM_SPEC = pl.BlockSpec(memory_space=pltpu.SEMAPHORE)
DATAFLOW = pltpu.SideEffectType.DATAFLOW_SIDE_EFFECTING


def _exchange_copies(srcs, lands, send_sems, recv_sems, scatter, wait):
    x, y, c = lax.axis_index("x"), lax.axis_index("y"), lax.axis_index("c")
    me = 4 * x + 2 * y + c
    for k in range(1, N_DEV):
        peer, peer_block = _peer(x, y, c, k)
        for a in range(len(srcs)):
            s = a * (N_DEV - 1) + k - 1
            src = srcs[a].at[peer_block] if scatter else srcs[a]
            copy = pltpu.make_async_remote_copy(
                src_ref=src, dst_ref=lands[a].at[peer_block if wait else me], send_sem=send_sems.at[s],
                recv_sem=recv_sems.at[s], device_id=peer, device_id_type=MESH)
            if wait:
                copy.wait_send()
                copy.wait_recv()
            else:
                copy.start()


def _exchange_start(name, arrays, scatter):
    n = len(arrays)
    land_shapes = [a.shape if scatter else (N_DEV,) + a.shape for a in arrays]

    def body(*refs):
        _exchange_copies(refs[:n], refs[n:2 * n], refs[2 * n], refs[2 * n + 1], scatter, wait=False)
        refs[-1][...] = jnp.zeros_like(refs[-1])

    sems = pltpu.SemaphoreType.DMA((n * (N_DEV - 1),))
    hbm = lambda t: pltpu.with_memory_space_constraint(t, pltpu.HBM)
    return pl.pallas_call(
        body, name=name,
        out_shape=(sems, sems, *[pltpu.HBM(a.shape, a.dtype) for a in arrays],
                   *[pltpu.HBM(ls, a.dtype) for ls, a in zip(land_shapes, arrays)], jax.ShapeDtypeStruct((8, LANES), F32)),
        in_specs=[HBM_SPEC] * (2 * n),
        out_specs=(SEM_SPEC, SEM_SPEC, *[HBM_SPEC] * (2 * n), pl.BlockSpec(memory_space=pltpu.VMEM)),
        input_output_aliases={a: 2 + a for a in range(2 * n)},
        compiler_params=pltpu.CompilerParams(has_side_effects=DATAFLOW),
    )(*[hbm(a) for a in arrays], *[hbm(lax.empty(ls, a.dtype)) for ls, a in zip(land_shapes, arrays)])


def _exchange_wait(name, started, scatter, after):
    n = (len(started) - 3) // 2
    send_sems, recv_sems = started[0], started[1]
    arrays, lands = started[2:2 + n], started[2 + n:2 + 2 * n]

    def body(*refs):
        _exchange_copies(refs[:n], refs[n:2 * n], refs[2 * n], refs[2 * n + 1], scatter, wait=True)

    return pl.pallas_call(
        body, name=name,
        out_shape=[pltpu.HBM(t.shape, t.dtype) for t in (*arrays, *lands)],
        in_specs=[HBM_SPEC] * (2 * n) + [SEM_SPEC, SEM_SPEC, pl.BlockSpec(memory_space=pl.ANY)],
        out_specs=[HBM_SPEC] * (2 * n),
        input_output_aliases={a: a for a in range(2 * n)},
        compiler_params=pltpu.CompilerParams(has_side_effects=DATAFLOW),
    )(*arrays, *lands, send_sems, recv_sems, after)[n:]


def _sum_adamw(name, land, own, w, m, v):
    _, M, N = land.shape
    rows = math.gcd(M, 128)

    def body(land_ref, own_ref, w_ref, m_ref, v_ref, grad_ref, delta_ref, nm_ref, nv_ref):
        x, y, c = lax.axis_index("x"), lax.axis_index("y"), lax.axis_index("c")
        g = own_ref[...]
        for k in range(1, N_DEV):
            g = g + land_ref[_peer(x, y, c, k)[1]].astype(F32)
        delta, nm, nv = _adamw(w_ref[...], g, m_ref[...], v_ref[...])
        grad_ref[...] = g
        delta_ref[...] = delta
        nm_ref[...] = nm
        nv_ref[...] = nv

    row = pl.BlockSpec((rows, N), lambda i: (i, 0))
    return pl.pallas_call(
        body, name=name, grid=(M // rows,), out_shape=[jax.ShapeDtypeStruct((M, N), F32)] * 4,
        in_specs=[pl.BlockSpec((N_DEV, rows, N), lambda i: (0, i, 0)), row, row, row, row], out_specs=[row] * 4,
        compiler_params=_cparams(),
    )(land, own, w, m, v)


def _all_reduce_small(g):
    R, C = g.shape

    def body(g_ref, out_ref, recv_ref, send_sems, recv_sems, local_sem):
        _exchange_and_sum(lambda b: g_ref, recv_ref, send_sems, recv_sems, local_sem)
        total = recv_ref[0]
        for b in range(1, N_DEV):
            total = total + recv_ref[b]
        out_ref[...] = total

    vmem = pl.BlockSpec(memory_space=pltpu.VMEM)
    return pl.pallas_call(
        body, name="all_reduce_small_grads", out_shape=jax.ShapeDtypeStruct((R, C), F32),
        in_specs=[vmem], out_specs=vmem,
        scratch_shapes=[pltpu.VMEM((N_DEV, R, C), F32), pltpu.SemaphoreType.DMA((N_DEV - 1,)),
                        pltpu.SemaphoreType.DMA((N_DEV - 1,)), pltpu.SemaphoreType.DMA(())],
        compiler_params=_cparams(has_side_effects=True),
    )(g)


def _adamw_small(name, w, g, m, v):
    def body(w_ref, g_ref, m_ref, v_ref, delta_ref, nm_ref, nv_ref):
        delta, nm, nv = _adamw(w_ref[...], g_ref[...], m_ref[...], v_ref[...])
        delta_ref[...] = delta
        nm_ref[...] = nm
        nv_ref[...] = nv

    vmem = pl.BlockSpec(memory_space=pltpu.VMEM)
    return pl.pallas_call(body, name=name, out_shape=[jax.ShapeDtypeStruct(w.shape, F32)] * 3,
                          in_specs=[vmem] * 4, out_specs=[vmem] * 3)(w, g, m, v)


def _in_proj(x, g, w_in):
    S = x.shape[0]
    tm = min(TOKEN_TILE, S)
    nconv = 2 * CONV_CH

    def body(x_ref, g_ref, w_ref, a_ref, uc_ref, qkv_ref):
        xf = x_ref[...]
        a = (xf * _rms_r(xf) * g_ref[...]).astype(BF16)
        a_ref[...] = a
        uc_ref[...] = _dot(a, w_ref[:, :nconv])
        qkv_ref[:, :SB_WIDTH] = (_dot(a, w_ref[:, nconv:nconv + SB_WIDTH]) * (1.0 / math.sqrt(SB_HEAD_DIM))).astype(BF16)
        qkv_ref[:, SB_WIDTH:] = _dot(a, w_ref[:, nconv + SB_WIDTH:]).astype(BF16)

    row = lambda n: pl.BlockSpec((tm, n), lambda i: (i, 0))
    return pl.pallas_call(
        body, name="in_proj", grid=(S // tm,),
        out_shape=[jax.ShapeDtypeStruct((S, D_MODEL), BF16), jax.ShapeDtypeStruct((S, nconv), F32),
                   jax.ShapeDtypeStruct((S, 3 * SB_WIDTH), BF16)],
        in_specs=[row(D_MODEL), _const((1, D_MODEL)), _resident(w_in.shape)],
        out_specs=[row(D_MODEL), row(nconv), row(3 * SB_WIDTH)],
        compiler_params=_cparams(),
    )(x, g, w_in)


def _glu(u):
    val, gate = u[:, :CONV_CH], u[:, CONV_CH:]
    sg = jax.nn.sigmoid(gate)
    return val, sg, val * sg


def _shift_copies(ext, shifted):
    n = shifted.shape[1]
    for r in range(1, SUBLANES):
        shifted[r - 1] = ext[r:r + n, :]


def _window(ext, shifted, start, rows):
    r = start % SUBLANES
    return ext[start:start + rows, :] if r == 0 else shifted[r - 1, start - r:start - r + rows, :]


def _conv_rows(glu_ext, glu_sh, cw_ref, r0, rows):
    base = r0 + CONV_HALO - (CONV_WIDTH - 1)
    acc = cw_ref[0:1, :] * _window(glu_ext, glu_sh, base, rows)
    for w in range(1, CONV_WIDTH):
        acc = acc + cw_ref[w:w + 1, :] * _window(glu_ext, glu_sh, base + w, rows)
    return acc


def _conv_fwd(u_conv, conv_w, conv_b, ln_g, ln_b):
    S = u_conv.shape[0]
    tc = min(TOKEN_TILE, S)

    def body(u_ref, cw_ref, cb_ref, lg_ref, lb_ref, out_ref, y_ref, glu_ext, glu_sh):
        i = pl.program_id(0)

        @pl.when(i == 0)
        def _():
            glu_ext[0:CONV_HALO, :] = jnp.zeros((CONV_HALO, CONV_CH), F32)

        @pl.when(i > 0)
        def _():
            glu_ext[0:CONV_HALO, :] = glu_ext[tc:tc + CONV_HALO, :]

        glu_ext[CONV_HALO:, :] = _glu(u_ref[...])[2]
        _shift_copies(glu_ext, glu_sh)
        for r0 in range(0, tc, CONV_CHUNK):
            y = _conv_rows(glu_ext, glu_sh, cw_ref, r0, CONV_CHUNK) + cb_ref[...]
            y_ref[r0:r0 + CONV_CHUNK, :] = y
            mu = jnp.mean(y, axis=-1, keepdims=True)
            yc = y - mu
            yn = yc * lax.rsqrt(jnp.mean(yc * yc, axis=-1, keepdims=True) + EPS)
            yl = yn * lg_ref[...] + lb_ref[...]
            out_ref[r0:r0 + CONV_CHUNK, :] = (yl * jax.nn.sigmoid(yl)).astype(BF16)

    return pl.pallas_call(
        body, name="conv_fwd", grid=(S // tc,),
        out_shape=[jax.ShapeDtypeStruct((S, CONV_CH), BF16), jax.ShapeDtypeStruct((S, CONV_CH), F32)],
        in_specs=[pl.BlockSpec((tc, 2 * CONV_CH), lambda i: (i, 0)), _const((CONV_HALO, CONV_CH)),
                  _const((1, CONV_CH)), _const((1, CONV_CH)), _const((1, CONV_CH))],
        out_specs=[pl.BlockSpec((tc, CONV_CH), lambda i: (i, 0))] * 2,
        scratch_shapes=[pltpu.VMEM((tc + CONV_HALO, CONV_CH), F32),
                        pltpu.VMEM((SUBLANES - 1, tc + CONV_HALO - SUBLANES, CONV_CH), F32)],
        compiler_params=_cparams(dimension_semantics=("arbitrary",)),
    )(u_conv, conv_w, conv_b, ln_g, ln_b)


def _head_masks():
    lane = lax.broadcasted_iota(jnp.int32, (1, LANES), 1)
    return lane < SB_HEAD_DIM


def _split_heads(t, first):
    z = jnp.zeros_like(t)
    return jnp.where(first, t, z), jnp.where(first, z, t)


def _head_sum(t, first):
    a = jnp.sum(jnp.where(first, t, 0.0), axis=-1, keepdims=True)
    b = jnp.sum(jnp.where(first, 0.0, t), axis=-1, keepdims=True)
    return a, b


def _attn_fwd(qkv, g_attn):
    S = qkv.shape[0]
    Q = min(ATTN_BLOCK, S)
    nq = S // Q
    assert nq <= LANES
    ntiles = nq * (nq + 1) // 2
    assert ATTN_UNROLL % 2 == 0 and ntiles >= 4 + ATTN_UNROLL
    npair = SB_WIDTH // LANES
    tiles = [(i, j) for i in range(nq) for j in range(i, -1, -1)]

    def body(q_ref, k_ref, v_ref, g_ref, o_ref, ao_ref, cl_ref, z_buf, l_buf, z2_buf, a_buf, c_buf, mask_buf):
        first = _head_masks()
        lane = lax.broadcasted_iota(jnp.int32, (1, LANES), 1)
        row = lax.broadcasted_iota(jnp.int32, (Q, Q), 0)
        col = lax.broadcasted_iota(jnp.int32, (Q, Q), 1)
        tri = jnp.where(row >= col, -1.0, 0.0).astype(BF16)
        heads = range(2)
        strips = [slice(r0, r0 + ATTN_STRIP) for r0 in range(0, Q, ATTN_STRIP)]
        rows = lambda j: pl.ds(pl.multiple_of(j * Q, Q), Q)
        wide = lambda t: jnp.tile(t, (1, Q // LANES))
        as_int = lambda t: int(t) if isinstance(t, (bool, int)) else t.astype(jnp.int32)

        keep = col < row
        mask_buf[0, 0] = jnp.ones((Q, Q), F32)
        mask_buf[0, 1] = jnp.zeros((Q, Q), F32)
        mask_buf[1, 0] = jnp.where(keep, 1.0, 0.0)
        mask_buf[1, 1] = jnp.where(keep, 0.0, MASKED)
        o_ref[...] = jnp.zeros_like(o_ref)

        def scores(t, slot):
            i, j = t
            qh = _split_heads(q_ref[rows(i), :], first)
            kb = k_ref[rows(j), :]
            for h in heads:
                z_buf[slot, h] = _dot_nt(qh[h], kb)

        def logs(t, slot):
            i, j = t
            diag = as_int(i == j)
            for h in heads:
                for r in strips:
                    z2 = jnp.minimum(z_buf[slot, h, r, :] * LOG2E, Z2_MAX)
                    nl = jnp.log2(1.0 + jnp.exp2(z2))
                    l_buf[slot, h, r, :] = (nl * mask_buf[diag, 0, r, :]).astype(BF16)
                    z2_buf[slot, h, r, :] = z2 + mask_buf[diag, 1, r, :]

        def sums(slot):
            return tuple(_dot(l_buf[slot, h], tri) for h in heads)

        def weights(t, slot, sm):
            i, j = t
            running = jnp.where(j == i, 0.0, 1.0)
            for h in heads:
                before = c_buf[h] * running
                for r in strips:
                    a_buf[slot, h, r, :] = jnp.exp2(z2_buf[slot, h, r, :] + sm[h][r] + wide(before[r])).astype(BF16)
                hl = slice(h * LANES, (h + 1) * LANES)
                cl_ref[rows(i), hl] = jnp.where(lane == j, before, cl_ref[rows(i), hl] * running)
                c_buf[h] = before + jnp.broadcast_to(sm[h][:, 0:1], (Q, LANES))

        def values(t, slot):
            i, j = t
            vh = _split_heads(v_ref[rows(j), :], first)
            o_ref[rows(i), :] += _dot(a_buf[slot, 0], vh[0]) + _dot(a_buf[slot, 1], vh[1])

        def iteration(t, p):
            ta, tb, tc, td = t
            if ta is not None:
                scores(ta, p)
            if tc is not None:
                sm = sums(p)
            if td is not None:
                values(td, 1 - p)
            if tb is not None:
                logs(tb, 1 - p)
            if tc is not None:
                weights(tc, p, sm)

        def window(n):
            return tuple(tiles[n - k] if 0 <= n - k < ntiles else None for k in range(4))

        def following(t):
            i, j = t
            last = j == 0
            return jnp.where(last, i + 1, i), jnp.where(last, i + 1, j - 1)

        peeled = 4 + (ntiles - 4) % ATTN_UNROLL

        def unrolled_iterations(_, t):
            for n in range(peeled, peeled + ATTN_UNROLL):
                iteration(t, n % 2)
                t = (following(t[0]),) + t[:3]
            return t

        c_buf[...] = jnp.zeros_like(c_buf)
        for n in range(peeled):
            iteration(window(n), n % 2)
        first_window = tuple((jnp.int32(i), jnp.int32(j)) for i, j in window(peeled))
        lax.fori_loop(0, (ntiles - peeled) // ATTN_UNROLL, unrolled_iterations, first_window)
        for n in range(ntiles, ntiles + 3):
            iteration(window(n), n % 2)

        def head_norm(b, carry):
            o = o_ref[rows(b), :]
            sa, sb = _head_sum(o * o, first)
            r = jnp.where(first, lax.rsqrt(sa * (1.0 / SB_HEAD_DIM) + EPS), lax.rsqrt(sb * (1.0 / SB_HEAD_DIM) + EPS))
            ao_ref[rows(b), :] = (o * r * g_ref[...]).astype(BF16)
            return carry

        lax.fori_loop(0, nq, head_norm, 0)

    col_block = lambda off: pl.BlockSpec((S, LANES), lambda p: (0, off + p), pipeline_mode=pl.Buffered(1))
    out_block = lambda n: pl.BlockSpec((S, n), lambda p: (0, p), pipeline_mode=pl.Buffered(1))
    return pl.pallas_call(
        body, name="attn_fwd", grid=(npair,),
        out_shape=[jax.ShapeDtypeStruct((S, SB_WIDTH), F32), jax.ShapeDtypeStruct((S, SB_WIDTH), BF16),
                   jax.ShapeDtypeStruct((S, 2 * SB_WIDTH), F32)],
        in_specs=[col_block(0), col_block(npair), col_block(2 * npair), pl.BlockSpec((1, LANES), lambda p: (0, p))],
        out_specs=[out_block(LANES), out_block(LANES), out_block(2 * LANES)],
        scratch_shapes=[pltpu.VMEM((2, 2, Q, Q), F32), pltpu.VMEM((2, 2, Q, Q), BF16), pltpu.VMEM((2, 2, Q, Q), F32),
                        pltpu.VMEM((2, 2, Q, Q), BF16), pltpu.VMEM((2, Q, LANES), F32), pltpu.VMEM((2, 2, Q, Q), F32)],
        compiler_params=_cparams(dimension_semantics=("arbitrary",)),
    )(qkv, qkv, qkv, g_attn)


def _out_proj(conv_out, attn_out, w_out, x, g_post_mix, g_pre_ffn):
    S = x.shape[0]
    tm = min(TOKEN_TILE, S)

    def body(co_ref, ao_ref, w_ref, x_ref, g1_ref, g2_ref, y_ref, h1_ref, fin_ref):
        y = _dot(co_ref[...], w_ref[:CONV_CH, :]) + _dot(ao_ref[...], w_ref[CONV_CH:, :])
        h1 = x_ref[...] + y * _rms_r(y) * g1_ref[...]
        y_ref[...] = y
        h1_ref[...] = h1
        fin_ref[...] = (h1 * _rms_r(h1) * g2_ref[...]).astype(BF16)

    row = lambda n: pl.BlockSpec((tm, n), lambda i: (i, 0))
    return pl.pallas_call(
        body, name="out_proj", grid=(S // tm,),
        out_shape=[jax.ShapeDtypeStruct((S, D_MODEL), F32), jax.ShapeDtypeStruct((S, D_MODEL), F32),
                   jax.ShapeDtypeStruct((S, D_MODEL), BF16)],
        in_specs=[row(CONV_CH), row(SB_WIDTH), _resident(w_out.shape), row(D_MODEL), _const((1, D_MODEL)),
                  _const((1, D_MODEL))],
        out_specs=[row(D_MODEL)] * 3,
        compiler_params=_cparams(),
    )(conv_out, attn_out, w_out, x, g_post_mix, g_pre_ffn)


def _ffn_fwd_loss(f_in, w_gate, w_up, w_down, h1, target, g_post_ffn):
    S = f_in.shape[0]
    tm = min(FFN_TILE, S)
    nt = S // tm

    def body(fin_ref, wg_ref, wu_ref, wd_ref, h1_ref, t_ref, g_ref, gt_ref, up_ref, df_ref, dh2_ref, loss_ref, dg_ref,
             sq_acc):
        i = pl.program_id(0)

        @pl.when(i == 0)
        def _():
            sq_acc[...] = jnp.zeros_like(sq_acc)
            dg_ref[...] = jnp.zeros_like(dg_ref)

        fin = fin_ref[...]
        f = jnp.zeros((tm, D_MODEL), F32)
        for c0 in range(0, D_FF, FF_CHUNK):
            cols = slice(c0, c0 + FF_CHUNK)
            gt = _dot(fin, wg_ref[:, cols])
            up = _dot(fin, wu_ref[:, cols])
            gt_ref[:, cols] = gt.astype(BF16)
            up_ref[:, cols] = up.astype(BF16)
            f = f + _dot((gt * jax.nn.sigmoid(gt) * up).astype(BF16), wd_ref[cols, :])
        r = _rms_r(f)
        g = g_ref[...]
        diff = h1_ref[...] + f * r * g - t_ref[...]
        sq_acc[...] += jnp.sum(diff * diff, axis=0, keepdims=True)
        dh2 = diff * (1.0 / D_MODEL)
        dh2_ref[...] = dh2
        dg_ref[...] += jnp.sum(dh2 * f * r, axis=0, keepdims=True)
        df_ref[...] = _rms_bwd(f, r, g, dh2).astype(BF16)

        @pl.when(i == nt - 1)
        def _():
            loss_ref[...] = jnp.broadcast_to((0.5 / D_MODEL) * jnp.sum(sq_acc[...], axis=-1, keepdims=True), (1, LANES))

    row = lambda n: pl.BlockSpec((tm, n), lambda i: (i, 0))
    return pl.pallas_call(
        body, name="ffn_fwd_loss", grid=(nt,),
        out_shape=[jax.ShapeDtypeStruct((S, D_FF), BF16), jax.ShapeDtypeStruct((S, D_FF), BF16),
                   jax.ShapeDtypeStruct((S, D_MODEL), BF16), jax.ShapeDtypeStruct((S, D_MODEL), F32),
                   jax.ShapeDtypeStruct((1, LANES), F32), jax.ShapeDtypeStruct((1, D_MODEL), F32)],
        in_specs=[row(D_MODEL), _resident(w_gate.shape), _resident(w_up.shape), _resident(w_down.shape),
                  row(D_MODEL), row(D_MODEL), _const((1, D_MODEL))],
        out_specs=[row(D_FF), row(D_FF), row(D_MODEL), row(D_MODEL), _const((1, LANES)), _const((1, D_MODEL))],
        scratch_shapes=[pltpu.VMEM((1, D_MODEL), F32)],
        compiler_params=_cparams(dimension_semantics=("arbitrary",)),
    )(f_in, w_gate, w_up, w_down, h1, target, g_post_ffn)


def _ffn_bwd(df, gt, up, w_gate, w_up, w_down):
    S = df.shape[0]
    tm = min(FFN_TILE, S)

    def body(df_ref, gt_ref, up_ref, wg_ref, wu_ref, wd_ref, dgt_ref, dup_ref, act_ref, dfin_ref):
        df = df_ref[...]
        dfin = jnp.zeros((tm, D_MODEL), F32)
        for c0 in range(0, D_FF, FF_CHUNK):
            cols = slice(c0, c0 + FF_CHUNK)
            dact = _dot_nt(df, wd_ref[cols, :])
            gt = gt_ref[:, cols].astype(F32)
            up = up_ref[:, cols].astype(F32)
            s = jax.nn.sigmoid(gt)
            silu = gt * s
            dgt = (dact * up * (s * (1.0 + gt * (1.0 - s)))).astype(BF16)
            dup = (dact * silu).astype(BF16)
            act_ref[:, cols] = (silu * up).astype(BF16)
            dgt_ref[:, cols] = dgt
            dup_ref[:, cols] = dup
            dfin = dfin + _dot_nt(dgt, wg_ref[:, cols]) + _dot_nt(dup, wu_ref[:, cols])
        dfin_ref[...] = dfin

    row = lambda n: pl.BlockSpec((tm, n), lambda i: (i, 0))
    return pl.pallas_call(
        body, name="ffn_bwd", grid=(S // tm,),
        out_shape=[jax.ShapeDtypeStruct((S, D_FF), BF16)] * 3 + [jax.ShapeDtypeStruct((S, D_MODEL), F32)],
        in_specs=[row(D_MODEL), row(D_FF), row(D_FF), _resident(w_gate.shape), _resident(w_up.shape),
                  _resident(w_down.shape)],
        out_specs=[row(D_FF)] * 3 + [row(D_MODEL)],
        compiler_params=_cparams(),
    )(df, gt, up, w_gate, w_up, w_down)


def _matmul_tn(name, x, y, tn):
    S, K = x.shape
    N = y.shape[1]
    ts = min(GRAD_TILE, S)

    def body(x_ref, y_ref, o_ref):
        @pl.when(pl.program_id(1) == 0)
        def _():
            o_ref[...] = jnp.zeros_like(o_ref)

        o_ref[...] += _dot_tn(x_ref[...].astype(BF16), y_ref[...].astype(BF16))

    return pl.pallas_call(
        body, name=name, grid=(N // tn, S // ts),
        out_shape=jax.ShapeDtypeStruct((K, N), F32),
        in_specs=[pl.BlockSpec((ts, K), lambda n, s: (s, 0)), pl.BlockSpec((ts, tn), lambda n, s: (s, n))],
        out_specs=pl.BlockSpec((K, tn), lambda n, s: (0, n)),
        compiler_params=_cparams(dimension_semantics=("arbitrary", "arbitrary")),
    )(x, y)


def _mix_bwd(dfin, h1, y, dh2, g_pre_ffn, g_post_mix, w_out):
    S = dfin.shape[0]
    tm = min(TOKEN_TILE, S)

    def body(dfin_ref, h1_ref, y_ref, dh2_ref, g2_ref, g1_ref, w_ref, dh1_ref, dy_ref, dco_ref, dao_ref, dg2_ref, dg1_ref):
        @pl.when(pl.program_id(0) == 0)
        def _():
            dg2_ref[...] = jnp.zeros_like(dg2_ref)
            dg1_ref[...] = jnp.zeros_like(dg1_ref)

        h1, dfin = h1_ref[...], dfin_ref[...]
        r2 = _rms_r(h1)
        dh1 = dh2_ref[...] + _rms_bwd(h1, r2, g2_ref[...], dfin)
        dg2_ref[...] += jnp.sum(dfin * h1 * r2, axis=0, keepdims=True)
        y = y_ref[...]
        r1 = _rms_r(y)
        dy = _rms_bwd(y, r1, g1_ref[...], dh1).astype(BF16)
        dg1_ref[...] += jnp.sum(dh1 * y * r1, axis=0, keepdims=True)
        dh1_ref[...] = dh1
        dy_ref[...] = dy
        dco_ref[...] = _dot_nt(dy, w_ref[:CONV_CH, :])
        dao_ref[...] = _dot_nt(dy, w_ref[CONV_CH:, :])

    row = lambda n: pl.BlockSpec((tm, n), lambda i: (i, 0))
    return pl.pallas_call(
        body, name="mix_bwd", grid=(S // tm,),
        out_shape=[jax.ShapeDtypeStruct((S, D_MODEL), F32), jax.ShapeDtypeStruct((S, D_MODEL), BF16),
                   jax.ShapeDtypeStruct((S, CONV_CH), F32), jax.ShapeDtypeStruct((S, SB_WIDTH), F32),
                   jax.ShapeDtypeStruct((1, D_MODEL), F32), jax.ShapeDtypeStruct((1, D_MODEL), F32)],
        in_specs=[row(D_MODEL)] * 4 + [_const((1, D_MODEL)), _const((1, D_MODEL)), _resident(w_out.shape)],
        out_specs=[row(D_MODEL), row(D_MODEL), row(CONV_CH), row(SB_WIDTH), _const((1, D_MODEL)), _const((1, D_MODEL))],
        compiler_params=_cparams(dimension_semantics=("arbitrary",)),
    )(dfin, h1, y, dh2, g_pre_ffn, g_post_mix, w_out)


def _attn_norm_bwd(o, dao, g_attn):
    S = o.shape[0]
    tm = min(TOKEN_TILE, S)
    inv_dh = 1.0 / SB_HEAD_DIM

    def body(o_ref, dao_ref, g_ref, do_ref, dg_ref):
        @pl.when(pl.program_id(0) == 0)
        def _():
            dg_ref[...] = jnp.zeros_like(dg_ref)

        first = _head_masks()
        for p in range(SB_WIDTH // LANES):
            cols = slice(p * LANES, (p + 1) * LANES)
            o, dao, g = o_ref[:, cols], dao_ref[:, cols], g_ref[:, cols]
            sa, sb = _head_sum(o * o, first)
            r = jnp.where(first, lax.rsqrt(sa * inv_dh + EPS), lax.rsqrt(sb * inv_dh + EPS))
            w = dao * g
            wa, wb = _head_sum(w * o, first)
            do_ref[:, cols] = (r * (w - o * (r * r) * (jnp.where(first, wa, wb) * inv_dh))).astype(BF16)
            dg_ref[:, cols] += jnp.sum(dao * o * r, axis=0, keepdims=True)

    row = pl.BlockSpec((tm, SB_WIDTH), lambda i: (i, 0))
    return pl.pallas_call(
        body, name="attn_norm_bwd", grid=(S // tm,),
        out_shape=[jax.ShapeDtypeStruct((S, SB_WIDTH), BF16), jax.ShapeDtypeStruct((1, SB_WIDTH), F32)],
        in_specs=[row, row, _const((1, SB_WIDTH))], out_specs=[row, _const((1, SB_WIDTH))],
        compiler_params=_cparams(dimension_semantics=("arbitrary",)),
    )(o, dao, g_attn)


def _attn_bwd(qkv, do, cl):
    S = qkv.shape[0]
    Q = min(ATTN_BLOCK, S)
    nq = S // Q
    ntiles = nq * (nq + 1) // 2
    assert ATTN_UNROLL % 2 == 0 and ntiles >= 4 + ATTN_UNROLL
    npair = SB_WIDTH // LANES
    tiles = [(i, j) for i in range(nq) for j in range(i + 1)]

    def body(q_ref, k_ref, v_ref, do_ref, cl_ref, dq_ref, dk_ref, dv_ref,
             z_buf, lb_buf, be_buf, g_buf, l_buf, a_buf, gb_buf, dz_buf, pg_buf, mask_buf):
        first = _head_masks()
        lane = lax.broadcasted_iota(jnp.int32, (1, LANES), 1)
        row = lax.broadcasted_iota(jnp.int32, (Q, Q), 0)
        col = lax.broadcasted_iota(jnp.int32, (Q, Q), 1)
        tri = jnp.where(row > col, -1.0, 0.0).astype(BF16)
        tpi = (row <= col).astype(BF16)
        heads = range(2)
        strips = [slice(r0, r0 + ATTN_STRIP) for r0 in range(0, Q, ATTN_STRIP)]
        rows = lambda j: pl.ds(pl.multiple_of(j * Q, Q), Q)
        wide = lambda t: jnp.tile(t, (1, Q // LANES))
        as_int = lambda t: int(t) if isinstance(t, (bool, int)) else t.astype(jnp.int32)

        keep = col < row
        mask_buf[0, 0] = jnp.ones((Q, Q), F32)
        mask_buf[0, 1] = jnp.zeros((Q, Q), F32)
        mask_buf[1, 0] = jnp.where(keep, 1.0, 0.0)
        mask_buf[1, 1] = jnp.where(keep, 0.0, MASKED)
        dq_ref[...] = jnp.zeros_like(dq_ref)
        dk_ref[...] = jnp.zeros_like(dk_ref)
        dv_ref[...] = jnp.zeros_like(dv_ref)

        def scores(t, slot):
            i, j = t
            qh = _split_heads(q_ref[rows(i), :], first)
            kb = k_ref[rows(j), :]
            for h in heads:
                z_buf[slot, h] = _dot_nt(qh[h], kb)

        def logs(t, slot):
            i, j = t
            diag = as_int(i == j)
            for h in heads:
                for r in strips:
                    z2 = jnp.minimum(z_buf[slot, h, r, :] * LOG2E, Z2_MAX)
                    nl = jnp.log2(1.0 + jnp.exp2(z2))
                    l_buf[slot, h, r, :] = (nl * mask_buf[diag, 0, r, :]).astype(BF16)
                    lb_buf[slot, h, r, :] = (z2 - nl) + mask_buf[diag, 1, r, :]

        def sums(t, slot):
            i, j = t
            doh = _split_heads(do_ref[rows(i), :], first)
            vb = v_ref[rows(j), :]
            return (tuple(_dot(l_buf[slot, h], tri) for h in heads), tuple(_dot_nt(doh[h], vb) for h in heads))

        def weights(t, slot, sm, da):
            i, j = t
            for h in heads:
                c = jnp.sum(jnp.where(lane == j, cl_ref[rows(i), h * LANES:(h + 1) * LANES], 0.0), axis=-1, keepdims=True)
                c = jnp.broadcast_to(c, (Q, LANES))
                for r in strips:
                    lb = lb_buf[slot, h, r, :]
                    a = jnp.exp2(lb + sm[h][r] + wide(c[r]))
                    g = da[h][r] * a
                    a_buf[slot, h, r, :] = a.astype(BF16)
                    be_buf[slot, h, r, :] = jnp.exp2(lb)
                    g_buf[slot, h, r, :] = g
                    gb_buf[slot, h, r, :] = g.astype(BF16)

        def prefix(t, slot):
            i, j = t
            doh = _split_heads(do_ref[rows(i), :], first)
            dv_ref[rows(j), :] += _dot_tn(a_buf[slot, 0], doh[0]) + _dot_tn(a_buf[slot, 1], doh[1])
            return tuple(_dot(gb_buf[slot, h], tpi) for h in heads)

        def dscores(t, slot, pm):
            i, j = t
            for h in heads:
                pg = pg_buf[h] * jnp.where(j == 0, 0.0, 1.0)
                for r in strips:
                    dz = g_buf[slot, h, r, :] - be_buf[slot, h, r, :] * (pm[h][r] + wide(pg[r]))
                    dz_buf[slot, h, r, :] = dz.astype(BF16)
                pg_buf[h] = pg + jnp.broadcast_to(pm[h][:, Q - 1:Q], (Q, LANES))

        def grads(t, slot):
            i, j = t
            qh = _split_heads(q_ref[rows(i), :], first)
            kh = _split_heads(k_ref[rows(j), :], first)
            dq_ref[rows(i), :] += _dot(dz_buf[slot, 0], kh[0]) + _dot(dz_buf[slot, 1], kh[1])
            dk_ref[rows(j), :] += _dot_tn(dz_buf[slot, 0], qh[0]) + _dot_tn(dz_buf[slot, 1], qh[1])

        def iteration(t, p):
            ta, tb, tc, td, te = t
            if ta is not None:
                scores(ta, p)
            if tc is not None:
                sm, da = sums(tc, p)
            if td is not None:
                pm = prefix(td, 1 - p)
            if te is not None:
                grads(te, p)
            if tb is not None:
                logs(tb, 1 - p)
            if tc is not None:
                weights(tc, p, sm, da)
            if td is not None:
                dscores(td, 1 - p, pm)

        def window(n):
            return tuple(tiles[n - k] if 0 <= n - k < ntiles else None for k in range(5))

        def following(t):
            i, j = t
            last = j == i
            return jnp.where(last, i + 1, i), jnp.where(last, 0, j + 1)

        peeled = 4 + (ntiles - 4) % ATTN_UNROLL

        def unrolled_iterations(_, t):
            for n in range(peeled, peeled + ATTN_UNROLL):
                iteration(t, n % 2)
                t = (following(t[0]),) + t[:4]
            return t

        pg_buf[...] = jnp.zeros_like(pg_buf)
        for n in range(peeled):
            iteration(window(n), n % 2)
        first_window = tuple((jnp.int32(i), jnp.int32(j)) for i, j in window(peeled))
        lax.fori_loop(0, (ntiles - peeled) // ATTN_UNROLL, unrolled_iterations, first_window)
        for n in range(ntiles, ntiles + 4):
            iteration(window(n), n % 2)
        dq_ref[...] = dq_ref[...] * (1.0 / math.sqrt(SB_HEAD_DIM))

    col_block = lambda off: pl.BlockSpec((S, LANES), lambda p: (0, off + p), pipeline_mode=pl.Buffered(1))
    return pl.pallas_call(
        body, name="attn_bwd", grid=(npair,),
        out_shape=[jax.ShapeDtypeStruct((S, SB_WIDTH), F32)] * 3,
        in_specs=[col_block(0), col_block(npair), col_block(2 * npair), col_block(0),
                  pl.BlockSpec((S, 2 * LANES), lambda p: (0, p), pipeline_mode=pl.Buffered(1))],
        out_specs=[pl.BlockSpec((S, LANES), lambda p: (0, p), pipeline_mode=pl.Buffered(1))] * 3,
        scratch_shapes=[pltpu.VMEM((2, 2, Q, Q), F32)] * 4 + [pltpu.VMEM((2, 2, Q, Q), BF16)] * 4
        + [pltpu.VMEM((2, Q, LANES), F32), pltpu.VMEM((2, 2, Q, Q), F32)],
        compiler_params=_cparams(dimension_semantics=("arbitrary",)),
    )(qkv, qkv, qkv, do, cl)


def _conv_bwd(u_conv, y_conv, dco, conv_w, ln_g, ln_b):
    S = u_conv.shape[0]
    tc = min(TOKEN_TILE, S)
    nt = S // tc
    per = tc // CONV_HALO
    groups = CONV_CHUNK // 8

    def body(u_ref, halo_ref, y_ref, dco_ref, cw_ref, lg_ref, lb_ref, du_ref, dcw_ref, dsm_ref, glu_ext, dyc_ext, sg_buf,
             dcw_acc, dsm_acc, glu_sh, dyc_sh):
        i = pl.program_id(0)
        ti = nt - 1 - i

        @pl.when(i == 0)
        def _():
            dyc_ext[tc:, :] = jnp.zeros((CONV_HALO, CONV_CH), F32)
            dcw_acc[...] = jnp.zeros_like(dcw_acc)
            dsm_acc[...] = jnp.zeros_like(dsm_acc)

        @pl.when(i > 0)
        def _():
            dyc_ext[tc:, :] = dyc_ext[0:CONV_HALO, :]

        glu_ext[0:CONV_HALO, :] = jnp.where(ti > 0, _glu(halo_ref[...])[2], 0.0)
        val, sg, glu = _glu(u_ref[...])
        glu_ext[CONV_HALO:, :] = glu
        sg_buf[...] = sg
        _shift_copies(glu_ext, glu_sh)

        dcb = jnp.zeros((8, CONV_CH), F32)
        dlg = jnp.zeros((8, CONV_CH), F32)
        dlb = jnp.zeros((8, CONV_CH), F32)
        fold = lambda t: jnp.sum(t.reshape(groups, 8, CONV_CH), axis=0)
        for r0 in range(0, tc, CONV_CHUNK):
            y = y_ref[r0:r0 + CONV_CHUNK, :]
            mu = jnp.mean(y, axis=-1, keepdims=True)
            yc = y - mu
            rstd = lax.rsqrt(jnp.mean(yc * yc, axis=-1, keepdims=True) + EPS)
            yn = yc * rstd
            yl = yn * lg_ref[...] + lb_ref[...]
            s = jax.nn.sigmoid(yl)
            dyl = dco_ref[r0:r0 + CONV_CHUNK, :] * (s * (1.0 + yl * (1.0 - s)))
            dlg = dlg + fold(dyl * yn)
            dlb = dlb + fold(dyl)
            wv = dyl * lg_ref[...]
            dyc = rstd * (wv - jnp.mean(wv, axis=-1, keepdims=True) - yn * jnp.mean(wv * yn, axis=-1, keepdims=True))
            dcb = dcb + fold(dyc)
            dyc_ext[r0:r0 + CONV_CHUNK, :] = dyc
        dsm_acc[0:8, :] += dcb
        dsm_acc[8:16, :] += dlg
        dsm_acc[16:24, :] += dlb
        _shift_copies(dyc_ext, dyc_sh)

        for r0 in range(0, tc, CONV_CHUNK):
            dyc = dyc_ext[r0:r0 + CONV_CHUNK, :]
            dglu = jnp.zeros((CONV_CHUNK, CONV_CH), F32)
            base = r0 + CONV_HALO - (CONV_WIDTH - 1)
            for w in range(CONV_WIDTH):
                back = r0 + (CONV_WIDTH - 1) - w
                dglu = dglu + cw_ref[w:w + 1, :] * _window(dyc_ext, dyc_sh, back, CONV_CHUNK)
                dcw_acc[8 * w:8 * w + 8, :] += fold(dyc * _window(glu_ext, glu_sh, base + w, CONV_CHUNK))
            sg = sg_buf[r0:r0 + CONV_CHUNK, :]
            v = u_ref[r0:r0 + CONV_CHUNK, :CONV_CH]
            du_ref[r0:r0 + CONV_CHUNK, :CONV_CH] = (dglu * sg).astype(BF16)
            du_ref[r0:r0 + CONV_CHUNK, CONV_CH:] = (dglu * v * sg * (1.0 - sg)).astype(BF16)

        @pl.when(i == nt - 1)
        def _():
            for w in range(CONV_WIDTH):
                dcw_ref[w:w + 1, :] = jnp.sum(dcw_acc[8 * w:8 * w + 8, :], axis=0, keepdims=True)
            dcw_ref[CONV_WIDTH:, :] = jnp.zeros((CONV_HALO - CONV_WIDTH, CONV_CH), F32)
            for k in range(3):
                dsm_ref[k:k + 1, :] = jnp.sum(dsm_acc[8 * k:8 * k + 8, :], axis=0, keepdims=True)
            dsm_ref[3:, :] = jnp.zeros((5, CONV_CH), F32)

    return pl.pallas_call(
        body, name="conv_bwd", grid=(nt,),
        out_shape=[jax.ShapeDtypeStruct((S, 2 * CONV_CH), BF16), jax.ShapeDtypeStruct((CONV_HALO, CONV_CH), F32),
                   jax.ShapeDtypeStruct((8, CONV_CH), F32)],
        in_specs=[pl.BlockSpec((tc, 2 * CONV_CH), lambda i: (nt - 1 - i, 0)),
                  pl.BlockSpec((CONV_HALO, 2 * CONV_CH), lambda i: (jnp.maximum((nt - 1 - i) * per - 1, 0), 0)),
                  pl.BlockSpec((tc, CONV_CH), lambda i: (nt - 1 - i, 0)), pl.BlockSpec((tc, CONV_CH), lambda i: (nt - 1 - i, 0)),
                  _const((CONV_HALO, CONV_CH)), _const((1, CONV_CH)), _const((1, CONV_CH))],
        out_specs=[pl.BlockSpec((tc, 2 * CONV_CH), lambda i: (nt - 1 - i, 0)), _const((CONV_HALO, CONV_CH)),
                   _const((8, CONV_CH))],
        scratch_shapes=[pltpu.VMEM((tc + CONV_HALO, CONV_CH), F32), pltpu.VMEM((tc + CONV_HALO, CONV_CH), F32),
                        pltpu.VMEM((tc, CONV_CH), F32), pltpu.VMEM((8 * CONV_HALO, CONV_CH), F32),
                        pltpu.VMEM((24, CONV_CH), F32)]
        + [pltpu.VMEM((SUBLANES - 1, tc + CONV_HALO - SUBLANES, CONV_CH), F32)] * 2,
        compiler_params=_cparams(dimension_semantics=("arbitrary",)),
    )(u_conv, u_conv, y_conv, dco, conv_w, ln_g, ln_b)


def _in_proj_bwd(du_conv, dq, dk, dv, w_in, x, g, dh1):
    S = x.shape[0]
    tm = min(TOKEN_TILE, S)
    nconv = 2 * CONV_CH

    def body(duc_ref, dq_ref, dk_ref, dv_ref, w_ref, x_ref, g_ref, dh1_ref, dx_ref, dg_ref):
        @pl.when(pl.program_id(0) == 0)
        def _():
            dg_ref[...] = jnp.zeros_like(dg_ref)

        da = _dot_nt(duc_ref[...], w_ref[:, :nconv])
        for n, ref in enumerate((dq_ref, dk_ref, dv_ref)):
            c0 = nconv + n * SB_WIDTH
            da = da + _dot_nt(ref[...].astype(BF16), w_ref[:, c0:c0 + SB_WIDTH])
        xf = x_ref[...]
        r = _rms_r(xf)
        dx_ref[...] = dh1_ref[...] + _rms_bwd(xf, r, g_ref[...], da)
        dg_ref[...] += jnp.sum(da * xf * r, axis=0, keepdims=True)

    row = lambda n: pl.BlockSpec((tm, n), lambda i: (i, 0))
    return pl.pallas_call(
        body, name="in_proj_bwd", grid=(S // tm,),
        out_shape=[jax.ShapeDtypeStruct((S, D_MODEL), F32), jax.ShapeDtypeStruct((1, D_MODEL), F32)],
        in_specs=[row(nconv), row(SB_WIDTH), row(SB_WIDTH), row(SB_WIDTH), _resident(w_in.shape), row(D_MODEL),
                  _const((1, D_MODEL)), row(D_MODEL)],
        out_specs=[row(D_MODEL), _const((1, D_MODEL))],
        compiler_params=_cparams(dimension_semantics=("arbitrary",)),
    )(du_conv, dq, dk, dv, w_in, x, g, dh1)


def _layer_grads(xs, target, g_pre_mix, w_in_f, conv_w_f, conv_b, conv_ln_g, conv_ln_b, attn_g, g_post_mix, g_pre_ffn,
                 g_post_ffn, late_weights, send_grads):
    a, u_conv, qkv = _in_proj(xs, g_pre_mix, w_in_f)
    conv_out, y_conv = _conv_fwd(u_conv, conv_w_f, conv_b, conv_ln_g, conv_ln_b)
    o, attn_out, cl = _attn_fwd(qkv, attn_g)
    w_out_f, w_gate_f, w_up_f, w_down_f = late_weights(attn_out)
    y, h1, f_in = _out_proj(conv_out, attn_out, w_out_f, xs, g_post_mix, g_pre_ffn)
    gt, up, df, dh2, loss_part, d_g_post_ffn = _ffn_fwd_loss(f_in, w_gate_f, w_up_f, w_down_f, h1, target, g_post_ffn)

    dgt, dup, act, dfin = _ffn_bwd(df, gt, up, w_gate_f, w_up_f, w_down_f)
    d_w_down = _matmul_tn("grad_w_down", act, df, 512)
    d_w_gate = _matmul_tn("grad_w_gate", f_in, dgt, FF_CHUNK)
    d_w_up = _matmul_tn("grad_w_up", f_in, dup, FF_CHUNK)
    sent = send_grads("ffn", (d_w_gate, d_w_up, d_w_down))
    dh1, dy, dco, dao, d_g_pre_ffn, d_g_post_mix = _mix_bwd(dfin, h1, y, dh2, g_pre_ffn + sent, g_post_mix, w_out_f)
    d_w_out = jnp.concatenate([_matmul_tn("grad_w_out_conv", conv_out, dy, D_MODEL),
                               _matmul_tn("grad_w_out_attn", attn_out, dy, D_MODEL)], axis=0)
    sent = send_grads("w_out", (d_w_out,))
    do, d_attn_g = _attn_norm_bwd(o, dao, attn_g + sent)
    dq, dk, dv = _attn_bwd(qkv, do, cl)
    du_conv, d_conv_w, d_conv_small = _conv_bwd(u_conv, y_conv, dco, conv_w_f, conv_ln_g, conv_ln_b)
    d_w_in = jnp.concatenate([_matmul_tn("grad_w_in_conv", a, du_conv, 2 * CONV_CH),
                              _matmul_tn("grad_w_in_q", a, dq, SB_WIDTH), _matmul_tn("grad_w_in_k", a, dk, SB_WIDTH),
                              _matmul_tn("grad_w_in_v", a, dv, SB_WIDTH)], axis=1)
    sent = send_grads("w_in", (d_w_in,))
    grad_x, d_g_pre_mix = _in_proj_bwd(du_conv, dq, dk, dv, w_in_f, xs, g_pre_mix + sent, dh1)
    return (loss_part, grad_x, d_conv_w, d_conv_small, d_attn_g, d_g_pre_mix, d_g_post_mix, d_g_pre_ffn, d_g_post_ffn)


def _cols_to_blocks(w):
    K, N = w.shape
    return jnp.transpose(w.reshape(K, N_DEV, N // N_DEV), (1, 0, 2))


def _blocks_to_cols(blocks):
    n_dev, K, n = blocks.shape
    return jnp.transpose(blocks, (1, 0, 2)).reshape(K, n_dev * n)


def kernel(x, g_pre_mix, w_in, conv_w, conv_b, conv_ln_g, conv_ln_b, attn_norm_g, w_out, g_post_mix, g_pre_ffn, w_gate, w_up, w_down, g_post_ffn, loss_target, m_g_pre_mix, m_w_in, m_conv_w, m_conv_b, m_conv_ln_g, m_conv_ln_b, m_attn_norm_g, m_w_out, m_g_post_mix, m_g_pre_ffn, m_w_gate, m_w_up, m_w_down, m_g_post_ffn, v_g_pre_mix, v_w_in, v_conv_w, v_conv_b, v_conv_ln_g, v_conv_ln_b, v_attn_norm_g, v_w_out, v_g_post_mix, v_g_pre_ffn, v_w_gate, v_w_up, v_w_down, v_g_post_ffn):
    xs = x[0]
    target = loss_target[0]
    S = xs.shape[0]
    me = 4 * lax.axis_index("x") + 2 * lax.axis_index("y") + lax.axis_index("c")
    cw_shard = conv_w.reshape(CONV_WIDTH, CONV_CH // N_DEV)
    attn_g = attn_norm_g.reshape(1, SB_WIDTH)

    gathered = _all_gather([w_in[0].astype(BF16), cw_shard])
    w_in_f = _blocks_to_cols(gathered[0])
    conv_w_f = jnp.pad(_blocks_to_cols(gathered[1]), ((0, CONV_HALO - CONV_WIDTH), (0, 0)))
    gathered_zero = gathered[2][0:1, 0:1].astype(BF16)
    late = [w_out[0].astype(BF16) + gathered_zero, w_gate[0].astype(BF16), w_up[0].astype(BF16), w_down[0].astype(BF16)]
    late_started = _exchange_start("all_gather_late_start", late, scatter=False)

    def late_weights(after):
        lands = _exchange_wait("all_gather_late_wait", late_started, False, after)
        wo, wg, wu, wd = [lax.dynamic_update_index_in_dim(land, own, me, 0) for land, own in zip(lands, late)]
        return wo.reshape(D_MODEL, D_MODEL), _blocks_to_cols(wg), _blocks_to_cols(wu), wd.reshape(D_FF, D_MODEL)

    started = {}

    def send_grads(name, grads):
        blocks = [g.reshape(N_DEV, g.shape[0] // N_DEV, g.shape[1]) if g.shape[1] == D_MODEL else _cols_to_blocks(g)
                  for g in grads]
        payload = BF16 if name == "w_in" else F32
        sent = _exchange_start("reduce_scatter_" + name + "_start", [b.astype(payload) for b in blocks], scatter=True)
        started[name] = (sent, blocks)
        return sent[-1][0:1, 0:1]

    (loss_part, grad_x, d_conv_w, d_conv_small, d_attn_g, d_g_pre_mix, d_g_post_mix, d_g_pre_ffn,
     d_g_post_ffn) = _layer_grads(
        xs, target, g_pre_mix + late_started[-1][0:1, 0:1], w_in_f, conv_w_f, conv_b, conv_ln_g, conv_ln_b, attn_g,
        g_post_mix, g_pre_ffn, g_post_ffn, late_weights, send_grads)

    def reduced(name, after, shards):
        st, blocks = started[name]
        lands = _exchange_wait("reduce_scatter_" + name + "_wait", st, True, after)
        return [_sum_adamw("adamw_" + wn, land, lax.dynamic_index_in_dim(blk, me, 0, keepdims=False), w[0], m[0], v[0])
                for land, blk, (wn, w, m, v) in zip(lands, blocks, shards)]

    two = lambda t: t.reshape(2, CONV_CH)
    small_g = jnp.concatenate([
        d_conv_w,
        d_conv_small[0:3],
        d_attn_g,
        two(d_g_pre_mix), two(d_g_post_mix), two(d_g_pre_ffn), two(d_g_post_ffn),
        jnp.broadcast_to(loss_part[0:1, 0:1], (1, CONV_CH)),
        jnp.zeros((3, CONV_CH), F32)], axis=0)
    small_g = _all_reduce_small(small_g)
    loss = small_g[44, 0]
    g_conv_w = lax.dynamic_slice(small_g, (0, me * (CONV_CH // N_DEV)), (CONV_WIDTH, CONV_CH // N_DEV))
    pack = lambda cb, lg, lb, ag, g1, g2, g3, g4: jnp.concatenate(
        [cb, lg, lb, ag.reshape(1, SB_WIDTH), two(g1), two(g2), two(g3), two(g4), jnp.zeros((4, CONV_CH), F32)], axis=0)
    sm_g = small_g[CONV_HALO:]
    sm_delta, sm_m, sm_v = _adamw_small(
        "adamw_small",
        pack(conv_b, conv_ln_g, conv_ln_b, attn_norm_g, g_pre_mix, g_post_mix, g_pre_ffn, g_post_ffn), sm_g,
        pack(m_conv_b, m_conv_ln_g, m_conv_ln_b, m_attn_norm_g, m_g_pre_mix, m_g_post_mix, m_g_pre_ffn, m_g_post_ffn),
        pack(v_conv_b, v_conv_ln_g, v_conv_ln_b, v_attn_norm_g, v_g_pre_mix, v_g_post_mix, v_g_pre_ffn, v_g_post_ffn))
    cw_delta, cw_m, cw_v = _adamw_small("adamw_conv_w", cw_shard, g_conv_w,
                                        m_conv_w.reshape(cw_shard.shape), v_conv_w.reshape(cw_shard.shape))

    ffn = reduced("ffn", grad_x, [("w_gate", w_gate, m_w_gate, v_w_gate), ("w_up", w_up, m_w_up, v_w_up),
                                  ("w_down", w_down, m_w_down, v_w_down)])
    big = {"w_gate": ffn[0], "w_up": ffn[1], "w_down": ffn[2],
           "w_out": reduced("w_out", ffn[2][0], [("w_out", w_out, m_w_out, v_w_out)])[0]}
    big["w_in"] = reduced("w_in", big["w_out"][0], [("w_in", w_in, m_w_in, v_w_in)])[0]

    def unpack(t):
        return {"conv_b": t[0:1], "conv_ln_g": t[1:2], "conv_ln_b": t[2:3], "attn_norm_g": t[3:4].reshape(1, SB_HEADS, SB_HEAD_DIM),
                "g_pre_mix": t[4:6].reshape(1, D_MODEL), "g_post_mix": t[6:8].reshape(1, D_MODEL),
                "g_pre_ffn": t[8:10].reshape(1, D_MODEL), "g_post_ffn": t[10:12].reshape(1, D_MODEL)}

    names = ["g_pre_mix", "w_in", "conv_w", "conv_b", "conv_ln_g", "conv_ln_b", "attn_norm_g", "w_out", "g_post_mix",
             "g_pre_ffn", "w_gate", "w_up", "w_down", "g_post_ffn"]
    kinds = []
    for idx, small in enumerate((sm_g, sm_delta, sm_m, sm_v)):
        d = unpack(small)
        d["conv_w"] = (g_conv_w, cw_delta, cw_m, cw_v)[idx].reshape(1, CONV_WIDTH, 1, CONV_CH // N_DEV)
        for n in big:
            d[n] = big[n][idx][None]
        kinds.append([d[n] for n in names])

    return (loss, grad_x[None], *kinds[0], *kinds[1], *kinds[2], *kinds[3])
```

```python
import functools
import math

import jax
import jax.numpy as jnp
from jax import lax
from jax.experimental import pallas as pl
from jax.experimental.pallas import tpu as pltpu

F32 = jnp.float32
BF16 = jnp.bfloat16
MESH = pl.DeviceIdType.MESH

N_DEV = 8
D_MODEL = 1024
CONV_CH = 512
CONV_WIDTH = 31
SB_HEADS = 8
SB_HEAD_DIM = 64
SB_WIDTH = SB_HEADS * SB_HEAD_DIM
D_FF = 2816
EPS = 1e-6
LOG2E = 1.4426950408889634
Z2_MAX = 100.0
MASKED = -1e30
ADAM_LR = 0.001
ADAM_B1 = 0.9
ADAM_B2 = 0.999
ADAM_EPS = 1e-08
ADAM_WD = 0.01
ADAM_STEP = 10

SUBLANES = 8
LANES = 128
VMEM_LIMIT = 56 * 1024 * 1024
TOKEN_TILE = 512
GRAD_TILE = 1024
FFN_TILE = 256
ATTN_FWD_UNROLL = 8
ATTN_BWD_UNROLL = 4
ATTN_STRIP = 32
ATTN_BLOCK = 256
CONV_HALO = 32
CONV_CHUNK = 64
FF_CHUNK = D_FF // 2


def _cparams(**kw):
    return pltpu.CompilerParams(vmem_limit_bytes=VMEM_LIMIT, **kw)


def _resident(shape):
    return pl.BlockSpec(shape, lambda *_: (0,) * len(shape), pipeline_mode=pl.Buffered(1))


def _const(shape):
    return pl.BlockSpec(shape, lambda *_: (0,) * len(shape))


def _rms_r(xf):
    return lax.rsqrt(jnp.mean(xf * xf, axis=-1, keepdims=True) + EPS)


def _rms_bwd(xf, r, g, dout):
    w = dout * g
    return r * (w - xf * (r * r) * jnp.mean(w * xf, axis=-1, keepdims=True))


def _dot(a, b):
    return jnp.dot(a, b, preferred_element_type=F32)


def _dot_nt(a, b):
    return lax.dot_general(a, b, (((1,), (1,)), ((), ())), preferred_element_type=F32)


def _dot_tn(a, b):
    return lax.dot_general(a, b, (((0,), (0,)), ((), ())), preferred_element_type=F32)


def _peer(x, y, c, k):
    px = 1 - x if (k >> 2) & 1 else x
    py = 1 - y if (k >> 1) & 1 else y
    pc = 1 - c if k & 1 else c
    return (px, py, pc), 4 * px + 2 * py + pc


def _all_gather(shards):
    n = len(shards)

    def body(*refs):
        ins, outs, done = refs[:n], refs[n:2 * n], refs[2 * n]
        send_sems, recv_sems, local_sems = refs[2 * n + 1:]
        x, y, c = lax.axis_index("x"), lax.axis_index("y"), lax.axis_index("c")
        me, sibling = (x, y, c), (x, y, 1 - c)
        chips = [(1 - x, y), (x, 1 - y), (1 - x, 1 - y)]
        number = lambda d: 4 * d[0] + 2 * d[1] + d[2]

        def copy(a, k, block, to, src=None):
            rows = outs[a].at[number(block)]
            return pltpu.make_async_remote_copy(
                src_ref=rows if src is None else src, dst_ref=rows, send_sem=send_sems.at[a * (N_DEV - 1) + k],
                recv_sem=recv_sems.at[a * (N_DEV - 1) + k], device_id=to, device_id_type=MESH)

        copies = [pltpu.make_async_copy(ins[a], outs[a].at[number(me)], local_sems.at[a]) for a in range(n)]
        for mine in copies:
            mine.start()
        sent = [copy(a, 0, me, sibling, src=ins[a]) for a in range(n)]
        sent += [copy(a, 1 + j, me, (*chip, c), src=ins[a]) for j, chip in enumerate(chips) for a in range(n)]
        for cp in sent:
            cp.start()
        for j, chip in enumerate(chips):
            for a in range(n):
                copy(a, 1 + j, (*chip, c), me).wait_recv()
                passed = copy(a, 4 + j, (*chip, c), sibling)
                passed.start()
                sent.append(passed)
        for a in range(n):
            copy(a, 0, sibling, me).wait_recv()
            for j, chip in enumerate(chips):
                copy(a, 4 + j, (*chip, 1 - c), me).wait_recv()
        for cp in sent:
            cp.wait_send()
        for mine in copies:
            mine.wait()
        done[...] = jnp.zeros_like(done)

    any_spec = pl.BlockSpec(memory_space=pl.ANY)
    return pl.pallas_call(
        body, name="all_gather_weights",
        out_shape=[jax.ShapeDtypeStruct((N_DEV,) + s.shape, s.dtype) for s in shards] + [jax.ShapeDtypeStruct((8, LANES), F32)],
        in_specs=[any_spec] * n, out_specs=[any_spec] * n + [pl.BlockSpec(memory_space=pltpu.VMEM)],
        scratch_shapes=[pltpu.SemaphoreType.DMA((n * (N_DEV - 1),)), pltpu.SemaphoreType.DMA((n * (N_DEV - 1),)),
                        pltpu.SemaphoreType.DMA((n,))],
        compiler_params=pltpu.CompilerParams(has_side_effects=True),
    )(*shards)


def _adamw(w, g, m, v):
    m = ADAM_B1 * m + (1.0 - ADAM_B1) * g
    v = ADAM_B2 * v + (1.0 - ADAM_B2) * (g * g)
    m_hat = m / (1.0 - ADAM_B1 ** ADAM_STEP)
    v_hat = v / (1.0 - ADAM_B2 ** ADAM_STEP)
    delta = -ADAM_LR * (m_hat / (jnp.sqrt(v_hat) + ADAM_EPS) + ADAM_WD * w)
    return delta, m, v


def _exchange_and_sum(src_block, recv_ref, send_sems, recv_sems, local_sem):
    x, y, c = lax.axis_index("x"), lax.axis_index("y"), lax.axis_index("c")
    me = 4 * x + 2 * y + c
    mine = pltpu.make_async_copy(src_block(me), recv_ref.at[me], local_sem)
    mine.start()
    for k in range(1, N_DEV):
        peer, peer_block = _peer(x, y, c, k)
        pltpu.make_async_remote_copy(
            src_ref=src_block(peer_block), dst_ref=recv_ref.at[me], send_sem=send_sems.at[k - 1],
            recv_sem=recv_sems.at[k - 1], device_id=peer, device_id_type=MESH).start()
    for k in range(1, N_DEV):
        peer, peer_block = _peer(x, y, c, k)
        arrived = pltpu.make_async_remote_copy(
            src_ref=src_block(peer_block), dst_ref=recv_ref.at[peer_block], send_sem=send_sems.at[k - 1],
            recv_sem=recv_sems.at[k - 1], device_id=peer, device_id_type=MESH)
        arrived.wait_send()
        arrived.wait_recv()
    mine.wait()


HBM_SPEC = pl.BlockSpec(memory_space=pltpu.HBM)
SEM_SPEC = pl.BlockSpec(memory_space=pltpu.SEMAPHORE)
DATAFLOW = pltpu.SideEffectType.DATAFLOW_SIDE_EFFECTING


def _exchange_copies(srcs, lands, send_sems, recv_sems, scatter, wait):
    x, y, c = lax.axis_index("x"), lax.axis_index("y"), lax.axis_index("c")
    me = 4 * x + 2 * y + c
    for k in range(1, N_DEV):
        peer, peer_block = _peer(x, y, c, k)
        for a in range(len(srcs)):
            s = a * (N_DEV - 1) + k - 1
            src = srcs[a].at[peer_block] if scatter else srcs[a]
            copy = pltpu.make_async_remote_copy(
                src_ref=src, dst_ref=lands[a].at[peer_block if wait else me], send_sem=send_sems.at[s],
                recv_sem=recv_sems.at[s], device_id=peer, device_id_type=MESH)
            if wait:
                copy.wait_send()
                copy.wait_recv()
            else:
                copy.start()


def _exchange_start(name, arrays, scatter):
    n = len(arrays)
    land_shapes = [a.shape if scatter else (N_DEV,) + a.shape for a in arrays]

    def body(*refs):
        _exchange_copies(refs[:n], refs[n:2 * n], refs[2 * n], refs[2 * n + 1], scatter, wait=False)
        refs[-1][...] = jnp.zeros_like(refs[-1])

    sems = pltpu.SemaphoreType.DMA((n * (N_DEV - 1),))
    hbm = lambda t: pltpu.with_memory_space_constraint(t, pltpu.HBM)
    return pl.pallas_call(
        body, name=name,
        out_shape=(sems, sems, *[pltpu.HBM(a.shape, a.dtype) for a in arrays],
                   *[pltpu.HBM(ls, a.dtype) for ls, a in zip(land_shapes, arrays)], jax.ShapeDtypeStruct((8, LANES), F32)),
        in_specs=[HBM_SPEC] * (2 * n),
        out_specs=(SEM_SPEC, SEM_SPEC, *[HBM_SPEC] * (2 * n), pl.BlockSpec(memory_space=pltpu.VMEM)),
        input_output_aliases={a: 2 + a for a in range(2 * n)},
        compiler_params=pltpu.CompilerParams(has_side_effects=DATAFLOW),
    )(*[hbm(a) for a in arrays], *[hbm(lax.empty(ls, a.dtype)) for ls, a in zip(land_shapes, arrays)])


def _exchange_wait(name, started, scatter, after):
    n = (len(started) - 3) // 2
    send_sems, recv_sems = started[0], started[1]
    arrays, lands = started[2:2 + n], started[2 + n:2 + 2 * n]

    def body(*refs):
        _exchange_copies(refs[:n], refs[n:2 * n], refs[2 * n], refs[2 * n + 1], scatter, wait=True)

    return pl.pallas_call(
        body, name=name,
        out_shape=[pltpu.HBM(t.shape, t.dtype) for t in (*arrays, *lands)],
        in_specs=[HBM_SPEC] * (2 * n) + [SEM_SPEC, SEM_SPEC, pl.BlockSpec(memory_space=pl.ANY)],
        out_specs=[HBM_SPEC] * (2 * n),
        input_output_aliases={a: a for a in range(2 * n)},
        compiler_params=pltpu.CompilerParams(has_side_effects=DATAFLOW),
    )(*arrays, *lands, send_sems, recv_sems, after)[n:]


def _sum_adamw(name, land, own, w, m, v):
    _, M, N = land.shape
    rows = math.gcd(M, 128)

    def body(land_ref, own_ref, w_ref, m_ref, v_ref, grad_ref, delta_ref, nm_ref, nv_ref):
        x, y, c = lax.axis_index("x"), lax.axis_index("y"), lax.axis_index("c")
        g = own_ref[...]
        for k in range(1, N_DEV):
            g = g + land_ref[_peer(x, y, c, k)[1]].astype(F32)
        delta, nm, nv = _adamw(w_ref[...], g, m_ref[...], v_ref[...])
        grad_ref[...] = g
        delta_ref[...] = delta
        nm_ref[...] = nm
        nv_ref[...] = nv

    row = pl.BlockSpec((rows, N), lambda i: (i, 0))
    return pl.pallas_call(
        body, name=name, grid=(M // rows,), out_shape=[jax.ShapeDtypeStruct((M, N), F32)] * 4,
        in_specs=[pl.BlockSpec((N_DEV, rows, N), lambda i: (0, i, 0)), row, row, row, row], out_specs=[row] * 4,
        compiler_params=_cparams(),
    )(land, own, w, m, v)


def _all_reduce_small(g):
    R, C = g.shape

    def body(g_ref, out_ref, recv_ref, send_sems, recv_sems, local_sem):
        _exchange_and_sum(lambda b: g_ref, recv_ref, send_sems, recv_sems, local_sem)
        total = recv_ref[0]
        for b in range(1, N_DEV):
            total = total + recv_ref[b]
        out_ref[...] = total

    vmem = pl.BlockSpec(memory_space=pltpu.VMEM)
    return pl.pallas_call(
        body, name="all_reduce_small_grads", out_shape=jax.ShapeDtypeStruct((R, C), F32),
        in_specs=[vmem], out_specs=vmem,
        scratch_shapes=[pltpu.VMEM((N_DEV, R, C), F32), pltpu.SemaphoreType.DMA((N_DEV - 1,)),
                        pltpu.SemaphoreType.DMA((N_DEV - 1,)), pltpu.SemaphoreType.DMA(())],
        compiler_params=_cparams(has_side_effects=True),
    )(g)


def _adamw_small(name, w, g, m, v):
    def body(w_ref, g_ref, m_ref, v_ref, delta_ref, nm_ref, nv_ref):
        delta, nm, nv = _adamw(w_ref[...], g_ref[...], m_ref[...], v_ref[...])
        delta_ref[...] = delta
        nm_ref[...] = nm
        nv_ref[...] = nv

    vmem = pl.BlockSpec(memory_space=pltpu.VMEM)
    return pl.pallas_call(body, name=name, out_shape=[jax.ShapeDtypeStruct(w.shape, F32)] * 3,
                          in_specs=[vmem] * 4, out_specs=[vmem] * 3)(w, g, m, v)


def _in_proj(x, g, w_in):
    S = x.shape[0]
    tm = min(TOKEN_TILE, S)
    nconv = 2 * CONV_CH

    def body(x_ref, g_ref, w_ref, a_ref, uc_ref, qkv_ref):
        xf = x_ref[...]
        a = (xf * _rms_r(xf) * g_ref[...]).astype(BF16)
        a_ref[...] = a
        uc_ref[...] = _dot(a, w_ref[:, :nconv])
        qkv_ref[:, :SB_WIDTH] = (_dot(a, w_ref[:, nconv:nconv + SB_WIDTH]) * (1.0 / math.sqrt(SB_HEAD_DIM))).astype(BF16)
        qkv_ref[:, SB_WIDTH:] = _dot(a, w_ref[:, nconv + SB_WIDTH:]).astype(BF16)

    row = lambda n: pl.BlockSpec((tm, n), lambda i: (i, 0))
    return pl.pallas_call(
        body, name="in_proj", grid=(S // tm,),
        out_shape=[jax.ShapeDtypeStruct((S, D_MODEL), BF16), jax.ShapeDtypeStruct((S, nconv), F32),
                   jax.ShapeDtypeStruct((S, 3 * SB_WIDTH), BF16)],
        in_specs=[row(D_MODEL), _const((1, D_MODEL)), _resident(w_in.shape)],
        out_specs=[row(D_MODEL), row(nconv), row(3 * SB_WIDTH)],
        compiler_params=_cparams(),
    )(x, g, w_in)


def _glu(u):
    val, gate = u[:, :CONV_CH], u[:, CONV_CH:]
    sg = jax.nn.sigmoid(gate)
    return val, sg, val * sg


def _shift_copies(ext, shifted):
    n = shifted.shape[1]
    for r in range(1, SUBLANES):
        shifted[r - 1] = ext[r:r + n, :]


def _window(ext, shifted, start, rows):
    r = start % SUBLANES
    return ext[start:start + rows, :] if r == 0 else shifted[r - 1, start - r:start - r + rows, :]


def _conv_rows(glu_ext, glu_sh, cw_ref, r0, rows):
    base = r0 + CONV_HALO - (CONV_WIDTH - 1)
    acc = cw_ref[0:1, :] * _window(glu_ext, glu_sh, base, rows)
    for w in range(1, CONV_WIDTH):
        acc = acc + cw_ref[w:w + 1, :] * _window(glu_ext, glu_sh, base + w, rows)
    return acc


def _conv_fwd(u_conv, conv_w, conv_b, ln_g, ln_b):
    S = u_conv.shape[0]
    tc = min(TOKEN_TILE, S)

    def body(u_ref, cw_ref, cb_ref, lg_ref, lb_ref, out_ref, y_ref, glu_ext, glu_sh):
        i = pl.program_id(0)

        @pl.when(i == 0)
        def _():
            glu_ext[0:CONV_HALO, :] = jnp.zeros((CONV_HALO, CONV_CH), F32)

        @pl.when(i > 0)
        def _():
            glu_ext[0:CONV_HALO, :] = glu_ext[tc:tc + CONV_HALO, :]

        glu_ext[CONV_HALO:, :] = _glu(u_ref[...])[2]
        _shift_copies(glu_ext, glu_sh)
        for r0 in range(0, tc, CONV_CHUNK):
            y = _conv_rows(glu_ext, glu_sh, cw_ref, r0, CONV_CHUNK) + cb_ref[...]
            y_ref[r0:r0 + CONV_CHUNK, :] = y
            mu = jnp.mean(y, axis=-1, keepdims=True)
            yc = y - mu
            yn = yc * lax.rsqrt(jnp.mean(yc * yc, axis=-1, keepdims=True) + EPS)
            yl = yn * lg_ref[...] + lb_ref[...]
            out_ref[r0:r0 + CONV_CHUNK, :] = (yl * jax.nn.sigmoid(yl)).astype(BF16)

    return pl.pallas_call(
        body, name="conv_fwd", grid=(S // tc,),
        out_shape=[jax.ShapeDtypeStruct((S, CONV_CH), BF16), jax.ShapeDtypeStruct((S, CONV_CH), F32)],
        in_specs=[pl.BlockSpec((tc, 2 * CONV_CH), lambda i: (i, 0)), _const((CONV_HALO, CONV_CH)),
                  _const((1, CONV_CH)), _const((1, CONV_CH)), _const((1, CONV_CH))],
        out_specs=[pl.BlockSpec((tc, CONV_CH), lambda i: (i, 0))] * 2,
        scratch_shapes=[pltpu.VMEM((tc + CONV_HALO, CONV_CH), F32),
                        pltpu.VMEM((SUBLANES - 1, tc + CONV_HALO - SUBLANES, CONV_CH), F32)],
        compiler_params=_cparams(dimension_semantics=("arbitrary",)),
    )(u_conv, conv_w, conv_b, ln_g, ln_b)


def _head_masks():
    lane = lax.broadcasted_iota(jnp.int32, (1, LANES), 1)
    return lane < SB_HEAD_DIM


def _split_heads(t, first):
    z = jnp.zeros_like(t)
    return jnp.where(first, t, z), jnp.where(first, z, t)


def _head_sum(t, first):
    a = jnp.sum(jnp.where(first, t, 0.0), axis=-1, keepdims=True)
    b = jnp.sum(jnp.where(first, 0.0, t), axis=-1, keepdims=True)
    return a, b


def _attn_fwd(qkv, g_attn):
    S = qkv.shape[0]
    Q = min(ATTN_BLOCK, S)
    nq = S // Q
    assert nq <= LANES
    ntiles = nq * (nq + 1) // 2
    unroll = ATTN_FWD_UNROLL
    assert unroll % 2 == 0 and ntiles >= 4 + unroll
    npair = SB_WIDTH // LANES
    tiles = [(i, j) for i in range(nq) for j in range(i, -1, -1)]

    def body(q_ref, k_ref, v_ref, g_ref, o_ref, ao_ref, cl_ref, z_buf, l_buf, z2_buf, a_buf, c_buf, mask_buf):
        first = _head_masks()
        lane = lax.broadcasted_iota(jnp.int32, (1, LANES), 1)
        row = lax.broadcasted_iota(jnp.int32, (Q, Q), 0)
        col = lax.broadcasted_iota(jnp.int32, (Q, Q), 1)
        tri = jnp.where(row >= col, -1.0, 0.0).astype(BF16)
        heads = range(2)
        strips = [slice(r0, r0 + ATTN_STRIP) for r0 in range(0, Q, ATTN_STRIP)]
        rows = lambda j: pl.ds(pl.multiple_of(j * Q, Q), Q)
        wide = lambda t: jnp.tile(t, (1, Q // LANES))
        as_int = lambda t: int(t) if isinstance(t, (bool, int)) else t.astype(jnp.int32)

        keep = col < row
        mask_buf[0, 0] = jnp.full((Q, Q), LOG2E, F32)
        mask_buf[0, 1] = jnp.zeros((Q, Q), F32)
        mask_buf[1, 0] = jnp.where(keep, LOG2E, 0.0)
        mask_buf[1, 1] = jnp.where(keep, 0.0, MASKED)
        o_ref[...] = jnp.zeros_like(o_ref)

        def scores(t, slot):
            i, j = t
            qh = _split_heads(q_ref[rows(i), :], first)
            kb = k_ref[rows(j), :]
            for h in heads:
                z_buf[slot, h] = _dot_nt(qh[h], kb)

        def logs(t, slot):
            i, j = t
            diag = as_int(i == j)
            for h in heads:
                for r in strips:
                    z2 = jnp.minimum(z_buf[slot, h, r, :] * LOG2E, Z2_MAX)
                    nl = jnp.log(1.0 + jnp.exp2(z2)) * mask_buf[diag, 0, r, :]
                    l_buf[slot, h, r, :] = nl.astype(BF16)
                    z2_buf[slot, h, r, :] = z2 + mask_buf[diag, 1, r, :]

        def sums(slot):
            return tuple(_dot(l_buf[slot, h], tri) for h in heads)

        def weights(t, slot, sm):
            i, j = t
            running = jnp.where(j == i, 0.0, 1.0)
            for h in heads:
                before = c_buf[h] * running
                for r in strips:
                    a_buf[slot, h, r, :] = jnp.exp2(z2_buf[slot, h, r, :] + sm[h][r] + wide(before[r])).astype(BF16)
                hl = slice(h * LANES, (h + 1) * LANES)
                cl_ref[rows(i), hl] = jnp.where(lane == j, before, cl_ref[rows(i), hl] * running)
                c_buf[h] = before + jnp.broadcast_to(sm[h][:, 0:1], (Q, LANES))

        def values(t, slot):
            i, j = t
            vh = _split_heads(v_ref[rows(j), :], first)
            o_ref[rows(i), :] += _dot(a_buf[slot, 0], vh[0]) + _dot(a_buf[slot, 1], vh[1])

        def iteration(t, p):
            ta, tb, tc, td = t
            if ta is not None:
                scores(ta, p)
            if tc is not None:
                sm = sums(p)
            if td is not None:
                values(td, 1 - p)
            if tb is not None:
                logs(tb, 1 - p)
            if tc is not None:
                weights(tc, p, sm)

        def window(n):
            return tuple(tiles[n - k] if 0 <= n - k < ntiles else None for k in range(4))

        def following(t):
            i, j = t
            last = j == 0
            return jnp.where(last, i + 1, i), jnp.where(last, i + 1, j - 1)

        peeled = 4 + (ntiles - 4) % unroll

        def unrolled_iterations(_, t):
            for n in range(peeled, peeled + unroll):
                iteration(t, n % 2)
                t = (following(t[0]),) + t[:3]
            return t

        c_buf[...] = jnp.zeros_like(c_buf)
        for n in range(peeled):
            iteration(window(n), n % 2)
        first_window = tuple((jnp.int32(i), jnp.int32(j)) for i, j in window(peeled))
        lax.fori_loop(0, (ntiles - peeled) // unroll, unrolled_iterations, first_window)
        for n in range(ntiles, ntiles + 3):
            iteration(window(n), n % 2)

        def head_norm(b, carry):
            o = o_ref[rows(b), :]
            sa, sb = _head_sum(o * o, first)
            r = jnp.where(first, lax.rsqrt(sa * (1.0 / SB_HEAD_DIM) + EPS), lax.rsqrt(sb * (1.0 / SB_HEAD_DIM) + EPS))
            ao_ref[rows(b), :] = (o * r * g_ref[...]).astype(BF16)
            return carry

        lax.fori_loop(0, nq, head_norm, 0)

    col_block = lambda off: pl.BlockSpec((S, LANES), lambda p: (0, off + p), pipeline_mode=pl.Buffered(1))
    out_block = lambda n: pl.BlockSpec((S, n), lambda p: (0, p), pipeline_mode=pl.Buffered(1))
    return pl.pallas_call(
        body, name="attn_fwd", grid=(npair,),
        out_shape=[jax.ShapeDtypeStruct((S, SB_WIDTH), F32), jax.ShapeDtypeStruct((S, SB_WIDTH), BF16),
                   jax.ShapeDtypeStruct((S, 2 * SB_WIDTH), F32)],
        in_specs=[col_block(0), col_block(npair), col_block(2 * npair), pl.BlockSpec((1, LANES), lambda p: (0, p))],
        out_specs=[out_block(LANES), out_block(LANES), out_block(2 * LANES)],
        scratch_shapes=[pltpu.VMEM((2, 2, Q, Q), F32), pltpu.VMEM((2, 2, Q, Q), BF16), pltpu.VMEM((2, 2, Q, Q), F32),
                        pltpu.VMEM((2, 2, Q, Q), BF16), pltpu.VMEM((2, Q, LANES), F32), pltpu.VMEM((2, 2, Q, Q), F32)],
        compiler_params=_cparams(dimension_semantics=("arbitrary",)),
    )(qkv, qkv, qkv, g_attn)


def _out_proj(conv_out, attn_out, w_out, x, g_post_mix, g_pre_ffn):
    S = x.shape[0]
    tm = min(TOKEN_TILE, S)

    def body(co_ref, ao_ref, w_ref, x_ref, g1_ref, g2_ref, y_ref, h1_ref, fin_ref):
        y = _dot(co_ref[...], w_ref[:CONV_CH, :]) + _dot(ao_ref[...], w_ref[CONV_CH:, :])
        h1 = x_ref[...] + y * _rms_r(y) * g1_ref[...]
        y_ref[...] = y
        h1_ref[...] = h1
        fin_ref[...] = (h1 * _rms_r(h1) * g2_ref[...]).astype(BF16)

    row = lambda n: pl.BlockSpec((tm, n), lambda i: (i, 0))
    return pl.pallas_call(
        body, name="out_proj", grid=(S // tm,),
        out_shape=[jax.ShapeDtypeStruct((S, D_MODEL), F32), jax.ShapeDtypeStruct((S, D_MODEL), F32),
                   jax.ShapeDtypeStruct((S, D_MODEL), BF16)],
        in_specs=[row(CONV_CH), row(SB_WIDTH), _resident(w_out.shape), row(D_MODEL), _const((1, D_MODEL)),
                  _const((1, D_MODEL))],
        out_specs=[row(D_MODEL)] * 3,
        compiler_params=_cparams(),
    )(conv_out, attn_out, w_out, x, g_post_mix, g_pre_ffn)


def _ffn_fwd_loss(f_in, w_gate, w_up, w_down, h1, target, g_post_ffn):
    S = f_in.shape[0]
    tm = min(TOKEN_TILE, S)
    nt = S // tm

    def body(fin_ref, wg_ref, wu_ref, wd_ref, h1_ref, t_ref, g_ref, gt_ref, up_ref, df_ref, dh2_ref, loss_ref, dg_ref,
             sq_acc):
        i = pl.program_id(0)

        @pl.when(i == 0)
        def _():
            sq_acc[...] = jnp.zeros_like(sq_acc)
            dg_ref[...] = jnp.zeros_like(dg_ref)

        fin = fin_ref[...]
        f = jnp.zeros((tm, D_MODEL), F32)
        for c0 in range(0, D_FF, FF_CHUNK):
            cols = slice(c0, c0 + FF_CHUNK)
            gt = _dot(fin, wg_ref[:, cols])
            up = _dot(fin, wu_ref[:, cols])
            gt_ref[:, cols] = gt.astype(BF16)
            up_ref[:, cols] = up.astype(BF16)
            f = f + _dot((gt * jax.nn.sigmoid(gt) * up).astype(BF16), wd_ref[cols, :])
        r = _rms_r(f)
        g = g_ref[...]
        diff = h1_ref[...] + f * r * g - t_ref[...]
        sq_acc[...] += jnp.sum(diff * diff, axis=0, keepdims=True)
        dh2 = diff * (1.0 / D_MODEL)
        dh2_ref[...] = dh2
        dg_ref[...] += jnp.sum(dh2 * f * r, axis=0, keepdims=True)
        df_ref[...] = _rms_bwd(f, r, g, dh2).astype(BF16)

        @pl.when(i == nt - 1)
        def _():
            loss_ref[...] = jnp.broadcast_to((0.5 / D_MODEL) * jnp.sum(sq_acc[...], axis=-1, keepdims=True), (1, LANES))

    row = lambda n: pl.BlockSpec((tm, n), lambda i: (i, 0))
    return pl.pallas_call(
        body, name="ffn_fwd_loss", grid=(nt,),
        out_shape=[jax.ShapeDtypeStruct((S, D_FF), BF16), jax.ShapeDtypeStruct((S, D_FF), BF16),
                   jax.ShapeDtypeStruct((S, D_MODEL), BF16), jax.ShapeDtypeStruct((S, D_MODEL), F32),
                   jax.ShapeDtypeStruct((1, LANES), F32), jax.ShapeDtypeStruct((1, D_MODEL), F32)],
        in_specs=[row(D_MODEL), _resident(w_gate.shape), _resident(w_up.shape), _resident(w_down.shape),
                  row(D_MODEL), row(D_MODEL), _const((1, D_MODEL))],
        out_specs=[row(D_FF), row(D_FF), row(D_MODEL), row(D_MODEL), _const((1, LANES)), _const((1, D_MODEL))],
        scratch_shapes=[pltpu.VMEM((1, D_MODEL), F32)],
        compiler_params=_cparams(dimension_semantics=("arbitrary",)),
    )(f_in, w_gate, w_up, w_down, h1, target, g_post_ffn)


def _ffn_bwd(df, gt, up, w_gate, w_up, w_down):
    S = df.shape[0]
    tm = min(FFN_TILE, S)

    def body(df_ref, gt_ref, up_ref, wg_ref, wu_ref, wd_ref, dgt_ref, dup_ref, act_ref, dfin_ref):
        df = df_ref[...]
        dfin = jnp.zeros((tm, D_MODEL), F32)
        for c0 in range(0, D_FF, FF_CHUNK):
            cols = slice(c0, c0 + FF_CHUNK)
            dact = _dot_nt(df, wd_ref[cols, :])
            gt = gt_ref[:, cols].astype(F32)
            up = up_ref[:, cols].astype(F32)
            s = jax.nn.sigmoid(gt)
            silu = gt * s
            dgt = (dact * up * (s * (1.0 + gt * (1.0 - s)))).astype(BF16)
            dup = (dact * silu).astype(BF16)
            act_ref[:, cols] = (silu * up).astype(BF16)
            dgt_ref[:, cols] = dgt
            dup_ref[:, cols] = dup
            dfin = dfin + _dot_nt(dgt, wg_ref[:, cols]) + _dot_nt(dup, wu_ref[:, cols])
        dfin_ref[...] = dfin

    row = lambda n: pl.BlockSpec((tm, n), lambda i: (i, 0))
    return pl.pallas_call(
        body, name="ffn_bwd", grid=(S // tm,),
        out_shape=[jax.ShapeDtypeStruct((S, D_FF), BF16)] * 3 + [jax.ShapeDtypeStruct((S, D_MODEL), F32)],
        in_specs=[row(D_MODEL), row(D_FF), row(D_FF), _resident(w_gate.shape), _resident(w_up.shape),
                  _resident(w_down.shape)],
        out_specs=[row(D_FF)] * 3 + [row(D_MODEL)],
        compiler_params=_cparams(),
    )(df, gt, up, w_gate, w_up, w_down)


def _matmul_tn(name, x, y, tn):
    S, K = x.shape
    N = y.shape[1]
    ts = min(GRAD_TILE, S)

    def body(x_ref, y_ref, o_ref):
        @pl.when(pl.program_id(1) == 0)
        def _():
            o_ref[...] = jnp.zeros_like(o_ref)

        o_ref[...] += _dot_tn(x_ref[...].astype(BF16), y_ref[...].astype(BF16))

    return pl.pallas_call(
        body, name=name, grid=(N // tn, S // ts),
        out_shape=jax.ShapeDtypeStruct((K, N), F32),
        in_specs=[pl.BlockSpec((ts, K), lambda n, s: (s, 0)), pl.BlockSpec((ts, tn), lambda n, s: (s, n))],
        out_specs=pl.BlockSpec((K, tn), lambda n, s: (0, n)),
        compiler_params=_cparams(dimension_semantics=("arbitrary", "arbitrary")),
    )(x, y)


def _mix_bwd(dfin, h1, y, dh2, g_pre_ffn, g_post_mix, w_out, o, g_attn):
    S = dfin.shape[0]
    tm = min(TOKEN_TILE, S)
    inv_dh = 1.0 / SB_HEAD_DIM

    def body(dfin_ref, h1_ref, y_ref, dh2_ref, g2_ref, g1_ref, w_ref, o_ref, ga_ref, dh1_ref, dy_ref, dco_ref, do_ref,
             dg2_ref, dg1_ref, dga_ref):
        @pl.when(pl.program_id(0) == 0)
        def _():
            dg2_ref[...] = jnp.zeros_like(dg2_ref)
            dg1_ref[...] = jnp.zeros_like(dg1_ref)
            dga_ref[...] = jnp.zeros_like(dga_ref)

        h1, dfin = h1_ref[...], dfin_ref[...]
        r2 = _rms_r(h1)
        dh1 = dh2_ref[...] + _rms_bwd(h1, r2, g2_ref[...], dfin)
        dg2_ref[...] += jnp.sum(dfin * h1 * r2, axis=0, keepdims=True)
        y = y_ref[...]
        r1 = _rms_r(y)
        dy = _rms_bwd(y, r1, g1_ref[...], dh1).astype(BF16)
        dg1_ref[...] += jnp.sum(dh1 * y * r1, axis=0, keepdims=True)
        dh1_ref[...] = dh1
        dy_ref[...] = dy
        dco_ref[...] = _dot_nt(dy, w_ref[:CONV_CH, :])
        dao_all = _dot_nt(dy, w_ref[CONV_CH:, :])
        first = _head_masks()
        for p in range(SB_WIDTH // LANES):
            cols = slice(p * LANES, (p + 1) * LANES)
            o, dao, g = o_ref[:, cols], dao_all[:, cols], ga_ref[:, cols]
            sa, sb = _head_sum(o * o, first)
            r = jnp.where(first, lax.rsqrt(sa * inv_dh + EPS), lax.rsqrt(sb * inv_dh + EPS))
            w = dao * g
            wa, wb = _head_sum(w * o, first)
            do_ref[:, cols] = (r * (w - o * (r * r) * (jnp.where(first, wa, wb) * inv_dh))).astype(BF16)
            dga_ref[:, cols] += jnp.sum(dao * o * r, axis=0, keepdims=True)

    row = lambda n: pl.BlockSpec((tm, n), lambda i: (i, 0))
    return pl.pallas_call(
        body, name="mix_bwd", grid=(S // tm,),
        out_shape=[jax.ShapeDtypeStruct((S, D_MODEL), F32), jax.ShapeDtypeStruct((S, D_MODEL), BF16),
                   jax.ShapeDtypeStruct((S, CONV_CH), F32), jax.ShapeDtypeStruct((S, SB_WIDTH), BF16),
                   jax.ShapeDtypeStruct((1, D_MODEL), F32), jax.ShapeDtypeStruct((1, D_MODEL), F32),
                   jax.ShapeDtypeStruct((1, SB_WIDTH), F32)],
        in_specs=[row(D_MODEL)] * 4 + [_const((1, D_MODEL)), _const((1, D_MODEL)), _resident(w_out.shape), row(SB_WIDTH),
                  _const((1, SB_WIDTH))],
        out_specs=[row(D_MODEL), row(D_MODEL), row(CONV_CH), row(SB_WIDTH), _const((1, D_MODEL)), _const((1, D_MODEL)),
                   _const((1, SB_WIDTH))],
        compiler_params=_cparams(dimension_semantics=("arbitrary",)),
    )(dfin, h1, y, dh2, g_pre_ffn, g_post_mix, w_out, o, g_attn)


def _attn_bwd(qkv, do, cl):
    S = qkv.shape[0]
    Q = min(ATTN_BLOCK, S)
    nq = S // Q
    ntiles = nq * (nq + 1) // 2
    unroll = ATTN_BWD_UNROLL
    assert unroll % 2 == 0 and ntiles >= 4 + unroll
    npair = SB_WIDTH // LANES
    tiles = [(i, j) for i in range(nq) for j in range(i + 1)]

    def body(q_ref, k_ref, v_ref, do_ref, cl_ref, dq_ref, dk_ref, dv_ref,
             z_buf, lb_buf, be_buf, g_buf, l_buf, a_buf, gb_buf, dz_buf, pg_buf, mask_buf):
        first = _head_masks()
        lane = lax.broadcasted_iota(jnp.int32, (1, LANES), 1)
        row = lax.broadcasted_iota(jnp.int32, (Q, Q), 0)
        col = lax.broadcasted_iota(jnp.int32, (Q, Q), 1)
        tri = jnp.where(row > col, -1.0, 0.0).astype(BF16)
        tpi = (row <= col).astype(BF16)
        heads = range(2)
        strips = [slice(r0, r0 + ATTN_STRIP) for r0 in range(0, Q, ATTN_STRIP)]
        rows = lambda j: pl.ds(pl.multiple_of(j * Q, Q), Q)
        wide = lambda t: jnp.tile(t, (1, Q // LANES))
        as_int = lambda t: int(t) if isinstance(t, (bool, int)) else t.astype(jnp.int32)

        keep = col < row
        mask_buf[0, 0] = jnp.full((Q, Q), LOG2E, F32)
        mask_buf[0, 1] = jnp.zeros((Q, Q), F32)
        mask_buf[1, 0] = jnp.where(keep, LOG2E, 0.0)
        mask_buf[1, 1] = jnp.where(keep, 0.0, MASKED)
        dq_ref[...] = jnp.zeros_like(dq_ref)
        dk_ref[...] = jnp.zeros_like(dk_ref)
        dv_ref[...] = jnp.zeros_like(dv_ref)

        def scores(t, slot):
            i, j = t
            qh = _split_heads(q_ref[rows(i), :], first)
            kb = k_ref[rows(j), :]
            for h in heads:
                z_buf[slot, h] = _dot_nt(qh[h], kb)

        def logs(t, slot):
            i, j = t
            diag = as_int(i == j)
            for h in heads:
                for r in strips:
                    z2 = jnp.minimum(z_buf[slot, h, r, :] * LOG2E, Z2_MAX)
                    nl = jnp.log(1.0 + jnp.exp2(z2)) * mask_buf[diag, 0, r, :]
                    l_buf[slot, h, r, :] = nl.astype(BF16)
                    lb_buf[slot, h, r, :] = (z2 - nl) + mask_buf[diag, 1, r, :]

        def sums(t, slot):
            i, j = t
            doh = _split_heads(do_ref[rows(i), :], first)
            vb = v_ref[rows(j), :]
            return (tuple(_dot(l_buf[slot, h], tri) for h in heads), tuple(_dot_nt(doh[h], vb) for h in heads))

        def weights(t, slot, sm, da):
            i, j = t
            for h in heads:
                c = jnp.sum(jnp.where(lane == j, cl_ref[rows(i), h * LANES:(h + 1) * LANES], 0.0), axis=-1, keepdims=True)
                c = jnp.broadcast_to(c, (Q, LANES))
                for r in strips:
                    lb = lb_buf[slot, h, r, :]
                    a = jnp.exp2(lb + sm[h][r] + wide(c[r]))
                    g = da[h][r] * a
                    a_buf[slot, h, r, :] = a.astype(BF16)
                    be_buf[slot, h, r, :] = jnp.exp2(lb)
                    g_buf[slot, h, r, :] = g
                    gb_buf[slot, h, r, :] = g.astype(BF16)

        def prefix(t, slot):
            i, j = t
            doh = _split_heads(do_ref[rows(i), :], first)
            dv_ref[rows(j), :] += _dot_tn(a_buf[slot, 0], doh[0]) + _dot_tn(a_buf[slot, 1], doh[1])
            return tuple(_dot(gb_buf[slot, h], tpi) for h in heads)

        def dscores(t, slot, pm):
            i, j = t
            for h in heads:
                pg = pg_buf[h] * jnp.where(j == 0, 0.0, 1.0)
                for r in strips:
                    dz = g_buf[slot, h, r, :] - be_buf[slot, h, r, :] * (pm[h][r] + wide(pg[r]))
                    dz_buf[slot, h, r, :] = dz.astype(BF16)
                pg_buf[h] = pg + jnp.broadcast_to(pm[h][:, Q - 1:Q], (Q, LANES))

        def grads(t, slot):
            i, j = t
            qh = _split_heads(q_ref[rows(i), :], first)
            kh = _split_heads(k_ref[rows(j), :], first)
            dq_ref[rows(i), :] += _dot(dz_buf[slot, 0], kh[0]) + _dot(dz_buf[slot, 1], kh[1])
            dk_ref[rows(j), :] += _dot_tn(dz_buf[slot, 0], qh[0]) + _dot_tn(dz_buf[slot, 1], qh[1])

        def iteration(t, p):
            ta, tb, tc, td, te = t
            if ta is not None:
                scores(ta, p)
            if tc is not None:
                sm, da = sums(tc, p)
            if td is not None:
                pm = prefix(td, 1 - p)
            if te is not None:
                grads(te, p)
            if tb is not None:
                logs(tb, 1 - p)
            if tc is not None:
                weights(tc, p, sm, da)
            if td is not None:
                dscores(td, 1 - p, pm)

        def window(n):
            return tuple(tiles[n - k] if 0 <= n - k < ntiles else None for k in range(5))

        def following(t):
            i, j = t
            last = j == i
            return jnp.where(last, i + 1, i), jnp.where(last, 0, j + 1)

        peeled = 4 + (ntiles - 4) % unroll

        def unrolled_iterations(_, t):
            for n in range(peeled, peeled + unroll):
                iteration(t, n % 2)
                t = (following(t[0]),) + t[:4]
            return t

        pg_buf[...] = jnp.zeros_like(pg_buf)
        for n in range(peeled):
            iteration(window(n), n % 2)
        first_window = tuple((jnp.int32(i), jnp.int32(j)) for i, j in window(peeled))
        lax.fori_loop(0, (ntiles - peeled) // unroll, unrolled_iterations, first_window)
        for n in range(ntiles, ntiles + 4):
            iteration(window(n), n % 2)
        dq_ref[...] = dq_ref[...] * (1.0 / math.sqrt(SB_HEAD_DIM))

    col_block = lambda off: pl.BlockSpec((S, LANES), lambda p: (0, off + p), pipeline_mode=pl.Buffered(1))
    return pl.pallas_call(
        body, name="attn_bwd", grid=(npair,),
        out_shape=[jax.ShapeDtypeStruct((S, SB_WIDTH), F32)] * 3,
        in_specs=[col_block(0), col_block(npair), col_block(2 * npair), col_block(0),
                  pl.BlockSpec((S, 2 * LANES), lambda p: (0, p), pipeline_mode=pl.Buffered(1))],
        out_specs=[pl.BlockSpec((S, LANES), lambda p: (0, p), pipeline_mode=pl.Buffered(1))] * 3,
        scratch_shapes=[pltpu.VMEM((2, 2, Q, Q), F32)] * 4 + [pltpu.VMEM((2, 2, Q, Q), BF16)] * 4
        + [pltpu.VMEM((2, Q, LANES), F32), pltpu.VMEM((2, 2, Q, Q), F32)],
        compiler_params=_cparams(dimension_semantics=("arbitrary",)),
    )(qkv, qkv, qkv, do, cl)


def _conv_bwd(u_conv, y_conv, dco, conv_w, ln_g, ln_b):
    S = u_conv.shape[0]
    tc = min(TOKEN_TILE, S)
    nt = S // tc
    per = tc // CONV_HALO
    groups = CONV_CHUNK // 8

    def body(u_ref, halo_ref, y_ref, dco_ref, cw_ref, lg_ref, lb_ref, du_ref, dcw_ref, dsm_ref, glu_ext, dyc_ext, sg_buf,
             dcw_acc, dsm_acc, glu_sh, dyc_sh):
        i = pl.program_id(0)
        ti = nt - 1 - i

        @pl.when(i == 0)
        def _():
            dyc_ext[tc:, :] = jnp.zeros((CONV_HALO, CONV_CH), F32)
            dcw_acc[...] = jnp.zeros_like(dcw_acc)
            dsm_acc[...] = jnp.zeros_like(dsm_acc)

        @pl.when(i > 0)
        def _():
            dyc_ext[tc:, :] = dyc_ext[0:CONV_HALO, :]

        glu_ext[0:CONV_HALO, :] = jnp.where(ti > 0, _glu(halo_ref[...])[2], 0.0)
        val, sg, glu = _glu(u_ref[...])
        glu_ext[CONV_HALO:, :] = glu
        sg_buf[...] = sg
        _shift_copies(glu_ext, glu_sh)

        dcb = jnp.zeros((8, CONV_CH), F32)
        dlg = jnp.zeros((8, CONV_CH), F32)
        dlb = jnp.zeros((8, CONV_CH), F32)
        fold = lambda t: jnp.sum(t.reshape(groups, 8, CONV_CH), axis=0)
        for r0 in range(0, tc, CONV_CHUNK):
            y = y_ref[r0:r0 + CONV_CHUNK, :]
            mu = jnp.mean(y, axis=-1, keepdims=True)
            yc = y - mu
            rstd = lax.rsqrt(jnp.mean(yc * yc, axis=-1, keepdims=True) + EPS)
            yn = yc * rstd
            yl = yn * lg_ref[...] + lb_ref[...]
            s = jax.nn.sigmoid(yl)
            dyl = dco_ref[r0:r0 + CONV_CHUNK, :] * (s * (1.0 + yl * (1.0 - s)))
            dlg = dlg + fold(dyl * yn)
            dlb = dlb + fold(dyl)
            wv = dyl * lg_ref[...]
            dyc = rstd * (wv - jnp.mean(wv, axis=-1, keepdims=True) - yn * jnp.mean(wv * yn, axis=-1, keepdims=True))
            dcb = dcb + fold(dyc)
            dyc_ext[r0:r0 + CONV_CHUNK, :] = dyc
        dsm_acc[0:8, :] += dcb
        dsm_acc[8:16, :] += dlg
        dsm_acc[16:24, :] += dlb
        _shift_copies(dyc_ext, dyc_sh)

        for r0 in range(0, tc, CONV_CHUNK):
            dyc = dyc_ext[r0:r0 + CONV_CHUNK, :]
            dglu = jnp.zeros((CONV_CHUNK, CONV_CH), F32)
            base = r0 + CONV_HALO - (CONV_WIDTH - 1)
            for w in range(CONV_WIDTH):
                back = r0 + (CONV_WIDTH - 1) - w
                dglu = dglu + cw_ref[w:w + 1, :] * _window(dyc_ext, dyc_sh, back, CONV_CHUNK)
                dcw_acc[8 * w:8 * w + 8, :] += fold(dyc * _window(glu_ext, glu_sh, base + w, CONV_CHUNK))
            sg = sg_buf[r0:r0 + CONV_CHUNK, :]
            v = u_ref[r0:r0 + CONV_CHUNK, :CONV_CH]
            du_ref[r0:r0 + CONV_CHUNK, :CONV_CH] = (dglu * sg).astype(BF16)
            du_ref[r0:r0 + CONV_CHUNK, CONV_CH:] = (dglu * v * sg * (1.0 - sg)).astype(BF16)

        @pl.when(i == nt - 1)
        def _():
            for w in range(CONV_WIDTH):
                dcw_ref[w:w + 1, :] = jnp.sum(dcw_acc[8 * w:8 * w + 8, :], axis=0, keepdims=True)
            dcw_ref[CONV_WIDTH:, :] = jnp.zeros((CONV_HALO - CONV_WIDTH, CONV_CH), F32)
            for k in range(3):
                dsm_ref[k:k + 1, :] = jnp.sum(dsm_acc[8 * k:8 * k + 8, :], axis=0, keepdims=True)
            dsm_ref[3:, :] = jnp.zeros((5, CONV_CH), F32)

    return pl.pallas_call(
        body, name="conv_bwd", grid=(nt,),
        out_shape=[jax.ShapeDtypeStruct((S, 2 * CONV_CH), BF16), jax.ShapeDtypeStruct((CONV_HALO, CONV_CH), F32),
                   jax.ShapeDtypeStruct((8, CONV_CH), F32)],
        in_specs=[pl.BlockSpec((tc, 2 * CONV_CH), lambda i: (nt - 1 - i, 0)),
                  pl.BlockSpec((CONV_HALO, 2 * CONV_CH), lambda i: (jnp.maximum((nt - 1 - i) * per - 1, 0), 0)),
                  pl.BlockSpec((tc, CONV_CH), lambda i: (nt - 1 - i, 0)), pl.BlockSpec((tc, CONV_CH), lambda i: (nt - 1 - i, 0)),
                  _const((CONV_HALO, CONV_CH)), _const((1, CONV_CH)), _const((1, CONV_CH))],
        out_specs=[pl.BlockSpec((tc, 2 * CONV_CH), lambda i: (nt - 1 - i, 0)), _const((CONV_HALO, CONV_CH)),
                   _const((8, CONV_CH))],
        scratch_shapes=[pltpu.VMEM((tc + CONV_HALO, CONV_CH), F32), pltpu.VMEM((tc + CONV_HALO, CONV_CH), F32),
                        pltpu.VMEM((tc, CONV_CH), F32), pltpu.VMEM((8 * CONV_HALO, CONV_CH), F32),
                        pltpu.VMEM((24, CONV_CH), F32)]
        + [pltpu.VMEM((SUBLANES - 1, tc + CONV_HALO - SUBLANES, CONV_CH), F32)] * 2,
        compiler_params=_cparams(dimension_semantics=("arbitrary",)),
    )(u_conv, u_conv, y_conv, dco, conv_w, ln_g, ln_b)


def _in_proj_bwd(du_conv, dq, dk, dv, w_in, x, g, dh1):
    S = x.shape[0]
    tm = min(TOKEN_TILE, S)
    nconv = 2 * CONV_CH

    def body(duc_ref, dq_ref, dk_ref, dv_ref, w_ref, x_ref, g_ref, dh1_ref, dx_ref, dg_ref):
        @pl.when(pl.program_id(0) == 0)
        def _():
            dg_ref[...] = jnp.zeros_like(dg_ref)

        da = _dot_nt(duc_ref[...], w_ref[:, :nconv])
        for n, ref in enumerate((dq_ref, dk_ref, dv_ref)):
            c0 = nconv + n * SB_WIDTH
            da = da + _dot_nt(ref[...].astype(BF16), w_ref[:, c0:c0 + SB_WIDTH])
        xf = x_ref[...]
        r = _rms_r(xf)
        dx_ref[...] = dh1_ref[...] + _rms_bwd(xf, r, g_ref[...], da)
        dg_ref[...] += jnp.sum(da * xf * r, axis=0, keepdims=True)

    row = lambda n: pl.BlockSpec((tm, n), lambda i: (i, 0))
    return pl.pallas_call(
        body, name="in_proj_bwd", grid=(S // tm,),
        out_shape=[jax.ShapeDtypeStruct((S, D_MODEL), F32), jax.ShapeDtypeStruct((1, D_MODEL), F32)],
        in_specs=[row(nconv), row(SB_WIDTH), row(SB_WIDTH), row(SB_WIDTH), _resident(w_in.shape), row(D_MODEL),
                  _const((1, D_MODEL)), row(D_MODEL)],
        out_specs=[row(D_MODEL), _const((1, D_MODEL))],
        compiler_params=_cparams(dimension_semantics=("arbitrary",)),
    )(du_conv, dq, dk, dv, w_in, x, g, dh1)


def _layer_grads(xs, target, g_pre_mix, w_in_f, conv_w_f, conv_b, conv_ln_g, conv_ln_b, attn_g, g_post_mix, g_pre_ffn,
                 g_post_ffn, late_weights, send_grads):
    a, u_conv, qkv = _in_proj(xs, g_pre_mix, w_in_f)
    conv_out, y_conv = _conv_fwd(u_conv, conv_w_f, conv_b, conv_ln_g, conv_ln_b)
    o, attn_out, cl = _attn_fwd(qkv, attn_g)
    w_out_f, w_gate_f, w_up_f, w_down_f = late_weights(attn_out)
    y, h1, f_in = _out_proj(conv_out, attn_out, w_out_f, xs, g_post_mix, g_pre_ffn)
    gt, up, df, dh2, loss_part, d_g_post_ffn = _ffn_fwd_loss(f_in, w_gate_f, w_up_f, w_down_f, h1, target, g_post_ffn)

    dgt, dup, act, dfin = _ffn_bwd(df, gt, up, w_gate_f, w_up_f, w_down_f)
    d_w_down = _matmul_tn("grad_w_down", act, df, 512)
    d_w_gate = _matmul_tn("grad_w_gate", f_in, dgt, FF_CHUNK)
    d_w_up = _matmul_tn("grad_w_up", f_in, dup, FF_CHUNK)
    sent = send_grads("ffn", (d_w_gate, d_w_up, d_w_down))
    dh1, dy, dco, do, d_g_pre_ffn, d_g_post_mix, d_attn_g = _mix_bwd(dfin, h1, y, dh2, g_pre_ffn + sent, g_post_mix,
                                                                     w_out_f, o, attn_g)
    d_w_out = jnp.concatenate([_matmul_tn("grad_w_out_conv", conv_out, dy, D_MODEL),
                               _matmul_tn("grad_w_out_attn", attn_out, dy, D_MODEL)], axis=0)
    sent = send_grads("w_out", (d_w_out,))
    dq, dk, dv = _attn_bwd(qkv, do, cl)
    du_conv, d_conv_w, d_conv_small = _conv_bwd(u_conv, y_conv, dco, conv_w_f, conv_ln_g + sent, conv_ln_b)
    d_w_in = jnp.concatenate([_matmul_tn("grad_w_in_conv", a, du_conv, 2 * CONV_CH),
                              _matmul_tn("grad_w_in_q", a, dq, SB_WIDTH), _matmul_tn("grad_w_in_k", a, dk, SB_WIDTH),
                              _matmul_tn("grad_w_in_v", a, dv, SB_WIDTH)], axis=1)
    sent = send_grads("w_in", (d_w_in,))
    grad_x, d_g_pre_mix = _in_proj_bwd(du_conv, dq, dk, dv, w_in_f, xs, g_pre_mix + sent, dh1)
    return (loss_part, grad_x, d_conv_w, d_conv_small, d_attn_g, d_g_pre_mix, d_g_post_mix, d_g_pre_ffn, d_g_post_ffn)


def _cols_to_blocks(w):
    K, N = w.shape
    return jnp.transpose(w.reshape(K, N_DEV, N // N_DEV), (1, 0, 2))


def _blocks_to_cols(blocks):
    n_dev, K, n = blocks.shape
    return jnp.transpose(blocks, (1, 0, 2)).reshape(K, n_dev * n)


def kernel(x, g_pre_mix, w_in, conv_w, conv_b, conv_ln_g, conv_ln_b, attn_norm_g, w_out, g_post_mix, g_pre_ffn, w_gate, w_up, w_down, g_post_ffn, loss_target, m_g_pre_mix, m_w_in, m_conv_w, m_conv_b, m_conv_ln_g, m_conv_ln_b, m_attn_norm_g, m_w_out, m_g_post_mix, m_g_pre_ffn, m_w_gate, m_w_up, m_w_down, m_g_post_ffn, v_g_pre_mix, v_w_in, v_conv_w, v_conv_b, v_conv_ln_g, v_conv_ln_b, v_attn_norm_g, v_w_out, v_g_post_mix, v_g_pre_ffn, v_w_gate, v_w_up, v_w_down, v_g_post_ffn):
    xs = x[0]
    target = loss_target[0]
    S = xs.shape[0]
    me = 4 * lax.axis_index("x") + 2 * lax.axis_index("y") + lax.axis_index("c")
    cw_shard = conv_w.reshape(CONV_WIDTH, CONV_CH // N_DEV)
    attn_g = attn_norm_g.reshape(1, SB_WIDTH)

    gathered = _all_gather([w_in[0].astype(BF16), cw_shard])
    w_in_f = _blocks_to_cols(gathered[0])
    conv_w_f = jnp.pad(_blocks_to_cols(gathered[1]), ((0, CONV_HALO - CONV_WIDTH), (0, 0)))
    gathered_zero = gathered[2][0:1, 0:1].astype(BF16)
    late = [w_out[0].astype(BF16) + gathered_zero, w_gate[0].astype(BF16), w_up[0].astype(BF16), w_down[0].astype(BF16)]
    late_started = _exchange_start("all_gather_late_start", late, scatter=False)

    def late_weights(after):
        lands = _exchange_wait("all_gather_late_wait", late_started, False, after)
        wo, wg, wu, wd = [lax.dynamic_update_index_in_dim(land, own, me, 0) for land, own in zip(lands, late)]
        return wo.reshape(D_MODEL, D_MODEL), _blocks_to_cols(wg), _blocks_to_cols(wu), wd.reshape(D_FF, D_MODEL)

    started = {}

    def send_grads(name, grads):
        blocks = [g.reshape(N_DEV, g.shape[0] // N_DEV, g.shape[1]) if g.shape[1] == D_MODEL else _cols_to_blocks(g)
                  for g in grads]
        payload = BF16 if name == "w_in" else F32
        sent = _exchange_start("reduce_scatter_" + name + "_start", [b.astype(payload) for b in blocks], scatter=True)
        started[name] = (sent, blocks)
        return sent[-1][0:1, 0:1]

    (loss_part, grad_x, d_conv_w, d_conv_small, d_attn_g, d_g_pre_mix, d_g_post_mix, d_g_pre_ffn,
     d_g_post_ffn) = _layer_grads(
        xs, target, g_pre_mix + late_started[-1][0:1, 0:1], w_in_f, conv_w_f, conv_b, conv_ln_g, conv_ln_b, attn_g,
        g_post_mix, g_pre_ffn, g_post_ffn, late_weights, send_grads)

    def reduced(name, after, shards):
        st, blocks = started[name]
        lands = _exchange_wait("reduce_scatter_" + name + "_wait", st, True, after)
        return [_sum_adamw("adamw_" + wn, land, lax.dynamic_index_in_dim(blk, me, 0, keepdims=False), w[0], m[0], v[0])
                for land, blk, (wn, w, m, v) in zip(lands, blocks, shards)]

    two = lambda t: t.reshape(2, CONV_CH)
    small_g = jnp.concatenate([
        d_conv_w,
        d_conv_small[0:3],
        d_attn_g,
        two(d_g_pre_mix), two(d_g_post_mix), two(d_g_pre_ffn), two(d_g_post_ffn),
        jnp.broadcast_to(loss_part[0:1, 0:1], (1, CONV_CH)),
        jnp.zeros((3, CONV_CH), F32)], axis=0)
    small_g = _all_reduce_small(small_g)
    loss = small_g[44, 0]
    g_conv_w = lax.dynamic_slice(small_g, (0, me * (CONV_CH // N_DEV)), (CONV_WIDTH, CONV_CH // N_DEV))
    pack = lambda cb, lg, lb, ag, g1, g2, g3, g4: jnp.concatenate(
        [cb, lg, lb, ag.reshape(1, SB_WIDTH), two(g1), two(g2), two(g3), two(g4), jnp.zeros((4, CONV_CH), F32)], axis=0)
    sm_g = small_g[CONV_HALO:]
    sm_delta, sm_m, sm_v = _adamw_small(
        "adamw_small",
        pack(conv_b, conv_ln_g, conv_ln_b, attn_norm_g, g_pre_mix, g_post_mix, g_pre_ffn, g_post_ffn), sm_g,
        pack(m_conv_b, m_conv_ln_g, m_conv_ln_b, m_attn_norm_g, m_g_pre_mix, m_g_post_mix, m_g_pre_ffn, m_g_post_ffn),
        pack(v_conv_b, v_conv_ln_g, v_conv_ln_b, v_attn_norm_g, v_g_pre_mix, v_g_post_mix, v_g_pre_ffn, v_g_post_ffn))
    cw_delta, cw_m, cw_v = _adamw_small("adamw_conv_w", cw_shard, g_conv_w,
                                        m_conv_w.reshape(cw_shard.shape), v_conv_w.reshape(cw_shard.shape))

    ffn = reduced("ffn", grad_x, [("w_gate", w_gate, m_w_gate, v_w_gate), ("w_up", w_up, m_w_up, v_w_up),
                                  ("w_down", w_down, m_w_down, v_w_down)])
    big = {"w_gate": ffn[0], "w_up": ffn[1], "w_down": ffn[2],
           "w_out": reduced("w_out", ffn[2][0], [("w_out", w_out, m_w_out, v_w_out)])[0]}
    big["w_in"] = reduced("w_in", big["w_out"][0], [("w_in", w_in, m_w_in, v_w_in)])[0]

    def unpack(t):
        return {"conv_b": t[0:1], "conv_ln_g": t[1:2], "conv_ln_b": t[2:3], "attn_norm_g": t[3:4].reshape(1, SB_HEADS, SB_HEAD_DIM),
                "g_pre_mix": t[4:6].reshape(1, D_MODEL), "g_post_mix": t[6:8].reshape(1, D_MODEL),
                "g_pre_ffn": t[8:10].reshape(1, D_MODEL), "g_post_ffn": t[10:12].reshape(1, D_MODEL)}

    names = ["g_pre_mix", "w_in", "conv_w", "conv_b", "conv_ln_g", "conv_ln_b", "attn_norm_g", "w_out", "g_post_mix",
             "g_pre_ffn", "w_gate", "w_up", "w_down", "g_post_ffn"]
    kinds = []
    for idx, small in enumerate((sm_g, sm_delta, sm_m, sm_v)):
        d = unpack(small)
        d["conv_w"] = (g_conv_w, cw_delta, cw_m, cw_v)[idx].reshape(1, CONV_WIDTH, 1, CONV_CH // N_DEV)
        for n in big:
            d[n] = big[n][idx][None]
        kinds.append([d[n] for n in names])

    return (loss, grad_x[None], *kinds[0], *kinds[1], *kinds[2], *kinds[3])
```

```python
import functools
import math

import jax
import jax.numpy as jnp
from jax import lax
from jax.experimental import pallas as pl
from jax.experimental.pallas import tpu as pltpu

F32 = jnp.float32
BF16 = jnp.bfloat16
MESH = pl.DeviceIdType.MESH

N_DEV = 8
D_MODEL = 1024
CONV_CH = 512
CONV_WIDTH = 31
SB_HEADS = 8
SB_HEAD_DIM = 64
SB_WIDTH = SB_HEADS * SB_HEAD_DIM
D_FF = 2816
EPS = 1e-6
LOG2E = 1.4426950408889634
Z2_MAX = 100.0
MASKED = -1e30
ADAM_LR = 0.001
ADAM_B1 = 0.9
ADAM_B2 = 0.999
ADAM_EPS = 1e-08
ADAM_WD = 0.01
ADAM_STEP = 10

SUBLANES = 8
LANES = 128
VMEM_LIMIT = 56 * 1024 * 1024
TOKEN_TILE = 512
GRAD_TILE = 1024
FFN_TILE = 256
ATTN_FWD_UNROLL = 8
ATTN_BWD_UNROLL = 4
ATTN_STRIP = 32
ATTN_BLOCK = 256
CONV_HALO = 32
CONV_CHUNK = 64
FF_CHUNK = D_FF // 2


def _cparams(**kw):
    return pltpu.CompilerParams(vmem_limit_bytes=VMEM_LIMIT, **kw)


def _resident(shape):
    return pl.BlockSpec(shape, lambda *_: (0,) * len(shape), pipeline_mode=pl.Buffered(1))


def _const(shape):
    return pl.BlockSpec(shape, lambda *_: (0,) * len(shape))


def _rms_r(xf):
    return lax.rsqrt(jnp.mean(xf * xf, axis=-1, keepdims=True) + EPS)


def _rms_bwd(xf, r, g, dout):
    w = dout * g
    return r * (w - xf * (r * r) * jnp.mean(w * xf, axis=-1, keepdims=True))


def _dot(a, b):
    return jnp.dot(a, b, preferred_element_type=F32)


def _dot_nt(a, b):
    return lax.dot_general(a, b, (((1,), (1,)), ((), ())), preferred_element_type=F32)


def _dot_tn(a, b):
    return lax.dot_general(a, b, (((0,), (0,)), ((), ())), preferred_element_type=F32)


def _peer(x, y, c, k):
    px = 1 - x if (k >> 2) & 1 else x
    py = 1 - y if (k >> 1) & 1 else y
    pc = 1 - c if k & 1 else c
    return (px, py, pc), 4 * px + 2 * py + pc


def _all_gather(shards):
    n = len(shards)

    def body(*refs):
        ins, outs, done = refs[:n], refs[n:2 * n], refs[2 * n]
        send_sems, recv_sems, local_sems = refs[2 * n + 1:]
        x, y, c = lax.axis_index("x"), lax.axis_index("y"), lax.axis_index("c")
        me, sibling = (x, y, c), (x, y, 1 - c)
        chips = [(1 - x, y), (x, 1 - y), (1 - x, 1 - y)]
        number = lambda d: 4 * d[0] + 2 * d[1] + d[2]

        def copy(a, k, block, to, src=None):
            rows = outs[a].at[number(block)]
            return pltpu.make_async_remote_copy(
                src_ref=rows if src is None else src, dst_ref=rows, send_sem=send_sems.at[a * (N_DEV - 1) + k],
                recv_sem=recv_sems.at[a * (N_DEV - 1) + k], device_id=to, device_id_type=MESH)

        copies = [pltpu.make_async_copy(ins[a], outs[a].at[number(me)], local_sems.at[a]) for a in range(n)]
        for mine in copies:
            mine.start()
        sent = [copy(a, 0, me, sibling, src=ins[a]) for a in range(n)]
        sent += [copy(a, 1 + j, me, (*chip, c), src=ins[a]) for j, chip in enumerate(chips) for a in range(n)]
        for cp in sent:
            cp.start()
        for j, chip in enumerate(chips):
            for a in range(n):
                copy(a, 1 + j, (*chip, c), me).wait_recv()
                passed = copy(a, 4 + j, (*chip, c), sibling)
                passed.start()
                sent.append(passed)
        for a in range(n):
            copy(a, 0, sibling, me).wait_recv()
            for j, chip in enumerate(chips):
                copy(a, 4 + j, (*chip, 1 - c), me).wait_recv()
        for cp in sent:
            cp.wait_send()
        for mine in copies:
            mine.wait()
        done[...] = jnp.zeros_like(done)

    any_spec = pl.BlockSpec(memory_space=pl.ANY)
    return pl.pallas_call(
        body, name="all_gather_weights",
        out_shape=[jax.ShapeDtypeStruct((N_DEV,) + s.shape, s.dtype) for s in shards] + [jax.ShapeDtypeStruct((8, LANES), F32)],
        in_specs=[any_spec] * n, out_specs=[any_spec] * n + [pl.BlockSpec(memory_space=pltpu.VMEM)],
        scratch_shapes=[pltpu.SemaphoreType.DMA((n * (N_DEV - 1),)), pltpu.SemaphoreType.DMA((n * (N_DEV - 1),)),
                        pltpu.SemaphoreType.DMA((n,))],
        compiler_params=pltpu.CompilerParams(has_side_effects=True),
    )(*shards)


def _adamw(w, g, m, v):
    m = ADAM_B1 * m + (1.0 - ADAM_B1) * g
    v = ADAM_B2 * v + (1.0 - ADAM_B2) * (g * g)
    m_hat = m / (1.0 - ADAM_B1 ** ADAM_STEP)
    v_hat = v / (1.0 - ADAM_B2 ** ADAM_STEP)
    delta = -ADAM_LR * (m_hat / (jnp.sqrt(v_hat) + ADAM_EPS) + ADAM_WD * w)
    return delta, m, v


def _exchange_and_sum(src_block, recv_ref, send_sems, recv_sems, local_sem):
    x, y, c = lax.axis_index("x"), lax.axis_index("y"), lax.axis_index("c")
    me = 4 * x + 2 * y + c
    mine = pltpu.make_async_copy(src_block(me), recv_ref.at[me], local_sem)
    mine.start()
    for k in range(1, N_DEV):
        peer, peer_block = _peer(x, y, c, k)
        pltpu.make_async_remote_copy(
            src_ref=src_block(peer_block), dst_ref=recv_ref.at[me], send_sem=send_sems.at[k - 1],
            recv_sem=recv_sems.at[k - 1], device_id=peer, device_id_type=MESH).start()
    for k in range(1, N_DEV):
        peer, peer_block = _peer(x, y, c, k)
        arrived = pltpu.make_async_remote_copy(
            src_ref=src_block(peer_block), dst_ref=recv_ref.at[peer_block], send_sem=send_sems.at[k - 1],
            recv_sem=recv_sems.at[k - 1], device_id=peer, device_id_type=MESH)
        arrived.wait_send()
        arrived.wait_recv()
    mine.wait()


HBM_SPEC = pl.BlockSpec(memory_space=pltpu.HBM)
SEM_SPEC = pl.BlockSpec(memory_space=pltpu.SEMAPHORE)
DATAFLOW = pltpu.SideEffectType.DATAFLOW_SIDE_EFFECTING


def _exchange_copies(srcs, lands, send_sems, recv_sems, scatter, wait):
    x, y, c = lax.axis_index("x"), lax.axis_index("y"), lax.axis_index("c")
    me = 4 * x + 2 * y + c
    for k in range(1, N_DEV):
        peer, peer_block = _peer(x, y, c, k)
        for a in range(len(srcs)):
            s = a * (N_DEV - 1) + k - 1
            src = srcs[a].at[peer_block] if scatter else srcs[a]
            copy = pltpu.make_async_remote_copy(
                src_ref=src, dst_ref=lands[a].at[peer_block if wait else me], send_sem=send_sems.at[s],
                recv_sem=recv_sems.at[s], device_id=peer, device_id_type=MESH)
            if wait:
                copy.wait_send()
                copy.wait_recv()
            else:
                copy.start()


def _exchange_start(name, arrays, scatter):
    n = len(arrays)
    land_shapes = [a.shape if scatter else (N_DEV,) + a.shape for a in arrays]

    def body(*refs):
        _exchange_copies(refs[:n], refs[n:2 * n], refs[2 * n], refs[2 * n + 1], scatter, wait=False)
        refs[-1][...] = jnp.zeros_like(refs[-1])

    sems = pltpu.SemaphoreType.DMA((n * (N_DEV - 1),))
    hbm = lambda t: pltpu.with_memory_space_constraint(t, pltpu.HBM)
    return pl.pallas_call(
        body, name=name,
        out_shape=(sems, sems, *[pltpu.HBM(a.shape, a.dtype) for a in arrays],
                   *[pltpu.HBM(ls, a.dtype) for ls, a in zip(land_shapes, arrays)], jax.ShapeDtypeStruct((8, LANES), F32)),
        in_specs=[HBM_SPEC] * (2 * n),
        out_specs=(SEM_SPEC, SEM_SPEC, *[HBM_SPEC] * (2 * n), pl.BlockSpec(memory_space=pltpu.VMEM)),
        input_output_aliases={a: 2 + a for a in range(2 * n)},
        compiler_params=pltpu.CompilerParams(has_side_effects=DATAFLOW),
    )(*[hbm(a) for a in arrays], *[hbm(lax.empty(ls, a.dtype)) for ls, a in zip(land_shapes, arrays)])


def _exchange_wait(name, started, scatter, after):
    n = (len(started) - 3) // 2
    send_sems, recv_sems = started[0], started[1]
    arrays, lands = started[2:2 + n], started[2 + n:2 + 2 * n]

    def body(*refs):
        _exchange_copies(refs[:n], refs[n:2 * n], refs[2 * n], refs[2 * n + 1], scatter, wait=True)

    return pl.pallas_call(
        body, name=name,
        out_shape=[pltpu.HBM(t.shape, t.dtype) for t in (*arrays, *lands)],
        in_specs=[HBM_SPEC] * (2 * n) + [SEM_SPEC, SEM_SPEC, pl.BlockSpec(memory_space=pl.ANY)],
        out_specs=[HBM_SPEC] * (2 * n),
        input_output_aliases={a: a for a in range(2 * n)},
        compiler_params=pltpu.CompilerParams(has_side_effects=DATAFLOW),
    )(*arrays, *lands, send_sems, recv_sems, after)[n:]


def _sum_adamw(name, land, own, w, m, v):
    _, M, N = land.shape
    rows = math.gcd(M, 128)

    def body(land_ref, own_ref, w_ref, m_ref, v_ref, grad_ref, delta_ref, nm_ref, nv_ref):
        x, y, c = lax.axis_index("x"), lax.axis_index("y"), lax.axis_index("c")
        g = own_ref[...]
        for k in range(1, N_DEV):
            g = g + land_ref[_peer(x, y, c, k)[1]].astype(F32)
        delta, nm, nv = _adamw(w_ref[...], g, m_ref[...], v_ref[...])
        grad_ref[...] = g
        delta_ref[...] = delta
        nm_ref[...] = nm
        nv_ref[...] = nv

    row = pl.BlockSpec((rows, N), lambda i: (i, 0))
    return pl.pallas_call(
        body, name=name, grid=(M // rows,), out_shape=[jax.ShapeDtypeStruct((M, N), F32)] * 4,
        in_specs=[pl.BlockSpec((N_DEV, rows, N), lambda i: (0, i, 0)), row, row, row, row], out_specs=[row] * 4,
        compiler_params=_cparams(),
    )(land, own, w, m, v)


def _all_reduce_small(g):
    R, C = g.shape

    def body(g_ref, out_ref, recv_ref, send_sems, recv_sems, local_sem):
        _exchange_and_sum(lambda b: g_ref, recv_ref, send_sems, recv_sems, local_sem)
        total = recv_ref[0]
        for b in range(1, N_DEV):
            total = total + recv_ref[b]
        out_ref[...] = total

    vmem = pl.BlockSpec(memory_space=pltpu.VMEM)
    return pl.pallas_call(
        body, name="all_reduce_small_grads", out_shape=jax.ShapeDtypeStruct((R, C), F32),
        in_specs=[vmem], out_specs=vmem,
        scratch_shapes=[pltpu.VMEM((N_DEV, R, C), F32), pltpu.SemaphoreType.DMA((N_DEV - 1,)),
                        pltpu.SemaphoreType.DMA((N_DEV - 1,)), pltpu.SemaphoreType.DMA(())],
        compiler_params=_cparams(has_side_effects=True),
    )(g)


def _adamw_small(name, w, g, m, v):
    def body(w_ref, g_ref, m_ref, v_ref, delta_ref, nm_ref, nv_ref):
        delta, nm, nv = _adamw(w_ref[...], g_ref[...], m_ref[...], v_ref[...])
        delta_ref[...] = delta
        nm_ref[...] = nm
        nv_ref[...] = nv

    vmem = pl.BlockSpec(memory_space=pltpu.VMEM)
    return pl.pallas_call(body, name=name, out_shape=[jax.ShapeDtypeStruct(w.shape, F32)] * 3,
                          in_specs=[vmem] * 4, out_specs=[vmem] * 3)(w, g, m, v)


def _in_proj(x, g, w_in):
    S = x.shape[0]
    tm = min(TOKEN_TILE, S)
    nconv = 2 * CONV_CH

    def body(x_ref, g_ref, w_ref, a_ref, uc_ref, qkv_ref):
        xf = x_ref[...]
        a = (xf * _rms_r(xf) * g_ref[...]).astype(BF16)
        a_ref[...] = a
        uc_ref[...] = _dot(a, w_ref[:, :nconv])
        qkv_ref[:, :SB_WIDTH] = (_dot(a, w_ref[:, nconv:nconv + SB_WIDTH]) * (1.0 / math.sqrt(SB_HEAD_DIM))).astype(BF16)
        qkv_ref[:, SB_WIDTH:] = _dot(a, w_ref[:, nconv + SB_WIDTH:]).astype(BF16)

    row = lambda n: pl.BlockSpec((tm, n), lambda i: (i, 0))
    return pl.pallas_call(
        body, name="in_proj", grid=(S // tm,),
        out_shape=[jax.ShapeDtypeStruct((S, D_MODEL), BF16), jax.ShapeDtypeStruct((S, nconv), F32),
                   jax.ShapeDtypeStruct((S, 3 * SB_WIDTH), BF16)],
        in_specs=[row(D_MODEL), _const((1, D_MODEL)), _resident(w_in.shape)],
        out_specs=[row(D_MODEL), row(nconv), row(3 * SB_WIDTH)],
        compiler_params=_cparams(),
    )(x, g, w_in)


def _glu(u):
    val, gate = u[:, :CONV_CH], u[:, CONV_CH:]
    sg = jax.nn.sigmoid(gate)
    return val, sg, val * sg


def _shift_copies(ext, shifted):
    n = shifted.shape[1]
    for r in range(1, SUBLANES):
        shifted[r - 1] = ext[r:r + n, :]


def _window(ext, shifted, start, rows):
    r = start % SUBLANES
    return ext[start:start + rows, :] if r == 0 else shifted[r - 1, start - r:start - r + rows, :]


def _conv_rows(glu_ext, glu_sh, cw_ref, r0, rows):
    base = r0 + CONV_HALO - (CONV_WIDTH - 1)
    acc = cw_ref[0:1, :] * _window(glu_ext, glu_sh, base, rows)
    for w in range(1, CONV_WIDTH):
        acc = acc + cw_ref[w:w + 1, :] * _window(glu_ext, glu_sh, base + w, rows)
    return acc


def _conv_fwd(u_conv, conv_w, conv_b, ln_g, ln_b):
    S = u_conv.shape[0]
    tc = min(TOKEN_TILE, S)

    def body(u_ref, cw_ref, cb_ref, lg_ref, lb_ref, out_ref, y_ref, glu_ext, glu_sh):
        i = pl.program_id(0)

        @pl.when(i == 0)
        def _():
            glu_ext[0:CONV_HALO, :] = jnp.zeros((CONV_HALO, CONV_CH), F32)

        @pl.when(i > 0)
        def _():
            glu_ext[0:CONV_HALO, :] = glu_ext[tc:tc + CONV_HALO, :]

        glu_ext[CONV_HALO:, :] = _glu(u_ref[...])[2]
        _shift_copies(glu_ext, glu_sh)
        for r0 in range(0, tc, CONV_CHUNK):
            y = _conv_rows(glu_ext, glu_sh, cw_ref, r0, CONV_CHUNK) + cb_ref[...]
            y_ref[r0:r0 + CONV_CHUNK, :] = y
            mu = jnp.mean(y, axis=-1, keepdims=True)
            yc = y - mu
            yn = yc * lax.rsqrt(jnp.mean(yc * yc, axis=-1, keepdims=True) + EPS)
            yl = yn * lg_ref[...] + lb_ref[...]
            out_ref[r0:r0 + CONV_CHUNK, :] = (yl * jax.nn.sigmoid(yl)).astype(BF16)

    return pl.pallas_call(
        body, name="conv_fwd", grid=(S // tc,),
        out_shape=[jax.ShapeDtypeStruct((S, CONV_CH), BF16), jax.ShapeDtypeStruct((S, CONV_CH), F32)],
        in_specs=[pl.BlockSpec((tc, 2 * CONV_CH), lambda i: (i, 0)), _const((CONV_HALO, CONV_CH)),
                  _const((1, CONV_CH)), _const((1, CONV_CH)), _const((1, CONV_CH))],
        out_specs=[pl.BlockSpec((tc, CONV_CH), lambda i: (i, 0))] * 2,
        scratch_shapes=[pltpu.VMEM((tc + CONV_HALO, CONV_CH), F32),
                        pltpu.VMEM((SUBLANES - 1, tc + CONV_HALO - SUBLANES, CONV_CH), F32)],
        compiler_params=_cparams(dimension_semantics=("arbitrary",)),
    )(u_conv, conv_w, conv_b, ln_g, ln_b)


def _head_masks():
    lane = lax.broadcasted_iota(jnp.int32, (1, LANES), 1)
    return lane < SB_HEAD_DIM


def _split_heads(t, first):
    z = jnp.zeros_like(t)
    return jnp.where(first, t, z), jnp.where(first, z, t)


def _split_heads_into(halves, src, first, nblocks, rows):
    def one_block(b, carry):
        r = pl.ds(pl.multiple_of(b * rows, rows), rows)
        halves[0, r, :], halves[1, r, :] = _split_heads(src[r, :], first)
        return carry

    lax.fori_loop(0, nblocks, one_block, 0)


def _head_sum(t, first):
    a = jnp.sum(jnp.where(first, t, 0.0), axis=-1, keepdims=True)
    b = jnp.sum(jnp.where(first, 0.0, t), axis=-1, keepdims=True)
    return a, b


def _attn_fwd(qkv, g_attn):
    S = qkv.shape[0]
    Q = min(ATTN_BLOCK, S)
    nq = S // Q
    assert nq <= LANES
    ntiles = nq * (nq + 1) // 2
    unroll = ATTN_FWD_UNROLL
    assert unroll % 2 == 0 and ntiles >= 4 + unroll
    npair = SB_WIDTH // LANES
    tiles = [(i, j) for i in range(nq) for j in range(i, -1, -1)]

    def body(q_ref, k_ref, v_ref, g_ref, o_ref, ao_ref, cl_ref, z_buf, l_buf, z2_buf, a_buf, c_buf, mask_buf, qh_buf,
             vh_buf):
        first = _head_masks()
        lane = lax.broadcasted_iota(jnp.int32, (1, LANES), 1)
        row = lax.broadcasted_iota(jnp.int32, (Q, Q), 0)
        col = lax.broadcasted_iota(jnp.int32, (Q, Q), 1)
        tri = jnp.where(row >= col, -1.0, 0.0).astype(BF16)
        heads = range(2)
        strips = [slice(r0, r0 + ATTN_STRIP) for r0 in range(0, Q, ATTN_STRIP)]
        rows = lambda j: pl.ds(pl.multiple_of(j * Q, Q), Q)
        wide = lambda t: jnp.tile(t, (1, Q // LANES))
        as_int = lambda t: int(t) if isinstance(t, (bool, int)) else t.astype(jnp.int32)

        keep = col < row
        mask_buf[0, 0] = jnp.full((Q, Q), LOG2E, F32)
        mask_buf[0, 1] = jnp.zeros((Q, Q), F32)
        mask_buf[1, 0] = jnp.where(keep, LOG2E, 0.0)
        mask_buf[1, 1] = jnp.where(keep, 0.0, MASKED)
        o_ref[...] = jnp.zeros_like(o_ref)
        _split_heads_into(qh_buf, q_ref, first, nq, Q)
        _split_heads_into(vh_buf, v_ref, first, nq, Q)

        def scores(t, slot):
            i, j = t
            kb = k_ref[rows(j), :]
            for h in heads:
                z_buf[slot, h] = _dot_nt(qh_buf[h, rows(i), :], kb)

        def logs(t, slot):
            i, j = t
            diag = as_int(i == j)
            for h in heads:
                for r in strips:
                    z2 = jnp.minimum(z_buf[slot, h, r, :] * LOG2E, Z2_MAX)
                    nl = jnp.log(1.0 + jnp.exp2(z2)) * mask_buf[diag, 0, r, :]
                    l_buf[slot, h, r, :] = nl.astype(BF16)
                    z2_buf[slot, h, r, :] = z2 + mask_buf[diag, 1, r, :]

        def sums(slot):
            return tuple(_dot(l_buf[slot, h], tri) for h in heads)

        def weights(t, slot, sm):
            i, j = t
            running = jnp.where(j == i, 0.0, 1.0)
            for h in heads:
                before = c_buf[h] * running
                for r in strips:
                    a_buf[slot, h, r, :] = jnp.exp2(z2_buf[slot, h, r, :] + sm[h][r] + wide(before[r])).astype(BF16)
                hl = slice(h * LANES, (h + 1) * LANES)
                cl_ref[rows(i), hl] = jnp.where(lane == j, before, cl_ref[rows(i), hl] * running)
                c_buf[h] = before + jnp.broadcast_to(sm[h][:, 0:1], (Q, LANES))

        def values(t, slot):
            i, j = t
            o_ref[rows(i), :] += _dot(a_buf[slot, 0], vh_buf[0, rows(j), :]) + _dot(a_buf[slot, 1], vh_buf[1, rows(j), :])

        def iteration(t, p):
            ta, tb, tc, td = t
            if ta is not None:
                scores(ta, p)
            if tc is not None:
                sm = sums(p)
            if td is not None:
                values(td, 1 - p)
            if tb is not None:
                logs(tb, 1 - p)
            if tc is not None:
                weights(tc, p, sm)

        def window(n):
            return tuple(tiles[n - k] if 0 <= n - k < ntiles else None for k in range(4))

        def following(t):
            i, j = t
            last = j == 0
            return jnp.where(last, i + 1, i), jnp.where(last, i + 1, j - 1)

        peeled = 4 + (ntiles - 4) % unroll

        def unrolled_iterations(_, t):
            for n in range(peeled, peeled + unroll):
                iteration(t, n % 2)
                t = (following(t[0]),) + t[:3]
            return t

        c_buf[...] = jnp.zeros_like(c_buf)
        for n in range(peeled):
            iteration(window(n), n % 2)
        first_window = tuple((jnp.int32(i), jnp.int32(j)) for i, j in window(peeled))
        lax.fori_loop(0, (ntiles - peeled) // unroll, unrolled_iterations, first_window)
        for n in range(ntiles, ntiles + 3):
            iteration(window(n), n % 2)

        def head_norm(b, carry):
            o = o_ref[rows(b), :]
            sa, sb = _head_sum(o * o, first)
            r = jnp.where(first, lax.rsqrt(sa * (1.0 / SB_HEAD_DIM) + EPS), lax.rsqrt(sb * (1.0 / SB_HEAD_DIM) + EPS))
            ao_ref[rows(b), :] = (o * r * g_ref[...]).astype(BF16)
            return carry

        lax.fori_loop(0, nq, head_norm, 0)

    col_block = lambda off: pl.BlockSpec((S, LANES), lambda p: (0, off + p), pipeline_mode=pl.Buffered(1))
    out_block = lambda n: pl.BlockSpec((S, n), lambda p: (0, p), pipeline_mode=pl.Buffered(1))
    return pl.pallas_call(
        body, name="attn_fwd", grid=(npair,),
        out_shape=[jax.ShapeDtypeStruct((S, SB_WIDTH), F32), jax.ShapeDtypeStruct((S, SB_WIDTH), BF16),
                   jax.ShapeDtypeStruct((S, 2 * SB_WIDTH), F32)],
        in_specs=[col_block(0), col_block(npair), col_block(2 * npair), pl.BlockSpec((1, LANES), lambda p: (0, p))],
        out_specs=[out_block(LANES), out_block(LANES), out_block(2 * LANES)],
        scratch_shapes=[pltpu.VMEM((2, 2, Q, Q), F32), pltpu.VMEM((2, 2, Q, Q), BF16), pltpu.VMEM((2, 2, Q, Q), F32),
                        pltpu.VMEM((2, 2, Q, Q), BF16), pltpu.VMEM((2, Q, LANES), F32), pltpu.VMEM((2, 2, Q, Q), F32),
                        pltpu.VMEM((2, S, LANES), BF16), pltpu.VMEM((2, S, LANES), BF16)],
        compiler_params=_cparams(dimension_semantics=("arbitrary",)),
    )(qkv, qkv, qkv, g_attn)


def _out_proj(conv_out, attn_out, w_out, x, g_post_mix, g_pre_ffn):
    S = x.shape[0]
    tm = min(TOKEN_TILE, S)

    def body(co_ref, ao_ref, w_ref, x_ref, g1_ref, g2_ref, y_ref, h1_ref, fin_ref):
        y = _dot(co_ref[...], w_ref[:CONV_CH, :]) + _dot(ao_ref[...], w_ref[CONV_CH:, :])
        h1 = x_ref[...] + y * _rms_r(y) * g1_ref[...]
        y_ref[...] = y
        h1_ref[...] = h1
        fin_ref[...] = (h1 * _rms_r(h1) * g2_ref[...]).astype(BF16)

    row = lambda n: pl.BlockSpec((tm, n), lambda i: (i, 0))
    return pl.pallas_call(
        body, name="out_proj", grid=(S // tm,),
        out_shape=[jax.ShapeDtypeStruct((S, D_MODEL), F32), jax.ShapeDtypeStruct((S, D_MODEL), F32),
                   jax.ShapeDtypeStruct((S, D_MODEL), BF16)],
        in_specs=[row(CONV_CH), row(SB_WIDTH), _resident(w_out.shape), row(D_MODEL), _const((1, D_MODEL)),
                  _const((1, D_MODEL))],
        out_specs=[row(D_MODEL)] * 3,
        compiler_params=_cparams(),
    )(conv_out, attn_out, w_out, x, g_post_mix, g_pre_ffn)


def _ffn_fwd_loss(f_in, w_gate, w_up, w_down, h1, target, g_post_ffn):
    S = f_in.shape[0]
    tm = min(TOKEN_TILE, S)
    nt = S // tm

    def body(fin_ref, wg_ref, wu_ref, wd_ref, h1_ref, t_ref, g_ref, gt_ref, up_ref, df_ref, dh2_ref, loss_ref, dg_ref,
             sq_acc):
        i = pl.program_id(0)

        @pl.when(i == 0)
        def _():
            sq_acc[...] = jnp.zeros_like(sq_acc)
            dg_ref[...] = jnp.zeros_like(dg_ref)

        fin = fin_ref[...]
        f = jnp.zeros((tm, D_MODEL), F32)
        for c0 in range(0, D_FF, FF_CHUNK):
            cols = slice(c0, c0 + FF_CHUNK)
            gt = _dot(fin, wg_ref[:, cols])
            up = _dot(fin, wu_ref[:, cols])
            gt_ref[:, cols] = gt.astype(BF16)
            up_ref[:, cols] = up.astype(BF16)
            f = f + _dot((gt * jax.nn.sigmoid(gt) * up).astype(BF16), wd_ref[cols, :])
        r = _rms_r(f)
        g = g_ref[...]
        diff = h1_ref[...] + f * r * g - t_ref[...]
        sq_acc[...] += jnp.sum(diff * diff, axis=0, keepdims=True)
        dh2 = diff * (1.0 / D_MODEL)
        dh2_ref[...] = dh2
        dg_ref[...] += jnp.sum(dh2 * f * r, axis=0, keepdims=True)
        df_ref[...] = _rms_bwd(f, r, g, dh2).astype(BF16)

        @pl.when(i == nt - 1)
        def _():
            loss_ref[...] = jnp.broadcast_to((0.5 / D_MODEL) * jnp.sum(sq_acc[...], axis=-1, keepdims=True), (1, LANES))

    row = lambda n: pl.BlockSpec((tm, n), lambda i: (i, 0))
    return pl.pallas_call(
        body, name="ffn_fwd_loss", grid=(nt,),
        out_shape=[jax.ShapeDtypeStruct((S, D_FF), BF16), jax.ShapeDtypeStruct((S, D_FF), BF16),
                   jax.ShapeDtypeStruct((S, D_MODEL), BF16), jax.ShapeDtypeStruct((S, D_MODEL), F32),
                   jax.ShapeDtypeStruct((1, LANES), F32), jax.ShapeDtypeStruct((1, D_MODEL), F32)],
        in_specs=[row(D_MODEL), _resident(w_gate.shape), _resident(w_up.shape), _resident(w_down.shape),
                  row(D_MODEL), row(D_MODEL), _const((1, D_MODEL))],
        out_specs=[row(D_FF), row(D_FF), row(D_MODEL), row(D_MODEL), _const((1, LANES)), _const((1, D_MODEL))],
        scratch_shapes=[pltpu.VMEM((1, D_MODEL), F32)],
        compiler_params=_cparams(dimension_semantics=("arbitrary",)),
    )(f_in, w_gate, w_up, w_down, h1, target, g_post_ffn)


def _ffn_bwd(df, gt, up, w_gate, w_up, w_down):
    S = df.shape[0]
    tm = min(FFN_TILE, S)

    def body(df_ref, gt_ref, up_ref, wg_ref, wu_ref, wd_ref, dgt_ref, dup_ref, act_ref, dfin_ref):
        df = df_ref[...]
        dfin = jnp.zeros((tm, D_MODEL), F32)
        for c0 in range(0, D_FF, FF_CHUNK):
            cols = slice(c0, c0 + FF_CHUNK)
            dact = _dot_nt(df, wd_ref[cols, :])
            gt = gt_ref[:, cols].astype(F32)
            up = up_ref[:, cols].astype(F32)
            s = jax.nn.sigmoid(gt)
            silu = gt * s
            dgt = (dact * up * (s * (1.0 + gt * (1.0 - s)))).astype(BF16)
            dup = (dact * silu).astype(BF16)
            act_ref[:, cols] = (silu * up).astype(BF16)
            dgt_ref[:, cols] = dgt
            dup_ref[:, cols] = dup
            dfin = dfin + _dot_nt(dgt, wg_ref[:, cols]) + _dot_nt(dup, wu_ref[:, cols])
        dfin_ref[...] = dfin

    row = lambda n: pl.BlockSpec((tm, n), lambda i: (i, 0))
    return pl.pallas_call(
        body, name="ffn_bwd", grid=(S // tm,),
        out_shape=[jax.ShapeDtypeStruct((S, D_FF), BF16)] * 3 + [jax.ShapeDtypeStruct((S, D_MODEL), F32)],
        in_specs=[row(D_MODEL), row(D_FF), row(D_FF), _resident(w_gate.shape), _resident(w_up.shape),
                  _resident(w_down.shape)],
        out_specs=[row(D_FF)] * 3 + [row(D_MODEL)],
        compiler_params=_cparams(),
    )(df, gt, up, w_gate, w_up, w_down)


def _matmul_tn(name, x, y, tn):
    S, K = x.shape
    N = y.shape[1]
    ts = min(GRAD_TILE, S)

    def body(x_ref, y_ref, o_ref):
        @pl.when(pl.program_id(1) == 0)
        def _():
            o_ref[...] = jnp.zeros_like(o_ref)

        o_ref[...] += _dot_tn(x_ref[...].astype(BF16), y_ref[...].astype(BF16))

    return pl.pallas_call(
        body, name=name, grid=(N // tn, S // ts),
        out_shape=jax.ShapeDtypeStruct((K, N), F32),
        in_specs=[pl.BlockSpec((ts, K), lambda n, s: (s, 0)), pl.BlockSpec((ts, tn), lambda n, s: (s, n))],
        out_specs=pl.BlockSpec((K, tn), lambda n, s: (0, n)),
        compiler_params=_cparams(dimension_semantics=("arbitrary", "arbitrary")),
    )(x, y)


def _mix_bwd(dfin, h1, y, dh2, g_pre_ffn, g_post_mix, w_out, o, g_attn):
    S = dfin.shape[0]
    tm = min(TOKEN_TILE, S)
    inv_dh = 1.0 / SB_HEAD_DIM

    def body(dfin_ref, h1_ref, y_ref, dh2_ref, g2_ref, g1_ref, w_ref, o_ref, ga_ref, dh1_ref, dy_ref, dco_ref, do_ref,
             dg2_ref, dg1_ref, dga_ref):
        @pl.when(pl.program_id(0) == 0)
        def _():
            dg2_ref[...] = jnp.zeros_like(dg2_ref)
            dg1_ref[...] = jnp.zeros_like(dg1_ref)
            dga_ref[...] = jnp.zeros_like(dga_ref)

        h1, dfin = h1_ref[...], dfin_ref[...]
        r2 = _rms_r(h1)
        dh1 = dh2_ref[...] + _rms_bwd(h1, r2, g2_ref[...], dfin)
        dg2_ref[...] += jnp.sum(dfin * h1 * r2, axis=0, keepdims=True)
        y = y_ref[...]
        r1 = _rms_r(y)
        dy = _rms_bwd(y, r1, g1_ref[...], dh1).astype(BF16)
        dg1_ref[...] += jnp.sum(dh1 * y * r1, axis=0, keepdims=True)
        dh1_ref[...] = dh1
        dy_ref[...] = dy
        dco_ref[...] = _dot_nt(dy, w_ref[:CONV_CH, :])
        dao_all = _dot_nt(dy, w_ref[CONV_CH:, :])
        first = _head_masks()
        for p in range(SB_WIDTH // LANES):
            cols = slice(p * LANES, (p + 1) * LANES)
            o, dao, g = o_ref[:, cols], dao_all[:, cols], ga_ref[:, cols]
            sa, sb = _head_sum(o * o, first)
            r = jnp.where(first, lax.rsqrt(sa * inv_dh + EPS), lax.rsqrt(sb * inv_dh + EPS))
            w = dao * g
            wa, wb = _head_sum(w * o, first)
            do_ref[:, cols] = (r * (w - o * (r * r) * (jnp.where(first, wa, wb) * inv_dh))).astype(BF16)
            dga_ref[:, cols] += jnp.sum(dao * o * r, axis=0, keepdims=True)

    row = lambda n: pl.BlockSpec((tm, n), lambda i: (i, 0))
    return pl.pallas_call(
        body, name="mix_bwd", grid=(S // tm,),
        out_shape=[jax.ShapeDtypeStruct((S, D_MODEL), F32), jax.ShapeDtypeStruct((S, D_MODEL), BF16),
                   jax.ShapeDtypeStruct((S, CONV_CH), F32), jax.ShapeDtypeStruct((S, SB_WIDTH), BF16),
                   jax.ShapeDtypeStruct((1, D_MODEL), F32), jax.ShapeDtypeStruct((1, D_MODEL), F32),
                   jax.ShapeDtypeStruct((1, SB_WIDTH), F32)],
        in_specs=[row(D_MODEL)] * 4 + [_const((1, D_MODEL)), _const((1, D_MODEL)), _resident(w_out.shape), row(SB_WIDTH),
                  _const((1, SB_WIDTH))],
        out_specs=[row(D_MODEL), row(D_MODEL), row(CONV_CH), row(SB_WIDTH), _const((1, D_MODEL)), _const((1, D_MODEL)),
                   _const((1, SB_WIDTH))],
        compiler_params=_cparams(dimension_semantics=("arbitrary",)),
    )(dfin, h1, y, dh2, g_pre_ffn, g_post_mix, w_out, o, g_attn)


def _attn_bwd(qkv, do, cl):
    S = qkv.shape[0]
    Q = min(ATTN_BLOCK, S)
    nq = S // Q
    ntiles = nq * (nq + 1) // 2
    unroll = ATTN_BWD_UNROLL
    assert unroll % 2 == 0 and ntiles >= 4 + unroll
    npair = SB_WIDTH // LANES
    tiles = [(i, j) for i in range(nq) for j in range(i + 1)]

    def body(q_ref, k_ref, v_ref, do_ref, cl_ref, dq_ref, dk_ref, dv_ref,
             z_buf, lb_buf, be_buf, g_buf, l_buf, a_buf, gb_buf, dz_buf, pg_buf, mask_buf, qh_buf, kh_buf, doh_buf):
        first = _head_masks()
        lane = lax.broadcasted_iota(jnp.int32, (1, LANES), 1)
        row = lax.broadcasted_iota(jnp.int32, (Q, Q), 0)
        col = lax.broadcasted_iota(jnp.int32, (Q, Q), 1)
        tri = jnp.where(row > col, -1.0, 0.0).astype(BF16)
        tpi = (row <= col).astype(BF16)
        heads = range(2)
        strips = [slice(r0, r0 + ATTN_STRIP) for r0 in range(0, Q, ATTN_STRIP)]
        rows = lambda j: pl.ds(pl.multiple_of(j * Q, Q), Q)
        wide = lambda t: jnp.tile(t, (1, Q // LANES))
        as_int = lambda t: int(t) if isinstance(t, (bool, int)) else t.astype(jnp.int32)

        keep = col < row
        mask_buf[0, 0] = jnp.full((Q, Q), LOG2E, F32)
        mask_buf[0, 1] = jnp.zeros((Q, Q), F32)
        mask_buf[1, 0] = jnp.where(keep, LOG2E, 0.0)
        mask_buf[1, 1] = jnp.where(keep, 0.0, MASKED)
        dq_ref[...] = jnp.zeros_like(dq_ref)
        dk_ref[...] = jnp.zeros_like(dk_ref)
        dv_ref[...] = jnp.zeros_like(dv_ref)
        _split_heads_into(qh_buf, q_ref, first, nq, Q)
        _split_heads_into(kh_buf, k_ref, first, nq, Q)
        _split_heads_into(doh_buf, do_ref, first, nq, Q)

        def scores(t, slot):
            i, j = t
            kb = k_ref[rows(j), :]
            for h in heads:
                z_buf[slot, h] = _dot_nt(qh_buf[h, rows(i), :], kb)

        def logs(t, slot):
            i, j = t
            diag = as_int(i == j)
            for h in heads:
                for r in strips:
                    z2 = jnp.minimum(z_buf[slot, h, r, :] * LOG2E, Z2_MAX)
                    nl = jnp.log(1.0 + jnp.exp2(z2)) * mask_buf[diag, 0, r, :]
                    l_buf[slot, h, r, :] = nl.astype(BF16)
                    lb_buf[slot, h, r, :] = (z2 - nl) + mask_buf[diag, 1, r, :]

        def sums(t, slot):
            i, j = t
            vb = v_ref[rows(j), :]
            return (tuple(_dot(l_buf[slot, h], tri) for h in heads),
                    tuple(_dot_nt(doh_buf[h, rows(i), :], vb) for h in heads))

        def weights(t, slot, sm, da):
            i, j = t
            for h in heads:
                c = jnp.sum(jnp.where(lane == j, cl_ref[rows(i), h * LANES:(h + 1) * LANES], 0.0), axis=-1, keepdims=True)
                c = jnp.broadcast_to(c, (Q, LANES))
                for r in strips:
                    lb = lb_buf[slot, h, r, :]
                    a = jnp.exp2(lb + sm[h][r] + wide(c[r]))
                    g = da[h][r] * a
                    a_buf[slot, h, r, :] = a.astype(BF16)
                    be_buf[slot, h, r, :] = jnp.exp2(lb)
                    g_buf[slot, h, r, :] = g
                    gb_buf[slot, h, r, :] = g.astype(BF16)

        def prefix(t, slot):
            i, j = t
            dv_ref[rows(j), :] += (_dot_tn(a_buf[slot, 0], doh_buf[0, rows(i), :])
                                   + _dot_tn(a_buf[slot, 1], doh_buf[1, rows(i), :]))
            return tuple(_dot(gb_buf[slot, h], tpi) for h in heads)

        def dscores(t, slot, pm):
            i, j = t
            for h in heads:
                pg = pg_buf[h] * jnp.where(j == 0, 0.0, 1.0)
                for r in strips:
                    dz = g_buf[slot, h, r, :] - be_buf[slot, h, r, :] * (pm[h][r] + wide(pg[r]))
                    dz_buf[slot, h, r, :] = dz.astype(BF16)
                pg_buf[h] = pg + jnp.broadcast_to(pm[h][:, Q - 1:Q], (Q, LANES))

        def grads(t, slot):
            i, j = t
            dq_ref[rows(i), :] += (_dot(dz_buf[slot, 0], kh_buf[0, rows(j), :])
                                   + _dot(dz_buf[slot, 1], kh_buf[1, rows(j), :]))
            dk_ref[rows(j), :] += (_dot_tn(dz_buf[slot, 0], qh_buf[0, rows(i), :])
                                   + _dot_tn(dz_buf[slot, 1], qh_buf[1, rows(i), :]))

        def iteration(t, p):
            ta, tb, tc, td, te = t
            if ta is not None:
                scores(ta, p)
            if tc is not None:
                sm, da = sums(tc, p)
            if td is not None:
                pm = prefix(td, 1 - p)
            if te is not None:
                grads(te, p)
            if tb is not None:
                logs(tb, 1 - p)
            if tc is not None:
                weights(tc, p, sm, da)
            if td is not None:
                dscores(td, 1 - p, pm)

        def window(n):
            return tuple(tiles[n - k] if 0 <= n - k < ntiles else None for k in range(5))

        def following(t):
            i, j = t
            last = j == i
            return jnp.where(last, i + 1, i), jnp.where(last, 0, j + 1)

        peeled = 4 + (ntiles - 4) % unroll

        def unrolled_iterations(_, t):
            for n in range(peeled, peeled + unroll):
                iteration(t, n % 2)
                t = (following(t[0]),) + t[:4]
            return t

        pg_buf[...] = jnp.zeros_like(pg_buf)
        for n in range(peeled):
            iteration(window(n), n % 2)
        first_window = tuple((jnp.int32(i), jnp.int32(j)) for i, j in window(peeled))
        lax.fori_loop(0, (ntiles - peeled) // unroll, unrolled_iterations, first_window)
        for n in range(ntiles, ntiles + 4):
            iteration(window(n), n % 2)
        dq_ref[...] = dq_ref[...] * (1.0 / math.sqrt(SB_HEAD_DIM))

    col_block = lambda off: pl.BlockSpec((S, LANES), lambda p: (0, off + p), pipeline_mode=pl.Buffered(1))
    return pl.pallas_call(
        body, name="attn_bwd", grid=(npair,),
        out_shape=[jax.ShapeDtypeStruct((S, SB_WIDTH), F32)] * 3,
        in_specs=[col_block(0), col_block(npair), col_block(2 * npair), col_block(0),
                  pl.BlockSpec((S, 2 * LANES), lambda p: (0, p), pipeline_mode=pl.Buffered(1))],
        out_specs=[pl.BlockSpec((S, LANES), lambda p: (0, p), pipeline_mode=pl.Buffered(1))] * 3,
        scratch_shapes=[pltpu.VMEM((2, 2, Q, Q), F32)] * 4 + [pltpu.VMEM((2, 2, Q, Q), BF16)] * 4
        + [pltpu.VMEM((2, Q, LANES), F32), pltpu.VMEM((2, 2, Q, Q), F32)] + [pltpu.VMEM((2, S, LANES), BF16)] * 3,
        compiler_params=_cparams(dimension_semantics=("arbitrary",)),
    )(qkv, qkv, qkv, do, cl)


def _conv_bwd(u_conv, y_conv, dco, conv_w, ln_g, ln_b):
    S = u_conv.shape[0]
    tc = min(TOKEN_TILE, S)
    nt = S // tc
    per = tc // CONV_HALO
    groups = CONV_CHUNK // 8

    def body(u_ref, halo_ref, y_ref, dco_ref, cw_ref, lg_ref, lb_ref, du_ref, dcw_ref, dsm_ref, glu_ext, dyc_ext, sg_buf,
             dcw_acc, dsm_acc, glu_sh, dyc_sh):
        i = pl.program_id(0)
        ti = nt - 1 - i

        @pl.when(i == 0)
        def _():
            dyc_ext[tc:, :] = jnp.zeros((CONV_HALO, CONV_CH), F32)
            dcw_acc[...] = jnp.zeros_like(dcw_acc)
            dsm_acc[...] = jnp.zeros_like(dsm_acc)

        @pl.when(i > 0)
        def _():
            dyc_ext[tc:, :] = dyc_ext[0:CONV_HALO, :]

        glu_ext[0:CONV_HALO, :] = jnp.where(ti > 0, _glu(halo_ref[...])[2], 0.0)
        val, sg, glu = _glu(u_ref[...])
        glu_ext[CONV_HALO:, :] = glu
        sg_buf[...] = sg
        _shift_copies(glu_ext, glu_sh)

        dcb = jnp.zeros((8, CONV_CH), F32)
        dlg = jnp.zeros((8, CONV_CH), F32)
        dlb = jnp.zeros((8, CONV_CH), F32)
        fold = lambda t: jnp.sum(t.reshape(groups, 8, CONV_CH), axis=0)
        for r0 in range(0, tc, CONV_CHUNK):
            y = y_ref[r0:r0 + CONV_CHUNK, :]
            mu = jnp.mean(y, axis=-1, keepdims=True)
            yc = y - mu
            rstd = lax.rsqrt(jnp.mean(yc * yc, axis=-1, keepdims=True) + EPS)
            yn = yc * rstd
            yl = yn * lg_ref[...] + lb_ref[...]
            s = jax.nn.sigmoid(yl)
            dyl = dco_ref[r0:r0 + CONV_CHUNK, :] * (s * (1.0 + yl * (1.0 - s)))
            dlg = dlg + fold(dyl * yn)
            dlb = dlb + fold(dyl)
            wv = dyl * lg_ref[...]
            dyc = rstd * (wv - jnp.mean(wv, axis=-1, keepdims=True) - yn * jnp.mean(wv * yn, axis=-1, keepdims=True))
            dcb = dcb + fold(dyc)
            dyc_ext[r0:r0 + CONV_CHUNK, :] = dyc
        dsm_acc[0:8, :] += dcb
        dsm_acc[8:16, :] += dlg
        dsm_acc[16:24, :] += dlb
        _shift_copies(dyc_ext, dyc_sh)

        for r0 in range(0, tc, CONV_CHUNK):
            dyc = dyc_ext[r0:r0 + CONV_CHUNK, :]
            dglu = jnp.zeros((CONV_CHUNK, CONV_CH), F32)
            base = r0 + CONV_HALO - (CONV_WIDTH - 1)
            for w in range(CONV_WIDTH):
                back = r0 + (CONV_WIDTH - 1) - w
                dglu = dglu + cw_ref[w:w + 1, :] * _window(dyc_ext, dyc_sh, back, CONV_CHUNK)
                dcw_acc[8 * w:8 * w + 8, :] += fold(dyc * _window(glu_ext, glu_sh, base + w, CONV_CHUNK))
            sg = sg_buf[r0:r0 + CONV_CHUNK, :]
            v = u_ref[r0:r0 + CONV_CHUNK, :CONV_CH]
            du_ref[r0:r0 + CONV_CHUNK, :CONV_CH] = (dglu * sg).astype(BF16)
            du_ref[r0:r0 + CONV_CHUNK, CONV_CH:] = (dglu * v * sg * (1.0 - sg)).astype(BF16)

        @pl.when(i == nt - 1)
        def _():
            for w in range(CONV_WIDTH):
                dcw_ref[w:w + 1, :] = jnp.sum(dcw_acc[8 * w:8 * w + 8, :], axis=0, keepdims=True)
            dcw_ref[CONV_WIDTH:, :] = jnp.zeros((CONV_HALO - CONV_WIDTH, CONV_CH), F32)
            for k in range(3):
                dsm_ref[k:k + 1, :] = jnp.sum(dsm_acc[8 * k:8 * k + 8, :], axis=0, keepdims=True)
            dsm_ref[3:, :] = jnp.zeros((5, CONV_CH), F32)

    return pl.pallas_call(
        body, name="conv_bwd", grid=(nt,),
        out_shape=[jax.ShapeDtypeStruct((S, 2 * CONV_CH), BF16), jax.ShapeDtypeStruct((CONV_HALO, CONV_CH), F32),
                   jax.ShapeDtypeStruct((8, CONV_CH), F32)],
        in_specs=[pl.BlockSpec((tc, 2 * CONV_CH), lambda i: (nt - 1 - i, 0)),
                  pl.BlockSpec((CONV_HALO, 2 * CONV_CH), lambda i: (jnp.maximum((nt - 1 - i) * per - 1, 0), 0)),
                  pl.BlockSpec((tc, CONV_CH), lambda i: (nt - 1 - i, 0)), pl.BlockSpec((tc, CONV_CH), lambda i: (nt - 1 - i, 0)),
                  _const((CONV_HALO, CONV_CH)), _const((1, CONV_CH)), _const((1, CONV_CH))],
        out_specs=[pl.BlockSpec((tc, 2 * CONV_CH), lambda i: (nt - 1 - i, 0)), _const((CONV_HALO, CONV_CH)),
                   _const((8, CONV_CH))],
        scratch_shapes=[pltpu.VMEM((tc + CONV_HALO, CONV_CH), F32), pltpu.VMEM((tc + CONV_HALO, CONV_CH), F32),
                        pltpu.VMEM((tc, CONV_CH), F32), pltpu.VMEM((8 * CONV_HALO, CONV_CH), F32),
                        pltpu.VMEM((24, CONV_CH), F32)]
        + [pltpu.VMEM((SUBLANES - 1, tc + CONV_HALO - SUBLANES, CONV_CH), F32)] * 2,
        compiler_params=_cparams(dimension_semantics=("arbitrary",)),
    )(u_conv, u_conv, y_conv, dco, conv_w, ln_g, ln_b)


def _in_proj_bwd(du_conv, dq, dk, dv, w_in, x, g, dh1):
    S = x.shape[0]
    tm = min(TOKEN_TILE, S)
    nconv = 2 * CONV_CH

    def body(duc_ref, dq_ref, dk_ref, dv_ref, w_ref, x_ref, g_ref, dh1_ref, dx_ref, dg_ref):
        @pl.when(pl.program_id(0) == 0)
        def _():
            dg_ref[...] = jnp.zeros_like(dg_ref)

        da = _dot_nt(duc_ref[...], w_ref[:, :nconv])
        for n, ref in enumerate((dq_ref, dk_ref, dv_ref)):
            c0 = nconv + n * SB_WIDTH
            da = da + _dot_nt(ref[...].astype(BF16), w_ref[:, c0:c0 + SB_WIDTH])
        xf = x_ref[...]
        r = _rms_r(xf)
        dx_ref[...] = dh1_ref[...] + _rms_bwd(xf, r, g_ref[...], da)
        dg_ref[...] += jnp.sum(da * xf * r, axis=0, keepdims=True)

    row = lambda n: pl.BlockSpec((tm, n), lambda i: (i, 0))
    return pl.pallas_call(
        body, name="in_proj_bwd", grid=(S // tm,),
        out_shape=[jax.ShapeDtypeStruct((S, D_MODEL), F32), jax.ShapeDtypeStruct((1, D_MODEL), F32)],
        in_specs=[row(nconv), row(SB_WIDTH), row(SB_WIDTH), row(SB_WIDTH), _resident(w_in.shape), row(D_MODEL),
                  _const((1, D_MODEL)), row(D_MODEL)],
        out_specs=[row(D_MODEL), _const((1, D_MODEL))],
        compiler_params=_cparams(dimension_semantics=("arbitrary",)),
    )(du_conv, dq, dk, dv, w_in, x, g, dh1)


def _layer_grads(xs, target, g_pre_mix, w_in_f, conv_w_f, conv_b, conv_ln_g, conv_ln_b, attn_g, g_post_mix, g_pre_ffn,
                 g_post_ffn, late_weights, send_grads):
    a, u_conv, qkv = _in_proj(xs, g_pre_mix, w_in_f)
    conv_out, y_conv = _conv_fwd(u_conv, conv_w_f, conv_b, conv_ln_g, conv_ln_b)
    o, attn_out, cl = _attn_fwd(qkv, attn_g)
    w_out_f, w_gate_f, w_up_f, w_down_f = late_weights(attn_out)
    y, h1, f_in = _out_proj(conv_out, attn_out, w_out_f, xs, g_post_mix, g_pre_ffn)
    gt, up, df, dh2, loss_part, d_g_post_ffn = _ffn_fwd_loss(f_in, w_gate_f, w_up_f, w_down_f, h1, target, g_post_ffn)

    dgt, dup, act, dfin = _ffn_bwd(df, gt, up, w_gate_f, w_up_f, w_down_f)
    d_w_down = _matmul_tn("grad_w_down", act, df, 512)
    d_w_gate = _matmul_tn("grad_w_gate", f_in, dgt, FF_CHUNK)
    d_w_up = _matmul_tn("grad_w_up", f_in, dup, FF_CHUNK)
    sent = send_grads("ffn", (d_w_gate, d_w_up, d_w_down))
    dh1, dy, dco, do, d_g_pre_ffn, d_g_post_mix, d_attn_g = _mix_bwd(dfin, h1, y, dh2, g_pre_ffn + sent, g_post_mix,
                                                                     w_out_f, o, attn_g)
    d_w_out = jnp.concatenate([_matmul_tn("grad_w_out_conv", conv_out, dy, D_MODEL),
                               _matmul_tn("grad_w_out_attn", attn_out, dy, D_MODEL)], axis=0)
    sent = send_grads("w_out", (d_w_out,))
    dq, dk, dv = _attn_bwd(qkv, do, cl)
    du_conv, d_conv_w, d_conv_small = _conv_bwd(u_conv, y_conv, dco, conv_w_f, conv_ln_g + sent, conv_ln_b)
    d_w_in = jnp.concatenate([_matmul_tn("grad_w_in_conv", a, du_conv, 2 * CONV_CH),
                              _matmul_tn("grad_w_in_q", a, dq, SB_WIDTH), _matmul_tn("grad_w_in_k", a, dk, SB_WIDTH),
                              _matmul_tn("grad_w_in_v", a, dv, SB_WIDTH)], axis=1)
    sent = send_grads("w_in", (d_w_in,))
    grad_x, d_g_pre_mix = _in_proj_bwd(du_conv, dq, dk, dv, w_in_f, xs, g_pre_mix + sent, dh1)
    return (loss_part, grad_x, d_conv_w, d_conv_small, d_attn_g, d_g_pre_mix, d_g_post_mix, d_g_pre_ffn, d_g_post_ffn)


def _cols_to_blocks(w):
    K, N = w.shape
    return jnp.transpose(w.reshape(K, N_DEV, N // N_DEV), (1, 0, 2))


def _blocks_to_cols(blocks):
    n_dev, K, n = blocks.shape
    return jnp.transpose(blocks, (1, 0, 2)).reshape(K, n_dev * n)


def kernel(x, g_pre_mix, w_in, conv_w, conv_b, conv_ln_g, conv_ln_b, attn_norm_g, w_out, g_post_mix, g_pre_ffn, w_gate, w_up, w_down, g_post_ffn, loss_target, m_g_pre_mix, m_w_in, m_conv_w, m_conv_b, m_conv_ln_g, m_conv_ln_b, m_attn_norm_g, m_w_out, m_g_post_mix, m_g_pre_ffn, m_w_gate, m_w_up, m_w_down, m_g_post_ffn, v_g_pre_mix, v_w_in, v_conv_w, v_conv_b, v_conv_ln_g, v_conv_ln_b, v_attn_norm_g, v_w_out, v_g_post_mix, v_g_pre_ffn, v_w_gate, v_w_up, v_w_down, v_g_post_ffn):
    xs = x[0]
    target = loss_target[0]
    S = xs.shape[0]
    me = 4 * lax.axis_index("x") + 2 * lax.axis_index("y") + lax.axis_index("c")
    cw_shard = conv_w.reshape(CONV_WIDTH, CONV_CH // N_DEV)
    attn_g = attn_norm_g.reshape(1, SB_WIDTH)

    gathered = _all_gather([w_in[0].astype(BF16), cw_shard])
    w_in_f = _blocks_to_cols(gathered[0])
    conv_w_f = jnp.pad(_blocks_to_cols(gathered[1]), ((0, CONV_HALO - CONV_WIDTH), (0, 0)))
    gathered_zero = gathered[2][0:1, 0:1].astype(BF16)
    late = [w_out[0].astype(BF16) + gathered_zero, w_gate[0].astype(BF16), w_up[0].astype(BF16), w_down[0].astype(BF16)]
    late_started = _exchange_start("all_gather_late_start", late, scatter=False)

    def late_weights(after):
        lands = _exchange_wait("all_gather_late_wait", late_started, False, after)
        wo, wg, wu, wd = [lax.dynamic_update_index_in_dim(land, own, me, 0) for land, own in zip(lands, late)]
        return wo.reshape(D_MODEL, D_MODEL), _blocks_to_cols(wg), _blocks_to_cols(wu), wd.reshape(D_FF, D_MODEL)

    started = {}

    def send_grads(name, grads):
        blocks = [g.reshape(N_DEV, g.shape[0] // N_DEV, g.shape[1]) if g.shape[1] == D_MODEL else _cols_to_blocks(g)
                  for g in grads]
        payload = BF16 if name == "w_in" else F32
        sent = _exchange_start("reduce_scatter_" + name + "_start", [b.astype(payload) for b in blocks], scatter=True)
        started[name] = (sent, blocks)
        return sent[-1][0:1, 0:1]

    (loss_part, grad_x, d_conv_w, d_conv_small, d_attn_g, d_g_pre_mix, d_g_post_mix, d_g_pre_ffn,
     d_g_post_ffn) = _layer_grads(
        xs, target, g_pre_mix + late_started[-1][0:1, 0:1], w_in_f, conv_w_f, conv_b, conv_ln_g, conv_ln_b, attn_g,
        g_post_mix, g_pre_ffn, g_post_ffn, late_weights, send_grads)

    def reduced(name, after, shards):
        st, blocks = started[name]
        lands = _exchange_wait("reduce_scatter_" + name + "_wait", st, True, after)
        return [_sum_adamw("adamw_" + wn, land, lax.dynamic_index_in_dim(blk, me, 0, keepdims=False), w[0], m[0], v[0])
                for land, blk, (wn, w, m, v) in zip(lands, blocks, shards)]

    two = lambda t: t.reshape(2, CONV_CH)
    small_g = jnp.concatenate([
        d_conv_w,
        d_conv_small[0:3],
        d_attn_g,
        two(d_g_pre_mix), two(d_g_post_mix), two(d_g_pre_ffn), two(d_g_post_ffn),
        jnp.broadcast_to(loss_part[0:1, 0:1], (1, CONV_CH)),
        jnp.zeros((3, CONV_CH), F32)], axis=0)
    small_g = _all_reduce_small(small_g)
    loss = small_g[44, 0]
    g_conv_w = lax.dynamic_slice(small_g, (0, me * (CONV_CH // N_DEV)), (CONV_WIDTH, CONV_CH // N_DEV))
    pack = lambda cb, lg, lb, ag, g1, g2, g3, g4: jnp.concatenate(
        [cb, lg, lb, ag.reshape(1, SB_WIDTH), two(g1), two(g2), two(g3), two(g4), jnp.zeros((4, CONV_CH), F32)], axis=0)
    sm_g = small_g[CONV_HALO:]
    sm_delta, sm_m, sm_v = _adamw_small(
        "adamw_small",
        pack(conv_b, conv_ln_g, conv_ln_b, attn_norm_g, g_pre_mix, g_post_mix, g_pre_ffn, g_post_ffn), sm_g,
        pack(m_conv_b, m_conv_ln_g, m_conv_ln_b, m_attn_norm_g, m_g_pre_mix, m_g_post_mix, m_g_pre_ffn, m_g_post_ffn),
        pack(v_conv_b, v_conv_ln_g, v_conv_ln_b, v_attn_norm_g, v_g_pre_mix, v_g_post_mix, v_g_pre_ffn, v_g_post_ffn))
    cw_delta, cw_m, cw_v = _adamw_small("adamw_conv_w", cw_shard, g_conv_w,
                                        m_conv_w.reshape(cw_shard.shape), v_conv_w.reshape(cw_shard.shape))

    ffn = reduced("ffn", grad_x, [("w_gate", w_gate, m_w_gate, v_w_gate), ("w_up", w_up, m_w_up, v_w_up),
                                  ("w_down", w_down, m_w_down, v_w_down)])
    big = {"w_gate": ffn[0], "w_up": ffn[1], "w_down": ffn[2],
           "w_out": reduced("w_out", ffn[2][0], [("w_out", w_out, m_w_out, v_w_out)])[0]}
    big["w_in"] = reduced("w_in", big["w_out"][0], [("w_in", w_in, m_w_in, v_w_in)])[0]

    def unpack(t):
        return {"conv_b": t[0:1], "conv_ln_g": t[1:2], "conv_ln_b": t[2:3], "attn_norm_g": t[3:4].reshape(1, SB_HEADS, SB_HEAD_DIM),
                "g_pre_mix": t[4:6].reshape(1, D_MODEL), "g_post_mix": t[6:8].reshape(1, D_MODEL),
                "g_pre_ffn": t[8:10].reshape(1, D_MODEL), "g_post_ffn": t[10:12].reshape(1, D_MODEL)}

    names = ["g_pre_mix", "w_in", "conv_w", "conv_b", "conv_ln_g", "conv_ln_b", "attn_norm_g", "w_out", "g_post_mix",
             "g_pre_ffn", "w_gate", "w_up", "w_down", "g_post_ffn"]
    kinds = []
    for idx, small in enumerate((sm_g, sm_delta, sm_m, sm_v)):
        d = unpack(small)
        d["conv_w"] = (g_conv_w, cw_delta, cw_m, cw_v)[idx].reshape(1, CONV_WIDTH, 1, CONV_CH // N_DEV)
        for n in big:
            d[n] = big[n][idx][None]
        kinds.append([d[n] for n in names])

    return (loss, grad_x[None], *kinds[0], *kinds[1], *kinds[2], *kinds[3])
```

```python
import math

import jax
import jax.numpy as jnp
from jax import lax
from jax.experimental import pallas as pl
from jax.experimental.pallas import tpu as pltpu

F32 = jnp.float32
BF16 = jnp.bfloat16
MESH = pl.DeviceIdType.MESH

N_DEV = 8
D_MODEL = 1024
CONV_CH = 512
CONV_WIDTH = 31
SB_HEADS = 8
SB_HEAD_DIM = 64
SB_WIDTH = SB_HEADS * SB_HEAD_DIM
D_FF = 2816
EPS = 1e-6
LOG2E = 1.4426950408889634
Z2_MAX = 100.0
MASKED = -1e30
ADAM_LR = 0.001
ADAM_B1 = 0.9
ADAM_B2 = 0.999
ADAM_EPS = 1e-08
ADAM_WD = 0.01
ADAM_STEP = 10

SUBLANES = 8
LANES = 128
VMEM_LIMIT = 56 * 1024 * 1024
TOKEN_TILE = 512
GRAD_TILE = 1024
FFN_TILE = 256
ATTN_FWD_UNROLL = 8
ATTN_BWD_UNROLL = 4
ATTN_STRIP = 32
ATTN_BLOCK = 256
CONV_HALO = 32
CONV_CHUNK = 64
FF_CHUNK = D_FF // 2


def _cparams(**kw):
    return pltpu.CompilerParams(vmem_limit_bytes=VMEM_LIMIT, **kw)


def _resident(shape):
    return pl.BlockSpec(shape, lambda *_: (0,) * len(shape), pipeline_mode=pl.Buffered(1))


def _const(shape):
    return pl.BlockSpec(shape, lambda *_: (0,) * len(shape))


def _rms_r(xf):
    return lax.rsqrt(jnp.mean(xf * xf, axis=-1, keepdims=True) + EPS)


def _rms_bwd(xf, r, g, dout):
    w = dout * g
    return r * (w - xf * (r * r) * jnp.mean(w * xf, axis=-1, keepdims=True))


def _dot(a, b):
    return jnp.dot(a, b, preferred_element_type=F32)


def _dot_nt(a, b):
    return lax.dot_general(a, b, (((1,), (1,)), ((), ())), preferred_element_type=F32)


def _dot_tn(a, b):
    return lax.dot_general(a, b, (((0,), (0,)), ((), ())), preferred_element_type=F32)


def _peer(x, y, c, k):
    px = 1 - x if (k >> 2) & 1 else x
    py = 1 - y if (k >> 1) & 1 else y
    pc = 1 - c if k & 1 else c
    return (px, py, pc), 4 * px + 2 * py + pc


def _all_gather(shards):
    n = len(shards)

    def body(*refs):
        ins, outs, done = refs[:n], refs[n:2 * n], refs[2 * n]
        send_sems, recv_sems, local_sems = refs[2 * n + 1:]
        x, y, c = lax.axis_index("x"), lax.axis_index("y"), lax.axis_index("c")
        me, sibling = (x, y, c), (x, y, 1 - c)
        chips = [(1 - x, y), (x, 1 - y), (1 - x, 1 - y)]
        number = lambda d: 4 * d[0] + 2 * d[1] + d[2]

        def copy(a, k, block, to, src=None):
            rows = outs[a].at[number(block)]
            return pltpu.make_async_remote_copy(
                src_ref=rows if src is None else src, dst_ref=rows, send_sem=send_sems.at[a * (N_DEV - 1) + k],
                recv_sem=recv_sems.at[a * (N_DEV - 1) + k], device_id=to, device_id_type=MESH)

        copies = [pltpu.make_async_copy(ins[a], outs[a].at[number(me)], local_sems.at[a]) for a in range(n)]
        for mine in copies:
            mine.start()
        sent = [copy(a, 0, me, sibling, src=ins[a]) for a in range(n)]
        sent += [copy(a, 1 + j, me, (*chip, c), src=ins[a]) for j, chip in enumerate(chips) for a in range(n)]
        for cp in sent:
            cp.start()
        for j, chip in enumerate(chips):
            for a in range(n):
                copy(a, 1 + j, (*chip, c), me).wait_recv()
                passed = copy(a, 4 + j, (*chip, c), sibling)
                passed.start()
                sent.append(passed)
        for a in range(n):
            copy(a, 0, sibling, me).wait_recv()
            for j, chip in enumerate(chips):
                copy(a, 4 + j, (*chip, 1 - c), me).wait_recv()
        for cp in sent:
            cp.wait_send()
        for mine in copies:
            mine.wait()
        done[...] = jnp.zeros_like(done)

    any_spec = pl.BlockSpec(memory_space=pl.ANY)
    return pl.pallas_call(
        body, name="all_gather_weights",
        out_shape=[jax.ShapeDtypeStruct((N_DEV,) + s.shape, s.dtype) for s in shards] + [jax.ShapeDtypeStruct((8, LANES), F32)],
        in_specs=[any_spec] * n, out_specs=[any_spec] * n + [pl.BlockSpec(memory_space=pltpu.VMEM)],
        scratch_shapes=[pltpu.SemaphoreType.DMA((n * (N_DEV - 1),)), pltpu.SemaphoreType.DMA((n * (N_DEV - 1),)),
                        pltpu.SemaphoreType.DMA((n,))],
        compiler_params=pltpu.CompilerParams(has_side_effects=True),
    )(*shards)


def _adamw(w, g, m, v):
    m = ADAM_B1 * m + (1.0 - ADAM_B1) * g
    v = ADAM_B2 * v + (1.0 - ADAM_B2) * (g * g)
    m_hat = m / (1.0 - ADAM_B1 ** ADAM_STEP)
    v_hat = v / (1.0 - ADAM_B2 ** ADAM_STEP)
    delta = -ADAM_LR * (m_hat / (jnp.sqrt(v_hat) + ADAM_EPS) + ADAM_WD * w)
    return delta, m, v


def _exchange_and_sum(src_block, recv_ref, send_sems, recv_sems, local_sem):
    x, y, c = lax.axis_index("x"), lax.axis_index("y"), lax.axis_index("c")
    me = 4 * x + 2 * y + c
    mine = pltpu.make_async_copy(src_block(me), recv_ref.at[me], local_sem)
    mine.start()
    for k in range(1, N_DEV):
        peer, peer_block = _peer(x, y, c, k)
        pltpu.make_async_remote_copy(
            src_ref=src_block(peer_block), dst_ref=recv_ref.at[me], send_sem=send_sems.at[k - 1],
            recv_sem=recv_sems.at[k - 1], device_id=peer, device_id_type=MESH).start()
    for k in range(1, N_DEV):
        peer, peer_block = _peer(x, y, c, k)
        arrived = pltpu.make_async_remote_copy(
            src_ref=src_block(peer_block), dst_ref=recv_ref.at[peer_block], send_sem=send_sems.at[k - 1],
            recv_sem=recv_sems.at[k - 1], device_id=peer, device_id_type=MESH)
        arrived.wait_send()
        arrived.wait_recv()
    mine.wait()


HBM_SPEC = pl.BlockSpec(memory_space=pltpu.HBM)
SEM_SPEC = pl.BlockSpec(memory_space=pltpu.SEMAPHORE)
DATAFLOW = pltpu.SideEffectType.DATAFLOW_SIDE_EFFECTING


def _exchange_copies(srcs, lands, send_sems, recv_sems, scatter, wait):
    x, y, c = lax.axis_index("x"), lax.axis_index("y"), lax.axis_index("c")
    me = 4 * x + 2 * y + c
    for k in range(1, N_DEV):
        peer, peer_block = _peer(x, y, c, k)
        for a in range(len(srcs)):
            s = a * (N_DEV - 1) + k - 1
            src = srcs[a].at[peer_block] if scatter else srcs[a]
            copy = pltpu.make_async_remote_copy(
                src_ref=src, dst_ref=lands[a].at[peer_block if wait else me], send_sem=send_sems.at[s],
                recv_sem=recv_sems.at[s], device_id=peer, device_id_type=MESH)
            if wait:
                copy.wait_send()
                copy.wait_recv()
            else:
                copy.start()


def _exchange_start(name, arrays, scatter):
    n = len(arrays)
    land_shapes = [a.shape if scatter else (N_DEV,) + a.shape for a in arrays]

    def body(*refs):
        _exchange_copies(refs[:n], refs[n:2 * n], refs[2 * n], refs[2 * n + 1], scatter, wait=False)
        refs[-1][...] = jnp.zeros_like(refs[-1])

    sems = pltpu.SemaphoreType.DMA((n * (N_DEV - 1),))
    hbm = lambda t: pltpu.with_memory_space_constraint(t, pltpu.HBM)
    return pl.pallas_call(
        body, name=name,
        out_shape=(sems, sems, *[pltpu.HBM(a.shape, a.dtype) for a in arrays],
                   *[pltpu.HBM(ls, a.dtype) for ls, a in zip(land_shapes, arrays)], jax.ShapeDtypeStruct((8, LANES), F32)),
        in_specs=[HBM_SPEC] * (2 * n),
        out_specs=(SEM_SPEC, SEM_SPEC, *[HBM_SPEC] * (2 * n), pl.BlockSpec(memory_space=pltpu.VMEM)),
        input_output_aliases={a: 2 + a for a in range(2 * n)},
        compiler_params=pltpu.CompilerParams(has_side_effects=DATAFLOW),
    )(*[hbm(a) for a in arrays], *[hbm(lax.empty(ls, a.dtype)) for ls, a in zip(land_shapes, arrays)])


def _exchange_wait(name, started, scatter, after):
    n = (len(started) - 3) // 2
    send_sems, recv_sems = started[0], started[1]
    arrays, lands = started[2:2 + n], started[2 + n:2 + 2 * n]

    def body(*refs):
        _exchange_copies(refs[:n], refs[n:2 * n], refs[2 * n], refs[2 * n + 1], scatter, wait=True)

    return pl.pallas_call(
        body, name=name,
        out_shape=[pltpu.HBM(t.shape, t.dtype) for t in (*arrays, *lands)],
        in_specs=[HBM_SPEC] * (2 * n) + [SEM_SPEC, SEM_SPEC, pl.BlockSpec(memory_space=pl.ANY)],
        out_specs=[HBM_SPEC] * (2 * n),
        input_output_aliases={a: a for a in range(2 * n)},
        compiler_params=pltpu.CompilerParams(has_side_effects=DATAFLOW),
    )(*arrays, *lands, send_sems, recv_sems, after)[n:]


def _sum_adamw(name, land, own, w, m, v):
    _, M, N = land.shape
    rows = math.gcd(M, 128)

    def body(land_ref, own_ref, w_ref, m_ref, v_ref, grad_ref, delta_ref, nm_ref, nv_ref):
        x, y, c = lax.axis_index("x"), lax.axis_index("y"), lax.axis_index("c")
        g = own_ref[...]
        for k in range(1, N_DEV):
            g = g + land_ref[_peer(x, y, c, k)[1]].astype(F32)
        delta, nm, nv = _adamw(w_ref[...], g, m_ref[...], v_ref[...])
        grad_ref[...] = g
        delta_ref[...] = delta
        nm_ref[...] = nm
        nv_ref[...] = nv

    row = pl.BlockSpec((rows, N), lambda i: (i, 0))
    return pl.pallas_call(
        body, name=name, grid=(M // rows,), out_shape=[jax.ShapeDtypeStruct((M, N), F32)] * 4,
        in_specs=[pl.BlockSpec((N_DEV, rows, N), lambda i: (0, i, 0)), row, row, row, row], out_specs=[row] * 4,
        compiler_params=_cparams(),
    )(land, own, w, m, v)


def _all_reduce_small(g):
    R, C = g.shape

    def body(g_ref, out_ref, recv_ref, send_sems, recv_sems, local_sem):
        _exchange_and_sum(lambda b: g_ref, recv_ref, send_sems, recv_sems, local_sem)
        total = recv_ref[0]
        for b in range(1, N_DEV):
            total = total + recv_ref[b]
        out_ref[...] = total

    vmem = pl.BlockSpec(memory_space=pltpu.VMEM)
    return pl.pallas_call(
        body, name="all_reduce_small_grads", out_shape=jax.ShapeDtypeStruct((R, C), F32),
        in_specs=[vmem], out_specs=vmem,
        scratch_shapes=[pltpu.VMEM((N_DEV, R, C), F32), pltpu.SemaphoreType.DMA((N_DEV - 1,)),
                        pltpu.SemaphoreType.DMA((N_DEV - 1,)), pltpu.SemaphoreType.DMA(())],
        compiler_params=_cparams(has_side_effects=True),
    )(g)


def _adamw_small(name, w, g, m, v):
    def body(w_ref, g_ref, m_ref, v_ref, delta_ref, nm_ref, nv_ref):
        delta, nm, nv = _adamw(w_ref[...], g_ref[...], m_ref[...], v_ref[...])
        delta_ref[...] = delta
        nm_ref[...] = nm
        nv_ref[...] = nv

    vmem = pl.BlockSpec(memory_space=pltpu.VMEM)
    return pl.pallas_call(body, name=name, out_shape=[jax.ShapeDtypeStruct(w.shape, F32)] * 3,
                          in_specs=[vmem] * 4, out_specs=[vmem] * 3)(w, g, m, v)


def _in_proj(x, g, w_in):
    S = x.shape[0]
    tm = min(TOKEN_TILE, S)
    nconv = 2 * CONV_CH

    def body(x_ref, g_ref, w_ref, a_ref, uc_ref, qkv_ref):
        xf = x_ref[...]
        a = (xf * _rms_r(xf) * g_ref[...]).astype(BF16)
        a_ref[...] = a
        uc_ref[...] = _dot(a, w_ref[:, :nconv])
        qkv_ref[:, :SB_WIDTH] = (_dot(a, w_ref[:, nconv:nconv + SB_WIDTH]) * (1.0 / math.sqrt(SB_HEAD_DIM))).astype(BF16)
        qkv_ref[:, SB_WIDTH:] = _dot(a, w_ref[:, nconv + SB_WIDTH:]).astype(BF16)

    row = lambda n: pl.BlockSpec((tm, n), lambda i: (i, 0))
    return pl.pallas_call(
        body, name="in_proj", grid=(S // tm,),
        out_shape=[jax.ShapeDtypeStruct((S, D_MODEL), BF16), jax.ShapeDtypeStruct((S, nconv), F32),
                   jax.ShapeDtypeStruct((S, 3 * SB_WIDTH), BF16)],
        in_specs=[row(D_MODEL), _const((1, D_MODEL)), _resident(w_in.shape)],
        out_specs=[row(D_MODEL), row(nconv), row(3 * SB_WIDTH)],
        compiler_params=_cparams(),
    )(x, g, w_in)


def _glu(u):
    val, gate = u[:, :CONV_CH], u[:, CONV_CH:]
    sg = jax.nn.sigmoid(gate)
    return val, sg, val * sg


def _shift_copies(ext, shifted):
    n = shifted.shape[1]
    for r in range(1, SUBLANES):
        shifted[r - 1] = ext[r:r + n, :]


def _window(ext, shifted, start, rows):
    r = start % SUBLANES
    return ext[start:start + rows, :] if r == 0 else shifted[r - 1, start - r:start - r + rows, :]


def _conv_rows(glu_ext, glu_sh, cw_ref, r0, rows):
    base = r0 + CONV_HALO - (CONV_WIDTH - 1)
    acc = cw_ref[0:1, :] * _window(glu_ext, glu_sh, base, rows)
    for w in range(1, CONV_WIDTH):
        acc = acc + cw_ref[w:w + 1, :] * _window(glu_ext, glu_sh, base + w, rows)
    return acc


def _conv_fwd(u_conv, conv_w, conv_b, ln_g, ln_b):
    S = u_conv.shape[0]
    tc = min(TOKEN_TILE, S)

    def body(u_ref, cw_ref, cb_ref, lg_ref, lb_ref, out_ref, y_ref, glu_ext, glu_sh):
        i = pl.program_id(0)

        @pl.when(i == 0)
        def _():
            glu_ext[0:CONV_HALO, :] = jnp.zeros((CONV_HALO, CONV_CH), F32)

        @pl.when(i > 0)
        def _():
            glu_ext[0:CONV_HALO, :] = glu_ext[tc:tc + CONV_HALO, :]

        glu_ext[CONV_HALO:, :] = _glu(u_ref[...])[2]
        _shift_copies(glu_ext, glu_sh)
        for r0 in range(0, tc, CONV_CHUNK):
            y = _conv_rows(glu_ext, glu_sh, cw_ref, r0, CONV_CHUNK) + cb_ref[...]
            y_ref[r0:r0 + CONV_CHUNK, :] = y
            mu = jnp.mean(y, axis=-1, keepdims=True)
            yc = y - mu
            yn = yc * lax.rsqrt(jnp.mean(yc * yc, axis=-1, keepdims=True) + EPS)
            yl = yn * lg_ref[...] + lb_ref[...]
            out_ref[r0:r0 + CONV_CHUNK, :] = (yl * jax.nn.sigmoid(yl)).astype(BF16)

    return pl.pallas_call(
        body, name="conv_fwd", grid=(S // tc,),
        out_shape=[jax.ShapeDtypeStruct((S, CONV_CH), BF16), jax.ShapeDtypeStruct((S, CONV_CH), F32)],
        in_specs=[pl.BlockSpec((tc, 2 * CONV_CH), lambda i: (i, 0)), _const((CONV_HALO, CONV_CH)),
                  _const((1, CONV_CH)), _const((1, CONV_CH)), _const((1, CONV_CH))],
        out_specs=[pl.BlockSpec((tc, CONV_CH), lambda i: (i, 0))] * 2,
        scratch_shapes=[pltpu.VMEM((tc + CONV_HALO, CONV_CH), F32),
                        pltpu.VMEM((SUBLANES - 1, tc + CONV_HALO - SUBLANES, CONV_CH), F32)],
        compiler_params=_cparams(dimension_semantics=("arbitrary",)),
    )(u_conv, conv_w, conv_b, ln_g, ln_b)


def _head_masks():
    lane = lax.broadcasted_iota(jnp.int32, (1, LANES), 1)
    return lane < SB_HEAD_DIM


def _split_heads(t, first):
    z = jnp.zeros_like(t)
    return jnp.where(first, t, z), jnp.where(first, z, t)


def _split_heads_into(halves, src, first, nblocks, rows):
    def one_block(b, carry):
        r = pl.ds(pl.multiple_of(b * rows, rows), rows)
        halves[0, r, :], halves[1, r, :] = _split_heads(src[r, :], first)
        return carry

    lax.fori_loop(0, nblocks, one_block, 0)


def _head_sum(t, first):
    a = jnp.sum(jnp.where(first, t, 0.0), axis=-1, keepdims=True)
    b = jnp.sum(jnp.where(first, 0.0, t), axis=-1, keepdims=True)
    return a, b


def _attn_fwd(qkv, g_attn):
    S = qkv.shape[0]
    Q = min(ATTN_BLOCK, S)
    nq = S // Q
    assert nq <= LANES
    ntiles = nq * (nq + 1) // 2
    unroll = ATTN_FWD_UNROLL
    assert unroll % 2 == 0 and ntiles >= 4 + unroll
    npair = SB_WIDTH // LANES
    tiles = [(i, j) for i in range(nq) for j in range(i, -1, -1)]

    def body(q_ref, k_ref, v_ref, g_ref, o_ref, ao_ref, cl_ref, z_buf, l_buf, z2_buf, a_buf, c_buf, mask_buf, qh_buf,
             vh_buf):
        first = _head_masks()
        lane = lax.broadcasted_iota(jnp.int32, (1, LANES), 1)
        row = lax.broadcasted_iota(jnp.int32, (Q, Q), 0)
        col = lax.broadcasted_iota(jnp.int32, (Q, Q), 1)
        tri = jnp.where(row >= col, -1.0, 0.0).astype(BF16)
        heads = range(2)
        strips = [slice(r0, r0 + ATTN_STRIP) for r0 in range(0, Q, ATTN_STRIP)]
        rows = lambda j: pl.ds(pl.multiple_of(j * Q, Q), Q)
        wide = lambda t: jnp.tile(t, (1, Q // LANES))
        as_int = lambda t: int(t) if isinstance(t, (bool, int)) else t.astype(jnp.int32)

        keep = col < row
        mask_buf[0, 0] = jnp.full((Q, Q), LOG2E, F32)
        mask_buf[0, 1] = jnp.zeros((Q, Q), F32)
        mask_buf[1, 0] = jnp.where(keep, LOG2E, 0.0)
        mask_buf[1, 1] = jnp.where(keep, 0.0, MASKED)
        o_ref[...] = jnp.zeros_like(o_ref)
        _split_heads_into(qh_buf, q_ref, first, nq, Q)
        _split_heads_into(vh_buf, v_ref, first, nq, Q)

        def scores(t, slot):
            i, j = t
            kb = k_ref[rows(j), :]
            for h in heads:
                z_buf[slot, h] = _dot_nt(qh_buf[h, rows(i), :], kb)

        def logs(t, slot):
            i, j = t
            diag = as_int(i == j)
            for h in heads:
                for r in strips:
                    z2 = jnp.minimum(z_buf[slot, h, r, :] * LOG2E, Z2_MAX)
                    nl = jnp.log(1.0 + jnp.exp2(z2)) * mask_buf[diag, 0, r, :]
                    l_buf[slot, h, r, :] = nl.astype(BF16)
                    z2_buf[slot, h, r, :] = z2 + mask_buf[diag, 1, r, :]

        def sums(slot):
            return tuple(_dot(l_buf[slot, h], tri) for h in heads)

        def weights(t, slot, sm):
            i, j = t
            running = jnp.where(j == i, 0.0, 1.0)
            for h in heads:
                before = c_buf[h] * running
                for r in strips:
                    a_buf[slot, h, r, :] = jnp.exp2(z2_buf[slot, h, r, :] + sm[h][r] + wide(before[r])).astype(BF16)
                hl = slice(h * LANES, (h + 1) * LANES)
                cl_ref[rows(i), hl] = jnp.where(lane == j, before, cl_ref[rows(i), hl])
                c_buf[h] = before + jnp.broadcast_to(sm[h][:, 0:1], (Q, LANES))

        def values(t, slot):
            i, j = t
            o_ref[rows(i), :] += _dot(a_buf[slot, 0], vh_buf[0, rows(j), :]) + _dot(a_buf[slot, 1], vh_buf[1, rows(j), :])

        def iteration(t, p):
            ta, tb, tc, td = t
            if ta is not None:
                scores(ta, p)
            if tc is not None:
                sm = sums(p)
            if td is not None:
                values(td, 1 - p)
            if tb is not None:
                logs(tb, 1 - p)
            if tc is not None:
                weights(tc, p, sm)

        def window(n):
            return tuple(tiles[n - k] if 0 <= n - k < ntiles else None for k in range(4))

        def following(t):
            i, j = t
            last = j == 0
            return jnp.where(last, i + 1, i), jnp.where(last, i + 1, j - 1)

        peeled = 4 + (ntiles - 4) % unroll

        def unrolled_iterations(_, t):
            for n in range(peeled, peeled + unroll):
                iteration(t, n % 2)
                t = (following(t[0]),) + t[:3]
            return t

        c_buf[...] = jnp.zeros_like(c_buf)
        for n in range(peeled):
            iteration(window(n), n % 2)
        first_window = tuple((jnp.int32(i), jnp.int32(j)) for i, j in window(peeled))
        lax.fori_loop(0, (ntiles - peeled) // unroll, unrolled_iterations, first_window)
        for n in range(ntiles, ntiles + 3):
            iteration(window(n), n % 2)

        def head_norm(b, carry):
            o = o_ref[rows(b), :]
            sa, sb = _head_sum(o * o, first)
            r = jnp.where(first, lax.rsqrt(sa * (1.0 / SB_HEAD_DIM) + EPS), lax.rsqrt(sb * (1.0 / SB_HEAD_DIM) + EPS))
            ao_ref[rows(b), :] = (o * r * g_ref[...]).astype(BF16)
            return carry

        lax.fori_loop(0, nq, head_norm, 0)

    col_block = lambda off: pl.BlockSpec((S, LANES), lambda p: (0, off + p), pipeline_mode=pl.Buffered(1))
    out_block = lambda n: pl.BlockSpec((S, n), lambda p: (0, p), pipeline_mode=pl.Buffered(1))
    return pl.pallas_call(
        body, name="attn_fwd", grid=(npair,),
        out_shape=[jax.ShapeDtypeStruct((S, SB_WIDTH), F32), jax.ShapeDtypeStruct((S, SB_WIDTH), BF16),
                   jax.ShapeDtypeStruct((S, 2 * SB_WIDTH), F32)],
        in_specs=[col_block(0), col_block(npair), col_block(2 * npair), pl.BlockSpec((1, LANES), lambda p: (0, p))],
        out_specs=[out_block(LANES), out_block(LANES), out_block(2 * LANES)],
        scratch_shapes=[pltpu.VMEM((2, 2, Q, Q), F32), pltpu.VMEM((2, 2, Q, Q), BF16), pltpu.VMEM((2, 2, Q, Q), F32),
                        pltpu.VMEM((2, 2, Q, Q), BF16), pltpu.VMEM((2, Q, LANES), F32), pltpu.VMEM((2, 2, Q, Q), F32),
                        pltpu.VMEM((2, S, LANES), BF16), pltpu.VMEM((2, S, LANES), BF16)],
        compiler_params=_cparams(dimension_semantics=("arbitrary",)),
    )(qkv, qkv, qkv, g_attn)


def _out_proj(conv_out, attn_out, w_out, x, g_post_mix, g_pre_ffn):
    S = x.shape[0]
    tm = min(TOKEN_TILE, S)

    def body(co_ref, ao_ref, w_ref, x_ref, g1_ref, g2_ref, y_ref, h1_ref, fin_ref):
        y = _dot(co_ref[...], w_ref[:CONV_CH, :]) + _dot(ao_ref[...], w_ref[CONV_CH:, :])
        h1 = x_ref[...] + y * _rms_r(y) * g1_ref[...]
        y_ref[...] = y
        h1_ref[...] = h1
        fin_ref[...] = (h1 * _rms_r(h1) * g2_ref[...]).astype(BF16)

    row = lambda n: pl.BlockSpec((tm, n), lambda i: (i, 0))
    return pl.pallas_call(
        body, name="out_proj", grid=(S // tm,),
        out_shape=[jax.ShapeDtypeStruct((S, D_MODEL), F32), jax.ShapeDtypeStruct((S, D_MODEL), F32),
                   jax.ShapeDtypeStruct((S, D_MODEL), BF16)],
        in_specs=[row(CONV_CH), row(SB_WIDTH), _resident(w_out.shape), row(D_MODEL), _const((1, D_MODEL)),
                  _const((1, D_MODEL))],
        out_specs=[row(D_MODEL)] * 3,
        compiler_params=_cparams(),
    )(conv_out, attn_out, w_out, x, g_post_mix, g_pre_ffn)


def _ffn_fwd_loss(f_in, w_gate, w_up, w_down, h1, target, g_post_ffn):
    S = f_in.shape[0]
    tm = min(TOKEN_TILE, S)
    nt = S // tm

    def body(fin_ref, wg_ref, wu_ref, wd_ref, h1_ref, t_ref, g_ref, gt_ref, up_ref, df_ref, dh2_ref, loss_ref, dg_ref,
             sq_acc):
        i = pl.program_id(0)

        @pl.when(i == 0)
        def _():
            sq_acc[...] = jnp.zeros_like(sq_acc)
            dg_ref[...] = jnp.zeros_like(dg_ref)

        fin = fin_ref[...]
        f = jnp.zeros((tm, D_MODEL), F32)
        for c0 in range(0, D_FF, FF_CHUNK):
            cols = slice(c0, c0 + FF_CHUNK)
            gt = _dot(fin, wg_ref[:, cols])
            up = _dot(fin, wu_ref[:, cols])
            gt_ref[:, cols] = gt.astype(BF16)
            up_ref[:, cols] = up.astype(BF16)
            f = f + _dot((gt * jax.nn.sigmoid(gt) * up).astype(BF16), wd_ref[cols, :])
        r = _rms_r(f)
        g = g_ref[...]
        diff = h1_ref[...] + f * r * g - t_ref[...]
        sq_acc[...] += jnp.sum(diff * diff, axis=0, keepdims=True)
        dh2 = diff * (1.0 / D_MODEL)
        dh2_ref[...] = dh2
        dg_ref[...] += jnp.sum(dh2 * f * r, axis=0, keepdims=True)
        df_ref[...] = _rms_bwd(f, r, g, dh2).astype(BF16)

        @pl.when(i == nt - 1)
        def _():
            loss_ref[...] = jnp.broadcast_to((0.5 / D_MODEL) * jnp.sum(sq_acc[...], axis=-1, keepdims=True), (1, LANES))

    row = lambda n: pl.BlockSpec((tm, n), lambda i: (i, 0))
    return pl.pallas_call(
        body, name="ffn_fwd_loss", grid=(nt,),
        out_shape=[jax.ShapeDtypeStruct((S, D_FF), BF16), jax.ShapeDtypeStruct((S, D_FF), BF16),
                   jax.ShapeDtypeStruct((S, D_MODEL), BF16), jax.ShapeDtypeStruct((S, D_MODEL), F32),
                   jax.ShapeDtypeStruct((1, LANES), F32), jax.ShapeDtypeStruct((1, D_MODEL), F32)],
        in_specs=[row(D_MODEL), _resident(w_gate.shape), _resident(w_up.shape), _resident(w_down.shape),
                  row(D_MODEL), row(D_MODEL), _const((1, D_MODEL))],
        out_specs=[row(D_FF), row(D_FF), row(D_MODEL), row(D_MODEL), _const((1, LANES)), _const((1, D_MODEL))],
        scratch_shapes=[pltpu.VMEM((1, D_MODEL), F32)],
        compiler_params=_cparams(dimension_semantics=("arbitrary",)),
    )(f_in, w_gate, w_up, w_down, h1, target, g_post_ffn)


def _ffn_bwd(df, gt, up, w_gate, w_up, w_down):
    S = df.shape[0]
    tm = min(FFN_TILE, S)

    def body(df_ref, gt_ref, up_ref, wg_ref, wu_ref, wd_ref, dgt_ref, dup_ref, act_ref, dfin_ref):
        df = df_ref[...]
        dfin = jnp.zeros((tm, D_MODEL), F32)
        for c0 in range(0, D_FF, FF_CHUNK):
            cols = slice(c0, c0 + FF_CHUNK)
            dact = _dot_nt(df, wd_ref[cols, :])
            gt = gt_ref[:, cols].astype(F32)
            up = up_ref[:, cols].astype(F32)
            s = jax.nn.sigmoid(gt)
            silu = gt * s
            dgt = (dact * up * (s * (1.0 + gt * (1.0 - s)))).astype(BF16)
            dup = (dact * silu).astype(BF16)
            act_ref[:, cols] = (silu * up).astype(BF16)
            dgt_ref[:, cols] = dgt
            dup_ref[:, cols] = dup
            dfin = dfin + _dot_nt(dgt, wg_ref[:, cols]) + _dot_nt(dup, wu_ref[:, cols])
        dfin_ref[...] = dfin

    row = lambda n: pl.BlockSpec((tm, n), lambda i: (i, 0))
    return pl.pallas_call(
        body, name="ffn_bwd", grid=(S // tm,),
        out_shape=[jax.ShapeDtypeStruct((S, D_FF), BF16)] * 3 + [jax.ShapeDtypeStruct((S, D_MODEL), F32)],
        in_specs=[row(D_MODEL), row(D_FF), row(D_FF), _resident(w_gate.shape), _resident(w_up.shape),
                  _resident(w_down.shape)],
        out_specs=[row(D_FF)] * 3 + [row(D_MODEL)],
        compiler_params=_cparams(),
    )(df, gt, up, w_gate, w_up, w_down)


def _matmul_tn(name, x, y, tn):
    S, K = x.shape
    N = y.shape[1]
    ts = min(GRAD_TILE, S)

    def body(x_ref, y_ref, o_ref):
        @pl.when(pl.program_id(1) == 0)
        def _():
            o_ref[...] = jnp.zeros_like(o_ref)

        o_ref[...] += _dot_tn(x_ref[...].astype(BF16), y_ref[...].astype(BF16))

    return pl.pallas_call(
        body, name=name, grid=(N // tn, S // ts),
        out_shape=jax.ShapeDtypeStruct((K, N), F32),
        in_specs=[pl.BlockSpec((ts, K), lambda n, s: (s, 0)), pl.BlockSpec((ts, tn), lambda n, s: (s, n))],
        out_specs=pl.BlockSpec((K, tn), lambda n, s: (0, n)),
        compiler_params=_cparams(dimension_semantics=("arbitrary", "arbitrary")),
    )(x, y)


def _mix_bwd(dfin, h1, y, dh2, g_pre_ffn, g_post_mix, w_out, o, g_attn):
    S = dfin.shape[0]
    tm = min(TOKEN_TILE, S)
    inv_dh = 1.0 / SB_HEAD_DIM

    def body(dfin_ref, h1_ref, y_ref, dh2_ref, g2_ref, g1_ref, w_ref, o_ref, ga_ref, dh1_ref, dy_ref, dco_ref, do_ref,
             dg2_ref, dg1_ref, dga_ref):
        @pl.when(pl.program_id(0) == 0)
        def _():
            dg2_ref[...] = jnp.zeros_like(dg2_ref)
            dg1_ref[...] = jnp.zeros_like(dg1_ref)
            dga_ref[...] = jnp.zeros_like(dga_ref)

        h1, dfin = h1_ref[...], dfin_ref[...]
        r2 = _rms_r(h1)
        dh1 = dh2_ref[...] + _rms_bwd(h1, r2, g2_ref[...], dfin)
        dg2_ref[...] += jnp.sum(dfin * h1 * r2, axis=0, keepdims=True)
        y = y_ref[...]
        r1 = _rms_r(y)
        dy = _rms_bwd(y, r1, g1_ref[...], dh1).astype(BF16)
        dg1_ref[...] += jnp.sum(dh1 * y * r1, axis=0, keepdims=True)
        dh1_ref[...] = dh1
        dy_ref[...] = dy
        dco_ref[...] = _dot_nt(dy, w_ref[:CONV_CH, :])
        dao_all = _dot_nt(dy, w_ref[CONV_CH:, :])
        first = _head_masks()
        for p in range(SB_WIDTH // LANES):
            cols = slice(p * LANES, (p + 1) * LANES)
            o, dao, g = o_ref[:, cols], dao_all[:, cols], ga_ref[:, cols]
            sa, sb = _head_sum(o * o, first)
            r = jnp.where(first, lax.rsqrt(sa * inv_dh + EPS), lax.rsqrt(sb * inv_dh + EPS))
            w = dao * g
            wa, wb = _head_sum(w * o, first)
            do_ref[:, cols] = (r * (w - o * (r * r) * (jnp.where(first, wa, wb) * inv_dh))).astype(BF16)
            dga_ref[:, cols] += jnp.sum(dao * o * r, axis=0, keepdims=True)

    row = lambda n: pl.BlockSpec((tm, n), lambda i: (i, 0))
    return pl.pallas_call(
        body, name="mix_bwd", grid=(S // tm,),
        out_shape=[jax.ShapeDtypeStruct((S, D_MODEL), F32), jax.ShapeDtypeStruct((S, D_MODEL), BF16),
                   jax.ShapeDtypeStruct((S, CONV_CH), F32), jax.ShapeDtypeStruct((S, SB_WIDTH), BF16),
                   jax.ShapeDtypeStruct((1, D_MODEL), F32), jax.ShapeDtypeStruct((1, D_MODEL), F32),
                   jax.ShapeDtypeStruct((1, SB_WIDTH), F32)],
        in_specs=[row(D_MODEL)] * 4 + [_const((1, D_MODEL)), _const((1, D_MODEL)), _resident(w_out.shape), row(SB_WIDTH),
                  _const((1, SB_WIDTH))],
        out_specs=[row(D_MODEL), row(D_MODEL), row(CONV_CH), row(SB_WIDTH), _const((1, D_MODEL)), _const((1, D_MODEL)),
                   _const((1, SB_WIDTH))],
        compiler_params=_cparams(dimension_semantics=("arbitrary",)),
    )(dfin, h1, y, dh2, g_pre_ffn, g_post_mix, w_out, o, g_attn)


def _attn_bwd(qkv, do, cl):
    S = qkv.shape[0]
    Q = min(ATTN_BLOCK, S)
    nq = S // Q
    ntiles = nq * (nq + 1) // 2
    unroll = ATTN_BWD_UNROLL
    assert unroll % 2 == 0 and ntiles >= 4 + unroll
    npair = SB_WIDTH // LANES
    tiles = [(i, j) for i in range(nq) for j in range(i + 1)]

    def body(q_ref, k_ref, v_ref, do_ref, cl_ref, dq_ref, dk_ref, dv_ref,
             z_buf, lb_buf, be_buf, g_buf, l_buf, a_buf, gb_buf, dz_buf, pg_buf, mask_buf, qh_buf, kh_buf, doh_buf):
        first = _head_masks()
        lane = lax.broadcasted_iota(jnp.int32, (1, LANES), 1)
        row = lax.broadcasted_iota(jnp.int32, (Q, Q), 0)
        col = lax.broadcasted_iota(jnp.int32, (Q, Q), 1)
        tri = jnp.where(row > col, -1.0, 0.0).astype(BF16)
        tpi = (row <= col).astype(BF16)
        heads = range(2)
        strips = [slice(r0, r0 + ATTN_STRIP) for r0 in range(0, Q, ATTN_STRIP)]
        rows = lambda j: pl.ds(pl.multiple_of(j * Q, Q), Q)
        wide = lambda t: jnp.tile(t, (1, Q // LANES))
        as_int = lambda t: int(t) if isinstance(t, (bool, int)) else t.astype(jnp.int32)

        keep = col < row
        mask_buf[0, 0] = jnp.full((Q, Q), LOG2E, F32)
        mask_buf[0, 1] = jnp.zeros((Q, Q), F32)
        mask_buf[1, 0] = jnp.where(keep, LOG2E, 0.0)
        mask_buf[1, 1] = jnp.where(keep, 0.0, MASKED)
        dq_ref[...] = jnp.zeros_like(dq_ref)
        dk_ref[...] = jnp.zeros_like(dk_ref)
        dv_ref[...] = jnp.zeros_like(dv_ref)
        _split_heads_into(qh_buf, q_ref, first, nq, Q)
        _split_heads_into(kh_buf, k_ref, first, nq, Q)
        _split_heads_into(doh_buf, do_ref, first, nq, Q)

        def scores(t, slot):
            i, j = t
            kb = k_ref[rows(j), :]
            for h in heads:
                z_buf[slot, h] = _dot_nt(qh_buf[h, rows(i), :], kb)

        def logs(t, slot):
            i, j = t
            diag = as_int(i == j)
            for h in heads:
                for r in strips:
                    z2 = jnp.minimum(z_buf[slot, h, r, :] * LOG2E, Z2_MAX)
                    nl = jnp.log(1.0 + jnp.exp2(z2)) * mask_buf[diag, 0, r, :]
                    l_buf[slot, h, r, :] = nl.astype(BF16)
                    lb_buf[slot, h, r, :] = (z2 - nl) + mask_buf[diag, 1, r, :]

        def sums(t, slot):
            i, j = t
            vb = v_ref[rows(j), :]
            return (tuple(_dot(l_buf[slot, h], tri) for h in heads),
                    tuple(_dot_nt(doh_buf[h, rows(i), :], vb) for h in heads))

        def weights(t, slot, sm, da):
            i, j = t
            for h in heads:
                c = jnp.sum(jnp.where(lane == j, cl_ref[rows(i), h * LANES:(h + 1) * LANES], 0.0), axis=-1, keepdims=True)
                c = jnp.broadcast_to(c, (Q, LANES))
                for r in strips:
                    lb = lb_buf[slot, h, r, :]
                    a = jnp.exp2(lb + sm[h][r] + wide(c[r]))
                    g = da[h][r] * a
                    a_buf[slot, h, r, :] = a.astype(BF16)
                    be_buf[slot, h, r, :] = jnp.exp2(lb)
                    g_buf[slot, h, r, :] = g
                    gb_buf[slot, h, r, :] = g.astype(BF16)

        def prefix(t, slot):
            i, j = t
            dv_ref[rows(j), :] += (_dot_tn(a_buf[slot, 0], doh_buf[0, rows(i), :])
                                   + _dot_tn(a_buf[slot, 1], doh_buf[1, rows(i), :]))
            return tuple(_dot(gb_buf[slot, h], tpi) for h in heads)

        def dscores(t, slot, pm):
            i, j = t
            for h in heads:
                pg = pg_buf[h] * jnp.where(j == 0, 0.0, 1.0)
                for r in strips:
                    dz = g_buf[slot, h, r, :] - be_buf[slot, h, r, :] * (pm[h][r] + wide(pg[r]))
                    dz_buf[slot, h, r, :] = dz.astype(BF16)
                pg_buf[h] = pg + jnp.broadcast_to(pm[h][:, Q - 1:Q], (Q, LANES))

        def grads(t, slot):
            i, j = t
            dq_ref[rows(i), :] += (_dot(dz_buf[slot, 0], kh_buf[0, rows(j), :])
                                   + _dot(dz_buf[slot, 1], kh_buf[1, rows(j), :]))
            dk_ref[rows(j), :] += (_dot_tn(dz_buf[slot, 0], qh_buf[0, rows(i), :])
                                   + _dot_tn(dz_buf[slot, 1], qh_buf[1, rows(i), :]))

        def iteration(t, p):
            ta, tb, tc, td, te = t
            if ta is not None:
                scores(ta, p)
            if tc is not None:
                sm, da = sums(tc, p)
            if td is not None:
                pm = prefix(td, 1 - p)
            if te is not None:
                grads(te, p)
            if tb is not None:
                logs(tb, 1 - p)
            if tc is not None:
                weights(tc, p, sm, da)
            if td is not None:
                dscores(td, 1 - p, pm)

        def window(n):
            return tuple(tiles[n - k] if 0 <= n - k < ntiles else None for k in range(5))

        def following(t):
            i, j = t
            last = j == i
            return jnp.where(last, i + 1, i), jnp.where(last, 0, j + 1)

        peeled = 4 + (ntiles - 4) % unroll

        def unrolled_iterations(_, t):
            for n in range(peeled, peeled + unroll):
                iteration(t, n % 2)
                t = (following(t[0]),) + t[:4]
            return t

        pg_buf[...] = jnp.zeros_like(pg_buf)
        for n in range(peeled):
            iteration(window(n), n % 2)
        first_window = tuple((jnp.int32(i), jnp.int32(j)) for i, j in window(peeled))
        lax.fori_loop(0, (ntiles - peeled) // unroll, unrolled_iterations, first_window)
        for n in range(ntiles, ntiles + 4):
            iteration(window(n), n % 2)
        dq_ref[...] = dq_ref[...] * (1.0 / math.sqrt(SB_HEAD_DIM))

    col_block = lambda off: pl.BlockSpec((S, LANES), lambda p: (0, off + p), pipeline_mode=pl.Buffered(1))
    return pl.pallas_call(
        body, name="attn_bwd", grid=(npair,),
        out_shape=[jax.ShapeDtypeStruct((S, SB_WIDTH), F32)] * 3,
        in_specs=[col_block(0), col_block(npair), col_block(2 * npair), col_block(0),
                  pl.BlockSpec((S, 2 * LANES), lambda p: (0, p), pipeline_mode=pl.Buffered(1))],
        out_specs=[pl.BlockSpec((S, LANES), lambda p: (0, p), pipeline_mode=pl.Buffered(1))] * 3,
        scratch_shapes=[pltpu.VMEM((2, 2, Q, Q), F32)] * 4 + [pltpu.VMEM((2, 2, Q, Q), BF16)] * 4
        + [pltpu.VMEM((2, Q, LANES), F32), pltpu.VMEM((2, 2, Q, Q), F32)] + [pltpu.VMEM((2, S, LANES), BF16)] * 3,
        compiler_params=_cparams(dimension_semantics=("arbitrary",)),
    )(qkv, qkv, qkv, do, cl)


def _conv_bwd(u_conv, y_conv, dco, conv_w, ln_g, ln_b):
    S = u_conv.shape[0]
    tc = min(TOKEN_TILE, S)
    nt = S // tc
    per = tc // CONV_HALO
    groups = CONV_CHUNK // 8

    def body(u_ref, halo_ref, y_ref, dco_ref, cw_ref, lg_ref, lb_ref, du_ref, dcw_ref, dsm_ref, glu_ext, dyc_ext, sg_buf,
             dcw_acc, dsm_acc, glu_sh, dyc_sh):
        i = pl.program_id(0)
        ti = nt - 1 - i

        @pl.when(i == 0)
        def _():
            dyc_ext[tc:, :] = jnp.zeros((CONV_HALO, CONV_CH), F32)
            dcw_acc[...] = jnp.zeros_like(dcw_acc)
            dsm_acc[...] = jnp.zeros_like(dsm_acc)

        @pl.when(i > 0)
        def _():
            dyc_ext[tc:, :] = dyc_ext[0:CONV_HALO, :]

        glu_ext[0:CONV_HALO, :] = jnp.where(ti > 0, _glu(halo_ref[...])[2], 0.0)
        _, sg, glu = _glu(u_ref[...])
        glu_ext[CONV_HALO:, :] = glu
        sg_buf[...] = sg
        _shift_copies(glu_ext, glu_sh)

        dcb = jnp.zeros((8, CONV_CH), F32)
        dlg = jnp.zeros((8, CONV_CH), F32)
        dlb = jnp.zeros((8, CONV_CH), F32)
        fold = lambda t: jnp.sum(t.reshape(groups, 8, CONV_CH), axis=0)
        for r0 in range(0, tc, CONV_CHUNK):
            y = y_ref[r0:r0 + CONV_CHUNK, :]
            mu = jnp.mean(y, axis=-1, keepdims=True)
            yc = y - mu
            rstd = lax.rsqrt(jnp.mean(yc * yc, axis=-1, keepdims=True) + EPS)
            yn = yc * rstd
            yl = yn * lg_ref[...] + lb_ref[...]
            s = jax.nn.sigmoid(yl)
            dyl = dco_ref[r0:r0 + CONV_CHUNK, :] * (s * (1.0 + yl * (1.0 - s)))
            dlg = dlg + fold(dyl * yn)
            dlb = dlb + fold(dyl)
            wv = dyl * lg_ref[...]
            dyc = rstd * (wv - jnp.mean(wv, axis=-1, keepdims=True) - yn * jnp.mean(wv * yn, axis=-1, keepdims=True))
            dcb = dcb + fold(dyc)
            dyc_ext[r0:r0 + CONV_CHUNK, :] = dyc
        dsm_acc[0:8, :] += dcb
        dsm_acc[8:16, :] += dlg
        dsm_acc[16:24, :] += dlb
        _shift_copies(dyc_ext, dyc_sh)

        for r0 in range(0, tc, CONV_CHUNK):
            dyc = dyc_ext[r0:r0 + CONV_CHUNK, :]
            dglu = jnp.zeros((CONV_CHUNK, CONV_CH), F32)
            base = r0 + CONV_HALO - (CONV_WIDTH - 1)
            for w in range(CONV_WIDTH):
                back = r0 + (CONV_WIDTH - 1) - w
                dglu = dglu + cw_ref[w:w + 1, :] * _window(dyc_ext, dyc_sh, back, CONV_CHUNK)
                dcw_acc[8 * w:8 * w + 8, :] += fold(dyc * _window(glu_ext, glu_sh, base + w, CONV_CHUNK))
            sg = sg_buf[r0:r0 + CONV_CHUNK, :]
            v = u_ref[r0:r0 + CONV_CHUNK, :CONV_CH]
            du_ref[r0:r0 + CONV_CHUNK, :CONV_CH] = (dglu * sg).astype(BF16)
            du_ref[r0:r0 + CONV_CHUNK, CONV_CH:] = (dglu * v * sg * (1.0 - sg)).astype(BF16)

        @pl.when(i == nt - 1)
        def _():
            for w in range(CONV_WIDTH):
                dcw_ref[w:w + 1, :] = jnp.sum(dcw_acc[8 * w:8 * w + 8, :], axis=0, keepdims=True)
            dcw_ref[CONV_WIDTH:, :] = jnp.zeros((CONV_HALO - CONV_WIDTH, CONV_CH), F32)
            for k in range(3):
                dsm_ref[k:k + 1, :] = jnp.sum(dsm_acc[8 * k:8 * k + 8, :], axis=0, keepdims=True)
            dsm_ref[3:, :] = jnp.zeros((5, CONV_CH), F32)

    return pl.pallas_call(
        body, name="conv_bwd", grid=(nt,),
        out_shape=[jax.ShapeDtypeStruct((S, 2 * CONV_CH), BF16), jax.ShapeDtypeStruct((CONV_HALO, CONV_CH), F32),
                   jax.ShapeDtypeStruct((8, CONV_CH), F32)],
        in_specs=[pl.BlockSpec((tc, 2 * CONV_CH), lambda i: (nt - 1 - i, 0)),
                  pl.BlockSpec((CONV_HALO, 2 * CONV_CH), lambda i: (jnp.maximum((nt - 1 - i) * per - 1, 0), 0)),
                  pl.BlockSpec((tc, CONV_CH), lambda i: (nt - 1 - i, 0)), pl.BlockSpec((tc, CONV_CH), lambda i: (nt - 1 - i, 0)),
                  _const((CONV_HALO, CONV_CH)), _const((1, CONV_CH)), _const((1, CONV_CH))],
        out_specs=[pl.BlockSpec((tc, 2 * CONV_CH), lambda i: (nt - 1 - i, 0)), _const((CONV_HALO, CONV_CH)),
                   _const((8, CONV_CH))],
        scratch_shapes=[pltpu.VMEM((tc + CONV_HALO, CONV_CH), F32), pltpu.VMEM((tc + CONV_HALO, CONV_CH), F32),
                        pltpu.VMEM((tc, CONV_CH), F32), pltpu.VMEM((8 * CONV_HALO, CONV_CH), F32),
                        pltpu.VMEM((24, CONV_CH), F32)]
        + [pltpu.VMEM((SUBLANES - 1, tc + CONV_HALO - SUBLANES, CONV_CH), F32)] * 2,
        compiler_params=_cparams(dimension_semantics=("arbitrary",)),
    )(u_conv, u_conv, y_conv, dco, conv_w, ln_g, ln_b)


def _in_proj_bwd(du_conv, dq, dk, dv, w_in, x, g, dh1):
    S = x.shape[0]
    tm = min(TOKEN_TILE, S)
    nconv = 2 * CONV_CH

    def body(duc_ref, dq_ref, dk_ref, dv_ref, w_ref, x_ref, g_ref, dh1_ref, dx_ref, dg_ref):
        @pl.when(pl.program_id(0) == 0)
        def _():
            dg_ref[...] = jnp.zeros_like(dg_ref)

        da = _dot_nt(duc_ref[...], w_ref[:, :nconv])
        for n, ref in enumerate((dq_ref, dk_ref, dv_ref)):
            c0 = nconv + n * SB_WIDTH
            da = da + _dot_nt(ref[...].astype(BF16), w_ref[:, c0:c0 + SB_WIDTH])
        xf = x_ref[...]
        r = _rms_r(xf)
        dx_ref[...] = dh1_ref[...] + _rms_bwd(xf, r, g_ref[...], da)
        dg_ref[...] += jnp.sum(da * xf * r, axis=0, keepdims=True)

    row = lambda n: pl.BlockSpec((tm, n), lambda i: (i, 0))
    return pl.pallas_call(
        body, name="in_proj_bwd", grid=(S // tm,),
        out_shape=[jax.ShapeDtypeStruct((S, D_MODEL), F32), jax.ShapeDtypeStruct((1, D_MODEL), F32)],
        in_specs=[row(nconv), row(SB_WIDTH), row(SB_WIDTH), row(SB_WIDTH), _resident(w_in.shape), row(D_MODEL),
                  _const((1, D_MODEL)), row(D_MODEL)],
        out_specs=[row(D_MODEL), _const((1, D_MODEL))],
        compiler_params=_cparams(dimension_semantics=("arbitrary",)),
    )(du_conv, dq, dk, dv, w_in, x, g, dh1)


def _layer_grads(xs, target, g_pre_mix, w_in_f, conv_w_f, conv_b, conv_ln_g, conv_ln_b, attn_g, g_post_mix, g_pre_ffn,
                 g_post_ffn, late_weights, send_grads):
    a, u_conv, qkv = _in_proj(xs, g_pre_mix, w_in_f)
    conv_out, y_conv = _conv_fwd(u_conv, conv_w_f, conv_b, conv_ln_g, conv_ln_b)
    o, attn_out, cl = _attn_fwd(qkv, attn_g)
    w_out_f, w_gate_f, w_up_f, w_down_f = late_weights(attn_out)
    y, h1, f_in = _out_proj(conv_out, attn_out, w_out_f, xs, g_post_mix, g_pre_ffn)
    gt, up, df, dh2, loss_part, d_g_post_ffn = _ffn_fwd_loss(f_in, w_gate_f, w_up_f, w_down_f, h1, target, g_post_ffn)

    dgt, dup, act, dfin = _ffn_bwd(df, gt, up, w_gate_f, w_up_f, w_down_f)
    d_w_down = _matmul_tn("grad_w_down", act, df, 512)
    d_w_gate = _matmul_tn("grad_w_gate", f_in, dgt, FF_CHUNK)
    d_w_up = _matmul_tn("grad_w_up", f_in, dup, FF_CHUNK)
    sent = send_grads("ffn", (d_w_gate, d_w_up, d_w_down))
    dh1, dy, dco, do, d_g_pre_ffn, d_g_post_mix, d_attn_g = _mix_bwd(dfin, h1, y, dh2, g_pre_ffn + sent, g_post_mix,
                                                                     w_out_f, o, attn_g)
    d_w_out = jnp.concatenate([_matmul_tn("grad_w_out_conv", conv_out, dy, D_MODEL),
                               _matmul_tn("grad_w_out_attn", attn_out, dy, D_MODEL)], axis=0)
    sent = send_grads("w_out", (d_w_out,))
    dq, dk, dv = _attn_bwd(qkv, do, cl)
    du_conv, d_conv_w, d_conv_small = _conv_bwd(u_conv, y_conv, dco, conv_w_f, conv_ln_g + sent, conv_ln_b)
    d_w_in = jnp.concatenate([_matmul_tn("grad_w_in_conv", a, du_conv, 2 * CONV_CH),
                              _matmul_tn("grad_w_in_q", a, dq, SB_WIDTH), _matmul_tn("grad_w_in_k", a, dk, SB_WIDTH),
                              _matmul_tn("grad_w_in_v", a, dv, SB_WIDTH)], axis=1)
    sent = send_grads("w_in", (d_w_in,))
    grad_x, d_g_pre_mix = _in_proj_bwd(du_conv, dq, dk, dv, w_in_f, xs, g_pre_mix + sent, dh1)
    return (loss_part, grad_x, d_conv_w, d_conv_small, d_attn_g, d_g_pre_mix, d_g_post_mix, d_g_pre_ffn, d_g_post_ffn)


def _cols_to_blocks(w):
    K, N = w.shape
    return jnp.transpose(w.reshape(K, N_DEV, N // N_DEV), (1, 0, 2))


def _blocks_to_cols(blocks):
    n_dev, K, n = blocks.shape
    return jnp.transpose(blocks, (1, 0, 2)).reshape(K, n_dev * n)


def kernel(x, g_pre_mix, w_in, conv_w, conv_b, conv_ln_g, conv_ln_b, attn_norm_g, w_out, g_post_mix, g_pre_ffn, w_gate, w_up, w_down, g_post_ffn, loss_target, m_g_pre_mix, m_w_in, m_conv_w, m_conv_b, m_conv_ln_g, m_conv_ln_b, m_attn_norm_g, m_w_out, m_g_post_mix, m_g_pre_ffn, m_w_gate, m_w_up, m_w_down, m_g_post_ffn, v_g_pre_mix, v_w_in, v_conv_w, v_conv_b, v_conv_ln_g, v_conv_ln_b, v_attn_norm_g, v_w_out, v_g_post_mix, v_g_pre_ffn, v_w_gate, v_w_up, v_w_down, v_g_post_ffn):
    xs = x[0]
    target = loss_target[0]
    me = 4 * lax.axis_index("x") + 2 * lax.axis_index("y") + lax.axis_index("c")
    cw_shard = conv_w.reshape(CONV_WIDTH, CONV_CH // N_DEV)
    attn_g = attn_norm_g.reshape(1, SB_WIDTH)

    gathered = _all_gather([w_in[0].astype(BF16), cw_shard])
    w_in_f = _blocks_to_cols(gathered[0])
    conv_w_f = jnp.pad(_blocks_to_cols(gathered[1]), ((0, CONV_HALO - CONV_WIDTH), (0, 0)))
    gathered_zero = gathered[2][0:1, 0:1].astype(BF16)
    late = [w_out[0].astype(BF16) + gathered_zero, w_gate[0].astype(BF16), w_up[0].astype(BF16), w_down[0].astype(BF16)]
    late_started = _exchange_start("all_gather_late_start", late, scatter=False)

    def late_weights(after):
        lands = _exchange_wait("all_gather_late_wait", late_started, False, after)
        wo, wg, wu, wd = [lax.dynamic_update_index_in_dim(land, own, me, 0) for land, own in zip(lands, late)]
        return wo.reshape(D_MODEL, D_MODEL), _blocks_to_cols(wg), _blocks_to_cols(wu), wd.reshape(D_FF, D_MODEL)

    started = {}

    def send_grads(name, grads):
        blocks = [g.reshape(N_DEV, g.shape[0] // N_DEV, g.shape[1]) if g.shape[1] == D_MODEL else _cols_to_blocks(g)
                  for g in grads]
        payload = BF16 if name == "w_in" else F32
        sent = _exchange_start("reduce_scatter_" + name + "_start", [b.astype(payload) for b in blocks], scatter=True)
        started[name] = (sent, blocks)
        return sent[-1][0:1, 0:1]

    (loss_part, grad_x, d_conv_w, d_conv_small, d_attn_g, d_g_pre_mix, d_g_post_mix, d_g_pre_ffn,
     d_g_post_ffn) = _layer_grads(
        xs, target, g_pre_mix + late_started[-1][0:1, 0:1], w_in_f, conv_w_f, conv_b, conv_ln_g, conv_ln_b, attn_g,
        g_post_mix, g_pre_ffn, g_post_ffn, late_weights, send_grads)

    def reduced(name, after, shards):
        st, blocks = started[name]
        lands = _exchange_wait("reduce_scatter_" + name + "_wait", st, True, after)
        return [_sum_adamw("adamw_" + wn, land, lax.dynamic_index_in_dim(blk, me, 0, keepdims=False), w[0], m[0], v[0])
                for land, blk, (wn, w, m, v) in zip(lands, blocks, shards)]

    two = lambda t: t.reshape(2, CONV_CH)
    small_g = jnp.concatenate([
        d_conv_w,
        d_conv_small[0:3],
        d_attn_g,
        two(d_g_pre_mix), two(d_g_post_mix), two(d_g_pre_ffn), two(d_g_post_ffn),
        jnp.broadcast_to(loss_part[0:1, 0:1], (1, CONV_CH)),
        jnp.zeros((3, CONV_CH), F32)], axis=0)
    small_g = _all_reduce_small(small_g)
    loss = small_g[44, 0]
    g_conv_w = lax.dynamic_slice(small_g, (0, me * (CONV_CH // N_DEV)), (CONV_WIDTH, CONV_CH // N_DEV))
    pack = lambda cb, lg, lb, ag, g1, g2, g3, g4: jnp.concatenate(
        [cb, lg, lb, ag.reshape(1, SB_WIDTH), two(g1), two(g2), two(g3), two(g4), jnp.zeros((4, CONV_CH), F32)], axis=0)
    sm_g = small_g[CONV_HALO:]
    sm_delta, sm_m, sm_v = _adamw_small(
        "adamw_small",
        pack(conv_b, conv_ln_g, conv_ln_b, attn_norm_g, g_pre_mix, g_post_mix, g_pre_ffn, g_post_ffn), sm_g,
        pack(m_conv_b, m_conv_ln_g, m_conv_ln_b, m_attn_norm_g, m_g_pre_mix, m_g_post_mix, m_g_pre_ffn, m_g_post_ffn),
        pack(v_conv_b, v_conv_ln_g, v_conv_ln_b, v_attn_norm_g, v_g_pre_mix, v_g_post_mix, v_g_pre_ffn, v_g_post_ffn))
    cw_delta, cw_m, cw_v = _adamw_small("adamw_conv_w", cw_shard, g_conv_w,
                                        m_conv_w.reshape(cw_shard.shape), v_conv_w.reshape(cw_shard.shape))

    ffn = reduced("ffn", grad_x, [("w_gate", w_gate, m_w_gate, v_w_gate), ("w_up", w_up, m_w_up, v_w_up),
                                  ("w_down", w_down, m_w_down, v_w_down)])
    big = {"w_gate": ffn[0], "w_up": ffn[1], "w_down": ffn[2],
           "w_out": reduced("w_out", ffn[2][0], [("w_out", w_out, m_w_out, v_w_out)])[0]}
    big["w_in"] = reduced("w_in", big["w_out"][0], [("w_in", w_in, m_w_in, v_w_in)])[0]

    def unpack(t):
        return {"conv_b": t[0:1], "conv_ln_g": t[1:2], "conv_ln_b": t[2:3], "attn_norm_g": t[3:4].reshape(1, SB_HEADS, SB_HEAD_DIM),
                "g_pre_mix": t[4:6].reshape(1, D_MODEL), "g_post_mix": t[6:8].reshape(1, D_MODEL),
                "g_pre_ffn": t[8:10].reshape(1, D_MODEL), "g_post_ffn": t[10:12].reshape(1, D_MODEL)}

    names = ["g_pre_mix", "w_in", "conv_w", "conv_b", "conv_ln_g", "conv_ln_b", "attn_norm_g", "w_out", "g_post_mix",
             "g_pre_ffn", "w_gate", "w_up", "w_down", "g_post_ffn"]
    kinds = []
    for idx, small in enumerate((sm_g, sm_delta, sm_m, sm_v)):
        d = unpack(small)
        d["conv_w"] = (g_conv_w, cw_delta, cw_m, cw_v)[idx].reshape(1, CONV_WIDTH, 1, CONV_CH // N_DEV)
        for n in big:
            d[n] = big[n][idx][None]
        kinds.append([d[n] for n in names])

    return (loss, grad_x[None], *kinds[0], *kinds[1], *kinds[2], *kinds[3])
```

```python
import math

import jax
import jax.numpy as jnp
from jax import lax
from jax.experimental import pallas as pl
from jax.experimental.pallas import tpu as pltpu

F32 = jnp.float32
BF16 = jnp.bfloat16
MESH = pl.DeviceIdType.MESH

N_DEV = 8
D_MODEL = 1024
CONV_CH = 512
CONV_WIDTH = 31
SB_HEADS = 8
SB_HEAD_DIM = 64
SB_WIDTH = SB_HEADS * SB_HEAD_DIM
D_FF = 2816
EPS = 1e-6
LOG2E = 1.4426950408889634
Z2_MAX = 100.0
MASKED = -1e30
ADAM_LR = 0.001
ADAM_B1 = 0.9
ADAM_B2 = 0.999
ADAM_EPS = 1e-08
ADAM_WD = 0.01
ADAM_STEP = 10

SUBLANES = 8
LANES = 128
VMEM_LIMIT = 56 * 1024 * 1024
TOKEN_TILE = 512
GRAD_TILE = 1024
FFN_TILE = 256
ATTN_FWD_UNROLL = 8
ATTN_BWD_UNROLL = 4
ATTN_STRIP = 32
ATTN_BLOCK = 256
CONV_HALO = 32
CONV_CHUNK = 64
FF_CHUNK = D_FF // 2


def _cparams(**kw):
    return pltpu.CompilerParams(vmem_limit_bytes=VMEM_LIMIT, **kw)


def _resident(shape):
    return pl.BlockSpec(shape, lambda *_: (0,) * len(shape), pipeline_mode=pl.Buffered(1))


def _const(shape):
    return pl.BlockSpec(shape, lambda *_: (0,) * len(shape))


def _rms_r(xf):
    return lax.rsqrt(jnp.mean(xf * xf, axis=-1, keepdims=True) + EPS)


def _rms_bwd(xf, r, g, dout):
    w = dout * g
    return r * (w - xf * (r * r) * jnp.mean(w * xf, axis=-1, keepdims=True))


def _dot(a, b):
    return jnp.dot(a, b, preferred_element_type=F32)


def _dot_nt(a, b):
    return lax.dot_general(a, b, (((1,), (1,)), ((), ())), preferred_element_type=F32)


def _dot_tn(a, b):
    return lax.dot_general(a, b, (((0,), (0,)), ((), ())), preferred_element_type=F32)


def _peer(x, y, c, k):
    px = 1 - x if (k >> 2) & 1 else x
    py = 1 - y if (k >> 1) & 1 else y
    pc = 1 - c if k & 1 else c
    return (px, py, pc), 4 * px + 2 * py + pc


def _all_gather(shards):
    n = len(shards)

    def body(*refs):
        ins, outs, done = refs[:n], refs[n:2 * n], refs[2 * n]
        send_sems, recv_sems, local_sems = refs[2 * n + 1:]
        x, y, c = lax.axis_index("x"), lax.axis_index("y"), lax.axis_index("c")
        me, sibling = (x, y, c), (x, y, 1 - c)
        chips = [(1 - x, y), (x, 1 - y), (1 - x, 1 - y)]
        number = lambda d: 4 * d[0] + 2 * d[1] + d[2]

        def copy(a, k, block, to, src=None):
            rows = outs[a].at[number(block)]
            return pltpu.make_async_remote_copy(
                src_ref=rows if src is None else src, dst_ref=rows, send_sem=send_sems.at[a * (N_DEV - 1) + k],
                recv_sem=recv_sems.at[a * (N_DEV - 1) + k], device_id=to, device_id_type=MESH)

        copies = [pltpu.make_async_copy(ins[a], outs[a].at[number(me)], local_sems.at[a]) for a in range(n)]
        for mine in copies:
            mine.start()
        sent = [copy(a, 0, me, sibling, src=ins[a]) for a in range(n)]
        sent += [copy(a, 1 + j, me, (*chip, c), src=ins[a]) for j, chip in enumerate(chips) for a in range(n)]
        for cp in sent:
            cp.start()
        for j, chip in enumerate(chips):
            for a in range(n):
                copy(a, 1 + j, (*chip, c), me).wait_recv()
                passed = copy(a, 4 + j, (*chip, c), sibling)
                passed.start()
                sent.append(passed)
        for a in range(n):
            copy(a, 0, sibling, me).wait_recv()
            for j, chip in enumerate(chips):
                copy(a, 4 + j, (*chip, 1 - c), me).wait_recv()
        for cp in sent:
            cp.wait_send()
        for mine in copies:
            mine.wait()
        done[...] = jnp.zeros_like(done)

    any_spec = pl.BlockSpec(memory_space=pl.ANY)
    return pl.pallas_call(
        body, name="all_gather_weights",
        out_shape=[jax.ShapeDtypeStruct((N_DEV,) + s.shape, s.dtype) for s in shards] + [jax.ShapeDtypeStruct((8, LANES), F32)],
        in_specs=[any_spec] * n, out_specs=[any_spec] * n + [pl.BlockSpec(memory_space=pltpu.VMEM)],
        scratch_shapes=[pltpu.SemaphoreType.DMA((n * (N_DEV - 1),)), pltpu.SemaphoreType.DMA((n * (N_DEV - 1),)),
                        pltpu.SemaphoreType.DMA((n,))],
        compiler_params=pltpu.CompilerParams(has_side_effects=True),
    )(*shards)


def _adamw(w, g, m, v):
    m = ADAM_B1 * m + (1.0 - ADAM_B1) * g
    v = ADAM_B2 * v + (1.0 - ADAM_B2) * (g * g)
    m_hat = m / (1.0 - ADAM_B1 ** ADAM_STEP)
    v_hat = v / (1.0 - ADAM_B2 ** ADAM_STEP)
    delta = -ADAM_LR * (m_hat / (jnp.sqrt(v_hat) + ADAM_EPS) + ADAM_WD * w)
    return delta, m, v


def _exchange_and_sum(src_block, recv_ref, send_sems, recv_sems, local_sem):
    x, y, c = lax.axis_index("x"), lax.axis_index("y"), lax.axis_index("c")
    me = 4 * x + 2 * y + c
    mine = pltpu.make_async_copy(src_block(me), recv_ref.at[me], local_sem)
    mine.start()
    for k in range(1, N_DEV):
        peer, peer_block = _peer(x, y, c, k)
        pltpu.make_async_remote_copy(
            src_ref=src_block(peer_block), dst_ref=recv_ref.at[me], send_sem=send_sems.at[k - 1],
            recv_sem=recv_sems.at[k - 1], device_id=peer, device_id_type=MESH).start()
    for k in range(1, N_DEV):
        peer, peer_block = _peer(x, y, c, k)
        arrived = pltpu.make_async_remote_copy(
            src_ref=src_block(peer_block), dst_ref=recv_ref.at[peer_block], send_sem=send_sems.at[k - 1],
            recv_sem=recv_sems.at[k - 1], device_id=peer, device_id_type=MESH)
        arrived.wait_send()
        arrived.wait_recv()
    mine.wait()


HBM_SPEC = pl.BlockSpec(memory_space=pltpu.HBM)
SEM_SPEC = pl.BlockSpec(memory_space=pltpu.SEMAPHORE)
DATAFLOW = pltpu.SideEffectType.DATAFLOW_SIDE_EFFECTING


def _exchange_copies(srcs, lands, send_sems, recv_sems, scatter, wait):
    x, y, c = lax.axis_index("x"), lax.axis_index("y"), lax.axis_index("c")
    me = 4 * x + 2 * y + c
    for k in range(1, N_DEV):
        peer, peer_block = _peer(x, y, c, k)
        for a in range(len(srcs)):
            s = a * (N_DEV - 1) + k - 1
            src = srcs[a].at[peer_block] if scatter else srcs[a]
            copy = pltpu.make_async_remote_copy(
                src_ref=src, dst_ref=lands[a].at[peer_block if wait else me], send_sem=send_sems.at[s],
                recv_sem=recv_sems.at[s], device_id=peer, device_id_type=MESH)
            if wait:
                copy.wait_send()
                copy.wait_recv()
            else:
                copy.start()


def _exchange_start(name, arrays, scatter):
    n = len(arrays)
    land_shapes = [a.shape if scatter else (N_DEV,) + a.shape for a in arrays]

    def body(*refs):
        _exchange_copies(refs[:n], refs[n:2 * n], refs[2 * n], refs[2 * n + 1], scatter, wait=False)
        refs[-1][...] = jnp.zeros_like(refs[-1])

    sems = pltpu.SemaphoreType.DMA((n * (N_DEV - 1),))
    hbm = lambda t: pltpu.with_memory_space_constraint(t, pltpu.HBM)
    return pl.pallas_call(
        body, name=name,
        out_shape=(sems, sems, *[pltpu.HBM(a.shape, a.dtype) for a in arrays],
                   *[pltpu.HBM(ls, a.dtype) for ls, a in zip(land_shapes, arrays)], jax.ShapeDtypeStruct((8, LANES), F32)),
        in_specs=[HBM_SPEC] * (2 * n),
        out_specs=(SEM_SPEC, SEM_SPEC, *[HBM_SPEC] * (2 * n), pl.BlockSpec(memory_space=pltpu.VMEM)),
        input_output_aliases={a: 2 + a for a in range(2 * n)},
        compiler_params=pltpu.CompilerParams(has_side_effects=DATAFLOW),
    )(*[hbm(a) for a in arrays], *[hbm(lax.empty(ls, a.dtype)) for ls, a in zip(land_shapes, arrays)])


def _exchange_wait(name, started, scatter, after):
    n = (len(started) - 3) // 2
    send_sems, recv_sems = started[0], started[1]
    arrays, lands = started[2:2 + n], started[2 + n:2 + 2 * n]

    def body(*refs):
        _exchange_copies(refs[:n], refs[n:2 * n], refs[2 * n], refs[2 * n + 1], scatter, wait=True)

    return pl.pallas_call(
        body, name=name,
        out_shape=[pltpu.HBM(t.shape, t.dtype) for t in (*arrays, *lands)],
        in_specs=[HBM_SPEC] * (2 * n) + [SEM_SPEC, SEM_SPEC, pl.BlockSpec(memory_space=pl.ANY)],
        out_specs=[HBM_SPEC] * (2 * n),
        input_output_aliases={a: a for a in range(2 * n)},
        compiler_params=pltpu.CompilerParams(has_side_effects=DATAFLOW),
    )(*arrays, *lands, send_sems, recv_sems, after)[n:]


def _sum_adamw(name, land, own, w, m, v):
    _, M, N = land.shape
    rows = math.gcd(M, 128)

    def body(land_ref, own_ref, w_ref, m_ref, v_ref, grad_ref, delta_ref, nm_ref, nv_ref):
        x, y, c = lax.axis_index("x"), lax.axis_index("y"), lax.axis_index("c")
        g = own_ref[...]
        for k in range(1, N_DEV):
            g = g + land_ref[_peer(x, y, c, k)[1]].astype(F32)
        delta, nm, nv = _adamw(w_ref[...], g, m_ref[...], v_ref[...])
        grad_ref[...] = g
        delta_ref[...] = delta
        nm_ref[...] = nm
        nv_ref[...] = nv

    row = pl.BlockSpec((rows, N), lambda i: (i, 0))
    return pl.pallas_call(
        body, name=name, grid=(M // rows,), out_shape=[jax.ShapeDtypeStruct((M, N), F32)] * 4,
        in_specs=[pl.BlockSpec((N_DEV, rows, N), lambda i: (0, i, 0)), row, row, row, row], out_specs=[row] * 4,
        compiler_params=_cparams(),
    )(land, own, w, m, v)


def _all_reduce_small(g):
    R, C = g.shape

    def body(g_ref, out_ref, recv_ref, send_sems, recv_sems, local_sem):
        _exchange_and_sum(lambda b: g_ref, recv_ref, send_sems, recv_sems, local_sem)
        total = recv_ref[0]
        for b in range(1, N_DEV):
            total = total + recv_ref[b]
        out_ref[...] = total

    vmem = pl.BlockSpec(memory_space=pltpu.VMEM)
    return pl.pallas_call(
        body, name="all_reduce_small_grads", out_shape=jax.ShapeDtypeStruct((R, C), F32),
        in_specs=[vmem], out_specs=vmem,
        scratch_shapes=[pltpu.VMEM((N_DEV, R, C), F32), pltpu.SemaphoreType.DMA((N_DEV - 1,)),
                        pltpu.SemaphoreType.DMA((N_DEV - 1,)), pltpu.SemaphoreType.DMA(())],
        compiler_params=_cparams(has_side_effects=True),
    )(g)


def _adamw_small(name, w, g, m, v):
    def body(w_ref, g_ref, m_ref, v_ref, delta_ref, nm_ref, nv_ref):
        delta, nm, nv = _adamw(w_ref[...], g_ref[...], m_ref[...], v_ref[...])
        delta_ref[...] = delta
        nm_ref[...] = nm
        nv_ref[...] = nv

    vmem = pl.BlockSpec(memory_space=pltpu.VMEM)
    return pl.pallas_call(body, name=name, out_shape=[jax.ShapeDtypeStruct(w.shape, F32)] * 3,
                          in_specs=[vmem] * 4, out_specs=[vmem] * 3)(w, g, m, v)


def _in_proj(x, g, w_in):
    S = x.shape[0]
    tm = min(TOKEN_TILE, S)
    nconv = 2 * CONV_CH

    def body(x_ref, g_ref, w_ref, a_ref, uc_ref, qkv_ref):
        xf = x_ref[...]
        a = (xf * _rms_r(xf) * g_ref[...]).astype(BF16)
        a_ref[...] = a
        uc_ref[...] = _dot(a, w_ref[:, :nconv])
        qkv_ref[:, :SB_WIDTH] = (_dot(a, w_ref[:, nconv:nconv + SB_WIDTH]) * (1.0 / math.sqrt(SB_HEAD_DIM))).astype(BF16)
        qkv_ref[:, SB_WIDTH:] = _dot(a, w_ref[:, nconv + SB_WIDTH:]).astype(BF16)

    row = lambda n: pl.BlockSpec((tm, n), lambda i: (i, 0))
    return pl.pallas_call(
        body, name="in_proj", grid=(S // tm,),
        out_shape=[jax.ShapeDtypeStruct((S, D_MODEL), BF16), jax.ShapeDtypeStruct((S, nconv), F32),
                   jax.ShapeDtypeStruct((S, 3 * SB_WIDTH), BF16)],
        in_specs=[row(D_MODEL), _const((1, D_MODEL)), _resident(w_in.shape)],
        out_specs=[row(D_MODEL), row(nconv), row(3 * SB_WIDTH)],
        compiler_params=_cparams(),
    )(x, g, w_in)


def _glu(u):
    val, gate = u[:, :CONV_CH], u[:, CONV_CH:]
    sg = jax.nn.sigmoid(gate)
    return val, sg, val * sg


def _shift_copies(ext, shifted):
    n = shifted.shape[1]
    for r in range(1, SUBLANES):
        shifted[r - 1] = ext[r:r + n, :]


def _window(ext, shifted, start, rows):
    r = start % SUBLANES
    return ext[start:start + rows, :] if r == 0 else shifted[r - 1, start - r:start - r + rows, :]


def _conv_rows(glu_ext, glu_sh, cw_ref, r0, rows):
    base = r0 + CONV_HALO - (CONV_WIDTH - 1)
    acc = cw_ref[0:1, :] * _window(glu_ext, glu_sh, base, rows)
    for w in range(1, CONV_WIDTH):
        acc = acc + cw_ref[w:w + 1, :] * _window(glu_ext, glu_sh, base + w, rows)
    return acc


def _conv_fwd(u_conv, conv_w, conv_b, ln_g, ln_b):
    S = u_conv.shape[0]
    tc = min(TOKEN_TILE, S)

    def body(u_ref, cw_ref, cb_ref, lg_ref, lb_ref, out_ref, y_ref, glu_ext, glu_sh):
        i = pl.program_id(0)

        @pl.when(i == 0)
        def _():
            glu_ext[0:CONV_HALO, :] = jnp.zeros((CONV_HALO, CONV_CH), F32)

        @pl.when(i > 0)
        def _():
            glu_ext[0:CONV_HALO, :] = glu_ext[tc:tc + CONV_HALO, :]

        glu_ext[CONV_HALO:, :] = _glu(u_ref[...])[2]
        _shift_copies(glu_ext, glu_sh)
        for r0 in range(0, tc, CONV_CHUNK):
            y = _conv_rows(glu_ext, glu_sh, cw_ref, r0, CONV_CHUNK) + cb_ref[...]
            y_ref[r0:r0 + CONV_CHUNK, :] = y
            mu = jnp.mean(y, axis=-1, keepdims=True)
            yc = y - mu
            yn = yc * lax.rsqrt(jnp.mean(yc * yc, axis=-1, keepdims=True) + EPS)
            yl = yn * lg_ref[...] + lb_ref[...]
            out_ref[r0:r0 + CONV_CHUNK, :] = (yl * jax.nn.sigmoid(yl)).astype(BF16)

    return pl.pallas_call(
        body, name="conv_fwd", grid=(S // tc,),
        out_shape=[jax.ShapeDtypeStruct((S, CONV_CH), BF16), jax.ShapeDtypeStruct((S, CONV_CH), F32)],
        in_specs=[pl.BlockSpec((tc, 2 * CONV_CH), lambda i: (i, 0)), _const((CONV_HALO, CONV_CH)),
                  _const((1, CONV_CH)), _const((1, CONV_CH)), _const((1, CONV_CH))],
        out_specs=[pl.BlockSpec((tc, CONV_CH), lambda i: (i, 0))] * 2,
        scratch_shapes=[pltpu.VMEM((tc + CONV_HALO, CONV_CH), F32),
                        pltpu.VMEM((SUBLANES - 1, tc + CONV_HALO - SUBLANES, CONV_CH), F32)],
        compiler_params=_cparams(dimension_semantics=("arbitrary",)),
    )(u_conv, conv_w, conv_b, ln_g, ln_b)


def _head_masks():
    lane = lax.broadcasted_iota(jnp.int32, (1, LANES), 1)
    return lane < SB_HEAD_DIM


def _split_heads(t, first):
    z = jnp.zeros_like(t)
    return jnp.where(first, t, z), jnp.where(first, z, t)


def _split_heads_into(halves, src, first, nblocks, rows):
    def one_block(b, carry):
        r = pl.ds(pl.multiple_of(b * rows, rows), rows)
        halves[0, r, :], halves[1, r, :] = _split_heads(src[r, :], first)
        return carry

    lax.fori_loop(0, nblocks, one_block, 0)


def _head_sum(t, first):
    a = jnp.sum(jnp.where(first, t, 0.0), axis=-1, keepdims=True)
    b = jnp.sum(jnp.where(first, 0.0, t), axis=-1, keepdims=True)
    return a, b


def _attn_fwd(qkv, g_attn):
    S = qkv.shape[0]
    Q = min(ATTN_BLOCK, S)
    nq = S // Q
    assert nq <= LANES
    ntiles = nq * (nq + 1) // 2
    unroll = ATTN_FWD_UNROLL
    assert unroll % 2 == 0 and ntiles >= 3 + unroll
    npair = SB_WIDTH // LANES
    tiles = [(i, j) for i in range(nq) for j in range(i, -1, -1)]

    def body(q_ref, k_ref, v_ref, g_ref, o_ref, ao_ref, cl_ref, l_buf, z2_buf, a_buf, c_buf, mask_buf, qh_buf,
             vh_buf):
        first = _head_masks()
        lane = lax.broadcasted_iota(jnp.int32, (1, LANES), 1)
        row = lax.broadcasted_iota(jnp.int32, (Q, Q), 0)
        col = lax.broadcasted_iota(jnp.int32, (Q, Q), 1)
        tri = jnp.where(row >= col, -1.0, 0.0).astype(BF16)
        heads = range(2)
        strips = [slice(r0, r0 + ATTN_STRIP) for r0 in range(0, Q, ATTN_STRIP)]
        rows = lambda j: pl.ds(pl.multiple_of(j * Q, Q), Q)
        wide = lambda t: jnp.tile(t, (1, Q // LANES))
        as_int = lambda t: int(t) if isinstance(t, (bool, int)) else t.astype(jnp.int32)

        keep = col < row
        mask_buf[0, 0] = jnp.full((Q, Q), LOG2E, F32)
        mask_buf[0, 1] = jnp.zeros((Q, Q), F32)
        mask_buf[1, 0] = jnp.where(keep, LOG2E, 0.0)
        mask_buf[1, 1] = jnp.where(keep, 0.0, MASKED)
        o_ref[...] = jnp.zeros_like(o_ref)
        _split_heads_into(qh_buf, q_ref, first, nq, Q)
        _split_heads_into(vh_buf, v_ref, first, nq, Q)

        def scores(t):
            i, j = t
            kb = k_ref[rows(j), :]
            return tuple(_dot_nt(qh_buf[h, rows(i), :], kb) for h in heads)

        def logs(t, slot, z):
            i, j = t
            diag = as_int(i == j)
            for h in heads:
                for r in strips:
                    z2 = jnp.minimum(z[h][r] * LOG2E, Z2_MAX)
                    nl = jnp.log(1.0 + jnp.exp2(z2)) * mask_buf[diag, 0, r, :]
                    l_buf[slot, h, r, :] = nl.astype(BF16)
                    z2_buf[slot, h, r, :] = z2 + mask_buf[diag, 1, r, :]

        def sums(slot):
            return tuple(_dot(l_buf[slot, h], tri) for h in heads)

        def weights(t, slot, sm):
            i, j = t
            running = jnp.where(j == i, 0.0, 1.0)
            for h in heads:
                before = c_buf[h] * running
                for r in strips:
                    a_buf[slot, h, r, :] = jnp.exp2(z2_buf[slot, h, r, :] + sm[h][r] + wide(before[r])).astype(BF16)
                hl = slice(h * LANES, (h + 1) * LANES)
                cl_ref[rows(i), hl] = jnp.where(lane == j, before, cl_ref[rows(i), hl])
                c_buf[h] = before + jnp.broadcast_to(sm[h][:, 0:1], (Q, LANES))

        def values(t, slot):
            i, j = t
            o_ref[rows(i), :] += _dot(a_buf[slot, 0], vh_buf[0, rows(j), :]) + _dot(a_buf[slot, 1], vh_buf[1, rows(j), :])

        def iteration(t, p):
            ta, tc, td = t
            if ta is not None:
                z = scores(ta)
            if tc is not None:
                sm = sums(1 - p)
            if td is not None:
                values(td, p)
            if tc is not None:
                weights(tc, 1 - p, sm)
            if ta is not None:
                logs(ta, p, z)

        def window(n):
            return tuple(tiles[n - k] if 0 <= n - k < ntiles else None for k in range(3))

        def following(t):
            i, j = t
            last = j == 0
            return jnp.where(last, i + 1, i), jnp.where(last, i + 1, j - 1)

        peeled = 3 + (ntiles - 3) % unroll

        def unrolled_iterations(_, t):
            for n in range(peeled, peeled + unroll):
                iteration(t, n % 2)
                t = (following(t[0]),) + t[:2]
            return t

        c_buf[...] = jnp.zeros_like(c_buf)
        for n in range(peeled):
            iteration(window(n), n % 2)
        first_window = tuple((jnp.int32(i), jnp.int32(j)) for i, j in window(peeled))
        lax.fori_loop(0, (ntiles - peeled) // unroll, unrolled_iterations, first_window)
        for n in range(ntiles, ntiles + 2):
            iteration(window(n), n % 2)

        def head_norm(b, carry):
            o = o_ref[rows(b), :]
            sa, sb = _head_sum(o * o, first)
            r = jnp.where(first, lax.rsqrt(sa * (1.0 / SB_HEAD_DIM) + EPS), lax.rsqrt(sb * (1.0 / SB_HEAD_DIM) + EPS))
            ao_ref[rows(b), :] = (o * r * g_ref[...]).astype(BF16)
            return carry

        lax.fori_loop(0, nq, head_norm, 0)

    col_block = lambda off: pl.BlockSpec((S, LANES), lambda p: (0, off + p), pipeline_mode=pl.Buffered(1))
    out_block = lambda n: pl.BlockSpec((S, n), lambda p: (0, p), pipeline_mode=pl.Buffered(1))
    return pl.pallas_call(
        body, name="attn_fwd", grid=(npair,),
        out_shape=[jax.ShapeDtypeStruct((S, SB_WIDTH), F32), jax.ShapeDtypeStruct((S, SB_WIDTH), BF16),
                   jax.ShapeDtypeStruct((S, 2 * SB_WIDTH), F32)],
        in_specs=[col_block(0), col_block(npair), col_block(2 * npair), pl.BlockSpec((1, LANES), lambda p: (0, p))],
        out_specs=[out_block(LANES), out_block(LANES), out_block(2 * LANES)],
        scratch_shapes=[pltpu.VMEM((2, 2, Q, Q), BF16), pltpu.VMEM((2, 2, Q, Q), F32),
                        pltpu.VMEM((2, 2, Q, Q), BF16), pltpu.VMEM((2, Q, LANES), F32), pltpu.VMEM((2, 2, Q, Q), F32),
                        pltpu.VMEM((2, S, LANES), BF16), pltpu.VMEM((2, S, LANES), BF16)],
        compiler_params=_cparams(dimension_semantics=("arbitrary",)),
    )(qkv, qkv, qkv, g_attn)


def _out_proj(conv_out, attn_out, w_out, x, g_post_mix, g_pre_ffn):
    S = x.shape[0]
    tm = min(TOKEN_TILE, S)

    def body(co_ref, ao_ref, w_ref, x_ref, g1_ref, g2_ref, y_ref, h1_ref, fin_ref):
        y = _dot(co_ref[...], w_ref[:CONV_CH, :]) + _dot(ao_ref[...], w_ref[CONV_CH:, :])
        h1 = x_ref[...] + y * _rms_r(y) * g1_ref[...]
        y_ref[...] = y
        h1_ref[...] = h1
        fin_ref[...] = (h1 * _rms_r(h1) * g2_ref[...]).astype(BF16)

    row = lambda n: pl.BlockSpec((tm, n), lambda i: (i, 0))
    return pl.pallas_call(
        body, name="out_proj", grid=(S // tm,),
        out_shape=[jax.ShapeDtypeStruct((S, D_MODEL), F32), jax.ShapeDtypeStruct((S, D_MODEL), F32),
                   jax.ShapeDtypeStruct((S, D_MODEL), BF16)],
        in_specs=[row(CONV_CH), row(SB_WIDTH), _resident(w_out.shape), row(D_MODEL), _const((1, D_MODEL)),
                  _const((1, D_MODEL))],
        out_specs=[row(D_MODEL)] * 3,
        compiler_params=_cparams(),
    )(conv_out, attn_out, w_out, x, g_post_mix, g_pre_ffn)


def _ffn_fwd_loss(f_in, w_gate, w_up, w_down, h1, target, g_post_ffn):
    S = f_in.shape[0]
    tm = min(TOKEN_TILE, S)
    nt = S // tm

    def body(fin_ref, wg_ref, wu_ref, wd_ref, h1_ref, t_ref, g_ref, gt_ref, up_ref, df_ref, dh2_ref, loss_ref, dg_ref,
             sq_acc):
        i = pl.program_id(0)

        @pl.when(i == 0)
        def _():
            sq_acc[...] = jnp.zeros_like(sq_acc)
            dg_ref[...] = jnp.zeros_like(dg_ref)

        fin = fin_ref[...]
        f = jnp.zeros((tm, D_MODEL), F32)
        for c0 in range(0, D_FF, FF_CHUNK):
            cols = slice(c0, c0 + FF_CHUNK)
            gt = _dot(fin, wg_ref[:, cols])
            up = _dot(fin, wu_ref[:, cols])
            gt_ref[:, cols] = gt.astype(BF16)
            up_ref[:, cols] = up.astype(BF16)
            f = f + _dot((gt * jax.nn.sigmoid(gt) * up).astype(BF16), wd_ref[cols, :])
        r = _rms_r(f)
        g = g_ref[...]
        diff = h1_ref[...] + f * r * g - t_ref[...]
        sq_acc[...] += jnp.sum(diff * diff, axis=0, keepdims=True)
        dh2 = diff * (1.0 / D_MODEL)
        dh2_ref[...] = dh2
        dg_ref[...] += jnp.sum(dh2 * f * r, axis=0, keepdims=True)
        df_ref[...] = _rms_bwd(f, r, g, dh2).astype(BF16)

        @pl.when(i == nt - 1)
        def _():
            loss_ref[...] = jnp.broadcast_to((0.5 / D_MODEL) * jnp.sum(sq_acc[...], axis=-1, keepdims=True), (1, LANES))

    row = lambda n: pl.BlockSpec((tm, n), lambda i: (i, 0))
    return pl.pallas_call(
        body, name="ffn_fwd_loss", grid=(nt,),
        out_shape=[jax.ShapeDtypeStruct((S, D_FF), BF16), jax.ShapeDtypeStruct((S, D_FF), BF16),
                   jax.ShapeDtypeStruct((S, D_MODEL), BF16), jax.ShapeDtypeStruct((S, D_MODEL), F32),
                   jax.ShapeDtypeStruct((1, LANES), F32), jax.ShapeDtypeStruct((1, D_MODEL), F32)],
        in_specs=[row(D_MODEL), _resident(w_gate.shape), _resident(w_up.shape), _resident(w_down.shape),
                  row(D_MODEL), row(D_MODEL), _const((1, D_MODEL))],
        out_specs=[row(D_FF), row(D_FF), row(D_MODEL), row(D_MODEL), _const((1, LANES)), _const((1, D_MODEL))],
        scratch_shapes=[pltpu.VMEM((1, D_MODEL), F32)],
        compiler_params=_cparams(dimension_semantics=("arbitrary",)),
    )(f_in, w_gate, w_up, w_down, h1, target, g_post_ffn)


def _ffn_bwd(df, gt, up, w_gate, w_up, w_down):
    S = df.shape[0]
    tm = min(FFN_TILE, S)

    def body(df_ref, gt_ref, up_ref, wg_ref, wu_ref, wd_ref, dgt_ref, dup_ref, act_ref, dfin_ref):
        df = df_ref[...]
        dfin = jnp.zeros((tm, D_MODEL), F32)
        for c0 in range(0, D_FF, FF_CHUNK):
            cols = slice(c0, c0 + FF_CHUNK)
            dact = _dot_nt(df, wd_ref[cols, :])
            gt = gt_ref[:, cols].astype(F32)
            up = up_ref[:, cols].astype(F32)
            s = jax.nn.sigmoid(gt)
            silu = gt * s
            dgt = (dact * up * (s * (1.0 + gt * (1.0 - s)))).astype(BF16)
            dup = (dact * silu).astype(BF16)
            act_ref[:, cols] = (silu * up).astype(BF16)
            dgt_ref[:, cols] = dgt
            dup_ref[:, cols] = dup
            dfin = dfin + _dot_nt(dgt, wg_ref[:, cols]) + _dot_nt(dup, wu_ref[:, cols])
        dfin_ref[...] = dfin

    row = lambda n: pl.BlockSpec((tm, n), lambda i: (i, 0))
    return pl.pallas_call(
        body, name="ffn_bwd", grid=(S // tm,),
        out_shape=[jax.ShapeDtypeStruct((S, D_FF), BF16)] * 3 + [jax.ShapeDtypeStruct((S, D_MODEL), F32)],
        in_specs=[row(D_MODEL), row(D_FF), row(D_FF), _resident(w_gate.shape), _resident(w_up.shape),
                  _resident(w_down.shape)],
        out_specs=[row(D_FF)] * 3 + [row(D_MODEL)],
        compiler_params=_cparams(),
    )(df, gt, up, w_gate, w_up, w_down)


def _matmul_tn(name, x, y, tn):
    S, K = x.shape
    N = y.shape[1]
    ts = min(GRAD_TILE, S)

    def body(x_ref, y_ref, o_ref):
        @pl.when(pl.program_id(1) == 0)
        def _():
            o_ref[...] = jnp.zeros_like(o_ref)

        o_ref[...] += _dot_tn(x_ref[...].astype(BF16), y_ref[...].astype(BF16))

    return pl.pallas_call(
        body, name=name, grid=(N // tn, S // ts),
        out_shape=jax.ShapeDtypeStruct((K, N), F32),
        in_specs=[pl.BlockSpec((ts, K), lambda n, s: (s, 0)), pl.BlockSpec((ts, tn), lambda n, s: (s, n))],
        out_specs=pl.BlockSpec((K, tn), lambda n, s: (0, n)),
        compiler_params=_cparams(dimension_semantics=("arbitrary", "arbitrary")),
    )(x, y)


def _mix_bwd(dfin, h1, y, dh2, g_pre_ffn, g_post_mix, w_out, o, g_attn):
    S = dfin.shape[0]
    tm = min(TOKEN_TILE, S)
    inv_dh = 1.0 / SB_HEAD_DIM

    def body(dfin_ref, h1_ref, y_ref, dh2_ref, g2_ref, g1_ref, w_ref, o_ref, ga_ref, dh1_ref, dy_ref, dco_ref, do_ref,
             dg2_ref, dg1_ref, dga_ref):
        @pl.when(pl.program_id(0) == 0)
        def _():
            dg2_ref[...] = jnp.zeros_like(dg2_ref)
            dg1_ref[...] = jnp.zeros_like(dg1_ref)
            dga_ref[...] = jnp.zeros_like(dga_ref)

        h1, dfin = h1_ref[...], dfin_ref[...]
        r2 = _rms_r(h1)
        dh1 = dh2_ref[...] + _rms_bwd(h1, r2, g2_ref[...], dfin)
        dg2_ref[...] += jnp.sum(dfin * h1 * r2, axis=0, keepdims=True)
        y = y_ref[...]
        r1 = _rms_r(y)
        dy = _rms_bwd(y, r1, g1_ref[...], dh1).astype(BF16)
        dg1_ref[...] += jnp.sum(dh1 * y * r1, axis=0, keepdims=True)
        dh1_ref[...] = dh1
        dy_ref[...] = dy
        dco_ref[...] = _dot_nt(dy, w_ref[:CONV_CH, :])
        dao_all = _dot_nt(dy, w_ref[CONV_CH:, :])
        first = _head_masks()
        for p in range(SB_WIDTH // LANES):
            cols = slice(p * LANES, (p + 1) * LANES)
            o, dao, g = o_ref[:, cols], dao_all[:, cols], ga_ref[:, cols]
            sa, sb = _head_sum(o * o, first)
            r = jnp.where(first, lax.rsqrt(sa * inv_dh + EPS), lax.rsqrt(sb * inv_dh + EPS))
            w = dao * g
            wa, wb = _head_sum(w * o, first)
            do_ref[:, cols] = (r * (w - o * (r * r) * (jnp.where(first, wa, wb) * inv_dh))).astype(BF16)
            dga_ref[:, cols] += jnp.sum(dao * o * r, axis=0, keepdims=True)

    row = lambda n: pl.BlockSpec((tm, n), lambda i: (i, 0))
    return pl.pallas_call(
        body, name="mix_bwd", grid=(S // tm,),
        out_shape=[jax.ShapeDtypeStruct((S, D_MODEL), F32), jax.ShapeDtypeStruct((S, D_MODEL), BF16),
                   jax.ShapeDtypeStruct((S, CONV_CH), F32), jax.ShapeDtypeStruct((S, SB_WIDTH), BF16),
                   jax.ShapeDtypeStruct((1, D_MODEL), F32), jax.ShapeDtypeStruct((1, D_MODEL), F32),
                   jax.ShapeDtypeStruct((1, SB_WIDTH), F32)],
        in_specs=[row(D_MODEL)] * 4 + [_const((1, D_MODEL)), _const((1, D_MODEL)), _resident(w_out.shape), row(SB_WIDTH),
                  _const((1, SB_WIDTH))],
        out_specs=[row(D_MODEL), row(D_MODEL), row(CONV_CH), row(SB_WIDTH), _const((1, D_MODEL)), _const((1, D_MODEL)),
                   _const((1, SB_WIDTH))],
        compiler_params=_cparams(dimension_semantics=("arbitrary",)),
    )(dfin, h1, y, dh2, g_pre_ffn, g_post_mix, w_out, o, g_attn)


def _attn_bwd(qkv, do, cl):
    S = qkv.shape[0]
    Q = min(ATTN_BLOCK, S)
    nq = S // Q
    ntiles = nq * (nq + 1) // 2
    unroll = ATTN_BWD_UNROLL
    assert unroll % 2 == 0 and ntiles >= 3 + unroll
    npair = SB_WIDTH // LANES
    tiles = [(i, j) for i in range(nq) for j in range(i + 1)]

    def body(q_ref, k_ref, v_ref, do_ref, cl_ref, dq_ref, dk_ref, dv_ref,
             lb_buf, be_buf, g_buf, l_buf, a_buf, gb_buf, dz_buf, pg_buf, mask_buf, qh_buf, kh_buf, doh_buf):
        first = _head_masks()
        lane = lax.broadcasted_iota(jnp.int32, (1, LANES), 1)
        row = lax.broadcasted_iota(jnp.int32, (Q, Q), 0)
        col = lax.broadcasted_iota(jnp.int32, (Q, Q), 1)
        tri = jnp.where(row > col, -1.0, 0.0).astype(BF16)
        tpi = (row <= col).astype(BF16)
        heads = range(2)
        strips = [slice(r0, r0 + ATTN_STRIP) for r0 in range(0, Q, ATTN_STRIP)]
        rows = lambda j: pl.ds(pl.multiple_of(j * Q, Q), Q)
        wide = lambda t: jnp.tile(t, (1, Q // LANES))
        as_int = lambda t: int(t) if isinstance(t, (bool, int)) else t.astype(jnp.int32)

        keep = col < row
        mask_buf[0, 0] = jnp.full((Q, Q), LOG2E, F32)
        mask_buf[0, 1] = jnp.zeros((Q, Q), F32)
        mask_buf[1, 0] = jnp.where(keep, LOG2E, 0.0)
        mask_buf[1, 1] = jnp.where(keep, 0.0, MASKED)
        dq_ref[...] = jnp.zeros_like(dq_ref)
        dk_ref[...] = jnp.zeros_like(dk_ref)
        dv_ref[...] = jnp.zeros_like(dv_ref)
        _split_heads_into(qh_buf, q_ref, first, nq, Q)
        _split_heads_into(kh_buf, k_ref, first, nq, Q)
        _split_heads_into(doh_buf, do_ref, first, nq, Q)

        def scores(t):
            i, j = t
            kb = k_ref[rows(j), :]
            return tuple(_dot_nt(qh_buf[h, rows(i), :], kb) for h in heads)

        def logs(t, slot, z):
            i, j = t
            diag = as_int(i == j)
            for h in heads:
                for r in strips:
                    z2 = jnp.minimum(z[h][r] * LOG2E, Z2_MAX)
                    nl = jnp.log(1.0 + jnp.exp2(z2)) * mask_buf[diag, 0, r, :]
                    l_buf[slot, h, r, :] = nl.astype(BF16)
                    lb_buf[slot, h, r, :] = (z2 - nl) + mask_buf[diag, 1, r, :]

        def sums(t, slot):
            i, j = t
            vb = v_ref[rows(j), :]
            return (tuple(_dot(l_buf[slot, h], tri) for h in heads),
                    tuple(_dot_nt(doh_buf[h, rows(i), :], vb) for h in heads))

        def weights(t, slot, sm, da):
            i, j = t
            for h in heads:
                c = jnp.sum(jnp.where(lane == j, cl_ref[rows(i), h * LANES:(h + 1) * LANES], 0.0), axis=-1, keepdims=True)
                c = jnp.broadcast_to(c, (Q, LANES))
                for r in strips:
                    lb = lb_buf[slot, h, r, :]
                    a = jnp.exp2(lb + sm[h][r] + wide(c[r]))
                    g = da[h][r] * a
                    a_buf[slot, h, r, :] = a.astype(BF16)
                    be_buf[slot, h, r, :] = jnp.exp2(lb)
                    g_buf[slot, h, r, :] = g
                    gb_buf[slot, h, r, :] = g.astype(BF16)

        def prefix(t, slot):
            i, j = t
            dv_ref[rows(j), :] += (_dot_tn(a_buf[slot, 0], doh_buf[0, rows(i), :])
                                   + _dot_tn(a_buf[slot, 1], doh_buf[1, rows(i), :]))
            return tuple(_dot(gb_buf[slot, h], tpi) for h in heads)

        def dscores(t, slot, pm):
            i, j = t
            for h in heads:
                pg = pg_buf[h] * jnp.where(j == 0, 0.0, 1.0)
                for r in strips:
                    dz = g_buf[slot, h, r, :] - be_buf[slot, h, r, :] * (pm[h][r] + wide(pg[r]))
                    dz_buf[slot, h, r, :] = dz.astype(BF16)
                pg_buf[h] = pg + jnp.broadcast_to(pm[h][:, Q - 1:Q], (Q, LANES))

        def grads(t, slot):
            i, j = t
            dq_ref[rows(i), :] += (_dot(dz_buf[slot, 0], kh_buf[0, rows(j), :])
                                   + _dot(dz_buf[slot, 1], kh_buf[1, rows(j), :]))
            dk_ref[rows(j), :] += (_dot_tn(dz_buf[slot, 0], qh_buf[0, rows(i), :])
                                   + _dot_tn(dz_buf[slot, 1], qh_buf[1, rows(i), :]))

        def iteration(t, p):
            ta, tb, tc, td = t
            if ta is not None:
                z = scores(ta)
            if tb is not None:
                sm, da = sums(tb, 1 - p)
            if tc is not None:
                pm = prefix(tc, p)
            if td is not None:
                grads(td, 1 - p)
            if tb is not None:
                weights(tb, 1 - p, sm, da)
            if tc is not None:
                dscores(tc, p, pm)
            if ta is not None:
                logs(ta, p, z)

        def window(n):
            return tuple(tiles[n - k] if 0 <= n - k < ntiles else None for k in range(4))

        def following(t):
            i, j = t
            last = j == i
            return jnp.where(last, i + 1, i), jnp.where(last, 0, j + 1)

        peeled = 3 + (ntiles - 3) % unroll

        def unrolled_iterations(_, t):
            for n in range(peeled, peeled + unroll):
                iteration(t, n % 2)
                t = (following(t[0]),) + t[:3]
            return t

        pg_buf[...] = jnp.zeros_like(pg_buf)
        for n in range(peeled):
            iteration(window(n), n % 2)
        first_window = tuple((jnp.int32(i), jnp.int32(j)) for i, j in window(peeled))
        lax.fori_loop(0, (ntiles - peeled) // unroll, unrolled_iterations, first_window)
        for n in range(ntiles, ntiles + 3):
            iteration(window(n), n % 2)
        dq_ref[...] = dq_ref[...] * (1.0 / math.sqrt(SB_HEAD_DIM))

    col_block = lambda off: pl.BlockSpec((S, LANES), lambda p: (0, off + p), pipeline_mode=pl.Buffered(1))
    return pl.pallas_call(
        body, name="attn_bwd", grid=(npair,),
        out_shape=[jax.ShapeDtypeStruct((S, SB_WIDTH), F32)] * 3,
        in_specs=[col_block(0), col_block(npair), col_block(2 * npair), col_block(0),
                  pl.BlockSpec((S, 2 * LANES), lambda p: (0, p), pipeline_mode=pl.Buffered(1))],
        out_specs=[pl.BlockSpec((S, LANES), lambda p: (0, p), pipeline_mode=pl.Buffered(1))] * 3,
        scratch_shapes=[pltpu.VMEM((2, 2, Q, Q), F32)] * 3 + [pltpu.VMEM((2, 2, Q, Q), BF16)] * 4
        + [pltpu.VMEM((2, Q, LANES), F32), pltpu.VMEM((2, 2, Q, Q), F32)] + [pltpu.VMEM((2, S, LANES), BF16)] * 3,
        compiler_params=_cparams(dimension_semantics=("arbitrary",)),
    )(qkv, qkv, qkv, do, cl)


def _conv_bwd(u_conv, y_conv, dco, conv_w, ln_g, ln_b):
    S = u_conv.shape[0]
    tc = min(TOKEN_TILE, S)
    nt = S // tc
    per = tc // CONV_HALO
    groups = CONV_CHUNK // 8

    def body(u_ref, halo_ref, y_ref, dco_ref, cw_ref, lg_ref, lb_ref, du_ref, dcw_ref, dsm_ref, glu_ext, dyc_ext, sg_buf,
             dcw_acc, dsm_acc, glu_sh, dyc_sh):
        i = pl.program_id(0)
        ti = nt - 1 - i

        @pl.when(i == 0)
        def _():
            dyc_ext[tc:, :] = jnp.zeros((CONV_HALO, CONV_CH), F32)
            dcw_acc[...] = jnp.zeros_like(dcw_acc)
            dsm_acc[...] = jnp.zeros_like(dsm_acc)

        @pl.when(i > 0)
        def _():
            dyc_ext[tc:, :] = dyc_ext[0:CONV_HALO, :]

        glu_ext[0:CONV_HALO, :] = jnp.where(ti > 0, _glu(halo_ref[...])[2], 0.0)
        _, sg, glu = _glu(u_ref[...])
        glu_ext[CONV_HALO:, :] = glu
        sg_buf[...] = sg
        _shift_copies(glu_ext, glu_sh)

        dcb = jnp.zeros((8, CONV_CH), F32)
        dlg = jnp.zeros((8, CONV_CH), F32)
        dlb = jnp.zeros((8, CONV_CH), F32)
        fold = lambda t: jnp.sum(t.reshape(groups, 8, CONV_CH), axis=0)
        for r0 in range(0, tc, CONV_CHUNK):
            y = y_ref[r0:r0 + CONV_CHUNK, :]
            mu = jnp.mean(y, axis=-1, keepdims=True)
            yc = y - mu
            rstd = lax.rsqrt(jnp.mean(yc * yc, axis=-1, keepdims=True) + EPS)
            yn = yc * rstd
            yl = yn * lg_ref[...] + lb_ref[...]
            s = jax.nn.sigmoid(yl)
            dyl = dco_ref[r0:r0 + CONV_CHUNK, :] * (s * (1.0 + yl * (1.0 - s)))
            dlg = dlg + fold(dyl * yn)
            dlb = dlb + fold(dyl)
            wv = dyl * lg_ref[...]
            dyc = rstd * (wv - jnp.mean(wv, axis=-1, keepdims=True) - yn * jnp.mean(wv * yn, axis=-1, keepdims=True))
            dcb = dcb + fold(dyc)
            dyc_ext[r0:r0 + CONV_CHUNK, :] = dyc
        dsm_acc[0:8, :] += dcb
        dsm_acc[8:16, :] += dlg
        dsm_acc[16:24, :] += dlb
        _shift_copies(dyc_ext, dyc_sh)

        for r0 in range(0, tc, CONV_CHUNK):
            dyc = dyc_ext[r0:r0 + CONV_CHUNK, :]
            dglu = jnp.zeros((CONV_CHUNK, CONV_CH), F32)
            base = r0 + CONV_HALO - (CONV_WIDTH - 1)
            for w in range(CONV_WIDTH):
                back = r0 + (CONV_WIDTH - 1) - w
                dglu = dglu + cw_ref[w:w + 1, :] * _window(dyc_ext, dyc_sh, back, CONV_CHUNK)
                dcw_acc[8 * w:8 * w + 8, :] += fold(dyc * _window(glu_ext, glu_sh, base + w, CONV_CHUNK))
            sg = sg_buf[r0:r0 + CONV_CHUNK, :]
            v = u_ref[r0:r0 + CONV_CHUNK, :CONV_CH]
            du_ref[r0:r0 + CONV_CHUNK, :CONV_CH] = (dglu * sg).astype(BF16)
            du_ref[r0:r0 + CONV_CHUNK, CONV_CH:] = (dglu * v * sg * (1.0 - sg)).astype(BF16)

        @pl.when(i == nt - 1)
        def _():
            for w in range(CONV_WIDTH):
                dcw_ref[w:w + 1, :] = jnp.sum(dcw_acc[8 * w:8 * w + 8, :], axis=0, keepdims=True)
            dcw_ref[CONV_WIDTH:, :] = jnp.zeros((CONV_HALO - CONV_WIDTH, CONV_CH), F32)
            for k in range(3):
                dsm_ref[k:k + 1, :] = jnp.sum(dsm_acc[8 * k:8 * k + 8, :], axis=0, keepdims=True)
            dsm_ref[3:, :] = jnp.zeros((5, CONV_CH), F32)

    return pl.pallas_call(
        body, name="conv_bwd", grid=(nt,),
        out_shape=[jax.ShapeDtypeStruct((S, 2 * CONV_CH), BF16), jax.ShapeDtypeStruct((CONV_HALO, CONV_CH), F32),
                   jax.ShapeDtypeStruct((8, CONV_CH), F32)],
        in_specs=[pl.BlockSpec((tc, 2 * CONV_CH), lambda i: (nt - 1 - i, 0)),
                  pl.BlockSpec((CONV_HALO, 2 * CONV_CH), lambda i: (jnp.maximum((nt - 1 - i) * per - 1, 0), 0)),
                  pl.BlockSpec((tc, CONV_CH), lambda i: (nt - 1 - i, 0)), pl.BlockSpec((tc, CONV_CH), lambda i: (nt - 1 - i, 0)),
                  _const((CONV_HALO, CONV_CH)), _const((1, CONV_CH)), _const((1, CONV_CH))],
        out_specs=[pl.BlockSpec((tc, 2 * CONV_CH), lambda i: (nt - 1 - i, 0)), _const((CONV_HALO, CONV_CH)),
                   _const((8, CONV_CH))],
        scratch_shapes=[pltpu.VMEM((tc + CONV_HALO, CONV_CH), F32), pltpu.VMEM((tc + CONV_HALO, CONV_CH), F32),
                        pltpu.VMEM((tc, CONV_CH), F32), pltpu.VMEM((8 * CONV_HALO, CONV_CH), F32),
                        pltpu.VMEM((24, CONV_CH), F32)]
        + [pltpu.VMEM((SUBLANES - 1, tc + CONV_HALO - SUBLANES, CONV_CH), F32)] * 2,
        compiler_params=_cparams(dimension_semantics=("arbitrary",)),
    )(u_conv, u_conv, y_conv, dco, conv_w, ln_g, ln_b)


def _in_proj_bwd(du_conv, dq, dk, dv, w_in, x, g, dh1):
    S = x.shape[0]
    tm = min(TOKEN_TILE, S)
    nconv = 2 * CONV_CH

    def body(duc_ref, dq_ref, dk_ref, dv_ref, w_ref, x_ref, g_ref, dh1_ref, dx_ref, dg_ref):
        @pl.when(pl.program_id(0) == 0)
        def _():
            dg_ref[...] = jnp.zeros_like(dg_ref)

        da = _dot_nt(duc_ref[...], w_ref[:, :nconv])
        for n, ref in enumerate((dq_ref, dk_ref, dv_ref)):
            c0 = nconv + n * SB_WIDTH
            da = da + _dot_nt(ref[...].astype(BF16), w_ref[:, c0:c0 + SB_WIDTH])
        xf = x_ref[...]
        r = _rms_r(xf)
        dx_ref[...] = dh1_ref[...] + _rms_bwd(xf, r, g_ref[...], da)
        dg_ref[...] += jnp.sum(da * xf * r, axis=0, keepdims=True)

    row = lambda n: pl.BlockSpec((tm, n), lambda i: (i, 0))
    return pl.pallas_call(
        body, name="in_proj_bwd", grid=(S // tm,),
        out_shape=[jax.ShapeDtypeStruct((S, D_MODEL), F32), jax.ShapeDtypeStruct((1, D_MODEL), F32)],
        in_specs=[row(nconv), row(SB_WIDTH), row(SB_WIDTH), row(SB_WIDTH), _resident(w_in.shape), row(D_MODEL),
                  _const((1, D_MODEL)), row(D_MODEL)],
        out_specs=[row(D_MODEL), _const((1, D_MODEL))],
        compiler_params=_cparams(dimension_semantics=("arbitrary",)),
    )(du_conv, dq, dk, dv, w_in, x, g, dh1)


def _layer_grads(xs, target, g_pre_mix, w_in_f, conv_w_f, conv_b, conv_ln_g, conv_ln_b, attn_g, g_post_mix, g_pre_ffn,
                 g_post_ffn, late_weights, send_grads):
    a, u_conv, qkv = _in_proj(xs, g_pre_mix, w_in_f)
    conv_out, y_conv = _conv_fwd(u_conv, conv_w_f, conv_b, conv_ln_g, conv_ln_b)
    o, attn_out, cl = _attn_fwd(qkv, attn_g)
    w_out_f, w_gate_f, w_up_f, w_down_f = late_weights(attn_out)
    y, h1, f_in = _out_proj(conv_out, attn_out, w_out_f, xs, g_post_mix, g_pre_ffn)
    gt, up, df, dh2, loss_part, d_g_post_ffn = _ffn_fwd_loss(f_in, w_gate_f, w_up_f, w_down_f, h1, target, g_post_ffn)

    dgt, dup, act, dfin = _ffn_bwd(df, gt, up, w_gate_f, w_up_f, w_down_f)
    d_w_down = _matmul_tn("grad_w_down", act, df, 512)
    d_w_gate = _matmul_tn("grad_w_gate", f_in, dgt, FF_CHUNK)
    d_w_up = _matmul_tn("grad_w_up", f_in, dup, FF_CHUNK)
    sent = send_grads("ffn", (d_w_gate, d_w_up, d_w_down))
    dh1, dy, dco, do, d_g_pre_ffn, d_g_post_mix, d_attn_g = _mix_bwd(dfin, h1, y, dh2, g_pre_ffn + sent, g_post_mix,
                                                                     w_out_f, o, attn_g)
    d_w_out = jnp.concatenate([_matmul_tn("grad_w_out_conv", conv_out, dy, D_MODEL),
                               _matmul_tn("grad_w_out_attn", attn_out, dy, D_MODEL)], axis=0)
    sent = send_grads("w_out", (d_w_out,))
    dq, dk, dv = _attn_bwd(qkv, do, cl)
    du_conv, d_conv_w, d_conv_small = _conv_bwd(u_conv, y_conv, dco, conv_w_f, conv_ln_g + sent, conv_ln_b)
    d_w_in = jnp.concatenate([_matmul_tn("grad_w_in_conv", a, du_conv, 2 * CONV_CH),
                              _matmul_tn("grad_w_in_q", a, dq, SB_WIDTH), _matmul_tn("grad_w_in_k", a, dk, SB_WIDTH),
                              _matmul_tn("grad_w_in_v", a, dv, SB_WIDTH)], axis=1)
    sent = send_grads("w_in", (d_w_in,))
    grad_x, d_g_pre_mix = _in_proj_bwd(du_conv, dq, dk, dv, w_in_f, xs, g_pre_mix + sent, dh1)
    return (loss_part, grad_x, d_conv_w, d_conv_small, d_attn_g, d_g_pre_mix, d_g_post_mix, d_g_pre_ffn, d_g_post_ffn)


def _cols_to_blocks(w):
    K, N = w.shape
    return jnp.transpose(w.reshape(K, N_DEV, N // N_DEV), (1, 0, 2))


def _blocks_to_cols(blocks):
    n_dev, K, n = blocks.shape
    return jnp.transpose(blocks, (1, 0, 2)).reshape(K, n_dev * n)


def kernel(x, g_pre_mix, w_in, conv_w, conv_b, conv_ln_g, conv_ln_b, attn_norm_g, w_out, g_post_mix, g_pre_ffn, w_gate, w_up, w_down, g_post_ffn, loss_target, m_g_pre_mix, m_w_in, m_conv_w, m_conv_b, m_conv_ln_g, m_conv_ln_b, m_attn_norm_g, m_w_out, m_g_post_mix, m_g_pre_ffn, m_w_gate, m_w_up, m_w_down, m_g_post_ffn, v_g_pre_mix, v_w_in, v_conv_w, v_conv_b, v_conv_ln_g, v_conv_ln_b, v_attn_norm_g, v_w_out, v_g_post_mix, v_g_pre_ffn, v_w_gate, v_w_up, v_w_down, v_g_post_ffn):
    xs = x[0]
    target = loss_target[0]
    me = 4 * lax.axis_index("x") + 2 * lax.axis_index("y") + lax.axis_index("c")
    cw_shard = conv_w.reshape(CONV_WIDTH, CONV_CH // N_DEV)
    attn_g = attn_norm_g.reshape(1, SB_WIDTH)

    gathered = _all_gather([w_in[0].astype(BF16), cw_shard])
    w_in_f = _blocks_to_cols(gathered[0])
    conv_w_f = jnp.pad(_blocks_to_cols(gathered[1]), ((0, CONV_HALO - CONV_WIDTH), (0, 0)))
    gathered_zero = gathered[2][0:1, 0:1].astype(BF16)
    late = [w_out[0].astype(BF16) + gathered_zero, w_gate[0].astype(BF16), w_up[0].astype(BF16), w_down[0].astype(BF16)]
    late_started = _exchange_start("all_gather_late_start", late, scatter=False)

    def late_weights(after):
        lands = _exchange_wait("all_gather_late_wait", late_started, False, after)
        wo, wg, wu, wd = [lax.dynamic_update_index_in_dim(land, own, me, 0) for land, own in zip(lands, late)]
        return wo.reshape(D_MODEL, D_MODEL), _blocks_to_cols(wg), _blocks_to_cols(wu), wd.reshape(D_FF, D_MODEL)

    started = {}

    def send_grads(name, grads):
        blocks = [g.reshape(N_DEV, g.shape[0] // N_DEV, g.shape[1]) if g.shape[1] == D_MODEL else _cols_to_blocks(g)
                  for g in grads]
        payload = BF16 if name == "w_in" else F32
        sent = _exchange_start("reduce_scatter_" + name + "_start", [b.astype(payload) for b in blocks], scatter=True)
        started[name] = (sent, blocks)
        return sent[-1][0:1, 0:1]

    (loss_part, grad_x, d_conv_w, d_conv_small, d_attn_g, d_g_pre_mix, d_g_post_mix, d_g_pre_ffn,
     d_g_post_ffn) = _layer_grads(
        xs, target, g_pre_mix + late_started[-1][0:1, 0:1], w_in_f, conv_w_f, conv_b, conv_ln_g, conv_ln_b, attn_g,
        g_post_mix, g_pre_ffn, g_post_ffn, late_weights, send_grads)

    def reduced(name, after, shards):
        st, blocks = started[name]
        lands = _exchange_wait("reduce_scatter_" + name + "_wait", st, True, after)
        return [_sum_adamw("adamw_" + wn, land, lax.dynamic_index_in_dim(blk, me, 0, keepdims=False), w[0], m[0], v[0])
                for land, blk, (wn, w, m, v) in zip(lands, blocks, shards)]

    two = lambda t: t.reshape(2, CONV_CH)
    small_g = jnp.concatenate([
        d_conv_w,
        d_conv_small[0:3],
        d_attn_g,
        two(d_g_pre_mix), two(d_g_post_mix), two(d_g_pre_ffn), two(d_g_post_ffn),
        jnp.broadcast_to(loss_part[0:1, 0:1], (1, CONV_CH)),
        jnp.zeros((3, CONV_CH), F32)], axis=0)
    small_g = _all_reduce_small(small_g)
    loss = small_g[44, 0]
    g_conv_w = lax.dynamic_slice(small_g, (0, me * (CONV_CH // N_DEV)), (CONV_WIDTH, CONV_CH // N_DEV))
    pack = lambda cb, lg, lb, ag, g1, g2, g3, g4: jnp.concatenate(
        [cb, lg, lb, ag.reshape(1, SB_WIDTH), two(g1), two(g2), two(g3), two(g4), jnp.zeros((4, CONV_CH), F32)], axis=0)
    sm_g = small_g[CONV_HALO:]
    sm_delta, sm_m, sm_v = _adamw_small(
        "adamw_small",
        pack(conv_b, conv_ln_g, conv_ln_b, attn_norm_g, g_pre_mix, g_post_mix, g_pre_ffn, g_post_ffn), sm_g,
        pack(m_conv_b, m_conv_ln_g, m_conv_ln_b, m_attn_norm_g, m_g_pre_mix, m_g_post_mix, m_g_pre_ffn, m_g_post_ffn),
        pack(v_conv_b, v_conv_ln_g, v_conv_ln_b, v_attn_norm_g, v_g_pre_mix, v_g_post_mix, v_g_pre_ffn, v_g_post_ffn))
    cw_delta, cw_m, cw_v = _adamw_small("adamw_conv_w", cw_shard, g_conv_w,
                                        m_conv_w.reshape(cw_shard.shape), v_conv_w.reshape(cw_shard.shape))

    ffn = reduced("ffn", grad_x, [("w_gate", w_gate, m_w_gate, v_w_gate), ("w_up", w_up, m_w_up, v_w_up),
                                  ("w_down", w_down, m_w_down, v_w_down)])
    big = {"w_gate": ffn[0], "w_up": ffn[1], "w_down": ffn[2],
           "w_out": reduced("w_out", ffn[2][0], [("w_out", w_out, m_w_out, v_w_out)])[0]}
    big["w_in"] = reduced("w_in", big["w_out"][0], [("w_in", w_in, m_w_in, v_w_in)])[0]

    def unpack(t):
        return {"conv_b": t[0:1], "conv_ln_g": t[1:2], "conv_ln_b": t[2:3], "attn_norm_g": t[3:4].reshape(1, SB_HEADS, SB_HEAD_DIM),
                "g_pre_mix": t[4:6].reshape(1, D_MODEL), "g_post_mix": t[6:8].reshape(1, D_MODEL),
                "g_pre_ffn": t[8:10].reshape(1, D_MODEL), "g_post_ffn": t[10:12].reshape(1, D_MODEL)}

    names = ["g_pre_mix", "w_in", "conv_w", "conv_b", "conv_ln_g", "conv_ln_b", "attn_norm_g", "w_out", "g_post_mix",
             "g_pre_ffn", "w_gate", "w_up", "w_down", "g_post_ffn"]
    kinds = []
    for idx, small in enumerate((sm_g, sm_delta, sm_m, sm_v)):
        d = unpack(small)
        d["conv_w"] = (g_conv_w, cw_delta, cw_m, cw_v)[idx].reshape(1, CONV_WIDTH, 1, CONV_CH // N_DEV)
        for n in big:
            d[n] = big[n][idx][None]
        kinds.append([d[n] for n in names])

    return (loss, grad_x[None], *kinds[0], *kinds[1], *kinds[2], *kinds[3])
```

```python
import math

import jax
import jax.numpy as jnp
from jax import lax
from jax.experimental import pallas as pl
from jax.experimental.pallas import tpu as pltpu

F32 = jnp.float32
BF16 = jnp.bfloat16
MESH = pl.DeviceIdType.MESH

N_DEV = 8
D_MODEL = 1024
CONV_CH = 512
CONV_WIDTH = 31
SB_HEADS = 8
SB_HEAD_DIM = 64
SB_WIDTH = SB_HEADS * SB_HEAD_DIM
D_FF = 2816
EPS = 1e-6
LOG2E = 1.4426950408889634
Z2_MAX = 100.0
MASKED = -1e30
ADAM_LR = 0.001
ADAM_B1 = 0.9
ADAM_B2 = 0.999
ADAM_EPS = 1e-08
ADAM_WD = 0.01
ADAM_STEP = 10

SUBLANES = 8
LANES = 128
VMEM_LIMIT = 56 * 1024 * 1024
TOKEN_TILE = 512
GRAD_TILE = 1024
FFN_TILE = 256
ATTN_FWD_UNROLL = 8
ATTN_BWD_UNROLL = 4
ATTN_STRIP = 32
ATTN_BLOCK = 256
CONV_HALO = 32
CONV_CHUNK = 64
FF_CHUNK = D_FF // 2


def _cparams(**kw):
    return pltpu.CompilerParams(vmem_limit_bytes=VMEM_LIMIT, **kw)


def _resident(shape):
    return pl.BlockSpec(shape, lambda *_: (0,) * len(shape), pipeline_mode=pl.Buffered(1))


def _const(shape):
    return pl.BlockSpec(shape, lambda *_: (0,) * len(shape))


def _rms_r(xf):
    return lax.rsqrt(jnp.mean(xf * xf, axis=-1, keepdims=True) + EPS)


def _rms_bwd(xf, r, g, dout):
    w = dout * g
    return r * (w - xf * (r * r) * jnp.mean(w * xf, axis=-1, keepdims=True))


def _dot(a, b):
    return jnp.dot(a, b, preferred_element_type=F32)


def _dot_nt(a, b):
    return lax.dot_general(a, b, (((1,), (1,)), ((), ())), preferred_element_type=F32)


def _dot_tn(a, b):
    return lax.dot_general(a, b, (((0,), (0,)), ((), ())), preferred_element_type=F32)


def _peer(x, y, c, k):
    px = 1 - x if (k >> 2) & 1 else x
    py = 1 - y if (k >> 1) & 1 else y
    pc = 1 - c if k & 1 else c
    return (px, py, pc), 4 * px + 2 * py + pc


def _all_gather(shards):
    n = len(shards)

    def body(*refs):
        ins, outs, done = refs[:n], refs[n:2 * n], refs[2 * n]
        send_sems, recv_sems, local_sems = refs[2 * n + 1:]
        x, y, c = lax.axis_index("x"), lax.axis_index("y"), lax.axis_index("c")
        me, sibling = (x, y, c), (x, y, 1 - c)
        chips = [(1 - x, y), (x, 1 - y), (1 - x, 1 - y)]
        number = lambda d: 4 * d[0] + 2 * d[1] + d[2]

        def copy(a, k, block, to, src=None):
            rows = outs[a].at[number(block)]
            return pltpu.make_async_remote_copy(
                src_ref=rows if src is None else src, dst_ref=rows, send_sem=send_sems.at[a * (N_DEV - 1) + k],
                recv_sem=recv_sems.at[a * (N_DEV - 1) + k], device_id=to, device_id_type=MESH)

        copies = [pltpu.make_async_copy(ins[a], outs[a].at[number(me)], local_sems.at[a]) for a in range(n)]
        for mine in copies:
            mine.start()
        sent = [copy(a, 0, me, sibling, src=ins[a]) for a in range(n)]
        sent += [copy(a, 1 + j, me, (*chip, c), src=ins[a]) for j, chip in enumerate(chips) for a in range(n)]
        for cp in sent:
            cp.start()
        for j, chip in enumerate(chips):
            for a in range(n):
                copy(a, 1 + j, (*chip, c), me).wait_recv()
                passed = copy(a, 4 + j, (*chip, c), sibling)
                passed.start()
                sent.append(passed)
        for a in range(n):
            copy(a, 0, sibling, me).wait_recv()
            for j, chip in enumerate(chips):
                copy(a, 4 + j, (*chip, 1 - c), me).wait_recv()
        for cp in sent:
            cp.wait_send()
        for mine in copies:
            mine.wait()
        done[...] = jnp.zeros_like(done)

    any_spec = pl.BlockSpec(memory_space=pl.ANY)
    return pl.pallas_call(
        body, name="all_gather_weights",
        out_shape=[jax.ShapeDtypeStruct((N_DEV,) + s.shape, s.dtype) for s in shards] + [jax.ShapeDtypeStruct((8, LANES), F32)],
        in_specs=[any_spec] * n, out_specs=[any_spec] * n + [pl.BlockSpec(memory_space=pltpu.VMEM)],
        scratch_shapes=[pltpu.SemaphoreType.DMA((n * (N_DEV - 1),)), pltpu.SemaphoreType.DMA((n * (N_DEV - 1),)),
                        pltpu.SemaphoreType.DMA((n,))],
        compiler_params=pltpu.CompilerParams(has_side_effects=True),
    )(*shards)


def _adamw(w, g, m, v):
    m = ADAM_B1 * m + (1.0 - ADAM_B1) * g
    v = ADAM_B2 * v + (1.0 - ADAM_B2) * (g * g)
    m_hat = m / (1.0 - ADAM_B1 ** ADAM_STEP)
    v_hat = v / (1.0 - ADAM_B2 ** ADAM_STEP)
    delta = -ADAM_LR * (m_hat / (jnp.sqrt(v_hat) + ADAM_EPS) + ADAM_WD * w)
    return delta, m, v


def _exchange_and_sum(src_block, recv_ref, send_sems, recv_sems, local_sem):
    x, y, c = lax.axis_index("x"), lax.axis_index("y"), lax.axis_index("c")
    me = 4 * x + 2 * y + c
    mine = pltpu.make_async_copy(src_block(me), recv_ref.at[me], local_sem)
    mine.start()
    for k in range(1, N_DEV):
        peer, peer_block = _peer(x, y, c, k)
        pltpu.make_async_remote_copy(
            src_ref=src_block(peer_block), dst_ref=recv_ref.at[me], send_sem=send_sems.at[k - 1],
            recv_sem=recv_sems.at[k - 1], device_id=peer, device_id_type=MESH).start()
    for k in range(1, N_DEV):
        peer, peer_block = _peer(x, y, c, k)
        arrived = pltpu.make_async_remote_copy(
            src_ref=src_block(peer_block), dst_ref=recv_ref.at[peer_block], send_sem=send_sems.at[k - 1],
            recv_sem=recv_sems.at[k - 1], device_id=peer, device_id_type=MESH)
        arrived.wait_send()
        arrived.wait_recv()
    mine.wait()


HBM_SPEC = pl.BlockSpec(memory_space=pltpu.HBM)
SEM_SPEC = pl.BlockSpec(memory_space=pltpu.SEMAPHORE)
DATAFLOW = pltpu.SideEffectType.DATAFLOW_SIDE_EFFECTING


def _exchange_copies(srcs, lands, send_sems, recv_sems, scatter, wait):
    x, y, c = lax.axis_index("x"), lax.axis_index("y"), lax.axis_index("c")
    me = 4 * x + 2 * y + c
    for k in range(1, N_DEV):
        peer, peer_block = _peer(x, y, c, k)
        for a in range(len(srcs)):
            s = a * (N_DEV - 1) + k - 1
            src = srcs[a].at[peer_block] if scatter else srcs[a]
            copy = pltpu.make_async_remote_copy(
                src_ref=src, dst_ref=lands[a].at[peer_block if wait else me], send_sem=send_sems.at[s],
                recv_sem=recv_sems.at[s], device_id=peer, device_id_type=MESH)
            if wait:
                copy.wait_send()
                copy.wait_recv()
            else:
                copy.start()


def _exchange_start(name, arrays, scatter):
    n = len(arrays)
    land_shapes = [a.shape if scatter else (N_DEV,) + a.shape for a in arrays]

    def body(*refs):
        _exchange_copies(refs[:n], refs[n:2 * n], refs[2 * n], refs[2 * n + 1], scatter, wait=False)
        refs[-1][...] = jnp.zeros_like(refs[-1])

    sems = pltpu.SemaphoreType.DMA((n * (N_DEV - 1),))
    hbm = lambda t: pltpu.with_memory_space_constraint(t, pltpu.HBM)
    return pl.pallas_call(
        body, name=name,
        out_shape=(sems, sems, *[pltpu.HBM(a.shape, a.dtype) for a in arrays],
                   *[pltpu.HBM(ls, a.dtype) for ls, a in zip(land_shapes, arrays)], jax.ShapeDtypeStruct((8, LANES), F32)),
        in_specs=[HBM_SPEC] * (2 * n),
        out_specs=(SEM_SPEC, SEM_SPEC, *[HBM_SPEC] * (2 * n), pl.BlockSpec(memory_space=pltpu.VMEM)),
        input_output_aliases={a: 2 + a for a in range(2 * n)},
        compiler_params=pltpu.CompilerParams(has_side_effects=DATAFLOW),
    )(*[hbm(a) for a in arrays], *[hbm(lax.empty(ls, a.dtype)) for ls, a in zip(land_shapes, arrays)])


def _exchange_wait(name, started, scatter, after):
    n = (len(started) - 3) // 2
    send_sems, recv_sems = started[0], started[1]
    arrays, lands = started[2:2 + n], started[2 + n:2 + 2 * n]

    def body(*refs):
        _exchange_copies(refs[:n], refs[n:2 * n], refs[2 * n], refs[2 * n + 1], scatter, wait=True)

    return pl.pallas_call(
        body, name=name,
        out_shape=[pltpu.HBM(t.shape, t.dtype) for t in (*arrays, *lands)],
        in_specs=[HBM_SPEC] * (2 * n) + [SEM_SPEC, SEM_SPEC, pl.BlockSpec(memory_space=pl.ANY)],
        out_specs=[HBM_SPEC] * (2 * n),
        input_output_aliases={a: a for a in range(2 * n)},
        compiler_params=pltpu.CompilerParams(has_side_effects=DATAFLOW),
    )(*arrays, *lands, send_sems, recv_sems, after)[n:]


def _sum_adamw(name, land, own, w, m, v):
    _, M, N = land.shape
    rows = math.gcd(M, 128)

    def body(land_ref, own_ref, w_ref, m_ref, v_ref, grad_ref, delta_ref, nm_ref, nv_ref):
        x, y, c = lax.axis_index("x"), lax.axis_index("y"), lax.axis_index("c")
        g = own_ref[...]
        for k in range(1, N_DEV):
            g = g + land_ref[_peer(x, y, c, k)[1]].astype(F32)
        delta, nm, nv = _adamw(w_ref[...], g, m_ref[...], v_ref[...])
        grad_ref[...] = g
        delta_ref[...] = delta
        nm_ref[...] = nm
        nv_ref[...] = nv

    row = pl.BlockSpec((rows, N), lambda i: (i, 0))
    return pl.pallas_call(
        body, name=name, grid=(M // rows,), out_shape=[jax.ShapeDtypeStruct((M, N), F32)] * 4,
        in_specs=[pl.BlockSpec((N_DEV, rows, N), lambda i: (0, i, 0)), row, row, row, row], out_specs=[row] * 4,
        compiler_params=_cparams(),
    )(land, own, w, m, v)


def _all_reduce_small(g):
    R, C = g.shape

    def body(g_ref, out_ref, recv_ref, send_sems, recv_sems, local_sem):
        _exchange_and_sum(lambda b: g_ref, recv_ref, send_sems, recv_sems, local_sem)
        total = recv_ref[0]
        for b in range(1, N_DEV):
            total = total + recv_ref[b]
        out_ref[...] = total

    vmem = pl.BlockSpec(memory_space=pltpu.VMEM)
    return pl.pallas_call(
        body, name="all_reduce_small_grads", out_shape=jax.ShapeDtypeStruct((R, C), F32),
        in_specs=[vmem], out_specs=vmem,
        scratch_shapes=[pltpu.VMEM((N_DEV, R, C), F32), pltpu.SemaphoreType.DMA((N_DEV - 1,)),
                        pltpu.SemaphoreType.DMA((N_DEV - 1,)), pltpu.SemaphoreType.DMA(())],
        compiler_params=_cparams(has_side_effects=True),
    )(g)


def _adamw_small(name, w, g, m, v):
    def body(w_ref, g_ref, m_ref, v_ref, delta_ref, nm_ref, nv_ref):
        delta, nm, nv = _adamw(w_ref[...], g_ref[...], m_ref[...], v_ref[...])
        delta_ref[...] = delta
        nm_ref[...] = nm
        nv_ref[...] = nv

    vmem = pl.BlockSpec(memory_space=pltpu.VMEM)
    return pl.pallas_call(body, name=name, out_shape=[jax.ShapeDtypeStruct(w.shape, F32)] * 3,
                          in_specs=[vmem] * 4, out_specs=[vmem] * 3)(w, g, m, v)


def _in_proj(x, g, w_in):
    S = x.shape[0]
    tm = min(TOKEN_TILE, S)
    nconv = 2 * CONV_CH

    def body(x_ref, g_ref, w_ref, a_ref, uc_ref, qkv_ref):
        xf = x_ref[...]
        a = (xf * _rms_r(xf) * g_ref[...]).astype(BF16)
        a_ref[...] = a
        uc_ref[...] = _dot(a, w_ref[:, :nconv])
        qkv_ref[:, :SB_WIDTH] = (_dot(a, w_ref[:, nconv:nconv + SB_WIDTH]) * (1.0 / math.sqrt(SB_HEAD_DIM))).astype(BF16)
        qkv_ref[:, SB_WIDTH:] = _dot(a, w_ref[:, nconv + SB_WIDTH:]).astype(BF16)

    row = lambda n: pl.BlockSpec((tm, n), lambda i: (i, 0))
    return pl.pallas_call(
        body, name="in_proj", grid=(S // tm,),
        out_shape=[jax.ShapeDtypeStruct((S, D_MODEL), BF16), jax.ShapeDtypeStruct((S, nconv), F32),
                   jax.ShapeDtypeStruct((S, 3 * SB_WIDTH), BF16)],
        in_specs=[row(D_MODEL), _const((1, D_MODEL)), _resident(w_in.shape)],
        out_specs=[row(D_MODEL), row(nconv), row(3 * SB_WIDTH)],
        compiler_params=_cparams(),
    )(x, g, w_in)


def _glu(u):
    val, gate = u[:, :CONV_CH], u[:, CONV_CH:]
    sg = jax.nn.sigmoid(gate)
    return val, sg, val * sg


def _shift_copies(ext, shifted):
    n = shifted.shape[1]
    for r in range(1, SUBLANES):
        shifted[r - 1] = ext[r:r + n, :]


def _window(ext, shifted, start, rows):
    r = start % SUBLANES
    return ext[start:start + rows, :] if r == 0 else shifted[r - 1, start - r:start - r + rows, :]


def _conv_rows(glu_ext, glu_sh, cw_ref, r0, rows):
    base = r0 + CONV_HALO - (CONV_WIDTH - 1)
    acc = cw_ref[0:1, :] * _window(glu_ext, glu_sh, base, rows)
    for w in range(1, CONV_WIDTH):
        acc = acc + cw_ref[w:w + 1, :] * _window(glu_ext, glu_sh, base + w, rows)
    return acc


def _conv_fwd(u_conv, conv_w, conv_b, ln_g, ln_b):
    S = u_conv.shape[0]
    tc = min(TOKEN_TILE, S)

    def body(u_ref, cw_ref, cb_ref, lg_ref, lb_ref, out_ref, y_ref, glu_ext, glu_sh):
        i = pl.program_id(0)

        @pl.when(i == 0)
        def _():
            glu_ext[0:CONV_HALO, :] = jnp.zeros((CONV_HALO, CONV_CH), F32)

        @pl.when(i > 0)
        def _():
            glu_ext[0:CONV_HALO, :] = glu_ext[tc:tc + CONV_HALO, :]

        glu_ext[CONV_HALO:, :] = _glu(u_ref[...])[2]
        _shift_copies(glu_ext, glu_sh)
        for r0 in range(0, tc, CONV_CHUNK):
            y = _conv_rows(glu_ext, glu_sh, cw_ref, r0, CONV_CHUNK) + cb_ref[...]
            y_ref[r0:r0 + CONV_CHUNK, :] = y
            mu = jnp.mean(y, axis=-1, keepdims=True)
            yc = y - mu
            yn = yc * lax.rsqrt(jnp.mean(yc * yc, axis=-1, keepdims=True) + EPS)
            yl = yn * lg_ref[...] + lb_ref[...]
            out_ref[r0:r0 + CONV_CHUNK, :] = (yl * jax.nn.sigmoid(yl)).astype(BF16)

    return pl.pallas_call(
        body, name="conv_fwd", grid=(S // tc,),
        out_shape=[jax.ShapeDtypeStruct((S, CONV_CH), BF16), jax.ShapeDtypeStruct((S, CONV_CH), F32)],
        in_specs=[pl.BlockSpec((tc, 2 * CONV_CH), lambda i: (i, 0)), _const((CONV_HALO, CONV_CH)),
                  _const((1, CONV_CH)), _const((1, CONV_CH)), _const((1, CONV_CH))],
        out_specs=[pl.BlockSpec((tc, CONV_CH), lambda i: (i, 0))] * 2,
        scratch_shapes=[pltpu.VMEM((tc + CONV_HALO, CONV_CH), F32),
                        pltpu.VMEM((SUBLANES - 1, tc + CONV_HALO - SUBLANES, CONV_CH), F32)],
        compiler_params=_cparams(dimension_semantics=("arbitrary",)),
    )(u_conv, conv_w, conv_b, ln_g, ln_b)


def _head_masks():
    lane = lax.broadcasted_iota(jnp.int32, (1, LANES), 1)
    return lane < SB_HEAD_DIM


def _split_heads(t, first):
    z = jnp.zeros_like(t)
    return jnp.where(first, t, z), jnp.where(first, z, t)


def _split_heads_into(halves, src, first, nblocks, rows):
    def one_block(b, carry):
        r = pl.ds(pl.multiple_of(b * rows, rows), rows)
        halves[0, r, :], halves[1, r, :] = _split_heads(src[r, :], first)
        return carry

    lax.fori_loop(0, nblocks, one_block, 0)


def _head_sum(t, first):
    a = jnp.sum(jnp.where(first, t, 0.0), axis=-1, keepdims=True)
    b = jnp.sum(jnp.where(first, 0.0, t), axis=-1, keepdims=True)
    return a, b


def _attn_fwd(qkv, g_attn):
    S = qkv.shape[0]
    Q = min(ATTN_BLOCK, S)
    nq = S // Q
    assert nq <= LANES
    ntiles = nq * (nq + 1) // 2
    unroll = ATTN_FWD_UNROLL
    assert unroll % 2 == 0 and ntiles >= 3 + unroll
    npair = SB_WIDTH // LANES
    tiles = [(i, j) for i in range(nq) for j in range(i, -1, -1)]

    def body(q_ref, k_ref, v_ref, g_ref, o_ref, ao_ref, cl_ref, l_buf, z2_buf, a_buf, c_buf, mask_buf, qh_buf,
             vh_buf):
        first = _head_masks()
        lane = lax.broadcasted_iota(jnp.int32, (1, LANES), 1)
        row = lax.broadcasted_iota(jnp.int32, (Q, Q), 0)
        col = lax.broadcasted_iota(jnp.int32, (Q, Q), 1)
        tri = jnp.where(row >= col, -1.0, 0.0).astype(BF16)
        heads = range(2)
        strips = [slice(r0, r0 + ATTN_STRIP) for r0 in range(0, Q, ATTN_STRIP)]
        rows = lambda j: pl.ds(pl.multiple_of(j * Q, Q), Q)
        wide = lambda t: jnp.tile(t, (1, Q // LANES))
        as_int = lambda t: int(t) if isinstance(t, (bool, int)) else t.astype(jnp.int32)

        keep = col < row
        mask_buf[0, 0] = jnp.full((Q, Q), LOG2E, F32)
        mask_buf[0, 1] = jnp.zeros((Q, Q), F32)
        mask_buf[1, 0] = jnp.where(keep, LOG2E, 0.0)
        mask_buf[1, 1] = jnp.where(keep, 0.0, MASKED)
        o_ref[...] = jnp.zeros_like(o_ref)
        _split_heads_into(qh_buf, q_ref, first, nq, Q)
        _split_heads_into(vh_buf, v_ref, first, nq, Q)

        def scores(t):
            i, j = t
            kb = k_ref[rows(j), :]
            return tuple(_dot_nt(qh_buf[h, rows(i), :], kb) for h in heads)

        def logs(t, slot, z):
            i, j = t
            diag = as_int(i == j)
            for h in heads:
                for r in strips:
                    z2 = jnp.minimum(z[h][r] * LOG2E, Z2_MAX)
                    nl = jnp.log(1.0 + jnp.exp2(z2)) * mask_buf[diag, 0, r, :]
                    l_buf[slot, h, r, :] = nl.astype(BF16)
                    z2_buf[slot, h, r, :] = z2 + mask_buf[diag, 1, r, :]

        def sums(slot):
            return tuple(_dot(l_buf[slot, h], tri) for h in heads)

        def weights(t, slot, sm):
            i, j = t
            running = jnp.where(j == i, 0.0, 1.0)
            for h in heads:
                before = c_buf[h] * running
                for r in strips:
                    a_buf[slot, h, r, :] = jnp.exp2(z2_buf[slot, h, r, :] + sm[h][r] + wide(before[r])).astype(BF16)
                hl = slice(h * LANES, (h + 1) * LANES)
                cl_ref[rows(i), hl] = jnp.where(lane == j, before, cl_ref[rows(i), hl])
                c_buf[h] = before + jnp.broadcast_to(sm[h][:, 0:1], (Q, LANES))

        def values(t, slot):
            i, j = t
            o_ref[rows(i), :] += _dot(a_buf[slot, 0], vh_buf[0, rows(j), :]) + _dot(a_buf[slot, 1], vh_buf[1, rows(j), :])

        def iteration(t, p):
            ta, tc, td = t
            if ta is not None:
                z = scores(ta)
            if tc is not None:
                sm = sums(1 - p)
            if td is not None:
                values(td, p)
            if tc is not None:
                weights(tc, 1 - p, sm)
            if ta is not None:
                logs(ta, p, z)

        def window(n):
            return tuple(tiles[n - k] if 0 <= n - k < ntiles else None for k in range(3))

        def following(t):
            i, j = t
            last = j == 0
            return jnp.where(last, i + 1, i), jnp.where(last, i + 1, j - 1)

        peeled = 3 + (ntiles - 3) % unroll

        def unrolled_iterations(_, t):
            for n in range(peeled, peeled + unroll):
                iteration(t, n % 2)
                t = (following(t[0]),) + t[:2]
            return t

        c_buf[...] = jnp.zeros_like(c_buf)
        for n in range(peeled):
            iteration(window(n), n % 2)
        first_window = tuple((jnp.int32(i), jnp.int32(j)) for i, j in window(peeled))
        lax.fori_loop(0, (ntiles - peeled) // unroll, unrolled_iterations, first_window)
        for n in range(ntiles, ntiles + 2):
            iteration(window(n), n % 2)

        def head_norm(b, carry):
            o = o_ref[rows(b), :]
            sa, sb = _head_sum(o * o, first)
            r = jnp.where(first, lax.rsqrt(sa * (1.0 / SB_HEAD_DIM) + EPS), lax.rsqrt(sb * (1.0 / SB_HEAD_DIM) + EPS))
            ao_ref[rows(b), :] = (o * r * g_ref[...]).astype(BF16)
            return carry

        lax.fori_loop(0, nq, head_norm, 0)

    col_block = lambda off: pl.BlockSpec((S, LANES), lambda p: (0, off + p), pipeline_mode=pl.Buffered(1))
    out_block = lambda n: pl.BlockSpec((S, n), lambda p: (0, p), pipeline_mode=pl.Buffered(1))
    return pl.pallas_call(
        body, name="attn_fwd", grid=(npair,),
        out_shape=[jax.ShapeDtypeStruct((S, SB_WIDTH), F32), jax.ShapeDtypeStruct((S, SB_WIDTH), BF16),
                   jax.ShapeDtypeStruct((S, 2 * SB_WIDTH), F32)],
        in_specs=[col_block(0), col_block(npair), col_block(2 * npair), pl.BlockSpec((1, LANES), lambda p: (0, p))],
        out_specs=[out_block(LANES), out_block(LANES), out_block(2 * LANES)],
        scratch_shapes=[pltpu.VMEM((2, 2, Q, Q), BF16), pltpu.VMEM((2, 2, Q, Q), F32),
                        pltpu.VMEM((2, 2, Q, Q), BF16), pltpu.VMEM((2, Q, LANES), F32), pltpu.VMEM((2, 2, Q, Q), F32),
                        pltpu.VMEM((2, S, LANES), BF16), pltpu.VMEM((2, S, LANES), BF16)],
        compiler_params=_cparams(dimension_semantics=("arbitrary",)),
    )(qkv, qkv, qkv, g_attn)


def _out_proj(conv_out, attn_out, w_out, x, g_post_mix, g_pre_ffn):
    S = x.shape[0]
    tm = min(TOKEN_TILE, S)

    def body(co_ref, ao_ref, w_ref, x_ref, g1_ref, g2_ref, y_ref, h1_ref, fin_ref):
        y = _dot(co_ref[...], w_ref[:CONV_CH, :]) + _dot(ao_ref[...], w_ref[CONV_CH:, :])
        h1 = x_ref[...] + y * _rms_r(y) * g1_ref[...]
        y_ref[...] = y
        h1_ref[...] = h1
        fin_ref[...] = (h1 * _rms_r(h1) * g2_ref[...]).astype(BF16)

    row = lambda n: pl.BlockSpec((tm, n), lambda i: (i, 0))
    return pl.pallas_call(
        body, name="out_proj", grid=(S // tm,),
        out_shape=[jax.ShapeDtypeStruct((S, D_MODEL), F32), jax.ShapeDtypeStruct((S, D_MODEL), F32),
                   jax.ShapeDtypeStruct((S, D_MODEL), BF16)],
        in_specs=[row(CONV_CH), row(SB_WIDTH), _resident(w_out.shape), row(D_MODEL), _const((1, D_MODEL)),
                  _const((1, D_MODEL))],
        out_specs=[row(D_MODEL)] * 3,
        compiler_params=_cparams(),
    )(conv_out, attn_out, w_out, x, g_post_mix, g_pre_ffn)


def _ffn_fwd_loss(f_in, w_gate, w_up, w_down, h1, target, g_post_ffn):
    S = f_in.shape[0]
    tm = min(TOKEN_TILE, S)
    nt = S // tm

    def body(fin_ref, wg_ref, wu_ref, wd_ref, h1_ref, t_ref, g_ref, gt_ref, up_ref, df_ref, dh2_ref, loss_ref, dg_ref,
             sq_acc):
        i = pl.program_id(0)

        @pl.when(i == 0)
        def _():
            sq_acc[...] = jnp.zeros_like(sq_acc)
            dg_ref[...] = jnp.zeros_like(dg_ref)

        fin = fin_ref[...]
        gt = _dot(fin, wg_ref[...])
        up = _dot(fin, wu_ref[...])
        gt_ref[...] = gt.astype(BF16)
        up_ref[...] = up.astype(BF16)
        f = _dot((gt * jax.nn.sigmoid(gt) * up).astype(BF16), wd_ref[...])
        r = _rms_r(f)
        g = g_ref[...]
        diff = h1_ref[...] + f * r * g - t_ref[...]
        sq_acc[...] += jnp.sum(diff * diff, axis=0, keepdims=True)
        dh2 = diff * (1.0 / D_MODEL)
        dh2_ref[...] = dh2
        dg_ref[...] += jnp.sum(dh2 * f * r, axis=0, keepdims=True)
        df_ref[...] = _rms_bwd(f, r, g, dh2).astype(BF16)

        @pl.when(i == nt - 1)
        def _():
            loss_ref[...] = jnp.broadcast_to((0.5 / D_MODEL) * jnp.sum(sq_acc[...], axis=-1, keepdims=True), (1, LANES))

    row = lambda n: pl.BlockSpec((tm, n), lambda i: (i, 0))
    return pl.pallas_call(
        body, name="ffn_fwd_loss", grid=(nt,),
        out_shape=[jax.ShapeDtypeStruct((S, D_FF), BF16), jax.ShapeDtypeStruct((S, D_FF), BF16),
                   jax.ShapeDtypeStruct((S, D_MODEL), BF16), jax.ShapeDtypeStruct((S, D_MODEL), F32),
                   jax.ShapeDtypeStruct((1, LANES), F32), jax.ShapeDtypeStruct((1, D_MODEL), F32)],
        in_specs=[row(D_MODEL), _resident(w_gate.shape), _resident(w_up.shape), _resident(w_down.shape),
                  row(D_MODEL), row(D_MODEL), _const((1, D_MODEL))],
        out_specs=[row(D_FF), row(D_FF), row(D_MODEL), row(D_MODEL), _const((1, LANES)), _const((1, D_MODEL))],
        scratch_shapes=[pltpu.VMEM((1, D_MODEL), F32)],
        compiler_params=_cparams(dimension_semantics=("arbitrary",)),
    )(f_in, w_gate, w_up, w_down, h1, target, g_post_ffn)


def _ffn_bwd(df, gt, up, w_gate, w_up, w_down):
    S = df.shape[0]
    tm = min(FFN_TILE, S)

    def body(df_ref, gt_ref, up_ref, wg_ref, wu_ref, wd_ref, dgt_ref, dup_ref, act_ref, dfin_ref):
        dact = _dot_nt(df_ref[...], wd_ref[...])
        gt = gt_ref[...].astype(F32)
        up = up_ref[...].astype(F32)
        s = jax.nn.sigmoid(gt)
        silu = gt * s
        dgt = (dact * up * (s * (1.0 + gt * (1.0 - s)))).astype(BF16)
        dup = (dact * silu).astype(BF16)
        act_ref[...] = (silu * up).astype(BF16)
        dgt_ref[...] = dgt
        dup_ref[...] = dup
        dfin_ref[...] = _dot_nt(dgt, wg_ref[...]) + _dot_nt(dup, wu_ref[...])

    row = lambda n: pl.BlockSpec((tm, n), lambda i: (i, 0))
    return pl.pallas_call(
        body, name="ffn_bwd", grid=(S // tm,),
        out_shape=[jax.ShapeDtypeStruct((S, D_FF), BF16)] * 3 + [jax.ShapeDtypeStruct((S, D_MODEL), F32)],
        in_specs=[row(D_MODEL), row(D_FF), row(D_FF), _resident(w_gate.shape), _resident(w_up.shape),
                  _resident(w_down.shape)],
        out_specs=[row(D_FF)] * 3 + [row(D_MODEL)],
        compiler_params=_cparams(),
    )(df, gt, up, w_gate, w_up, w_down)


def _matmul_tn(name, x, y, tn):
    S, K = x.shape
    N = y.shape[1]
    ts = min(GRAD_TILE, S)

    def body(x_ref, y_ref, o_ref):
        @pl.when(pl.program_id(1) == 0)
        def _():
            o_ref[...] = jnp.zeros_like(o_ref)

        o_ref[...] += _dot_tn(x_ref[...].astype(BF16), y_ref[...].astype(BF16))

    return pl.pallas_call(
        body, name=name, grid=(N // tn, S // ts),
        out_shape=jax.ShapeDtypeStruct((K, N), F32),
        in_specs=[pl.BlockSpec((ts, K), lambda n, s: (s, 0)), pl.BlockSpec((ts, tn), lambda n, s: (s, n))],
        out_specs=pl.BlockSpec((K, tn), lambda n, s: (0, n)),
        compiler_params=_cparams(dimension_semantics=("arbitrary", "arbitrary")),
    )(x, y)


def _mix_bwd(dfin, h1, y, dh2, g_pre_ffn, g_post_mix, w_out, o, g_attn):
    S = dfin.shape[0]
    tm = min(TOKEN_TILE, S)
    inv_dh = 1.0 / SB_HEAD_DIM

    def body(dfin_ref, h1_ref, y_ref, dh2_ref, g2_ref, g1_ref, w_ref, o_ref, ga_ref, dh1_ref, dy_ref, dco_ref, do_ref,
             dg2_ref, dg1_ref, dga_ref):
        @pl.when(pl.program_id(0) == 0)
        def _():
            dg2_ref[...] = jnp.zeros_like(dg2_ref)
            dg1_ref[...] = jnp.zeros_like(dg1_ref)
            dga_ref[...] = jnp.zeros_like(dga_ref)

        h1, dfin = h1_ref[...], dfin_ref[...]
        r2 = _rms_r(h1)
        dh1 = dh2_ref[...] + _rms_bwd(h1, r2, g2_ref[...], dfin)
        dg2_ref[...] += jnp.sum(dfin * h1 * r2, axis=0, keepdims=True)
        y = y_ref[...]
        r1 = _rms_r(y)
        dy = _rms_bwd(y, r1, g1_ref[...], dh1).astype(BF16)
        dg1_ref[...] += jnp.sum(dh1 * y * r1, axis=0, keepdims=True)
        dh1_ref[...] = dh1
        dy_ref[...] = dy
        dco_ref[...] = _dot_nt(dy, w_ref[:CONV_CH, :])
        dao_all = _dot_nt(dy, w_ref[CONV_CH:, :])
        first = _head_masks()
        for p in range(SB_WIDTH // LANES):
            cols = slice(p * LANES, (p + 1) * LANES)
            o, dao, g = o_ref[:, cols], dao_all[:, cols], ga_ref[:, cols]
            sa, sb = _head_sum(o * o, first)
            r = jnp.where(first, lax.rsqrt(sa * inv_dh + EPS), lax.rsqrt(sb * inv_dh + EPS))
            w = dao * g
            wa, wb = _head_sum(w * o, first)
            do_ref[:, cols] = (r * (w - o * (r * r) * (jnp.where(first, wa, wb) * inv_dh))).astype(BF16)
            dga_ref[:, cols] += jnp.sum(dao * o * r, axis=0, keepdims=True)

    row = lambda n: pl.BlockSpec((tm, n), lambda i: (i, 0))
    return pl.pallas_call(
        body, name="mix_bwd", grid=(S // tm,),
        out_shape=[jax.ShapeDtypeStruct((S, D_MODEL), F32), jax.ShapeDtypeStruct((S, D_MODEL), BF16),
                   jax.ShapeDtypeStruct((S, CONV_CH), F32), jax.ShapeDtypeStruct((S, SB_WIDTH), BF16),
                   jax.ShapeDtypeStruct((1, D_MODEL), F32), jax.ShapeDtypeStruct((1, D_MODEL), F32),
                   jax.ShapeDtypeStruct((1, SB_WIDTH), F32)],
        in_specs=[row(D_MODEL)] * 4 + [_const((1, D_MODEL)), _const((1, D_MODEL)), _resident(w_out.shape), row(SB_WIDTH),
                  _const((1, SB_WIDTH))],
        out_specs=[row(D_MODEL), row(D_MODEL), row(CONV_CH), row(SB_WIDTH), _const((1, D_MODEL)), _const((1, D_MODEL)),
                   _const((1, SB_WIDTH))],
        compiler_params=_cparams(dimension_semantics=("arbitrary",)),
    )(dfin, h1, y, dh2, g_pre_ffn, g_post_mix, w_out, o, g_attn)


def _attn_bwd(qkv, do, cl):
    S = qkv.shape[0]
    Q = min(ATTN_BLOCK, S)
    nq = S // Q
    ntiles = nq * (nq + 1) // 2
    unroll = ATTN_BWD_UNROLL
    assert unroll % 2 == 0 and ntiles >= 3 + unroll
    npair = SB_WIDTH // LANES
    tiles = [(i, j) for i in range(nq) for j in range(i + 1)]

    def body(q_ref, k_ref, v_ref, do_ref, cl_ref, dq_ref, dk_ref, dv_ref,
             lb_buf, be_buf, g_buf, l_buf, a_buf, gb_buf, dz_buf, pg_buf, mask_buf, qh_buf, kh_buf, doh_buf):
        first = _head_masks()
        lane = lax.broadcasted_iota(jnp.int32, (1, LANES), 1)
        row = lax.broadcasted_iota(jnp.int32, (Q, Q), 0)
        col = lax.broadcasted_iota(jnp.int32, (Q, Q), 1)
        tri = jnp.where(row > col, -1.0, 0.0).astype(BF16)
        tpi = (row <= col).astype(BF16)
        heads = range(2)
        strips = [slice(r0, r0 + ATTN_STRIP) for r0 in range(0, Q, ATTN_STRIP)]
        rows = lambda j: pl.ds(pl.multiple_of(j * Q, Q), Q)
        wide = lambda t: jnp.tile(t, (1, Q // LANES))
        as_int = lambda t: int(t) if isinstance(t, (bool, int)) else t.astype(jnp.int32)

        keep = col < row
        mask_buf[0, 0] = jnp.full((Q, Q), LOG2E, F32)
        mask_buf[0, 1] = jnp.zeros((Q, Q), F32)
        mask_buf[1, 0] = jnp.where(keep, LOG2E, 0.0)
        mask_buf[1, 1] = jnp.where(keep, 0.0, MASKED)
        dq_ref[...] = jnp.zeros_like(dq_ref)
        dk_ref[...] = jnp.zeros_like(dk_ref)
        dv_ref[...] = jnp.zeros_like(dv_ref)
        _split_heads_into(qh_buf, q_ref, first, nq, Q)
        _split_heads_into(kh_buf, k_ref, first, nq, Q)
        _split_heads_into(doh_buf, do_ref, first, nq, Q)

        def scores(t):
            i, j = t
            kb = k_ref[rows(j), :]
            return tuple(_dot_nt(qh_buf[h, rows(i), :], kb) for h in heads)

        def logs(t, slot, z):
            i, j = t
            diag = as_int(i == j)
            for h in heads:
                for r in strips:
                    z2 = jnp.minimum(z[h][r] * LOG2E, Z2_MAX)
                    nl = jnp.log(1.0 + jnp.exp2(z2)) * mask_buf[diag, 0, r, :]
                    l_buf[slot, h, r, :] = nl.astype(BF16)
                    lb_buf[slot, h, r, :] = (z2 - nl) + mask_buf[diag, 1, r, :]

        def sums(t, slot):
            i, j = t
            vb = v_ref[rows(j), :]
            return (tuple(_dot(l_buf[slot, h], tri) for h in heads),
                    tuple(_dot_nt(doh_buf[h, rows(i), :], vb) for h in heads))

        def weights(t, slot, sm, da):
            i, j = t
            for h in heads:
                c = jnp.sum(jnp.where(lane == j, cl_ref[rows(i), h * LANES:(h + 1) * LANES], 0.0), axis=-1, keepdims=True)
                c = jnp.broadcast_to(c, (Q, LANES))
                for r in strips:
                    lb = lb_buf[slot, h, r, :]
                    a = jnp.exp2(lb + sm[h][r] + wide(c[r]))
                    g = da[h][r] * a
                    a_buf[slot, h, r, :] = a.astype(BF16)
                    be_buf[slot, h, r, :] = jnp.exp2(lb)
                    g_buf[slot, h, r, :] = g
                    gb_buf[slot, h, r, :] = g.astype(BF16)

        def prefix(t, slot):
            i, j = t
            dv_ref[rows(j), :] += (_dot_tn(a_buf[slot, 0], doh_buf[0, rows(i), :])
                                   + _dot_tn(a_buf[slot, 1], doh_buf[1, rows(i), :]))
            return tuple(_dot(gb_buf[slot, h], tpi) for h in heads)

        def dscores(t, slot, pm):
            i, j = t
            for h in heads:
                pg = pg_buf[h] * jnp.where(j == 0, 0.0, 1.0)
                for r in strips:
                    dz = g_buf[slot, h, r, :] - be_buf[slot, h, r, :] * (pm[h][r] + wide(pg[r]))
                    dz_buf[slot, h, r, :] = dz.astype(BF16)
                pg_buf[h] = pg + jnp.broadcast_to(pm[h][:, Q - 1:Q], (Q, LANES))

        def grads(t, slot):
            i, j = t
            dq_ref[rows(i), :] += (_dot(dz_buf[slot, 0], kh_buf[0, rows(j), :])
                                   + _dot(dz_buf[slot, 1], kh_buf[1, rows(j), :]))
            dk_ref[rows(j), :] += (_dot_tn(dz_buf[slot, 0], qh_buf[0, rows(i), :])
                                   + _dot_tn(dz_buf[slot, 1], qh_buf[1, rows(i), :]))

        def iteration(t, p):
            ta, tb, tc, td = t
            if ta is not None:
                z = scores(ta)
            if tb is not None:
                sm, da = sums(tb, 1 - p)
            if tc is not None:
                pm = prefix(tc, p)
            if td is not None:
                grads(td, 1 - p)
            if tb is not None:
                weights(tb, 1 - p, sm, da)
            if tc is not None:
                dscores(tc, p, pm)
            if ta is not None:
                logs(ta, p, z)

        def window(n):
            return tuple(tiles[n - k] if 0 <= n - k < ntiles else None for k in range(4))

        def following(t):
            i, j = t
            last = j == i
            return jnp.where(last, i + 1, i), jnp.where(last, 0, j + 1)

        peeled = 3 + (ntiles - 3) % unroll

        def unrolled_iterations(_, t):
            for n in range(peeled, peeled + unroll):
                iteration(t, n % 2)
                t = (following(t[0]),) + t[:3]
            return t

        pg_buf[...] = jnp.zeros_like(pg_buf)
        for n in range(peeled):
            iteration(window(n), n % 2)
        first_window = tuple((jnp.int32(i), jnp.int32(j)) for i, j in window(peeled))
        lax.fori_loop(0, (ntiles - peeled) // unroll, unrolled_iterations, first_window)
        for n in range(ntiles, ntiles + 3):
            iteration(window(n), n % 2)
        dq_ref[...] = dq_ref[...] * (1.0 / math.sqrt(SB_HEAD_DIM))

    col_block = lambda off: pl.BlockSpec((S, LANES), lambda p: (0, off + p), pipeline_mode=pl.Buffered(1))
    return pl.pallas_call(
        body, name="attn_bwd", grid=(npair,),
        out_shape=[jax.ShapeDtypeStruct((S, SB_WIDTH), F32)] * 3,
        in_specs=[col_block(0), col_block(npair), col_block(2 * npair), col_block(0),
                  pl.BlockSpec((S, 2 * LANES), lambda p: (0, p), pipeline_mode=pl.Buffered(1))],
        out_specs=[pl.BlockSpec((S, LANES), lambda p: (0, p), pipeline_mode=pl.Buffered(1))] * 3,
        scratch_shapes=[pltpu.VMEM((2, 2, Q, Q), F32)] * 3 + [pltpu.VMEM((2, 2, Q, Q), BF16)] * 4
        + [pltpu.VMEM((2, Q, LANES), F32), pltpu.VMEM((2, 2, Q, Q), F32)] + [pltpu.VMEM((2, S, LANES), BF16)] * 3,
        compiler_params=_cparams(dimension_semantics=("arbitrary",)),
    )(qkv, qkv, qkv, do, cl)


def _conv_bwd(u_conv, y_conv, dco, conv_w, ln_g, ln_b):
    S = u_conv.shape[0]
    tc = min(TOKEN_TILE, S)
    nt = S // tc
    per = tc // CONV_HALO
    groups = CONV_CHUNK // 8

    def body(u_ref, halo_ref, y_ref, dco_ref, cw_ref, lg_ref, lb_ref, du_ref, dcw_ref, dsm_ref, glu_ext, dyc_ext, sg_buf,
             dcw_acc, dsm_acc, glu_sh, dyc_sh):
        i = pl.program_id(0)
        ti = nt - 1 - i

        @pl.when(i == 0)
        def _():
            dyc_ext[tc:, :] = jnp.zeros((CONV_HALO, CONV_CH), F32)
            dcw_acc[...] = jnp.zeros_like(dcw_acc)
            dsm_acc[...] = jnp.zeros_like(dsm_acc)

        @pl.when(i > 0)
        def _():
            dyc_ext[tc:, :] = dyc_ext[0:CONV_HALO, :]

        glu_ext[0:CONV_HALO, :] = jnp.where(ti > 0, _glu(halo_ref[...])[2], 0.0)
        _, sg, glu = _glu(u_ref[...])
        glu_ext[CONV_HALO:, :] = glu
        sg_buf[...] = sg
        _shift_copies(glu_ext, glu_sh)

        dcb = jnp.zeros((8, CONV_CH), F32)
        dlg = jnp.zeros((8, CONV_CH), F32)
        dlb = jnp.zeros((8, CONV_CH), F32)
        fold = lambda t: jnp.sum(t.reshape(groups, 8, CONV_CH), axis=0)
        for r0 in range(0, tc, CONV_CHUNK):
            y = y_ref[r0:r0 + CONV_CHUNK, :]
            mu = jnp.mean(y, axis=-1, keepdims=True)
            yc = y - mu
            rstd = lax.rsqrt(jnp.mean(yc * yc, axis=-1, keepdims=True) + EPS)
            yn = yc * rstd
            yl = yn * lg_ref[...] + lb_ref[...]
            s = jax.nn.sigmoid(yl)
            dyl = dco_ref[r0:r0 + CONV_CHUNK, :] * (s * (1.0 + yl * (1.0 - s)))
            dlg = dlg + fold(dyl * yn)
            dlb = dlb + fold(dyl)
            wv = dyl * lg_ref[...]
            dyc = rstd * (wv - jnp.mean(wv, axis=-1, keepdims=True) - yn * jnp.mean(wv * yn, axis=-1, keepdims=True))
            dcb = dcb + fold(dyc)
            dyc_ext[r0:r0 + CONV_CHUNK, :] = dyc
        dsm_acc[0:8, :] += dcb
        dsm_acc[8:16, :] += dlg
        dsm_acc[16:24, :] += dlb
        _shift_copies(dyc_ext, dyc_sh)

        for r0 in range(0, tc, CONV_CHUNK):
            dyc = dyc_ext[r0:r0 + CONV_CHUNK, :]
            dglu = jnp.zeros((CONV_CHUNK, CONV_CH), F32)
            base = r0 + CONV_HALO - (CONV_WIDTH - 1)
            for w in range(CONV_WIDTH):
                back = r0 + (CONV_WIDTH - 1) - w
                dglu = dglu + cw_ref[w:w + 1, :] * _window(dyc_ext, dyc_sh, back, CONV_CHUNK)
                dcw_acc[8 * w:8 * w + 8, :] += fold(dyc * _window(glu_ext, glu_sh, base + w, CONV_CHUNK))
            sg = sg_buf[r0:r0 + CONV_CHUNK, :]
            v = u_ref[r0:r0 + CONV_CHUNK, :CONV_CH]
            du_ref[r0:r0 + CONV_CHUNK, :CONV_CH] = (dglu * sg).astype(BF16)
            du_ref[r0:r0 + CONV_CHUNK, CONV_CH:] = (dglu * v * sg * (1.0 - sg)).astype(BF16)

        @pl.when(i == nt - 1)
        def _():
            for w in range(CONV_WIDTH):
                dcw_ref[w:w + 1, :] = jnp.sum(dcw_acc[8 * w:8 * w + 8, :], axis=0, keepdims=True)
            dcw_ref[CONV_WIDTH:, :] = jnp.zeros((CONV_HALO - CONV_WIDTH, CONV_CH), F32)
            for k in range(3):
                dsm_ref[k:k + 1, :] = jnp.sum(dsm_acc[8 * k:8 * k + 8, :], axis=0, keepdims=True)
            dsm_ref[3:, :] = jnp.zeros((5, CONV_CH), F32)

    return pl.pallas_call(
        body, name="conv_bwd", grid=(nt,),
        out_shape=[jax.ShapeDtypeStruct((S, 2 * CONV_CH), BF16), jax.ShapeDtypeStruct((CONV_HALO, CONV_CH), F32),
                   jax.ShapeDtypeStruct((8, CONV_CH), F32)],
        in_specs=[pl.BlockSpec((tc, 2 * CONV_CH), lambda i: (nt - 1 - i, 0)),
                  pl.BlockSpec((CONV_HALO, 2 * CONV_CH), lambda i: (jnp.maximum((nt - 1 - i) * per - 1, 0), 0)),
                  pl.BlockSpec((tc, CONV_CH), lambda i: (nt - 1 - i, 0)), pl.BlockSpec((tc, CONV_CH), lambda i: (nt - 1 - i, 0)),
                  _const((CONV_HALO, CONV_CH)), _const((1, CONV_CH)), _const((1, CONV_CH))],
        out_specs=[pl.BlockSpec((tc, 2 * CONV_CH), lambda i: (nt - 1 - i, 0)), _const((CONV_HALO, CONV_CH)),
                   _const((8, CONV_CH))],
        scratch_shapes=[pltpu.VMEM((tc + CONV_HALO, CONV_CH), F32), pltpu.VMEM((tc + CONV_HALO, CONV_CH), F32),
                        pltpu.VMEM((tc, CONV_CH), F32), pltpu.VMEM((8 * CONV_HALO, CONV_CH), F32),
                        pltpu.VMEM((24, CONV_CH), F32)]
        + [pltpu.VMEM((SUBLANES - 1, tc + CONV_HALO - SUBLANES, CONV_CH), F32)] * 2,
        compiler_params=_cparams(dimension_semantics=("arbitrary",)),
    )(u_conv, u_conv, y_conv, dco, conv_w, ln_g, ln_b)


def _in_proj_bwd(du_conv, dq, dk, dv, w_in, x, g, dh1):
    S = x.shape[0]
    tm = min(TOKEN_TILE, S)
    nconv = 2 * CONV_CH

    def body(duc_ref, dq_ref, dk_ref, dv_ref, w_ref, x_ref, g_ref, dh1_ref, dx_ref, dg_ref):
        @pl.when(pl.program_id(0) == 0)
        def _():
            dg_ref[...] = jnp.zeros_like(dg_ref)

        da = _dot_nt(duc_ref[...], w_ref[:, :nconv])
        for n, ref in enumerate((dq_ref, dk_ref, dv_ref)):
            c0 = nconv + n * SB_WIDTH
            da = da + _dot_nt(ref[...].astype(BF16), w_ref[:, c0:c0 + SB_WIDTH])
        xf = x_ref[...]
        r = _rms_r(xf)
        dx_ref[...] = dh1_ref[...] + _rms_bwd(xf, r, g_ref[...], da)
        dg_ref[...] += jnp.sum(da * xf * r, axis=0, keepdims=True)

    row = lambda n: pl.BlockSpec((tm, n), lambda i: (i, 0))
    return pl.pallas_call(
        body, name="in_proj_bwd", grid=(S // tm,),
        out_shape=[jax.ShapeDtypeStruct((S, D_MODEL), F32), jax.ShapeDtypeStruct((1, D_MODEL), F32)],
        in_specs=[row(nconv), row(SB_WIDTH), row(SB_WIDTH), row(SB_WIDTH), _resident(w_in.shape), row(D_MODEL),
                  _const((1, D_MODEL)), row(D_MODEL)],
        out_specs=[row(D_MODEL), _const((1, D_MODEL))],
        compiler_params=_cparams(dimension_semantics=("arbitrary",)),
    )(du_conv, dq, dk, dv, w_in, x, g, dh1)


def _layer_grads(xs, target, g_pre_mix, w_in_f, conv_w_f, conv_b, conv_ln_g, conv_ln_b, attn_g, g_post_mix, g_pre_ffn,
                 g_post_ffn, late_weights, send_grads):
    a, u_conv, qkv = _in_proj(xs, g_pre_mix, w_in_f)
    conv_out, y_conv = _conv_fwd(u_conv, conv_w_f, conv_b, conv_ln_g, conv_ln_b)
    o, attn_out, cl = _attn_fwd(qkv, attn_g)
    w_out_f, w_gate_f, w_up_f, w_down_f = late_weights(attn_out)
    y, h1, f_in = _out_proj(conv_out, attn_out, w_out_f, xs, g_post_mix, g_pre_ffn)
    gt, up, df, dh2, loss_part, d_g_post_ffn = _ffn_fwd_loss(f_in, w_gate_f, w_up_f, w_down_f, h1, target, g_post_ffn)

    dgt, dup, act, dfin = _ffn_bwd(df, gt, up, w_gate_f, w_up_f, w_down_f)
    d_w_down = _matmul_tn("grad_w_down", act, df, 512)
    d_w_gate = _matmul_tn("grad_w_gate", f_in, dgt, FF_CHUNK)
    d_w_up = _matmul_tn("grad_w_up", f_in, dup, FF_CHUNK)
    sent = send_grads("ffn", (d_w_gate, d_w_up, d_w_down))
    dh1, dy, dco, do, d_g_pre_ffn, d_g_post_mix, d_attn_g = _mix_bwd(dfin, h1, y, dh2, g_pre_ffn + sent, g_post_mix,
                                                                     w_out_f, o, attn_g)
    d_w_out = jnp.concatenate([_matmul_tn("grad_w_out_conv", conv_out, dy, D_MODEL),
                               _matmul_tn("grad_w_out_attn", attn_out, dy, D_MODEL)], axis=0)
    sent = send_grads("w_out", (d_w_out,))
    dq, dk, dv = _attn_bwd(qkv, do, cl)
    du_conv, d_conv_w, d_conv_small = _conv_bwd(u_conv, y_conv, dco, conv_w_f, conv_ln_g + sent, conv_ln_b)
    d_w_in = jnp.concatenate([_matmul_tn("grad_w_in_conv", a, du_conv, 2 * CONV_CH),
                              _matmul_tn("grad_w_in_q", a, dq, SB_WIDTH), _matmul_tn("grad_w_in_k", a, dk, SB_WIDTH),
                              _matmul_tn("grad_w_in_v", a, dv, SB_WIDTH)], axis=1)
    sent = send_grads("w_in", (d_w_in,))
    grad_x, d_g_pre_mix = _in_proj_bwd(du_conv, dq, dk, dv, w_in_f, xs, g_pre_mix + sent, dh1)
    return (loss_part, grad_x, d_conv_w, d_conv_small, d_attn_g, d_g_pre_mix, d_g_post_mix, d_g_pre_ffn, d_g_post_ffn)


def _cols_to_blocks(w):
    K, N = w.shape
    return jnp.transpose(w.reshape(K, N_DEV, N // N_DEV), (1, 0, 2))


def _blocks_to_cols(blocks):
    n_dev, K, n = blocks.shape
    return jnp.transpose(blocks, (1, 0, 2)).reshape(K, n_dev * n)


def kernel(x, g_pre_mix, w_in, conv_w, conv_b, conv_ln_g, conv_ln_b, attn_norm_g, w_out, g_post_mix, g_pre_ffn, w_gate, w_up, w_down, g_post_ffn, loss_target, m_g_pre_mix, m_w_in, m_conv_w, m_conv_b, m_conv_ln_g, m_conv_ln_b, m_attn_norm_g, m_w_out, m_g_post_mix, m_g_pre_ffn, m_w_gate, m_w_up, m_w_down, m_g_post_ffn, v_g_pre_mix, v_w_in, v_conv_w, v_conv_b, v_conv_ln_g, v_conv_ln_b, v_attn_norm_g, v_w_out, v_g_post_mix, v_g_pre_ffn, v_w_gate, v_w_up, v_w_down, v_g_post_ffn):
    xs = x[0]
    target = loss_target[0]
    me = 4 * lax.axis_index("x") + 2 * lax.axis_index("y") + lax.axis_index("c")
    cw_shard = conv_w.reshape(CONV_WIDTH, CONV_CH // N_DEV)
    attn_g = attn_norm_g.reshape(1, SB_WIDTH)

    gathered = _all_gather([w_in[0].astype(BF16), cw_shard])
    w_in_f = _blocks_to_cols(gathered[0])
    conv_w_f = jnp.pad(_blocks_to_cols(gathered[1]), ((0, CONV_HALO - CONV_WIDTH), (0, 0)))
    gathered_zero = gathered[2][0:1, 0:1].astype(BF16)
    late = [w_out[0].astype(BF16) + gathered_zero, w_gate[0].astype(BF16), w_up[0].astype(BF16), w_down[0].astype(BF16)]
    late_started = _exchange_start("all_gather_late_start", late, scatter=False)

    def late_weights(after):
        lands = _exchange_wait("all_gather_late_wait", late_started, False, after)
        wo, wg, wu, wd = [lax.dynamic_update_index_in_dim(land, own, me, 0) for land, own in zip(lands, late)]
        return wo.reshape(D_MODEL, D_MODEL), _blocks_to_cols(wg), _blocks_to_cols(wu), wd.reshape(D_FF, D_MODEL)

    started = {}

    def send_grads(name, grads):
        blocks = [g.reshape(N_DEV, g.shape[0] // N_DEV, g.shape[1]) if g.shape[1] == D_MODEL else _cols_to_blocks(g)
                  for g in grads]
        payload = BF16 if name == "w_in" else F32
        sent = _exchange_start("reduce_scatter_" + name + "_start", [b.astype(payload) for b in blocks], scatter=True)
        started[name] = (sent, blocks)
        return sent[-1][0:1, 0:1]

    (loss_part, grad_x, d_conv_w, d_conv_small, d_attn_g, d_g_pre_mix, d_g_post_mix, d_g_pre_ffn,
     d_g_post_ffn) = _layer_grads(
        xs, target, g_pre_mix + late_started[-1][0:1, 0:1], w_in_f, conv_w_f, conv_b, conv_ln_g, conv_ln_b, attn_g,
        g_post_mix, g_pre_ffn, g_post_ffn, late_weights, send_grads)

    def reduced(name, after, shards):
        st, blocks = started[name]
        lands = _exchange_wait("reduce_scatter_" + name + "_wait", st, True, after)
        return [_sum_adamw("adamw_" + wn, land, lax.dynamic_index_in_dim(blk, me, 0, keepdims=False), w[0], m[0], v[0])
                for land, blk, (wn, w, m, v) in zip(lands, blocks, shards)]

    two = lambda t: t.reshape(2, CONV_CH)
    small_g = jnp.concatenate([
        d_conv_w,
        d_conv_small[0:3],
        d_attn_g,
        two(d_g_pre_mix), two(d_g_post_mix), two(d_g_pre_ffn), two(d_g_post_ffn),
        jnp.broadcast_to(loss_part[0:1, 0:1], (1, CONV_CH)),
        jnp.zeros((3, CONV_CH), F32)], axis=0)
    small_g = _all_reduce_small(small_g)
    loss = small_g[44, 0]
    g_conv_w = lax.dynamic_slice(small_g, (0, me * (CONV_CH // N_DEV)), (CONV_WIDTH, CONV_CH // N_DEV))
    pack = lambda cb, lg, lb, ag, g1, g2, g3, g4: jnp.concatenate(
        [cb, lg, lb, ag.reshape(1, SB_WIDTH), two(g1), two(g2), two(g3), two(g4), jnp.zeros((4, CONV_CH), F32)], axis=0)
    sm_g = small_g[CONV_HALO:]
    sm_delta, sm_m, sm_v = _adamw_small(
        "adamw_small",
        pack(conv_b, conv_ln_g, conv_ln_b, attn_norm_g, g_pre_mix, g_post_mix, g_pre_ffn, g_post_ffn), sm_g,
        pack(m_conv_b, m_conv_ln_g, m_conv_ln_b, m_attn_norm_g, m_g_pre_mix, m_g_post_mix, m_g_pre_ffn, m_g_post_ffn),
        pack(v_conv_b, v_conv_ln_g, v_conv_ln_b, v_attn_norm_g, v_g_pre_mix, v_g_post_mix, v_g_pre_ffn, v_g_post_ffn))
    cw_delta, cw_m, cw_v = _adamw_small("adamw_conv_w", cw_shard, g_conv_w,
                                        m_conv_w.reshape(cw_shard.shape), v_conv_w.reshape(cw_shard.shape))

    ffn = reduced("ffn", grad_x, [("w_gate", w_gate, m_w_gate, v_w_gate), ("w_up", w_up, m_w_up, v_w_up),
                                  ("w_down", w_down, m_w_down, v_w_down)])
    big = {"w_gate": ffn[0], "w_up": ffn[1], "w_down": ffn[2],
           "w_out": reduced("w_out", ffn[2][0], [("w_out", w_out, m_w_out, v_w_out)])[0]}
    big["w_in"] = reduced("w_in", big["w_out"][0], [("w_in", w_in, m_w_in, v_w_in)])[0]

    def unpack(t):
        return {"conv_b": t[0:1], "conv_ln_g": t[1:2], "conv_ln_b": t[2:3], "attn_norm_g": t[3:4].reshape(1, SB_HEADS, SB_HEAD_DIM),
                "g_pre_mix": t[4:6].reshape(1, D_MODEL), "g_post_mix": t[6:8].reshape(1, D_MODEL),
                "g_pre_ffn": t[8:10].reshape(1, D_MODEL), "g_post_ffn": t[10:12].reshape(1, D_MODEL)}

    names = ["g_pre_mix", "w_in", "conv_w", "conv_b", "conv_ln_g", "conv_ln_b", "attn_norm_g", "w_out", "g_post_mix",
             "g_pre_ffn", "w_gate", "w_up", "w_down", "g_post_ffn"]
    kinds = []
    for idx, small in enumerate((sm_g, sm_delta, sm_m, sm_v)):
        d = unpack(small)
        d["conv_w"] = (g_conv_w, cw_delta, cw_m, cw_v)[idx].reshape(1, CONV_WIDTH, 1, CONV_CH // N_DEV)
        for n in big:
            d[n] = big[n][idx][None]
        kinds.append([d[n] for n in names])

    return (loss, grad_x[None], *kinds[0], *kinds[1], *kinds[2], *kinds[3])
```

```python
import math

import jax
import jax.numpy as jnp
from jax import lax
from jax.experimental import pallas as pl
from jax.experimental.pallas import tpu as pltpu

F32 = jnp.float32
BF16 = jnp.bfloat16
MESH = pl.DeviceIdType.MESH

N_DEV = 8
D_MODEL = 1024
CONV_CH = 512
CONV_WIDTH = 31
SB_HEADS = 8
SB_HEAD_DIM = 64
SB_WIDTH = SB_HEADS * SB_HEAD_DIM
D_FF = 2816
EPS = 1e-6
LOG2E = 1.4426950408889634
Z2_MAX = 100.0
MASKED = -1e30
ADAM_LR = 0.001
ADAM_B1 = 0.9
ADAM_B2 = 0.999
ADAM_EPS = 1e-08
ADAM_WD = 0.01
ADAM_STEP = 10

SUBLANES = 8
LANES = 128
VMEM_LIMIT = 56 * 1024 * 1024
TOKEN_TILE = 512
GRAD_TILE = 1024
FFN_TILE = 256
ATTN_FWD_UNROLL = 8
ATTN_BWD_UNROLL = 4
ATTN_STRIP = 32
ATTN_BLOCK = 256
CONV_HALO = 32
CONV_CHUNK = 64


def _cparams(**kw):
    return pltpu.CompilerParams(vmem_limit_bytes=VMEM_LIMIT, **kw)


def _resident(shape):
    return pl.BlockSpec(shape, lambda *_: (0,) * len(shape), pipeline_mode=pl.Buffered(1))


def _const(shape):
    return pl.BlockSpec(shape, lambda *_: (0,) * len(shape))


def _rms_r(xf):
    return lax.rsqrt(jnp.mean(xf * xf, axis=-1, keepdims=True) + EPS)


def _rms_bwd(xf, r, g, dout):
    w = dout * g
    return r * (w - xf * (r * r) * jnp.mean(w * xf, axis=-1, keepdims=True))


def _dot(a, b):
    return jnp.dot(a, b, preferred_element_type=F32)


def _dot_nt(a, b):
    return lax.dot_general(a, b, (((1,), (1,)), ((), ())), preferred_element_type=F32)


def _dot_tn(a, b):
    return lax.dot_general(a, b, (((0,), (0,)), ((), ())), preferred_element_type=F32)


def _peer(x, y, c, k):
    px = 1 - x if (k >> 2) & 1 else x
    py = 1 - y if (k >> 1) & 1 else y
    pc = 1 - c if k & 1 else c
    return (px, py, pc), 4 * px + 2 * py + pc


def _all_gather(shards):
    n = len(shards)

    def body(*refs):
        ins, outs, done = refs[:n], refs[n:2 * n], refs[2 * n]
        send_sems, recv_sems, local_sems = refs[2 * n + 1:]
        x, y, c = lax.axis_index("x"), lax.axis_index("y"), lax.axis_index("c")
        me, sibling = (x, y, c), (x, y, 1 - c)
        chips = [(1 - x, y), (x, 1 - y), (1 - x, 1 - y)]
        number = lambda d: 4 * d[0] + 2 * d[1] + d[2]

        def copy(a, k, block, to, src=None):
            rows = outs[a].at[number(block)]
            return pltpu.make_async_remote_copy(
                src_ref=rows if src is None else src, dst_ref=rows, send_sem=send_sems.at[a * (N_DEV - 1) + k],
                recv_sem=recv_sems.at[a * (N_DEV - 1) + k], device_id=to, device_id_type=MESH)

        copies = [pltpu.make_async_copy(ins[a], outs[a].at[number(me)], local_sems.at[a]) for a in range(n)]
        for mine in copies:
            mine.start()
        sent = [copy(a, 0, me, sibling, src=ins[a]) for a in range(n)]
        sent += [copy(a, 1 + j, me, (*chip, c), src=ins[a]) for j, chip in enumerate(chips) for a in range(n)]
        for cp in sent:
            cp.start()
        for j, chip in enumerate(chips):
            for a in range(n):
                copy(a, 1 + j, (*chip, c), me).wait_recv()
                passed = copy(a, 4 + j, (*chip, c), sibling)
                passed.start()
                sent.append(passed)
        for a in range(n):
            copy(a, 0, sibling, me).wait_recv()
            for j, chip in enumerate(chips):
                copy(a, 4 + j, (*chip, 1 - c), me).wait_recv()
        for cp in sent:
            cp.wait_send()
        for mine in copies:
            mine.wait()
        done[...] = jnp.zeros_like(done)

    any_spec = pl.BlockSpec(memory_space=pl.ANY)
    return pl.pallas_call(
        body, name="all_gather_weights",
        out_shape=[jax.ShapeDtypeStruct((N_DEV,) + s.shape, s.dtype) for s in shards] + [jax.ShapeDtypeStruct((8, LANES), F32)],
        in_specs=[any_spec] * n, out_specs=[any_spec] * n + [pl.BlockSpec(memory_space=pltpu.VMEM)],
        scratch_shapes=[pltpu.SemaphoreType.DMA((n * (N_DEV - 1),)), pltpu.SemaphoreType.DMA((n * (N_DEV - 1),)),
                        pltpu.SemaphoreType.DMA((n,))],
        compiler_params=pltpu.CompilerParams(has_side_effects=True),
    )(*shards)


def _adamw(w, g, m, v):
    m = ADAM_B1 * m + (1.0 - ADAM_B1) * g
    v = ADAM_B2 * v + (1.0 - ADAM_B2) * (g * g)
    m_hat = m / (1.0 - ADAM_B1 ** ADAM_STEP)
    v_hat = v / (1.0 - ADAM_B2 ** ADAM_STEP)
    delta = -ADAM_LR * (m_hat / (jnp.sqrt(v_hat) + ADAM_EPS) + ADAM_WD * w)
    return delta, m, v


def _exchange_and_sum(src_block, recv_ref, send_sems, recv_sems, local_sem):
    x, y, c = lax.axis_index("x"), lax.axis_index("y"), lax.axis_index("c")
    me = 4 * x + 2 * y + c
    mine = pltpu.make_async_copy(src_block(me), recv_ref.at[me], local_sem)
    mine.start()
    for k in range(1, N_DEV):
        peer, peer_block = _peer(x, y, c, k)
        pltpu.make_async_remote_copy(
            src_ref=src_block(peer_block), dst_ref=recv_ref.at[me], send_sem=send_sems.at[k - 1],
            recv_sem=recv_sems.at[k - 1], device_id=peer, device_id_type=MESH).start()
    for k in range(1, N_DEV):
        peer, peer_block = _peer(x, y, c, k)
        arrived = pltpu.make_async_remote_copy(
            src_ref=src_block(peer_block), dst_ref=recv_ref.at[peer_block], send_sem=send_sems.at[k - 1],
            recv_sem=recv_sems.at[k - 1], device_id=peer, device_id_type=MESH)
        arrived.wait_send()
        arrived.wait_recv()
    mine.wait()


HBM_SPEC = pl.BlockSpec(memory_space=pltpu.HBM)
SEM_SPEC = pl.BlockSpec(memory_space=pltpu.SEMAPHORE)
DATAFLOW = pltpu.SideEffectType.DATAFLOW_SIDE_EFFECTING


def _exchange_copies(srcs, lands, send_sems, recv_sems, scatter, wait):
    x, y, c = lax.axis_index("x"), lax.axis_index("y"), lax.axis_index("c")
    me = 4 * x + 2 * y + c
    for k in range(1, N_DEV):
        peer, peer_block = _peer(x, y, c, k)
        for a in range(len(srcs)):
            s = a * (N_DEV - 1) + k - 1
            src = srcs[a].at[peer_block] if scatter else srcs[a]
            copy = pltpu.make_async_remote_copy(
                src_ref=src, dst_ref=lands[a].at[peer_block if wait else me], send_sem=send_sems.at[s],
                recv_sem=recv_sems.at[s], device_id=peer, device_id_type=MESH)
            if wait:
                copy.wait_send()
                copy.wait_recv()
            else:
                copy.start()


def _exchange_start(name, arrays, scatter):
    n = len(arrays)
    land_shapes = [a.shape if scatter else (N_DEV,) + a.shape for a in arrays]

    def body(*refs):
        _exchange_copies(refs[:n], refs[n:2 * n], refs[2 * n], refs[2 * n + 1], scatter, wait=False)
        refs[-1][...] = jnp.zeros_like(refs[-1])

    sems = pltpu.SemaphoreType.DMA((n * (N_DEV - 1),))
    hbm = lambda t: pltpu.with_memory_space_constraint(t, pltpu.HBM)
    return pl.pallas_call(
        body, name=name,
        out_shape=(sems, sems, *[pltpu.HBM(a.shape, a.dtype) for a in arrays],
                   *[pltpu.HBM(ls, a.dtype) for ls, a in zip(land_shapes, arrays)], jax.ShapeDtypeStruct((8, LANES), F32)),
        in_specs=[HBM_SPEC] * (2 * n),
        out_specs=(SEM_SPEC, SEM_SPEC, *[HBM_SPEC] * (2 * n), pl.BlockSpec(memory_space=pltpu.VMEM)),
        input_output_aliases={a: 2 + a for a in range(2 * n)},
        compiler_params=pltpu.CompilerParams(has_side_effects=DATAFLOW),
    )(*[hbm(a) for a in arrays], *[hbm(lax.empty(ls, a.dtype)) for ls, a in zip(land_shapes, arrays)])


def _exchange_wait(name, started, scatter, after):
    n = (len(started) - 3) // 2
    send_sems, recv_sems = started[0], started[1]
    arrays, lands = started[2:2 + n], started[2 + n:2 + 2 * n]

    def body(*refs):
        _exchange_copies(refs[:n], refs[n:2 * n], refs[2 * n], refs[2 * n + 1], scatter, wait=True)

    return pl.pallas_call(
        body, name=name,
        out_shape=[pltpu.HBM(t.shape, t.dtype) for t in (*arrays, *lands)],
        in_specs=[HBM_SPEC] * (2 * n) + [SEM_SPEC, SEM_SPEC, pl.BlockSpec(memory_space=pl.ANY)],
        out_specs=[HBM_SPEC] * (2 * n),
        input_output_aliases={a: a for a in range(2 * n)},
        compiler_params=pltpu.CompilerParams(has_side_effects=DATAFLOW),
    )(*arrays, *lands, send_sems, recv_sems, after)[n:]


def _sum_adamw(name, land, own, w, m, v):
    _, M, N = land.shape
    rows = math.gcd(M, 128)

    def body(land_ref, own_ref, w_ref, m_ref, v_ref, grad_ref, delta_ref, nm_ref, nv_ref):
        x, y, c = lax.axis_index("x"), lax.axis_index("y"), lax.axis_index("c")
        g = own_ref[...]
        for k in range(1, N_DEV):
            g = g + land_ref[_peer(x, y, c, k)[1]].astype(F32)
        delta, nm, nv = _adamw(w_ref[...], g, m_ref[...], v_ref[...])
        grad_ref[...] = g
        delta_ref[...] = delta
        nm_ref[...] = nm
        nv_ref[...] = nv

    row = pl.BlockSpec((rows, N), lambda i: (i, 0))
    return pl.pallas_call(
        body, name=name, grid=(M // rows,), out_shape=[jax.ShapeDtypeStruct((M, N), F32)] * 4,
        in_specs=[pl.BlockSpec((N_DEV, rows, N), lambda i: (0, i, 0)), row, row, row, row], out_specs=[row] * 4,
        compiler_params=_cparams(),
    )(land, own, w, m, v)


def _all_reduce_small(g):
    R, C = g.shape

    def body(g_ref, out_ref, recv_ref, send_sems, recv_sems, local_sem):
        _exchange_and_sum(lambda b: g_ref, recv_ref, send_sems, recv_sems, local_sem)
        total = recv_ref[0]
        for b in range(1, N_DEV):
            total = total + recv_ref[b]
        out_ref[...] = total

    vmem = pl.BlockSpec(memory_space=pltpu.VMEM)
    return pl.pallas_call(
        body, name="all_reduce_small_grads", out_shape=jax.ShapeDtypeStruct((R, C), F32),
        in_specs=[vmem], out_specs=vmem,
        scratch_shapes=[pltpu.VMEM((N_DEV, R, C), F32), pltpu.SemaphoreType.DMA((N_DEV - 1,)),
                        pltpu.SemaphoreType.DMA((N_DEV - 1,)), pltpu.SemaphoreType.DMA(())],
        compiler_params=_cparams(has_side_effects=True),
    )(g)


def _adamw_small(name, w, g, m, v):
    def body(w_ref, g_ref, m_ref, v_ref, delta_ref, nm_ref, nv_ref):
        delta, nm, nv = _adamw(w_ref[...], g_ref[...], m_ref[...], v_ref[...])
        delta_ref[...] = delta
        nm_ref[...] = nm
        nv_ref[...] = nv

    vmem = pl.BlockSpec(memory_space=pltpu.VMEM)
    return pl.pallas_call(body, name=name, out_shape=[jax.ShapeDtypeStruct(w.shape, F32)] * 3,
                          in_specs=[vmem] * 4, out_specs=[vmem] * 3)(w, g, m, v)


def _in_proj(x, g, w_in):
    S = x.shape[0]
    tm = min(TOKEN_TILE, S)
    nconv = 2 * CONV_CH

    def body(x_ref, g_ref, w_ref, a_ref, uc_ref, qkv_ref):
        xf = x_ref[...]
        a = (xf * _rms_r(xf) * g_ref[...]).astype(BF16)
        a_ref[...] = a
        uc_ref[...] = _dot(a, w_ref[:, :nconv])
        qkv_ref[:, :SB_WIDTH] = (_dot(a, w_ref[:, nconv:nconv + SB_WIDTH]) * (1.0 / math.sqrt(SB_HEAD_DIM))).astype(BF16)
        qkv_ref[:, SB_WIDTH:] = _dot(a, w_ref[:, nconv + SB_WIDTH:]).astype(BF16)

    row = lambda n: pl.BlockSpec((tm, n), lambda i: (i, 0))
    return pl.pallas_call(
        body, name="in_proj", grid=(S // tm,),
        out_shape=[jax.ShapeDtypeStruct((S, D_MODEL), BF16), jax.ShapeDtypeStruct((S, nconv), F32),
                   jax.ShapeDtypeStruct((S, 3 * SB_WIDTH), BF16)],
        in_specs=[row(D_MODEL), _const((1, D_MODEL)), _resident(w_in.shape)],
        out_specs=[row(D_MODEL), row(nconv), row(3 * SB_WIDTH)],
        compiler_params=_cparams(),
    )(x, g, w_in)


def _glu(u):
    val, gate = u[:, :CONV_CH], u[:, CONV_CH:]
    sg = jax.nn.sigmoid(gate)
    return val, sg, val * sg


def _shift_copies(ext, shifted):
    n = shifted.shape[1]
    for r in range(1, SUBLANES):
        shifted[r - 1] = ext[r:r + n, :]


def _window(ext, shifted, start, rows):
    r = start % SUBLANES
    return ext[start:start + rows, :] if r == 0 else shifted[r - 1, start - r:start - r + rows, :]


def _conv_rows(glu_ext, glu_sh, cw_ref, r0, rows):
    base = r0 + CONV_HALO - (CONV_WIDTH - 1)
    acc = cw_ref[0:1, :] * _window(glu_ext, glu_sh, base, rows)
    for w in range(1, CONV_WIDTH):
        acc = acc + cw_ref[w:w + 1, :] * _window(glu_ext, glu_sh, base + w, rows)
    return acc


def _conv_fwd(u_conv, conv_w, conv_b, ln_g, ln_b):
    S = u_conv.shape[0]
    tc = min(TOKEN_TILE, S)

    def body(u_ref, cw_ref, cb_ref, lg_ref, lb_ref, out_ref, y_ref, glu_ext, glu_sh):
        i = pl.program_id(0)

        @pl.when(i == 0)
        def _():
            glu_ext[0:CONV_HALO, :] = jnp.zeros((CONV_HALO, CONV_CH), F32)

        @pl.when(i > 0)
        def _():
            glu_ext[0:CONV_HALO, :] = glu_ext[tc:tc + CONV_HALO, :]

        glu_ext[CONV_HALO:, :] = _glu(u_ref[...])[2]
        _shift_copies(glu_ext, glu_sh)
        for r0 in range(0, tc, CONV_CHUNK):
            y = _conv_rows(glu_ext, glu_sh, cw_ref, r0, CONV_CHUNK) + cb_ref[...]
            y_ref[r0:r0 + CONV_CHUNK, :] = y
            mu = jnp.mean(y, axis=-1, keepdims=True)
            yc = y - mu
            yn = yc * lax.rsqrt(jnp.mean(yc * yc, axis=-1, keepdims=True) + EPS)
            yl = yn * lg_ref[...] + lb_ref[...]
            out_ref[r0:r0 + CONV_CHUNK, :] = (yl * jax.nn.sigmoid(yl)).astype(BF16)

    return pl.pallas_call(
        body, name="conv_fwd", grid=(S // tc,),
        out_shape=[jax.ShapeDtypeStruct((S, CONV_CH), BF16), jax.ShapeDtypeStruct((S, CONV_CH), F32)],
        in_specs=[pl.BlockSpec((tc, 2 * CONV_CH), lambda i: (i, 0)), _const((CONV_HALO, CONV_CH)),
                  _const((1, CONV_CH)), _const((1, CONV_CH)), _const((1, CONV_CH))],
        out_specs=[pl.BlockSpec((tc, CONV_CH), lambda i: (i, 0))] * 2,
        scratch_shapes=[pltpu.VMEM((tc + CONV_HALO, CONV_CH), F32),
                        pltpu.VMEM((SUBLANES - 1, tc + CONV_HALO - SUBLANES, CONV_CH), F32)],
        compiler_params=_cparams(dimension_semantics=("arbitrary",)),
    )(u_conv, conv_w, conv_b, ln_g, ln_b)


def _head_masks():
    lane = lax.broadcasted_iota(jnp.int32, (1, LANES), 1)
    return lane < SB_HEAD_DIM


def _split_heads(t, first):
    z = jnp.zeros_like(t)
    return jnp.where(first, t, z), jnp.where(first, z, t)


def _split_heads_into(halves, src, first, nblocks, rows):
    def one_block(b, carry):
        r = pl.ds(pl.multiple_of(b * rows, rows), rows)
        halves[0, r, :], halves[1, r, :] = _split_heads(src[r, :], first)
        return carry

    lax.fori_loop(0, nblocks, one_block, 0)


def _head_sum(t, first):
    a = jnp.sum(jnp.where(first, t, 0.0), axis=-1, keepdims=True)
    b = jnp.sum(jnp.where(first, 0.0, t), axis=-1, keepdims=True)
    return a, b


def _attn_fwd(qkv, g_attn):
    S = qkv.shape[0]
    Q = min(ATTN_BLOCK, S)
    nq = S // Q
    assert nq <= LANES
    ntiles = nq * (nq + 1) // 2
    unroll = ATTN_FWD_UNROLL
    assert unroll % 2 == 0 and ntiles >= 3 + unroll
    npair = SB_WIDTH // LANES
    tiles = [(i, j) for i in range(nq) for j in range(i, -1, -1)]

    def body(q_ref, k_ref, v_ref, g_ref, o_ref, ao_ref, cl_ref, l_buf, z2_buf, a_buf, c_buf, mask_buf, qh_buf,
             vh_buf):
        first = _head_masks()
        lane = lax.broadcasted_iota(jnp.int32, (1, LANES), 1)
        row = lax.broadcasted_iota(jnp.int32, (Q, Q), 0)
        col = lax.broadcasted_iota(jnp.int32, (Q, Q), 1)
        tri = jnp.where(row >= col, -1.0, 0.0).astype(BF16)
        heads = range(2)
        strips = [slice(r0, r0 + ATTN_STRIP) for r0 in range(0, Q, ATTN_STRIP)]
        rows = lambda j: pl.ds(pl.multiple_of(j * Q, Q), Q)
        wide = lambda t: jnp.tile(t, (1, Q // LANES))
        as_int = lambda t: int(t) if isinstance(t, (bool, int)) else t.astype(jnp.int32)

        keep = col < row
        mask_buf[0, 0] = jnp.full((Q, Q), LOG2E, F32)
        mask_buf[0, 1] = jnp.zeros((Q, Q), F32)
        mask_buf[1, 0] = jnp.where(keep, LOG2E, 0.0)
        mask_buf[1, 1] = jnp.where(keep, 0.0, MASKED)
        o_ref[...] = jnp.zeros_like(o_ref)
        _split_heads_into(qh_buf, q_ref, first, nq, Q)
        _split_heads_into(vh_buf, v_ref, first, nq, Q)

        def scores(t):
            i, j = t
            kb = k_ref[rows(j), :]
            return tuple(_dot_nt(qh_buf[h, rows(i), :], kb) for h in heads)

        def logs(t, slot, z):
            i, j = t
            diag = as_int(i == j)
            for h in heads:
                for r in strips:
                    z2 = jnp.minimum(z[h][r] * LOG2E, Z2_MAX)
                    nl = jnp.log(1.0 + jnp.exp2(z2)) * mask_buf[diag, 0, r, :]
                    l_buf[slot, h, r, :] = nl.astype(BF16)
                    z2_buf[slot, h, r, :] = z2 + mask_buf[diag, 1, r, :]

        def sums(slot):
            return tuple(_dot(l_buf[slot, h], tri) for h in heads)

        def weights(t, slot, sm):
            i, j = t
            running = jnp.where(j == i, 0.0, 1.0)
            for h in heads:
                before = c_buf[h] * running
                for r in strips:
                    a_buf[slot, h, r, :] = jnp.exp2(z2_buf[slot, h, r, :] + sm[h][r] + wide(before[r])).astype(BF16)
                hl = slice(h * LANES, (h + 1) * LANES)
                cl_ref[rows(i), hl] = jnp.where(lane == j, before, cl_ref[rows(i), hl])
                c_buf[h] = before + jnp.broadcast_to(sm[h][:, 0:1], (Q, LANES))

        def values(t, slot):
            i, j = t
            o_ref[rows(i), :] += _dot(a_buf[slot, 0], vh_buf[0, rows(j), :]) + _dot(a_buf[slot, 1], vh_buf[1, rows(j), :])

        def iteration(t, p):
            ta, tc, td = t
            if ta is not None:
                z = scores(ta)
            if tc is not None:
                sm = sums(1 - p)
            if td is not None:
                values(td, p)
            if tc is not None:
                weights(tc, 1 - p, sm)
            if ta is not None:
                logs(ta, p, z)

        def window(n):
            return tuple(tiles[n - k] if 0 <= n - k < ntiles else None for k in range(3))

        def following(t):
            i, j = t
            last = j == 0
            return jnp.where(last, i + 1, i), jnp.where(last, i + 1, j - 1)

        peeled = 3 + (ntiles - 3) % unroll

        def unrolled_iterations(_, t):
            for n in range(peeled, peeled + unroll):
                iteration(t, n % 2)
                t = (following(t[0]),) + t[:2]
            return t

        c_buf[...] = jnp.zeros_like(c_buf)
        for n in range(peeled):
            iteration(window(n), n % 2)
        first_window = tuple((jnp.int32(i), jnp.int32(j)) for i, j in window(peeled))
        lax.fori_loop(0, (ntiles - peeled) // unroll, unrolled_iterations, first_window)
        for n in range(ntiles, ntiles + 2):
            iteration(window(n), n % 2)

        def head_norm(b, carry):
            o = o_ref[rows(b), :]
            sa, sb = _head_sum(o * o, first)
            r = jnp.where(first, lax.rsqrt(sa * (1.0 / SB_HEAD_DIM) + EPS), lax.rsqrt(sb * (1.0 / SB_HEAD_DIM) + EPS))
            ao_ref[rows(b), :] = (o * r * g_ref[...]).astype(BF16)
            return carry

        lax.fori_loop(0, nq, head_norm, 0)

    col_block = lambda off: pl.BlockSpec((S, LANES), lambda p: (0, off + p), pipeline_mode=pl.Buffered(1))
    out_block = lambda n: pl.BlockSpec((S, n), lambda p: (0, p), pipeline_mode=pl.Buffered(1))
    return pl.pallas_call(
        body, name="attn_fwd", grid=(npair,),
        out_shape=[jax.ShapeDtypeStruct((S, SB_WIDTH), F32), jax.ShapeDtypeStruct((S, SB_WIDTH), BF16),
                   jax.ShapeDtypeStruct((S, 2 * SB_WIDTH), F32)],
        in_specs=[col_block(0), col_block(npair), col_block(2 * npair), pl.BlockSpec((1, LANES), lambda p: (0, p))],
        out_specs=[out_block(LANES), out_block(LANES), out_block(2 * LANES)],
        scratch_shapes=[pltpu.VMEM((2, 2, Q, Q), BF16), pltpu.VMEM((2, 2, Q, Q), F32),
                        pltpu.VMEM((2, 2, Q, Q), BF16), pltpu.VMEM((2, Q, LANES), F32), pltpu.VMEM((2, 2, Q, Q), F32),
                        pltpu.VMEM((2, S, LANES), BF16), pltpu.VMEM((2, S, LANES), BF16)],
        compiler_params=_cparams(dimension_semantics=("arbitrary",)),
    )(qkv, qkv, qkv, g_attn)


def _out_proj(conv_out, attn_out, w_out, x, g_post_mix, g_pre_ffn):
    S = x.shape[0]
    tm = min(TOKEN_TILE, S)

    def body(co_ref, ao_ref, w_ref, x_ref, g1_ref, g2_ref, y_ref, h1_ref, fin_ref):
        y = _dot(co_ref[...], w_ref[:CONV_CH, :]) + _dot(ao_ref[...], w_ref[CONV_CH:, :])
        h1 = x_ref[...] + y * _rms_r(y) * g1_ref[...]
        y_ref[...] = y
        h1_ref[...] = h1
        fin_ref[...] = (h1 * _rms_r(h1) * g2_ref[...]).astype(BF16)

    row = lambda n: pl.BlockSpec((tm, n), lambda i: (i, 0))
    return pl.pallas_call(
        body, name="out_proj", grid=(S // tm,),
        out_shape=[jax.ShapeDtypeStruct((S, D_MODEL), F32), jax.ShapeDtypeStruct((S, D_MODEL), F32),
                   jax.ShapeDtypeStruct((S, D_MODEL), BF16)],
        in_specs=[row(CONV_CH), row(SB_WIDTH), _resident(w_out.shape), row(D_MODEL), _const((1, D_MODEL)),
                  _const((1, D_MODEL))],
        out_specs=[row(D_MODEL)] * 3,
        compiler_params=_cparams(),
    )(conv_out, attn_out, w_out, x, g_post_mix, g_pre_ffn)


def _ffn_fwd_loss(f_in, w_gate, w_up, w_down, h1, target, g_post_ffn):
    S = f_in.shape[0]
    tm = min(TOKEN_TILE, S)
    nt = S // tm

    def body(fin_ref, wg_ref, wu_ref, wd_ref, h1_ref, t_ref, g_ref, gt_ref, up_ref, df_ref, dh2_ref, loss_ref, dg_ref,
             sq_acc):
        i = pl.program_id(0)

        @pl.when(i == 0)
        def _():
            sq_acc[...] = jnp.zeros_like(sq_acc)
            dg_ref[...] = jnp.zeros_like(dg_ref)

        fin = fin_ref[...]
        gt = _dot(fin, wg_ref[...])
        up = _dot(fin, wu_ref[...])
        gt_ref[...] = gt.astype(BF16)
        up_ref[...] = up.astype(BF16)
        f = _dot((gt * jax.nn.sigmoid(gt) * up).astype(BF16), wd_ref[...])
        r = _rms_r(f)
        g = g_ref[...]
        diff = h1_ref[...] + f * r * g - t_ref[...]
        sq_acc[...] += jnp.sum(diff * diff, axis=0, keepdims=True)
        dh2 = diff * (1.0 / D_MODEL)
        dh2_ref[...] = dh2
        dg_ref[...] += jnp.sum(dh2 * f * r, axis=0, keepdims=True)
        df_ref[...] = _rms_bwd(f, r, g, dh2).astype(BF16)

        @pl.when(i == nt - 1)
        def _():
            loss_ref[...] = jnp.broadcast_to((0.5 / D_MODEL) * jnp.sum(sq_acc[...], axis=-1, keepdims=True), (1, LANES))

    row = lambda n: pl.BlockSpec((tm, n), lambda i: (i, 0))
    return pl.pallas_call(
        body, name="ffn_fwd_loss", grid=(nt,),
        out_shape=[jax.ShapeDtypeStruct((S, D_FF), BF16), jax.ShapeDtypeStruct((S, D_FF), BF16),
                   jax.ShapeDtypeStruct((S, D_MODEL), BF16), jax.ShapeDtypeStruct((S, D_MODEL), F32),
                   jax.ShapeDtypeStruct((1, LANES), F32), jax.ShapeDtypeStruct((1, D_MODEL), F32)],
        in_specs=[row(D_MODEL), _resident(w_gate.shape), _resident(w_up.shape), _resident(w_down.shape),
                  row(D_MODEL), row(D_MODEL), _const((1, D_MODEL))],
        out_specs=[row(D_FF), row(D_FF), row(D_MODEL), row(D_MODEL), _const((1, LANES)), _const((1, D_MODEL))],
        scratch_shapes=[pltpu.VMEM((1, D_MODEL), F32)],
        compiler_params=_cparams(dimension_semantics=("arbitrary",)),
    )(f_in, w_gate, w_up, w_down, h1, target, g_post_ffn)


def _ffn_bwd(df, gt, up, w_gate, w_up, w_down):
    S = df.shape[0]
    tm = min(FFN_TILE, S)

    def body(df_ref, gt_ref, up_ref, wg_ref, wu_ref, wd_ref, dgt_ref, dup_ref, act_ref, dfin_ref):
        dact = _dot_nt(df_ref[...], wd_ref[...])
        gt = gt_ref[...].astype(F32)
        up = up_ref[...].astype(F32)
        s = jax.nn.sigmoid(gt)
        silu = gt * s
        dgt = (dact * up * (s * (1.0 + gt * (1.0 - s)))).astype(BF16)
        dup = (dact * silu).astype(BF16)
        act_ref[...] = (silu * up).astype(BF16)
        dgt_ref[...] = dgt
        dup_ref[...] = dup
        dfin_ref[...] = _dot_nt(dgt, wg_ref[...]) + _dot_nt(dup, wu_ref[...])

    row = lambda n: pl.BlockSpec((tm, n), lambda i: (i, 0))
    return pl.pallas_call(
        body, name="ffn_bwd", grid=(S // tm,),
        out_shape=[jax.ShapeDtypeStruct((S, D_FF), BF16)] * 3 + [jax.ShapeDtypeStruct((S, D_MODEL), F32)],
        in_specs=[row(D_MODEL), row(D_FF), row(D_FF), _resident(w_gate.shape), _resident(w_up.shape),
                  _resident(w_down.shape)],
        out_specs=[row(D_FF)] * 3 + [row(D_MODEL)],
        compiler_params=_cparams(),
    )(df, gt, up, w_gate, w_up, w_down)


def _matmul_tn(name, x, y, tn):
    S, K = x.shape
    N = y.shape[1]
    ts = min(GRAD_TILE, S)

    def body(x_ref, y_ref, o_ref):
        @pl.when(pl.program_id(1) == 0)
        def _():
            o_ref[...] = jnp.zeros_like(o_ref)

        o_ref[...] += _dot_tn(x_ref[...].astype(BF16), y_ref[...].astype(BF16))

    return pl.pallas_call(
        body, name=name, grid=(N // tn, S // ts),
        out_shape=jax.ShapeDtypeStruct((K, N), F32),
        in_specs=[pl.BlockSpec((ts, K), lambda n, s: (s, 0)), pl.BlockSpec((ts, tn), lambda n, s: (s, n))],
        out_specs=pl.BlockSpec((K, tn), lambda n, s: (0, n)),
        compiler_params=_cparams(dimension_semantics=("arbitrary", "arbitrary")),
    )(x, y)


def _mix_bwd(dfin, h1, y, dh2, g_pre_ffn, g_post_mix, w_out, o, g_attn):
    S = dfin.shape[0]
    tm = min(TOKEN_TILE, S)
    inv_dh = 1.0 / SB_HEAD_DIM

    def body(dfin_ref, h1_ref, y_ref, dh2_ref, g2_ref, g1_ref, w_ref, o_ref, ga_ref, dh1_ref, dy_ref, dco_ref, do_ref,
             dg2_ref, dg1_ref, dga_ref):
        @pl.when(pl.program_id(0) == 0)
        def _():
            dg2_ref[...] = jnp.zeros_like(dg2_ref)
            dg1_ref[...] = jnp.zeros_like(dg1_ref)
            dga_ref[...] = jnp.zeros_like(dga_ref)

        h1, dfin = h1_ref[...], dfin_ref[...]
        r2 = _rms_r(h1)
        dh1 = dh2_ref[...] + _rms_bwd(h1, r2, g2_ref[...], dfin)
        dg2_ref[...] += jnp.sum(dfin * h1 * r2, axis=0, keepdims=True)
        y = y_ref[...]
        r1 = _rms_r(y)
        dy = _rms_bwd(y, r1, g1_ref[...], dh1).astype(BF16)
        dg1_ref[...] += jnp.sum(dh1 * y * r1, axis=0, keepdims=True)
        dh1_ref[...] = dh1
        dy_ref[...] = dy
        dco_ref[...] = _dot_nt(dy, w_ref[:CONV_CH, :])
        dao_all = _dot_nt(dy, w_ref[CONV_CH:, :])
        first = _head_masks()
        for p in range(SB_WIDTH // LANES):
            cols = slice(p * LANES, (p + 1) * LANES)
            o, dao, g = o_ref[:, cols], dao_all[:, cols], ga_ref[:, cols]
            sa, sb = _head_sum(o * o, first)
            r = jnp.where(first, lax.rsqrt(sa * inv_dh + EPS), lax.rsqrt(sb * inv_dh + EPS))
            w = dao * g
            wa, wb = _head_sum(w * o, first)
            do_ref[:, cols] = (r * (w - o * (r * r) * (jnp.where(first, wa, wb) * inv_dh))).astype(BF16)
            dga_ref[:, cols] += jnp.sum(dao * o * r, axis=0, keepdims=True)

    row = lambda n: pl.BlockSpec((tm, n), lambda i: (i, 0))
    return pl.pallas_call(
        body, name="mix_bwd", grid=(S // tm,),
        out_shape=[jax.ShapeDtypeStruct((S, D_MODEL), F32), jax.ShapeDtypeStruct((S, D_MODEL), BF16),
                   jax.ShapeDtypeStruct((S, CONV_CH), F32), jax.ShapeDtypeStruct((S, SB_WIDTH), BF16),
                   jax.ShapeDtypeStruct((1, D_MODEL), F32), jax.ShapeDtypeStruct((1, D_MODEL), F32),
                   jax.ShapeDtypeStruct((1, SB_WIDTH), F32)],
        in_specs=[row(D_MODEL)] * 4 + [_const((1, D_MODEL)), _const((1, D_MODEL)), _resident(w_out.shape), row(SB_WIDTH),
                  _const((1, SB_WIDTH))],
        out_specs=[row(D_MODEL), row(D_MODEL), row(CONV_CH), row(SB_WIDTH), _const((1, D_MODEL)), _const((1, D_MODEL)),
                   _const((1, SB_WIDTH))],
        compiler_params=_cparams(dimension_semantics=("arbitrary",)),
    )(dfin, h1, y, dh2, g_pre_ffn, g_post_mix, w_out, o, g_attn)


def _attn_bwd(qkv, do, cl):
    S = qkv.shape[0]
    Q = min(ATTN_BLOCK, S)
    nq = S // Q
    ntiles = nq * (nq + 1) // 2
    unroll = ATTN_BWD_UNROLL
    assert unroll % 2 == 0 and ntiles >= 3 + unroll
    npair = SB_WIDTH // LANES
    tiles = [(i, j) for i in range(nq) for j in range(i + 1)]

    def body(q_ref, k_ref, v_ref, do_ref, cl_ref, dq_ref, dk_ref, dv_ref,
             lb_buf, be_buf, g_buf, l_buf, a_buf, gb_buf, dz_buf, pg_buf, mask_buf, qh_buf, kh_buf, doh_buf):
        first = _head_masks()
        lane = lax.broadcasted_iota(jnp.int32, (1, LANES), 1)
        row = lax.broadcasted_iota(jnp.int32, (Q, Q), 0)
        col = lax.broadcasted_iota(jnp.int32, (Q, Q), 1)
        tri = jnp.where(row > col, -1.0, 0.0).astype(BF16)
        tpi = (row <= col).astype(BF16)
        heads = range(2)
        strips = [slice(r0, r0 + ATTN_STRIP) for r0 in range(0, Q, ATTN_STRIP)]
        rows = lambda j: pl.ds(pl.multiple_of(j * Q, Q), Q)
        wide = lambda t: jnp.tile(t, (1, Q // LANES))
        as_int = lambda t: int(t) if isinstance(t, (bool, int)) else t.astype(jnp.int32)

        keep = col < row
        mask_buf[0, 0] = jnp.full((Q, Q), LOG2E, F32)
        mask_buf[0, 1] = jnp.zeros((Q, Q), F32)
        mask_buf[1, 0] = jnp.where(keep, LOG2E, 0.0)
        mask_buf[1, 1] = jnp.where(keep, 0.0, MASKED)
        dq_ref[...] = jnp.zeros_like(dq_ref)
        dk_ref[...] = jnp.zeros_like(dk_ref)
        dv_ref[...] = jnp.zeros_like(dv_ref)
        _split_heads_into(qh_buf, q_ref, first, nq, Q)
        _split_heads_into(kh_buf, k_ref, first, nq, Q)
        _split_heads_into(doh_buf, do_ref, first, nq, Q)

        def scores(t):
            i, j = t
            kb = k_ref[rows(j), :]
            return tuple(_dot_nt(qh_buf[h, rows(i), :], kb) for h in heads)

        def logs(t, slot, z):
            i, j = t
            diag = as_int(i == j)
            for h in heads:
                for r in strips:
                    z2 = jnp.minimum(z[h][r] * LOG2E, Z2_MAX)
                    nl = jnp.log(1.0 + jnp.exp2(z2)) * mask_buf[diag, 0, r, :]
                    l_buf[slot, h, r, :] = nl.astype(BF16)
                    lb_buf[slot, h, r, :] = (z2 - nl) + mask_buf[diag, 1, r, :]

        def sums(t, slot):
            i, j = t
            vb = v_ref[rows(j), :]
            return (tuple(_dot(l_buf[slot, h], tri) for h in heads),
                    tuple(_dot_nt(doh_buf[h, rows(i), :], vb) for h in heads))

        def weights(t, slot, sm, da):
            i, j = t
            for h in heads:
                c = jnp.sum(jnp.where(lane == j, cl_ref[rows(i), h * LANES:(h + 1) * LANES], 0.0), axis=-1, keepdims=True)
                c = jnp.broadcast_to(c, (Q, LANES))
                for r in strips:
                    lb = lb_buf[slot, h, r, :]
                    a = jnp.exp2(lb + sm[h][r] + wide(c[r]))
                    g = da[h][r] * a
                    a_buf[slot, h, r, :] = a.astype(BF16)
                    be_buf[slot, h, r, :] = jnp.exp2(lb)
                    g_buf[slot, h, r, :] = g
                    gb_buf[slot, h, r, :] = g.astype(BF16)

        def prefix(t, slot):
            i, j = t
            dv_ref[rows(j), :] += (_dot_tn(a_buf[slot, 0], doh_buf[0, rows(i), :])
                                   + _dot_tn(a_buf[slot, 1], doh_buf[1, rows(i), :]))
            return tuple(_dot(gb_buf[slot, h], tpi) for h in heads)

        def dscores(t, slot, pm):
            i, j = t
            for h in heads:
                pg = pg_buf[h] * jnp.where(j == 0, 0.0, 1.0)
                for r in strips:
                    dz = g_buf[slot, h, r, :] - be_buf[slot, h, r, :] * (pm[h][r] + wide(pg[r]))
                    dz_buf[slot, h, r, :] = dz.astype(BF16)
                pg_buf[h] = pg + jnp.broadcast_to(pm[h][:, Q - 1:Q], (Q, LANES))

        def grads(t, slot):
            i, j = t
            dq_ref[rows(i), :] += (_dot(dz_buf[slot, 0], kh_buf[0, rows(j), :])
                                   + _dot(dz_buf[slot, 1], kh_buf[1, rows(j), :]))
            dk_ref[rows(j), :] += (_dot_tn(dz_buf[slot, 0], qh_buf[0, rows(i), :])
                                   + _dot_tn(dz_buf[slot, 1], qh_buf[1, rows(i), :]))

        def iteration(t, p):
            ta, tb, tc, td = t
            if ta is not None:
                z = scores(ta)
            if tb is not None:
                sm, da = sums(tb, 1 - p)
            if tc is not None:
                pm = prefix(tc, p)
            if td is not None:
                grads(td, 1 - p)
            if tb is not None:
                weights(tb, 1 - p, sm, da)
            if tc is not None:
                dscores(tc, p, pm)
            if ta is not None:
                logs(ta, p, z)

        def window(n):
            return tuple(tiles[n - k] if 0 <= n - k < ntiles else None for k in range(4))

        def following(t):
            i, j = t
            last = j == i
            return jnp.where(last, i + 1, i), jnp.where(last, 0, j + 1)

        peeled = 3 + (ntiles - 3) % unroll

        def unrolled_iterations(_, t):
            for n in range(peeled, peeled + unroll):
                iteration(t, n % 2)
                t = (following(t[0]),) + t[:3]
            return t

        pg_buf[...] = jnp.zeros_like(pg_buf)
        for n in range(peeled):
            iteration(window(n), n % 2)
        first_window = tuple((jnp.int32(i), jnp.int32(j)) for i, j in window(peeled))
        lax.fori_loop(0, (ntiles - peeled) // unroll, unrolled_iterations, first_window)
        for n in range(ntiles, ntiles + 3):
            iteration(window(n), n % 2)
        dq_ref[...] = dq_ref[...] * (1.0 / math.sqrt(SB_HEAD_DIM))

    col_block = lambda off: pl.BlockSpec((S, LANES), lambda p: (0, off + p), pipeline_mode=pl.Buffered(1))
    return pl.pallas_call(
        body, name="attn_bwd", grid=(npair,),
        out_shape=[jax.ShapeDtypeStruct((S, SB_WIDTH), F32)] * 3,
        in_specs=[col_block(0), col_block(npair), col_block(2 * npair), col_block(0),
                  pl.BlockSpec((S, 2 * LANES), lambda p: (0, p), pipeline_mode=pl.Buffered(1))],
        out_specs=[pl.BlockSpec((S, LANES), lambda p: (0, p), pipeline_mode=pl.Buffered(1))] * 3,
        scratch_shapes=[pltpu.VMEM((2, 2, Q, Q), F32)] * 3 + [pltpu.VMEM((2, 2, Q, Q), BF16)] * 4
        + [pltpu.VMEM((2, Q, LANES), F32), pltpu.VMEM((2, 2, Q, Q), F32)] + [pltpu.VMEM((2, S, LANES), BF16)] * 3,
        compiler_params=_cparams(dimension_semantics=("arbitrary",)),
    )(qkv, qkv, qkv, do, cl)


def _conv_bwd(u_conv, y_conv, dco, conv_w, ln_g, ln_b):
    S = u_conv.shape[0]
    tc = min(TOKEN_TILE, S)
    nt = S // tc
    per = tc // CONV_HALO
    groups = CONV_CHUNK // 8

    def body(u_ref, halo_ref, y_ref, dco_ref, cw_ref, lg_ref, lb_ref, du_ref, dcw_ref, dsm_ref, glu_ext, dyc_ext, sg_buf,
             dcw_acc, dsm_acc, glu_sh, dyc_sh):
        i = pl.program_id(0)
        ti = nt - 1 - i

        @pl.when(i == 0)
        def _():
            dyc_ext[tc:, :] = jnp.zeros((CONV_HALO, CONV_CH), F32)
            dcw_acc[...] = jnp.zeros_like(dcw_acc)
            dsm_acc[...] = jnp.zeros_like(dsm_acc)

        @pl.when(i > 0)
        def _():
            dyc_ext[tc:, :] = dyc_ext[0:CONV_HALO, :]

        glu_ext[0:CONV_HALO, :] = jnp.where(ti > 0, _glu(halo_ref[...])[2], 0.0)
        _, sg, glu = _glu(u_ref[...])
        glu_ext[CONV_HALO:, :] = glu
        sg_buf[...] = sg
        _shift_copies(glu_ext, glu_sh)

        dcb = jnp.zeros((8, CONV_CH), F32)
        dlg = jnp.zeros((8, CONV_CH), F32)
        dlb = jnp.zeros((8, CONV_CH), F32)
        fold = lambda t: jnp.sum(t.reshape(groups, 8, CONV_CH), axis=0)
        for r0 in range(0, tc, CONV_CHUNK):
            y = y_ref[r0:r0 + CONV_CHUNK, :]
            mu = jnp.mean(y, axis=-1, keepdims=True)
            yc = y - mu
            rstd = lax.rsqrt(jnp.mean(yc * yc, axis=-1, keepdims=True) + EPS)
            yn = yc * rstd
            yl = yn * lg_ref[...] + lb_ref[...]
            s = jax.nn.sigmoid(yl)
            dyl = dco_ref[r0:r0 + CONV_CHUNK, :] * (s * (1.0 + yl * (1.0 - s)))
            dlg = dlg + fold(dyl * yn)
            dlb = dlb + fold(dyl)
            wv = dyl * lg_ref[...]
            dyc = rstd * (wv - jnp.mean(wv, axis=-1, keepdims=True) - yn * jnp.mean(wv * yn, axis=-1, keepdims=True))
            dcb = dcb + fold(dyc)
            dyc_ext[r0:r0 + CONV_CHUNK, :] = dyc
        dsm_acc[0:8, :] += dcb
        dsm_acc[8:16, :] += dlg
        dsm_acc[16:24, :] += dlb
        _shift_copies(dyc_ext, dyc_sh)

        for r0 in range(0, tc, CONV_CHUNK):
            dyc = dyc_ext[r0:r0 + CONV_CHUNK, :]
            dglu = jnp.zeros((CONV_CHUNK, CONV_CH), F32)
            base = r0 + CONV_HALO - (CONV_WIDTH - 1)
            for w in range(CONV_WIDTH):
                back = r0 + (CONV_WIDTH - 1) - w
                dglu = dglu + cw_ref[w:w + 1, :] * _window(dyc_ext, dyc_sh, back, CONV_CHUNK)
                dcw_acc[8 * w:8 * w + 8, :] += fold(dyc * _window(glu_ext, glu_sh, base + w, CONV_CHUNK))
            sg = sg_buf[r0:r0 + CONV_CHUNK, :]
            v = u_ref[r0:r0 + CONV_CHUNK, :CONV_CH]
            du_ref[r0:r0 + CONV_CHUNK, :CONV_CH] = (dglu * sg).astype(BF16)
            du_ref[r0:r0 + CONV_CHUNK, CONV_CH:] = (dglu * v * sg * (1.0 - sg)).astype(BF16)

        @pl.when(i == nt - 1)
        def _():
            for w in range(CONV_WIDTH):
                dcw_ref[w:w + 1, :] = jnp.sum(dcw_acc[8 * w:8 * w + 8, :], axis=0, keepdims=True)
            dcw_ref[CONV_WIDTH:, :] = jnp.zeros((CONV_HALO - CONV_WIDTH, CONV_CH), F32)
            for k in range(3):
                dsm_ref[k:k + 1, :] = jnp.sum(dsm_acc[8 * k:8 * k + 8, :], axis=0, keepdims=True)
            dsm_ref[3:, :] = jnp.zeros((5, CONV_CH), F32)

    return pl.pallas_call(
        body, name="conv_bwd", grid=(nt,),
        out_shape=[jax.ShapeDtypeStruct((S, 2 * CONV_CH), BF16), jax.ShapeDtypeStruct((CONV_HALO, CONV_CH), F32),
                   jax.ShapeDtypeStruct((8, CONV_CH), F32)],
        in_specs=[pl.BlockSpec((tc, 2 * CONV_CH), lambda i: (nt - 1 - i, 0)),
                  pl.BlockSpec((CONV_HALO, 2 * CONV_CH), lambda i: (jnp.maximum((nt - 1 - i) * per - 1, 0), 0)),
                  pl.BlockSpec((tc, CONV_CH), lambda i: (nt - 1 - i, 0)), pl.BlockSpec((tc, CONV_CH), lambda i: (nt - 1 - i, 0)),
                  _const((CONV_HALO, CONV_CH)), _const((1, CONV_CH)), _const((1, CONV_CH))],
        out_specs=[pl.BlockSpec((tc, 2 * CONV_CH), lambda i: (nt - 1 - i, 0)), _const((CONV_HALO, CONV_CH)),
                   _const((8, CONV_CH))],
        scratch_shapes=[pltpu.VMEM((tc + CONV_HALO, CONV_CH), F32), pltpu.VMEM((tc + CONV_HALO, CONV_CH), F32),
                        pltpu.VMEM((tc, CONV_CH), F32), pltpu.VMEM((8 * CONV_HALO, CONV_CH), F32),
                        pltpu.VMEM((24, CONV_CH), F32)]
        + [pltpu.VMEM((SUBLANES - 1, tc + CONV_HALO - SUBLANES, CONV_CH), F32)] * 2,
        compiler_params=_cparams(dimension_semantics=("arbitrary",)),
    )(u_conv, u_conv, y_conv, dco, conv_w, ln_g, ln_b)


def _in_proj_bwd(du_conv, dq, dk, dv, w_in, x, g, dh1):
    S = x.shape[0]
    tm = min(TOKEN_TILE, S)
    nconv = 2 * CONV_CH

    def body(duc_ref, dq_ref, dk_ref, dv_ref, w_ref, x_ref, g_ref, dh1_ref, dx_ref, dg_ref):
        @pl.when(pl.program_id(0) == 0)
        def _():
            dg_ref[...] = jnp.zeros_like(dg_ref)

        da = _dot_nt(duc_ref[...], w_ref[:, :nconv])
        for n, ref in enumerate((dq_ref, dk_ref, dv_ref)):
            c0 = nconv + n * SB_WIDTH
            da = da + _dot_nt(ref[...].astype(BF16), w_ref[:, c0:c0 + SB_WIDTH])
        xf = x_ref[...]
        r = _rms_r(xf)
        dx_ref[...] = dh1_ref[...] + _rms_bwd(xf, r, g_ref[...], da)
        dg_ref[...] += jnp.sum(da * xf * r, axis=0, keepdims=True)

    row = lambda n: pl.BlockSpec((tm, n), lambda i: (i, 0))
    return pl.pallas_call(
        body, name="in_proj_bwd", grid=(S // tm,),
        out_shape=[jax.ShapeDtypeStruct((S, D_MODEL), F32), jax.ShapeDtypeStruct((1, D_MODEL), F32)],
        in_specs=[row(nconv), row(SB_WIDTH), row(SB_WIDTH), row(SB_WIDTH), _resident(w_in.shape), row(D_MODEL),
                  _const((1, D_MODEL)), row(D_MODEL)],
        out_specs=[row(D_MODEL), _const((1, D_MODEL))],
        compiler_params=_cparams(dimension_semantics=("arbitrary",)),
    )(du_conv, dq, dk, dv, w_in, x, g, dh1)


def _layer_grads(xs, target, g_pre_mix, w_in_f, conv_w_f, conv_b, conv_ln_g, conv_ln_b, attn_g, g_post_mix, g_pre_ffn,
                 g_post_ffn, late_weights, send_grads):
    a, u_conv, qkv = _in_proj(xs, g_pre_mix, w_in_f)
    conv_out, y_conv = _conv_fwd(u_conv, conv_w_f, conv_b, conv_ln_g, conv_ln_b)
    o, attn_out, cl = _attn_fwd(qkv, attn_g)
    w_out_f, w_gate_f, w_up_f, w_down_f = late_weights(attn_out)
    y, h1, f_in = _out_proj(conv_out, attn_out, w_out_f, xs, g_post_mix, g_pre_ffn)
    gt, up, df, dh2, loss_part, d_g_post_ffn = _ffn_fwd_loss(f_in, w_gate_f, w_up_f, w_down_f, h1, target, g_post_ffn)

    dgt, dup, act, dfin = _ffn_bwd(df, gt, up, w_gate_f, w_up_f, w_down_f)
    d_w_down = _matmul_tn("grad_w_down", act, df, D_MODEL)
    d_w_gate = _matmul_tn("grad_w_gate", f_in, dgt, D_FF)
    d_w_up = _matmul_tn("grad_w_up", f_in, dup, D_FF)
    sent = send_grads("ffn", (d_w_gate, d_w_up, d_w_down))
    dh1, dy, dco, do, d_g_pre_ffn, d_g_post_mix, d_attn_g = _mix_bwd(dfin, h1, y, dh2, g_pre_ffn + sent, g_post_mix,
                                                                     w_out_f, o, attn_g)
    d_w_out = jnp.concatenate([_matmul_tn("grad_w_out_conv", conv_out, dy, D_MODEL),
                               _matmul_tn("grad_w_out_attn", attn_out, dy, D_MODEL)], axis=0)
    sent = send_grads("w_out", (d_w_out,))
    dq, dk, dv = _attn_bwd(qkv, do, cl)
    du_conv, d_conv_w, d_conv_small = _conv_bwd(u_conv, y_conv, dco, conv_w_f, conv_ln_g + sent, conv_ln_b)
    d_w_in = jnp.concatenate([_matmul_tn("grad_w_in_conv", a, du_conv, 2 * CONV_CH),
                              _matmul_tn("grad_w_in_q", a, dq, SB_WIDTH), _matmul_tn("grad_w_in_k", a, dk, SB_WIDTH),
                              _matmul_tn("grad_w_in_v", a, dv, SB_WIDTH)], axis=1)
    sent = send_grads("w_in", (d_w_in,))
    grad_x, d_g_pre_mix = _in_proj_bwd(du_conv, dq, dk, dv, w_in_f, xs, g_pre_mix + sent, dh1)
    return (loss_part, grad_x, d_conv_w, d_conv_small, d_attn_g, d_g_pre_mix, d_g_post_mix, d_g_pre_ffn, d_g_post_ffn)


def _cols_to_blocks(w):
    K, N = w.shape
    return jnp.transpose(w.reshape(K, N_DEV, N // N_DEV), (1, 0, 2))


def _blocks_to_cols(blocks):
    n_dev, K, n = blocks.shape
    return jnp.transpose(blocks, (1, 0, 2)).reshape(K, n_dev * n)


def kernel(x, g_pre_mix, w_in, conv_w, conv_b, conv_ln_g, conv_ln_b, attn_norm_g, w_out, g_post_mix, g_pre_ffn, w_gate, w_up, w_down, g_post_ffn, loss_target, m_g_pre_mix, m_w_in, m_conv_w, m_conv_b, m_conv_ln_g, m_conv_ln_b, m_attn_norm_g, m_w_out, m_g_post_mix, m_g_pre_ffn, m_w_gate, m_w_up, m_w_down, m_g_post_ffn, v_g_pre_mix, v_w_in, v_conv_w, v_conv_b, v_conv_ln_g, v_conv_ln_b, v_attn_norm_g, v_w_out, v_g_post_mix, v_g_pre_ffn, v_w_gate, v_w_up, v_w_down, v_g_post_ffn):
    xs = x[0]
    target = loss_target[0]
    me = 4 * lax.axis_index("x") + 2 * lax.axis_index("y") + lax.axis_index("c")
    cw_shard = conv_w.reshape(CONV_WIDTH, CONV_CH // N_DEV)
    attn_g = attn_norm_g.reshape(1, SB_WIDTH)

    gathered = _all_gather([w_in[0].astype(BF16), cw_shard])
    w_in_f = _blocks_to_cols(gathered[0])
    conv_w_f = jnp.pad(_blocks_to_cols(gathered[1]), ((0, CONV_HALO - CONV_WIDTH), (0, 0)))
    gathered_zero = gathered[2][0:1, 0:1].astype(BF16)
    late = [w_out[0].astype(BF16) + gathered_zero, w_gate[0].astype(BF16), w_up[0].astype(BF16), w_down[0].astype(BF16)]
    late_started = _exchange_start("all_gather_late_start", late, scatter=False)

    def late_weights(after):
        lands = _exchange_wait("all_gather_late_wait", late_started, False, after)
        wo, wg, wu, wd = [lax.dynamic_update_index_in_dim(land, own, me, 0) for land, own in zip(lands, late)]
        return wo.reshape(D_MODEL, D_MODEL), _blocks_to_cols(wg), _blocks_to_cols(wu), wd.reshape(D_FF, D_MODEL)

    started = {}

    def send_grads(name, grads):
        blocks = [g.reshape(N_DEV, g.shape[0] // N_DEV, g.shape[1]) if g.shape[1] == D_MODEL else _cols_to_blocks(g)
                  for g in grads]
        payload = BF16 if name == "w_in" else F32
        sent = _exchange_start("reduce_scatter_" + name + "_start", [b.astype(payload) for b in blocks], scatter=True)
        started[name] = (sent, blocks)
        return sent[-1][0:1, 0:1]

    (loss_part, grad_x, d_conv_w, d_conv_small, d_attn_g, d_g_pre_mix, d_g_post_mix, d_g_pre_ffn,
     d_g_post_ffn) = _layer_grads(
        xs, target, g_pre_mix + late_started[-1][0:1, 0:1], w_in_f, conv_w_f, conv_b, conv_ln_g, conv_ln_b, attn_g,
        g_post_mix, g_pre_ffn, g_post_ffn, late_weights, send_grads)

    def reduced(name, after, shards):
        st, blocks = started[name]
        lands = _exchange_wait("reduce_scatter_" + name + "_wait", st, True, after)
        return [_sum_adamw("adamw_" + wn, land, lax.dynamic_index_in_dim(blk, me, 0, keepdims=False), w[0], m[0], v[0])
                for land, blk, (wn, w, m, v) in zip(lands, blocks, shards)]

    two = lambda t: t.reshape(2, CONV_CH)
    small_g = jnp.concatenate([
        d_conv_w,
        d_conv_small[0:3],
        d_attn_g,
        two(d_g_pre_mix), two(d_g_post_mix), two(d_g_pre_ffn), two(d_g_post_ffn),
        jnp.broadcast_to(loss_part[0:1, 0:1], (1, CONV_CH)),
        jnp.zeros((3, CONV_CH), F32)], axis=0)
    small_g = _all_reduce_small(small_g)
    loss = small_g[44, 0]
    g_conv_w = lax.dynamic_slice(small_g, (0, me * (CONV_CH // N_DEV)), (CONV_WIDTH, CONV_CH // N_DEV))
    pack = lambda cb, lg, lb, ag, g1, g2, g3, g4: jnp.concatenate(
        [cb, lg, lb, ag.reshape(1, SB_WIDTH), two(g1), two(g2), two(g3), two(g4), jnp.zeros((4, CONV_CH), F32)], axis=0)
    sm_g = small_g[CONV_HALO:]
    sm_delta, sm_m, sm_v = _adamw_small(
        "adamw_small",
        pack(conv_b, conv_ln_g, conv_ln_b, attn_norm_g, g_pre_mix, g_post_mix, g_pre_ffn, g_post_ffn), sm_g,
        pack(m_conv_b, m_conv_ln_g, m_conv_ln_b, m_attn_norm_g, m_g_pre_mix, m_g_post_mix, m_g_pre_ffn, m_g_post_ffn),
        pack(v_conv_b, v_conv_ln_g, v_conv_ln_b, v_attn_norm_g, v_g_pre_mix, v_g_post_mix, v_g_pre_ffn, v_g_post_ffn))
    cw_delta, cw_m, cw_v = _adamw_small("adamw_conv_w", cw_shard, g_conv_w,
                                        m_conv_w.reshape(cw_shard.shape), v_conv_w.reshape(cw_shard.shape))

    ffn = reduced("ffn", grad_x, [("w_gate", w_gate, m_w_gate, v_w_gate), ("w_up", w_up, m_w_up, v_w_up),
                                  ("w_down", w_down, m_w_down, v_w_down)])
    big = {"w_gate": ffn[0], "w_up": ffn[1], "w_down": ffn[2],
           "w_out": reduced("w_out", ffn[2][0], [("w_out", w_out, m_w_out, v_w_out)])[0]}
    big["w_in"] = reduced("w_in", big["w_out"][0], [("w_in", w_in, m_w_in, v_w_in)])[0]

    def unpack(t):
        return {"conv_b": t[0:1], "conv_ln_g": t[1:2], "conv_ln_b": t[2:3], "attn_norm_g": t[3:4].reshape(1, SB_HEADS, SB_HEAD_DIM),
                "g_pre_mix": t[4:6].reshape(1, D_MODEL), "g_post_mix": t[6:8].reshape(1, D_MODEL),
                "g_pre_ffn": t[8:10].reshape(1, D_MODEL), "g_post_ffn": t[10:12].reshape(1, D_MODEL)}

    names = ["g_pre_mix", "w_in", "conv_w", "conv_b", "conv_ln_g", "conv_ln_b", "attn_norm_g", "w_out", "g_post_mix",
             "g_pre_ffn", "w_gate", "w_up", "w_down", "g_post_ffn"]
    kinds = []
    for idx, small in enumerate((sm_g, sm_delta, sm_m, sm_v)):
        d = unpack(small)
        d["conv_w"] = (g_conv_w, cw_delta, cw_m, cw_v)[idx].reshape(1, CONV_WIDTH, 1, CONV_CH // N_DEV)
        for n in big:
            d[n] = big[n][idx][None]
        kinds.append([d[n] for n in names])

    return (loss, grad_x[None], *kinds[0], *kinds[1], *kinds[2], *kinds[3])
```

```python
import itertools
import math

import jax
import jax.numpy as jnp
from jax import lax
from jax.experimental import pallas as pl
from jax.experimental.pallas import tpu as pltpu

F32 = jnp.float32
BF16 = jnp.bfloat16
MESH = pl.DeviceIdType.MESH

N_DEV = 8
D_MODEL = 1024
CONV_CH = 512
CONV_WIDTH = 31
SB_HEADS = 8
SB_HEAD_DIM = 64
SB_WIDTH = SB_HEADS * SB_HEAD_DIM
D_FF = 2816
EPS = 1e-6
LOG2E = 1.4426950408889634
Z2_MAX = 100.0
MASKED = -1e30
ADAM_LR = 0.001
ADAM_B1 = 0.9
ADAM_B2 = 0.999
ADAM_EPS = 1e-08
ADAM_WD = 0.01
ADAM_STEP = 10

SUBLANES = 8
LANES = 128
VMEM_LIMIT = 56 * 1024 * 1024
TOKEN_TILE = 512
GRAD_TILE = 1024
FFN_TILE = 256
ATTN_FWD_UNROLL = 8
ATTN_BWD_UNROLL = 4
ATTN_STRIP = 32
ATTN_BLOCK = 256
CONV_HALO = 32
CONV_CHUNK = 64


def _cparams(**kw):
    return pltpu.CompilerParams(vmem_limit_bytes=VMEM_LIMIT, **kw)


def _resident(shape):
    return pl.BlockSpec(shape, lambda *_: (0,) * len(shape), pipeline_mode=pl.Buffered(1))


def _const(shape):
    return pl.BlockSpec(shape, lambda *_: (0,) * len(shape))


def _rms_r(xf):
    return lax.rsqrt(jnp.mean(xf * xf, axis=-1, keepdims=True) + EPS)


def _rms_bwd(xf, r, g, dout):
    w = dout * g
    return r * (w - xf * (r * r) * jnp.mean(w * xf, axis=-1, keepdims=True))


def _dot(a, b):
    return jnp.dot(a, b, preferred_element_type=F32)


def _dot_nt(a, b):
    return lax.dot_general(a, b, (((1,), (1,)), ((), ())), preferred_element_type=F32)


def _dot_tn(a, b):
    return lax.dot_general(a, b, (((0,), (0,)), ((), ())), preferred_element_type=F32)


def _peer(x, y, c, k):
    px = 1 - x if (k >> 2) & 1 else x
    py = 1 - y if (k >> 1) & 1 else y
    pc = 1 - c if k & 1 else c
    return (px, py, pc), 4 * px + 2 * py + pc


def _all_gather(shards):
    n = len(shards)

    def body(*refs):
        ins, outs, done = refs[:n], refs[n:2 * n], refs[2 * n]
        send_sems, recv_sems, local_sems = refs[2 * n + 1:]
        x, y, c = lax.axis_index("x"), lax.axis_index("y"), lax.axis_index("c")
        me, sibling = (x, y, c), (x, y, 1 - c)
        chips = [(1 - x, y), (x, 1 - y), (1 - x, 1 - y)]
        number = lambda d: 4 * d[0] + 2 * d[1] + d[2]

        def copy(a, k, block, to, src=None):
            rows = outs[a].at[number(block)]
            return pltpu.make_async_remote_copy(
                src_ref=rows if src is None else src, dst_ref=rows, send_sem=send_sems.at[a * (N_DEV - 1) + k],
                recv_sem=recv_sems.at[a * (N_DEV - 1) + k], device_id=to, device_id_type=MESH)

        copies = [pltpu.make_async_copy(ins[a], outs[a].at[number(me)], local_sems.at[a]) for a in range(n)]
        for mine in copies:
            mine.start()
        sent = [copy(a, 0, me, sibling, src=ins[a]) for a in range(n)]
        sent += [copy(a, 1 + j, me, (*chip, c), src=ins[a]) for j, chip in enumerate(chips) for a in range(n)]
        for cp in sent:
            cp.start()
        for j, chip in enumerate(chips):
            for a in range(n):
                copy(a, 1 + j, (*chip, c), me).wait_recv()
                passed = copy(a, 4 + j, (*chip, c), sibling)
                passed.start()
                sent.append(passed)
        for a in range(n):
            copy(a, 0, sibling, me).wait_recv()
            for j, chip in enumerate(chips):
                copy(a, 4 + j, (*chip, 1 - c), me).wait_recv()
        for cp in sent:
            cp.wait_send()
        for mine in copies:
            mine.wait()
        done[...] = jnp.zeros_like(done)

    any_spec = pl.BlockSpec(memory_space=pl.ANY)
    return pl.pallas_call(
        body, name="all_gather_weights",
        out_shape=[jax.ShapeDtypeStruct((N_DEV,) + s.shape, s.dtype) for s in shards] + [jax.ShapeDtypeStruct((8, LANES), F32)],
        in_specs=[any_spec] * n, out_specs=[any_spec] * n + [pl.BlockSpec(memory_space=pltpu.VMEM)],
        scratch_shapes=[pltpu.SemaphoreType.DMA((n * (N_DEV - 1),)), pltpu.SemaphoreType.DMA((n * (N_DEV - 1),)),
                        pltpu.SemaphoreType.DMA((n,))],
        compiler_params=pltpu.CompilerParams(has_side_effects=True),
    )(*shards)


def _adamw(w, g, m, v):
    m = ADAM_B1 * m + (1.0 - ADAM_B1) * g
    v = ADAM_B2 * v + (1.0 - ADAM_B2) * (g * g)
    m_hat = m / (1.0 - ADAM_B1 ** ADAM_STEP)
    v_hat = v / (1.0 - ADAM_B2 ** ADAM_STEP)
    delta = -ADAM_LR * (m_hat / (jnp.sqrt(v_hat) + ADAM_EPS) + ADAM_WD * w)
    return delta, m, v


def _exchange_and_sum(src_block, recv_ref, send_sems, recv_sems, local_sem):
    x, y, c = lax.axis_index("x"), lax.axis_index("y"), lax.axis_index("c")
    me = 4 * x + 2 * y + c
    mine = pltpu.make_async_copy(src_block(me), recv_ref.at[me], local_sem)
    mine.start()
    for k in range(1, N_DEV):
        peer, peer_block = _peer(x, y, c, k)
        pltpu.make_async_remote_copy(
            src_ref=src_block(peer_block), dst_ref=recv_ref.at[me], send_sem=send_sems.at[k - 1],
            recv_sem=recv_sems.at[k - 1], device_id=peer, device_id_type=MESH).start()
    for k in range(1, N_DEV):
        peer, peer_block = _peer(x, y, c, k)
        arrived = pltpu.make_async_remote_copy(
            src_ref=src_block(peer_block), dst_ref=recv_ref.at[peer_block], send_sem=send_sems.at[k - 1],
            recv_sem=recv_sems.at[k - 1], device_id=peer, device_id_type=MESH)
        arrived.wait_send()
        arrived.wait_recv()
    mine.wait()


HBM_SPEC = pl.BlockSpec(memory_space=pltpu.HBM)
SEM_SPEC = pl.BlockSpec(memory_space=pltpu.SEMAPHORE)
DATAFLOW = pltpu.SideEffectType.DATAFLOW_SIDE_EFFECTING


def _exchange_copies(srcs, lands, send_sems, recv_sems, scatter, wait):
    x, y, c = lax.axis_index("x"), lax.axis_index("y"), lax.axis_index("c")
    me = 4 * x + 2 * y + c
    for k in range(1, N_DEV):
        peer, peer_block = _peer(x, y, c, k)
        for a in range(len(srcs)):
            s = a * (N_DEV - 1) + k - 1
            src = srcs[a].at[peer_block] if scatter else srcs[a]
            copy = pltpu.make_async_remote_copy(
                src_ref=src, dst_ref=lands[a].at[peer_block if wait else me], send_sem=send_sems.at[s],
                recv_sem=recv_sems.at[s], device_id=peer, device_id_type=MESH)
            if wait:
                copy.wait_send()
                copy.wait_recv()
            else:
                copy.start()


def _exchange_start(name, arrays, scatter):
    n = len(arrays)
    land_shapes = [a.shape if scatter else (N_DEV,) + a.shape for a in arrays]

    def body(*refs):
        _exchange_copies(refs[:n], refs[n:2 * n], refs[2 * n], refs[2 * n + 1], scatter, wait=False)
        refs[-1][...] = jnp.zeros_like(refs[-1])

    sems = pltpu.SemaphoreType.DMA((n * (N_DEV - 1),))
    hbm = lambda t: pltpu.with_memory_space_constraint(t, pltpu.HBM)
    return pl.pallas_call(
        body, name=name,
        out_shape=(sems, sems, *[pltpu.HBM(a.shape, a.dtype) for a in arrays],
                   *[pltpu.HBM(ls, a.dtype) for ls, a in zip(land_shapes, arrays)], jax.ShapeDtypeStruct((8, LANES), F32)),
        in_specs=[HBM_SPEC] * (2 * n),
        out_specs=(SEM_SPEC, SEM_SPEC, *[HBM_SPEC] * (2 * n), pl.BlockSpec(memory_space=pltpu.VMEM)),
        input_output_aliases={a: 2 + a for a in range(2 * n)},
        compiler_params=pltpu.CompilerParams(has_side_effects=DATAFLOW),
    )(*[hbm(a) for a in arrays], *[hbm(lax.empty(ls, a.dtype)) for ls, a in zip(land_shapes, arrays)])


def _exchange_wait(name, started, scatter, after):
    n = (len(started) - 3) // 2
    send_sems, recv_sems = started[0], started[1]
    arrays, lands = started[2:2 + n], started[2 + n:2 + 2 * n]

    def body(*refs):
        _exchange_copies(refs[:n], refs[n:2 * n], refs[2 * n], refs[2 * n + 1], scatter, wait=True)

    return pl.pallas_call(
        body, name=name,
        out_shape=[pltpu.HBM(t.shape, t.dtype) for t in (*arrays, *lands)],
        in_specs=[HBM_SPEC] * (2 * n) + [SEM_SPEC, SEM_SPEC, pl.BlockSpec(memory_space=pl.ANY)],
        out_specs=[HBM_SPEC] * (2 * n),
        input_output_aliases={a: a for a in range(2 * n)},
        compiler_params=pltpu.CompilerParams(has_side_effects=DATAFLOW),
    )(*arrays, *lands, send_sems, recv_sems, after)[n:]


def _sum_adamw(name, land, own, w, m, v):
    _, M, N = land.shape
    rows = math.gcd(M, 128)

    def body(land_ref, own_ref, w_ref, m_ref, v_ref, grad_ref, delta_ref, nm_ref, nv_ref):
        x, y, c = lax.axis_index("x"), lax.axis_index("y"), lax.axis_index("c")
        g = own_ref[...]
        for k in range(1, N_DEV):
            g = g + land_ref[_peer(x, y, c, k)[1]].astype(F32)
        delta, nm, nv = _adamw(w_ref[...], g, m_ref[...], v_ref[...])
        grad_ref[...] = g
        delta_ref[...] = delta
        nm_ref[...] = nm
        nv_ref[...] = nv

    row = pl.BlockSpec((rows, N), lambda i: (i, 0))
    return pl.pallas_call(
        body, name=name, grid=(M // rows,), out_shape=[jax.ShapeDtypeStruct((M, N), F32)] * 4,
        in_specs=[pl.BlockSpec((N_DEV, rows, N), lambda i: (0, i, 0)), row, row, row, row], out_specs=[row] * 4,
        compiler_params=_cparams(),
    )(land, own, w, m, v)


def _all_reduce_small(g):
    R, C = g.shape

    def body(g_ref, out_ref, recv_ref, send_sems, recv_sems, local_sem):
        _exchange_and_sum(lambda b: g_ref, recv_ref, send_sems, recv_sems, local_sem)
        total = recv_ref[0]
        for b in range(1, N_DEV):
            total = total + recv_ref[b]
        out_ref[...] = total

    vmem = pl.BlockSpec(memory_space=pltpu.VMEM)
    return pl.pallas_call(
        body, name="all_reduce_small_grads", out_shape=jax.ShapeDtypeStruct((R, C), F32),
        in_specs=[vmem], out_specs=vmem,
        scratch_shapes=[pltpu.VMEM((N_DEV, R, C), F32), pltpu.SemaphoreType.DMA((N_DEV - 1,)),
                        pltpu.SemaphoreType.DMA((N_DEV - 1,)), pltpu.SemaphoreType.DMA(())],
        compiler_params=_cparams(has_side_effects=True),
    )(g)


def _adamw_small(name, w, g, m, v):
    def body(w_ref, g_ref, m_ref, v_ref, delta_ref, nm_ref, nv_ref):
        delta, nm, nv = _adamw(w_ref[...], g_ref[...], m_ref[...], v_ref[...])
        delta_ref[...] = delta
        nm_ref[...] = nm
        nv_ref[...] = nv

    vmem = pl.BlockSpec(memory_space=pltpu.VMEM)
    return pl.pallas_call(body, name=name, out_shape=[jax.ShapeDtypeStruct(w.shape, F32)] * 3,
                          in_specs=[vmem] * 4, out_specs=[vmem] * 3)(w, g, m, v)


def _in_proj(x, g, w_in):
    S = x.shape[0]
    tm = min(TOKEN_TILE, S)
    nconv = 2 * CONV_CH

    def body(x_ref, g_ref, w_ref, a_ref, uc_ref, qkv_ref):
        xf = x_ref[...]
        a = (xf * _rms_r(xf) * g_ref[...]).astype(BF16)
        a_ref[...] = a
        uc_ref[...] = _dot(a, w_ref[:, :nconv])
        qkv_ref[:, :SB_WIDTH] = (_dot(a, w_ref[:, nconv:nconv + SB_WIDTH]) * (1.0 / math.sqrt(SB_HEAD_DIM))).astype(BF16)
        qkv_ref[:, SB_WIDTH:] = _dot(a, w_ref[:, nconv + SB_WIDTH:]).astype(BF16)

    row = lambda n: pl.BlockSpec((tm, n), lambda i: (i, 0))
    return pl.pallas_call(
        body, name="in_proj", grid=(S // tm,),
        out_shape=[jax.ShapeDtypeStruct((S, D_MODEL), BF16), jax.ShapeDtypeStruct((S, nconv), F32),
                   jax.ShapeDtypeStruct((S, 3 * SB_WIDTH), BF16)],
        in_specs=[row(D_MODEL), _const((1, D_MODEL)), _resident(w_in.shape)],
        out_specs=[row(D_MODEL), row(nconv), row(3 * SB_WIDTH)],
        compiler_params=_cparams(),
    )(x, g, w_in)


def _glu(u):
    val, gate = u[:, :CONV_CH], u[:, CONV_CH:]
    sg = jax.nn.sigmoid(gate)
    return val, sg, val * sg


def _shift_copies(ext, shifted):
    n = shifted.shape[1]
    for r in range(1, SUBLANES):
        shifted[r - 1] = ext[r:r + n, :]


def _window(ext, shifted, start, rows):
    r = start % SUBLANES
    return ext[start:start + rows, :] if r == 0 else shifted[r - 1, start - r:start - r + rows, :]


def _conv_rows(glu_ext, glu_sh, cw_ref, r0, rows):
    base = r0 + CONV_HALO - (CONV_WIDTH - 1)
    acc = cw_ref[0:1, :] * _window(glu_ext, glu_sh, base, rows)
    for w in range(1, CONV_WIDTH):
        acc = acc + cw_ref[w:w + 1, :] * _window(glu_ext, glu_sh, base + w, rows)
    return acc


def _conv_fwd(u_conv, conv_w, conv_b, ln_g, ln_b):
    S = u_conv.shape[0]
    tc = min(TOKEN_TILE, S)

    def body(u_ref, cw_ref, cb_ref, lg_ref, lb_ref, out_ref, y_ref, glu_ext, glu_sh):
        i = pl.program_id(0)

        @pl.when(i == 0)
        def _():
            glu_ext[0:CONV_HALO, :] = jnp.zeros((CONV_HALO, CONV_CH), F32)

        @pl.when(i > 0)
        def _():
            glu_ext[0:CONV_HALO, :] = glu_ext[tc:tc + CONV_HALO, :]

        glu_ext[CONV_HALO:, :] = _glu(u_ref[...])[2]
        _shift_copies(glu_ext, glu_sh)
        for r0 in range(0, tc, CONV_CHUNK):
            y = _conv_rows(glu_ext, glu_sh, cw_ref, r0, CONV_CHUNK) + cb_ref[...]
            y_ref[r0:r0 + CONV_CHUNK, :] = y
            mu = jnp.mean(y, axis=-1, keepdims=True)
            yc = y - mu
            yn = yc * lax.rsqrt(jnp.mean(yc * yc, axis=-1, keepdims=True) + EPS)
            yl = yn * lg_ref[...] + lb_ref[...]
            out_ref[r0:r0 + CONV_CHUNK, :] = (yl * jax.nn.sigmoid(yl)).astype(BF16)

    return pl.pallas_call(
        body, name="conv_fwd", grid=(S // tc,),
        out_shape=[jax.ShapeDtypeStruct((S, CONV_CH), BF16), jax.ShapeDtypeStruct((S, CONV_CH), F32)],
        in_specs=[pl.BlockSpec((tc, 2 * CONV_CH), lambda i: (i, 0)), _const((CONV_HALO, CONV_CH)),
                  _const((1, CONV_CH)), _const((1, CONV_CH)), _const((1, CONV_CH))],
        out_specs=[pl.BlockSpec((tc, CONV_CH), lambda i: (i, 0))] * 2,
        scratch_shapes=[pltpu.VMEM((tc + CONV_HALO, CONV_CH), F32),
                        pltpu.VMEM((SUBLANES - 1, tc + CONV_HALO - SUBLANES, CONV_CH), F32)],
        compiler_params=_cparams(dimension_semantics=("arbitrary",)),
    )(u_conv, conv_w, conv_b, ln_g, ln_b)


def _head_masks():
    lane = lax.broadcasted_iota(jnp.int32, (1, LANES), 1)
    return lane < SB_HEAD_DIM


def _split_heads(t, first):
    z = jnp.zeros_like(t)
    return jnp.where(first, t, z), jnp.where(first, z, t)


def _split_heads_into(halves, src, first, nblocks, rows):
    def one_block(b, carry):
        r = pl.ds(pl.multiple_of(b * rows, rows), rows)
        halves[0, r, :], halves[1, r, :] = _split_heads(src[r, :], first)
        return carry

    lax.fori_loop(0, nblocks, one_block, 0)


def _head_sum(t, first):
    a = jnp.sum(jnp.where(first, t, 0.0), axis=-1, keepdims=True)
    b = jnp.sum(jnp.where(first, 0.0, t), axis=-1, keepdims=True)
    return a, b


def _attn_fwd(qkv, g_attn):
    S = qkv.shape[0]
    Q = min(ATTN_BLOCK, S)
    nq = S // Q
    assert nq <= LANES
    ntiles = nq * (nq + 1) // 2
    unroll = ATTN_FWD_UNROLL
    assert unroll % 2 == 0 and ntiles >= 3 + unroll
    npair = SB_WIDTH // LANES
    tiles = [(i, j) for i in range(nq) for j in range(i, -1, -1)]

    def body(q_ref, k_ref, v_ref, g_ref, o_ref, ao_ref, cl_ref, l_buf, z2_buf, a_buf, c_buf, mask_buf, qh_buf,
             vh_buf):
        first = _head_masks()
        lane = lax.broadcasted_iota(jnp.int32, (1, LANES), 1)
        row = lax.broadcasted_iota(jnp.int32, (Q, Q), 0)
        col = lax.broadcasted_iota(jnp.int32, (Q, Q), 1)
        tri = jnp.where(row >= col, -1.0, 0.0).astype(BF16)
        heads = range(2)
        strips = [slice(r0, r0 + ATTN_STRIP) for r0 in range(0, Q, ATTN_STRIP)]
        rows = lambda j: pl.ds(pl.multiple_of(j * Q, Q), Q)
        wide = lambda t: jnp.tile(t, (1, Q // LANES))
        as_int = lambda t: int(t) if isinstance(t, (bool, int)) else t.astype(jnp.int32)

        keep = col < row
        mask_buf[0, 0] = jnp.full((Q, Q), LOG2E, F32)
        mask_buf[0, 1] = jnp.zeros((Q, Q), F32)
        mask_buf[1, 0] = jnp.where(keep, LOG2E, 0.0)
        mask_buf[1, 1] = jnp.where(keep, 0.0, MASKED)
        o_ref[...] = jnp.zeros_like(o_ref)
        _split_heads_into(qh_buf, q_ref, first, nq, Q)
        _split_heads_into(vh_buf, v_ref, first, nq, Q)

        def scores(t):
            i, j = t
            kb = k_ref[rows(j), :]
            return tuple(_dot_nt(qh_buf[h, rows(i), :], kb) for h in heads)

        def logs(t, slot, z):
            i, j = t
            diag = as_int(i == j)
            for h in heads:
                for r in strips:
                    z2 = jnp.minimum(z[h][r] * LOG2E, Z2_MAX)
                    nl = jnp.log(1.0 + jnp.exp2(z2)) * mask_buf[diag, 0, r, :]
                    l_buf[slot, h, r, :] = nl.astype(BF16)
                    z2_buf[slot, h, r, :] = z2 + mask_buf[diag, 1, r, :]

        def sums(slot):
            return tuple(_dot(l_buf[slot, h], tri) for h in heads)

        def weights(t, slot, sm):
            i, j = t
            running = jnp.where(j == i, 0.0, 1.0)
            for h in heads:
                before = c_buf[h] * running
                for r in strips:
                    a_buf[slot, h, r, :] = jnp.exp2(z2_buf[slot, h, r, :] + sm[h][r] + wide(before[r])).astype(BF16)
                hl = slice(h * LANES, (h + 1) * LANES)
                cl_ref[rows(i), hl] = jnp.where(lane == j, before, cl_ref[rows(i), hl])
                c_buf[h] = before + jnp.broadcast_to(sm[h][:, 0:1], (Q, LANES))

        def values(t, slot):
            i, j = t
            o_ref[rows(i), :] += _dot(a_buf[slot, 0], vh_buf[0, rows(j), :]) + _dot(a_buf[slot, 1], vh_buf[1, rows(j), :])

        def iteration(t, p):
            ta, tc, td = t
            if ta is not None:
                z = scores(ta)
            if tc is not None:
                sm = sums(1 - p)
            if td is not None:
                values(td, p)
            if tc is not None:
                weights(tc, 1 - p, sm)
            if ta is not None:
                logs(ta, p, z)

        def window(n):
            return tuple(tiles[n - k] if 0 <= n - k < ntiles else None for k in range(3))

        def following(t):
            i, j = t
            last = j == 0
            return jnp.where(last, i + 1, i), jnp.where(last, i + 1, j - 1)

        peeled = 3 + (ntiles - 3) % unroll

        def unrolled_iterations(_, t):
            for n in range(peeled, peeled + unroll):
                iteration(t, n % 2)
                t = (following(t[0]),) + t[:2]
            return t

        c_buf[...] = jnp.zeros_like(c_buf)
        for n in range(peeled):
            iteration(window(n), n % 2)
        first_window = tuple((jnp.int32(i), jnp.int32(j)) for i, j in window(peeled))
        lax.fori_loop(0, (ntiles - peeled) // unroll, unrolled_iterations, first_window)
        for n in range(ntiles, ntiles + 2):
            iteration(window(n), n % 2)

        def head_norm(b, carry):
            o = o_ref[rows(b), :]
            sa, sb = _head_sum(o * o, first)
            r = jnp.where(first, lax.rsqrt(sa * (1.0 / SB_HEAD_DIM) + EPS), lax.rsqrt(sb * (1.0 / SB_HEAD_DIM) + EPS))
            ao_ref[rows(b), :] = (o * r * g_ref[...]).astype(BF16)
            return carry

        lax.fori_loop(0, nq, head_norm, 0)

    col_block = lambda off: pl.BlockSpec((S, LANES), lambda p: (0, off + p), pipeline_mode=pl.Buffered(1))
    out_block = lambda n: pl.BlockSpec((S, n), lambda p: (0, p), pipeline_mode=pl.Buffered(1))
    return pl.pallas_call(
        body, name="attn_fwd", grid=(npair,),
        out_shape=[jax.ShapeDtypeStruct((S, SB_WIDTH), F32), jax.ShapeDtypeStruct((S, SB_WIDTH), BF16),
                   jax.ShapeDtypeStruct((S, 2 * SB_WIDTH), F32)],
        in_specs=[col_block(0), col_block(npair), col_block(2 * npair), pl.BlockSpec((1, LANES), lambda p: (0, p))],
        out_specs=[out_block(LANES), out_block(LANES), out_block(2 * LANES)],
        scratch_shapes=[pltpu.VMEM((2, 2, Q, Q), BF16), pltpu.VMEM((2, 2, Q, Q), F32),
                        pltpu.VMEM((2, 2, Q, Q), BF16), pltpu.VMEM((2, Q, LANES), F32), pltpu.VMEM((2, 2, Q, Q), F32),
                        pltpu.VMEM((2, S, LANES), BF16), pltpu.VMEM((2, S, LANES), BF16)],
        compiler_params=_cparams(dimension_semantics=("arbitrary",)),
    )(qkv, qkv, qkv, g_attn)


def _out_proj(conv_out, attn_out, w_out, x, g_post_mix, g_pre_ffn):
    S = x.shape[0]
    tm = min(TOKEN_TILE, S)

    def body(co_ref, ao_ref, w_ref, x_ref, g1_ref, g2_ref, y_ref, h1_ref, fin_ref):
        y = _dot(co_ref[...], w_ref[:CONV_CH, :]) + _dot(ao_ref[...], w_ref[CONV_CH:, :])
        h1 = x_ref[...] + y * _rms_r(y) * g1_ref[...]
        y_ref[...] = y
        h1_ref[...] = h1
        fin_ref[...] = (h1 * _rms_r(h1) * g2_ref[...]).astype(BF16)

    row = lambda n: pl.BlockSpec((tm, n), lambda i: (i, 0))
    return pl.pallas_call(
        body, name="out_proj", grid=(S // tm,),
        out_shape=[jax.ShapeDtypeStruct((S, D_MODEL), F32), jax.ShapeDtypeStruct((S, D_MODEL), F32),
                   jax.ShapeDtypeStruct((S, D_MODEL), BF16)],
        in_specs=[row(CONV_CH), row(SB_WIDTH), _resident(w_out.shape), row(D_MODEL), _const((1, D_MODEL)),
                  _const((1, D_MODEL))],
        out_specs=[row(D_MODEL)] * 3,
        compiler_params=_cparams(),
    )(conv_out, attn_out, w_out, x, g_post_mix, g_pre_ffn)


def _ffn_fwd_loss(f_in, w_gate, w_up, w_down, h1, target, g_post_ffn):
    S = f_in.shape[0]
    tm = min(TOKEN_TILE, S)
    nt = S // tm

    def body(fin_ref, wg_ref, wu_ref, wd_ref, h1_ref, t_ref, g_ref, gt_ref, up_ref, df_ref, dh2_ref, loss_ref, dg_ref,
             sq_acc):
        i = pl.program_id(0)

        @pl.when(i == 0)
        def _():
            sq_acc[...] = jnp.zeros_like(sq_acc)
            dg_ref[...] = jnp.zeros_like(dg_ref)

        fin = fin_ref[...]
        gt = _dot(fin, wg_ref[...])
        up = _dot(fin, wu_ref[...])
        gt_ref[...] = gt.astype(BF16)
        up_ref[...] = up.astype(BF16)
        f = _dot((gt * jax.nn.sigmoid(gt) * up).astype(BF16), wd_ref[...])
        r = _rms_r(f)
        g = g_ref[...]
        diff = h1_ref[...] + f * r * g - t_ref[...]
        sq_acc[...] += jnp.sum(diff * diff, axis=0, keepdims=True)
        dh2 = diff * (1.0 / D_MODEL)
        dh2_ref[...] = dh2
        dg_ref[...] += jnp.sum(dh2 * f * r, axis=0, keepdims=True)
        df_ref[...] = _rms_bwd(f, r, g, dh2).astype(BF16)

        @pl.when(i == nt - 1)
        def _():
            loss_ref[...] = jnp.broadcast_to((0.5 / D_MODEL) * jnp.sum(sq_acc[...], axis=-1, keepdims=True), (1, LANES))

    row = lambda n: pl.BlockSpec((tm, n), lambda i: (i, 0))
    return pl.pallas_call(
        body, name="ffn_fwd_loss", grid=(nt,),
        out_shape=[jax.ShapeDtypeStruct((S, D_FF), BF16), jax.ShapeDtypeStruct((S, D_FF), BF16),
                   jax.ShapeDtypeStruct((S, D_MODEL), BF16), jax.ShapeDtypeStruct((S, D_MODEL), F32),
                   jax.ShapeDtypeStruct((1, LANES), F32), jax.ShapeDtypeStruct((1, D_MODEL), F32)],
        in_specs=[row(D_MODEL), _resident(w_gate.shape), _resident(w_up.shape), _resident(w_down.shape),
                  row(D_MODEL), row(D_MODEL), _const((1, D_MODEL))],
        out_specs=[row(D_FF), row(D_FF), row(D_MODEL), row(D_MODEL), _const((1, LANES)), _const((1, D_MODEL))],
        scratch_shapes=[pltpu.VMEM((1, D_MODEL), F32)],
        compiler_params=_cparams(dimension_semantics=("arbitrary",)),
    )(f_in, w_gate, w_up, w_down, h1, target, g_post_ffn)


def _ffn_bwd(df, gt, up, w_gate, w_up, w_down):
    S = df.shape[0]
    tm = min(FFN_TILE, S)

    def body(df_ref, gt_ref, up_ref, wg_ref, wu_ref, wd_ref, dgt_ref, dup_ref, act_ref, dfin_ref):
        dact = _dot_nt(df_ref[...], wd_ref[...])
        gt = gt_ref[...].astype(F32)
        up = up_ref[...].astype(F32)
        s = jax.nn.sigmoid(gt)
        silu = gt * s
        dgt = (dact * up * (s * (1.0 + gt * (1.0 - s)))).astype(BF16)
        dup = (dact * silu).astype(BF16)
        act_ref[...] = (silu * up).astype(BF16)
        dgt_ref[...] = dgt
        dup_ref[...] = dup
        dfin_ref[...] = _dot_nt(dgt, wg_ref[...]) + _dot_nt(dup, wu_ref[...])

    row = lambda n: pl.BlockSpec((tm, n), lambda i: (i, 0))
    return pl.pallas_call(
        body, name="ffn_bwd", grid=(S // tm,),
        out_shape=[jax.ShapeDtypeStruct((S, D_FF), BF16)] * 3 + [jax.ShapeDtypeStruct((S, D_MODEL), F32)],
        in_specs=[row(D_MODEL), row(D_FF), row(D_FF), _resident(w_gate.shape), _resident(w_up.shape),
                  _resident(w_down.shape)],
        out_specs=[row(D_FF)] * 3 + [row(D_MODEL)],
        compiler_params=_cparams(),
    )(df, gt, up, w_gate, w_up, w_down)


def _matmul_tn(name, xs, ys):
    S = xs[0].shape[0]
    ts = min(GRAD_TILE, S)
    row_ends = list(itertools.accumulate(x.shape[1] for x in xs))
    col_ends = list(itertools.accumulate(y.shape[1] for y in ys))

    def body(*refs):
        x_refs, y_refs, o_ref = refs[:len(xs)], refs[len(xs):-1], refs[-1]

        @pl.when(pl.program_id(0) == 0)
        def _():
            o_ref[...] = jnp.zeros_like(o_ref)

        yb = [y_ref[...].astype(BF16) for y_ref in y_refs]
        for x_ref, r1 in zip(x_refs, row_ends):
            xb = x_ref[...].astype(BF16)
            for y, c1 in zip(yb, col_ends):
                o_ref[r1 - xb.shape[1]:r1, c1 - y.shape[1]:c1] += _dot_tn(xb, y)

    return pl.pallas_call(
        body, name=name, grid=(S // ts,),
        out_shape=jax.ShapeDtypeStruct((row_ends[-1], col_ends[-1]), F32),
        in_specs=[pl.BlockSpec((ts, t.shape[1]), lambda s: (s, 0)) for t in (*xs, *ys)],
        out_specs=_resident((row_ends[-1], col_ends[-1])),
        compiler_params=_cparams(dimension_semantics=("arbitrary",)),
    )(*xs, *ys)


def _mix_bwd(dfin, h1, y, dh2, g_pre_ffn, g_post_mix, w_out, o, g_attn):
    S = dfin.shape[0]
    tm = min(TOKEN_TILE, S)
    inv_dh = 1.0 / SB_HEAD_DIM

    def body(dfin_ref, h1_ref, y_ref, dh2_ref, g2_ref, g1_ref, w_ref, o_ref, ga_ref, dh1_ref, dy_ref, dco_ref, do_ref,
             dg2_ref, dg1_ref, dga_ref):
        @pl.when(pl.program_id(0) == 0)
        def _():
            dg2_ref[...] = jnp.zeros_like(dg2_ref)
            dg1_ref[...] = jnp.zeros_like(dg1_ref)
            dga_ref[...] = jnp.zeros_like(dga_ref)

        h1, dfin = h1_ref[...], dfin_ref[...]
        r2 = _rms_r(h1)
        dh1 = dh2_ref[...] + _rms_bwd(h1, r2, g2_ref[...], dfin)
        dg2_ref[...] += jnp.sum(dfin * h1 * r2, axis=0, keepdims=True)
        y = y_ref[...]
        r1 = _rms_r(y)
        dy = _rms_bwd(y, r1, g1_ref[...], dh1).astype(BF16)
        dg1_ref[...] += jnp.sum(dh1 * y * r1, axis=0, keepdims=True)
        dh1_ref[...] = dh1
        dy_ref[...] = dy
        dco_ref[...] = _dot_nt(dy, w_ref[:CONV_CH, :])
        dao_all = _dot_nt(dy, w_ref[CONV_CH:, :])
        first = _head_masks()
        for p in range(SB_WIDTH // LANES):
            cols = slice(p * LANES, (p + 1) * LANES)
            o, dao, g = o_ref[:, cols], dao_all[:, cols], ga_ref[:, cols]
            sa, sb = _head_sum(o * o, first)
            r = jnp.where(first, lax.rsqrt(sa * inv_dh + EPS), lax.rsqrt(sb * inv_dh + EPS))
            w = dao * g
            wa, wb = _head_sum(w * o, first)
            do_ref[:, cols] = (r * (w - o * (r * r) * (jnp.where(first, wa, wb) * inv_dh))).astype(BF16)
            dga_ref[:, cols] += jnp.sum(dao * o * r, axis=0, keepdims=True)

    row = lambda n: pl.BlockSpec((tm, n), lambda i: (i, 0))
    return pl.pallas_call(
        body, name="mix_bwd", grid=(S // tm,),
        out_shape=[jax.ShapeDtypeStruct((S, D_MODEL), F32), jax.ShapeDtypeStruct((S, D_MODEL), BF16),
                   jax.ShapeDtypeStruct((S, CONV_CH), F32), jax.ShapeDtypeStruct((S, SB_WIDTH), BF16),
                   jax.ShapeDtypeStruct((1, D_MODEL), F32), jax.ShapeDtypeStruct((1, D_MODEL), F32),
                   jax.ShapeDtypeStruct((1, SB_WIDTH), F32)],
        in_specs=[row(D_MODEL)] * 4 + [_const((1, D_MODEL)), _const((1, D_MODEL)), _resident(w_out.shape), row(SB_WIDTH),
                  _const((1, SB_WIDTH))],
        out_specs=[row(D_MODEL), row(D_MODEL), row(CONV_CH), row(SB_WIDTH), _const((1, D_MODEL)), _const((1, D_MODEL)),
                   _const((1, SB_WIDTH))],
        compiler_params=_cparams(dimension_semantics=("arbitrary",)),
    )(dfin, h1, y, dh2, g_pre_ffn, g_post_mix, w_out, o, g_attn)


def _attn_bwd(qkv, do, cl):
    S = qkv.shape[0]
    Q = min(ATTN_BLOCK, S)
    nq = S // Q
    ntiles = nq * (nq + 1) // 2
    unroll = ATTN_BWD_UNROLL
    assert unroll % 2 == 0 and ntiles >= 3 + unroll
    npair = SB_WIDTH // LANES
    tiles = [(i, j) for i in range(nq) for j in range(i + 1)]

    def body(q_ref, k_ref, v_ref, do_ref, cl_ref, dq_ref, dk_ref, dv_ref,
             lb_buf, be_buf, g_buf, l_buf, a_buf, gb_buf, dz_buf, pg_buf, mask_buf, qh_buf, kh_buf, doh_buf):
        first = _head_masks()
        lane = lax.broadcasted_iota(jnp.int32, (1, LANES), 1)
        row = lax.broadcasted_iota(jnp.int32, (Q, Q), 0)
        col = lax.broadcasted_iota(jnp.int32, (Q, Q), 1)
        tri = jnp.where(row > col, -1.0, 0.0).astype(BF16)
        tpi = (row <= col).astype(BF16)
        heads = range(2)
        strips = [slice(r0, r0 + ATTN_STRIP) for r0 in range(0, Q, ATTN_STRIP)]
        rows = lambda j: pl.ds(pl.multiple_of(j * Q, Q), Q)
        wide = lambda t: jnp.tile(t, (1, Q // LANES))
        as_int = lambda t: int(t) if isinstance(t, (bool, int)) else t.astype(jnp.int32)

        keep = col < row
        mask_buf[0, 0] = jnp.full((Q, Q), LOG2E, F32)
        mask_buf[0, 1] = jnp.zeros((Q, Q), F32)
        mask_buf[1, 0] = jnp.where(keep, LOG2E, 0.0)
        mask_buf[1, 1] = jnp.where(keep, 0.0, MASKED)
        dq_ref[...] = jnp.zeros_like(dq_ref)
        dk_ref[...] = jnp.zeros_like(dk_ref)
        dv_ref[...] = jnp.zeros_like(dv_ref)
        _split_heads_into(qh_buf, q_ref, first, nq, Q)
        _split_heads_into(kh_buf, k_ref, first, nq, Q)
        _split_heads_into(doh_buf, do_ref, first, nq, Q)

        def scores(t):
            i, j = t
            kb = k_ref[rows(j), :]
            return tuple(_dot_nt(qh_buf[h, rows(i), :], kb) for h in heads)

        def logs(t, slot, z):
            i, j = t
            diag = as_int(i == j)
            for h in heads:
                for r in strips:
                    z2 = jnp.minimum(z[h][r] * LOG2E, Z2_MAX)
                    nl = jnp.log(1.0 + jnp.exp2(z2)) * mask_buf[diag, 0, r, :]
                    l_buf[slot, h, r, :] = nl.astype(BF16)
                    lb_buf[slot, h, r, :] = (z2 - nl) + mask_buf[diag, 1, r, :]

        def sums(t, slot):
            i, j = t
            vb = v_ref[rows(j), :]
            return (tuple(_dot(l_buf[slot, h], tri) for h in heads),
                    tuple(_dot_nt(doh_buf[h, rows(i), :], vb) for h in heads))

        def weights(t, slot, sm, da):
            i, j = t
            for h in heads:
                c = jnp.sum(jnp.where(lane == j, cl_ref[rows(i), h * LANES:(h + 1) * LANES], 0.0), axis=-1, keepdims=True)
                c = jnp.broadcast_to(c, (Q, LANES))
                for r in strips:
                    lb = lb_buf[slot, h, r, :]
                    a = jnp.exp2(lb + sm[h][r] + wide(c[r]))
                    g = da[h][r] * a
                    a_buf[slot, h, r, :] = a.astype(BF16)
                    be_buf[slot, h, r, :] = jnp.exp2(lb)
                    g_buf[slot, h, r, :] = g
                    gb_buf[slot, h, r, :] = g.astype(BF16)

        def prefix(t, slot):
            i, j = t
            dv_ref[rows(j), :] += (_dot_tn(a_buf[slot, 0], doh_buf[0, rows(i), :])
                                   + _dot_tn(a_buf[slot, 1], doh_buf[1, rows(i), :]))
            return tuple(_dot(gb_buf[slot, h], tpi) for h in heads)

        def dscores(t, slot, pm):
            i, j = t
            for h in heads:
                pg = pg_buf[h] * jnp.where(j == 0, 0.0, 1.0)
                for r in strips:
                    dz = g_buf[slot, h, r, :] - be_buf[slot, h, r, :] * (pm[h][r] + wide(pg[r]))
                    dz_buf[slot, h, r, :] = dz.astype(BF16)
                pg_buf[h] = pg + jnp.broadcast_to(pm[h][:, Q - 1:Q], (Q, LANES))

        def grads(t, slot):
            i, j = t
            dq_ref[rows(i), :] += (_dot(dz_buf[slot, 0], kh_buf[0, rows(j), :])
                                   + _dot(dz_buf[slot, 1], kh_buf[1, rows(j), :]))
            dk_ref[rows(j), :] += (_dot_tn(dz_buf[slot, 0], qh_buf[0, rows(i), :])
                                   + _dot_tn(dz_buf[slot, 1], qh_buf[1, rows(i), :]))

        def iteration(t, p):
            ta, tb, tc, td = t
            if ta is not None:
                z = scores(ta)
            if tb is not None:
                sm, da = sums(tb, 1 - p)
            if tc is not None:
                pm = prefix(tc, p)
            if td is not None:
                grads(td, 1 - p)
            if tb is not None:
                weights(tb, 1 - p, sm, da)
            if tc is not None:
                dscores(tc, p, pm)
            if ta is not None:
                logs(ta, p, z)

        def window(n):
            return tuple(tiles[n - k] if 0 <= n - k < ntiles else None for k in range(4))

        def following(t):
            i, j = t
            last = j == i
            return jnp.where(last, i + 1, i), jnp.where(last, 0, j + 1)

        peeled = 3 + (ntiles - 3) % unroll

        def unrolled_iterations(_, t):
            for n in range(peeled, peeled + unroll):
                iteration(t, n % 2)
                t = (following(t[0]),) + t[:3]
            return t

        pg_buf[...] = jnp.zeros_like(pg_buf)
        for n in range(peeled):
            iteration(window(n), n % 2)
        first_window = tuple((jnp.int32(i), jnp.int32(j)) for i, j in window(peeled))
        lax.fori_loop(0, (ntiles - peeled) // unroll, unrolled_iterations, first_window)
        for n in range(ntiles, ntiles + 3):
            iteration(window(n), n % 2)
        dq_ref[...] = dq_ref[...] * (1.0 / math.sqrt(SB_HEAD_DIM))

    col_block = lambda off: pl.BlockSpec((S, LANES), lambda p: (0, off + p), pipeline_mode=pl.Buffered(1))
    return pl.pallas_call(
        body, name="attn_bwd", grid=(npair,),
        out_shape=[jax.ShapeDtypeStruct((S, SB_WIDTH), F32)] * 3,
        in_specs=[col_block(0), col_block(npair), col_block(2 * npair), col_block(0),
                  pl.BlockSpec((S, 2 * LANES), lambda p: (0, p), pipeline_mode=pl.Buffered(1))],
        out_specs=[pl.BlockSpec((S, LANES), lambda p: (0, p), pipeline_mode=pl.Buffered(1))] * 3,
        scratch_shapes=[pltpu.VMEM((2, 2, Q, Q), F32)] * 3 + [pltpu.VMEM((2, 2, Q, Q), BF16)] * 4
        + [pltpu.VMEM((2, Q, LANES), F32), pltpu.VMEM((2, 2, Q, Q), F32)] + [pltpu.VMEM((2, S, LANES), BF16)] * 3,
        compiler_params=_cparams(dimension_semantics=("arbitrary",)),
    )(qkv, qkv, qkv, do, cl)


def _conv_bwd(u_conv, y_conv, dco, conv_w, ln_g, ln_b):
    S = u_conv.shape[0]
    tc = min(TOKEN_TILE, S)
    nt = S // tc
    per = tc // CONV_HALO
    groups = CONV_CHUNK // 8

    def body(u_ref, halo_ref, y_ref, dco_ref, cw_ref, lg_ref, lb_ref, du_ref, dcw_ref, dsm_ref, glu_ext, dyc_ext, sg_buf,
             dcw_acc, dsm_acc, glu_sh, dyc_sh):
        i = pl.program_id(0)
        ti = nt - 1 - i

        @pl.when(i == 0)
        def _():
            dyc_ext[tc:, :] = jnp.zeros((CONV_HALO, CONV_CH), F32)
            dcw_acc[...] = jnp.zeros_like(dcw_acc)
            dsm_acc[...] = jnp.zeros_like(dsm_acc)

        @pl.when(i > 0)
        def _():
            dyc_ext[tc:, :] = dyc_ext[0:CONV_HALO, :]

        glu_ext[0:CONV_HALO, :] = jnp.where(ti > 0, _glu(halo_ref[...])[2], 0.0)
        _, sg, glu = _glu(u_ref[...])
        glu_ext[CONV_HALO:, :] = glu
        sg_buf[...] = sg
        _shift_copies(glu_ext, glu_sh)

        dcb = jnp.zeros((8, CONV_CH), F32)
        dlg = jnp.zeros((8, CONV_CH), F32)
        dlb = jnp.zeros((8, CONV_CH), F32)
        fold = lambda t: jnp.sum(t.reshape(groups, 8, CONV_CH), axis=0)
        for r0 in range(0, tc, CONV_CHUNK):
            y = y_ref[r0:r0 + CONV_CHUNK, :]
            mu = jnp.mean(y, axis=-1, keepdims=True)
            yc = y - mu
            rstd = lax.rsqrt(jnp.mean(yc * yc, axis=-1, keepdims=True) + EPS)
            yn = yc * rstd
            yl = yn * lg_ref[...] + lb_ref[...]
            s = jax.nn.sigmoid(yl)
            dyl = dco_ref[r0:r0 + CONV_CHUNK, :] * (s * (1.0 + yl * (1.0 - s)))
            dlg = dlg + fold(dyl * yn)
            dlb = dlb + fold(dyl)
            wv = dyl * lg_ref[...]
            dyc = rstd * (wv - jnp.mean(wv, axis=-1, keepdims=True) - yn * jnp.mean(wv * yn, axis=-1, keepdims=True))
            dcb = dcb + fold(dyc)
            dyc_ext[r0:r0 + CONV_CHUNK, :] = dyc
        dsm_acc[0:8, :] += dcb
        dsm_acc[8:16, :] += dlg
        dsm_acc[16:24, :] += dlb
        _shift_copies(dyc_ext, dyc_sh)

        for r0 in range(0, tc, CONV_CHUNK):
            dyc = dyc_ext[r0:r0 + CONV_CHUNK, :]
            dglu = jnp.zeros((CONV_CHUNK, CONV_CH), F32)
            base = r0 + CONV_HALO - (CONV_WIDTH - 1)
            for w in range(CONV_WIDTH):
                back = r0 + (CONV_WIDTH - 1) - w
                dglu = dglu + cw_ref[w:w + 1, :] * _window(dyc_ext, dyc_sh, back, CONV_CHUNK)
                dcw_acc[8 * w:8 * w + 8, :] += fold(dyc * _window(glu_ext, glu_sh, base + w, CONV_CHUNK))
            sg = sg_buf[r0:r0 + CONV_CHUNK, :]
            v = u_ref[r0:r0 + CONV_CHUNK, :CONV_CH]
            du_ref[r0:r0 + CONV_CHUNK, :CONV_CH] = (dglu * sg).astype(BF16)
            du_ref[r0:r0 + CONV_CHUNK, CONV_CH:] = (dglu * v * sg * (1.0 - sg)).astype(BF16)

        @pl.when(i == nt - 1)
        def _():
            for w in range(CONV_WIDTH):
                dcw_ref[w:w + 1, :] = jnp.sum(dcw_acc[8 * w:8 * w + 8, :], axis=0, keepdims=True)
            dcw_ref[CONV_WIDTH:, :] = jnp.zeros((CONV_HALO - CONV_WIDTH, CONV_CH), F32)
            for k in range(3):
                dsm_ref[k:k + 1, :] = jnp.sum(dsm_acc[8 * k:8 * k + 8, :], axis=0, keepdims=True)
            dsm_ref[3:, :] = jnp.zeros((5, CONV_CH), F32)

    return pl.pallas_call(
        body, name="conv_bwd", grid=(nt,),
        out_shape=[jax.ShapeDtypeStruct((S, 2 * CONV_CH), BF16), jax.ShapeDtypeStruct((CONV_HALO, CONV_CH), F32),
                   jax.ShapeDtypeStruct((8, CONV_CH), F32)],
        in_specs=[pl.BlockSpec((tc, 2 * CONV_CH), lambda i: (nt - 1 - i, 0)),
                  pl.BlockSpec((CONV_HALO, 2 * CONV_CH), lambda i: (jnp.maximum((nt - 1 - i) * per - 1, 0), 0)),
                  pl.BlockSpec((tc, CONV_CH), lambda i: (nt - 1 - i, 0)), pl.BlockSpec((tc, CONV_CH), lambda i: (nt - 1 - i, 0)),
                  _const((CONV_HALO, CONV_CH)), _const((1, CONV_CH)), _const((1, CONV_CH))],
        out_specs=[pl.BlockSpec((tc, 2 * CONV_CH), lambda i: (nt - 1 - i, 0)), _const((CONV_HALO, CONV_CH)),
                   _const((8, CONV_CH))],
        scratch_shapes=[pltpu.VMEM((tc + CONV_HALO, CONV_CH), F32), pltpu.VMEM((tc + CONV_HALO, CONV_CH), F32),
                        pltpu.VMEM((tc, CONV_CH), F32), pltpu.VMEM((8 * CONV_HALO, CONV_CH), F32),
                        pltpu.VMEM((24, CONV_CH), F32)]
        + [pltpu.VMEM((SUBLANES - 1, tc + CONV_HALO - SUBLANES, CONV_CH), F32)] * 2,
        compiler_params=_cparams(dimension_semantics=("arbitrary",)),
    )(u_conv, u_conv, y_conv, dco, conv_w, ln_g, ln_b)


def _in_proj_bwd(du_conv, dq, dk, dv, w_in, x, g, dh1):
    S = x.shape[0]
    tm = min(TOKEN_TILE, S)
    nconv = 2 * CONV_CH

    def body(duc_ref, dq_ref, dk_ref, dv_ref, w_ref, x_ref, g_ref, dh1_ref, dx_ref, dg_ref):
        @pl.when(pl.program_id(0) == 0)
        def _():
            dg_ref[...] = jnp.zeros_like(dg_ref)

        da = _dot_nt(duc_ref[...], w_ref[:, :nconv])
        for n, ref in enumerate((dq_ref, dk_ref, dv_ref)):
            c0 = nconv + n * SB_WIDTH
            da = da + _dot_nt(ref[...].astype(BF16), w_ref[:, c0:c0 + SB_WIDTH])
        xf = x_ref[...]
        r = _rms_r(xf)
        dx_ref[...] = dh1_ref[...] + _rms_bwd(xf, r, g_ref[...], da)
        dg_ref[...] += jnp.sum(da * xf * r, axis=0, keepdims=True)

    row = lambda n: pl.BlockSpec((tm, n), lambda i: (i, 0))
    return pl.pallas_call(
        body, name="in_proj_bwd", grid=(S // tm,),
        out_shape=[jax.ShapeDtypeStruct((S, D_MODEL), F32), jax.ShapeDtypeStruct((1, D_MODEL), F32)],
        in_specs=[row(nconv), row(SB_WIDTH), row(SB_WIDTH), row(SB_WIDTH), _resident(w_in.shape), row(D_MODEL),
                  _const((1, D_MODEL)), row(D_MODEL)],
        out_specs=[row(D_MODEL), _const((1, D_MODEL))],
        compiler_params=_cparams(dimension_semantics=("arbitrary",)),
    )(du_conv, dq, dk, dv, w_in, x, g, dh1)


def _layer_grads(xs, target, g_pre_mix, w_in_f, conv_w_f, conv_b, conv_ln_g, conv_ln_b, attn_g, g_post_mix, g_pre_ffn,
                 g_post_ffn, late_weights, send_grads):
    a, u_conv, qkv = _in_proj(xs, g_pre_mix, w_in_f)
    conv_out, y_conv = _conv_fwd(u_conv, conv_w_f, conv_b, conv_ln_g, conv_ln_b)
    o, attn_out, cl = _attn_fwd(qkv, attn_g)
    w_out_f, w_gate_f, w_up_f, w_down_f = late_weights(attn_out)
    y, h1, f_in = _out_proj(conv_out, attn_out, w_out_f, xs, g_post_mix, g_pre_ffn)
    gt, up, df, dh2, loss_part, d_g_post_ffn = _ffn_fwd_loss(f_in, w_gate_f, w_up_f, w_down_f, h1, target, g_post_ffn)

    dgt, dup, act, dfin = _ffn_bwd(df, gt, up, w_gate_f, w_up_f, w_down_f)
    d_w_down = _matmul_tn("grad_w_down", [act], [df])
    d_w_gate = _matmul_tn("grad_w_gate", [f_in], [dgt])
    d_w_up = _matmul_tn("grad_w_up", [f_in], [dup])
    sent = send_grads("ffn", (d_w_gate, d_w_up, d_w_down))
    dh1, dy, dco, do, d_g_pre_ffn, d_g_post_mix, d_attn_g = _mix_bwd(dfin, h1, y, dh2, g_pre_ffn + sent, g_post_mix,
                                                                     w_out_f, o, attn_g)
    d_w_out = _matmul_tn("grad_w_out", [conv_out, attn_out], [dy])
    sent = send_grads("w_out", (d_w_out,))
    dq, dk, dv = _attn_bwd(qkv, do, cl)
    du_conv, d_conv_w, d_conv_small = _conv_bwd(u_conv, y_conv, dco, conv_w_f, conv_ln_g + sent, conv_ln_b)
    d_w_in = _matmul_tn("grad_w_in", [a], [du_conv, dq, dk, dv])
    sent = send_grads("w_in", (d_w_in,))
    grad_x, d_g_pre_mix = _in_proj_bwd(du_conv, dq, dk, dv, w_in_f, xs, g_pre_mix + sent, dh1)
    return (loss_part, grad_x, d_conv_w, d_conv_small, d_attn_g, d_g_pre_mix, d_g_post_mix, d_g_pre_ffn, d_g_post_ffn)


def _cols_to_blocks(w):
    K, N = w.shape
    return jnp.transpose(w.reshape(K, N_DEV, N // N_DEV), (1, 0, 2))


def _blocks_to_cols(blocks):
    n_dev, K, n = blocks.shape
    return jnp.transpose(blocks, (1, 0, 2)).reshape(K, n_dev * n)


def kernel(x, g_pre_mix, w_in, conv_w, conv_b, conv_ln_g, conv_ln_b, attn_norm_g, w_out, g_post_mix, g_pre_ffn, w_gate, w_up, w_down, g_post_ffn, loss_target, m_g_pre_mix, m_w_in, m_conv_w, m_conv_b, m_conv_ln_g, m_conv_ln_b, m_attn_norm_g, m_w_out, m_g_post_mix, m_g_pre_ffn, m_w_gate, m_w_up, m_w_down, m_g_post_ffn, v_g_pre_mix, v_w_in, v_conv_w, v_conv_b, v_conv_ln_g, v_conv_ln_b, v_attn_norm_g, v_w_out, v_g_post_mix, v_g_pre_ffn, v_w_gate, v_w_up, v_w_down, v_g_post_ffn):
    xs = x[0]
    target = loss_target[0]
    me = 4 * lax.axis_index("x") + 2 * lax.axis_index("y") + lax.axis_index("c")
    cw_shard = conv_w.reshape(CONV_WIDTH, CONV_CH // N_DEV)
    attn_g = attn_norm_g.reshape(1, SB_WIDTH)

    gathered = _all_gather([w_in[0].astype(BF16), cw_shard])
    w_in_f = _blocks_to_cols(gathered[0])
    conv_w_f = jnp.pad(_blocks_to_cols(gathered[1]), ((0, CONV_HALO - CONV_WIDTH), (0, 0)))
    gathered_zero = gathered[2][0:1, 0:1].astype(BF16)
    late = [w_out[0].astype(BF16) + gathered_zero, w_gate[0].astype(BF16), w_up[0].astype(BF16), w_down[0].astype(BF16)]
    late_started = _exchange_start("all_gather_late_start", late, scatter=False)

    def late_weights(after):
        lands = _exchange_wait("all_gather_late_wait", late_started, False, after)
        wo, wg, wu, wd = [lax.dynamic_update_index_in_dim(land, own, me, 0) for land, own in zip(lands, late)]
        return wo.reshape(D_MODEL, D_MODEL), _blocks_to_cols(wg), _blocks_to_cols(wu), wd.reshape(D_FF, D_MODEL)

    started = {}

    def send_grads(name, grads):
        blocks = [g.reshape(N_DEV, g.shape[0] // N_DEV, g.shape[1]) if g.shape[1] == D_MODEL else _cols_to_blocks(g)
                  for g in grads]
        payload = BF16 if name == "w_in" else F32
        sent = _exchange_start("reduce_scatter_" + name + "_start", [b.astype(payload) for b in blocks], scatter=True)
        started[name] = (sent, blocks)
        return sent[-1][0:1, 0:1]

    (loss_part, grad_x, d_conv_w, d_conv_small, d_attn_g, d_g_pre_mix, d_g_post_mix, d_g_pre_ffn,
     d_g_post_ffn) = _layer_grads(
        xs, target, g_pre_mix + late_started[-1][0:1, 0:1], w_in_f, conv_w_f, conv_b, conv_ln_g, conv_ln_b, attn_g,
        g_post_mix, g_pre_ffn, g_post_ffn, late_weights, send_grads)

    def reduced(name, after, shards):
        st, blocks = started[name]
        lands = _exchange_wait("reduce_scatter_" + name + "_wait", st, True, after)
        return [_sum_adamw("adamw_" + wn, land, lax.dynamic_index_in_dim(blk, me, 0, keepdims=False), w[0], m[0], v[0])
                for land, blk, (wn, w, m, v) in zip(lands, blocks, shards)]

    two = lambda t: t.reshape(2, CONV_CH)
    small_g = jnp.concatenate([
        d_conv_w,
        d_conv_small[0:3],
        d_attn_g,
        two(d_g_pre_mix), two(d_g_post_mix), two(d_g_pre_ffn), two(d_g_post_ffn),
        jnp.broadcast_to(loss_part[0:1, 0:1], (1, CONV_CH)),
        jnp.zeros((3, CONV_CH), F32)], axis=0)
    small_g = _all_reduce_small(small_g)
    loss = small_g[44, 0]
    g_conv_w = lax.dynamic_slice(small_g, (0, me * (CONV_CH // N_DEV)), (CONV_WIDTH, CONV_CH // N_DEV))
    pack = lambda cb, lg, lb, ag, g1, g2, g3, g4: jnp.concatenate(
        [cb, lg, lb, ag.reshape(1, SB_WIDTH), two(g1), two(g2), two(g3), two(g4), jnp.zeros((4, CONV_CH), F32)], axis=0)
    sm_g = small_g[CONV_HALO:]
    sm_delta, sm_m, sm_v = _adamw_small(
        "adamw_small",
        pack(conv_b, conv_ln_g, conv_ln_b, attn_norm_g, g_pre_mix, g_post_mix, g_pre_ffn, g_post_ffn), sm_g,
        pack(m_conv_b, m_conv_ln_g, m_conv_ln_b, m_attn_norm_g, m_g_pre_mix, m_g_post_mix, m_g_pre_ffn, m_g_post_ffn),
        pack(v_conv_b, v_conv_ln_g, v_conv_ln_b, v_attn_norm_g, v_g_pre_mix, v_g_post_mix, v_g_pre_ffn, v_g_post_ffn))
    cw_delta, cw_m, cw_v = _adamw_small("adamw_conv_w", cw_shard, g_conv_w,
                                        m_conv_w.reshape(cw_shard.shape), v_conv_w.reshape(cw_shard.shape))

    ffn = reduced("ffn", grad_x, [("w_gate", w_gate, m_w_gate, v_w_gate), ("w_up", w_up, m_w_up, v_w_up),
                                  ("w_down", w_down, m_w_down, v_w_down)])
    big = {"w_gate": ffn[0], "w_up": ffn[1], "w_down": ffn[2],
           "w_out": reduced("w_out", ffn[2][0], [("w_out", w_out, m_w_out, v_w_out)])[0]}
    big["w_in"] = reduced("w_in", big["w_out"][0], [("w_in", w_in, m_w_in, v_w_in)])[0]

    def unpack(t):
        return {"conv_b": t[0:1], "conv_ln_g": t[1:2], "conv_ln_b": t[2:3], "attn_norm_g": t[3:4].reshape(1, SB_HEADS, SB_HEAD_DIM),
                "g_pre_mix": t[4:6].reshape(1, D_MODEL), "g_post_mix": t[6:8].reshape(1, D_MODEL),
                "g_pre_ffn": t[8:10].reshape(1, D_MODEL), "g_post_ffn": t[10:12].reshape(1, D_MODEL)}

    names = ["g_pre_mix", "w_in", "conv_w", "conv_b", "conv_ln_g", "conv_ln_b", "attn_norm_g", "w_out", "g_post_mix",
             "g_pre_ffn", "w_gate", "w_up", "w_down", "g_post_ffn"]
    kinds = []
    for idx, small in enumerate((sm_g, sm_delta, sm_m, sm_v)):
        d = unpack(small)
        d["conv_w"] = (g_conv_w, cw_delta, cw_m, cw_v)[idx].reshape(1, CONV_WIDTH, 1, CONV_CH // N_DEV)
        for n in big:
            d[n] = big[n][idx][None]
        kinds.append([d[n] for n in names])

    return (loss, grad_x[None], *kinds[0], *kinds[1], *kinds[2], *kinds[3])
```
